```python
import math
import jax, jax.numpy as jnp
from jax import lax
import numpy as np

D_MODEL = 1024
BATCH = 32
SEQ = 2048
DEPTH = 2

D_MIX = D_MODEL
SSM_WIDTH = D_MIX // 2
SSM_CH = 16
SSM_GROUPS = SSM_WIDTH // SSM_CH
SSM_STATE = 64
GM_WIDTH = D_MIX - SSM_WIDTH
GM_HEADS = 4
GM_HEAD_DIM = GM_WIDTH // GM_HEADS
GM_CHUNK = 128
D_FF = ((8 * D_MODEL // 3 + 127) // 128) * 128
IN_COLS = SSM_WIDTH + 2 * GM_WIDTH
EPS = 1e-6
DT_MIN = 1e-3
DT_MAX = 1e-1

kernel_name = "hybrid_s5_gmlp_macaron"


def rmsnorm(x, g):
    xf = x.astype(jnp.float32)
    y = xf * lax.rsqrt(jnp.mean(xf * xf, axis=-1, keepdims=True) + EPS)
    return (y * g.astype(jnp.float32)).astype(x.dtype)


def swiglu(h, w_in, w_out):
    gu = h @ w_in
    g, u = gu[..., :D_FF], gu[..., D_FF:]
    return (jax.nn.silu(g) * u) @ w_out


def s5_group(u, a_re, a_im, log_dt, b_re, b_im, c_re, c_im, d_skip, glu_w, glu_b):
    bsz, seq = u.shape[0], u.shape[1]
    f32 = jnp.float32
    ug = u.reshape(bsz, seq, SSM_GROUPS, SSM_CH).astype(f32)
    lam = lax.complex(a_re.astype(f32), a_im.astype(f32))
    dt = jnp.exp(log_dt.astype(f32))[:, None]
    lam_bar = jnp.exp(lam * dt)
    b_mat = lax.complex(b_re.astype(f32), b_im.astype(f32))
    b_bar = ((lam_bar - 1.0) / lam)[..., None] * b_mat
    bu = jnp.einsum('blgc,gpc->blgp', ug, b_bar)
    a_elems = jnp.broadcast_to(lam_bar, (seq, SSM_GROUPS, SSM_STATE))

    def combine(left, right):
        a_l, b_l = left
        a_r, b_r = right
        return a_r * a_l, a_r * b_l + b_r

    def scan_one(bu_b):
        _, h = lax.associative_scan(combine, (a_elems, bu_b), axis=0)
        return h

    h = jax.vmap(scan_one)(bu)
    c_mat = lax.complex(c_re.astype(f32), c_im.astype(f32))
    y = jnp.einsum('blgp,gcp->blgc', h, c_mat).real + d_skip.astype(f32) * ug
    y = jax.nn.gelu(y)
    z = jnp.einsum('blgc,gce->blge', y, glu_w.astype(f32)) + glu_b.astype(f32)
    out = z[..., :SSM_CH] * jax.nn.sigmoid(z[..., SSM_CH:])
    return out.reshape(bsz, seq, SSM_WIDTH).astype(u.dtype)


def gmlp_group(u, v, v_gain, w_s, b_s):
    bsz, seq = u.shape[0], u.shape[1]
    u = jax.nn.gelu(u)
    v = jax.nn.gelu(v).reshape(bsz, seq // GM_CHUNK, GM_CHUNK, GM_HEADS, GM_HEAD_DIM)
    v = rmsnorm(v, v_gain.reshape(GM_HEADS, GM_HEAD_DIM))
    mask = jnp.tril(jnp.ones((GM_CHUNK, GM_CHUNK), dtype=bool))
    ws = jnp.where(mask[None], w_s, jnp.zeros_like(w_s))
    s = jnp.einsum('hts,bnshd->bnthd', ws, v) + b_s.T[None, None, :, :, None]
    return u * s.reshape(bsz, seq, GM_WIDTH)


def _fwd_setup_inputs(seed: int = 0) -> dict:
    key = jax.random.key(seed)
    ks = jax.random.split(key, 26)
    f32 = jnp.float32
    nrm = lambda k, shape, scale: scale * jax.random.normal(k, shape, f32)
    gain = lambda k, shape: 1.0 + 0.05 * jax.random.normal(k, shape, f32)
    n_idx = jnp.arange(SSM_STATE, dtype=f32)
    return {
        "x": jax.random.normal(ks[0], (BATCH, SEQ, D_MODEL), f32),
        "norm_ffn1": gain(ks[1], (DEPTH, D_MODEL)),
        "ffn1_w_in": nrm(ks[2], (DEPTH, D_MODEL, 2 * D_FF), D_MODEL ** -0.5),
        "ffn1_w_out": nrm(ks[3], (DEPTH, D_FF, D_MODEL), D_FF ** -0.5),
        "norm_mix": gain(ks[4], (DEPTH, D_MODEL)),
        "mix_w_in": nrm(ks[5], (DEPTH, D_MODEL, IN_COLS), D_MODEL ** -0.5),
        "ssm_a_re": -0.5 + nrm(ks[6], (DEPTH, SSM_GROUPS, SSM_STATE), 0.01),
        "ssm_a_im": math.pi * n_idx + nrm(ks[7], (DEPTH, SSM_GROUPS, SSM_STATE), 0.01),
        "ssm_log_dt": jax.random.uniform(ks[8], (DEPTH, SSM_GROUPS), f32,
                                         math.log(DT_MIN), math.log(DT_MAX)),
        "ssm_b_re": nrm(ks[9], (DEPTH, SSM_GROUPS, SSM_STATE, SSM_CH), (2 * SSM_CH) ** -0.5),
        "ssm_b_im": nrm(ks[10], (DEPTH, SSM_GROUPS, SSM_STATE, SSM_CH), (2 * SSM_CH) ** -0.5),
        "ssm_c_re": nrm(ks[11], (DEPTH, SSM_GROUPS, SSM_CH, SSM_STATE), (2 * SSM_STATE) ** -0.5),
        "ssm_c_im": nrm(ks[12], (DEPTH, SSM_GROUPS, SSM_CH, SSM_STATE), (2 * SSM_STATE) ** -0.5),
        "ssm_d": nrm(ks[13], (DEPTH, SSM_GROUPS, SSM_CH), 1.0),
        "ssm_glu_w": nrm(ks[14], (DEPTH, SSM_GROUPS, SSM_CH, 2 * SSM_CH), SSM_CH ** -0.5),
        "ssm_glu_b": nrm(ks[15], (DEPTH, SSM_GROUPS, 2 * SSM_CH), 0.02),
        "gm_v_gain": gain(ks[16], (DEPTH, GM_WIDTH)),
        "gm_w_s": nrm(ks[17], (DEPTH, GM_HEADS, GM_CHUNK, GM_CHUNK), 0.5 * GM_CHUNK ** -0.5),
        "gm_b_s": 1.0 + nrm(ks[18], (DEPTH, GM_HEADS, GM_CHUNK), 0.1),
        "gain_ssm_out": gain(ks[19], (DEPTH, SSM_WIDTH)),
        "gain_gm_out": gain(ks[20], (DEPTH, GM_WIDTH)),
        "mix_w_out": nrm(ks[21], (DEPTH, D_MIX, D_MODEL), D_MIX ** -0.5),
        "norm_ffn2": gain(ks[22], (DEPTH, D_MODEL)),
        "ffn2_w_in": nrm(ks[23], (DEPTH, D_MODEL, 2 * D_FF), D_MODEL ** -0.5),
        "ffn2_w_out": nrm(ks[24], (DEPTH, D_FF, D_MODEL), D_FF ** -0.5),
        "norm_final": gain(ks[25], (D_MODEL,)),
    }


def _fwd_reference(x, norm_ffn1, ffn1_w_in, ffn1_w_out, norm_mix, mix_w_in,
              ssm_a_re, ssm_a_im, ssm_log_dt, ssm_b_re, ssm_b_im, ssm_c_re, ssm_c_im,
              ssm_d, ssm_glu_w, ssm_glu_b, gm_v_gain, gm_w_s, gm_b_s,
              gain_ssm_out, gain_gm_out, mix_w_out, norm_ffn2, ffn2_w_in, ffn2_w_out,
              norm_final):
    for l in range(DEPTH):
        x = x + 0.5 * swiglu(rmsnorm(x, norm_ffn1[l]), ffn1_w_in[l], ffn1_w_out[l])
        z = rmsnorm(x, norm_mix[l]) @ mix_w_in[l]
        u_ssm = z[..., :SSM_WIDTH]
        u_gm = z[..., SSM_WIDTH:SSM_WIDTH + GM_WIDTH]
        v_gm = z[..., SSM_WIDTH + GM_WIDTH:]
        y_ssm = s5_group(u_ssm, ssm_a_re[l], ssm_a_im[l], ssm_log_dt[l],
                         ssm_b_re[l], ssm_b_im[l], ssm_c_re[l], ssm_c_im[l],
                         ssm_d[l], ssm_glu_w[l], ssm_glu_b[l])
        y_gm = gmlp_group(u_gm, v_gm, gm_v_gain[l], gm_w_s[l], gm_b_s[l])
        y = jnp.concatenate([rmsnorm(y_ssm, gain_ssm_out[l]),
                             rmsnorm(y_gm, gain_gm_out[l])], axis=-1)
        x = x + y @ mix_w_out[l]
        x = x + 0.5 * swiglu(rmsnorm(x, norm_ffn2[l]), ffn2_w_in[l], ffn2_w_out[l])
    return rmsnorm(x, norm_final)


import jax as _jax
import jax.numpy as _jnp

TWIN_FORMAT = 'train_step'
FWD_PARAMS = ['x', 'norm_ffn1', 'ffn1_w_in', 'ffn1_w_out', 'norm_mix', 'mix_w_in', 'ssm_a_re', 'ssm_a_im', 'ssm_log_dt', 'ssm_b_re', 'ssm_b_im', 'ssm_c_re', 'ssm_c_im', 'ssm_d', 'ssm_glu_w', 'ssm_glu_b', 'gm_v_gain', 'gm_w_s', 'gm_b_s', 'gain_ssm_out', 'gain_gm_out', 'mix_w_out', 'norm_ffn2', 'ffn2_w_in', 'ffn2_w_out', 'norm_final']
TWIN_WEIGHTS = ['norm_ffn1', 'ffn1_w_in', 'ffn1_w_out', 'norm_mix', 'mix_w_in', 'ssm_a_re', 'ssm_a_im', 'ssm_log_dt', 'ssm_b_re', 'ssm_b_im', 'ssm_c_re', 'ssm_c_im', 'ssm_d', 'ssm_glu_w', 'ssm_glu_b', 'gm_v_gain', 'gm_w_s', 'gm_b_s', 'gain_ssm_out', 'gain_gm_out', 'mix_w_out', 'norm_ffn2', 'ffn2_w_in', 'ffn2_w_out', 'norm_final']
TWIN_DIFF_INPUT = 'x'
TWIN_INPUTS = ['x', 'norm_ffn1', 'ffn1_w_in', 'ffn1_w_out', 'norm_mix', 'mix_w_in', 'ssm_a_re', 'ssm_a_im', 'ssm_log_dt', 'ssm_b_re', 'ssm_b_im', 'ssm_c_re', 'ssm_c_im', 'ssm_d', 'ssm_glu_w', 'ssm_glu_b', 'gm_v_gain', 'gm_w_s', 'gm_b_s', 'gain_ssm_out', 'gain_gm_out', 'mix_w_out', 'norm_ffn2', 'ffn2_w_in', 'ffn2_w_out', 'norm_final', 'loss_target', 'm_norm_ffn1', 'm_ffn1_w_in', 'm_ffn1_w_out', 'm_norm_mix', 'm_mix_w_in', 'm_ssm_a_re', 'm_ssm_a_im', 'm_ssm_log_dt', 'm_ssm_b_re', 'm_ssm_b_im', 'm_ssm_c_re', 'm_ssm_c_im', 'm_ssm_d', 'm_ssm_glu_w', 'm_ssm_glu_b', 'm_gm_v_gain', 'm_gm_w_s', 'm_gm_b_s', 'm_gain_ssm_out', 'm_gain_gm_out', 'm_mix_w_out', 'm_norm_ffn2', 'm_ffn2_w_in', 'm_ffn2_w_out', 'm_norm_final', 'v_norm_ffn1', 'v_ffn1_w_in', 'v_ffn1_w_out', 'v_norm_mix', 'v_mix_w_in', 'v_ssm_a_re', 'v_ssm_a_im', 'v_ssm_log_dt', 'v_ssm_b_re', 'v_ssm_b_im', 'v_ssm_c_re', 'v_ssm_c_im', 'v_ssm_d', 'v_ssm_glu_w', 'v_ssm_glu_b', 'v_gm_v_gain', 'v_gm_w_s', 'v_gm_b_s', 'v_gain_ssm_out', 'v_gain_gm_out', 'v_mix_w_out', 'v_norm_ffn2', 'v_ffn2_w_in', 'v_ffn2_w_out', 'v_norm_final']
TWIN_OUTPUTS = ['loss', 'grad_x', 'grad_norm_ffn1', 'grad_ffn1_w_in', 'grad_ffn1_w_out', 'grad_norm_mix', 'grad_mix_w_in', 'grad_ssm_a_re', 'grad_ssm_a_im', 'grad_ssm_log_dt', 'grad_ssm_b_re', 'grad_ssm_b_im', 'grad_ssm_c_re', 'grad_ssm_c_im', 'grad_ssm_d', 'grad_ssm_glu_w', 'grad_ssm_glu_b', 'grad_gm_v_gain', 'grad_gm_w_s', 'grad_gm_b_s', 'grad_gain_ssm_out', 'grad_gain_gm_out', 'grad_mix_w_out', 'grad_norm_ffn2', 'grad_ffn2_w_in', 'grad_ffn2_w_out', 'grad_norm_final', 'delta_norm_ffn1', 'delta_ffn1_w_in', 'delta_ffn1_w_out', 'delta_norm_mix', 'delta_mix_w_in', 'delta_ssm_a_re', 'delta_ssm_a_im', 'delta_ssm_log_dt', 'delta_ssm_b_re', 'delta_ssm_b_im', 'delta_ssm_c_re', 'delta_ssm_c_im', 'delta_ssm_d', 'delta_ssm_glu_w', 'delta_ssm_glu_b', 'delta_gm_v_gain', 'delta_gm_w_s', 'delta_gm_b_s', 'delta_gain_ssm_out', 'delta_gain_gm_out', 'delta_mix_w_out', 'delta_norm_ffn2', 'delta_ffn2_w_in', 'delta_ffn2_w_out', 'delta_norm_final', 'new_m_norm_ffn1', 'new_m_ffn1_w_in', 'new_m_ffn1_w_out', 'new_m_norm_mix', 'new_m_mix_w_in', 'new_m_ssm_a_re', 'new_m_ssm_a_im', 'new_m_ssm_log_dt', 'new_m_ssm_b_re', 'new_m_ssm_b_im', 'new_m_ssm_c_re', 'new_m_ssm_c_im', 'new_m_ssm_d', 'new_m_ssm_glu_w', 'new_m_ssm_glu_b', 'new_m_gm_v_gain', 'new_m_gm_w_s', 'new_m_gm_b_s', 'new_m_gain_ssm_out', 'new_m_gain_gm_out', 'new_m_mix_w_out', 'new_m_norm_ffn2', 'new_m_ffn2_w_in', 'new_m_ffn2_w_out', 'new_m_norm_final', 'new_v_norm_ffn1', 'new_v_ffn1_w_in', 'new_v_ffn1_w_out', 'new_v_norm_mix', 'new_v_mix_w_in', 'new_v_ssm_a_re', 'new_v_ssm_a_im', 'new_v_ssm_log_dt', 'new_v_ssm_b_re', 'new_v_ssm_b_im', 'new_v_ssm_c_re', 'new_v_ssm_c_im', 'new_v_ssm_d', 'new_v_ssm_glu_w', 'new_v_ssm_glu_b', 'new_v_gm_v_gain', 'new_v_gm_w_s', 'new_v_gm_b_s', 'new_v_gain_ssm_out', 'new_v_gain_gm_out', 'new_v_mix_w_out', 'new_v_norm_ffn2', 'new_v_ffn2_w_in', 'new_v_ffn2_w_out', 'new_v_norm_final']
TWIN_LEAF_KINDS = {'loss': 'loss', 'grad_x': 'grad_x', 'grad_norm_ffn1': 'grad_w', 'grad_ffn1_w_in': 'grad_w', 'grad_ffn1_w_out': 'grad_w', 'grad_norm_mix': 'grad_w', 'grad_mix_w_in': 'grad_w', 'grad_ssm_a_re': 'grad_w', 'grad_ssm_a_im': 'grad_w', 'grad_ssm_log_dt': 'grad_w', 'grad_ssm_b_re': 'grad_w', 'grad_ssm_b_im': 'grad_w', 'grad_ssm_c_re': 'grad_w', 'grad_ssm_c_im': 'grad_w', 'grad_ssm_d': 'grad_w', 'grad_ssm_glu_w': 'grad_w', 'grad_ssm_glu_b': 'grad_w', 'grad_gm_v_gain': 'grad_w', 'grad_gm_w_s': 'grad_w', 'grad_gm_b_s': 'grad_w', 'grad_gain_ssm_out': 'grad_w', 'grad_gain_gm_out': 'grad_w', 'grad_mix_w_out': 'grad_w', 'grad_norm_ffn2': 'grad_w', 'grad_ffn2_w_in': 'grad_w', 'grad_ffn2_w_out': 'grad_w', 'grad_norm_final': 'grad_w', 'delta_norm_ffn1': 'delta_w', 'delta_ffn1_w_in': 'delta_w', 'delta_ffn1_w_out': 'delta_w', 'delta_norm_mix': 'delta_w', 'delta_mix_w_in': 'delta_w', 'delta_ssm_a_re': 'delta_w', 'delta_ssm_a_im': 'delta_w', 'delta_ssm_log_dt': 'delta_w', 'delta_ssm_b_re': 'delta_w', 'delta_ssm_b_im': 'delta_w', 'delta_ssm_c_re': 'delta_w', 'delta_ssm_c_im': 'delta_w', 'delta_ssm_d': 'delta_w', 'delta_ssm_glu_w': 'delta_w', 'delta_ssm_glu_b': 'delta_w', 'delta_gm_v_gain': 'delta_w', 'delta_gm_w_s': 'delta_w', 'delta_gm_b_s': 'delta_w', 'delta_gain_ssm_out': 'delta_w', 'delta_gain_gm_out': 'delta_w', 'delta_mix_w_out': 'delta_w', 'delta_norm_ffn2': 'delta_w', 'delta_ffn2_w_in': 'delta_w', 'delta_ffn2_w_out': 'delta_w', 'delta_norm_final': 'delta_w', 'new_m_norm_ffn1': 'new_m', 'new_m_ffn1_w_in': 'new_m', 'new_m_ffn1_w_out': 'new_m', 'new_m_norm_mix': 'new_m', 'new_m_mix_w_in': 'new_m', 'new_m_ssm_a_re': 'new_m', 'new_m_ssm_a_im': 'new_m', 'new_m_ssm_log_dt': 'new_m', 'new_m_ssm_b_re': 'new_m', 'new_m_ssm_b_im': 'new_m', 'new_m_ssm_c_re': 'new_m', 'new_m_ssm_c_im': 'new_m', 'new_m_ssm_d': 'new_m', 'new_m_ssm_glu_w': 'new_m', 'new_m_ssm_glu_b': 'new_m', 'new_m_gm_v_gain': 'new_m', 'new_m_gm_w_s': 'new_m', 'new_m_gm_b_s': 'new_m', 'new_m_gain_ssm_out': 'new_m', 'new_m_gain_gm_out': 'new_m', 'new_m_mix_w_out': 'new_m', 'new_m_norm_ffn2': 'new_m', 'new_m_ffn2_w_in': 'new_m', 'new_m_ffn2_w_out': 'new_m', 'new_m_norm_final': 'new_m', 'new_v_norm_ffn1': 'new_v', 'new_v_ffn1_w_in': 'new_v', 'new_v_ffn1_w_out': 'new_v', 'new_v_norm_mix': 'new_v', 'new_v_mix_w_in': 'new_v', 'new_v_ssm_a_re': 'new_v', 'new_v_ssm_a_im': 'new_v', 'new_v_ssm_log_dt': 'new_v', 'new_v_ssm_b_re': 'new_v', 'new_v_ssm_b_im': 'new_v', 'new_v_ssm_c_re': 'new_v', 'new_v_ssm_c_im': 'new_v', 'new_v_ssm_d': 'new_v', 'new_v_ssm_glu_w': 'new_v', 'new_v_ssm_glu_b': 'new_v', 'new_v_gm_v_gain': 'new_v', 'new_v_gm_w_s': 'new_v', 'new_v_gm_b_s': 'new_v', 'new_v_gain_ssm_out': 'new_v', 'new_v_gain_gm_out': 'new_v', 'new_v_mix_w_out': 'new_v', 'new_v_norm_ffn2': 'new_v', 'new_v_ffn2_w_in': 'new_v', 'new_v_ffn2_w_out': 'new_v', 'new_v_norm_final': 'new_v'}


def _forward(args):
    return _fwd_reference(*[args[k] for k in FWD_PARAMS])


def _output_shape():
    out = _jax.eval_shape(lambda: _forward(_fwd_setup_inputs(0)))
    return out.shape, out.dtype

N_MICROBATCH = 1
ADAM_LR = 0.001
ADAM_B1 = 0.9
ADAM_B2 = 0.999
ADAM_EPS = 1e-08
ADAM_WD = 0.01
ADAM_STEP = 10
PER_EXAMPLE_BATCH_AXIS = {'x': 0, 'loss_target': 0}
SHARED_INPUTS = []
_WEIGHT_DTYPES = {'norm_ffn1': _jnp.float32, 'ffn1_w_in': _jnp.float32, 'ffn1_w_out': _jnp.float32, 'norm_mix': _jnp.float32, 'mix_w_in': _jnp.float32, 'ssm_a_re': _jnp.float32, 'ssm_a_im': _jnp.float32, 'ssm_log_dt': _jnp.float32, 'ssm_b_re': _jnp.float32, 'ssm_b_im': _jnp.float32, 'ssm_c_re': _jnp.float32, 'ssm_c_im': _jnp.float32, 'ssm_d': _jnp.float32, 'ssm_glu_w': _jnp.float32, 'ssm_glu_b': _jnp.float32, 'gm_v_gain': _jnp.float32, 'gm_w_s': _jnp.float32, 'gm_b_s': _jnp.float32, 'gain_ssm_out': _jnp.float32, 'gain_gm_out': _jnp.float32, 'mix_w_out': _jnp.float32, 'norm_ffn2': _jnp.float32, 'ffn2_w_in': _jnp.float32, 'ffn2_w_out': _jnp.float32, 'norm_final': _jnp.float32}
MOMENT_SCALE = {'norm_ffn1': 1.010447e-01, 'ffn1_w_in': 4.320731e-02, 'ffn1_w_out': 7.119898e-02, 'norm_mix': 1.888772e-01, 'mix_w_in': 1.664071e-01, 'ssm_a_re': 1.947259e-02, 'ssm_a_im': 1.840850e-02, 'ssm_log_dt': 3.344908e+00, 'ssm_b_re': 8.063380e-03, 'ssm_b_im': 9.062160e-03, 'ssm_c_re': 2.092270e-02, 'ssm_c_im': 2.215282e-02, 'ssm_d': 3.880471e-01, 'ssm_glu_w': 2.885443e-01, 'ssm_glu_b': 9.446385e-01, 'gm_v_gain': 5.503030e-02, 'gm_w_s': 1.083434e-01, 'gm_b_s': 1.428523e-01, 'gain_ssm_out': 3.625098e-01, 'gain_gm_out': 3.272595e-01, 'mix_w_out': 3.577974e-01, 'norm_ffn2': 7.851360e-02, 'ffn2_w_in': 3.247042e-02, 'ffn2_w_out': 5.443175e-02, 'norm_final': 6.478877e+01}


def _to_microbatches(a, axis):
    t = _jnp.moveaxis(a, axis, 0)
    t = t.reshape((N_MICROBATCH, t.shape[0] // N_MICROBATCH) + t.shape[1:])
    return _jnp.moveaxis(t, 1, axis + 1)


def setup_inputs(seed: int = 0) -> dict:
    inp = _fwd_setup_inputs(seed)
    key = _jax.random.fold_in(_jax.random.key(seed), 7919)
    shape, _ = _output_shape()
    out = dict(inp)
    out["loss_target"] = _jax.random.normal(_jax.random.fold_in(key, 0), shape, _jnp.float32)
    for i, name in enumerate(TWIN_WEIGHTS):
        w = inp[name].astype(_jnp.float32)
        if MOMENT_SCALE is None:
            s = _jnp.sqrt(_jnp.mean(_jnp.square(w)) + 1e-30)
        else:
            s = MOMENT_SCALE[name]
        km, kv = _jax.random.split(_jax.random.fold_in(key, i + 1))
        out[name] = w
        out["m_" + name] = s * _jax.random.normal(km, w.shape, _jnp.float32)
        out["v_" + name] = (s * s) * _jax.random.uniform(kv, w.shape, _jnp.float32, 0.5, 1.5)
    if N_MICROBATCH > 1:
        for name, axis in PER_EXAMPLE_BATCH_AXIS.items():
            out[name] = _to_microbatches(out[name], axis)
    return {'x': out['x'], 'norm_ffn1': out['norm_ffn1'], 'ffn1_w_in': out['ffn1_w_in'], 'ffn1_w_out': out['ffn1_w_out'], 'norm_mix': out['norm_mix'], 'mix_w_in': out['mix_w_in'], 'ssm_a_re': out['ssm_a_re'], 'ssm_a_im': out['ssm_a_im'], 'ssm_log_dt': out['ssm_log_dt'], 'ssm_b_re': out['ssm_b_re'], 'ssm_b_im': out['ssm_b_im'], 'ssm_c_re': out['ssm_c_re'], 'ssm_c_im': out['ssm_c_im'], 'ssm_d': out['ssm_d'], 'ssm_glu_w': out['ssm_glu_w'], 'ssm_glu_b': out['ssm_glu_b'], 'gm_v_gain': out['gm_v_gain'], 'gm_w_s': out['gm_w_s'], 'gm_b_s': out['gm_b_s'], 'gain_ssm_out': out['gain_ssm_out'], 'gain_gm_out': out['gain_gm_out'], 'mix_w_out': out['mix_w_out'], 'norm_ffn2': out['norm_ffn2'], 'ffn2_w_in': out['ffn2_w_in'], 'ffn2_w_out': out['ffn2_w_out'], 'norm_final': out['norm_final'], 'loss_target': out['loss_target'], 'm_norm_ffn1': out['m_norm_ffn1'], 'm_ffn1_w_in': out['m_ffn1_w_in'], 'm_ffn1_w_out': out['m_ffn1_w_out'], 'm_norm_mix': out['m_norm_mix'], 'm_mix_w_in': out['m_mix_w_in'], 'm_ssm_a_re': out['m_ssm_a_re'], 'm_ssm_a_im': out['m_ssm_a_im'], 'm_ssm_log_dt': out['m_ssm_log_dt'], 'm_ssm_b_re': out['m_ssm_b_re'], 'm_ssm_b_im': out['m_ssm_b_im'], 'm_ssm_c_re': out['m_ssm_c_re'], 'm_ssm_c_im': out['m_ssm_c_im'], 'm_ssm_d': out['m_ssm_d'], 'm_ssm_glu_w': out['m_ssm_glu_w'], 'm_ssm_glu_b': out['m_ssm_glu_b'], 'm_gm_v_gain': out['m_gm_v_gain'], 'm_gm_w_s': out['m_gm_w_s'], 'm_gm_b_s': out['m_gm_b_s'], 'm_gain_ssm_out': out['m_gain_ssm_out'], 'm_gain_gm_out': out['m_gain_gm_out'], 'm_mix_w_out': out['m_mix_w_out'], 'm_norm_ffn2': out['m_norm_ffn2'], 'm_ffn2_w_in': out['m_ffn2_w_in'], 'm_ffn2_w_out': out['m_ffn2_w_out'], 'm_norm_final': out['m_norm_final'], 'v_norm_ffn1': out['v_norm_ffn1'], 'v_ffn1_w_in': out['v_ffn1_w_in'], 'v_ffn1_w_out': out['v_ffn1_w_out'], 'v_norm_mix': out['v_norm_mix'], 'v_mix_w_in': out['v_mix_w_in'], 'v_ssm_a_re': out['v_ssm_a_re'], 'v_ssm_a_im': out['v_ssm_a_im'], 'v_ssm_log_dt': out['v_ssm_log_dt'], 'v_ssm_b_re': out['v_ssm_b_re'], 'v_ssm_b_im': out['v_ssm_b_im'], 'v_ssm_c_re': out['v_ssm_c_re'], 'v_ssm_c_im': out['v_ssm_c_im'], 'v_ssm_d': out['v_ssm_d'], 'v_ssm_glu_w': out['v_ssm_glu_w'], 'v_ssm_glu_b': out['v_ssm_glu_b'], 'v_gm_v_gain': out['v_gm_v_gain'], 'v_gm_w_s': out['v_gm_w_s'], 'v_gm_b_s': out['v_gm_b_s'], 'v_gain_ssm_out': out['v_gain_ssm_out'], 'v_gain_gm_out': out['v_gain_gm_out'], 'v_mix_w_out': out['v_mix_w_out'], 'v_norm_ffn2': out['v_norm_ffn2'], 'v_ffn2_w_in': out['v_ffn2_w_in'], 'v_ffn2_w_out': out['v_ffn2_w_out'], 'v_norm_final': out['v_norm_final']}


def _loss(weights, diff, rest, loss_target):
    with _jax.named_scope("forward"):
        args = {**rest, TWIN_DIFF_INPUT: diff, **{k: w.astype(_WEIGHT_DTYPES[k]) for k, w in weights.items()}}
        y = _forward(args)
    with _jax.named_scope("loss_head"):
        err = _jnp.square(y.astype(_jnp.float32) - loss_target)
        return 0.5 * _jnp.sum(_jnp.mean(err, axis=-1)) if err.ndim else 0.5 * err


def _adamw(w, g, m, v):
    m = ADAM_B1 * m + (1.0 - ADAM_B1) * g
    v = ADAM_B2 * v + (1.0 - ADAM_B2) * _jnp.square(g)
    m_hat = m / (1.0 - ADAM_B1 ** ADAM_STEP)
    v_hat = v / (1.0 - ADAM_B2 ** ADAM_STEP)
    delta = -ADAM_LR * (m_hat / (_jnp.sqrt(v_hat) + ADAM_EPS) + ADAM_WD * w)
    return delta, m, v


def reference(x, norm_ffn1, ffn1_w_in, ffn1_w_out, norm_mix, mix_w_in, ssm_a_re, ssm_a_im, ssm_log_dt, ssm_b_re, ssm_b_im, ssm_c_re, ssm_c_im, ssm_d, ssm_glu_w, ssm_glu_b, gm_v_gain, gm_w_s, gm_b_s, gain_ssm_out, gain_gm_out, mix_w_out, norm_ffn2, ffn2_w_in, ffn2_w_out, norm_final, loss_target, m_norm_ffn1, m_ffn1_w_in, m_ffn1_w_out, m_norm_mix, m_mix_w_in, m_ssm_a_re, m_ssm_a_im, m_ssm_log_dt, m_ssm_b_re, m_ssm_b_im, m_ssm_c_re, m_ssm_c_im, m_ssm_d, m_ssm_glu_w, m_ssm_glu_b, m_gm_v_gain, m_gm_w_s, m_gm_b_s, m_gain_ssm_out, m_gain_gm_out, m_mix_w_out, m_norm_ffn2, m_ffn2_w_in, m_ffn2_w_out, m_norm_final, v_norm_ffn1, v_ffn1_w_in, v_ffn1_w_out, v_norm_mix, v_mix_w_in, v_ssm_a_re, v_ssm_a_im, v_ssm_log_dt, v_ssm_b_re, v_ssm_b_im, v_ssm_c_re, v_ssm_c_im, v_ssm_d, v_ssm_glu_w, v_ssm_glu_b, v_gm_v_gain, v_gm_w_s, v_gm_b_s, v_gain_ssm_out, v_gain_gm_out, v_mix_w_out, v_norm_ffn2, v_ffn2_w_in, v_ffn2_w_out, v_norm_final):
    given = dict(x=x, norm_ffn1=norm_ffn1, ffn1_w_in=ffn1_w_in, ffn1_w_out=ffn1_w_out, norm_mix=norm_mix, mix_w_in=mix_w_in, ssm_a_re=ssm_a_re, ssm_a_im=ssm_a_im, ssm_log_dt=ssm_log_dt, ssm_b_re=ssm_b_re, ssm_b_im=ssm_b_im, ssm_c_re=ssm_c_re, ssm_c_im=ssm_c_im, ssm_d=ssm_d, ssm_glu_w=ssm_glu_w, ssm_glu_b=ssm_glu_b, gm_v_gain=gm_v_gain, gm_w_s=gm_w_s, gm_b_s=gm_b_s, gain_ssm_out=gain_ssm_out, gain_gm_out=gain_gm_out, mix_w_out=mix_w_out, norm_ffn2=norm_ffn2, ffn2_w_in=ffn2_w_in, ffn2_w_out=ffn2_w_out, norm_final=norm_final, loss_target=loss_target, m_norm_ffn1=m_norm_ffn1, m_ffn1_w_in=m_ffn1_w_in, m_ffn1_w_out=m_ffn1_w_out, m_norm_mix=m_norm_mix, m_mix_w_in=m_mix_w_in, m_ssm_a_re=m_ssm_a_re, m_ssm_a_im=m_ssm_a_im, m_ssm_log_dt=m_ssm_log_dt, m_ssm_b_re=m_ssm_b_re, m_ssm_b_im=m_ssm_b_im, m_ssm_c_re=m_ssm_c_re, m_ssm_c_im=m_ssm_c_im, m_ssm_d=m_ssm_d, m_ssm_glu_w=m_ssm_glu_w, m_ssm_glu_b=m_ssm_glu_b, m_gm_v_gain=m_gm_v_gain, m_gm_w_s=m_gm_w_s, m_gm_b_s=m_gm_b_s, m_gain_ssm_out=m_gain_ssm_out, m_gain_gm_out=m_gain_gm_out, m_mix_w_out=m_mix_w_out, m_norm_ffn2=m_norm_ffn2, m_ffn2_w_in=m_ffn2_w_in, m_ffn2_w_out=m_ffn2_w_out, m_norm_final=m_norm_final, v_norm_ffn1=v_norm_ffn1, v_ffn1_w_in=v_ffn1_w_in, v_ffn1_w_out=v_ffn1_w_out, v_norm_mix=v_norm_mix, v_mix_w_in=v_mix_w_in, v_ssm_a_re=v_ssm_a_re, v_ssm_a_im=v_ssm_a_im, v_ssm_log_dt=v_ssm_log_dt, v_ssm_b_re=v_ssm_b_re, v_ssm_b_im=v_ssm_b_im, v_ssm_c_re=v_ssm_c_re, v_ssm_c_im=v_ssm_c_im, v_ssm_d=v_ssm_d, v_ssm_glu_w=v_ssm_glu_w, v_ssm_glu_b=v_ssm_glu_b, v_gm_v_gain=v_gm_v_gain, v_gm_w_s=v_gm_w_s, v_gm_b_s=v_gm_b_s, v_gain_ssm_out=v_gain_ssm_out, v_gain_gm_out=v_gain_gm_out, v_mix_w_out=v_mix_w_out, v_norm_ffn2=v_norm_ffn2, v_ffn2_w_in=v_ffn2_w_in, v_ffn2_w_out=v_ffn2_w_out, v_norm_final=v_norm_final)
    weights = {n: given[n] for n in TWIN_WEIGHTS}
    shared = {n: given[n] for n in SHARED_INPUTS}
    per_example = {n: given[n] for n in ['x']}
    grad_fn = _jax.value_and_grad(_loss, argnums=(0, 1))

    def one_microbatch(ex, loss_target):
        ex = dict(ex)
        diff = ex.pop(TWIN_DIFF_INPUT)
        return grad_fn(weights, diff, {**shared, **ex}, loss_target)

    if N_MICROBATCH == 1:
        loss, (grad_w, grad_x) = one_microbatch(per_example, given["loss_target"])
    else:
        def body(carry, xs):
            loss_sum, grad_sum = carry
            l_k, (gw_k, gx_k) = one_microbatch(xs[0], xs[1])
            with _jax.named_scope("update"):
                return (loss_sum + l_k, _jax.tree.map(_jnp.add, grad_sum, gw_k)), gx_k

        init = (_jnp.zeros((), _jnp.float32), _jax.tree.map(_jnp.zeros_like, weights))
        (loss, grad_w), grad_x = _jax.lax.scan(body, init, (per_example, given["loss_target"]))
    with _jax.named_scope("update"):
        delta_w, new_m, new_v = {}, {}, {}
        for n in TWIN_WEIGHTS:
            delta_w[n], new_m[n], new_v[n] = _adamw(weights[n], grad_w[n], given["m_" + n], given["v_" + n])
    return (loss, grad_x, *[grad_w[n] for n in TWIN_WEIGHTS], *[delta_w[n] for n in TWIN_WEIGHTS],
            *[new_m[n] for n in TWIN_WEIGHTS], *[new_v[n] for n in TWIN_WEIGHTS])
```

```python
import functools
import math

import jax
import jax.numpy as jnp
from jax import lax
from jax.experimental import pallas as pl
from jax.experimental.pallas import tpu as pltpu

F32 = jnp.float32
BF16 = jnp.bfloat16
MESH = pl.DeviceIdType.MESH

EPS = 1e-6
SSM_CH = 16
SSM_STATE = 64
GM_CHUNK = 128
GM_HEAD_DIM = 128
SUBLANES = 8
LANES = 128
GROUPS_PER_BLOCK = LANES // SSM_CH
STATES_PER_BLOCK = GROUPS_PER_BLOCK * SSM_STATE
SSM_TIME_CHUNK = 128
N_CHIPS = 4
N_DEV = 8

ADAM_LR = 0.001
ADAM_B1 = 0.9
ADAM_B2 = 0.999
ADAM_EPS = 1e-08
ADAM_WD = 0.01
ADAM_STEP = 10

VMEM_LIMIT = 56 * 1024 * 1024


def _tile(dim, pref, align):
    best = None
    t = align
    while t <= min(dim, pref):
        if dim % t == 0:
            best = t
        t += align
    return best if best is not None else dim


def _params(*sem):
    return pltpu.CompilerParams(dimension_semantics=sem, vmem_limit_bytes=VMEM_LIMIT)


def _gelu(x):
    c = math.sqrt(2.0 / math.pi)
    return 0.5 * x * (1.0 + jnp.tanh(c * (x + 0.044715 * x * x * x)))


def _gelu_and_grad(x):
    c = math.sqrt(2.0 / math.pi)
    t = jnp.tanh(c * (x + 0.044715 * x * x * x))
    g = 0.5 * x * (1.0 + t)
    dg = 0.5 * (1.0 + t) + 0.5 * x * (1.0 - t * t) * c * (1.0 + 3.0 * 0.044715 * x * x)
    return g, dg


def _sigmoid(x):
    return 1.0 / (1.0 + jnp.exp(-x))


def _matmul(a, b, mode, *, out_dtype=F32, scale=1.0, res=None, tm=512, tn=1024, tk=1024, name="mm"):
    if mode == "nn":
        (m, k), (k2, n) = a.shape, b.shape
    elif mode == "nt":
        (m, k), (n, k2) = a.shape, b.shape
    else:
        (k, m), (k2, n) = a.shape, b.shape
    assert k == k2, (a.shape, b.shape, mode)
    tm = _tile(m, tm, 16 if mode != "tn" else LANES)
    tn = _tile(n, tn, LANES)
    tk = _tile(k, tk, LANES if mode != "tn" else 16)
    nk = k // tk
    grid = (m // tm, n // tn, nk)
    if mode == "nn":
        a_spec = pl.BlockSpec((tm, tk), lambda i, j, kk: (i, kk))
        b_spec = pl.BlockSpec((tk, tn), lambda i, j, kk: (kk, j))
        dims = (((1,), (0,)), ((), ()))
    elif mode == "nt":
        a_spec = pl.BlockSpec((tm, tk), lambda i, j, kk: (i, kk))
        b_spec = pl.BlockSpec((tn, tk), lambda i, j, kk: (j, kk))
        dims = (((1,), (1,)), ((), ()))
    else:
        a_spec = pl.BlockSpec((tk, tm), lambda i, j, kk: (kk, i))
        b_spec = pl.BlockSpec((tk, tn), lambda i, j, kk: (kk, j))
        dims = (((0,), (0,)), ((), ()))
    o_spec = pl.BlockSpec((tm, tn), lambda i, j, kk: (i, j))
    has_res = res is not None

    def body(*refs):
        if has_res:
            a_ref, b_ref, r_ref, o_ref = refs[:4]
            acc_ref = refs[4] if nk > 1 else None
        else:
            a_ref, b_ref, o_ref = refs[:3]
            r_ref = None
            acc_ref = refs[3] if nk > 1 else None
        part = lax.dot_general(a_ref[...].astype(BF16), b_ref[...].astype(BF16), dims,
                               preferred_element_type=F32)

        def finish(r):
            if scale != 1.0:
                r = r * scale
            if has_res:
                r = r + r_ref[...].astype(F32)
            o_ref[...] = r.astype(o_ref.dtype)

        if nk == 1:
            finish(part)
        else:
            kk = pl.program_id(2)

            @pl.when(kk == 0)
            def _():
                acc_ref[...] = part

            @pl.when(kk > 0)
            def _():
                acc_ref[...] += part

            @pl.when(kk == nk - 1)
            def _():
                finish(acc_ref[...])

    in_specs = [a_spec, b_spec]
    args = [a, b]
    if has_res:
        in_specs.append(o_spec)
        args.append(res)
    return pl.pallas_call(
        body, name=name, grid=grid, in_specs=in_specs, out_specs=o_spec,
        out_shape=jax.ShapeDtypeStruct((m, n), out_dtype),
        scratch_shapes=[pltpu.VMEM((tm, tn), F32)] if nk > 1 else [],
        compiler_params=_params("parallel", "parallel", "arbitrary"),
    )(*args)


def _rmsnorm_fwd(x, gain, name):
    n, d = x.shape
    tm = _tile(n, 512, 16)

    def body(x_ref, g_ref, o_ref):
        xv = x_ref[...]
        r = lax.rsqrt(jnp.mean(xv * xv, axis=-1, keepdims=True) + EPS)
        o_ref[...] = (xv * r * g_ref[...]).astype(o_ref.dtype)

    return pl.pallas_call(
        body, name=name, grid=(n // tm,),
        in_specs=[pl.BlockSpec((tm, d), lambda i: (i, 0)), pl.BlockSpec((1, d), lambda i: (0, 0))],
        out_specs=pl.BlockSpec((tm, d), lambda i: (i, 0)),
        out_shape=jax.ShapeDtypeStruct((n, d), BF16),
        compiler_params=_params("parallel"),
    )(x, gain.reshape(1, d))


def _rmsnorm_bwd(x, gain, dh, dres, name):
    n, d = x.shape
    tm = _tile(n, 512, 8)
    steps = n // tm

    def body(x_ref, g_ref, dh_ref, dr_ref, dx_ref, dg_ref, acc_ref):
        i = pl.program_id(0)
        xv = x_ref[...]
        dhv = dh_ref[...].astype(F32)
        r = lax.rsqrt(jnp.mean(xv * xv, axis=-1, keepdims=True) + EPS)
        xh = xv * r
        dyg = dhv * g_ref[...]
        mean = jnp.mean(dyg * xh, axis=-1, keepdims=True)
        dx_ref[...] = dr_ref[...] + r * (dyg - xh * mean)
        part = jnp.sum((dhv * xh).reshape(tm // SUBLANES, SUBLANES, d), axis=0)

        @pl.when(i == 0)
        def _():
            acc_ref[...] = part

        @pl.when(i > 0)
        def _():
            acc_ref[...] += part

        @pl.when(i == steps - 1)
        def _():
            dg_ref[...] = jnp.sum(acc_ref[...], axis=0, keepdims=True)

    row = pl.BlockSpec((tm, d), lambda i: (i, 0))
    vec = pl.BlockSpec((1, d), lambda i: (0, 0))
    dx, dg = pl.pallas_call(
        body, name=name, grid=(steps,),
        in_specs=[row, vec, row, row], out_specs=[row, vec],
        out_shape=[jax.ShapeDtypeStruct((n, d), F32), jax.ShapeDtypeStruct((1, d), F32)],
        scratch_shapes=[pltpu.VMEM((SUBLANES, d), F32)],
        compiler_params=_params("arbitrary"),
    )(x, gain.reshape(1, d), dh, dres)
    return dx, dg.reshape(d)


def _loss_head(x, gain, target):
    n, d = x.shape
    tm = _tile(n, 512, 8)
    steps = n // tm

    def body(x_ref, g_ref, t_ref, dx_ref, dg_ref, loss_ref, acc_ref, lacc_ref):
        i = pl.program_id(0)
        xv = x_ref[...]
        g = g_ref[...]
        r = lax.rsqrt(jnp.mean(xv * xv, axis=-1, keepdims=True) + EPS)
        xh = xv * r
        err = xh * g - t_ref[...]
        dy = err * (1.0 / d)
        dyg = dy * g
        mean = jnp.mean(dyg * xh, axis=-1, keepdims=True)
        dx_ref[...] = r * (dyg - xh * mean)
        part = jnp.sum((dy * xh).reshape(tm // SUBLANES, SUBLANES, d), axis=0)
        lpart = jnp.sum((err * err).reshape(tm // SUBLANES, SUBLANES, d), axis=0)

        @pl.when(i == 0)
        def _():
            acc_ref[...] = part
            lacc_ref[...] = lpart

        @pl.when(i > 0)
        def _():
            acc_ref[...] += part
            lacc_ref[...] += lpart

        @pl.when(i == steps - 1)
        def _():
            dg_ref[...] = jnp.sum(acc_ref[...], axis=0, keepdims=True)
            tot = jnp.sum(jnp.sum(lacc_ref[...], axis=0, keepdims=True), axis=1, keepdims=True)
            loss_ref[...] = jnp.broadcast_to(tot * (0.5 / d), loss_ref.shape)

    row = pl.BlockSpec((tm, d), lambda i: (i, 0))
    vec = pl.BlockSpec((1, d), lambda i: (0, 0))
    dx, dg, loss = pl.pallas_call(
        body, name="loss_head", grid=(steps,),
        in_specs=[row, vec, row],
        out_specs=[row, vec, pl.BlockSpec((1, LANES), lambda i: (0, 0))],
        out_shape=[jax.ShapeDtypeStruct((n, d), F32), jax.ShapeDtypeStruct((1, d), F32),
                   jax.ShapeDtypeStruct((1, LANES), F32)],
        scratch_shapes=[pltpu.VMEM((SUBLANES, d), F32), pltpu.VMEM((SUBLANES, d), F32)],
        compiler_params=_params("arbitrary"),
    )(x, gain.reshape(1, d), target)
    return dx, dg.reshape(d), loss[0, 0]


def _swiglu_fwd(gu, name):
    n, f2 = gu.shape
    f = f2 // 2
    tm = _tile(n, 512, 16)
    tc = _tile(f, 1536, LANES)
    nj = f // tc

    def body(g_ref, u_ref, o_ref):
        g = g_ref[...].astype(F32)
        o_ref[...] = (g * _sigmoid(g) * u_ref[...].astype(F32)).astype(o_ref.dtype)

    return pl.pallas_call(
        body, name=name, grid=(n // tm, nj),
        in_specs=[pl.BlockSpec((tm, tc), lambda i, j: (i, j)), pl.BlockSpec((tm, tc), lambda i, j: (i, j + nj))],
        out_specs=pl.BlockSpec((tm, tc), lambda i, j: (i, j)),
        out_shape=jax.ShapeDtypeStruct((n, f), BF16),
        compiler_params=_params("parallel", "parallel"),
    )(gu, gu)


def _swiglu_bwd(gu, da, name):
    n, f2 = gu.shape
    f = f2 // 2
    tm = _tile(n, 512, 16)
    tc = _tile(f, 1536, LANES)
    nj = f // tc

    def body(g_ref, u_ref, da_ref, o_ref):
        j = pl.program_id(1)
        g = g_ref[...].astype(F32)
        dav = da_ref[...].astype(F32)
        s = _sigmoid(g)

        @pl.when(j < nj)
        def _():
            o_ref[...] = (dav * u_ref[...].astype(F32) * (s * (1.0 + g * (1.0 - s)))).astype(o_ref.dtype)

        @pl.when(j >= nj)
        def _():
            o_ref[...] = (dav * g * s).astype(o_ref.dtype)

    return pl.pallas_call(
        body, name=name, grid=(n // tm, 2 * nj),
        in_specs=[pl.BlockSpec((tm, tc), lambda i, j: (i, j % nj)),
                  pl.BlockSpec((tm, tc), lambda i, j: (i, j % nj + nj)),
                  pl.BlockSpec((tm, tc), lambda i, j: (i, j % nj))],
        out_specs=pl.BlockSpec((tm, tc), lambda i, j: (i, j)),
        out_shape=jax.ShapeDtypeStruct((n, f2), BF16),
        compiler_params=_params("parallel", "parallel"),
    )(gu, gu, da)


def _tril_mask():
    t = lax.broadcasted_iota(jnp.int32, (GM_CHUNK, GM_CHUNK), 0)
    s = lax.broadcasted_iota(jnp.int32, (GM_CHUNK, GM_CHUNK), 1)
    return s <= t


def _gmlp_fwd(zgm, v_gain, w_s, bias_tile, name):
    n, w2 = zgm.shape
    w = w2 // 2
    heads = w // GM_HEAD_DIM
    tm = _tile(n, 512, GM_CHUNK)
    nq = tm // GM_CHUNK

    def body(u_ref, v_ref, gain_ref, w_ref, b_ref, o_ref):
        mask = _tril_mask()
        ug = _gelu(u_ref[...])
        vg = _gelu(v_ref[...])
        for h in range(heads):
            cols = slice(h * GM_HEAD_DIM, (h + 1) * GM_HEAD_DIM)
            vh = vg[:, cols]
            r = lax.rsqrt(jnp.mean(vh * vh, axis=-1, keepdims=True) + EPS)
            vn = (vh * r * gain_ref[:, cols]).astype(BF16)
            wm = jnp.where(mask, w_ref[h], 0.0).astype(BF16)
            for q in range(nq):
                rows = slice(q * GM_CHUNK, (q + 1) * GM_CHUNK)
                s = jnp.dot(wm, vn[rows], preferred_element_type=F32) + b_ref[:, cols]
                o_ref[rows, cols] = ug[rows, cols] * s

    return pl.pallas_call(
        body, name=name, grid=(n // tm,),
        in_specs=[pl.BlockSpec((tm, w), lambda i: (i, 0)), pl.BlockSpec((tm, w), lambda i: (i, 1)),
                  pl.BlockSpec((1, w), lambda i: (0, 0)),
                  pl.BlockSpec((heads, GM_CHUNK, GM_CHUNK), lambda i: (0, 0, 0)),
                  pl.BlockSpec((GM_CHUNK, w), lambda i: (0, 0))],
        out_specs=pl.BlockSpec((tm, w), lambda i: (i, 0)),
        out_shape=jax.ShapeDtypeStruct((n, w), F32),
        compiler_params=_params("parallel"),
    )(zgm, zgm, v_gain.reshape(1, w), w_s, bias_tile)


def _gmlp_bwd(zgm, dy, v_gain, w_s, bias_tile, name):
    n, w2 = zgm.shape
    w = w2 // 2
    heads = w // GM_HEAD_DIM
    tm = _tile(n, 512, GM_CHUNK)
    nq = tm // GM_CHUNK
    steps = n // tm

    def body(z_ref, dy_ref, gain_ref, w_ref, b_ref, dz_ref, dw_ref, db_ref, dgain_ref):
        i = pl.program_id(0)
        mask = _tril_mask()

        @pl.when(i == 0)
        def _():
            dw_ref[...] = jnp.zeros_like(dw_ref)
            db_ref[...] = jnp.zeros_like(db_ref)
            dgain_ref[...] = jnp.zeros_like(dgain_ref)

        ug, dug_du = _gelu_and_grad(z_ref[:, 0:w])
        vg, dvg_dv = _gelu_and_grad(z_ref[:, w:w2])
        dyv = dy_ref[...]
        for h in range(heads):
            cols = slice(h * GM_HEAD_DIM, (h + 1) * GM_HEAD_DIM)
            vh = vg[:, cols]
            r = lax.rsqrt(jnp.mean(vh * vh, axis=-1, keepdims=True) + EPS)
            vhat = vh * r
            gain = gain_ref[:, cols]
            vn = (vhat * gain).astype(BF16)
            wm = jnp.where(mask, w_ref[h], 0.0).astype(BF16)
            dvn_parts = []
            for q in range(nq):
                rows = slice(q * GM_CHUNK, (q + 1) * GM_CHUNK)
                s = jnp.dot(wm, vn[rows], preferred_element_type=F32) + b_ref[:, cols]
                dyq = dyv[rows, cols]
                dz_ref[rows, cols] = dyq * s * dug_du[rows, cols]
                ds = dyq * ug[rows, cols]
                db_ref[:, cols] += ds
                dsb = ds.astype(BF16)
                dw_ref[h] += lax.dot_general(dsb, vn[rows], (((1,), (1,)), ((), ())), preferred_element_type=F32)
                dvn_parts.append(lax.dot_general(wm, dsb, (((0,), (0,)), ((), ())), preferred_element_type=F32))
            dvn = jnp.concatenate(dvn_parts, axis=0) if nq > 1 else dvn_parts[0]
            dgain_ref[:, cols] += jnp.sum(dvn * vhat, axis=0, keepdims=True)
            dvhat = dvn * gain
            mean = jnp.mean(dvhat * vhat, axis=-1, keepdims=True)
            dz_ref[:, w + h * GM_HEAD_DIM:w + (h + 1) * GM_HEAD_DIM] = r * (dvhat - vhat * mean) * dvg_dv[:, cols]

        @pl.when(i == steps - 1)
        def _():
            for h in range(heads):
                dw_ref[h] = jnp.where(mask, dw_ref[h], 0.0)

    dz, dw, db, dgain = pl.pallas_call(
        body, name=name, grid=(steps,),
        in_specs=[pl.BlockSpec((tm, w2), lambda i: (i, 0)), pl.BlockSpec((tm, w), lambda i: (i, 0)),
                  pl.BlockSpec((1, w), lambda i: (0, 0)),
                  pl.BlockSpec((heads, GM_CHUNK, GM_CHUNK), lambda i: (0, 0, 0)),
                  pl.BlockSpec((GM_CHUNK, w), lambda i: (0, 0))],
        out_specs=[pl.BlockSpec((tm, w2), lambda i: (i, 0)),
                   pl.BlockSpec((heads, GM_CHUNK, GM_CHUNK), lambda i: (0, 0, 0)),
                   pl.BlockSpec((GM_CHUNK, w), lambda i: (0, 0)),
                   pl.BlockSpec((1, w), lambda i: (0, 0))],
        out_shape=[jax.ShapeDtypeStruct((n, w2), F32), jax.ShapeDtypeStruct((heads, GM_CHUNK, GM_CHUNK), F32),
                   jax.ShapeDtypeStruct((GM_CHUNK, w), F32), jax.ShapeDtypeStruct((1, w), F32)],
        compiler_params=_params("arbitrary"),
    )(zgm, dy, v_gain.reshape(1, w), w_s, bias_tile)
    return dz, dw, db, dgain.reshape(w)


def _mixnorm_fwd(y_ssm, y_gm, g1, g2, name):
    n, w = y_ssm.shape
    tm = _tile(n, 512, 16)

    def body(a_ref, b_ref, g1_ref, g2_ref, o_ref):
        for src, g_ref, lo in ((a_ref, g1_ref, 0), (b_ref, g2_ref, w)):
            v = src[...]
            r = lax.rsqrt(jnp.mean(v * v, axis=-1, keepdims=True) + EPS)
            o_ref[:, lo:lo + w] = (v * r * g_ref[...]).astype(o_ref.dtype)

    row = pl.BlockSpec((tm, w), lambda i: (i, 0))
    vec = pl.BlockSpec((1, w), lambda i: (0, 0))
    return pl.pallas_call(
        body, name=name, grid=(n // tm,),
        in_specs=[row, row, vec, vec], out_specs=pl.BlockSpec((tm, 2 * w), lambda i: (i, 0)),
        out_shape=jax.ShapeDtypeStruct((n, 2 * w), BF16),
        compiler_params=_params("parallel"),
    )(y_ssm, y_gm, g1.reshape(1, w), g2.reshape(1, w))


def _mixnorm_bwd(y_ssm, y_gm, g1, g2, dycat, name):
    n, w = y_ssm.shape
    tm = _tile(n, 512, 8)
    steps = n // tm

    def body(a_ref, b_ref, g1_ref, g2_ref, d_ref, da_ref, db_ref, dg1_ref, dg2_ref):
        i = pl.program_id(0)

        @pl.when(i == 0)
        def _():
            dg1_ref[...] = jnp.zeros_like(dg1_ref)
            dg2_ref[...] = jnp.zeros_like(dg2_ref)

        for src, g_ref, lo, dst, dg_ref in ((a_ref, g1_ref, 0, da_ref, dg1_ref), (b_ref, g2_ref, w, db_ref, dg2_ref)):
            v = src[...]
            dh = d_ref[:, lo:lo + w]
            r = lax.rsqrt(jnp.mean(v * v, axis=-1, keepdims=True) + EPS)
            vh = v * r
            dyg = dh * g_ref[...]
            mean = jnp.mean(dyg * vh, axis=-1, keepdims=True)
            dst[...] = r * (dyg - vh * mean)
            dg_ref[...] += jnp.sum(dh * vh, axis=0, keepdims=True)

    row = pl.BlockSpec((tm, w), lambda i: (i, 0))
    vec = pl.BlockSpec((1, w), lambda i: (0, 0))
    da, db, dg1, dg2 = pl.pallas_call(
        body, name=name, grid=(steps,),
        in_specs=[row, row, vec, vec, pl.BlockSpec((tm, 2 * w), lambda i: (i, 0))],
        out_specs=[row, row, vec, vec],
        out_shape=[jax.ShapeDtypeStruct((n, w), F32), jax.ShapeDtypeStruct((n, w), F32),
                   jax.ShapeDtypeStruct((1, w), F32), jax.ShapeDtypeStruct((1, w), F32)],
        compiler_params=_params("arbitrary"),
    )(y_ssm, y_gm, g1.reshape(1, w), g2.reshape(1, w), dycat)
    return da, db, dg1.reshape(w), dg2.reshape(w)


def _discretise(a_re, a_im, log_dt, bt_re, bt_im):
    dt = jnp.exp(log_dt)
    e = jnp.exp(a_re * dt)
    ang = a_im * dt
    lr = e * jnp.cos(ang)
    li = e * jnp.sin(ang)
    den = a_re * a_re + a_im * a_im
    cr = ((lr - 1.0) * a_re + li * a_im) / den
    ci = (li * a_re - (lr - 1.0) * a_im) / den
    cr3 = cr[:, None, :]
    ci3 = ci[:, None, :]
    return lr, li, cr3 * bt_re - ci3 * bt_im, cr3 * bt_im + ci3 * bt_re


def _disc_fwd(a_re, a_im, log_dt, bt_re, bt_im):
    g, p = a_re.shape
    c = bt_re.shape[1]

    def body(are_ref, aim_ref, ldt_ref, bre_ref, bim_ref, lr_ref, li_ref, bbr_ref, bbi_ref):
        lr, li, bbr, bbi = _discretise(are_ref[...], aim_ref[...], ldt_ref[...], bre_ref[...], bim_ref[...])
        lr_ref[...] = lr
        li_ref[...] = li
        bbr_ref[...] = bbr
        bbi_ref[...] = bbi

    return pl.pallas_call(
        body, name="s5_discretise",
        out_shape=[jax.ShapeDtypeStruct((g, p), F32), jax.ShapeDtypeStruct((g, p), F32),
                   jax.ShapeDtypeStruct((g, c, p), F32), jax.ShapeDtypeStruct((g, c, p), F32)],
    )(a_re, a_im, log_dt, bt_re, bt_im)


def _disc_bwd(a_re, a_im, log_dt, bt_re, bt_im, dlr, dli, dbbr, dbbi):
    g, p = a_re.shape
    c = bt_re.shape[1]

    def body(are_ref, aim_ref, ldt_ref, bre_ref, bim_ref, dlr_ref, dli_ref, dbbr_ref, dbbi_ref,
             dare_ref, daim_ref, dldt_ref, dbre_ref, dbim_ref):
        _, vjp = jax.vjp(_discretise, are_ref[...], aim_ref[...], ldt_ref[...], bre_ref[...], bim_ref[...])
        dare, daim, dldt, dbre, dbim = vjp((dlr_ref[...], dli_ref[...], dbbr_ref[...], dbbi_ref[...]))
        dare_ref[...] = dare
        daim_ref[...] = daim
        dldt_ref[...] = dldt
        dbre_ref[...] = dbre
        dbim_ref[...] = dbim

    return pl.pallas_call(
        body, name="s5_discretise_bwd",
        out_shape=[jax.ShapeDtypeStruct((g, p), F32), jax.ShapeDtypeStruct((g, p), F32),
                   jax.ShapeDtypeStruct((g, 1), F32),
                   jax.ShapeDtypeStruct((g, c, p), F32), jax.ShapeDtypeStruct((g, c, p), F32)],
    )(a_re, a_im, log_dt, bt_re, bt_im, dlr, dli, dbbr, dbbi)


def _block_diag(w, nb):
    g, a, b = w.shape
    gpb = g // nb
    eye = jnp.eye(gpb, dtype=w.dtype)
    w4 = w.reshape(nb, gpb, a, b)
    return jnp.einsum("ngab,gh->ngahb", w4, eye).reshape(nb, gpb * a, gpb * b)


def _block_diag_extract(m, gpb):
    nb, ga, gb = m.shape
    a, b = ga // gpb, gb // gpb
    m5 = m.reshape(nb, gpb, a, gpb, b)
    idx = jnp.arange(gpb)
    return m5[:, idx, :, idx, :].transpose(1, 0, 2, 3).reshape(nb * gpb, a, b)


def _ssm_operands(lr, li, bbr, bbi, c_re, c_im, d_skip, glu_w, glu_b):
    g = lr.shape[0]
    nb = g // GROUPS_PER_BLOCK
    s = STATES_PER_BLOCK
    lam = jnp.concatenate([lr.reshape(nb, 1, s), li.reshape(nb, 1, s)], axis=-1)
    b_bd = jnp.concatenate([_block_diag(bbr, nb), _block_diag(bbi, nb)], axis=-1)
    ct_re = jnp.swapaxes(c_re, 1, 2)
    ct_im = jnp.swapaxes(c_im, 1, 2)
    c_bd = jnp.concatenate([_block_diag(ct_re, nb), -_block_diag(ct_im, nb)], axis=1)
    dsk = d_skip.reshape(nb, 1, LANES)
    w_bd = jnp.concatenate([_block_diag(glu_w[:, :, :SSM_CH], nb), _block_diag(glu_w[:, :, SSM_CH:], nb)], axis=-1)
    bias = jnp.concatenate([glu_b[:, :SSM_CH].reshape(nb, 1, LANES), glu_b[:, SSM_CH:].reshape(nb, 1, LANES)], axis=-1)
    return lam, b_bd.astype(BF16), c_bd.astype(BF16), dsk, w_bd.astype(BF16), bias


def _ssm_fwd(u8, ops, name):
    lam, b_bd, c_bd, dsk, w_bd, bias = ops
    rows_total, w = u8.shape
    seq = rows_total // SUBLANES
    nb = w // LANES
    s = STATES_PER_BLOCK
    tc = _tile(seq, SSM_TIME_CHUNK, 8)
    nk = seq // tc
    rows = tc * SUBLANES

    def body(u_ref, lam_ref, b_ref, c_ref, d_ref, w_ref, bias_ref, y_ref, hb_ref, buf, st):
        k = pl.program_id(1)

        @pl.when(k == 0)
        def _():
            st[...] = jnp.zeros_like(st)

        hb_ref[...] = st[...]
        u = u_ref[...]
        buf[...] = jnp.dot(u.astype(BF16), b_ref[0], preferred_element_type=F32)
        lr = jnp.broadcast_to(lam_ref[0, :, 0:s], (SUBLANES, s))
        li = jnp.broadcast_to(lam_ref[0, :, s:2 * s], (SUBLANES, s))

        def step(t, carry):
            hr, hi = carry
            r0 = pl.multiple_of(t * SUBLANES, SUBLANES)
            nr = lr * hr - li * hi + buf[pl.ds(r0, SUBLANES), 0:s]
            ni = lr * hi + li * hr + buf[pl.ds(r0, SUBLANES), s:2 * s]
            buf[pl.ds(r0, SUBLANES), 0:s] = nr
            buf[pl.ds(r0, SUBLANES), s:2 * s] = ni
            return nr, ni

        hr, hi = lax.fori_loop(0, tc, step, (st[:, 0:s], st[:, s:2 * s]), unroll=4)
        st[:, 0:s] = hr
        st[:, s:2 * s] = hi
        y = jnp.dot(buf[...].astype(BF16), c_ref[0], preferred_element_type=F32) + d_ref[0] * u
        z = jnp.dot(_gelu(y).astype(BF16), w_ref[0], preferred_element_type=F32) + bias_ref[0]
        y_ref[...] = z[:, 0:LANES] * _sigmoid(z[:, LANES:2 * LANES])

    blk = lambda shape: pl.BlockSpec(shape, lambda b, k: (b, 0, 0))
    y8, hb = pl.pallas_call(
        body, name=name, grid=(nb, nk),
        in_specs=[pl.BlockSpec((rows, LANES), lambda b, k: (k, b)),
                  blk((1, 1, 2 * s)), blk((1, LANES, 2 * s)), blk((1, 2 * s, LANES)),
                  blk((1, 1, LANES)), blk((1, LANES, 2 * LANES)), blk((1, 1, 2 * LANES))],
        out_specs=[pl.BlockSpec((rows, LANES), lambda b, k: (k, b)),
                   pl.BlockSpec((SUBLANES, 2 * s), lambda b, k: (k, b))],
        out_shape=[jax.ShapeDtypeStruct((rows_total, w), F32),
                   jax.ShapeDtypeStruct((nk * SUBLANES, nb * 2 * s), F32)],
        scratch_shapes=[pltpu.VMEM((rows, 2 * s), F32), pltpu.VMEM((SUBLANES, 2 * s), F32)],
        compiler_params=_params("parallel", "arbitrary"),
    )(u8, lam, b_bd, c_bd, dsk, w_bd, bias)
    return y8, hb


def _ssm_bwd(u8, dy8, hb, ops, name):
    lam, b_bd, c_bd, dsk, w_bd, bias = ops
    rows_total, w = u8.shape
    seq = rows_total // SUBLANES
    nb = w // LANES
    s = STATES_PER_BLOCK
    tc = _tile(seq, SSM_TIME_CHUNK, 8)
    nk = seq // tc
    rows = tc * SUBLANES
    tn_dims = (((0,), (0,)), ((), ()))
    nt_dims = (((1,), (1,)), ((), ()))

    def body(u_ref, dy_ref, hb_ref, lam_ref, b_ref, c_ref, d_ref, w_ref, bias_ref,
             du_ref, dlam_ref, db_ref, dct_ref, dd_ref, dw_ref, dbias_ref, hbuf, gbuf, gst, lacc):
        k = pl.program_id(1)

        @pl.when(k == 0)
        def _():
            gst[...] = jnp.zeros_like(gst)
            lacc[...] = jnp.zeros_like(lacc)
            db_ref[...] = jnp.zeros_like(db_ref)
            dct_ref[...] = jnp.zeros_like(dct_ref)
            dd_ref[...] = jnp.zeros_like(dd_ref)
            dw_ref[...] = jnp.zeros_like(dw_ref)
            dbias_ref[...] = jnp.zeros_like(dbias_ref)

        u = u_ref[...]
        ub = u.astype(BF16)
        lr = jnp.broadcast_to(lam_ref[0, :, 0:s], (SUBLANES, s))
        li = jnp.broadcast_to(lam_ref[0, :, s:2 * s], (SUBLANES, s))
        hbuf[...] = jnp.dot(ub, b_ref[0], preferred_element_type=F32)

        def fstep(t, carry):
            hr, hi = carry
            r0 = pl.multiple_of(t * SUBLANES, SUBLANES)
            nr = lr * hr - li * hi + hbuf[pl.ds(r0, SUBLANES), 0:s]
            ni = lr * hi + li * hr + hbuf[pl.ds(r0, SUBLANES), s:2 * s]
            hbuf[pl.ds(r0, SUBLANES), 0:s] = nr
            hbuf[pl.ds(r0, SUBLANES), s:2 * s] = ni
            return nr, ni

        lax.fori_loop(0, tc, fstep, (hb_ref[:, 0:s], hb_ref[:, s:2 * s]), unroll=4)
        hb16 = hbuf[...].astype(BF16)
        y = jnp.dot(hb16, c_ref[0], preferred_element_type=F32) + d_ref[0] * u
        yg, dyg_dy = _gelu_and_grad(y)
        yg16 = yg.astype(BF16)
        z = jnp.dot(yg16, w_ref[0], preferred_element_type=F32) + bias_ref[0]
        z1 = z[:, 0:LANES]
        sg = _sigmoid(z[:, LANES:2 * LANES])
        dout = dy_ref[...]
        dz = jnp.concatenate([dout * sg, dout * z1 * sg * (1.0 - sg)], axis=-1)
        dz16 = dz.astype(BF16)
        dw_ref[0] += lax.dot_general(yg16, dz16, tn_dims, preferred_element_type=F32)
        dbias_ref[0] += jnp.sum(dz, axis=0, keepdims=True)
        dy = lax.dot_general(dz16, w_ref[0], nt_dims, preferred_element_type=F32) * dyg_dy
        dy16 = dy.astype(BF16)
        dd_ref[0] += jnp.sum(dy * u, axis=0, keepdims=True)
        dct_ref[0] += lax.dot_general(dy16, hb16, tn_dims, preferred_element_type=F32)
        gbuf[...] = lax.dot_general(dy16, c_ref[0], nt_dims, preferred_element_type=F32)

        def bstep(i, carry):
            gr, gi, ar, ai = carry
            t = tc - 1 - i
            r0 = pl.multiple_of(t * SUBLANES, SUBLANES)
            ngr = gbuf[pl.ds(r0, SUBLANES), 0:s] + lr * gr + li * gi
            ngi = gbuf[pl.ds(r0, SUBLANES), s:2 * s] - li * gr + lr * gi
            gbuf[pl.ds(r0, SUBLANES), 0:s] = ngr
            gbuf[pl.ds(r0, SUBLANES), s:2 * s] = ngi
            p0 = pl.multiple_of(jnp.maximum(t - 1, 0) * SUBLANES, SUBLANES)
            first = t == 0
            hpr = jnp.where(first, hb_ref[:, 0:s], hbuf[pl.ds(p0, SUBLANES), 0:s])
            hpi = jnp.where(first, hb_ref[:, s:2 * s], hbuf[pl.ds(p0, SUBLANES), s:2 * s])
            return ngr, ngi, ar + ngr * hpr + ngi * hpi, ai - ngr * hpi + ngi * hpr

        gr, gi, ar, ai = lax.fori_loop(
            0, tc, bstep, (gst[:, 0:s], gst[:, s:2 * s], lacc[:, 0:s], lacc[:, s:2 * s]), unroll=2)
        gst[:, 0:s] = gr
        gst[:, s:2 * s] = gi
        lacc[:, 0:s] = ar
        lacc[:, s:2 * s] = ai
        g16 = gbuf[...].astype(BF16)
        du_ref[...] = dy * d_ref[0] + lax.dot_general(g16, b_ref[0], nt_dims, preferred_element_type=F32)
        db_ref[0] += lax.dot_general(ub, g16, tn_dims, preferred_element_type=F32)

        @pl.when(k == nk - 1)
        def _():
            dlam_ref[0] = jnp.sum(lacc[...], axis=0, keepdims=True)

    blk = lambda shape: pl.BlockSpec(shape, lambda b, k: (b, 0, 0))
    rev = lambda b, k: (nk - 1 - k, b)
    outs = pl.pallas_call(
        body, name=name, grid=(nb, nk),
        in_specs=[pl.BlockSpec((rows, LANES), rev), pl.BlockSpec((rows, LANES), rev),
                  pl.BlockSpec((SUBLANES, 2 * s), rev),
                  blk((1, 1, 2 * s)), blk((1, LANES, 2 * s)), blk((1, 2 * s, LANES)),
                  blk((1, 1, LANES)), blk((1, LANES, 2 * LANES)), blk((1, 1, 2 * LANES))],
        out_specs=[pl.BlockSpec((rows, LANES), rev),
                   blk((1, 1, 2 * s)), blk((1, LANES, 2 * s)), blk((1, LANES, 2 * s)),
                   blk((1, 1, LANES)), blk((1, LANES, 2 * LANES)), blk((1, 1, 2 * LANES))],
        out_shape=[jax.ShapeDtypeStruct((rows_total, w), F32),
                   jax.ShapeDtypeStruct((nb, 1, 2 * s), F32), jax.ShapeDtypeStruct((nb, LANES, 2 * s), F32),
                   jax.ShapeDtypeStruct((nb, LANES, 2 * s), F32), jax.ShapeDtypeStruct((nb, 1, LANES), F32),
                   jax.ShapeDtypeStruct((nb, LANES, 2 * LANES), F32), jax.ShapeDtypeStruct((nb, 1, 2 * LANES), F32)],
        scratch_shapes=[pltpu.VMEM((rows, 2 * s), F32), pltpu.VMEM((rows, 2 * s), F32),
                        pltpu.VMEM((SUBLANES, 2 * s), F32), pltpu.VMEM((SUBLANES, 2 * s), F32)],
        compiler_params=_params("parallel", "arbitrary"),
    )(u8, dy8, hb, lam, b_bd, c_bd, dsk, w_bd, bias)
    return outs


def _to_scan_rows(a, nseq, seq):
    w = a.shape[-1]
    t = jnp.swapaxes(a.reshape(nseq, seq, w), 0, 1)
    t = jnp.pad(t, ((0, 0), (0, SUBLANES - nseq), (0, 0)))
    return t.reshape(seq * SUBLANES, w)


def _from_scan_rows(a8, nseq, seq):
    w = a8.shape[-1]
    t = a8.reshape(seq, SUBLANES, w)[:, :nseq]
    return jnp.swapaxes(t, 0, 1).reshape(nseq * seq, w)


ANY = pl.BlockSpec(memory_space=pl.ANY)

BIG = (("ffn1_w_in", True), ("ffn1_w_out", False), ("mix_w_in", True), ("mix_w_out", False),
       ("ffn2_w_in", True), ("ffn2_w_out", False))


def _my_place():
    return lax.axis_index("x"), lax.axis_index("y"), lax.axis_index("c")


def _other_chips(x, y):
    return [(1 - x, y), (x, 1 - y), (1 - x, 1 - y)]


def _half_of_shard(ref, col_sharded, chip, core, full_rows, full_cols):
    if col_sharded:
        hr, cs = full_rows // 2, full_cols // N_CHIPS
        return ref.at[:, pl.ds(pl.multiple_of(core * hr, 8), hr), pl.ds(pl.multiple_of(chip * cs, LANES), cs)]
    rs = full_rows // N_CHIPS
    return ref.at[:, pl.ds(pl.multiple_of(chip * rs + core * (rs // 2), 8), rs // 2), :]


def _all_gather_weights(shards):
    full_shapes = []
    for (name, col), sh in zip(BIG, shards):
        l, rs, cs = sh.shape
        full_shapes.append((l, rs, cs * N_CHIPS) if col else (l, rs * N_CHIPS, cs))
    nw = len(shards)

    def body(*refs):
        ins, outs = refs[:nw], refs[nw:2 * nw]
        send_sems, recv_sems, local_sems = refs[2 * nw:]
        x, y, c = _my_place()
        me = 2 * x + y
        chips = _other_chips(x, y)
        locals_, sends, fwds = [], [], []
        for wi in range(nw):
            col = BIG[wi][1]
            _, fr, fc = full_shapes[wi]
            src, dst = ins[wi], outs[wi]
            rs = src.shape[1]
            hs = rs // 2
            if col:
                cs = src.shape[2]
                place = dst.at[:, :, pl.ds(pl.multiple_of(me * cs, LANES), cs)]
            else:
                place = dst.at[:, pl.ds(pl.multiple_of(me * rs, 8), rs), :]
            cp = pltpu.make_async_copy(src, place, local_sems.at[wi])
            cp.start()
            locals_.append(cp)
            my_half = src.at[:, pl.ds(pl.multiple_of(c * hs, 8), hs), :]
            for j, (px, py) in enumerate(chips):
                cp = pltpu.make_async_remote_copy(
                    src_ref=my_half, dst_ref=_half_of_shard(dst, col, me, c, fr, fc),
                    send_sem=send_sems.at[wi * 6 + j], recv_sem=recv_sems.at[wi * 6 + j],
                    device_id=(px, py, c), device_id_type=MESH)
                cp.start()
                sends.append(cp)
        for wi in range(nw):
            col = BIG[wi][1]
            _, fr, fc = full_shapes[wi]
            dst = outs[wi]
            for j, (px, py) in enumerate(chips):
                got = _half_of_shard(dst, col, 2 * px + py, c, fr, fc)
                pltpu.make_async_remote_copy(
                    src_ref=got, dst_ref=got, send_sem=send_sems.at[wi * 6 + j], recv_sem=recv_sems.at[wi * 6 + j],
                    device_id=(px, py, c), device_id_type=MESH).wait_recv()
                cp = pltpu.make_async_remote_copy(
                    src_ref=got, dst_ref=got, send_sem=send_sems.at[wi * 6 + 3 + j], recv_sem=recv_sems.at[wi * 6 + 3 + j],
                    device_id=(x, y, 1 - c), device_id_type=MESH)
                cp.start()
                fwds.append(cp)
        for wi in range(nw):
            col = BIG[wi][1]
            _, fr, fc = full_shapes[wi]
            dst = outs[wi]
            for j, (px, py) in enumerate(chips):
                theirs = _half_of_shard(dst, col, 2 * px + py, 1 - c, fr, fc)
                pltpu.make_async_remote_copy(
                    src_ref=theirs, dst_ref=theirs, send_sem=send_sems.at[wi * 6 + 3 + j],
                    recv_sem=recv_sems.at[wi * 6 + 3 + j], device_id=(x, y, 1 - c), device_id_type=MESH).wait_recv()
        for cp in sends + fwds:
            cp.wait_send()
        for cp in locals_:
            cp.wait()

    return pl.pallas_call(
        body, name="all_gather_weights",
        in_specs=[ANY] * nw, out_specs=[ANY] * nw,
        out_shape=[jax.ShapeDtypeStruct(s, BF16) for s in full_shapes],
        scratch_shapes=[pltpu.SemaphoreType.DMA((6 * nw,)), pltpu.SemaphoreType.DMA((6 * nw,)),
                        pltpu.SemaphoreType.DMA((nw,))],
    )(*shards)


def _pair_exchange(grads):
    nw = len(grads)
    n_copies = sum(1 if col else N_CHIPS for _, col in BIG)

    def body(*refs):
        ins, outs = refs[:nw], refs[nw:2 * nw]
        send_sems, recv_sems = refs[2 * nw:]
        x, y, c = _my_place()
        copies = []
        idx = 0
        for wi in range(nw):
            col = BIG[wi][1]
            src, dst = ins[wi], outs[wi]
            fr = src.shape[1]
            if col:
                hr = fr // 2
                pieces = [(src.at[:, pl.ds(pl.multiple_of((1 - c) * hr, 8), hr), :], dst)]
            else:
                rs = fr // N_CHIPS
                hs = rs // 2
                pieces = [(src.at[:, pl.ds(pl.multiple_of(k * rs + (1 - c) * hs, 8), hs), :],
                           dst.at[:, pl.ds(k * hs, hs), :]) for k in range(N_CHIPS)]
            for s_ref, d_ref in pieces:
                cp = pltpu.make_async_remote_copy(
                    src_ref=s_ref, dst_ref=d_ref, send_sem=send_sems.at[idx], recv_sem=recv_sems.at[idx],
                    device_id=(x, y, 1 - c), device_id_type=MESH)
                cp.start()
                copies.append(cp)
                idx += 1
        for cp in copies:
            cp.wait()

    return pl.pallas_call(
        body, name="grad_pair_exchange",
        in_specs=[ANY] * nw, out_specs=[ANY] * nw,
        out_shape=[jax.ShapeDtypeStruct((g.shape[0], g.shape[1] // 2, g.shape[2]), F32) for g in grads],
        scratch_shapes=[pltpu.SemaphoreType.DMA((n_copies,)), pltpu.SemaphoreType.DMA((n_copies,))],
    )(*grads)


def _pair_sum(grad, other, col, core, name):
    l, fr, fc = grad.shape
    if col:
        pieces, pr = l, fr // 2
        gview = grad.reshape(l * 2, pr, fc)
        gidx = lambda p, i, cref: (p * 2 + cref[0], i, 0)
    else:
        pr = fr // N_CHIPS // 2
        pieces = l * N_CHIPS
        gview = grad.reshape(pieces * 2, pr, fc)
        gidx = lambda p, i, cref: (p * 2 + cref[0], i, 0)
    oview = other.reshape(pieces, pr, fc)
    tr = _tile(pr, 256, 16)

    def body(c_ref, g_ref, o_ref, out_ref):
        out_ref[...] = (g_ref[...] + o_ref[...]).astype(out_ref.dtype)

    out = pl.pallas_call(
        body, name=name,
        grid_spec=pltpu.PrefetchScalarGridSpec(
            num_scalar_prefetch=1, grid=(pieces, pr // tr),
            in_specs=[pl.BlockSpec((1, tr, fc), gidx), pl.BlockSpec((1, tr, fc), lambda p, i, cref: (p, i, 0))],
            out_specs=pl.BlockSpec((1, tr, fc), lambda p, i, cref: (p, i, 0))),
        out_shape=jax.ShapeDtypeStruct((pieces, pr, fc), BF16),
        compiler_params=_params("parallel", "parallel"),
    )(core, gview, oview)
    return out.reshape(l, fr // 2, fc)


def _chip_exchange(psums):
    nw = len(psums)
    out_shapes = []
    for (name, col), p in zip(BIG, psums):
        l, hr, fc = p.shape
        out_shapes.append((N_CHIPS, l, hr, fc // N_CHIPS) if col else (N_CHIPS, l, hr // N_CHIPS, fc))

    def body(*refs):
        ins, outs = refs[:nw], refs[nw:2 * nw]
        send_sems, recv_sems, local_sems = refs[2 * nw:]
        x, y, c = _my_place()
        me = 2 * x + y
        chips = _other_chips(x, y)
        copies = []
        for wi in range(nw):
            col = BIG[wi][1]
            src, dst = ins[wi], outs[wi]

            def piece(chip):
                if col:
                    cs = src.shape[2] // N_CHIPS
                    return src.at[:, :, pl.ds(pl.multiple_of(chip * cs, LANES), cs)]
                ps = src.shape[1] // N_CHIPS
                return src.at[:, pl.ds(pl.multiple_of(chip * ps, 8), ps), :]

            cp = pltpu.make_async_copy(piece(me), dst.at[me], local_sems.at[wi])
            cp.start()
            copies.append(cp)
            for j, (px, py) in enumerate(chips):
                cp = pltpu.make_async_remote_copy(
                    src_ref=piece(2 * px + py), dst_ref=dst.at[me],
                    send_sem=send_sems.at[wi * 3 + j], recv_sem=recv_sems.at[wi * 3 + j],
                    device_id=(px, py, c), device_id_type=MESH)
                cp.start()
                copies.append(cp)
        for cp in copies:
            cp.wait()

    return pl.pallas_call(
        body, name="grad_chip_exchange",
        in_specs=[ANY] * nw, out_specs=[ANY] * nw,
        out_shape=[jax.ShapeDtypeStruct(s, BF16) for s in out_shapes],
        scratch_shapes=[pltpu.SemaphoreType.DMA((3 * nw,)), pltpu.SemaphoreType.DMA((3 * nw,)),
                        pltpu.SemaphoreType.DMA((nw,))],
    )(*psums)


def _chip_sum(slots, core, name):
    _, l, hr, cs = slots.shape
    tr = _tile(hr, 256, 16)

    def body(c_ref, s_ref, out_ref):
        acc = s_ref[0, 0].astype(F32)
        for i in range(1, N_CHIPS):
            acc = acc + s_ref[i, 0].astype(F32)
        out_ref[0] = acc

    out = pl.pallas_call(
        body, name=name,
        grid_spec=pltpu.PrefetchScalarGridSpec(
            num_scalar_prefetch=1, grid=(l, hr // tr),
            in_specs=[pl.BlockSpec((N_CHIPS, 1, tr, cs), lambda p, i, cref: (0, p, i, 0))],
            out_specs=pl.BlockSpec((1, tr, cs), lambda p, i, cref: (p * 2 + cref[0], i, 0))),
        out_shape=jax.ShapeDtypeStruct((l * 2, hr, cs), F32),
        compiler_params=_params("parallel", "parallel"),
    )(core, slots)
    return out.reshape(l, 2 * hr, cs)


def _pair_share(reduced):
    nw = len(reduced)

    def body(*refs):
        ins, outs = refs[:nw], refs[nw:2 * nw]
        send_sems, recv_sems = refs[2 * nw:]
        x, y, c = _my_place()
        copies = []
        for wi in range(nw):
            hs = outs[wi].shape[1] // 2
            mine = outs[wi].at[:, pl.ds(pl.multiple_of(c * hs, 8), hs), :]
            cp = pltpu.make_async_remote_copy(
                src_ref=mine, dst_ref=mine, send_sem=send_sems.at[wi], recv_sem=recv_sems.at[wi],
                device_id=(x, y, 1 - c), device_id_type=MESH)
            cp.start()
            copies.append(cp)
        for cp in copies:
            cp.wait()

    return pl.pallas_call(
        body, name="grad_pair_share",
        in_specs=[ANY] * nw, out_specs=[ANY] * nw,
        out_shape=[jax.ShapeDtypeStruct(r.shape, F32) for r in reduced],
        input_output_aliases={i: i for i in range(nw)},
        scratch_shapes=[pltpu.SemaphoreType.DMA((nw,)), pltpu.SemaphoreType.DMA((nw,))],
    )(*reduced)


def _all_reduce_small(flat):
    rows, lanes = flat.shape
    seg = rows // N_DEV

    def body(in_ref, out_ref, recv_ref, send_sems, recv_sems):
        x, y, c = _my_place()
        me = 4 * x + 2 * y + c

        def peer(r):
            fx, fy, fc = (r >> 2) & 1, (r >> 1) & 1, r & 1
            px = jnp.where(fx == 1, 1 - x, x)
            py = jnp.where(fy == 1, 1 - y, y)
            pc = jnp.where(fc == 1, 1 - c, c)
            return px, py, pc

        first = []
        for r in range(1, N_DEV):
            px, py, pc = peer(r)
            theirs = in_ref.at[pl.ds(pl.multiple_of((4 * px + 2 * py + pc) * seg, 8), seg), :]
            cp = pltpu.make_async_remote_copy(
                src_ref=theirs, dst_ref=recv_ref.at[r], send_sem=send_sems.at[r - 1], recv_sem=recv_sems.at[r - 1],
                device_id=(px, py, pc), device_id_type=MESH)
            cp.start()
            first.append(cp)
        for cp in first:
            cp.wait()
        my_rows = pl.ds(pl.multiple_of(me * seg, 8), seg)
        acc = in_ref[my_rows, :]
        for r in range(1, N_DEV):
            acc = acc + recv_ref[r]
        out_ref[my_rows, :] = acc
        second = []
        for r in range(1, N_DEV):
            px, py, pc = peer(r)
            cp = pltpu.make_async_remote_copy(
                src_ref=out_ref.at[my_rows, :], dst_ref=out_ref.at[my_rows, :],
                send_sem=send_sems.at[6 + r], recv_sem=recv_sems.at[6 + r],
                device_id=(px, py, pc), device_id_type=MESH)
            cp.start()
            second.append(cp)
        for r in range(1, N_DEV):
            px, py, pc = peer(r)
            theirs = out_ref.at[pl.ds(pl.multiple_of((4 * px + 2 * py + pc) * seg, 8), seg), :]
            pltpu.make_async_remote_copy(
                src_ref=theirs, dst_ref=theirs, send_sem=send_sems.at[6 + r], recv_sem=recv_sems.at[6 + r],
                device_id=(px, py, pc), device_id_type=MESH).wait_recv()
        for cp in second:
            cp.wait_send()

    vm = pl.BlockSpec(memory_space=pltpu.VMEM)
    return pl.pallas_call(
        body, name="all_reduce_small",
        in_specs=[vm], out_specs=vm,
        out_shape=jax.ShapeDtypeStruct((rows, lanes), F32),
        scratch_shapes=[pltpu.VMEM((N_DEV, seg, lanes), F32),
                        pltpu.SemaphoreType.DMA((2 * (N_DEV - 1),)), pltpu.SemaphoreType.DMA((2 * (N_DEV - 1),))],
        compiler_params=pltpu.CompilerParams(vmem_limit_bytes=VMEM_LIMIT),
    )(flat)


def _adamw(w, g, m, v, name):
    rows, cols = w.shape
    tr = _tile(rows, 256, 8)
    c1 = 1.0 - ADAM_B1 ** ADAM_STEP
    c2 = 1.0 - ADAM_B2 ** ADAM_STEP

    def body(w_ref, g_ref, m_ref, v_ref, d_ref, nm_ref, nv_ref):
        gv = g_ref[...]
        nm = ADAM_B1 * m_ref[...] + (1.0 - ADAM_B1) * gv
        nv = ADAM_B2 * v_ref[...] + (1.0 - ADAM_B2) * (gv * gv)
        d_ref[...] = -ADAM_LR * ((nm / c1) / (jnp.sqrt(nv / c2) + ADAM_EPS) + ADAM_WD * w_ref[...])
        nm_ref[...] = nm
        nv_ref[...] = nv

    blk = pl.BlockSpec((tr, cols), lambda i: (i, 0))
    sds = jax.ShapeDtypeStruct((rows, cols), F32)
    return pl.pallas_call(
        body, name=name, grid=(rows // tr,),
        in_specs=[blk] * 4, out_specs=[blk] * 3, out_shape=[sds] * 3,
        compiler_params=_params("parallel"),
    )(w, g, m, v)


SMALL = ("norm_ffn1", "norm_mix", "ssm_a_re", "ssm_a_im", "ssm_log_dt", "ssm_b_re", "ssm_b_im", "ssm_c_re",
         "ssm_c_im", "ssm_d", "ssm_glu_w", "ssm_glu_b", "gm_v_gain", "gm_w_s", "gm_b_s", "gain_ssm_out",
         "gain_gm_out", "norm_ffn2", "norm_final")
WEIGHTS = ("norm_ffn1", "ffn1_w_in", "ffn1_w_out", "norm_mix", "mix_w_in", "ssm_a_re", "ssm_a_im", "ssm_log_dt",
           "ssm_b_re", "ssm_b_im", "ssm_c_re", "ssm_c_im", "ssm_d", "ssm_glu_w", "ssm_glu_b", "gm_v_gain", "gm_w_s",
           "gm_b_s", "gain_ssm_out", "gain_gm_out", "mix_w_out", "norm_ffn2", "ffn2_w_in", "ffn2_w_out", "norm_final")


def _ffn_fwd(x, gain, w_in, w_out, tag):
    h = _rmsnorm_fwd(x, gain, f"{tag}_norm")
    gu = _matmul(h, w_in, "nn", out_dtype=BF16, tm=1024, tn=1536, tk=1024, name=f"{tag}_in")
    a = _swiglu_fwd(gu, f"{tag}_act")
    out = _matmul(a, w_out, "nn", scale=0.5, res=x, tm=512, tn=1024, tk=1536, name=f"{tag}_out")
    return out, (x, h, gu, a)


def _ffn_bwd(dout, saved, gain, w_in, w_out, tag):
    x, h, gu, a = saved
    da = _matmul(dout, w_out, "nt", out_dtype=BF16, scale=0.5, tm=512, tn=1536, tk=1024, name=f"{tag}_out_dx")
    dw_out = _matmul(a, dout, "tn", scale=0.5, tm=1536, tn=1024, tk=512, name=f"{tag}_out_dw")
    dgu = _swiglu_bwd(gu, da, f"{tag}_act_bwd")
    dh = _matmul(dgu, w_in, "nt", tm=512, tn=1024, tk=1536, name=f"{tag}_in_dx")
    dw_in = _matmul(h, dgu, "tn", tm=1024, tn=1536, tk=512, name=f"{tag}_in_dw")
    dx, dgain = _rmsnorm_bwd(x, gain, dh, dout, f"{tag}_norm_bwd")
    return dx, dgain, dw_in, dw_out


def kernel(x, norm_ffn1, ffn1_w_in, ffn1_w_out, norm_mix, mix_w_in, ssm_a_re, ssm_a_im, ssm_log_dt, ssm_b_re, ssm_b_im, ssm_c_re, ssm_c_im, ssm_d, ssm_glu_w, ssm_glu_b, gm_v_gain, gm_w_s, gm_b_s, gain_ssm_out, gain_gm_out, mix_w_out, norm_ffn2, ffn2_w_in, ffn2_w_out, norm_final, loss_target, m_norm_ffn1, m_ffn1_w_in, m_ffn1_w_out, m_norm_mix, m_mix_w_in, m_ssm_a_re, m_ssm_a_im, m_ssm_log_dt, m_ssm_b_re, m_ssm_b_im, m_ssm_c_re, m_ssm_c_im, m_ssm_d, m_ssm_glu_w, m_ssm_glu_b, m_gm_v_gain, m_gm_w_s, m_gm_b_s, m_gain_ssm_out, m_gain_gm_out, m_mix_w_out, m_norm_ffn2, m_ffn2_w_in, m_ffn2_w_out, m_norm_final, v_norm_ffn1, v_ffn1_w_in, v_ffn1_w_out, v_norm_mix, v_mix_w_in, v_ssm_a_re, v_ssm_a_im, v_ssm_log_dt, v_ssm_b_re, v_ssm_b_im, v_ssm_c_re, v_ssm_c_im, v_ssm_d, v_ssm_glu_w, v_ssm_glu_b, v_gm_v_gain, v_gm_w_s, v_gm_b_s, v_gain_ssm_out, v_gain_gm_out, v_mix_w_out, v_norm_ffn2, v_ffn2_w_in, v_ffn2_w_out, v_norm_final):
    wts = dict(norm_ffn1=norm_ffn1, ffn1_w_in=ffn1_w_in, ffn1_w_out=ffn1_w_out, norm_mix=norm_mix, mix_w_in=mix_w_in,
               ssm_a_re=ssm_a_re, ssm_a_im=ssm_a_im, ssm_log_dt=ssm_log_dt, ssm_b_re=ssm_b_re, ssm_b_im=ssm_b_im,
               ssm_c_re=ssm_c_re, ssm_c_im=ssm_c_im, ssm_d=ssm_d, ssm_glu_w=ssm_glu_w, ssm_glu_b=ssm_glu_b,
               gm_v_gain=gm_v_gain, gm_w_s=gm_w_s, gm_b_s=gm_b_s, gain_ssm_out=gain_ssm_out, gain_gm_out=gain_gm_out,
               mix_w_out=mix_w_out, norm_ffn2=norm_ffn2, ffn2_w_in=ffn2_w_in, ffn2_w_out=ffn2_w_out,
               norm_final=norm_final)
    mom = dict(norm_ffn1=m_norm_ffn1, ffn1_w_in=m_ffn1_w_in, ffn1_w_out=m_ffn1_w_out, norm_mix=m_norm_mix,
               mix_w_in=m_mix_w_in, ssm_a_re=m_ssm_a_re, ssm_a_im=m_ssm_a_im, ssm_log_dt=m_ssm_log_dt,
               ssm_b_re=m_ssm_b_re, ssm_b_im=m_ssm_b_im, ssm_c_re=m_ssm_c_re, ssm_c_im=m_ssm_c_im, ssm_d=m_ssm_d,
               ssm_glu_w=m_ssm_glu_w, ssm_glu_b=m_ssm_glu_b, gm_v_gain=m_gm_v_gain, gm_w_s=m_gm_w_s, gm_b_s=m_gm_b_s,
               gain_ssm_out=m_gain_ssm_out, gain_gm_out=m_gain_gm_out, mix_w_out=m_mix_w_out, norm_ffn2=m_norm_ffn2,
               ffn2_w_in=m_ffn2_w_in, ffn2_w_out=m_ffn2_w_out, norm_final=m_norm_final)
    var = dict(norm_ffn1=v_norm_ffn1, ffn1_w_in=v_ffn1_w_in, ffn1_w_out=v_ffn1_w_out, norm_mix=v_norm_mix,
               mix_w_in=v_mix_w_in, ssm_a_re=v_ssm_a_re, ssm_a_im=v_ssm_a_im, ssm_log_dt=v_ssm_log_dt,
               ssm_b_re=v_ssm_b_re, ssm_b_im=v_ssm_b_im, ssm_c_re=v_ssm_c_re, ssm_c_im=v_ssm_c_im, ssm_d=v_ssm_d,
               ssm_glu_w=v_ssm_glu_w, ssm_glu_b=v_ssm_glu_b, gm_v_gain=v_gm_v_gain, gm_w_s=v_gm_w_s, gm_b_s=v_gm_b_s,
               gain_ssm_out=v_gain_ssm_out, gain_gm_out=v_gain_gm_out, mix_w_out=v_mix_w_out, norm_ffn2=v_norm_ffn2,
               ffn2_w_in=v_ffn2_w_in, ffn2_w_out=v_ffn2_w_out, norm_final=v_norm_final)

    nseq, seq, d = x.shape
    n = nseq * seq
    depth = norm_ffn1.shape[0]
    width = gain_ssm_out.shape[1]
    groups = ssm_a_re.shape[1]
    heads = gm_w_s.shape[1]
    core = lax.axis_index("c").astype(jnp.int32).reshape(1)

    full = _all_gather_weights([wts[name].astype(BF16) for name, _ in BIG])
    full = dict(zip([name for name, _ in BIG], full))

    xs = x.reshape(n, d)
    saved = []
    for l in range(depth):
        x1, s_ffn1 = _ffn_fwd(xs, norm_ffn1[l], full["ffn1_w_in"][l], full["ffn1_w_out"][l], "ffn1")
        hm = _rmsnorm_fwd(x1, norm_mix[l], "mix_norm")
        w_mi = full["mix_w_in"][l]
        u_ssm = _matmul(hm, w_mi[:, :width], "nn", tm=1024, tn=512, tk=1024, name="mix_in_ssm")
        zgm = _matmul(hm, w_mi[:, width:], "nn", tm=1024, tn=1024, tk=1024, name="mix_in_gm")
        bt_re = jnp.swapaxes(ssm_b_re[l], 1, 2)
        bt_im = jnp.swapaxes(ssm_b_im[l], 1, 2)
        disc_in = (ssm_a_re[l], ssm_a_im[l], ssm_log_dt[l].reshape(groups, 1), bt_re, bt_im)
        lr, li, bbr, bbi = _disc_fwd(*disc_in)
        ops = _ssm_operands(lr, li, bbr, bbi, ssm_c_re[l], ssm_c_im[l], ssm_d[l], ssm_glu_w[l], ssm_glu_b[l])
        u8 = _to_scan_rows(u_ssm, nseq, seq)
        y8, hb = _ssm_fwd(u8, ops, "s5_fwd")
        y_ssm = _from_scan_rows(y8, nseq, seq)
        bias_tile = jnp.broadcast_to(gm_b_s[l].T[:, :, None], (GM_CHUNK, heads, GM_HEAD_DIM)).reshape(GM_CHUNK, width)
        y_gm = _gmlp_fwd(zgm, gm_v_gain[l], gm_w_s[l], bias_tile, "gmlp_fwd")
        ycat = _mixnorm_fwd(y_ssm, y_gm, gain_ssm_out[l], gain_gm_out[l], "mix_out_norm")
        x2 = _matmul(ycat, full["mix_w_out"][l], "nn", res=x1, tm=512, tn=1024, tk=1024, name="mix_out")
        x3, s_ffn2 = _ffn_fwd(x2, norm_ffn2[l], full["ffn2_w_in"][l], full["ffn2_w_out"][l], "ffn2")
        saved.append(dict(ffn1=s_ffn1, x1=x1, hm=hm, zgm=zgm, disc_in=disc_in, ops=ops, u8=u8, hb=hb, y_ssm=y_ssm,
                          bias_tile=bias_tile, y_gm=y_gm, ycat=ycat, ffn2=s_ffn2))
        xs = x3

    dx, g_norm_final, loss_part = _loss_head(xs, norm_final, loss_target.reshape(n, d))
    big = {name: [None] * depth for name, _ in BIG}
    small = {name: [None] * depth for name in SMALL if name != "norm_final"}
    gpb = GROUPS_PER_BLOCK
    s_blk = STATES_PER_BLOCK
    for l in reversed(range(depth)):
        sv = saved[l]
        dx, small["norm_ffn2"][l], big["ffn2_w_in"][l], big["ffn2_w_out"][l] = _ffn_bwd(
            dx, sv["ffn2"], norm_ffn2[l], full["ffn2_w_in"][l], full["ffn2_w_out"][l], "ffn2")
        w_mi = full["mix_w_in"][l]
        dycat = _matmul(dx, full["mix_w_out"][l], "nt", tm=512, tn=1024, tk=1024, name="mix_out_dx")
        big["mix_w_out"][l] = _matmul(sv["ycat"], dx, "tn", tm=1024, tn=1024, tk=512, name="mix_out_dw")
        dy_ssm, dy_gm, small["gain_ssm_out"][l], small["gain_gm_out"][l] = _mixnorm_bwd(
            sv["y_ssm"], sv["y_gm"], gain_ssm_out[l], gain_gm_out[l], dycat, "mix_out_norm_bwd")
        dzgm, small["gm_w_s"][l], dbias_tile, small["gm_v_gain"][l] = _gmlp_bwd(
            sv["zgm"], dy_gm, gm_v_gain[l], gm_w_s[l], sv["bias_tile"], "gmlp_bwd")
        small["gm_b_s"][l] = dbias_tile.reshape(GM_CHUNK, heads, GM_HEAD_DIM).sum(-1).T
        dy8 = _to_scan_rows(dy_ssm, nseq, seq)
        du8, dlam, db_bd, dct_bd, dd, dw_bd, dbias = _ssm_bwd(sv["u8"], dy8, sv["hb"], sv["ops"], "s5_bwd")
        du_ssm = _from_scan_rows(du8, nseq, seq)
        dlr = dlam[:, 0, :s_blk].reshape(groups, SSM_STATE)
        dli = dlam[:, 0, s_blk:].reshape(groups, SSM_STATE)
        dbbr = _block_diag_extract(db_bd[:, :, :s_blk], gpb)
        dbbi = _block_diag_extract(db_bd[:, :, s_blk:], gpb)
        da_re, da_im, dldt, dbt_re, dbt_im = _disc_bwd(*sv["disc_in"], dlr, dli, dbbr, dbbi)
        small["ssm_a_re"][l], small["ssm_a_im"][l], small["ssm_log_dt"][l] = da_re, da_im, dldt.reshape(groups)
        small["ssm_b_re"][l] = jnp.swapaxes(dbt_re, 1, 2)
        small["ssm_b_im"][l] = jnp.swapaxes(dbt_im, 1, 2)
        small["ssm_c_re"][l] = _block_diag_extract(dct_bd[:, :, :s_blk], gpb)
        small["ssm_c_im"][l] = -_block_diag_extract(dct_bd[:, :, s_blk:], gpb)
        small["ssm_d"][l] = dd.reshape(groups, SSM_CH)
        small["ssm_glu_w"][l] = jnp.concatenate(
            [_block_diag_extract(dw_bd[:, :, :LANES], gpb), _block_diag_extract(dw_bd[:, :, LANES:], gpb)], axis=-1)
        small["ssm_glu_b"][l] = jnp.concatenate(
            [dbias[:, 0, :LANES].reshape(groups, SSM_CH), dbias[:, 0, LANES:].reshape(groups, SSM_CH)], axis=-1)
        dhm = _matmul(du_ssm, w_mi[:, :width], "nt", tm=512, tn=1024, tk=512, name="mix_in_ssm_dx")
        dhm = _matmul(dzgm, w_mi[:, width:], "nt", res=dhm, tm=512, tn=1024, tk=1024, name="mix_in_gm_dx")
        big["mix_w_in"][l] = jnp.concatenate(
            [_matmul(sv["hm"], du_ssm, "tn", tm=1024, tn=512, tk=512, name="mix_in_ssm_dw"),
             _matmul(sv["hm"], dzgm, "tn", tm=1024, tn=1024, tk=512, name="mix_in_gm_dw")], axis=1)
        dx, small["norm_mix"][l] = _rmsnorm_bwd(sv["x1"], norm_mix[l], dhm, dx, "mix_norm_bwd")
        dx, small["norm_ffn1"][l], big["ffn1_w_in"][l], big["ffn1_w_out"][l] = _ffn_bwd(
            dx, sv["ffn1"], norm_ffn1[l], full["ffn1_w_in"][l], full["ffn1_w_out"][l], "ffn1")
    grad_x = dx.reshape(nseq, seq, d)

    partial = [jnp.stack(big[name]) for name, _ in BIG]
    other = _pair_exchange(partial)
    psums = [_pair_sum(g, o, col, core, f"grad_pair_sum_{name}") for (name, col), g, o in zip(BIG, partial, other)]
    slots = _chip_exchange(psums)
    reduced = [_chip_sum(s, core, f"grad_chip_sum_{name}") for (name, _), s in zip(BIG, slots)]
    reduced = _pair_share(reduced)
    grads = dict(zip([name for name, _ in BIG], reduced))

    pieces = [jnp.stack(small[name]).reshape(-1) for name in SMALL if name != "norm_final"]
    pieces += [g_norm_final.reshape(-1), loss_part.reshape(1)]
    sizes = [p.shape[0] for p in pieces]
    total = sum(sizes)
    rows = -(-total // (LANES * N_DEV * SUBLANES)) * N_DEV * SUBLANES
    flat = jnp.concatenate(pieces + [jnp.zeros((rows * LANES - total,), F32)]).reshape(rows, LANES)
    flat = _all_reduce_small(flat).reshape(-1)
    offs = 0
    for name, size in zip(SMALL, sizes[:-1]):
        grads[name] = flat[offs:offs + size].reshape(wts[name].shape)
        offs += size
    loss = flat[offs]

    delta, new_m, new_v = {}, {}, {}
    for name, _ in BIG:
        shape = wts[name].shape
        two_d = lambda a: a.reshape(shape[0] * shape[1], shape[2])
        dl, nm, nv = _adamw(two_d(wts[name]), two_d(grads[name]), two_d(mom[name]), two_d(var[name]), f"adamw_{name}")
        delta[name], new_m[name], new_v[name] = dl.reshape(shape), nm.reshape(shape), nv.reshape(shape)
    small_sizes = [math.prod(wts[name].shape) for name in SMALL]
    small_total = sum(small_sizes)
    srows = -(-small_total // (LANES * SUBLANES)) * SUBLANES

    def pack(tree, fill):
        parts = [tree[name].reshape(-1) for name in SMALL]
        parts.append(jnp.full((srows * LANES - small_total,), fill, F32))
        return jnp.concatenate(parts).reshape(srows, LANES)

    dl, nm, nv = _adamw(pack(wts, 0.0), pack(grads, 0.0), pack(mom, 0.0), pack(var, 1.0), "adamw_small")
    offs = 0
    for name, size in zip(SMALL, small_sizes):
        shape = wts[name].shape
        delta[name] = dl.reshape(-1)[offs:offs + size].reshape(shape)
        new_m[name] = nm.reshape(-1)[offs:offs + size].reshape(shape)
        new_v[name] = nv.reshape(-1)[offs:offs + size].reshape(shape)
        offs += size

    return (loss, grad_x, *[grads[k] for k in WEIGHTS], *[delta[k] for k in WEIGHTS],
            *[new_m[k] for k in WEIGHTS], *[new_v[k] for k in WEIGHTS])
```

```python
import functools
import math

import jax
import jax.numpy as jnp
from jax import lax
from jax.experimental import pallas as pl
from jax.experimental.pallas import tpu as pltpu

F32 = jnp.float32
BF16 = jnp.bfloat16
MESH = pl.DeviceIdType.MESH

EPS = 1e-6
SSM_CH = 16
SSM_STATE = 64
GM_CHUNK = 128
GM_HEAD_DIM = 128
SUBLANES = 8
LANES = 128
GROUPS_PER_BLOCK = LANES // SSM_CH
STATES_PER_BLOCK = GROUPS_PER_BLOCK * SSM_STATE
SSM_TIME_CHUNK = 128
N_CHIPS = 4
N_DEV = 8

ADAM_LR = 0.001
ADAM_B1 = 0.9
ADAM_B2 = 0.999
ADAM_EPS = 1e-08
ADAM_WD = 0.01
ADAM_STEP = 10

VMEM_LIMIT = 56 * 1024 * 1024


def _tile(dim, pref, align):
    best = None
    t = align
    while t <= min(dim, pref):
        if dim % t == 0:
            best = t
        t += align
    return best if best is not None else dim


def _params(*sem):
    return pltpu.CompilerParams(dimension_semantics=sem, vmem_limit_bytes=VMEM_LIMIT)


def _gelu(x):
    c = math.sqrt(2.0 / math.pi)
    return 0.5 * x * (1.0 + jnp.tanh(c * (x + 0.044715 * x * x * x)))


def _gelu_and_grad(x):
    c = math.sqrt(2.0 / math.pi)
    t = jnp.tanh(c * (x + 0.044715 * x * x * x))
    g = 0.5 * x * (1.0 + t)
    dg = 0.5 * (1.0 + t) + 0.5 * x * (1.0 - t * t) * c * (1.0 + 3.0 * 0.044715 * x * x)
    return g, dg


def _sigmoid(x):
    return 0.5 * jnp.tanh(0.5 * x) + 0.5


def _matmul(a, b, mode, *, out_dtype=F32, scale=1.0, res=None, tm=512, tn=1024, tk=1024, name="mm",
            out_cols=None, col_off=0, into=None):
    if mode == "nn":
        (m, k), (k2, n) = a.shape, b.shape
    elif mode == "nt":
        (m, k), (n, k2) = a.shape, b.shape
    else:
        (k, m), (k2, n) = a.shape, b.shape
    assert k == k2, (a.shape, b.shape, mode)
    tm = _tile(m, tm, 16 if mode != "tn" else LANES)
    tn = _tile(n, tn, LANES)
    tk = _tile(k, tk, LANES if mode != "tn" else 16)
    nk = k // tk
    grid = (m // tm, n // tn, nk)
    if mode == "nn":
        a_spec = pl.BlockSpec((tm, tk), lambda i, j, kk: (i, kk))
        b_spec = pl.BlockSpec((tk, tn), lambda i, j, kk: (kk, j))
        dims = (((1,), (0,)), ((), ()))
    elif mode == "nt":
        a_spec = pl.BlockSpec((tm, tk), lambda i, j, kk: (i, kk))
        b_spec = pl.BlockSpec((tn, tk), lambda i, j, kk: (j, kk))
        dims = (((1,), (1,)), ((), ()))
    else:
        a_spec = pl.BlockSpec((tk, tm), lambda i, j, kk: (kk, i))
        b_spec = pl.BlockSpec((tk, tn), lambda i, j, kk: (kk, j))
        dims = (((0,), (0,)), ((), ()))
    assert col_off % tn == 0
    off = col_off // tn
    r_spec = pl.BlockSpec((tm, tn), lambda i, j, kk: (i, j))
    o_spec = pl.BlockSpec((tm, tn), lambda i, j, kk: (i, j + off))
    has_res = res is not None
    has_into = into is not None

    def body(*refs):
        refs = list(refs)
        a_ref, b_ref = refs[:2]
        pos = 2
        r_ref = None
        if has_res:
            r_ref = refs[pos]
            pos += 1
        if has_into:
            pos += 1
        o_ref = refs[pos]
        acc_ref = refs[pos + 1] if nk > 1 else None
        part = lax.dot_general(a_ref[...].astype(BF16), b_ref[...].astype(BF16), dims,
                               preferred_element_type=F32)

        def finish(r):
            if scale != 1.0:
                r = r * scale
            if has_res:
                r = r + r_ref[...].astype(F32)
            o_ref[...] = r.astype(o_ref.dtype)

        if nk == 1:
            finish(part)
        else:
            kk = pl.program_id(2)

            @pl.when(kk == 0)
            def _():
                acc_ref[...] = part

            @pl.when(kk > 0)
            def _():
                acc_ref[...] += part

            @pl.when(kk == nk - 1)
            def _():
                finish(acc_ref[...])

    in_specs = [a_spec, b_spec]
    args = [a, b]
    if has_res:
        in_specs.append(r_spec)
        args.append(res)
    aliases = {}
    if has_into:
        in_specs.append(pl.BlockSpec(memory_space=pl.ANY))
        args.append(into)
        aliases = {len(args) - 1: 0}
    return pl.pallas_call(
        body, name=name, grid=grid, in_specs=in_specs, out_specs=o_spec,
        out_shape=jax.ShapeDtypeStruct((m, n if out_cols is None else out_cols), out_dtype),
        scratch_shapes=[pltpu.VMEM((tm, tn), F32)] if nk > 1 else [],
        input_output_aliases=aliases,
        compiler_params=_params("parallel", "parallel", "arbitrary"),
    )(*args)


def _loss_head(x, gain, target):
    n, d = x.shape
    tm = _tile(n, 512, 8)
    steps = n // tm

    def body(x_ref, g_ref, t_ref, dx_ref, dg_ref, loss_ref, acc_ref, lacc_ref):
        i = pl.program_id(0)
        xv = x_ref[...]
        g = g_ref[...]
        r = lax.rsqrt(jnp.mean(xv * xv, axis=-1, keepdims=True) + EPS)
        xh = xv * r
        err = xh * g - t_ref[...]
        dy = err * (1.0 / d)
        dyg = dy * g
        mean = jnp.mean(dyg * xh, axis=-1, keepdims=True)
        dx_ref[...] = r * (dyg - xh * mean)
        part = jnp.sum((dy * xh).reshape(tm // SUBLANES, SUBLANES, d), axis=0)
        lpart = jnp.sum((err * err).reshape(tm // SUBLANES, SUBLANES, d), axis=0)

        @pl.when(i == 0)
        def _():
            acc_ref[...] = part
            lacc_ref[...] = lpart

        @pl.when(i > 0)
        def _():
            acc_ref[...] += part
            lacc_ref[...] += lpart

        @pl.when(i == steps - 1)
        def _():
            dg_ref[...] = jnp.sum(acc_ref[...], axis=0, keepdims=True)
            tot = jnp.sum(jnp.sum(lacc_ref[...], axis=0, keepdims=True), axis=1, keepdims=True)
            loss_ref[...] = jnp.broadcast_to(tot * (0.5 / d), loss_ref.shape)

    row = pl.BlockSpec((tm, d), lambda i: (i, 0))
    vec = pl.BlockSpec((1, d), lambda i: (0, 0))
    dx, dg, loss = pl.pallas_call(
        body, name="loss_head", grid=(steps,),
        in_specs=[row, vec, row],
        out_specs=[row, vec, pl.BlockSpec((1, LANES), lambda i: (0, 0))],
        out_shape=[jax.ShapeDtypeStruct((n, d), F32), jax.ShapeDtypeStruct((1, d), F32),
                   jax.ShapeDtypeStruct((1, LANES), F32)],
        scratch_shapes=[pltpu.VMEM((SUBLANES, d), F32), pltpu.VMEM((SUBLANES, d), F32)],
        compiler_params=_params("arbitrary"),
    )(x, gain.reshape(1, d), target)
    return dx, dg.reshape(d), loss[0, 0]


def _rms_rows(xv):
    return lax.rsqrt(jnp.mean(xv * xv, axis=-1, keepdims=True) + EPS)


def _ffn_in_fwd(x, gain, w_in, name):
    n, d = x.shape
    f = w_in.shape[1] // 2
    tm = _tile(n, 512, 16)
    tn = _tile(f, 1536, LANES)
    nj = f // tn

    def body(x_ref, gain_ref, wg_ref, wu_ref, h_ref, g_ref, u_ref, a_ref):
        @pl.when(pl.program_id(1) == 0)
        def _():
            xv = x_ref[...]
            h_ref[...] = (xv * _rms_rows(xv) * gain_ref[...]).astype(h_ref.dtype)

        h = h_ref[...]
        g = jnp.dot(h, wg_ref[...], preferred_element_type=F32)
        u = jnp.dot(h, wu_ref[...], preferred_element_type=F32)
        g_ref[...] = g.astype(g_ref.dtype)
        u_ref[...] = u.astype(u_ref.dtype)
        a_ref[...] = (g * _sigmoid(g) * u).astype(a_ref.dtype)

    row = pl.BlockSpec((tm, d), lambda i, j: (i, 0))
    tile = pl.BlockSpec((tm, tn), lambda i, j: (i, j))
    act = jax.ShapeDtypeStruct((n, f), BF16)
    return pl.pallas_call(
        body, name=name, grid=(n // tm, nj),
        in_specs=[row, pl.BlockSpec((1, d), lambda i, j: (0, 0)),
                  pl.BlockSpec((d, tn), lambda i, j: (0, j)), pl.BlockSpec((d, tn), lambda i, j: (0, j + nj))],
        out_specs=[row, tile, tile, tile],
        out_shape=[jax.ShapeDtypeStruct((n, d), BF16), act, act, act],
        compiler_params=_params("parallel", "arbitrary"),
    )(x, gain.reshape(1, d), w_in, w_in)


def _ffn_out_bwd(dout, w_out, g, u, name):
    n, d = dout.shape
    f = w_out.shape[0]
    tm = _tile(n, 512, 16)
    tn = _tile(f, 1536, LANES)

    def body(d_ref, w_ref, g_ref, u_ref, dg_ref, du_ref):
        da = 0.5 * lax.dot_general(d_ref[...].astype(BF16), w_ref[...], (((1,), (1,)), ((), ())),
                                   preferred_element_type=F32)
        gv = g_ref[...].astype(F32)
        s = _sigmoid(gv)
        dg_ref[...] = (da * u_ref[...].astype(F32) * (s * (1.0 + gv * (1.0 - s)))).astype(dg_ref.dtype)
        du_ref[...] = (da * gv * s).astype(du_ref.dtype)

    tile = pl.BlockSpec((tm, tn), lambda i, j: (i, j))
    act = jax.ShapeDtypeStruct((n, f), BF16)
    return pl.pallas_call(
        body, name=name, grid=(n // tm, f // tn),
        in_specs=[pl.BlockSpec((tm, d), lambda i, j: (i, 0)), pl.BlockSpec((tn, d), lambda i, j: (j, 0)), tile, tile],
        out_specs=[tile, tile], out_shape=[act, act],
        compiler_params=_params("parallel", "parallel"),
    )(dout, w_out, g, u)


def _proj_in_bwd(parts, w, x, gain, dres, name):
    n, d = x.shape
    tm = _tile(n, 256, 8)
    steps = n // tm
    np_ = len(parts)
    offs = [off for _, off in parts]
    widths = [a.shape[1] for a, _ in parts]

    def body(*refs):
        a_refs = refs[:np_]
        w_ref, x_ref, g_ref, dr_ref, dx_ref, dg_ref, acc_ref = refs[np_:]
        i = pl.program_id(0)
        dh = None
        for a_ref, off, kp in zip(a_refs, offs, widths):
            part = lax.dot_general(a_ref[...].astype(BF16), w_ref[:, off:off + kp], (((1,), (1,)), ((), ())),
                                   preferred_element_type=F32)
            dh = part if dh is None else dh + part
        xv = x_ref[...]
        r = _rms_rows(xv)
        xh = xv * r
        dyg = dh * g_ref[...]
        mean = jnp.mean(dyg * xh, axis=-1, keepdims=True)
        dx_ref[...] = dr_ref[...] + r * (dyg - xh * mean)
        part = jnp.sum((dh * xh).reshape(tm // SUBLANES, SUBLANES, d), axis=0)

        @pl.when(i == 0)
        def _():
            acc_ref[...] = part

        @pl.when(i > 0)
        def _():
            acc_ref[...] += part

        @pl.when(i == steps - 1)
        def _():
            dg_ref[...] = jnp.sum(acc_ref[...], axis=0, keepdims=True)

    row = pl.BlockSpec((tm, d), lambda i: (i, 0))
    vec = pl.BlockSpec((1, d), lambda i: (0, 0))
    dx, dg = pl.pallas_call(
        body, name=name, grid=(steps,),
        in_specs=[pl.BlockSpec((tm, kp), lambda i: (i, 0)) for kp in widths]
        + [pl.BlockSpec(w.shape, lambda i: (0, 0)), row, vec, row],
        out_specs=[row, vec],
        out_shape=[jax.ShapeDtypeStruct((n, d), F32), jax.ShapeDtypeStruct((1, d), F32)],
        scratch_shapes=[pltpu.VMEM((SUBLANES, d), F32)],
        compiler_params=_params("arbitrary"),
    )(*[a for a, _ in parts], w, x, gain.reshape(1, d), dres)
    return dx, dg.reshape(d)


def _mix_in_fwd(x, gain, w, width, name):
    n, d = x.shape
    cols = w.shape[1]
    tm = _tile(n, 512, 16)

    def body(x_ref, gain_ref, w_ref, h_ref, u_ref, z_ref):
        xv = x_ref[...]
        h = (xv * _rms_rows(xv) * gain_ref[...]).astype(h_ref.dtype)
        h_ref[...] = h
        z = jnp.dot(h, w_ref[...], preferred_element_type=F32)
        u_ref[...] = z[:, 0:width]
        z_ref[...] = z[:, width:cols]

    row = pl.BlockSpec((tm, d), lambda i: (i, 0))
    return pl.pallas_call(
        body, name=name, grid=(n // tm,),
        in_specs=[row, pl.BlockSpec((1, d), lambda i: (0, 0)), pl.BlockSpec((d, cols), lambda i: (0, 0))],
        out_specs=[row, pl.BlockSpec((tm, width), lambda i: (i, 0)), pl.BlockSpec((tm, cols - width), lambda i: (i, 0))],
        out_shape=[jax.ShapeDtypeStruct((n, d), BF16), jax.ShapeDtypeStruct((n, width), F32),
                   jax.ShapeDtypeStruct((n, cols - width), F32)],
        compiler_params=_params("parallel"),
    )(x, gain.reshape(1, d), w)


def _tril_mask():
    t = lax.broadcasted_iota(jnp.int32, (GM_CHUNK, GM_CHUNK), 0)
    s = lax.broadcasted_iota(jnp.int32, (GM_CHUNK, GM_CHUNK), 1)
    return s <= t


def _gmlp_fwd(zgm, v_gain, w_s, bias_tile, name):
    n, w2 = zgm.shape
    w = w2 // 2
    heads = w // GM_HEAD_DIM
    tm = _tile(n, 512, GM_CHUNK)
    nq = tm // GM_CHUNK

    def body(u_ref, v_ref, gain_ref, w_ref, b_ref, o_ref):
        mask = _tril_mask()
        ug = _gelu(u_ref[...])
        vg = _gelu(v_ref[...])
        for h in range(heads):
            cols = slice(h * GM_HEAD_DIM, (h + 1) * GM_HEAD_DIM)
            vh = vg[:, cols]
            r = lax.rsqrt(jnp.mean(vh * vh, axis=-1, keepdims=True) + EPS)
            vn = (vh * r * gain_ref[:, cols]).astype(BF16)
            wm = jnp.where(mask, w_ref[h], 0.0).astype(BF16)
            for q in range(nq):
                rows = slice(q * GM_CHUNK, (q + 1) * GM_CHUNK)
                s = jnp.dot(wm, vn[rows], preferred_element_type=F32) + b_ref[:, cols]
                o_ref[rows, cols] = ug[rows, cols] * s

    return pl.pallas_call(
        body, name=name, grid=(n // tm,),
        in_specs=[pl.BlockSpec((tm, w), lambda i: (i, 0)), pl.BlockSpec((tm, w), lambda i: (i, 1)),
                  pl.BlockSpec((1, w), lambda i: (0, 0)),
                  pl.BlockSpec((heads, GM_CHUNK, GM_CHUNK), lambda i: (0, 0, 0)),
                  pl.BlockSpec((GM_CHUNK, w), lambda i: (0, 0))],
        out_specs=pl.BlockSpec((tm, w), lambda i: (i, 0)),
        out_shape=jax.ShapeDtypeStruct((n, w), F32),
        compiler_params=_params("parallel"),
    )(zgm, zgm, v_gain.reshape(1, w), w_s, bias_tile)


def _gmlp_bwd(zgm, dy, v_gain, w_s, bias_tile, name):
    n, w2 = zgm.shape
    w = w2 // 2
    heads = w // GM_HEAD_DIM
    tm = _tile(n, 512, GM_CHUNK)
    nq = tm // GM_CHUNK
    steps = n // tm

    def body(z_ref, dy_ref, gain_ref, w_ref, b_ref, dz_ref, dw_ref, db_ref, dgain_ref):
        i = pl.program_id(0)
        mask = _tril_mask()

        @pl.when(i == 0)
        def _():
            dw_ref[...] = jnp.zeros_like(dw_ref)
            db_ref[...] = jnp.zeros_like(db_ref)
            dgain_ref[...] = jnp.zeros_like(dgain_ref)

        ug, dug_du = _gelu_and_grad(z_ref[:, 0:w])
        vg, dvg_dv = _gelu_and_grad(z_ref[:, w:w2])
        dyv = dy_ref[...]
        for h in range(heads):
            cols = slice(h * GM_HEAD_DIM, (h + 1) * GM_HEAD_DIM)
            vh = vg[:, cols]
            r = lax.rsqrt(jnp.mean(vh * vh, axis=-1, keepdims=True) + EPS)
            vhat = vh * r
            gain = gain_ref[:, cols]
            vn = (vhat * gain).astype(BF16)
            wm = jnp.where(mask, w_ref[h], 0.0).astype(BF16)
            dvn_parts = []
            for q in range(nq):
                rows = slice(q * GM_CHUNK, (q + 1) * GM_CHUNK)
                s = jnp.dot(wm, vn[rows], preferred_element_type=F32) + b_ref[:, cols]
                dyq = dyv[rows, cols]
                dz_ref[rows, cols] = dyq * s * dug_du[rows, cols]
                ds = dyq * ug[rows, cols]
                db_ref[:, cols] += ds
                dsb = ds.astype(BF16)
                dw_ref[h] += lax.dot_general(dsb, vn[rows], (((1,), (1,)), ((), ())), preferred_element_type=F32)
                dvn_parts.append(lax.dot_general(wm, dsb, (((0,), (0,)), ((), ())), preferred_element_type=F32))
            dvn = jnp.concatenate(dvn_parts, axis=0) if nq > 1 else dvn_parts[0]
            dgain_ref[:, cols] += jnp.sum(dvn * vhat, axis=0, keepdims=True)
            dvhat = dvn * gain
            mean = jnp.mean(dvhat * vhat, axis=-1, keepdims=True)
            dz_ref[:, w + h * GM_HEAD_DIM:w + (h + 1) * GM_HEAD_DIM] = r * (dvhat - vhat * mean) * dvg_dv[:, cols]

        @pl.when(i == steps - 1)
        def _():
            for h in range(heads):
                dw_ref[h] = jnp.where(mask, dw_ref[h], 0.0)

    dz, dw, db, dgain = pl.pallas_call(
        body, name=name, grid=(steps,),
        in_specs=[pl.BlockSpec((tm, w2), lambda i: (i, 0)), pl.BlockSpec((tm, w), lambda i: (i, 0)),
                  pl.BlockSpec((1, w), lambda i: (0, 0)),
                  pl.BlockSpec((heads, GM_CHUNK, GM_CHUNK), lambda i: (0, 0, 0)),
                  pl.BlockSpec((GM_CHUNK, w), lambda i: (0, 0))],
        out_specs=[pl.BlockSpec((tm, w2), lambda i: (i, 0)),
                   pl.BlockSpec((heads, GM_CHUNK, GM_CHUNK), lambda i: (0, 0, 0)),
                   pl.BlockSpec((GM_CHUNK, w), lambda i: (0, 0)),
                   pl.BlockSpec((1, w), lambda i: (0, 0))],
        out_shape=[jax.ShapeDtypeStruct((n, w2), F32), jax.ShapeDtypeStruct((heads, GM_CHUNK, GM_CHUNK), F32),
                   jax.ShapeDtypeStruct((GM_CHUNK, w), F32), jax.ShapeDtypeStruct((1, w), F32)],
        compiler_params=_params("arbitrary"),
    )(zgm, dy, v_gain.reshape(1, w), w_s, bias_tile)
    return dz, dw, db, dgain.reshape(w)


def _mixnorm_fwd(y_ssm, y_gm, g1, g2, name):
    n, w = y_ssm.shape
    tm = _tile(n, 512, 16)

    def body(a_ref, b_ref, g1_ref, g2_ref, o_ref):
        for src, g_ref, lo in ((a_ref, g1_ref, 0), (b_ref, g2_ref, w)):
            v = src[...]
            r = lax.rsqrt(jnp.mean(v * v, axis=-1, keepdims=True) + EPS)
            o_ref[:, lo:lo + w] = (v * r * g_ref[...]).astype(o_ref.dtype)

    row = pl.BlockSpec((tm, w), lambda i: (i, 0))
    vec = pl.BlockSpec((1, w), lambda i: (0, 0))
    return pl.pallas_call(
        body, name=name, grid=(n // tm,),
        in_specs=[row, row, vec, vec], out_specs=pl.BlockSpec((tm, 2 * w), lambda i: (i, 0)),
        out_shape=jax.ShapeDtypeStruct((n, 2 * w), BF16),
        compiler_params=_params("parallel"),
    )(y_ssm, y_gm, g1.reshape(1, w), g2.reshape(1, w))


def _mixnorm_bwd(y_ssm, y_gm, g1, g2, dycat, name):
    n, w = y_ssm.shape
    tm = _tile(n, 512, 8)
    steps = n // tm

    def body(a_ref, b_ref, g1_ref, g2_ref, d_ref, da_ref, db_ref, dg1_ref, dg2_ref):
        i = pl.program_id(0)

        @pl.when(i == 0)
        def _():
            dg1_ref[...] = jnp.zeros_like(dg1_ref)
            dg2_ref[...] = jnp.zeros_like(dg2_ref)

        for src, g_ref, lo, dst, dg_ref in ((a_ref, g1_ref, 0, da_ref, dg1_ref), (b_ref, g2_ref, w, db_ref, dg2_ref)):
            v = src[...]
            dh = d_ref[:, lo:lo + w]
            r = lax.rsqrt(jnp.mean(v * v, axis=-1, keepdims=True) + EPS)
            vh = v * r
            dyg = dh * g_ref[...]
            mean = jnp.mean(dyg * vh, axis=-1, keepdims=True)
            dst[...] = r * (dyg - vh * mean)
            dg_ref[...] += jnp.sum(dh * vh, axis=0, keepdims=True)

    row = pl.BlockSpec((tm, w), lambda i: (i, 0))
    vec = pl.BlockSpec((1, w), lambda i: (0, 0))
    da, db, dg1, dg2 = pl.pallas_call(
        body, name=name, grid=(steps,),
        in_specs=[row, row, vec, vec, pl.BlockSpec((tm, 2 * w), lambda i: (i, 0))],
        out_specs=[row, row, vec, vec],
        out_shape=[jax.ShapeDtypeStruct((n, w), F32), jax.ShapeDtypeStruct((n, w), F32),
                   jax.ShapeDtypeStruct((1, w), F32), jax.ShapeDtypeStruct((1, w), F32)],
        compiler_params=_params("arbitrary"),
    )(y_ssm, y_gm, g1.reshape(1, w), g2.reshape(1, w), dycat)
    return da, db, dg1.reshape(w), dg2.reshape(w)


def _discretise(a_re, a_im, log_dt, bt_re, bt_im):
    dt = jnp.exp(log_dt)
    e = jnp.exp(a_re * dt)
    ang = a_im * dt
    lr = e * jnp.cos(ang)
    li = e * jnp.sin(ang)
    den = a_re * a_re + a_im * a_im
    cr = ((lr - 1.0) * a_re + li * a_im) / den
    ci = (li * a_re - (lr - 1.0) * a_im) / den
    cr3 = cr[:, None, :]
    ci3 = ci[:, None, :]
    return lr, li, cr3 * bt_re - ci3 * bt_im, cr3 * bt_im + ci3 * bt_re


def _disc_fwd(a_re, a_im, log_dt, bt_re, bt_im):
    g, p = a_re.shape
    c = bt_re.shape[1]

    def body(are_ref, aim_ref, ldt_ref, bre_ref, bim_ref, lr_ref, li_ref, bbr_ref, bbi_ref):
        lr, li, bbr, bbi = _discretise(are_ref[...], aim_ref[...], ldt_ref[...], bre_ref[...], bim_ref[...])
        lr_ref[...] = lr
        li_ref[...] = li
        bbr_ref[...] = bbr
        bbi_ref[...] = bbi

    return pl.pallas_call(
        body, name="s5_discretise",
        out_shape=[jax.ShapeDtypeStruct((g, p), F32), jax.ShapeDtypeStruct((g, p), F32),
                   jax.ShapeDtypeStruct((g, c, p), F32), jax.ShapeDtypeStruct((g, c, p), F32)],
    )(a_re, a_im, log_dt, bt_re, bt_im)


def _disc_bwd(a_re, a_im, log_dt, bt_re, bt_im, dlr, dli, dbbr, dbbi):
    g, p = a_re.shape
    c = bt_re.shape[1]

    def body(are_ref, aim_ref, ldt_ref, bre_ref, bim_ref, dlr_ref, dli_ref, dbbr_ref, dbbi_ref,
             dare_ref, daim_ref, dldt_ref, dbre_ref, dbim_ref):
        _, vjp = jax.vjp(_discretise, are_ref[...], aim_ref[...], ldt_ref[...], bre_ref[...], bim_ref[...])
        dare, daim, dldt, dbre, dbim = vjp((dlr_ref[...], dli_ref[...], dbbr_ref[...], dbbi_ref[...]))
        dare_ref[...] = dare
        daim_ref[...] = daim
        dldt_ref[...] = dldt
        dbre_ref[...] = dbre
        dbim_ref[...] = dbim

    return pl.pallas_call(
        body, name="s5_discretise_bwd",
        out_shape=[jax.ShapeDtypeStruct((g, p), F32), jax.ShapeDtypeStruct((g, p), F32),
                   jax.ShapeDtypeStruct((g, 1), F32),
                   jax.ShapeDtypeStruct((g, c, p), F32), jax.ShapeDtypeStruct((g, c, p), F32)],
    )(a_re, a_im, log_dt, bt_re, bt_im, dlr, dli, dbbr, dbbi)


def _block_diag(w, nb):
    g, a, b = w.shape
    gpb = g // nb
    eye = jnp.eye(gpb, dtype=w.dtype)
    w4 = w.reshape(nb, gpb, a, b)
    return jnp.einsum("ngab,gh->ngahb", w4, eye).reshape(nb, gpb * a, gpb * b)


def _block_diag_extract(m, gpb):
    nb, ga, gb = m.shape
    a, b = ga // gpb, gb // gpb
    m5 = m.reshape(nb, gpb, a, gpb, b)
    idx = jnp.arange(gpb)
    return m5[:, idx, :, idx, :].transpose(1, 0, 2, 3).reshape(nb * gpb, a, b)


def _ssm_operands(lr, li, bbr, bbi, c_re, c_im, d_skip, glu_w, glu_b):
    g = lr.shape[0]
    nb = g // GROUPS_PER_BLOCK
    s = STATES_PER_BLOCK
    lam = jnp.concatenate([lr.reshape(nb, 1, s), li.reshape(nb, 1, s)], axis=-1)
    b_bd = jnp.concatenate([_block_diag(bbr, nb), _block_diag(bbi, nb)], axis=-1)
    ct_re = jnp.swapaxes(c_re, 1, 2)
    ct_im = jnp.swapaxes(c_im, 1, 2)
    c_bd = jnp.concatenate([_block_diag(ct_re, nb), -_block_diag(ct_im, nb)], axis=1)
    dsk = d_skip.reshape(nb, 1, LANES)
    w_bd = jnp.concatenate([_block_diag(glu_w[:, :, :SSM_CH], nb), _block_diag(glu_w[:, :, SSM_CH:], nb)], axis=-1)
    bias = jnp.concatenate([glu_b[:, :SSM_CH].reshape(nb, 1, LANES), glu_b[:, SSM_CH:].reshape(nb, 1, LANES)], axis=-1)
    return lam, b_bd.astype(BF16), c_bd.astype(BF16), dsk, w_bd.astype(BF16), bias


def _ssm_fwd(u8, ops, name):
    lam, b_bd, c_bd, dsk, w_bd, bias = ops
    rows_total, w = u8.shape
    seq = rows_total // SUBLANES
    nb = w // LANES
    s = STATES_PER_BLOCK
    tc = _tile(seq, SSM_TIME_CHUNK, 8)
    nk = seq // tc
    rows = tc * SUBLANES

    def body(u_ref, lam_ref, b_ref, c_ref, d_ref, w_ref, bias_ref, y_ref, hb_ref, buf, st):
        k = pl.program_id(1)

        @pl.when(k == 0)
        def _():
            st[...] = jnp.zeros_like(st)

        hb_ref[...] = st[...]
        u = u_ref[...]
        buf[...] = jnp.dot(u.astype(BF16), b_ref[0], preferred_element_type=F32)
        lr = jnp.broadcast_to(lam_ref[0, :, 0:s], (SUBLANES, s))
        li = jnp.broadcast_to(lam_ref[0, :, s:2 * s], (SUBLANES, s))

        def step(t, carry):
            hr, hi = carry
            r0 = pl.multiple_of(t * SUBLANES, SUBLANES)
            nr = lr * hr - li * hi + buf[pl.ds(r0, SUBLANES), 0:s]
            ni = lr * hi + li * hr + buf[pl.ds(r0, SUBLANES), s:2 * s]
            buf[pl.ds(r0, SUBLANES), 0:s] = nr
            buf[pl.ds(r0, SUBLANES), s:2 * s] = ni
            return nr, ni

        hr, hi = lax.fori_loop(0, tc, step, (st[:, 0:s], st[:, s:2 * s]), unroll=4)
        st[:, 0:s] = hr
        st[:, s:2 * s] = hi
        y = jnp.dot(buf[...].astype(BF16), c_ref[0], preferred_element_type=F32) + d_ref[0] * u
        z = jnp.dot(_gelu(y).astype(BF16), w_ref[0], preferred_element_type=F32) + bias_ref[0]
        y_ref[...] = z[:, 0:LANES] * _sigmoid(z[:, LANES:2 * LANES])

    blk = lambda shape: pl.BlockSpec(shape, lambda b, k: (b, 0, 0))
    y8, hb = pl.pallas_call(
        body, name=name, grid=(nb, nk),
        in_specs=[pl.BlockSpec((rows, LANES), lambda b, k: (k, b)),
                  blk((1, 1, 2 * s)), blk((1, LANES, 2 * s)), blk((1, 2 * s, LANES)),
                  blk((1, 1, LANES)), blk((1, LANES, 2 * LANES)), blk((1, 1, 2 * LANES))],
        out_specs=[pl.BlockSpec((rows, LANES), lambda b, k: (k, b)),
                   pl.BlockSpec((SUBLANES, 2 * s), lambda b, k: (k, b))],
        out_shape=[jax.ShapeDtypeStruct((rows_total, w), F32),
                   jax.ShapeDtypeStruct((nk * SUBLANES, nb * 2 * s), F32)],
        scratch_shapes=[pltpu.VMEM((rows, 2 * s), F32), pltpu.VMEM((SUBLANES, 2 * s), F32)],
        compiler_params=_params("parallel", "arbitrary"),
    )(u8, lam, b_bd, c_bd, dsk, w_bd, bias)
    return y8, hb


def _ssm_bwd(u8, dy8, hb, ops, name):
    lam, b_bd, c_bd, dsk, w_bd, bias = ops
    rows_total, w = u8.shape
    seq = rows_total // SUBLANES
    nb = w // LANES
    s = STATES_PER_BLOCK
    tc = _tile(seq, SSM_TIME_CHUNK, 8)
    nk = seq // tc
    rows = tc * SUBLANES
    tn_dims = (((0,), (0,)), ((), ()))
    nt_dims = (((1,), (1,)), ((), ()))

    def body(u_ref, dy_ref, hb_ref, lam_ref, b_ref, c_ref, d_ref, w_ref, bias_ref,
             du_ref, dlam_ref, db_ref, dct_ref, dd_ref, dw_ref, dbias_ref, hbuf, gbuf, gst, lacc):
        k = pl.program_id(1)

        @pl.when(k == 0)
        def _():
            gst[...] = jnp.zeros_like(gst)
            lacc[...] = jnp.zeros_like(lacc)
            db_ref[...] = jnp.zeros_like(db_ref)
            dct_ref[...] = jnp.zeros_like(dct_ref)
            dd_ref[...] = jnp.zeros_like(dd_ref)
            dw_ref[...] = jnp.zeros_like(dw_ref)
            dbias_ref[...] = jnp.zeros_like(dbias_ref)

        u = u_ref[...]
        ub = u.astype(BF16)
        lr = jnp.broadcast_to(lam_ref[0, :, 0:s], (SUBLANES, s))
        li = jnp.broadcast_to(lam_ref[0, :, s:2 * s], (SUBLANES, s))
        hbuf[...] = jnp.dot(ub, b_ref[0], preferred_element_type=F32)

        def fstep(t, carry):
            hr, hi = carry
            r0 = pl.multiple_of(t * SUBLANES, SUBLANES)
            nr = lr * hr - li * hi + hbuf[pl.ds(r0, SUBLANES), 0:s]
            ni = lr * hi + li * hr + hbuf[pl.ds(r0, SUBLANES), s:2 * s]
            hbuf[pl.ds(r0, SUBLANES), 0:s] = nr
            hbuf[pl.ds(r0, SUBLANES), s:2 * s] = ni
            return nr, ni

        lax.fori_loop(0, tc, fstep, (hb_ref[:, 0:s], hb_ref[:, s:2 * s]), unroll=4)
        hb16 = hbuf[...].astype(BF16)
        y = jnp.dot(hb16, c_ref[0], preferred_element_type=F32) + d_ref[0] * u
        yg, dyg_dy = _gelu_and_grad(y)
        yg16 = yg.astype(BF16)
        z = jnp.dot(yg16, w_ref[0], preferred_element_type=F32) + bias_ref[0]
        z1 = z[:, 0:LANES]
        sg = _sigmoid(z[:, LANES:2 * LANES])
        dout = dy_ref[...]
        dz = jnp.concatenate([dout * sg, dout * z1 * sg * (1.0 - sg)], axis=-1)
        dz16 = dz.astype(BF16)
        dw_ref[0] += lax.dot_general(yg16, dz16, tn_dims, preferred_element_type=F32)
        dbias_ref[0] += jnp.sum(dz, axis=0, keepdims=True)
        dy = lax.dot_general(dz16, w_ref[0], nt_dims, preferred_element_type=F32) * dyg_dy
        dy16 = dy.astype(BF16)
        dd_ref[0] += jnp.sum(dy * u, axis=0, keepdims=True)
        dct_ref[0] += lax.dot_general(dy16, hb16, tn_dims, preferred_element_type=F32)
        gbuf[...] = lax.dot_general(dy16, c_ref[0], nt_dims, preferred_element_type=F32)

        def bstep(i, carry):
            gr, gi, ar, ai = carry
            t = tc - 1 - i
            r0 = pl.multiple_of(t * SUBLANES, SUBLANES)
            ngr = gbuf[pl.ds(r0, SUBLANES), 0:s] + lr * gr + li * gi
            ngi = gbuf[pl.ds(r0, SUBLANES), s:2 * s] - li * gr + lr * gi
            gbuf[pl.ds(r0, SUBLANES), 0:s] = ngr
            gbuf[pl.ds(r0, SUBLANES), s:2 * s] = ngi
            p0 = pl.multiple_of(jnp.maximum(t - 1, 0) * SUBLANES, SUBLANES)
            first = t == 0
            hpr = jnp.where(first, hb_ref[:, 0:s], hbuf[pl.ds(p0, SUBLANES), 0:s])
            hpi = jnp.where(first, hb_ref[:, s:2 * s], hbuf[pl.ds(p0, SUBLANES), s:2 * s])
            return ngr, ngi, ar + ngr * hpr + ngi * hpi, ai - ngr * hpi + ngi * hpr

        gr, gi, ar, ai = lax.fori_loop(
            0, tc, bstep, (gst[:, 0:s], gst[:, s:2 * s], lacc[:, 0:s], lacc[:, s:2 * s]), unroll=2)
        gst[:, 0:s] = gr
        gst[:, s:2 * s] = gi
        lacc[:, 0:s] = ar
        lacc[:, s:2 * s] = ai
        g16 = gbuf[...].astype(BF16)
        du_ref[...] = dy * d_ref[0] + lax.dot_general(g16, b_ref[0], nt_dims, preferred_element_type=F32)
        db_ref[0] += lax.dot_general(ub, g16, tn_dims, preferred_element_type=F32)

        @pl.when(k == nk - 1)
        def _():
            dlam_ref[0] = jnp.sum(lacc[...], axis=0, keepdims=True)

    blk = lambda shape: pl.BlockSpec(shape, lambda b, k: (b, 0, 0))
    rev = lambda b, k: (nk - 1 - k, b)
    outs = pl.pallas_call(
        body, name=name, grid=(nb, nk),
        in_specs=[pl.BlockSpec((rows, LANES), rev), pl.BlockSpec((rows, LANES), rev),
                  pl.BlockSpec((SUBLANES, 2 * s), rev),
                  blk((1, 1, 2 * s)), blk((1, LANES, 2 * s)), blk((1, 2 * s, LANES)),
                  blk((1, 1, LANES)), blk((1, LANES, 2 * LANES)), blk((1, 1, 2 * LANES))],
        out_specs=[pl.BlockSpec((rows, LANES), rev),
                   blk((1, 1, 2 * s)), blk((1, LANES, 2 * s)), blk((1, LANES, 2 * s)),
                   blk((1, 1, LANES)), blk((1, LANES, 2 * LANES)), blk((1, 1, 2 * LANES))],
        out_shape=[jax.ShapeDtypeStruct((rows_total, w), F32),
                   jax.ShapeDtypeStruct((nb, 1, 2 * s), F32), jax.ShapeDtypeStruct((nb, LANES, 2 * s), F32),
                   jax.ShapeDtypeStruct((nb, LANES, 2 * s), F32), jax.ShapeDtypeStruct((nb, 1, LANES), F32),
                   jax.ShapeDtypeStruct((nb, LANES, 2 * LANES), F32), jax.ShapeDtypeStruct((nb, 1, 2 * LANES), F32)],
        scratch_shapes=[pltpu.VMEM((rows, 2 * s), F32), pltpu.VMEM((rows, 2 * s), F32),
                        pltpu.VMEM((SUBLANES, 2 * s), F32), pltpu.VMEM((SUBLANES, 2 * s), F32)],
        compiler_params=_params("parallel", "arbitrary"),
    )(u8, dy8, hb, lam, b_bd, c_bd, dsk, w_bd, bias)
    return outs


def _to_scan_rows(a, nseq, seq):
    w = a.shape[-1]
    t = jnp.swapaxes(a.reshape(nseq, seq, w), 0, 1)
    t = jnp.pad(t, ((0, 0), (0, SUBLANES - nseq), (0, 0)))
    return t.reshape(seq * SUBLANES, w)


def _from_scan_rows(a8, nseq, seq):
    w = a8.shape[-1]
    t = a8.reshape(seq, SUBLANES, w)[:, :nseq]
    return jnp.swapaxes(t, 0, 1).reshape(nseq * seq, w)


ANY = pl.BlockSpec(memory_space=pl.ANY)

BIG = (("ffn1_w_in", True), ("ffn1_w_out", False), ("mix_w_in", True), ("mix_w_out", False),
       ("ffn2_w_in", True), ("ffn2_w_out", False))


def _my_place():
    return lax.axis_index("x"), lax.axis_index("y"), lax.axis_index("c")


def _other_chips(x, y):
    return [(1 - x, y), (x, 1 - y), (1 - x, 1 - y)]


def _half_of_shard(ref, col_sharded, chip, core):
    full_rows, full_cols = ref.shape
    if col_sharded:
        hr, cs = full_rows // 2, full_cols // N_CHIPS
        return ref.at[pl.ds(pl.multiple_of(core * hr, 8), hr), pl.ds(chip * cs, cs)]
    rs = full_rows // N_CHIPS
    return ref.at[pl.ds(pl.multiple_of(chip * rs + core * (rs // 2), 8), rs // 2), :]


def _all_gather_weights(shards, cols):
    full_shapes = [(sh.shape[0], sh.shape[1] * N_CHIPS) if col else (sh.shape[0] * N_CHIPS, sh.shape[1])
                   for sh, col in zip(shards, cols)]
    nw = len(shards)

    def body(*refs):
        ins, outs = refs[:nw], refs[nw:2 * nw]
        send_sems, recv_sems, local_sems = refs[2 * nw:]
        x, y, c = _my_place()
        me = 2 * x + y
        chips = _other_chips(x, y)
        locals_, sends, fwds = [], [], []
        for wi in range(nw):
            col = cols[wi]
            src, dst = ins[wi], outs[wi]
            rs, cs = src.shape
            hs = rs // 2
            if col:
                place = dst.at[:, pl.ds(me * cs, cs)]
            else:
                place = dst.at[pl.ds(pl.multiple_of(me * rs, 8), rs), :]
            cp = pltpu.make_async_copy(src, place, local_sems.at[wi])
            cp.start()
            locals_.append(cp)
            my_half = src.at[pl.ds(pl.multiple_of(c * hs, 8), hs), :]
            for j, (px, py) in enumerate(chips):
                cp = pltpu.make_async_remote_copy(
                    src_ref=my_half, dst_ref=_half_of_shard(dst, col, me, c),
                    send_sem=send_sems.at[wi * 6 + j], recv_sem=recv_sems.at[wi * 6 + j],
                    device_id=(px, py, c), device_id_type=MESH)
                cp.start()
                sends.append(cp)
        for wi in range(nw):
            dst = outs[wi]
            for j, (px, py) in enumerate(chips):
                got = _half_of_shard(dst, cols[wi], 2 * px + py, c)
                pltpu.make_async_remote_copy(
                    src_ref=got, dst_ref=got, send_sem=send_sems.at[wi * 6 + j], recv_sem=recv_sems.at[wi * 6 + j],
                    device_id=(px, py, c), device_id_type=MESH).wait_recv()
                cp = pltpu.make_async_remote_copy(
                    src_ref=got, dst_ref=got, send_sem=send_sems.at[wi * 6 + 3 + j], recv_sem=recv_sems.at[wi * 6 + 3 + j],
                    device_id=(x, y, 1 - c), device_id_type=MESH)
                cp.start()
                fwds.append(cp)
        for wi in range(nw):
            dst = outs[wi]
            for j, (px, py) in enumerate(chips):
                theirs = _half_of_shard(dst, cols[wi], 2 * px + py, 1 - c)
                pltpu.make_async_remote_copy(
                    src_ref=theirs, dst_ref=theirs, send_sem=send_sems.at[wi * 6 + 3 + j],
                    recv_sem=recv_sems.at[wi * 6 + 3 + j], device_id=(x, y, 1 - c), device_id_type=MESH).wait_recv()
        for cp in sends + fwds:
            cp.wait_send()
        for cp in locals_:
            cp.wait()

    return pl.pallas_call(
        body, name="all_gather_weights",
        in_specs=[ANY] * nw, out_specs=[ANY] * nw,
        out_shape=[jax.ShapeDtypeStruct(s, BF16) for s in full_shapes],
        scratch_shapes=[pltpu.SemaphoreType.DMA((6 * nw,)), pltpu.SemaphoreType.DMA((6 * nw,)),
                        pltpu.SemaphoreType.DMA((nw,))],
    )(*shards)


def _pair_exchange(grads, cols):
    nw = len(grads)
    n_copies = sum(1 if col else N_CHIPS for col in cols)

    def body(*refs):
        ins, outs = refs[:nw], refs[nw:2 * nw]
        send_sems, recv_sems = refs[2 * nw:]
        x, y, c = _my_place()
        copies = []
        idx = 0
        for wi in range(nw):
            src, dst = ins[wi], outs[wi]
            fr = src.shape[0]
            if cols[wi]:
                hr = fr // 2
                pieces = [(src.at[pl.ds(pl.multiple_of((1 - c) * hr, 8), hr), :], dst)]
            else:
                rs = fr // N_CHIPS
                hs = rs // 2
                pieces = [(src.at[pl.ds(pl.multiple_of(k * rs + (1 - c) * hs, 8), hs), :],
                           dst.at[pl.ds(k * hs, hs), :]) for k in range(N_CHIPS)]
            for s_ref, d_ref in pieces:
                cp = pltpu.make_async_remote_copy(
                    src_ref=s_ref, dst_ref=d_ref, send_sem=send_sems.at[idx], recv_sem=recv_sems.at[idx],
                    device_id=(x, y, 1 - c), device_id_type=MESH)
                cp.start()
                copies.append(cp)
                idx += 1
        for cp in copies:
            cp.wait()

    return pl.pallas_call(
        body, name="grad_pair_exchange",
        in_specs=[ANY] * nw, out_specs=[ANY] * nw,
        out_shape=[jax.ShapeDtypeStruct((g.shape[0] // 2, g.shape[1]), F32) for g in grads],
        scratch_shapes=[pltpu.SemaphoreType.DMA((n_copies,)), pltpu.SemaphoreType.DMA((n_copies,))],
    )(*grads)


def _pair_sum(grad, other, col, core, name):
    fr, fc = grad.shape
    pieces = 1 if col else N_CHIPS
    pr = fr // 2 // pieces
    gview = grad.reshape(pieces * 2, pr, fc)
    oview = other.reshape(pieces, pr, fc)
    tr = _tile(pr, 256, 16)

    def body(c_ref, g_ref, o_ref, out_ref):
        out_ref[...] = (g_ref[...] + o_ref[...]).astype(out_ref.dtype)

    out = pl.pallas_call(
        body, name=name,
        grid_spec=pltpu.PrefetchScalarGridSpec(
            num_scalar_prefetch=1, grid=(pieces, pr // tr),
            in_specs=[pl.BlockSpec((1, tr, fc), lambda p, i, cref: (p * 2 + cref[0], i, 0)),
                      pl.BlockSpec((1, tr, fc), lambda p, i, cref: (p, i, 0))],
            out_specs=pl.BlockSpec((1, tr, fc), lambda p, i, cref: (p, i, 0))),
        out_shape=jax.ShapeDtypeStruct((pieces, pr, fc), BF16),
        compiler_params=_params("parallel", "parallel"),
    )(core, gview, oview)
    return out.reshape(fr // 2, fc)


def _chip_exchange(psums, cols, layers):
    nw = len(psums)
    out_shapes = []
    for wi in range(0, nw, layers):
        hr, fc = psums[wi].shape
        out_shapes.append((N_CHIPS, layers, hr, fc // N_CHIPS) if cols[wi] else (N_CHIPS, layers, hr // N_CHIPS, fc))
    n_out = len(out_shapes)

    def body(*refs):
        ins, outs = refs[:nw], refs[nw:nw + n_out]
        send_sems, recv_sems, local_sems = refs[nw + n_out:]
        x, y, c = _my_place()
        me = 2 * x + y
        chips = _other_chips(x, y)
        copies = []
        for wi in range(nw):
            col = cols[wi]
            src = ins[wi]
            mine = outs[wi // layers].at[me, wi % layers]

            def piece(chip):
                if col:
                    cs = src.shape[1] // N_CHIPS
                    return src.at[:, pl.ds(chip * cs, cs)]
                ps = src.shape[0] // N_CHIPS
                return src.at[pl.ds(pl.multiple_of(chip * ps, 8), ps), :]

            cp = pltpu.make_async_copy(piece(me), mine, local_sems.at[wi])
            cp.start()
            copies.append(cp)
            for j, (px, py) in enumerate(chips):
                cp = pltpu.make_async_remote_copy(
                    src_ref=piece(2 * px + py), dst_ref=mine,
                    send_sem=send_sems.at[wi * 3 + j], recv_sem=recv_sems.at[wi * 3 + j],
                    device_id=(px, py, c), device_id_type=MESH)
                cp.start()
                copies.append(cp)
        for cp in copies:
            cp.wait()

    return pl.pallas_call(
        body, name="grad_chip_exchange",
        in_specs=[ANY] * nw, out_specs=[ANY] * n_out,
        out_shape=[jax.ShapeDtypeStruct(s, BF16) for s in out_shapes],
        scratch_shapes=[pltpu.SemaphoreType.DMA((3 * nw,)), pltpu.SemaphoreType.DMA((3 * nw,)),
                        pltpu.SemaphoreType.DMA((nw,))],
    )(*psums)


def _chip_sum(slots, core, name):
    _, l, hr, cs = slots.shape
    tr = _tile(hr, 256, 16)

    def body(c_ref, s_ref, out_ref):
        acc = s_ref[0, 0].astype(F32)
        for i in range(1, N_CHIPS):
            acc = acc + s_ref[i, 0].astype(F32)
        out_ref[0] = acc

    out = pl.pallas_call(
        body, name=name,
        grid_spec=pltpu.PrefetchScalarGridSpec(
            num_scalar_prefetch=1, grid=(l, hr // tr),
            in_specs=[pl.BlockSpec((N_CHIPS, 1, tr, cs), lambda p, i, cref: (0, p, i, 0))],
            out_specs=pl.BlockSpec((1, tr, cs), lambda p, i, cref: (p * 2 + cref[0], i, 0))),
        out_shape=jax.ShapeDtypeStruct((l * 2, hr, cs), F32),
        compiler_params=_params("parallel", "parallel"),
    )(core, slots)
    return out.reshape(l, 2 * hr, cs)


def _pair_share(reduced):
    nw = len(reduced)

    def body(*refs):
        ins, outs = refs[:nw], refs[nw:2 * nw]
        send_sems, recv_sems = refs[2 * nw:]
        x, y, c = _my_place()
        copies = []
        for wi in range(nw):
            hs = outs[wi].shape[1] // 2
            mine = outs[wi].at[:, pl.ds(pl.multiple_of(c * hs, 8), hs), :]
            cp = pltpu.make_async_remote_copy(
                src_ref=mine, dst_ref=mine, send_sem=send_sems.at[wi], recv_sem=recv_sems.at[wi],
                device_id=(x, y, 1 - c), device_id_type=MESH)
            cp.start()
            copies.append(cp)
        for cp in copies:
            cp.wait()

    return pl.pallas_call(
        body, name="grad_pair_share",
        in_specs=[ANY] * nw, out_specs=[ANY] * nw,
        out_shape=[jax.ShapeDtypeStruct(r.shape, F32) for r in reduced],
        input_output_aliases={i: i for i in range(nw)},
        scratch_shapes=[pltpu.SemaphoreType.DMA((nw,)), pltpu.SemaphoreType.DMA((nw,))],
    )(*reduced)


def _all_reduce_small(flat):
    rows, lanes = flat.shape
    seg = rows // N_DEV

    def body(in_ref, out_ref, recv_ref, send_sems, recv_sems):
        x, y, c = _my_place()
        me = 4 * x + 2 * y + c

        def peer(r):
            fx, fy, fc = (r >> 2) & 1, (r >> 1) & 1, r & 1
            px = jnp.where(fx == 1, 1 - x, x)
            py = jnp.where(fy == 1, 1 - y, y)
            pc = jnp.where(fc == 1, 1 - c, c)
            return px, py, pc

        first = []
        for r in range(1, N_DEV):
            px, py, pc = peer(r)
            theirs = in_ref.at[pl.ds(pl.multiple_of((4 * px + 2 * py + pc) * seg, 8), seg), :]
            cp = pltpu.make_async_remote_copy(
                src_ref=theirs, dst_ref=recv_ref.at[r], send_sem=send_sems.at[r - 1], recv_sem=recv_sems.at[r - 1],
                device_id=(px, py, pc), device_id_type=MESH)
            cp.start()
            first.append(cp)
        for cp in first:
            cp.wait()
        my_rows = pl.ds(pl.multiple_of(me * seg, 8), seg)
        acc = in_ref[my_rows, :]
        for r in range(1, N_DEV):
            acc = acc + recv_ref[r]
        out_ref[my_rows, :] = acc
        second = []
        for r in range(1, N_DEV):
            px, py, pc = peer(r)
            cp = pltpu.make_async_remote_copy(
                src_ref=out_ref.at[my_rows, :], dst_ref=out_ref.at[my_rows, :],
                send_sem=send_sems.at[6 + r], recv_sem=recv_sems.at[6 + r],
                device_id=(px, py, pc), device_id_type=MESH)
            cp.start()
            second.append(cp)
        for r in range(1, N_DEV):
            px, py, pc = peer(r)
            theirs = out_ref.at[pl.ds(pl.multiple_of((4 * px + 2 * py + pc) * seg, 8), seg), :]
            pltpu.make_async_remote_copy(
                src_ref=theirs, dst_ref=theirs, send_sem=send_sems.at[6 + r], recv_sem=recv_sems.at[6 + r],
                device_id=(px, py, pc), device_id_type=MESH).wait_recv()
        for cp in second:
            cp.wait_send()

    vm = pl.BlockSpec(memory_space=pltpu.VMEM)
    return pl.pallas_call(
        body, name="all_reduce_small",
        in_specs=[vm], out_specs=vm,
        out_shape=jax.ShapeDtypeStruct((rows, lanes), F32),
        scratch_shapes=[pltpu.VMEM((N_DEV, seg, lanes), F32),
                        pltpu.SemaphoreType.DMA((2 * (N_DEV - 1),)), pltpu.SemaphoreType.DMA((2 * (N_DEV - 1),))],
        compiler_params=pltpu.CompilerParams(vmem_limit_bytes=VMEM_LIMIT),
    )(flat)


def _adamw(w, g, m, v, name):
    rows, cols = w.shape
    tr = _tile(rows, 256, 8)
    c1 = 1.0 - ADAM_B1 ** ADAM_STEP
    c2 = 1.0 - ADAM_B2 ** ADAM_STEP

    def body(w_ref, g_ref, m_ref, v_ref, d_ref, nm_ref, nv_ref):
        gv = g_ref[...]
        nm = ADAM_B1 * m_ref[...] + (1.0 - ADAM_B1) * gv
        nv = ADAM_B2 * v_ref[...] + (1.0 - ADAM_B2) * (gv * gv)
        d_ref[...] = -ADAM_LR * ((nm / c1) / (jnp.sqrt(nv / c2) + ADAM_EPS) + ADAM_WD * w_ref[...])
        nm_ref[...] = nm
        nv_ref[...] = nv

    blk = pl.BlockSpec((tr, cols), lambda i: (i, 0))
    sds = jax.ShapeDtypeStruct((rows, cols), F32)
    return pl.pallas_call(
        body, name=name, grid=(rows // tr,),
        in_specs=[blk] * 4, out_specs=[blk] * 3, out_shape=[sds] * 3,
        compiler_params=_params("parallel"),
    )(w, g, m, v)


SMALL = ("norm_ffn1", "norm_mix", "ssm_a_re", "ssm_a_im", "ssm_log_dt", "ssm_b_re", "ssm_b_im", "ssm_c_re",
         "ssm_c_im", "ssm_d", "ssm_glu_w", "ssm_glu_b", "gm_v_gain", "gm_w_s", "gm_b_s", "gain_ssm_out",
         "gain_gm_out", "norm_ffn2", "norm_final")
WEIGHTS = ("norm_ffn1", "ffn1_w_in", "ffn1_w_out", "norm_mix", "mix_w_in", "ssm_a_re", "ssm_a_im", "ssm_log_dt",
           "ssm_b_re", "ssm_b_im", "ssm_c_re", "ssm_c_im", "ssm_d", "ssm_glu_w", "ssm_glu_b", "gm_v_gain", "gm_w_s",
           "gm_b_s", "gain_ssm_out", "gain_gm_out", "mix_w_out", "norm_ffn2", "ffn2_w_in", "ffn2_w_out", "norm_final")


def _ffn_fwd(x, gain, w_in, w_out, tag):
    h, g, u, a = _ffn_in_fwd(x, gain, w_in, f"{tag}_in")
    out = _matmul(a, w_out, "nn", scale=0.5, res=x, tm=512, tn=1024, tk=4096, name=f"{tag}_out")
    return out, (x, h, g, u, a)


def _ffn_bwd(dout, saved, gain, w_in, w_out, tag):
    x, h, g, u, a = saved
    f = g.shape[1]
    dg, du = _ffn_out_bwd(dout, w_out, g, u, f"{tag}_out_dx")
    dw_out = _matmul(a, dout, "tn", scale=0.5, tm=1536, tn=1024, tk=2048, name=f"{tag}_out_dw")
    dw_in = _matmul(h, dg, "tn", tm=1024, tn=1536, tk=2048, name=f"{tag}_in_dw_g", out_cols=2 * f)
    dw_in = _matmul(h, du, "tn", tm=1024, tn=1536, tk=2048, name=f"{tag}_in_dw_u", out_cols=2 * f, col_off=f,
                    into=dw_in)
    dx, dgain = _proj_in_bwd([(dg, 0), (du, f)], w_in, x, gain, dout, f"{tag}_in_dx")
    return dx, dgain, dw_in, dw_out


def kernel(x, norm_ffn1, ffn1_w_in, ffn1_w_out, norm_mix, mix_w_in, ssm_a_re, ssm_a_im, ssm_log_dt, ssm_b_re, ssm_b_im, ssm_c_re, ssm_c_im, ssm_d, ssm_glu_w, ssm_glu_b, gm_v_gain, gm_w_s, gm_b_s, gain_ssm_out, gain_gm_out, mix_w_out, norm_ffn2, ffn2_w_in, ffn2_w_out, norm_final, loss_target, m_norm_ffn1, m_ffn1_w_in, m_ffn1_w_out, m_norm_mix, m_mix_w_in, m_ssm_a_re, m_ssm_a_im, m_ssm_log_dt, m_ssm_b_re, m_ssm_b_im, m_ssm_c_re, m_ssm_c_im, m_ssm_d, m_ssm_glu_w, m_ssm_glu_b, m_gm_v_gain, m_gm_w_s, m_gm_b_s, m_gain_ssm_out, m_gain_gm_out, m_mix_w_out, m_norm_ffn2, m_ffn2_w_in, m_ffn2_w_out, m_norm_final, v_norm_ffn1, v_ffn1_w_in, v_ffn1_w_out, v_norm_mix, v_mix_w_in, v_ssm_a_re, v_ssm_a_im, v_ssm_log_dt, v_ssm_b_re, v_ssm_b_im, v_ssm_c_re, v_ssm_c_im, v_ssm_d, v_ssm_glu_w, v_ssm_glu_b, v_gm_v_gain, v_gm_w_s, v_gm_b_s, v_gain_ssm_out, v_gain_gm_out, v_mix_w_out, v_norm_ffn2, v_ffn2_w_in, v_ffn2_w_out, v_norm_final):
    wts = dict(norm_ffn1=norm_ffn1, ffn1_w_in=ffn1_w_in, ffn1_w_out=ffn1_w_out, norm_mix=norm_mix, mix_w_in=mix_w_in,
               ssm_a_re=ssm_a_re, ssm_a_im=ssm_a_im, ssm_log_dt=ssm_log_dt, ssm_b_re=ssm_b_re, ssm_b_im=ssm_b_im,
               ssm_c_re=ssm_c_re, ssm_c_im=ssm_c_im, ssm_d=ssm_d, ssm_glu_w=ssm_glu_w, ssm_glu_b=ssm_glu_b,
               gm_v_gain=gm_v_gain, gm_w_s=gm_w_s, gm_b_s=gm_b_s, gain_ssm_out=gain_ssm_out, gain_gm_out=gain_gm_out,
               mix_w_out=mix_w_out, norm_ffn2=norm_ffn2, ffn2_w_in=ffn2_w_in, ffn2_w_out=ffn2_w_out,
               norm_final=norm_final)
    mom = dict(norm_ffn1=m_norm_ffn1, ffn1_w_in=m_ffn1_w_in, ffn1_w_out=m_ffn1_w_out, norm_mix=m_norm_mix,
               mix_w_in=m_mix_w_in, ssm_a_re=m_ssm_a_re, ssm_a_im=m_ssm_a_im, ssm_log_dt=m_ssm_log_dt,
               ssm_b_re=m_ssm_b_re, ssm_b_im=m_ssm_b_im, ssm_c_re=m_ssm_c_re, ssm_c_im=m_ssm_c_im, ssm_d=m_ssm_d,
               ssm_glu_w=m_ssm_glu_w, ssm_glu_b=m_ssm_glu_b, gm_v_gain=m_gm_v_gain, gm_w_s=m_gm_w_s, gm_b_s=m_gm_b_s,
               gain_ssm_out=m_gain_ssm_out, gain_gm_out=m_gain_gm_out, mix_w_out=m_mix_w_out, norm_ffn2=m_norm_ffn2,
               ffn2_w_in=m_ffn2_w_in, ffn2_w_out=m_ffn2_w_out, norm_final=m_norm_final)
    var = dict(norm_ffn1=v_norm_ffn1, ffn1_w_in=v_ffn1_w_in, ffn1_w_out=v_ffn1_w_out, norm_mix=v_norm_mix,
               mix_w_in=v_mix_w_in, ssm_a_re=v_ssm_a_re, ssm_a_im=v_ssm_a_im, ssm_log_dt=v_ssm_log_dt,
               ssm_b_re=v_ssm_b_re, ssm_b_im=v_ssm_b_im, ssm_c_re=v_ssm_c_re, ssm_c_im=v_ssm_c_im, ssm_d=v_ssm_d,
               ssm_glu_w=v_ssm_glu_w, ssm_glu_b=v_ssm_glu_b, gm_v_gain=v_gm_v_gain, gm_w_s=v_gm_w_s, gm_b_s=v_gm_b_s,
               gain_ssm_out=v_gain_ssm_out, gain_gm_out=v_gain_gm_out, mix_w_out=v_mix_w_out, norm_ffn2=v_norm_ffn2,
               ffn2_w_in=v_ffn2_w_in, ffn2_w_out=v_ffn2_w_out, norm_final=v_norm_final)

    nseq, seq, d = x.shape
    n = nseq * seq
    depth = norm_ffn1.shape[0]
    width = gain_ssm_out.shape[1]
    groups = ssm_a_re.shape[1]
    heads = gm_w_s.shape[1]
    core = lax.axis_index("c").astype(jnp.int32).reshape(1)

    items = [(name, col, l) for name, col in BIG for l in range(depth)]
    item_cols = [col for _, col, _ in items]
    gathered = _all_gather_weights([wts[name][l].astype(BF16) for name, _, l in items], item_cols)
    full = {name: [None] * depth for name, _ in BIG}
    for (name, _, l), w in zip(items, gathered):
        full[name][l] = w

    xs = x.reshape(n, d)
    saved = []
    for l in range(depth):
        x1, s_ffn1 = _ffn_fwd(xs, norm_ffn1[l], full["ffn1_w_in"][l], full["ffn1_w_out"][l], "ffn1")
        hm, u_ssm, zgm = _mix_in_fwd(x1, norm_mix[l], full["mix_w_in"][l], width, "mix_in")
        bt_re = jnp.swapaxes(ssm_b_re[l], 1, 2)
        bt_im = jnp.swapaxes(ssm_b_im[l], 1, 2)
        disc_in = (ssm_a_re[l], ssm_a_im[l], ssm_log_dt[l].reshape(groups, 1), bt_re, bt_im)
        lr, li, bbr, bbi = _disc_fwd(*disc_in)
        ops = _ssm_operands(lr, li, bbr, bbi, ssm_c_re[l], ssm_c_im[l], ssm_d[l], ssm_glu_w[l], ssm_glu_b[l])
        u8 = _to_scan_rows(u_ssm, nseq, seq)
        y8, hb = _ssm_fwd(u8, ops, "s5_fwd")
        y_ssm = _from_scan_rows(y8, nseq, seq)
        bias_tile = jnp.broadcast_to(gm_b_s[l].T[:, :, None], (GM_CHUNK, heads, GM_HEAD_DIM)).reshape(GM_CHUNK, width)
        y_gm = _gmlp_fwd(zgm, gm_v_gain[l], gm_w_s[l], bias_tile, "gmlp_fwd")
        ycat = _mixnorm_fwd(y_ssm, y_gm, gain_ssm_out[l], gain_gm_out[l], "mix_out_norm")
        x2 = _matmul(ycat, full["mix_w_out"][l], "nn", res=x1, tm=512, tn=1024, tk=1024, name="mix_out")
        x3, s_ffn2 = _ffn_fwd(x2, norm_ffn2[l], full["ffn2_w_in"][l], full["ffn2_w_out"][l], "ffn2")
        saved.append(dict(ffn1=s_ffn1, x1=x1, hm=hm, zgm=zgm, disc_in=disc_in, ops=ops, u8=u8, hb=hb, y_ssm=y_ssm,
                          bias_tile=bias_tile, y_gm=y_gm, ycat=ycat, ffn2=s_ffn2))
        xs = x3

    dx, g_norm_final, loss_part = _loss_head(xs, norm_final, loss_target.reshape(n, d))
    big = {name: [None] * depth for name, _ in BIG}
    small = {name: [None] * depth for name in SMALL if name != "norm_final"}
    gpb = GROUPS_PER_BLOCK
    s_blk = STATES_PER_BLOCK
    for l in reversed(range(depth)):
        sv = saved[l]
        dx, small["norm_ffn2"][l], big["ffn2_w_in"][l], big["ffn2_w_out"][l] = _ffn_bwd(
            dx, sv["ffn2"], norm_ffn2[l], full["ffn2_w_in"][l], full["ffn2_w_out"][l], "ffn2")
        dycat = _matmul(dx, full["mix_w_out"][l], "nt", tm=512, tn=1024, tk=1024, name="mix_out_dx")
        big["mix_w_out"][l] = _matmul(sv["ycat"], dx, "tn", tm=1024, tn=1024, tk=2048, name="mix_out_dw")
        dy_ssm, dy_gm, small["gain_ssm_out"][l], small["gain_gm_out"][l] = _mixnorm_bwd(
            sv["y_ssm"], sv["y_gm"], gain_ssm_out[l], gain_gm_out[l], dycat, "mix_out_norm_bwd")
        dzgm, small["gm_w_s"][l], dbias_tile, small["gm_v_gain"][l] = _gmlp_bwd(
            sv["zgm"], dy_gm, gm_v_gain[l], gm_w_s[l], sv["bias_tile"], "gmlp_bwd")
        small["gm_b_s"][l] = dbias_tile.reshape(GM_CHUNK, heads, GM_HEAD_DIM).sum(-1).T
        dy8 = _to_scan_rows(dy_ssm, nseq, seq)
        du8, dlam, db_bd, dct_bd, dd, dw_bd, dbias = _ssm_bwd(sv["u8"], dy8, sv["hb"], sv["ops"], "s5_bwd")
        du_ssm = _from_scan_rows(du8, nseq, seq)
        dlr = dlam[:, 0, :s_blk].reshape(groups, SSM_STATE)
        dli = dlam[:, 0, s_blk:].reshape(groups, SSM_STATE)
        dbbr = _block_diag_extract(db_bd[:, :, :s_blk], gpb)
        dbbi = _block_diag_extract(db_bd[:, :, s_blk:], gpb)
        da_re, da_im, dldt, dbt_re, dbt_im = _disc_bwd(*sv["disc_in"], dlr, dli, dbbr, dbbi)
        small["ssm_a_re"][l], small["ssm_a_im"][l], small["ssm_log_dt"][l] = da_re, da_im, dldt.reshape(groups)
        small["ssm_b_re"][l] = jnp.swapaxes(dbt_re, 1, 2)
        small["ssm_b_im"][l] = jnp.swapaxes(dbt_im, 1, 2)
        small["ssm_c_re"][l] = _block_diag_extract(dct_bd[:, :, :s_blk], gpb)
        small["ssm_c_im"][l] = -_block_diag_extract(dct_bd[:, :, s_blk:], gpb)
        small["ssm_d"][l] = dd.reshape(groups, SSM_CH)
        small["ssm_glu_w"][l] = jnp.concatenate(
            [_block_diag_extract(dw_bd[:, :, :LANES], gpb), _block_diag_extract(dw_bd[:, :, LANES:], gpb)], axis=-1)
        small["ssm_glu_b"][l] = jnp.concatenate(
            [dbias[:, 0, :LANES].reshape(groups, SSM_CH), dbias[:, 0, LANES:].reshape(groups, SSM_CH)], axis=-1)
        cols_mi = 3 * width
        dw_mi = _matmul(sv["hm"], du_ssm, "tn", tm=1024, tn=width, tk=2048, name="mix_in_dw_ssm", out_cols=cols_mi)
        big["mix_w_in"][l] = _matmul(sv["hm"], dzgm, "tn", tm=1024, tn=width, tk=2048, name="mix_in_dw_gm",
                                     out_cols=cols_mi, col_off=width, into=dw_mi)
        dx, small["norm_mix"][l] = _proj_in_bwd([(du_ssm, 0), (dzgm, width)], full["mix_w_in"][l], sv["x1"],
                                                norm_mix[l], dx, "mix_in_dx")
        dx, small["norm_ffn1"][l], big["ffn1_w_in"][l], big["ffn1_w_out"][l] = _ffn_bwd(
            dx, sv["ffn1"], norm_ffn1[l], full["ffn1_w_in"][l], full["ffn1_w_out"][l], "ffn1")
    grad_x = dx.reshape(nseq, seq, d)

    partial = [big[name][l] for name, _, l in items]
    other = _pair_exchange(partial, item_cols)
    psums = [_pair_sum(g, o, col, core, f"grad_pair_sum_{name}")
             for (name, col, _), g, o in zip(items, partial, other)]
    slots = _chip_exchange(psums, item_cols, depth)
    reduced = [_chip_sum(s, core, f"grad_chip_sum_{name}") for (name, _), s in zip(BIG, slots)]
    reduced = _pair_share(reduced)
    grads = dict(zip([name for name, _ in BIG], reduced))

    pieces = [jnp.stack(small[name]).reshape(-1) for name in SMALL if name != "norm_final"]
    pieces += [g_norm_final.reshape(-1), loss_part.reshape(1)]
    sizes = [p.shape[0] for p in pieces]
    total = sum(sizes)
    rows = -(-total // (LANES * N_DEV * SUBLANES)) * N_DEV * SUBLANES
    pad = rows * LANES - total

    def pack(parts, fill):
        return jnp.concatenate(parts + [jnp.full((pad,), fill, F32)]).reshape(rows, LANES)

    flat_g = _all_reduce_small(pack(pieces, 0.0))
    loss = flat_g.reshape(-1)[total - 1]

    delta, new_m, new_v = {}, {}, {}
    for name, _ in BIG:
        shape = wts[name].shape
        two_d = lambda a: a.reshape(shape[0] * shape[1], shape[2])
        dl, nm, nv = _adamw(two_d(wts[name]), two_d(grads[name]), two_d(mom[name]), two_d(var[name]), f"adamw_{name}")
        delta[name], new_m[name], new_v[name] = dl.reshape(shape), nm.reshape(shape), nv.reshape(shape)
    one = [jnp.zeros((1,), F32)]
    dl, nm, nv = _adamw(pack([wts[k].reshape(-1) for k in SMALL] + one, 0.0), flat_g,
                        pack([mom[k].reshape(-1) for k in SMALL] + one, 0.0),
                        pack([var[k].reshape(-1) for k in SMALL] + one, 1.0), "adamw_small")
    offs = 0
    for name, size in zip(SMALL, sizes[:-1]):
        shape = wts[name].shape
        grads[name] = flat_g.reshape(-1)[offs:offs + size].reshape(shape)
        delta[name] = dl.reshape(-1)[offs:offs + size].reshape(shape)
        new_m[name] = nm.reshape(-1)[offs:offs + size].reshape(shape)
        new_v[name] = nv.reshape(-1)[offs:offs + size].reshape(shape)
        offs += size

    return (loss, grad_x, *[grads[k] for k in WEIGHTS], *[delta[k] for k in WEIGHTS],
            *[new_m[k] for k in WEIGHTS], *[new_v[k] for k in WEIGHTS])
```

```python
import functools
import math

import jax
import jax.numpy as jnp
from jax import lax
from jax.experimental import pallas as pl
from jax.experimental.pallas import tpu as pltpu

F32 = jnp.float32
BF16 = jnp.bfloat16
MESH = pl.DeviceIdType.MESH

EPS = 1e-6
SSM_CH = 16
SSM_STATE = 64
GM_CHUNK = 128
GM_HEAD_DIM = 128
SUBLANES = 8
LANES = 128
GROUPS_PER_BLOCK = LANES // SSM_CH
STATES_PER_BLOCK = GROUPS_PER_BLOCK * SSM_STATE
SSM_TIME_CHUNK = 128
N_CHIPS = 4
N_DEV = 8

ADAM_LR = 0.001
ADAM_B1 = 0.9
ADAM_B2 = 0.999
ADAM_EPS = 1e-08
ADAM_WD = 0.01
ADAM_STEP = 10

VMEM_LIMIT = 56 * 1024 * 1024


def _tile(dim, pref, align):
    best = None
    t = align
    while t <= min(dim, pref):
        if dim % t == 0:
            best = t
        t += align
    return best if best is not None else dim


def _params(*sem):
    return pltpu.CompilerParams(dimension_semantics=sem, vmem_limit_bytes=VMEM_LIMIT)


def _gelu(x):
    c = math.sqrt(2.0 / math.pi)
    return 0.5 * x * (1.0 + jnp.tanh(c * (x + 0.044715 * x * x * x)))


def _gelu_and_grad(x):
    c = math.sqrt(2.0 / math.pi)
    t = jnp.tanh(c * (x + 0.044715 * x * x * x))
    g = 0.5 * x * (1.0 + t)
    dg = 0.5 * (1.0 + t) + 0.5 * x * (1.0 - t * t) * c * (1.0 + 3.0 * 0.044715 * x * x)
    return g, dg


def _sigmoid(x):
    return 0.5 * jnp.tanh(0.5 * x) + 0.5


def _matmul(a, b, mode, *, out_dtype=F32, scale=1.0, res=None, tm=512, tn=1024, tk=1024, name="mm",
            out_cols=None, col_off=0, into=None, comm=None):
    if mode == "nn":
        (m, k), (k2, n) = a.shape, b.shape
    elif mode == "nt":
        (m, k), (n, k2) = a.shape, b.shape
    else:
        (k, m), (k2, n) = a.shape, b.shape
    assert k == k2, (a.shape, b.shape, mode)
    tm = _tile(m, tm, 16 if mode != "tn" else LANES)
    tn = _tile(n, tn, LANES)
    tk = _tile(k, tk, LANES if mode != "tn" else 16)
    nk = k // tk
    grid = (m // tm, n // tn, nk)
    if mode == "nn":
        a_spec = pl.BlockSpec((tm, tk), lambda i, j, kk: (i, kk))
        b_spec = pl.BlockSpec((tk, tn), lambda i, j, kk: (kk, j))
        dims = (((1,), (0,)), ((), ()))
    elif mode == "nt":
        a_spec = pl.BlockSpec((tm, tk), lambda i, j, kk: (i, kk))
        b_spec = pl.BlockSpec((tn, tk), lambda i, j, kk: (j, kk))
        dims = (((1,), (1,)), ((), ()))
    else:
        a_spec = pl.BlockSpec((tk, tm), lambda i, j, kk: (kk, i))
        b_spec = pl.BlockSpec((tk, tn), lambda i, j, kk: (kk, j))
        dims = (((0,), (0,)), ((), ()))
    assert col_off % tn == 0
    off = col_off // tn
    r_spec = pl.BlockSpec((tm, tn), lambda i, j, kk: (i, j))
    o_spec = pl.BlockSpec((tm, tn), lambda i, j, kk: (i, j + off))
    has_res = res is not None
    has_into = into is not None

    def body(*refs):
        refs = list(refs)
        a_ref, b_ref = refs[:2]
        pos = 2
        r_ref = None
        if has_res:
            r_ref = refs[pos]
            pos += 1
        if has_into:
            pos += 1
        o_ref = refs[pos]
        acc_ref = refs[pos + 1] if nk > 1 else None
        part = lax.dot_general(a_ref[...].astype(BF16), b_ref[...].astype(BF16), dims,
                               preferred_element_type=F32)

        def finish(r):
            if scale != 1.0:
                r = r * scale
            if has_res:
                r = r + r_ref[...].astype(F32)
            o_ref[...] = r.astype(o_ref.dtype)

        if nk == 1:
            finish(part)
        else:
            kk = pl.program_id(2)

            @pl.when(kk == 0)
            def _():
                acc_ref[...] = part

            @pl.when(kk > 0)
            def _():
                acc_ref[...] += part

            @pl.when(kk == nk - 1)
            def _():
                finish(acc_ref[...])

    in_specs = [a_spec, b_spec]
    args = [a, b]
    if has_res:
        in_specs.append(r_spec)
        args.append(res)
    aliases = {}
    if has_into:
        in_specs.append(pl.BlockSpec(memory_space=pl.ANY))
        args.append(into)
        aliases = {len(args) - 1: 0}
    (out,), comm_outs = _call(
        body, name=name, grid=grid, in_specs=in_specs, out_specs=[o_spec],
        out_shape=[jax.ShapeDtypeStruct((m, n if out_cols is None else out_cols), out_dtype)],
        scratch_shapes=[pltpu.VMEM((tm, tn), F32)] if nk > 1 else [],
        aliases=aliases, semantics=("parallel", "parallel", "arbitrary"), args=args, comm=comm)
    return out if comm is None else (out, comm_outs)


class _Comm:
    def __init__(self, ins, outs, sems, start, finish, alias=None):
        self.ins, self.outs, self.sems, self.start, self.finish = list(ins), list(outs), list(sems), start, finish
        self.alias = dict(alias or {})


def _call(body, *, name, grid, in_specs, out_specs, out_shape, args, scratch_shapes=(), semantics=(), aliases=None,
          comm=None):
    in_specs, out_specs, out_shape = list(in_specs), list(out_specs), list(out_shape)
    scratch_shapes = list(scratch_shapes)
    aliases = dict(aliases or {})
    if comm is None:
        outs = pl.pallas_call(
            body, name=name, grid=grid, in_specs=in_specs, out_specs=out_specs, out_shape=out_shape,
            scratch_shapes=scratch_shapes, input_output_aliases=aliases, compiler_params=_params(*semantics),
        )(*args)
        return list(outs), []
    n_in, n_out, n_scr = len(in_specs), len(out_specs), len(scratch_shapes)
    c_in, c_out = len(comm.ins), len(comm.outs)
    for ci, co in comm.alias.items():
        aliases[n_in + ci] = n_out + co

    def hosted(*refs):
        refs = list(refs)
        ins, cins = refs[:n_in], refs[n_in:n_in + c_in]
        p = n_in + c_in
        outs, couts = refs[p:p + n_out], refs[p + n_out:p + n_out + c_out]
        p += n_out + c_out
        scr, sems = refs[p:p + n_scr], refs[p + n_scr:]
        ids = [pl.program_id(a) for a in range(len(grid))]
        first = functools.reduce(jnp.logical_and, [i == 0 for i in ids])
        last = functools.reduce(jnp.logical_and, [i == g - 1 for i, g in zip(ids, grid)])

        @pl.when(first)
        def _():
            comm.start(cins, couts, sems)

        body(*ins, *outs, *scr)

        @pl.when(last)
        def _():
            comm.finish(cins, couts, sems)

    any_spec = pl.BlockSpec(memory_space=pl.ANY)
    outs = pl.pallas_call(
        hosted, name=name, grid=grid, in_specs=in_specs + [any_spec] * c_in, out_specs=out_specs + [any_spec] * c_out,
        out_shape=out_shape + comm.outs, scratch_shapes=scratch_shapes + comm.sems, input_output_aliases=aliases,
        compiler_params=_params(*(["arbitrary"] * len(grid))),
    )(*args, *comm.ins)
    return list(outs[:n_out]), list(outs[n_out:])


def _run_comm(comm, name):
    c_in, c_out = len(comm.ins), len(comm.outs)

    def body(*refs):
        refs = list(refs)
        cins, couts, sems = refs[:c_in], refs[c_in:c_in + c_out], refs[c_in + c_out:]
        comm.start(cins, couts, sems)
        comm.finish(cins, couts, sems)

    any_spec = pl.BlockSpec(memory_space=pl.ANY)
    return list(pl.pallas_call(
        body, name=name, in_specs=[any_spec] * c_in, out_specs=[any_spec] * c_out, out_shape=comm.outs,
        scratch_shapes=comm.sems, input_output_aliases=comm.alias,
    )(*comm.ins))


def _loss_head(x, gain, target):
    n, d = x.shape
    tm = _tile(n, 512, 8)
    steps = n // tm

    def body(x_ref, g_ref, t_ref, dx_ref, dg_ref, loss_ref, acc_ref, lacc_ref):
        i = pl.program_id(0)
        xv = x_ref[...]
        g = g_ref[...]
        r = lax.rsqrt(jnp.mean(xv * xv, axis=-1, keepdims=True) + EPS)
        xh = xv * r
        err = xh * g - t_ref[...]
        dy = err * (1.0 / d)
        dyg = dy * g
        mean = jnp.mean(dyg * xh, axis=-1, keepdims=True)
        dx_ref[...] = r * (dyg - xh * mean)
        part = jnp.sum((dy * xh).reshape(tm // SUBLANES, SUBLANES, d), axis=0)
        lpart = jnp.sum((err * err).reshape(tm // SUBLANES, SUBLANES, d), axis=0)

        @pl.when(i == 0)
        def _():
            acc_ref[...] = part
            lacc_ref[...] = lpart

        @pl.when(i > 0)
        def _():
            acc_ref[...] += part
            lacc_ref[...] += lpart

        @pl.when(i == steps - 1)
        def _():
            dg_ref[...] = jnp.sum(acc_ref[...], axis=0, keepdims=True)
            tot = jnp.sum(jnp.sum(lacc_ref[...], axis=0, keepdims=True), axis=1, keepdims=True)
            loss_ref[...] = jnp.broadcast_to(tot * (0.5 / d), loss_ref.shape)

    row = pl.BlockSpec((tm, d), lambda i: (i, 0))
    vec = pl.BlockSpec((1, d), lambda i: (0, 0))
    dx, dg, loss = pl.pallas_call(
        body, name="loss_head", grid=(steps,),
        in_specs=[row, vec, row],
        out_specs=[row, vec, pl.BlockSpec((1, LANES), lambda i: (0, 0))],
        out_shape=[jax.ShapeDtypeStruct((n, d), F32), jax.ShapeDtypeStruct((1, d), F32),
                   jax.ShapeDtypeStruct((1, LANES), F32)],
        scratch_shapes=[pltpu.VMEM((SUBLANES, d), F32), pltpu.VMEM((SUBLANES, d), F32)],
        compiler_params=_params("arbitrary"),
    )(x, gain.reshape(1, d), target)
    return dx, dg.reshape(d), loss[0, 0]


def _rms_rows(xv):
    return lax.rsqrt(jnp.mean(xv * xv, axis=-1, keepdims=True) + EPS)


def _ffn_in_fwd(x, gain, w_in, name, comm=None):
    n, d = x.shape
    f = w_in.shape[1] // 2
    tm = _tile(n, 256, 16)
    tn = _tile(f, 4096, LANES)
    nj = f // tn

    def body(x_ref, gain_ref, wg_ref, wu_ref, h_ref, g_ref, u_ref, a_ref):
        @pl.when(pl.program_id(1) == 0)
        def _():
            xv = x_ref[...]
            h_ref[...] = (xv * _rms_rows(xv) * gain_ref[...]).astype(h_ref.dtype)

        h = h_ref[...]
        g = jnp.dot(h, wg_ref[...], preferred_element_type=F32)
        u = jnp.dot(h, wu_ref[...], preferred_element_type=F32)
        g_ref[...] = g.astype(g_ref.dtype)
        u_ref[...] = u.astype(u_ref.dtype)
        a_ref[...] = (g * _sigmoid(g) * u).astype(a_ref.dtype)

    row = pl.BlockSpec((tm, d), lambda i, j: (i, 0))
    tile = pl.BlockSpec((tm, tn), lambda i, j: (i, j))
    act = jax.ShapeDtypeStruct((n, f), BF16)
    outs, comm_outs = _call(
        body, name=name, grid=(n // tm, nj),
        in_specs=[row, pl.BlockSpec((1, d), lambda i, j: (0, 0)),
                  pl.BlockSpec((d, tn), lambda i, j: (0, j)), pl.BlockSpec((d, tn), lambda i, j: (0, j + nj))],
        out_specs=[row, tile, tile, tile],
        out_shape=[jax.ShapeDtypeStruct((n, d), BF16), act, act, act],
        semantics=("parallel", "arbitrary"), args=(x, gain.reshape(1, d), w_in, w_in), comm=comm)
    return outs if comm is None else (outs, comm_outs)


def _ffn_out_bwd(dout, w_out, g, u, name, comm=None):
    n, d = dout.shape
    f = w_out.shape[0]
    tm = _tile(n, 256, 16)
    tn = _tile(f, 4096, LANES)

    def body(d_ref, w_ref, g_ref, u_ref, dg_ref, du_ref):
        da = 0.5 * lax.dot_general(d_ref[...].astype(BF16), w_ref[...], (((1,), (1,)), ((), ())),
                                   preferred_element_type=F32)
        gv = g_ref[...].astype(F32)
        s = _sigmoid(gv)
        dg_ref[...] = (da * u_ref[...].astype(F32) * (s * (1.0 + gv * (1.0 - s)))).astype(dg_ref.dtype)
        du_ref[...] = (da * gv * s).astype(du_ref.dtype)

    tile = pl.BlockSpec((tm, tn), lambda i, j: (i, j))
    act = jax.ShapeDtypeStruct((n, f), BF16)
    outs, comm_outs = _call(
        body, name=name, grid=(n // tm, f // tn),
        in_specs=[pl.BlockSpec((tm, d), lambda i, j: (i, 0)), pl.BlockSpec((tn, d), lambda i, j: (j, 0)), tile, tile],
        out_specs=[tile, tile], out_shape=[act, act],
        semantics=("parallel", "parallel"), args=(dout, w_out, g, u), comm=comm)
    return outs if comm is None else (outs, comm_outs)


def _proj_in_bwd(parts, w, x, gain, dres, name, comm=None):
    n, d = x.shape
    tm = _tile(n, 256, 8)
    steps = n // tm
    np_ = len(parts)
    offs = [off for _, off in parts]
    widths = [a.shape[1] for a, _ in parts]

    def body(*refs):
        a_refs = refs[:np_]
        w_ref, x_ref, g_ref, dr_ref, dx_ref, dg_ref, acc_ref = refs[np_:]
        i = pl.program_id(0)
        dh = None
        for a_ref, off, kp in zip(a_refs, offs, widths):
            part = lax.dot_general(a_ref[...].astype(BF16), w_ref[:, off:off + kp], (((1,), (1,)), ((), ())),
                                   preferred_element_type=F32)
            dh = part if dh is None else dh + part
        xv = x_ref[...]
        r = _rms_rows(xv)
        xh = xv * r
        dyg = dh * g_ref[...]
        mean = jnp.mean(dyg * xh, axis=-1, keepdims=True)
        dx_ref[...] = dr_ref[...] + r * (dyg - xh * mean)
        part = jnp.sum((dh * xh).reshape(tm // SUBLANES, SUBLANES, d), axis=0)

        @pl.when(i == 0)
        def _():
            acc_ref[...] = part

        @pl.when(i > 0)
        def _():
            acc_ref[...] += part

        @pl.when(i == steps - 1)
        def _():
            dg_ref[...] = jnp.sum(acc_ref[...], axis=0, keepdims=True)

    row = pl.BlockSpec((tm, d), lambda i: (i, 0))
    vec = pl.BlockSpec((1, d), lambda i: (0, 0))
    (dx, dg), comm_outs = _call(
        body, name=name, grid=(steps,),
        in_specs=[pl.BlockSpec((tm, kp), lambda i: (i, 0)) for kp in widths]
        + [pl.BlockSpec(w.shape, lambda i: (0, 0)), row, vec, row],
        out_specs=[row, vec],
        out_shape=[jax.ShapeDtypeStruct((n, d), F32), jax.ShapeDtypeStruct((1, d), F32)],
        scratch_shapes=[pltpu.VMEM((SUBLANES, d), F32)],
        semantics=("arbitrary",), args=(*[a for a, _ in parts], w, x, gain.reshape(1, d), dres), comm=comm)
    return (dx, dg.reshape(d)) if comm is None else (dx, dg.reshape(d), comm_outs)


def _mix_in_fwd(x, gain, w, width, name):
    n, d = x.shape
    cols = w.shape[1]
    tm = _tile(n, 512, 16)

    def body(x_ref, gain_ref, w_ref, h_ref, u_ref, z_ref):
        xv = x_ref[...]
        h = (xv * _rms_rows(xv) * gain_ref[...]).astype(h_ref.dtype)
        h_ref[...] = h
        z = jnp.dot(h, w_ref[...], preferred_element_type=F32)
        u_ref[...] = z[:, 0:width]
        z_ref[...] = z[:, width:cols]

    row = pl.BlockSpec((tm, d), lambda i: (i, 0))
    return pl.pallas_call(
        body, name=name, grid=(n // tm,),
        in_specs=[row, pl.BlockSpec((1, d), lambda i: (0, 0)), pl.BlockSpec((d, cols), lambda i: (0, 0))],
        out_specs=[row, pl.BlockSpec((tm, width), lambda i: (i, 0)), pl.BlockSpec((tm, cols - width), lambda i: (i, 0))],
        out_shape=[jax.ShapeDtypeStruct((n, d), BF16), jax.ShapeDtypeStruct((n, width), F32),
                   jax.ShapeDtypeStruct((n, cols - width), F32)],
        compiler_params=_params("parallel"),
    )(x, gain.reshape(1, d), w)


def _tril_mask():
    t = lax.broadcasted_iota(jnp.int32, (GM_CHUNK, GM_CHUNK), 0)
    s = lax.broadcasted_iota(jnp.int32, (GM_CHUNK, GM_CHUNK), 1)
    return s <= t


def _gmlp_fwd(zgm, v_gain, w_s, bias_tile, name):
    n, w2 = zgm.shape
    w = w2 // 2
    heads = w // GM_HEAD_DIM
    tm = _tile(n, 512, GM_CHUNK)
    nq = tm // GM_CHUNK

    def body(u_ref, v_ref, gain_ref, w_ref, b_ref, o_ref):
        mask = _tril_mask()
        ug = _gelu(u_ref[...])
        vg = _gelu(v_ref[...])
        for h in range(heads):
            cols = slice(h * GM_HEAD_DIM, (h + 1) * GM_HEAD_DIM)
            vh = vg[:, cols]
            r = lax.rsqrt(jnp.mean(vh * vh, axis=-1, keepdims=True) + EPS)
            vn = (vh * r * gain_ref[:, cols]).astype(BF16)
            wm = jnp.where(mask, w_ref[h], 0.0).astype(BF16)
            for q in range(nq):
                rows = slice(q * GM_CHUNK, (q + 1) * GM_CHUNK)
                s = jnp.dot(wm, vn[rows], preferred_element_type=F32) + b_ref[:, cols]
                o_ref[rows, cols] = ug[rows, cols] * s

    return pl.pallas_call(
        body, name=name, grid=(n // tm,),
        in_specs=[pl.BlockSpec((tm, w), lambda i: (i, 0)), pl.BlockSpec((tm, w), lambda i: (i, 1)),
                  pl.BlockSpec((1, w), lambda i: (0, 0)),
                  pl.BlockSpec((heads, GM_CHUNK, GM_CHUNK), lambda i: (0, 0, 0)),
                  pl.BlockSpec((GM_CHUNK, w), lambda i: (0, 0))],
        out_specs=pl.BlockSpec((tm, w), lambda i: (i, 0)),
        out_shape=jax.ShapeDtypeStruct((n, w), F32),
        compiler_params=_params("parallel"),
    )(zgm, zgm, v_gain.reshape(1, w), w_s, bias_tile)


def _gmlp_bwd(zgm, dy, v_gain, w_s, bias_tile, name):
    n, w2 = zgm.shape
    w = w2 // 2
    heads = w // GM_HEAD_DIM
    tm = _tile(n, 512, GM_CHUNK)
    nq = tm // GM_CHUNK
    steps = n // tm

    def body(z_ref, dy_ref, gain_ref, w_ref, b_ref, dz_ref, dw_ref, db_ref, dgain_ref):
        i = pl.program_id(0)
        mask = _tril_mask()

        @pl.when(i == 0)
        def _():
            dw_ref[...] = jnp.zeros_like(dw_ref)
            db_ref[...] = jnp.zeros_like(db_ref)
            dgain_ref[...] = jnp.zeros_like(dgain_ref)

        ug, dug_du = _gelu_and_grad(z_ref[:, 0:w])
        vg, dvg_dv = _gelu_and_grad(z_ref[:, w:w2])
        dyv = dy_ref[...]
        for h in range(heads):
            cols = slice(h * GM_HEAD_DIM, (h + 1) * GM_HEAD_DIM)
            vh = vg[:, cols]
            r = lax.rsqrt(jnp.mean(vh * vh, axis=-1, keepdims=True) + EPS)
            vhat = vh * r
            gain = gain_ref[:, cols]
            vn = (vhat * gain).astype(BF16)
            wm = jnp.where(mask, w_ref[h], 0.0).astype(BF16)
            dvn_parts = []
            for q in range(nq):
                rows = slice(q * GM_CHUNK, (q + 1) * GM_CHUNK)
                s = jnp.dot(wm, vn[rows], preferred_element_type=F32) + b_ref[:, cols]
                dyq = dyv[rows, cols]
                dz_ref[rows, cols] = dyq * s * dug_du[rows, cols]
                ds = dyq * ug[rows, cols]
                db_ref[:, cols] += ds
                dsb = ds.astype(BF16)
                dw_ref[h] += lax.dot_general(dsb, vn[rows], (((1,), (1,)), ((), ())), preferred_element_type=F32)
                dvn_parts.append(lax.dot_general(wm, dsb, (((0,), (0,)), ((), ())), preferred_element_type=F32))
            dvn = jnp.concatenate(dvn_parts, axis=0) if nq > 1 else dvn_parts[0]
            dgain_ref[:, cols] += jnp.sum(dvn * vhat, axis=0, keepdims=True)
            dvhat = dvn * gain
            mean = jnp.mean(dvhat * vhat, axis=-1, keepdims=True)
            dz_ref[:, w + h * GM_HEAD_DIM:w + (h + 1) * GM_HEAD_DIM] = r * (dvhat - vhat * mean) * dvg_dv[:, cols]

        @pl.when(i == steps - 1)
        def _():
            for h in range(heads):
                dw_ref[h] = jnp.where(mask, dw_ref[h], 0.0)

    dz, dw, db, dgain = pl.pallas_call(
        body, name=name, grid=(steps,),
        in_specs=[pl.BlockSpec((tm, w2), lambda i: (i, 0)), pl.BlockSpec((tm, w), lambda i: (i, 0)),
                  pl.BlockSpec((1, w), lambda i: (0, 0)),
                  pl.BlockSpec((heads, GM_CHUNK, GM_CHUNK), lambda i: (0, 0, 0)),
                  pl.BlockSpec((GM_CHUNK, w), lambda i: (0, 0))],
        out_specs=[pl.BlockSpec((tm, w2), lambda i: (i, 0)),
                   pl.BlockSpec((heads, GM_CHUNK, GM_CHUNK), lambda i: (0, 0, 0)),
                   pl.BlockSpec((GM_CHUNK, w), lambda i: (0, 0)),
                   pl.BlockSpec((1, w), lambda i: (0, 0))],
        out_shape=[jax.ShapeDtypeStruct((n, w2), F32), jax.ShapeDtypeStruct((heads, GM_CHUNK, GM_CHUNK), F32),
                   jax.ShapeDtypeStruct((GM_CHUNK, w), F32), jax.ShapeDtypeStruct((1, w), F32)],
        compiler_params=_params("arbitrary"),
    )(zgm, dy, v_gain.reshape(1, w), w_s, bias_tile)
    return dz, dw, db, dgain.reshape(w)


def _mixnorm_fwd(y_ssm, y_gm, g1, g2, name):
    n, w = y_ssm.shape
    tm = _tile(n, 512, 16)

    def body(a_ref, b_ref, g1_ref, g2_ref, o_ref):
        for src, g_ref, lo in ((a_ref, g1_ref, 0), (b_ref, g2_ref, w)):
            v = src[...]
            r = lax.rsqrt(jnp.mean(v * v, axis=-1, keepdims=True) + EPS)
            o_ref[:, lo:lo + w] = (v * r * g_ref[...]).astype(o_ref.dtype)

    row = pl.BlockSpec((tm, w), lambda i: (i, 0))
    vec = pl.BlockSpec((1, w), lambda i: (0, 0))
    return pl.pallas_call(
        body, name=name, grid=(n // tm,),
        in_specs=[row, row, vec, vec], out_specs=pl.BlockSpec((tm, 2 * w), lambda i: (i, 0)),
        out_shape=jax.ShapeDtypeStruct((n, 2 * w), BF16),
        compiler_params=_params("parallel"),
    )(y_ssm, y_gm, g1.reshape(1, w), g2.reshape(1, w))


def _mixnorm_bwd(y_ssm, y_gm, g1, g2, dycat, name):
    n, w = y_ssm.shape
    tm = _tile(n, 512, 8)
    steps = n // tm

    def body(a_ref, b_ref, g1_ref, g2_ref, d_ref, da_ref, db_ref, dg1_ref, dg2_ref):
        i = pl.program_id(0)

        @pl.when(i == 0)
        def _():
            dg1_ref[...] = jnp.zeros_like(dg1_ref)
            dg2_ref[...] = jnp.zeros_like(dg2_ref)

        for src, g_ref, lo, dst, dg_ref in ((a_ref, g1_ref, 0, da_ref, dg1_ref), (b_ref, g2_ref, w, db_ref, dg2_ref)):
            v = src[...]
            dh = d_ref[:, lo:lo + w]
            r = lax.rsqrt(jnp.mean(v * v, axis=-1, keepdims=True) + EPS)
            vh = v * r
            dyg = dh * g_ref[...]
            mean = jnp.mean(dyg * vh, axis=-1, keepdims=True)
            dst[...] = r * (dyg - vh * mean)
            dg_ref[...] += jnp.sum(dh * vh, axis=0, keepdims=True)

    row = pl.BlockSpec((tm, w), lambda i: (i, 0))
    vec = pl.BlockSpec((1, w), lambda i: (0, 0))
    da, db, dg1, dg2 = pl.pallas_call(
        body, name=name, grid=(steps,),
        in_specs=[row, row, vec, vec, pl.BlockSpec((tm, 2 * w), lambda i: (i, 0))],
        out_specs=[row, row, vec, vec],
        out_shape=[jax.ShapeDtypeStruct((n, w), F32), jax.ShapeDtypeStruct((n, w), F32),
                   jax.ShapeDtypeStruct((1, w), F32), jax.ShapeDtypeStruct((1, w), F32)],
        compiler_params=_params("arbitrary"),
    )(y_ssm, y_gm, g1.reshape(1, w), g2.reshape(1, w), dycat)
    return da, db, dg1.reshape(w), dg2.reshape(w)


def _discretise(a_re, a_im, log_dt, bt_re, bt_im):
    dt = jnp.exp(log_dt)
    e = jnp.exp(a_re * dt)
    ang = a_im * dt
    lr = e * jnp.cos(ang)
    li = e * jnp.sin(ang)
    den = a_re * a_re + a_im * a_im
    cr = ((lr - 1.0) * a_re + li * a_im) / den
    ci = (li * a_re - (lr - 1.0) * a_im) / den
    cr3 = cr[:, None, :]
    ci3 = ci[:, None, :]
    return lr, li, cr3 * bt_re - ci3 * bt_im, cr3 * bt_im + ci3 * bt_re


def _disc_fwd(a_re, a_im, log_dt, bt_re, bt_im):
    g, p = a_re.shape
    c = bt_re.shape[1]

    def body(are_ref, aim_ref, ldt_ref, bre_ref, bim_ref, lr_ref, li_ref, bbr_ref, bbi_ref):
        lr, li, bbr, bbi = _discretise(are_ref[...], aim_ref[...], ldt_ref[...], bre_ref[...], bim_ref[...])
        lr_ref[...] = lr
        li_ref[...] = li
        bbr_ref[...] = bbr
        bbi_ref[...] = bbi

    return pl.pallas_call(
        body, name="s5_discretise",
        out_shape=[jax.ShapeDtypeStruct((g, p), F32), jax.ShapeDtypeStruct((g, p), F32),
                   jax.ShapeDtypeStruct((g, c, p), F32), jax.ShapeDtypeStruct((g, c, p), F32)],
    )(a_re, a_im, log_dt, bt_re, bt_im)


def _disc_bwd(a_re, a_im, log_dt, bt_re, bt_im, dlr, dli, dbbr, dbbi):
    g, p = a_re.shape
    c = bt_re.shape[1]

    def body(are_ref, aim_ref, ldt_ref, bre_ref, bim_ref, dlr_ref, dli_ref, dbbr_ref, dbbi_ref,
             dare_ref, daim_ref, dldt_ref, dbre_ref, dbim_ref):
        _, vjp = jax.vjp(_discretise, are_ref[...], aim_ref[...], ldt_ref[...], bre_ref[...], bim_ref[...])
        dare, daim, dldt, dbre, dbim = vjp((dlr_ref[...], dli_ref[...], dbbr_ref[...], dbbi_ref[...]))
        dare_ref[...] = dare
        daim_ref[...] = daim
        dldt_ref[...] = dldt
        dbre_ref[...] = dbre
        dbim_ref[...] = dbim

    return pl.pallas_call(
        body, name="s5_discretise_bwd",
        out_shape=[jax.ShapeDtypeStruct((g, p), F32), jax.ShapeDtypeStruct((g, p), F32),
                   jax.ShapeDtypeStruct((g, 1), F32),
                   jax.ShapeDtypeStruct((g, c, p), F32), jax.ShapeDtypeStruct((g, c, p), F32)],
    )(a_re, a_im, log_dt, bt_re, bt_im, dlr, dli, dbbr, dbbi)


def _block_diag(w, nb):
    g, a, b = w.shape
    gpb = g // nb
    eye = jnp.eye(gpb, dtype=w.dtype)
    w4 = w.reshape(nb, gpb, a, b)
    return jnp.einsum("ngab,gh->ngahb", w4, eye).reshape(nb, gpb * a, gpb * b)


def _block_diag_extract(m, gpb):
    nb, ga, gb = m.shape
    a, b = ga // gpb, gb // gpb
    m5 = m.reshape(nb, gpb, a, gpb, b)
    idx = jnp.arange(gpb)
    return m5[:, idx, :, idx, :].transpose(1, 0, 2, 3).reshape(nb * gpb, a, b)


def _ssm_operands(lr, li, bbr, bbi, c_re, c_im, d_skip, glu_w, glu_b):
    g = lr.shape[0]
    nb = g // GROUPS_PER_BLOCK
    s = STATES_PER_BLOCK
    lam = jnp.concatenate([lr.reshape(nb, 1, s), li.reshape(nb, 1, s)], axis=-1)
    b_bd = jnp.concatenate([_block_diag(bbr, nb), _block_diag(bbi, nb)], axis=-1)
    ct_re = jnp.swapaxes(c_re, 1, 2)
    ct_im = jnp.swapaxes(c_im, 1, 2)
    c_bd = jnp.concatenate([_block_diag(ct_re, nb), -_block_diag(ct_im, nb)], axis=1)
    dsk = d_skip.reshape(nb, 1, LANES)
    w_bd = jnp.concatenate([_block_diag(glu_w[:, :, :SSM_CH], nb), _block_diag(glu_w[:, :, SSM_CH:], nb)], axis=-1)
    bias = jnp.concatenate([glu_b[:, :SSM_CH].reshape(nb, 1, LANES), glu_b[:, SSM_CH:].reshape(nb, 1, LANES)], axis=-1)
    return lam, b_bd.astype(BF16), c_bd.astype(BF16), dsk, w_bd.astype(BF16), bias


def _ssm_fwd(u8, ops, name, comm=None):
    lam, b_bd, c_bd, dsk, w_bd, bias = ops
    rows_total, w = u8.shape
    seq = rows_total // SUBLANES
    nb = w // LANES
    s = STATES_PER_BLOCK
    tc = _tile(seq, SSM_TIME_CHUNK, 8)
    nk = seq // tc
    rows = tc * SUBLANES

    def body(u_ref, lam_ref, b_ref, c_ref, d_ref, w_ref, bias_ref, y_ref, hb_ref, buf, st):
        k = pl.program_id(1)

        @pl.when(k == 0)
        def _():
            st[...] = jnp.zeros_like(st)

        hb_ref[...] = st[...]
        u = u_ref[...]
        buf[...] = jnp.dot(u.astype(BF16), b_ref[0], preferred_element_type=F32)
        lr = jnp.broadcast_to(lam_ref[0, :, 0:s], (SUBLANES, s))
        li = jnp.broadcast_to(lam_ref[0, :, s:2 * s], (SUBLANES, s))

        def step(t, carry):
            hr, hi = carry
            r0 = pl.multiple_of(t * SUBLANES, SUBLANES)
            nr = lr * hr - li * hi + buf[pl.ds(r0, SUBLANES), 0:s]
            ni = lr * hi + li * hr + buf[pl.ds(r0, SUBLANES), s:2 * s]
            buf[pl.ds(r0, SUBLANES), 0:s] = nr
            buf[pl.ds(r0, SUBLANES), s:2 * s] = ni
            return nr, ni

        hr, hi = lax.fori_loop(0, tc, step, (st[:, 0:s], st[:, s:2 * s]), unroll=4)
        st[:, 0:s] = hr
        st[:, s:2 * s] = hi
        y = jnp.dot(buf[...].astype(BF16), c_ref[0], preferred_element_type=F32) + d_ref[0] * u
        z = jnp.dot(_gelu(y).astype(BF16), w_ref[0], preferred_element_type=F32) + bias_ref[0]
        y_ref[...] = z[:, 0:LANES] * _sigmoid(z[:, LANES:2 * LANES])

    blk = lambda shape: pl.BlockSpec(shape, lambda b, k: (b, 0, 0))
    (y8, hb), comm_outs = _call(
        body, name=name, grid=(nb, nk),
        in_specs=[pl.BlockSpec((rows, LANES), lambda b, k: (k, b)),
                  blk((1, 1, 2 * s)), blk((1, LANES, 2 * s)), blk((1, 2 * s, LANES)),
                  blk((1, 1, LANES)), blk((1, LANES, 2 * LANES)), blk((1, 1, 2 * LANES))],
        out_specs=[pl.BlockSpec((rows, LANES), lambda b, k: (k, b)),
                   pl.BlockSpec((SUBLANES, 2 * s), lambda b, k: (k, b))],
        out_shape=[jax.ShapeDtypeStruct((rows_total, w), F32),
                   jax.ShapeDtypeStruct((nk * SUBLANES, nb * 2 * s), F32)],
        scratch_shapes=[pltpu.VMEM((rows, 2 * s), F32), pltpu.VMEM((SUBLANES, 2 * s), F32)],
        semantics=("parallel", "arbitrary"), args=(u8, lam, b_bd, c_bd, dsk, w_bd, bias), comm=comm)
    return (y8, hb) if comm is None else (y8, hb, comm_outs)


def _ssm_bwd(u8, dy8, hb, ops, name, comm=None):
    lam, b_bd, c_bd, dsk, w_bd, bias = ops
    rows_total, w = u8.shape
    seq = rows_total // SUBLANES
    nb = w // LANES
    s = STATES_PER_BLOCK
    tc = _tile(seq, SSM_TIME_CHUNK, 8)
    nk = seq // tc
    rows = tc * SUBLANES
    tn_dims = (((0,), (0,)), ((), ()))
    nt_dims = (((1,), (1,)), ((), ()))

    def body(u_ref, dy_ref, hb_ref, lam_ref, b_ref, c_ref, d_ref, w_ref, bias_ref,
             du_ref, dlam_ref, db_ref, dct_ref, dd_ref, dw_ref, dbias_ref, hbuf, gbuf, gst, lacc):
        k = pl.program_id(1)

        @pl.when(k == 0)
        def _():
            gst[...] = jnp.zeros_like(gst)
            lacc[...] = jnp.zeros_like(lacc)
            db_ref[...] = jnp.zeros_like(db_ref)
            dct_ref[...] = jnp.zeros_like(dct_ref)
            dd_ref[...] = jnp.zeros_like(dd_ref)
            dw_ref[...] = jnp.zeros_like(dw_ref)
            dbias_ref[...] = jnp.zeros_like(dbias_ref)

        u = u_ref[...]
        ub = u.astype(BF16)
        lr = jnp.broadcast_to(lam_ref[0, :, 0:s], (SUBLANES, s))
        li = jnp.broadcast_to(lam_ref[0, :, s:2 * s], (SUBLANES, s))
        hbuf[...] = jnp.dot(ub, b_ref[0], preferred_element_type=F32)

        def fstep(t, carry):
            hr, hi = carry
            r0 = pl.multiple_of(t * SUBLANES, SUBLANES)
            nr = lr * hr - li * hi + hbuf[pl.ds(r0, SUBLANES), 0:s]
            ni = lr * hi + li * hr + hbuf[pl.ds(r0, SUBLANES), s:2 * s]
            hbuf[pl.ds(r0, SUBLANES), 0:s] = nr
            hbuf[pl.ds(r0, SUBLANES), s:2 * s] = ni
            return nr, ni

        lax.fori_loop(0, tc, fstep, (hb_ref[:, 0:s], hb_ref[:, s:2 * s]), unroll=4)
        hb16 = hbuf[...].astype(BF16)
        y = jnp.dot(hb16, c_ref[0], preferred_element_type=F32) + d_ref[0] * u
        yg, dyg_dy = _gelu_and_grad(y)
        yg16 = yg.astype(BF16)
        z = jnp.dot(yg16, w_ref[0], preferred_element_type=F32) + bias_ref[0]
        z1 = z[:, 0:LANES]
        sg = _sigmoid(z[:, LANES:2 * LANES])
        dout = dy_ref[...]
        dz = jnp.concatenate([dout * sg, dout * z1 * sg * (1.0 - sg)], axis=-1)
        dz16 = dz.astype(BF16)
        dw_ref[0] += lax.dot_general(yg16, dz16, tn_dims, preferred_element_type=F32)
        dbias_ref[0] += jnp.sum(dz, axis=0, keepdims=True)
        dy = lax.dot_general(dz16, w_ref[0], nt_dims, preferred_element_type=F32) * dyg_dy
        dy16 = dy.astype(BF16)
        dd_ref[0] += jnp.sum(dy * u, axis=0, keepdims=True)
        dct_ref[0] += lax.dot_general(dy16, hb16, tn_dims, preferred_element_type=F32)
        gbuf[...] = lax.dot_general(dy16, c_ref[0], nt_dims, preferred_element_type=F32)

        def bstep(i, carry):
            gr, gi, ar, ai = carry
            t = tc - 1 - i
            r0 = pl.multiple_of(t * SUBLANES, SUBLANES)
            ngr = gbuf[pl.ds(r0, SUBLANES), 0:s] + lr * gr + li * gi
            ngi = gbuf[pl.ds(r0, SUBLANES), s:2 * s] - li * gr + lr * gi
            gbuf[pl.ds(r0, SUBLANES), 0:s] = ngr
            gbuf[pl.ds(r0, SUBLANES), s:2 * s] = ngi
            p0 = pl.multiple_of(jnp.maximum(t - 1, 0) * SUBLANES, SUBLANES)
            first = t == 0
            hpr = jnp.where(first, hb_ref[:, 0:s], hbuf[pl.ds(p0, SUBLANES), 0:s])
            hpi = jnp.where(first, hb_ref[:, s:2 * s], hbuf[pl.ds(p0, SUBLANES), s:2 * s])
            return ngr, ngi, ar + ngr * hpr + ngi * hpi, ai - ngr * hpi + ngi * hpr

        gr, gi, ar, ai = lax.fori_loop(
            0, tc, bstep, (gst[:, 0:s], gst[:, s:2 * s], lacc[:, 0:s], lacc[:, s:2 * s]), unroll=2)
        gst[:, 0:s] = gr
        gst[:, s:2 * s] = gi
        lacc[:, 0:s] = ar
        lacc[:, s:2 * s] = ai
        g16 = gbuf[...].astype(BF16)
        du_ref[...] = dy * d_ref[0] + lax.dot_general(g16, b_ref[0], nt_dims, preferred_element_type=F32)
        db_ref[0] += lax.dot_general(ub, g16, tn_dims, preferred_element_type=F32)

        @pl.when(k == nk - 1)
        def _():
            dlam_ref[0] = jnp.sum(lacc[...], axis=0, keepdims=True)

    blk = lambda shape: pl.BlockSpec(shape, lambda b, k: (b, 0, 0))
    rev = lambda b, k: (nk - 1 - k, b)
    outs, comm_outs = _call(
        body, name=name, grid=(nb, nk),
        in_specs=[pl.BlockSpec((rows, LANES), rev), pl.BlockSpec((rows, LANES), rev),
                  pl.BlockSpec((SUBLANES, 2 * s), rev),
                  blk((1, 1, 2 * s)), blk((1, LANES, 2 * s)), blk((1, 2 * s, LANES)),
                  blk((1, 1, LANES)), blk((1, LANES, 2 * LANES)), blk((1, 1, 2 * LANES))],
        out_specs=[pl.BlockSpec((rows, LANES), rev),
                   blk((1, 1, 2 * s)), blk((1, LANES, 2 * s)), blk((1, LANES, 2 * s)),
                   blk((1, 1, LANES)), blk((1, LANES, 2 * LANES)), blk((1, 1, 2 * LANES))],
        out_shape=[jax.ShapeDtypeStruct((rows_total, w), F32),
                   jax.ShapeDtypeStruct((nb, 1, 2 * s), F32), jax.ShapeDtypeStruct((nb, LANES, 2 * s), F32),
                   jax.ShapeDtypeStruct((nb, LANES, 2 * s), F32), jax.ShapeDtypeStruct((nb, 1, LANES), F32),
                   jax.ShapeDtypeStruct((nb, LANES, 2 * LANES), F32), jax.ShapeDtypeStruct((nb, 1, 2 * LANES), F32)],
        scratch_shapes=[pltpu.VMEM((rows, 2 * s), F32), pltpu.VMEM((rows, 2 * s), F32),
                        pltpu.VMEM((SUBLANES, 2 * s), F32), pltpu.VMEM((SUBLANES, 2 * s), F32)],
        semantics=("parallel", "arbitrary"), args=(u8, dy8, hb, lam, b_bd, c_bd, dsk, w_bd, bias), comm=comm)
    return outs if comm is None else (outs, comm_outs)


def _to_scan_rows(a, nseq, seq):
    w = a.shape[-1]
    t = jnp.swapaxes(a.reshape(nseq, seq, w), 0, 1)
    t = jnp.pad(t, ((0, 0), (0, SUBLANES - nseq), (0, 0)))
    return t.reshape(seq * SUBLANES, w)


def _from_scan_rows(a8, nseq, seq):
    w = a8.shape[-1]
    t = a8.reshape(seq, SUBLANES, w)[:, :nseq]
    return jnp.swapaxes(t, 0, 1).reshape(nseq * seq, w)


ANY = pl.BlockSpec(memory_space=pl.ANY)

BIG = (("ffn1_w_in", True), ("ffn1_w_out", False), ("mix_w_in", True), ("mix_w_out", False),
       ("ffn2_w_in", True), ("ffn2_w_out", False))


def _my_place():
    return lax.axis_index("x"), lax.axis_index("y"), lax.axis_index("c")


def _other_chips(x, y):
    return [(1 - x, y), (x, 1 - y), (1 - x, 1 - y)]


def _half_of_shard(ref, col_sharded, chip, core):
    full_rows, full_cols = ref.shape
    if col_sharded:
        hr, cs = full_rows // 2, full_cols // N_CHIPS
        return ref.at[pl.ds(pl.multiple_of(core * hr, 8), hr), pl.ds(chip * cs, cs)]
    rs = full_rows // N_CHIPS
    return ref.at[pl.ds(pl.multiple_of(chip * rs + core * (rs // 2), 8), rs // 2), :]


def _gather_comm(shards, cols):
    full_shapes = [(sh.shape[0], sh.shape[1] * N_CHIPS) if col else (sh.shape[0] * N_CHIPS, sh.shape[1])
                   for sh, col in zip(shards, cols)]
    nw = len(shards)

    def first_copies(ins, outs, sems):
        send_sems, recv_sems, local_sems = sems
        x, y, c = _my_place()
        me = 2 * x + y
        locals_, sends = [], []
        for wi in range(nw):
            src, dst = ins[wi], outs[wi]
            rs, cs = src.shape
            hs = rs // 2
            if cols[wi]:
                place = dst.at[:, pl.ds(me * cs, cs)]
            else:
                place = dst.at[pl.ds(pl.multiple_of(me * rs, 8), rs), :]
            locals_.append(pltpu.make_async_copy(src, place, local_sems.at[wi]))
            my_half = src.at[pl.ds(pl.multiple_of(c * hs, 8), hs), :]
            for j, (px, py) in enumerate(_other_chips(x, y)):
                sends.append(pltpu.make_async_remote_copy(
                    src_ref=my_half, dst_ref=_half_of_shard(dst, cols[wi], me, c),
                    send_sem=send_sems.at[wi * 6 + j], recv_sem=recv_sems.at[wi * 6 + j],
                    device_id=(px, py, c), device_id_type=MESH))
        return locals_, sends

    def start(ins, outs, sems):
        locals_, sends = first_copies(ins, outs, sems)
        for cp in locals_ + sends:
            cp.start()

    def finish(ins, outs, sems):
        send_sems, recv_sems, _ = sems
        x, y, c = _my_place()
        chips = _other_chips(x, y)
        locals_, sends = first_copies(ins, outs, sems)
        fwds = []
        for wi in range(nw):
            dst = outs[wi]
            for j, (px, py) in enumerate(chips):
                got = _half_of_shard(dst, cols[wi], 2 * px + py, c)
                pltpu.make_async_remote_copy(
                    src_ref=got, dst_ref=got, send_sem=send_sems.at[wi * 6 + j], recv_sem=recv_sems.at[wi * 6 + j],
                    device_id=(px, py, c), device_id_type=MESH).wait_recv()
                cp = pltpu.make_async_remote_copy(
                    src_ref=got, dst_ref=got, send_sem=send_sems.at[wi * 6 + 3 + j], recv_sem=recv_sems.at[wi * 6 + 3 + j],
                    device_id=(x, y, 1 - c), device_id_type=MESH)
                cp.start()
                fwds.append(cp)
        for wi in range(nw):
            dst = outs[wi]
            for j, (px, py) in enumerate(chips):
                theirs = _half_of_shard(dst, cols[wi], 2 * px + py, 1 - c)
                pltpu.make_async_remote_copy(
                    src_ref=theirs, dst_ref=theirs, send_sem=send_sems.at[wi * 6 + 3 + j],
                    recv_sem=recv_sems.at[wi * 6 + 3 + j], device_id=(x, y, 1 - c), device_id_type=MESH).wait_recv()
        for cp in sends + fwds:
            cp.wait_send()
        for cp in locals_:
            cp.wait()

    return _Comm(shards, [jax.ShapeDtypeStruct(s, BF16) for s in full_shapes],
                 [pltpu.SemaphoreType.DMA((6 * nw,)), pltpu.SemaphoreType.DMA((6 * nw,)),
                  pltpu.SemaphoreType.DMA((nw,))], start, finish)


def _pair_exchange_comm(grads, cols):
    nw = len(grads)
    n_copies = sum(1 if col else N_CHIPS for col in cols)

    def copies(ins, outs, sems):
        send_sems, recv_sems = sems
        x, y, c = _my_place()
        out = []
        for wi in range(nw):
            src, dst = ins[wi], outs[wi]
            fr = src.shape[0]
            if cols[wi]:
                hr = fr // 2
                pieces = [(src.at[pl.ds(pl.multiple_of((1 - c) * hr, 8), hr), :], dst)]
            else:
                rs = fr // N_CHIPS
                hs = rs // 2
                pieces = [(src.at[pl.ds(pl.multiple_of(k * rs + (1 - c) * hs, 8), hs), :],
                           dst.at[pl.ds(k * hs, hs), :]) for k in range(N_CHIPS)]
            for s_ref, d_ref in pieces:
                out.append(pltpu.make_async_remote_copy(
                    src_ref=s_ref, dst_ref=d_ref, send_sem=send_sems.at[len(out)], recv_sem=recv_sems.at[len(out)],
                    device_id=(x, y, 1 - c), device_id_type=MESH))
        return out

    def start(ins, outs, sems):
        for cp in copies(ins, outs, sems):
            cp.start()

    def finish(ins, outs, sems):
        for cp in copies(ins, outs, sems):
            cp.wait()

    return _Comm(grads, [jax.ShapeDtypeStruct((g.shape[0] // 2, g.shape[1]), F32) for g in grads],
                 [pltpu.SemaphoreType.DMA((n_copies,)), pltpu.SemaphoreType.DMA((n_copies,))], start, finish)


def _pair_sum(grad, other, col, core, name):
    fr, fc = grad.shape
    pieces = 1 if col else N_CHIPS
    pr = fr // 2 // pieces
    gview = grad.reshape(pieces * 2, pr, fc)
    oview = other.reshape(pieces, pr, fc)
    tr = _tile(pr, 256, 16)

    def body(c_ref, g_ref, o_ref, out_ref):
        out_ref[...] = (g_ref[...] + o_ref[...]).astype(out_ref.dtype)

    out = pl.pallas_call(
        body, name=name,
        grid_spec=pltpu.PrefetchScalarGridSpec(
            num_scalar_prefetch=1, grid=(pieces, pr // tr),
            in_specs=[pl.BlockSpec((1, tr, fc), lambda p, i, cref: (p * 2 + cref[0], i, 0)),
                      pl.BlockSpec((1, tr, fc), lambda p, i, cref: (p, i, 0))],
            out_specs=pl.BlockSpec((1, tr, fc), lambda p, i, cref: (p, i, 0))),
        out_shape=jax.ShapeDtypeStruct((pieces, pr, fc), BF16),
        compiler_params=_params("parallel", "parallel"),
    )(core, gview, oview)
    return out.reshape(fr // 2, fc)


def _chip_exchange_comm(psums, cols):
    nw = len(psums)
    out_shapes = [(N_CHIPS, p.shape[0], p.shape[1] // N_CHIPS) if col else (N_CHIPS, p.shape[0] // N_CHIPS, p.shape[1])
                  for p, col in zip(psums, cols)]

    def copies(ins, outs, sems):
        send_sems, recv_sems, local_sems = sems
        x, y, c = _my_place()
        me = 2 * x + y
        out = []
        for wi in range(nw):
            src = ins[wi]
            mine = outs[wi].at[me]

            def piece(chip, src=src, col=cols[wi]):
                if col:
                    cs = src.shape[1] // N_CHIPS
                    return src.at[:, pl.ds(chip * cs, cs)]
                ps = src.shape[0] // N_CHIPS
                return src.at[pl.ds(pl.multiple_of(chip * ps, 8), ps), :]

            out.append(pltpu.make_async_copy(piece(me), mine, local_sems.at[wi]))
            for j, (px, py) in enumerate(_other_chips(x, y)):
                out.append(pltpu.make_async_remote_copy(
                    src_ref=piece(2 * px + py), dst_ref=mine,
                    send_sem=send_sems.at[wi * 3 + j], recv_sem=recv_sems.at[wi * 3 + j],
                    device_id=(px, py, c), device_id_type=MESH))
        return out

    def start(ins, outs, sems):
        for cp in copies(ins, outs, sems):
            cp.start()

    def finish(ins, outs, sems):
        for cp in copies(ins, outs, sems):
            cp.wait()

    return _Comm(psums, [jax.ShapeDtypeStruct(s, BF16) for s in out_shapes],
                 [pltpu.SemaphoreType.DMA((3 * nw,)), pltpu.SemaphoreType.DMA((3 * nw,)),
                  pltpu.SemaphoreType.DMA((nw,))], start, finish)


def _chip_sum(slots, core, layer, layers, into, name):
    _, hr, cs = slots.shape
    tr = _tile(hr, 256, 16)

    def body(c_ref, s_ref, *rest):
        out_ref = rest[-1]
        acc = s_ref[0].astype(F32)
        for i in range(1, N_CHIPS):
            acc = acc + s_ref[i].astype(F32)
        out_ref[0] = acc

    in_specs = [pl.BlockSpec((N_CHIPS, tr, cs), lambda i, cref: (0, i, 0))]
    args = [core, slots]
    aliases = {}
    if into is not None:
        in_specs.append(pl.BlockSpec(memory_space=pl.ANY))
        args.append(into.reshape(layers * 2, hr, cs))
        aliases = {2: 0}
    out = pl.pallas_call(
        body, name=name,
        grid_spec=pltpu.PrefetchScalarGridSpec(
            num_scalar_prefetch=1, grid=(hr // tr,), in_specs=in_specs,
            out_specs=pl.BlockSpec((1, tr, cs), lambda i, cref: (layer * 2 + cref[0], i, 0))),
        out_shape=jax.ShapeDtypeStruct((layers * 2, hr, cs), F32),
        input_output_aliases=aliases,
        compiler_params=_params("parallel"),
    )(*args)
    return out.reshape(layers, 2 * hr, cs)


def _pair_share_comm(reduced):
    nw = len(reduced)

    def copies(ins, outs, sems):
        send_sems, recv_sems = sems
        x, y, c = _my_place()
        out = []
        for wi in range(nw):
            hs = outs[wi].shape[1] // 2
            mine = outs[wi].at[:, pl.ds(pl.multiple_of(c * hs, 8), hs), :]
            out.append(pltpu.make_async_remote_copy(
                src_ref=mine, dst_ref=mine, send_sem=send_sems.at[wi], recv_sem=recv_sems.at[wi],
                device_id=(x, y, 1 - c), device_id_type=MESH))
        return out

    def start(ins, outs, sems):
        for cp in copies(ins, outs, sems):
            cp.start()

    def finish(ins, outs, sems):
        for cp in copies(ins, outs, sems):
            cp.wait()

    return _Comm(reduced, [jax.ShapeDtypeStruct(r.shape, F32) for r in reduced],
                 [pltpu.SemaphoreType.DMA((nw,)), pltpu.SemaphoreType.DMA((nw,))], start, finish,
                 alias={i: i for i in range(nw)})


def _all_reduce_small(flat):
    rows, lanes = flat.shape
    seg = rows // N_DEV

    def body(in_ref, out_ref, recv_ref, send_sems, recv_sems):
        x, y, c = _my_place()
        me = 4 * x + 2 * y + c

        def peer(r):
            fx, fy, fc = (r >> 2) & 1, (r >> 1) & 1, r & 1
            px = jnp.where(fx == 1, 1 - x, x)
            py = jnp.where(fy == 1, 1 - y, y)
            pc = jnp.where(fc == 1, 1 - c, c)
            return px, py, pc

        first = []
        for r in range(1, N_DEV):
            px, py, pc = peer(r)
            theirs = in_ref.at[pl.ds(pl.multiple_of((4 * px + 2 * py + pc) * seg, 8), seg), :]
            cp = pltpu.make_async_remote_copy(
                src_ref=theirs, dst_ref=recv_ref.at[r], send_sem=send_sems.at[r - 1], recv_sem=recv_sems.at[r - 1],
                device_id=(px, py, pc), device_id_type=MESH)
            cp.start()
            first.append(cp)
        for cp in first:
            cp.wait()
        my_rows = pl.ds(pl.multiple_of(me * seg, 8), seg)
        acc = in_ref[my_rows, :]
        for r in range(1, N_DEV):
            acc = acc + recv_ref[r]
        out_ref[my_rows, :] = acc
        second = []
        for r in range(1, N_DEV):
            px, py, pc = peer(r)
            cp = pltpu.make_async_remote_copy(
                src_ref=out_ref.at[my_rows, :], dst_ref=out_ref.at[my_rows, :],
                send_sem=send_sems.at[6 + r], recv_sem=recv_sems.at[6 + r],
                device_id=(px, py, pc), device_id_type=MESH)
            cp.start()
            second.append(cp)
        for r in range(1, N_DEV):
            px, py, pc = peer(r)
            theirs = out_ref.at[pl.ds(pl.multiple_of((4 * px + 2 * py + pc) * seg, 8), seg), :]
            pltpu.make_async_remote_copy(
                src_ref=theirs, dst_ref=theirs, send_sem=send_sems.at[6 + r], recv_sem=recv_sems.at[6 + r],
                device_id=(px, py, pc), device_id_type=MESH).wait_recv()
        for cp in second:
            cp.wait_send()

    vm = pl.BlockSpec(memory_space=pltpu.VMEM)
    return pl.pallas_call(
        body, name="all_reduce_small",
        in_specs=[vm], out_specs=vm,
        out_shape=jax.ShapeDtypeStruct((rows, lanes), F32),
        scratch_shapes=[pltpu.VMEM((N_DEV, seg, lanes), F32),
                        pltpu.SemaphoreType.DMA((2 * (N_DEV - 1),)), pltpu.SemaphoreType.DMA((2 * (N_DEV - 1),))],
        compiler_params=pltpu.CompilerParams(vmem_limit_bytes=VMEM_LIMIT),
    )(flat)


def _adamw(w, g, m, v, name):
    rows, cols = w.shape
    tr = _tile(rows, 256, 8)
    c1 = 1.0 - ADAM_B1 ** ADAM_STEP
    c2 = 1.0 - ADAM_B2 ** ADAM_STEP

    def body(w_ref, g_ref, m_ref, v_ref, d_ref, nm_ref, nv_ref):
        gv = g_ref[...]
        nm = ADAM_B1 * m_ref[...] + (1.0 - ADAM_B1) * gv
        nv = ADAM_B2 * v_ref[...] + (1.0 - ADAM_B2) * (gv * gv)
        d_ref[...] = -ADAM_LR * ((nm / c1) / (jnp.sqrt(nv / c2) + ADAM_EPS) + ADAM_WD * w_ref[...])
        nm_ref[...] = nm
        nv_ref[...] = nv

    blk = pl.BlockSpec((tr, cols), lambda i: (i, 0))
    sds = jax.ShapeDtypeStruct((rows, cols), F32)
    return pl.pallas_call(
        body, name=name, grid=(rows // tr,),
        in_specs=[blk] * 4, out_specs=[blk] * 3, out_shape=[sds] * 3,
        compiler_params=_params("parallel"),
    )(w, g, m, v)


SMALL = ("norm_ffn1", "norm_mix", "ssm_a_re", "ssm_a_im", "ssm_log_dt", "ssm_b_re", "ssm_b_im", "ssm_c_re",
         "ssm_c_im", "ssm_d", "ssm_glu_w", "ssm_glu_b", "gm_v_gain", "gm_w_s", "gm_b_s", "gain_ssm_out",
         "gain_gm_out", "norm_ffn2", "norm_final")
WEIGHTS = ("norm_ffn1", "ffn1_w_in", "ffn1_w_out", "norm_mix", "mix_w_in", "ssm_a_re", "ssm_a_im", "ssm_log_dt",
           "ssm_b_re", "ssm_b_im", "ssm_c_re", "ssm_c_im", "ssm_d", "ssm_glu_w", "ssm_glu_b", "gm_v_gain", "gm_w_s",
           "gm_b_s", "gain_ssm_out", "gain_gm_out", "mix_w_out", "norm_ffn2", "ffn2_w_in", "ffn2_w_out", "norm_final")


def _ffn_fwd(x, gain, w_in, w_out, tag, hosted=None):
    if hosted is None:
        h, g, u, a = _ffn_in_fwd(x, gain, w_in, f"{tag}_in")
    else:
        (h, g, u, a), got = _ffn_in_fwd(x, gain, w_in, f"{tag}_in_hosting", comm=hosted[0]())
        hosted[1](got)
    out = _matmul(a, w_out, "nn", scale=0.5, res=x, tm=512, tn=1024, tk=4096, name=f"{tag}_out")
    return out, (x, h, g, u, a)


def _ffn_bwd(dout, saved, gain, w_in, w_out, tag, first=None, last=None):
    x, h, g, u, a = saved
    f = g.shape[1]
    if first is None:
        dg, du = _ffn_out_bwd(dout, w_out, g, u, f"{tag}_out_dx")
    else:
        (dg, du), got = _ffn_out_bwd(dout, w_out, g, u, f"{tag}_out_dx_hosting", comm=first[0]())
        first[1](got)
    dw_out = _matmul(a, dout, "tn", scale=0.5, tm=1536, tn=1024, tk=2048, name=f"{tag}_out_dw")
    dw_in = _matmul(h, dg, "tn", tm=1024, tn=1536, tk=2048, name=f"{tag}_in_dw_g", out_cols=2 * f)
    dw_in = _matmul(h, du, "tn", tm=1024, tn=1536, tk=2048, name=f"{tag}_in_dw_u", out_cols=2 * f, col_off=f,
                    into=dw_in)
    if last is None:
        dx, dgain = _proj_in_bwd([(dg, 0), (du, f)], w_in, x, gain, dout, f"{tag}_in_dx")
    else:
        dx, dgain, got = _proj_in_bwd([(dg, 0), (du, f)], w_in, x, gain, dout, f"{tag}_in_dx_hosting", comm=last[0]())
        last[1](got)
    return dx, dgain, dw_in, dw_out


def kernel(x, norm_ffn1, ffn1_w_in, ffn1_w_out, norm_mix, mix_w_in, ssm_a_re, ssm_a_im, ssm_log_dt, ssm_b_re, ssm_b_im, ssm_c_re, ssm_c_im, ssm_d, ssm_glu_w, ssm_glu_b, gm_v_gain, gm_w_s, gm_b_s, gain_ssm_out, gain_gm_out, mix_w_out, norm_ffn2, ffn2_w_in, ffn2_w_out, norm_final, loss_target, m_norm_ffn1, m_ffn1_w_in, m_ffn1_w_out, m_norm_mix, m_mix_w_in, m_ssm_a_re, m_ssm_a_im, m_ssm_log_dt, m_ssm_b_re, m_ssm_b_im, m_ssm_c_re, m_ssm_c_im, m_ssm_d, m_ssm_glu_w, m_ssm_glu_b, m_gm_v_gain, m_gm_w_s, m_gm_b_s, m_gain_ssm_out, m_gain_gm_out, m_mix_w_out, m_norm_ffn2, m_ffn2_w_in, m_ffn2_w_out, m_norm_final, v_norm_ffn1, v_ffn1_w_in, v_ffn1_w_out, v_norm_mix, v_mix_w_in, v_ssm_a_re, v_ssm_a_im, v_ssm_log_dt, v_ssm_b_re, v_ssm_b_im, v_ssm_c_re, v_ssm_c_im, v_ssm_d, v_ssm_glu_w, v_ssm_glu_b, v_gm_v_gain, v_gm_w_s, v_gm_b_s, v_gain_ssm_out, v_gain_gm_out, v_mix_w_out, v_norm_ffn2, v_ffn2_w_in, v_ffn2_w_out, v_norm_final):
    wts = dict(norm_ffn1=norm_ffn1, ffn1_w_in=ffn1_w_in, ffn1_w_out=ffn1_w_out, norm_mix=norm_mix, mix_w_in=mix_w_in,
               ssm_a_re=ssm_a_re, ssm_a_im=ssm_a_im, ssm_log_dt=ssm_log_dt, ssm_b_re=ssm_b_re, ssm_b_im=ssm_b_im,
               ssm_c_re=ssm_c_re, ssm_c_im=ssm_c_im, ssm_d=ssm_d, ssm_glu_w=ssm_glu_w, ssm_glu_b=ssm_glu_b,
               gm_v_gain=gm_v_gain, gm_w_s=gm_w_s, gm_b_s=gm_b_s, gain_ssm_out=gain_ssm_out, gain_gm_out=gain_gm_out,
               mix_w_out=mix_w_out, norm_ffn2=norm_ffn2, ffn2_w_in=ffn2_w_in, ffn2_w_out=ffn2_w_out,
               norm_final=norm_final)
    mom = dict(norm_ffn1=m_norm_ffn1, ffn1_w_in=m_ffn1_w_in, ffn1_w_out=m_ffn1_w_out, norm_mix=m_norm_mix,
               mix_w_in=m_mix_w_in, ssm_a_re=m_ssm_a_re, ssm_a_im=m_ssm_a_im, ssm_log_dt=m_ssm_log_dt,
               ssm_b_re=m_ssm_b_re, ssm_b_im=m_ssm_b_im, ssm_c_re=m_ssm_c_re, ssm_c_im=m_ssm_c_im, ssm_d=m_ssm_d,
               ssm_glu_w=m_ssm_glu_w, ssm_glu_b=m_ssm_glu_b, gm_v_gain=m_gm_v_gain, gm_w_s=m_gm_w_s, gm_b_s=m_gm_b_s,
               gain_ssm_out=m_gain_ssm_out, gain_gm_out=m_gain_gm_out, mix_w_out=m_mix_w_out, norm_ffn2=m_norm_ffn2,
               ffn2_w_in=m_ffn2_w_in, ffn2_w_out=m_ffn2_w_out, norm_final=m_norm_final)
    var = dict(norm_ffn1=v_norm_ffn1, ffn1_w_in=v_ffn1_w_in, ffn1_w_out=v_ffn1_w_out, norm_mix=v_norm_mix,
               mix_w_in=v_mix_w_in, ssm_a_re=v_ssm_a_re, ssm_a_im=v_ssm_a_im, ssm_log_dt=v_ssm_log_dt,
               ssm_b_re=v_ssm_b_re, ssm_b_im=v_ssm_b_im, ssm_c_re=v_ssm_c_re, ssm_c_im=v_ssm_c_im, ssm_d=v_ssm_d,
               ssm_glu_w=v_ssm_glu_w, ssm_glu_b=v_ssm_glu_b, gm_v_gain=v_gm_v_gain, gm_w_s=v_gm_w_s, gm_b_s=v_gm_b_s,
               gain_ssm_out=v_gain_ssm_out, gain_gm_out=v_gain_gm_out, mix_w_out=v_mix_w_out, norm_ffn2=v_norm_ffn2,
               ffn2_w_in=v_ffn2_w_in, ffn2_w_out=v_ffn2_w_out, norm_final=v_norm_final)

    nseq, seq, d = x.shape
    n = nseq * seq
    depth = norm_ffn1.shape[0]
    width = gain_ssm_out.shape[1]
    groups = ssm_a_re.shape[1]
    heads = gm_w_s.shape[1]
    core = lax.axis_index("c").astype(jnp.int32).reshape(1)

    is_col = dict(BIG)
    full = {name: [None] * depth for name, _ in BIG}

    def gather_comm(pairs):
        return _gather_comm([wts[nm][l].astype(BF16) for nm, l in pairs], [is_col[nm] for nm, _ in pairs])

    def store(pairs, arrays):
        for (nm, l), w in zip(pairs, arrays):
            full[nm][l] = w

    pairs = [("ffn1_w_in", 0), ("ffn1_w_out", 0)]
    store(pairs, _run_comm(gather_comm(pairs), "all_gather_first"))

    xs = x.reshape(n, d)
    saved = []
    for l in range(depth):
        pairs = [(nm, l) for nm in ("mix_w_in", "mix_w_out", "ffn2_w_in", "ffn2_w_out")]
        x1, s_ffn1 = _ffn_fwd(xs, norm_ffn1[l], full["ffn1_w_in"][l], full["ffn1_w_out"][l], "ffn1",
                              hosted=(functools.partial(gather_comm, pairs), functools.partial(store, pairs)))
        hm, u_ssm, zgm = _mix_in_fwd(x1, norm_mix[l], full["mix_w_in"][l], width, "mix_in")
        bt_re = jnp.swapaxes(ssm_b_re[l], 1, 2)
        bt_im = jnp.swapaxes(ssm_b_im[l], 1, 2)
        disc_in = (ssm_a_re[l], ssm_a_im[l], ssm_log_dt[l].reshape(groups, 1), bt_re, bt_im)
        lr, li, bbr, bbi = _disc_fwd(*disc_in)
        ops = _ssm_operands(lr, li, bbr, bbi, ssm_c_re[l], ssm_c_im[l], ssm_d[l], ssm_glu_w[l], ssm_glu_b[l])
        u8 = _to_scan_rows(u_ssm, nseq, seq)
        if l + 1 < depth:
            pairs = [(nm, l + 1) for nm in ("ffn1_w_in", "ffn1_w_out", "mix_w_in", "mix_w_out")]
            y8, hb, got = _ssm_fwd(u8, ops, "s5_fwd_hosting", comm=gather_comm(pairs))
            store(pairs, got)
        else:
            y8, hb = _ssm_fwd(u8, ops, "s5_fwd")
        y_ssm = _from_scan_rows(y8, nseq, seq)
        bias_tile = jnp.broadcast_to(gm_b_s[l].T[:, :, None], (GM_CHUNK, heads, GM_HEAD_DIM)).reshape(GM_CHUNK, width)
        y_gm = _gmlp_fwd(zgm, gm_v_gain[l], gm_w_s[l], bias_tile, "gmlp_fwd")
        ycat = _mixnorm_fwd(y_ssm, y_gm, gain_ssm_out[l], gain_gm_out[l], "mix_out_norm")
        x2 = _matmul(ycat, full["mix_w_out"][l], "nn", res=x1, tm=512, tn=1024, tk=1024, name="mix_out")
        hosted = None
        if l + 1 < depth:
            pairs = [(nm, l + 1) for nm in ("ffn2_w_in", "ffn2_w_out")]
            hosted = (functools.partial(gather_comm, pairs), functools.partial(store, pairs))
        x3, s_ffn2 = _ffn_fwd(x2, norm_ffn2[l], full["ffn2_w_in"][l], full["ffn2_w_out"][l], "ffn2", hosted=hosted)
        saved.append(dict(ffn1=s_ffn1, x1=x1, hm=hm, zgm=zgm, disc_in=disc_in, ops=ops, u8=u8, hb=hb, y_ssm=y_ssm,
                          bias_tile=bias_tile, y_gm=y_gm, ycat=ycat, ffn2=s_ffn2))
        xs = x3

    dx, g_norm_final, loss_part = _loss_head(xs, norm_final, loss_target.reshape(n, d))
    big = {name: [None] * depth for name, _ in BIG}
    small = {name: [None] * depth for name in SMALL if name != "norm_final"}
    gpb = GROUPS_PER_BLOCK
    s_blk = STATES_PER_BLOCK
    psum_of, reduced = {}, {}

    def swap_comm(pairs):
        return _pair_exchange_comm([big[nm][l] for nm, l in pairs], [is_col[nm] for nm, _ in pairs])

    def take_swapped(pairs, others):
        for (nm, l), other in zip(pairs, others):
            psum_of[nm, l] = _pair_sum(big[nm][l], other, is_col[nm], core, f"grad_pair_sum_{nm}")

    def send_comm(pairs):
        return _chip_exchange_comm([psum_of[p] for p in pairs], [is_col[nm] for nm, _ in pairs])

    def take_sent(pairs, slots):
        for (nm, l), s in zip(pairs, slots):
            reduced[nm] = _chip_sum(s, core, l, depth, reduced.get(nm), f"grad_chip_sum_{nm}")

    def hosting(make, take, pairs):
        return functools.partial(make, pairs), functools.partial(take, pairs)

    for l in reversed(range(depth)):
        sv = saved[l]
        above = [(nm, l + 1) for nm in ("mix_w_in", "mix_w_out", "ffn1_w_in", "ffn1_w_out")] if l + 1 < depth else []
        dx, small["norm_ffn2"][l], big["ffn2_w_in"][l], big["ffn2_w_out"][l] = _ffn_bwd(
            dx, sv["ffn2"], norm_ffn2[l], full["ffn2_w_in"][l], full["ffn2_w_out"][l], "ffn2",
            first=hosting(swap_comm, take_swapped, above) if above else None)
        mine = [("ffn2_w_in", l), ("ffn2_w_out", l)]
        dycat, got = _matmul(dx, full["mix_w_out"][l], "nt", tm=512, tn=1024, tk=1024, name="mix_out_dx",
                             comm=swap_comm(mine))
        take_swapped(mine, got)
        big["mix_w_out"][l] = _matmul(sv["ycat"], dx, "tn", tm=1024, tn=1024, tk=2048, name="mix_out_dw")
        dy_ssm, dy_gm, small["gain_ssm_out"][l], small["gain_gm_out"][l] = _mixnorm_bwd(
            sv["y_ssm"], sv["y_gm"], gain_ssm_out[l], gain_gm_out[l], dycat, "mix_out_norm_bwd")
        dzgm, small["gm_w_s"][l], dbias_tile, small["gm_v_gain"][l] = _gmlp_bwd(
            sv["zgm"], dy_gm, gm_v_gain[l], gm_w_s[l], sv["bias_tile"], "gmlp_bwd")
        small["gm_b_s"][l] = dbias_tile.reshape(GM_CHUNK, heads, GM_HEAD_DIM).sum(-1).T
        dy8 = _to_scan_rows(dy_ssm, nseq, seq)
        (du8, dlam, db_bd, dct_bd, dd, dw_bd, dbias), got = _ssm_bwd(
            sv["u8"], dy8, sv["hb"], sv["ops"], "s5_bwd", comm=send_comm(mine + above))
        take_sent(mine + above, got)
        du_ssm = _from_scan_rows(du8, nseq, seq)
        dlr = dlam[:, 0, :s_blk].reshape(groups, SSM_STATE)
        dli = dlam[:, 0, s_blk:].reshape(groups, SSM_STATE)
        dbbr = _block_diag_extract(db_bd[:, :, :s_blk], gpb)
        dbbi = _block_diag_extract(db_bd[:, :, s_blk:], gpb)
        da_re, da_im, dldt, dbt_re, dbt_im = _disc_bwd(*sv["disc_in"], dlr, dli, dbbr, dbbi)
        small["ssm_a_re"][l], small["ssm_a_im"][l], small["ssm_log_dt"][l] = da_re, da_im, dldt.reshape(groups)
        small["ssm_b_re"][l] = jnp.swapaxes(dbt_re, 1, 2)
        small["ssm_b_im"][l] = jnp.swapaxes(dbt_im, 1, 2)
        small["ssm_c_re"][l] = _block_diag_extract(dct_bd[:, :, :s_blk], gpb)
        small["ssm_c_im"][l] = -_block_diag_extract(dct_bd[:, :, s_blk:], gpb)
        small["ssm_d"][l] = dd.reshape(groups, SSM_CH)
        small["ssm_glu_w"][l] = jnp.concatenate(
            [_block_diag_extract(dw_bd[:, :, :LANES], gpb), _block_diag_extract(dw_bd[:, :, LANES:], gpb)], axis=-1)
        small["ssm_glu_b"][l] = jnp.concatenate(
            [dbias[:, 0, :LANES].reshape(groups, SSM_CH), dbias[:, 0, LANES:].reshape(groups, SSM_CH)], axis=-1)
        cols_mi = 3 * width
        dw_mi = _matmul(sv["hm"], du_ssm, "tn", tm=1024, tn=width, tk=2048, name="mix_in_dw_ssm", out_cols=cols_mi)
        big["mix_w_in"][l] = _matmul(sv["hm"], dzgm, "tn", tm=1024, tn=width, tk=2048, name="mix_in_dw_gm",
                                     out_cols=cols_mi, col_off=width, into=dw_mi)
        dx, small["norm_mix"][l] = _proj_in_bwd([(du_ssm, 0), (dzgm, width)], full["mix_w_in"][l], sv["x1"],
                                                norm_mix[l], dx, "mix_in_dx")
        mix = [("mix_w_in", l), ("mix_w_out", l)]
        dx, small["norm_ffn1"][l], big["ffn1_w_in"][l], big["ffn1_w_out"][l] = _ffn_bwd(
            dx, sv["ffn1"], norm_ffn1[l], full["ffn1_w_in"][l], full["ffn1_w_out"][l], "ffn1",
            first=hosting(swap_comm, take_swapped, mix) if l == 0 else None,
            last=hosting(send_comm, take_sent, mix) if l == 0 else None)
    grad_x = dx.reshape(nseq, seq, d)

    tail = [("ffn1_w_in", 0), ("ffn1_w_out", 0)]
    take_swapped(tail, _run_comm(swap_comm(tail), "grad_pair_exchange_tail"))
    take_sent(tail, _run_comm(send_comm(tail), "grad_chip_exchange_tail"))
    names = [name for name, _ in BIG]
    grads = dict(zip(names, _run_comm(_pair_share_comm([reduced[nm] for nm in names]), "grad_pair_share")))

    pieces = [jnp.stack(small[name]).reshape(-1) for name in SMALL if name != "norm_final"]
    pieces += [g_norm_final.reshape(-1), loss_part.reshape(1)]
    sizes = [p.shape[0] for p in pieces]
    total = sum(sizes)
    rows = -(-total // (LANES * N_DEV * SUBLANES)) * N_DEV * SUBLANES
    pad = rows * LANES - total

    def pack(parts, fill):
        return jnp.concatenate(parts + [jnp.full((pad,), fill, F32)]).reshape(rows, LANES)

    flat_g = _all_reduce_small(pack(pieces, 0.0))
    loss = flat_g.reshape(-1)[total - 1]

    delta, new_m, new_v = {}, {}, {}
    for name, _ in BIG:
        shape = wts[name].shape
        two_d = lambda a: a.reshape(shape[0] * shape[1], shape[2])
        dl, nm, nv = _adamw(two_d(wts[name]), two_d(grads[name]), two_d(mom[name]), two_d(var[name]), f"adamw_{name}")
        delta[name], new_m[name], new_v[name] = dl.reshape(shape), nm.reshape(shape), nv.reshape(shape)
    one = [jnp.zeros((1,), F32)]
    dl, nm, nv = _adamw(pack([wts[k].reshape(-1) for k in SMALL] + one, 0.0), flat_g,
                        pack([mom[k].reshape(-1) for k in SMALL] + one, 0.0),
                        pack([var[k].reshape(-1) for k in SMALL] + one, 1.0), "adamw_small")
    offs = 0
    for name, size in zip(SMALL, sizes[:-1]):
        shape = wts[name].shape
        grads[name] = flat_g.reshape(-1)[offs:offs + size].reshape(shape)
        delta[name] = dl.reshape(-1)[offs:offs + size].reshape(shape)
        new_m[name] = nm.reshape(-1)[offs:offs + size].reshape(shape)
        new_v[name] = nv.reshape(-1)[offs:offs + size].reshape(shape)
        offs += size

    return (loss, grad_x, *[grads[k] for k in WEIGHTS], *[delta[k] for k in WEIGHTS],
            *[new_m[k] for k in WEIGHTS], *[new_v[k] for k in WEIGHTS])
```

```python
import functools
import math

import jax
import jax.numpy as jnp
from jax import lax
from jax.experimental import pallas as pl
from jax.experimental.pallas import tpu as pltpu

F32 = jnp.float32
BF16 = jnp.bfloat16
MESH = pl.DeviceIdType.MESH

EPS = 1e-6
SSM_CH = 16
SSM_STATE = 64
GM_CHUNK = 128
GM_HEAD_DIM = 128
SUBLANES = 8
LANES = 128
GROUPS_PER_BLOCK = LANES // SSM_CH
STATES_PER_BLOCK = GROUPS_PER_BLOCK * SSM_STATE
SSM_TIME_CHUNK = 128
N_CHIPS = 4
N_DEV = 8

ADAM_LR = 0.001
ADAM_B1 = 0.9
ADAM_B2 = 0.999
ADAM_EPS = 1e-08
ADAM_WD = 0.01
ADAM_STEP = 10

VMEM_LIMIT = 56 * 1024 * 1024


def _tile(dim, pref, align):
    best = None
    t = align
    while t <= min(dim, pref):
        if dim % t == 0:
            best = t
        t += align
    return best if best is not None else dim


def _params(*sem):
    return pltpu.CompilerParams(dimension_semantics=sem, vmem_limit_bytes=VMEM_LIMIT)


def _gelu(x):
    c = math.sqrt(2.0 / math.pi)
    return 0.5 * x * (1.0 + jnp.tanh(c * (x + 0.044715 * x * x * x)))


def _gelu_and_grad(x):
    c = math.sqrt(2.0 / math.pi)
    t = jnp.tanh(c * (x + 0.044715 * x * x * x))
    g = 0.5 * x * (1.0 + t)
    dg = 0.5 * (1.0 + t) + 0.5 * x * (1.0 - t * t) * c * (1.0 + 3.0 * 0.044715 * x * x)
    return g, dg


def _sigmoid(x):
    return 0.5 * jnp.tanh(0.5 * x) + 0.5


def _matmul(a, b, mode, *, out_dtype=F32, scale=1.0, res=None, tm=512, tn=1024, tk=1024, name="mm",
            out_cols=None, col_off=0, into=None, comm=None):
    if mode == "nn":
        (m, k), (k2, n) = a.shape, b.shape
    elif mode == "nt":
        (m, k), (n, k2) = a.shape, b.shape
    else:
        (k, m), (k2, n) = a.shape, b.shape
    assert k == k2, (a.shape, b.shape, mode)
    tm = _tile(m, tm, 16 if mode != "tn" else LANES)
    tn = _tile(n, tn, LANES)
    tk = _tile(k, tk, LANES if mode != "tn" else 16)
    nk = k // tk
    grid = (m // tm, n // tn, nk)
    if mode == "nn":
        a_spec = pl.BlockSpec((tm, tk), lambda i, j, kk: (i, kk))
        b_spec = pl.BlockSpec((tk, tn), lambda i, j, kk: (kk, j))
        dims = (((1,), (0,)), ((), ()))
    elif mode == "nt":
        a_spec = pl.BlockSpec((tm, tk), lambda i, j, kk: (i, kk))
        b_spec = pl.BlockSpec((tn, tk), lambda i, j, kk: (j, kk))
        dims = (((1,), (1,)), ((), ()))
    else:
        a_spec = pl.BlockSpec((tk, tm), lambda i, j, kk: (kk, i))
        b_spec = pl.BlockSpec((tk, tn), lambda i, j, kk: (kk, j))
        dims = (((0,), (0,)), ((), ()))
    assert col_off % tn == 0
    off = col_off // tn
    r_spec = pl.BlockSpec((tm, tn), lambda i, j, kk: (i, j))
    o_spec = pl.BlockSpec((tm, tn), lambda i, j, kk: (i, j + off))
    has_res = res is not None
    has_into = into is not None

    def body(*refs):
        refs = list(refs)
        a_ref, b_ref = refs[:2]
        pos = 2
        r_ref = None
        if has_res:
            r_ref = refs[pos]
            pos += 1
        if has_into:
            pos += 1
        o_ref = refs[pos]
        acc_ref = refs[pos + 1] if nk > 1 else None
        part = lax.dot_general(a_ref[...].astype(BF16), b_ref[...].astype(BF16), dims,
                               preferred_element_type=F32)

        def finish(r):
            if scale != 1.0:
                r = r * scale
            if has_res:
                r = r + r_ref[...].astype(F32)
            o_ref[...] = r.astype(o_ref.dtype)

        if nk == 1:
            finish(part)
        else:
            kk = pl.program_id(2)

            @pl.when(kk == 0)
            def _():
                acc_ref[...] = part

            @pl.when(kk > 0)
            def _():
                acc_ref[...] += part

            @pl.when(kk == nk - 1)
            def _():
                finish(acc_ref[...])

    in_specs = [a_spec, b_spec]
    args = [a, b]
    if has_res:
        in_specs.append(r_spec)
        args.append(res)
    aliases = {}
    if has_into:
        in_specs.append(pl.BlockSpec(memory_space=pl.ANY))
        args.append(into)
        aliases = {len(args) - 1: 0}
    (out,), comm_outs = _call(
        body, name=name, grid=grid, in_specs=in_specs, out_specs=[o_spec],
        out_shape=[jax.ShapeDtypeStruct((m, n if out_cols is None else out_cols), out_dtype)],
        scratch_shapes=[pltpu.VMEM((tm, tn), F32)] if nk > 1 else [],
        aliases=aliases, semantics=("parallel", "parallel", "arbitrary"), args=args, comm=comm)
    return out if comm is None else (out, comm_outs)


class _Comm:
    def __init__(self, ins, outs, sems, start, finish, alias=None):
        self.ins, self.outs, self.sems, self.start, self.finish = list(ins), list(outs), list(sems), start, finish
        self.alias = dict(alias or {})


def _call(body, *, name, grid, in_specs, out_specs, out_shape, args, scratch_shapes=(), semantics=(), aliases=None,
          comm=None):
    in_specs, out_specs, out_shape = list(in_specs), list(out_specs), list(out_shape)
    scratch_shapes = list(scratch_shapes)
    aliases = dict(aliases or {})
    if comm is None:
        outs = pl.pallas_call(
            body, name=name, grid=grid, in_specs=in_specs, out_specs=out_specs, out_shape=out_shape,
            scratch_shapes=scratch_shapes, input_output_aliases=aliases, compiler_params=_params(*semantics),
        )(*args)
        return list(outs), []
    n_in, n_out, n_scr = len(in_specs), len(out_specs), len(scratch_shapes)
    c_in, c_out = len(comm.ins), len(comm.outs)
    for ci, co in comm.alias.items():
        aliases[n_in + ci] = n_out + co

    def hosted(*refs):
        refs = list(refs)
        ins, cins = refs[:n_in], refs[n_in:n_in + c_in]
        p = n_in + c_in
        outs, couts = refs[p:p + n_out], refs[p + n_out:p + n_out + c_out]
        p += n_out + c_out
        scr, sems = refs[p:p + n_scr], refs[p + n_scr:]
        ids = [pl.program_id(a) for a in range(len(grid))]
        first = functools.reduce(jnp.logical_and, [i == 0 for i in ids])
        last = functools.reduce(jnp.logical_and, [i == g - 1 for i, g in zip(ids, grid)])

        @pl.when(first)
        def _():
            comm.start(cins, couts, sems)

        body(*ins, *outs, *scr)

        @pl.when(last)
        def _():
            comm.finish(cins, couts, sems)

    any_spec = pl.BlockSpec(memory_space=pl.ANY)
    outs = pl.pallas_call(
        hosted, name=name, grid=grid, in_specs=in_specs + [any_spec] * c_in, out_specs=out_specs + [any_spec] * c_out,
        out_shape=out_shape + comm.outs, scratch_shapes=scratch_shapes + comm.sems, input_output_aliases=aliases,
        compiler_params=_params(*(["arbitrary"] * len(grid))),
    )(*args, *comm.ins)
    return list(outs[:n_out]), list(outs[n_out:])


def _run_comm(comm, name):
    c_in, c_out = len(comm.ins), len(comm.outs)

    def body(*refs):
        refs = list(refs)
        cins, couts, sems = refs[:c_in], refs[c_in:c_in + c_out], refs[c_in + c_out:]
        comm.start(cins, couts, sems)
        comm.finish(cins, couts, sems)

    any_spec = pl.BlockSpec(memory_space=pl.ANY)
    return list(pl.pallas_call(
        body, name=name, in_specs=[any_spec] * c_in, out_specs=[any_spec] * c_out, out_shape=comm.outs,
        scratch_shapes=comm.sems, input_output_aliases=comm.alias,
    )(*comm.ins))


def _loss_head(x, gain, target):
    n, d = x.shape
    tm = _tile(n, 512, 8)
    steps = n // tm

    def body(x_ref, g_ref, t_ref, dx_ref, dg_ref, loss_ref, acc_ref, lacc_ref):
        i = pl.program_id(0)
        xv = x_ref[...]
        g = g_ref[...]
        r = lax.rsqrt(jnp.mean(xv * xv, axis=-1, keepdims=True) + EPS)
        xh = xv * r
        err = xh * g - t_ref[...]
        dy = err * (1.0 / d)
        dyg = dy * g
        mean = jnp.mean(dyg * xh, axis=-1, keepdims=True)
        dx_ref[...] = r * (dyg - xh * mean)
        part = jnp.sum((dy * xh).reshape(tm // SUBLANES, SUBLANES, d), axis=0)
        lpart = jnp.sum((err * err).reshape(tm // SUBLANES, SUBLANES, d), axis=0)

        @pl.when(i == 0)
        def _():
            acc_ref[...] = part
            lacc_ref[...] = lpart

        @pl.when(i > 0)
        def _():
            acc_ref[...] += part
            lacc_ref[...] += lpart

        @pl.when(i == steps - 1)
        def _():
            dg_ref[...] = jnp.sum(acc_ref[...], axis=0, keepdims=True)
            tot = jnp.sum(jnp.sum(lacc_ref[...], axis=0, keepdims=True), axis=1, keepdims=True)
            loss_ref[...] = jnp.broadcast_to(tot * (0.5 / d), loss_ref.shape)

    row = pl.BlockSpec((tm, d), lambda i: (i, 0))
    vec = pl.BlockSpec((1, d), lambda i: (0, 0))
    dx, dg, loss = pl.pallas_call(
        body, name="loss_head", grid=(steps,),
        in_specs=[row, vec, row],
        out_specs=[row, vec, pl.BlockSpec((1, LANES), lambda i: (0, 0))],
        out_shape=[jax.ShapeDtypeStruct((n, d), F32), jax.ShapeDtypeStruct((1, d), F32),
                   jax.ShapeDtypeStruct((1, LANES), F32)],
        scratch_shapes=[pltpu.VMEM((SUBLANES, d), F32), pltpu.VMEM((SUBLANES, d), F32)],
        compiler_params=_params("arbitrary"),
    )(x, gain.reshape(1, d), target)
    return dx, dg.reshape(d), loss[0, 0]


def _rms_rows(xv):
    return lax.rsqrt(jnp.mean(xv * xv, axis=-1, keepdims=True) + EPS)


def _ffn_in_fwd(x, gain, w_in, name, comm=None):
    n, d = x.shape
    f = w_in.shape[1] // 2
    tm = _tile(n, 256, 16)
    tn = _tile(f, 4096, LANES)
    nj = f // tn

    def body(x_ref, gain_ref, wg_ref, wu_ref, h_ref, g_ref, u_ref, a_ref):
        @pl.when(pl.program_id(1) == 0)
        def _():
            xv = x_ref[...]
            h_ref[...] = (xv * _rms_rows(xv) * gain_ref[...]).astype(h_ref.dtype)

        h = h_ref[...]
        g = jnp.dot(h, wg_ref[...], preferred_element_type=F32)
        u = jnp.dot(h, wu_ref[...], preferred_element_type=F32)
        g_ref[...] = g.astype(g_ref.dtype)
        u_ref[...] = u.astype(u_ref.dtype)
        a_ref[...] = (g * _sigmoid(g) * u).astype(a_ref.dtype)

    row = pl.BlockSpec((tm, d), lambda i, j: (i, 0))
    tile = pl.BlockSpec((tm, tn), lambda i, j: (i, j))
    act = jax.ShapeDtypeStruct((n, f), BF16)
    outs, comm_outs = _call(
        body, name=name, grid=(n // tm, nj),
        in_specs=[row, pl.BlockSpec((1, d), lambda i, j: (0, 0)),
                  pl.BlockSpec((d, tn), lambda i, j: (0, j)), pl.BlockSpec((d, tn), lambda i, j: (0, j + nj))],
        out_specs=[row, tile, tile, tile],
        out_shape=[jax.ShapeDtypeStruct((n, d), BF16), act, act, act],
        semantics=("parallel", "arbitrary"), args=(x, gain.reshape(1, d), w_in, w_in), comm=comm)
    return outs if comm is None else (outs, comm_outs)


def _ffn_out_bwd(dout, w_out, g, u, name, comm=None):
    n, d = dout.shape
    f = w_out.shape[0]
    tm = _tile(n, 256, 16)
    tn = _tile(f, 4096, LANES)

    def body(d_ref, w_ref, g_ref, u_ref, dg_ref, du_ref):
        da = 0.5 * lax.dot_general(d_ref[...].astype(BF16), w_ref[...], (((1,), (1,)), ((), ())),
                                   preferred_element_type=F32)
        gv = g_ref[...].astype(F32)
        s = _sigmoid(gv)
        dg_ref[...] = (da * u_ref[...].astype(F32) * (s * (1.0 + gv * (1.0 - s)))).astype(dg_ref.dtype)
        du_ref[...] = (da * gv * s).astype(du_ref.dtype)

    tile = pl.BlockSpec((tm, tn), lambda i, j: (i, j))
    act = jax.ShapeDtypeStruct((n, f), BF16)
    outs, comm_outs = _call(
        body, name=name, grid=(n // tm, f // tn),
        in_specs=[pl.BlockSpec((tm, d), lambda i, j: (i, 0)), pl.BlockSpec((tn, d), lambda i, j: (j, 0)), tile, tile],
        out_specs=[tile, tile], out_shape=[act, act],
        semantics=("parallel", "parallel"), args=(dout, w_out, g, u), comm=comm)
    return outs if comm is None else (outs, comm_outs)


def _proj_in_bwd(parts, w, x, gain, dres, name, comm=None):
    n, d = x.shape
    tm = _tile(n, 256, 8)
    steps = n // tm
    np_ = len(parts)
    offs = [off for _, off in parts]
    widths = [a.shape[1] for a, _ in parts]

    def body(*refs):
        a_refs = refs[:np_]
        w_ref, x_ref, g_ref, dr_ref, dx_ref, dg_ref, acc_ref = refs[np_:]
        i = pl.program_id(0)
        dh = None
        for a_ref, off, kp in zip(a_refs, offs, widths):
            part = lax.dot_general(a_ref[...].astype(BF16), w_ref[:, off:off + kp], (((1,), (1,)), ((), ())),
                                   preferred_element_type=F32)
            dh = part if dh is None else dh + part
        xv = x_ref[...]
        r = _rms_rows(xv)
        xh = xv * r
        dyg = dh * g_ref[...]
        mean = jnp.mean(dyg * xh, axis=-1, keepdims=True)
        dx_ref[...] = dr_ref[...] + r * (dyg - xh * mean)
        part = jnp.sum((dh * xh).reshape(tm // SUBLANES, SUBLANES, d), axis=0)

        @pl.when(i == 0)
        def _():
            acc_ref[...] = part

        @pl.when(i > 0)
        def _():
            acc_ref[...] += part

        @pl.when(i == steps - 1)
        def _():
            dg_ref[...] = jnp.sum(acc_ref[...], axis=0, keepdims=True)

    row = pl.BlockSpec((tm, d), lambda i: (i, 0))
    vec = pl.BlockSpec((1, d), lambda i: (0, 0))
    (dx, dg), comm_outs = _call(
        body, name=name, grid=(steps,),
        in_specs=[pl.BlockSpec((tm, kp), lambda i: (i, 0)) for kp in widths]
        + [pl.BlockSpec(w.shape, lambda i: (0, 0)), row, vec, row],
        out_specs=[row, vec],
        out_shape=[jax.ShapeDtypeStruct((n, d), F32), jax.ShapeDtypeStruct((1, d), F32)],
        scratch_shapes=[pltpu.VMEM((SUBLANES, d), F32)],
        semantics=("arbitrary",), args=(*[a for a, _ in parts], w, x, gain.reshape(1, d), dres), comm=comm)
    return (dx, dg.reshape(d)) if comm is None else (dx, dg.reshape(d), comm_outs)


def _mix_in_fwd(x, gain, w, width, name):
    n, d = x.shape
    cols = w.shape[1]
    tm = _tile(n, 512, 16)

    def body(x_ref, gain_ref, w_ref, h_ref, u_ref, z_ref):
        xv = x_ref[...]
        h = (xv * _rms_rows(xv) * gain_ref[...]).astype(h_ref.dtype)
        h_ref[...] = h
        z = jnp.dot(h, w_ref[...], preferred_element_type=F32)
        u_ref[...] = z[:, 0:width]
        z_ref[...] = z[:, width:cols]

    row = pl.BlockSpec((tm, d), lambda i: (i, 0))
    return pl.pallas_call(
        body, name=name, grid=(n // tm,),
        in_specs=[row, pl.BlockSpec((1, d), lambda i: (0, 0)), pl.BlockSpec((d, cols), lambda i: (0, 0))],
        out_specs=[row, pl.BlockSpec((tm, width), lambda i: (i, 0)), pl.BlockSpec((tm, cols - width), lambda i: (i, 0))],
        out_shape=[jax.ShapeDtypeStruct((n, d), BF16), jax.ShapeDtypeStruct((n, width), F32),
                   jax.ShapeDtypeStruct((n, cols - width), F32)],
        compiler_params=_params("parallel"),
    )(x, gain.reshape(1, d), w)


def _tril_mask():
    t = lax.broadcasted_iota(jnp.int32, (GM_CHUNK, GM_CHUNK), 0)
    s = lax.broadcasted_iota(jnp.int32, (GM_CHUNK, GM_CHUNK), 1)
    return s <= t


def _gmlp_fwd(zgm, v_gain, w_s, bias_tile, name):
    n, w2 = zgm.shape
    w = w2 // 2
    heads = w // GM_HEAD_DIM
    tm = _tile(n, 512, GM_CHUNK)
    nq = tm // GM_CHUNK

    def body(u_ref, v_ref, gain_ref, w_ref, b_ref, o_ref):
        mask = _tril_mask()
        ug = _gelu(u_ref[...])
        vg = _gelu(v_ref[...])
        for h in range(heads):
            cols = slice(h * GM_HEAD_DIM, (h + 1) * GM_HEAD_DIM)
            vh = vg[:, cols]
            r = lax.rsqrt(jnp.mean(vh * vh, axis=-1, keepdims=True) + EPS)
            vn = (vh * r * gain_ref[:, cols]).astype(BF16)
            wm = jnp.where(mask, w_ref[h], 0.0).astype(BF16)
            for q in range(nq):
                rows = slice(q * GM_CHUNK, (q + 1) * GM_CHUNK)
                s = jnp.dot(wm, vn[rows], preferred_element_type=F32) + b_ref[:, cols]
                o_ref[rows, cols] = ug[rows, cols] * s

    return pl.pallas_call(
        body, name=name, grid=(n // tm,),
        in_specs=[pl.BlockSpec((tm, w), lambda i: (i, 0)), pl.BlockSpec((tm, w), lambda i: (i, 1)),
                  pl.BlockSpec((1, w), lambda i: (0, 0)),
                  pl.BlockSpec((heads, GM_CHUNK, GM_CHUNK), lambda i: (0, 0, 0)),
                  pl.BlockSpec((GM_CHUNK, w), lambda i: (0, 0))],
        out_specs=pl.BlockSpec((tm, w), lambda i: (i, 0)),
        out_shape=jax.ShapeDtypeStruct((n, w), F32),
        compiler_params=_params("parallel"),
    )(zgm, zgm, v_gain.reshape(1, w), w_s, bias_tile)


def _gmlp_bwd(zgm, dy, v_gain, w_s, bias_tile, name):
    n, w2 = zgm.shape
    w = w2 // 2
    heads = w // GM_HEAD_DIM
    tm = _tile(n, 512, GM_CHUNK)
    nq = tm // GM_CHUNK
    steps = n // tm

    def body(z_ref, dy_ref, gain_ref, w_ref, b_ref, dz_ref, dw_ref, db_ref, dgain_ref):
        i = pl.program_id(0)
        mask = _tril_mask()

        @pl.when(i == 0)
        def _():
            dw_ref[...] = jnp.zeros_like(dw_ref)
            db_ref[...] = jnp.zeros_like(db_ref)
            dgain_ref[...] = jnp.zeros_like(dgain_ref)

        ug, dug_du = _gelu_and_grad(z_ref[:, 0:w])
        vg, dvg_dv = _gelu_and_grad(z_ref[:, w:w2])
        dyv = dy_ref[...]
        for h in range(heads):
            cols = slice(h * GM_HEAD_DIM, (h + 1) * GM_HEAD_DIM)
            vh = vg[:, cols]
            r = lax.rsqrt(jnp.mean(vh * vh, axis=-1, keepdims=True) + EPS)
            vhat = vh * r
            gain = gain_ref[:, cols]
            vn = (vhat * gain).astype(BF16)
            wm = jnp.where(mask, w_ref[h], 0.0).astype(BF16)
            dvn_parts = []
            for q in range(nq):
                rows = slice(q * GM_CHUNK, (q + 1) * GM_CHUNK)
                s = jnp.dot(wm, vn[rows], preferred_element_type=F32) + b_ref[:, cols]
                dyq = dyv[rows, cols]
                dz_ref[rows, cols] = dyq * s * dug_du[rows, cols]
                ds = dyq * ug[rows, cols]
                db_ref[:, cols] += ds
                dsb = ds.astype(BF16)
                dw_ref[h] += lax.dot_general(dsb, vn[rows], (((1,), (1,)), ((), ())), preferred_element_type=F32)
                dvn_parts.append(lax.dot_general(wm, dsb, (((0,), (0,)), ((), ())), preferred_element_type=F32))
            dvn = jnp.concatenate(dvn_parts, axis=0) if nq > 1 else dvn_parts[0]
            dgain_ref[:, cols] += jnp.sum(dvn * vhat, axis=0, keepdims=True)
            dvhat = dvn * gain
            mean = jnp.mean(dvhat * vhat, axis=-1, keepdims=True)
            dz_ref[:, w + h * GM_HEAD_DIM:w + (h + 1) * GM_HEAD_DIM] = r * (dvhat - vhat * mean) * dvg_dv[:, cols]

        @pl.when(i == steps - 1)
        def _():
            for h in range(heads):
                dw_ref[h] = jnp.where(mask, dw_ref[h], 0.0)

    dz, dw, db, dgain = pl.pallas_call(
        body, name=name, grid=(steps,),
        in_specs=[pl.BlockSpec((tm, w2), lambda i: (i, 0)), pl.BlockSpec((tm, w), lambda i: (i, 0)),
                  pl.BlockSpec((1, w), lambda i: (0, 0)),
                  pl.BlockSpec((heads, GM_CHUNK, GM_CHUNK), lambda i: (0, 0, 0)),
                  pl.BlockSpec((GM_CHUNK, w), lambda i: (0, 0))],
        out_specs=[pl.BlockSpec((tm, w2), lambda i: (i, 0)),
                   pl.BlockSpec((heads, GM_CHUNK, GM_CHUNK), lambda i: (0, 0, 0)),
                   pl.BlockSpec((GM_CHUNK, w), lambda i: (0, 0)),
                   pl.BlockSpec((1, w), lambda i: (0, 0))],
        out_shape=[jax.ShapeDtypeStruct((n, w2), F32), jax.ShapeDtypeStruct((heads, GM_CHUNK, GM_CHUNK), F32),
                   jax.ShapeDtypeStruct((GM_CHUNK, w), F32), jax.ShapeDtypeStruct((1, w), F32)],
        compiler_params=_params("arbitrary"),
    )(zgm, dy, v_gain.reshape(1, w), w_s, bias_tile)
    return dz, dw, db, dgain.reshape(w)


def _mixnorm_fwd(y_ssm, y_gm, g1, g2, name):
    n, w = y_ssm.shape
    tm = _tile(n, 512, 16)

    def body(a_ref, b_ref, g1_ref, g2_ref, o_ref):
        for src, g_ref, lo in ((a_ref, g1_ref, 0), (b_ref, g2_ref, w)):
            v = src[...]
            r = lax.rsqrt(jnp.mean(v * v, axis=-1, keepdims=True) + EPS)
            o_ref[:, lo:lo + w] = (v * r * g_ref[...]).astype(o_ref.dtype)

    row = pl.BlockSpec((tm, w), lambda i: (i, 0))
    vec = pl.BlockSpec((1, w), lambda i: (0, 0))
    return pl.pallas_call(
        body, name=name, grid=(n // tm,),
        in_specs=[row, row, vec, vec], out_specs=pl.BlockSpec((tm, 2 * w), lambda i: (i, 0)),
        out_shape=jax.ShapeDtypeStruct((n, 2 * w), BF16),
        compiler_params=_params("parallel"),
    )(y_ssm, y_gm, g1.reshape(1, w), g2.reshape(1, w))


def _mixnorm_bwd(y_ssm, y_gm, g1, g2, dycat, name):
    n, w = y_ssm.shape
    tm = _tile(n, 512, 8)
    steps = n // tm

    def body(a_ref, b_ref, g1_ref, g2_ref, d_ref, da_ref, db_ref, dg1_ref, dg2_ref):
        i = pl.program_id(0)

        @pl.when(i == 0)
        def _():
            dg1_ref[...] = jnp.zeros_like(dg1_ref)
            dg2_ref[...] = jnp.zeros_like(dg2_ref)

        for src, g_ref, lo, dst, dg_ref in ((a_ref, g1_ref, 0, da_ref, dg1_ref), (b_ref, g2_ref, w, db_ref, dg2_ref)):
            v = src[...]
            dh = d_ref[:, lo:lo + w]
            r = lax.rsqrt(jnp.mean(v * v, axis=-1, keepdims=True) + EPS)
            vh = v * r
            dyg = dh * g_ref[...]
            mean = jnp.mean(dyg * vh, axis=-1, keepdims=True)
            dst[...] = r * (dyg - vh * mean)
            dg_ref[...] += jnp.sum(dh * vh, axis=0, keepdims=True)

    row = pl.BlockSpec((tm, w), lambda i: (i, 0))
    vec = pl.BlockSpec((1, w), lambda i: (0, 0))
    da, db, dg1, dg2 = pl.pallas_call(
        body, name=name, grid=(steps,),
        in_specs=[row, row, vec, vec, pl.BlockSpec((tm, 2 * w), lambda i: (i, 0))],
        out_specs=[row, row, vec, vec],
        out_shape=[jax.ShapeDtypeStruct((n, w), F32), jax.ShapeDtypeStruct((n, w), F32),
                   jax.ShapeDtypeStruct((1, w), F32), jax.ShapeDtypeStruct((1, w), F32)],
        compiler_params=_params("arbitrary"),
    )(y_ssm, y_gm, g1.reshape(1, w), g2.reshape(1, w), dycat)
    return da, db, dg1.reshape(w), dg2.reshape(w)


def _discretise(a_re, a_im, log_dt, bt_re, bt_im):
    dt = jnp.exp(log_dt)
    e = jnp.exp(a_re * dt)
    ang = a_im * dt
    lr = e * jnp.cos(ang)
    li = e * jnp.sin(ang)
    den = a_re * a_re + a_im * a_im
    cr = ((lr - 1.0) * a_re + li * a_im) / den
    ci = (li * a_re - (lr - 1.0) * a_im) / den
    cr3 = cr[:, None, :]
    ci3 = ci[:, None, :]
    return lr, li, cr3 * bt_re - ci3 * bt_im, cr3 * bt_im + ci3 * bt_re


def _disc_fwd(a_re, a_im, log_dt, bt_re, bt_im):
    g, p = a_re.shape
    c = bt_re.shape[1]

    def body(are_ref, aim_ref, ldt_ref, bre_ref, bim_ref, lr_ref, li_ref, bbr_ref, bbi_ref):
        lr, li, bbr, bbi = _discretise(are_ref[...], aim_ref[...], ldt_ref[...], bre_ref[...], bim_ref[...])
        lr_ref[...] = lr
        li_ref[...] = li
        bbr_ref[...] = bbr
        bbi_ref[...] = bbi

    return pl.pallas_call(
        body, name="s5_discretise",
        out_shape=[jax.ShapeDtypeStruct((g, p), F32), jax.ShapeDtypeStruct((g, p), F32),
                   jax.ShapeDtypeStruct((g, c, p), F32), jax.ShapeDtypeStruct((g, c, p), F32)],
    )(a_re, a_im, log_dt, bt_re, bt_im)


def _disc_bwd(a_re, a_im, log_dt, bt_re, bt_im, dlr, dli, dbbr, dbbi):
    g, p = a_re.shape
    c = bt_re.shape[1]

    def body(are_ref, aim_ref, ldt_ref, bre_ref, bim_ref, dlr_ref, dli_ref, dbbr_ref, dbbi_ref,
             dare_ref, daim_ref, dldt_ref, dbre_ref, dbim_ref):
        _, vjp = jax.vjp(_discretise, are_ref[...], aim_ref[...], ldt_ref[...], bre_ref[...], bim_ref[...])
        dare, daim, dldt, dbre, dbim = vjp((dlr_ref[...], dli_ref[...], dbbr_ref[...], dbbi_ref[...]))
        dare_ref[...] = dare
        daim_ref[...] = daim
        dldt_ref[...] = dldt
        dbre_ref[...] = dbre
        dbim_ref[...] = dbim

    return pl.pallas_call(
        body, name="s5_discretise_bwd",
        out_shape=[jax.ShapeDtypeStruct((g, p), F32), jax.ShapeDtypeStruct((g, p), F32),
                   jax.ShapeDtypeStruct((g, 1), F32),
                   jax.ShapeDtypeStruct((g, c, p), F32), jax.ShapeDtypeStruct((g, c, p), F32)],
    )(a_re, a_im, log_dt, bt_re, bt_im, dlr, dli, dbbr, dbbi)


def _block_diag(w, nb):
    g, a, b = w.shape
    gpb = g // nb
    eye = jnp.eye(gpb, dtype=w.dtype)
    w4 = w.reshape(nb, gpb, a, b)
    return jnp.einsum("ngab,gh->ngahb", w4, eye).reshape(nb, gpb * a, gpb * b)


def _block_diag_extract(m, gpb):
    nb, ga, gb = m.shape
    a, b = ga // gpb, gb // gpb
    m5 = m.reshape(nb, gpb, a, gpb, b)
    idx = jnp.arange(gpb)
    return m5[:, idx, :, idx, :].transpose(1, 0, 2, 3).reshape(nb * gpb, a, b)


def _ssm_operands(lr, li, bbr, bbi, c_re, c_im, d_skip, glu_w, glu_b):
    g = lr.shape[0]
    nb = g // GROUPS_PER_BLOCK
    s = STATES_PER_BLOCK
    lam = jnp.concatenate([lr.reshape(nb, 1, s), li.reshape(nb, 1, s)], axis=-1)
    b_bd = jnp.concatenate([_block_diag(bbr, nb), _block_diag(bbi, nb)], axis=-1)
    ct_re = jnp.swapaxes(c_re, 1, 2)
    ct_im = jnp.swapaxes(c_im, 1, 2)
    c_bd = jnp.concatenate([_block_diag(ct_re, nb), -_block_diag(ct_im, nb)], axis=1)
    dsk = d_skip.reshape(nb, 1, LANES)
    w_bd = jnp.concatenate([_block_diag(glu_w[:, :, :SSM_CH], nb), _block_diag(glu_w[:, :, SSM_CH:], nb)], axis=-1)
    bias = jnp.concatenate([glu_b[:, :SSM_CH].reshape(nb, 1, LANES), glu_b[:, SSM_CH:].reshape(nb, 1, LANES)], axis=-1)
    return lam, b_bd.astype(BF16), c_bd.astype(BF16), dsk, w_bd.astype(BF16), bias


def _roll_rows(v, shift):
    return v if shift % SUBLANES == 0 else pltpu.roll(v, shift % SUBLANES, 0)


def _scan_chunk_rows(seq, nseq):
    return _tile(seq, max(8 * SSM_TIME_CHUNK // nseq, 8), max(SUBLANES // nseq, 1) * 8)


def _ssm_fwd(u8, ops, nseq, name, comm=None):
    lam, b_bd, c_bd, dsk, w_bd, bias = ops
    rows_total, w = u8.shape
    seq = rows_total // nseq
    nb = w // LANES
    s = STATES_PER_BLOCK
    tc = _scan_chunk_rows(seq, nseq)
    nk = seq // tc
    rows = tc * nseq
    stages = SUBLANES // nseq

    def body(u_ref, lam_ref, b_ref, c_ref, d_ref, w_ref, bias_ref, y_ref, hb_ref, buf, st):
        k = pl.program_id(1)

        @pl.when(k == 0)
        def _():
            st[...] = jnp.zeros_like(st)

        hb_ref[...] = st[...]
        u = u_ref[...]
        buf[...] = jnp.dot(u.astype(BF16), b_ref[0], preferred_element_type=F32)
        lr = jnp.broadcast_to(lam_ref[0, :, 0:s], (SUBLANES, s))
        li = jnp.broadcast_to(lam_ref[0, :, s:2 * s], (SUBLANES, s))
        row = lax.broadcasted_iota(jnp.int32, (SUBLANES, s), 0)

        def step(i, carry):
            pr, pi = carry
            r0 = pl.multiple_of(i * SUBLANES, SUBLANES)
            br = buf[pl.ds(r0, SUBLANES), 0:s]
            bi = buf[pl.ds(r0, SUBLANES), s:2 * s]
            outr = outi = None
            for j in range(stages):
                rr = _roll_rows(pr, nseq)
                ri = _roll_rows(pi, nseq)
                pr = lr * rr - li * ri + br
                pi = lr * ri + li * rr + bi
                outr = pr if j == 0 else jnp.where(row >= j * nseq, pr, outr)
                outi = pi if j == 0 else jnp.where(row >= j * nseq, pi, outi)
            buf[pl.ds(r0, SUBLANES), 0:s] = outr
            buf[pl.ds(r0, SUBLANES), s:2 * s] = outi
            return outr, outi

        hr, hi = lax.fori_loop(0, rows // SUBLANES, step, (st[:, 0:s], st[:, s:2 * s]), unroll=2)
        st[:, 0:s] = hr
        st[:, s:2 * s] = hi
        y = jnp.dot(buf[...].astype(BF16), c_ref[0], preferred_element_type=F32) + d_ref[0] * u
        z = jnp.dot(_gelu(y).astype(BF16), w_ref[0], preferred_element_type=F32) + bias_ref[0]
        y_ref[...] = z[:, 0:LANES] * _sigmoid(z[:, LANES:2 * LANES])

    blk = lambda shape: pl.BlockSpec(shape, lambda b, k: (b, 0, 0))
    (y8, hb), comm_outs = _call(
        body, name=name, grid=(nb, nk),
        in_specs=[pl.BlockSpec((rows, LANES), lambda b, k: (k, b)),
                  blk((1, 1, 2 * s)), blk((1, LANES, 2 * s)), blk((1, 2 * s, LANES)),
                  blk((1, 1, LANES)), blk((1, LANES, 2 * LANES)), blk((1, 1, 2 * LANES))],
        out_specs=[pl.BlockSpec((rows, LANES), lambda b, k: (k, b)),
                   pl.BlockSpec((SUBLANES, 2 * s), lambda b, k: (k, b))],
        out_shape=[jax.ShapeDtypeStruct((rows_total, w), F32),
                   jax.ShapeDtypeStruct((nk * SUBLANES, nb * 2 * s), F32)],
        scratch_shapes=[pltpu.VMEM((rows, 2 * s), F32), pltpu.VMEM((SUBLANES, 2 * s), F32)],
        semantics=("parallel", "arbitrary"), args=(u8, lam, b_bd, c_bd, dsk, w_bd, bias), comm=comm)
    return (y8, hb) if comm is None else (y8, hb, comm_outs)


def _ssm_bwd(u8, dy8, hb, ops, nseq, name, comm=None):
    lam, b_bd, c_bd, dsk, w_bd, bias = ops
    rows_total, w = u8.shape
    seq = rows_total // nseq
    nb = w // LANES
    s = STATES_PER_BLOCK
    tc = _scan_chunk_rows(seq, nseq)
    nk = seq // tc
    rows = tc * nseq
    nblk = rows // SUBLANES
    stages = SUBLANES // nseq
    tn_dims = (((0,), (0,)), ((), ()))
    nt_dims = (((1,), (1,)), ((), ()))

    def body(u_ref, dy_ref, hb_ref, lam_ref, b_ref, c_ref, d_ref, w_ref, bias_ref,
             du_ref, dlam_ref, db_ref, dct_ref, dd_ref, dw_ref, dbias_ref, hbuf, gbuf, gst, lacc):
        k = pl.program_id(1)

        @pl.when(k == 0)
        def _():
            gst[...] = jnp.zeros_like(gst)
            lacc[...] = jnp.zeros_like(lacc)
            db_ref[...] = jnp.zeros_like(db_ref)
            dct_ref[...] = jnp.zeros_like(dct_ref)
            dd_ref[...] = jnp.zeros_like(dd_ref)
            dw_ref[...] = jnp.zeros_like(dw_ref)
            dbias_ref[...] = jnp.zeros_like(dbias_ref)

        u = u_ref[...]
        ub = u.astype(BF16)
        lr = jnp.broadcast_to(lam_ref[0, :, 0:s], (SUBLANES, s))
        li = jnp.broadcast_to(lam_ref[0, :, s:2 * s], (SUBLANES, s))
        row = lax.broadcasted_iota(jnp.int32, (SUBLANES, s), 0)
        hbuf[...] = jnp.dot(ub, b_ref[0], preferred_element_type=F32)

        def fstep(i, carry):
            pr, pi = carry
            r0 = pl.multiple_of(i * SUBLANES, SUBLANES)
            br = hbuf[pl.ds(r0, SUBLANES), 0:s]
            bi = hbuf[pl.ds(r0, SUBLANES), s:2 * s]
            outr = outi = None
            for j in range(stages):
                rr = _roll_rows(pr, nseq)
                ri = _roll_rows(pi, nseq)
                pr = lr * rr - li * ri + br
                pi = lr * ri + li * rr + bi
                outr = pr if j == 0 else jnp.where(row >= j * nseq, pr, outr)
                outi = pi if j == 0 else jnp.where(row >= j * nseq, pi, outi)
            hbuf[pl.ds(r0, SUBLANES), 0:s] = outr
            hbuf[pl.ds(r0, SUBLANES), s:2 * s] = outi
            return outr, outi

        lax.fori_loop(0, nblk, fstep, (hb_ref[:, 0:s], hb_ref[:, s:2 * s]), unroll=2)
        hb16 = hbuf[...].astype(BF16)
        y = jnp.dot(hb16, c_ref[0], preferred_element_type=F32) + d_ref[0] * u
        yg, dyg_dy = _gelu_and_grad(y)
        yg16 = yg.astype(BF16)
        z = jnp.dot(yg16, w_ref[0], preferred_element_type=F32) + bias_ref[0]
        z1 = z[:, 0:LANES]
        sg = _sigmoid(z[:, LANES:2 * LANES])
        dout = dy_ref[...]
        dz = jnp.concatenate([dout * sg, dout * z1 * sg * (1.0 - sg)], axis=-1)
        dz16 = dz.astype(BF16)
        dw_ref[0] += lax.dot_general(yg16, dz16, tn_dims, preferred_element_type=F32)
        dbias_ref[0] += jnp.sum(dz, axis=0, keepdims=True)
        dy = lax.dot_general(dz16, w_ref[0], nt_dims, preferred_element_type=F32) * dyg_dy
        dy16 = dy.astype(BF16)
        dd_ref[0] += jnp.sum(dy * u, axis=0, keepdims=True)
        dct_ref[0] += lax.dot_general(dy16, hb16, tn_dims, preferred_element_type=F32)
        gbuf[...] = lax.dot_general(dy16, c_ref[0], nt_dims, preferred_element_type=F32)

        def bstep(i, carry):
            pr, pi, ar, ai = carry
            blk = nblk - 1 - i
            r0 = pl.multiple_of(blk * SUBLANES, SUBLANES)
            dr = gbuf[pl.ds(r0, SUBLANES), 0:s]
            di = gbuf[pl.ds(r0, SUBLANES), s:2 * s]
            outr = outi = None
            for j in reversed(range(stages)):
                rr = _roll_rows(pr, SUBLANES - nseq)
                ri = _roll_rows(pi, SUBLANES - nseq)
                pr = dr + lr * rr + li * ri
                pi = di - li * rr + lr * ri
                outr = pr if j == stages - 1 else jnp.where(row < (j + 1) * nseq, pr, outr)
                outi = pi if j == stages - 1 else jnp.where(row < (j + 1) * nseq, pi, outi)
            gbuf[pl.ds(r0, SUBLANES), 0:s] = outr
            gbuf[pl.ds(r0, SUBLANES), s:2 * s] = outi
            p0 = pl.multiple_of(jnp.maximum(blk - 1, 0) * SUBLANES, SUBLANES)
            first = blk == 0
            before_r = jnp.where(first, hb_ref[:, 0:s], hbuf[pl.ds(p0, SUBLANES), 0:s])
            before_i = jnp.where(first, hb_ref[:, s:2 * s], hbuf[pl.ds(p0, SUBLANES), s:2 * s])
            if stages > 1:
                last_rows = row >= SUBLANES - nseq
                before_r = _roll_rows(jnp.where(last_rows, before_r, hbuf[pl.ds(r0, SUBLANES), 0:s]), nseq)
                before_i = _roll_rows(jnp.where(last_rows, before_i, hbuf[pl.ds(r0, SUBLANES), s:2 * s]), nseq)
            return (outr, outi, ar + outr * before_r + outi * before_i, ai - outr * before_i + outi * before_r)

        gr, gi, ar, ai = lax.fori_loop(
            0, nblk, bstep, (gst[:, 0:s], gst[:, s:2 * s], lacc[:, 0:s], lacc[:, s:2 * s]))
        gst[:, 0:s] = gr
        gst[:, s:2 * s] = gi
        lacc[:, 0:s] = ar
        lacc[:, s:2 * s] = ai
        g16 = gbuf[...].astype(BF16)
        du_ref[...] = dy * d_ref[0] + lax.dot_general(g16, b_ref[0], nt_dims, preferred_element_type=F32)
        db_ref[0] += lax.dot_general(ub, g16, tn_dims, preferred_element_type=F32)

        @pl.when(k == nk - 1)
        def _():
            dlam_ref[0] = jnp.sum(lacc[...], axis=0, keepdims=True)

    blk = lambda shape: pl.BlockSpec(shape, lambda b, k: (b, 0, 0))
    rev = lambda b, k: (nk - 1 - k, b)
    outs, comm_outs = _call(
        body, name=name, grid=(nb, nk),
        in_specs=[pl.BlockSpec((rows, LANES), rev), pl.BlockSpec((rows, LANES), rev),
                  pl.BlockSpec((SUBLANES, 2 * s), rev),
                  blk((1, 1, 2 * s)), blk((1, LANES, 2 * s)), blk((1, 2 * s, LANES)),
                  blk((1, 1, LANES)), blk((1, LANES, 2 * LANES)), blk((1, 1, 2 * LANES))],
        out_specs=[pl.BlockSpec((rows, LANES), rev),
                   blk((1, 1, 2 * s)), blk((1, LANES, 2 * s)), blk((1, LANES, 2 * s)),
                   blk((1, 1, LANES)), blk((1, LANES, 2 * LANES)), blk((1, 1, 2 * LANES))],
        out_shape=[jax.ShapeDtypeStruct((rows_total, w), F32),
                   jax.ShapeDtypeStruct((nb, 1, 2 * s), F32), jax.ShapeDtypeStruct((nb, LANES, 2 * s), F32),
                   jax.ShapeDtypeStruct((nb, LANES, 2 * s), F32), jax.ShapeDtypeStruct((nb, 1, LANES), F32),
                   jax.ShapeDtypeStruct((nb, LANES, 2 * LANES), F32), jax.ShapeDtypeStruct((nb, 1, 2 * LANES), F32)],
        scratch_shapes=[pltpu.VMEM((rows, 2 * s), F32), pltpu.VMEM((rows, 2 * s), F32),
                        pltpu.VMEM((SUBLANES, 2 * s), F32), pltpu.VMEM((SUBLANES, 2 * s), F32)],
        semantics=("parallel", "arbitrary"), args=(u8, dy8, hb, lam, b_bd, c_bd, dsk, w_bd, bias), comm=comm)
    return outs if comm is None else (outs, comm_outs)


def _to_scan_rows(a, nseq, seq):
    w = a.shape[-1]
    return jnp.swapaxes(a.reshape(nseq, seq, w), 0, 1).reshape(seq * nseq, w)


def _from_scan_rows(a8, nseq, seq):
    w = a8.shape[-1]
    return jnp.swapaxes(a8.reshape(seq, nseq, w), 0, 1).reshape(nseq * seq, w)


ANY = pl.BlockSpec(memory_space=pl.ANY)

BIG = (("ffn1_w_in", True), ("ffn1_w_out", False), ("mix_w_in", True), ("mix_w_out", False),
       ("ffn2_w_in", True), ("ffn2_w_out", False))


def _my_place():
    return lax.axis_index("x"), lax.axis_index("y"), lax.axis_index("c")


def _other_chips(x, y):
    return [(1 - x, y), (x, 1 - y), (1 - x, 1 - y)]


def _half_of_shard(ref, col_sharded, chip, core):
    full_rows, full_cols = ref.shape
    if col_sharded:
        hr, cs = full_rows // 2, full_cols // N_CHIPS
        return ref.at[pl.ds(pl.multiple_of(core * hr, 8), hr), pl.ds(chip * cs, cs)]
    rs = full_rows // N_CHIPS
    return ref.at[pl.ds(pl.multiple_of(chip * rs + core * (rs // 2), 8), rs // 2), :]


def _gather_comm(shards, cols):
    full_shapes = [(sh.shape[0], sh.shape[1] * N_CHIPS) if col else (sh.shape[0] * N_CHIPS, sh.shape[1])
                   for sh, col in zip(shards, cols)]
    nw = len(shards)

    def first_copies(ins, outs, sems):
        send_sems, recv_sems, local_sems = sems
        x, y, c = _my_place()
        me = 2 * x + y
        locals_, sends = [], []
        for wi in range(nw):
            src, dst = ins[wi], outs[wi]
            rs, cs = src.shape
            hs = rs // 2
            if cols[wi]:
                place = dst.at[:, pl.ds(me * cs, cs)]
            else:
                place = dst.at[pl.ds(pl.multiple_of(me * rs, 8), rs), :]
            locals_.append(pltpu.make_async_copy(src, place, local_sems.at[wi]))
            my_half = src.at[pl.ds(pl.multiple_of(c * hs, 8), hs), :]
            for j, (px, py) in enumerate(_other_chips(x, y)):
                sends.append(pltpu.make_async_remote_copy(
                    src_ref=my_half, dst_ref=_half_of_shard(dst, cols[wi], me, c),
                    send_sem=send_sems.at[wi * 6 + j], recv_sem=recv_sems.at[wi * 6 + j],
                    device_id=(px, py, c), device_id_type=MESH))
        return locals_, sends

    def start(ins, outs, sems):
        locals_, sends = first_copies(ins, outs, sems)
        for cp in locals_ + sends:
            cp.start()

    def finish(ins, outs, sems):
        send_sems, recv_sems, _ = sems
        x, y, c = _my_place()
        chips = _other_chips(x, y)
        locals_, sends = first_copies(ins, outs, sems)
        fwds = []
        for wi in range(nw):
            dst = outs[wi]
            for j, (px, py) in enumerate(chips):
                got = _half_of_shard(dst, cols[wi], 2 * px + py, c)
                pltpu.make_async_remote_copy(
                    src_ref=got, dst_ref=got, send_sem=send_sems.at[wi * 6 + j], recv_sem=recv_sems.at[wi * 6 + j],
                    device_id=(px, py, c), device_id_type=MESH).wait_recv()
                cp = pltpu.make_async_remote_copy(
                    src_ref=got, dst_ref=got, send_sem=send_sems.at[wi * 6 + 3 + j], recv_sem=recv_sems.at[wi * 6 + 3 + j],
                    device_id=(x, y, 1 - c), device_id_type=MESH)
                cp.start()
                fwds.append(cp)
        for wi in range(nw):
            dst = outs[wi]
            for j, (px, py) in enumerate(chips):
                theirs = _half_of_shard(dst, cols[wi], 2 * px + py, 1 - c)
                pltpu.make_async_remote_copy(
                    src_ref=theirs, dst_ref=theirs, send_sem=send_sems.at[wi * 6 + 3 + j],
                    recv_sem=recv_sems.at[wi * 6 + 3 + j], device_id=(x, y, 1 - c), device_id_type=MESH).wait_recv()
        for cp in sends + fwds:
            cp.wait_send()
        for cp in locals_:
            cp.wait()

    return _Comm(shards, [jax.ShapeDtypeStruct(s, BF16) for s in full_shapes],
                 [pltpu.SemaphoreType.DMA((6 * nw,)), pltpu.SemaphoreType.DMA((6 * nw,)),
                  pltpu.SemaphoreType.DMA((nw,))], start, finish)


def _pair_exchange_comm(grads, cols):
    nw = len(grads)
    n_copies = sum(1 if col else N_CHIPS for col in cols)

    def copies(ins, outs, sems):
        send_sems, recv_sems = sems
        x, y, c = _my_place()
        out = []
        for wi in range(nw):
            src, dst = ins[wi], outs[wi]
            fr = src.shape[0]
            if cols[wi]:
                hr = fr // 2
                pieces = [(src.at[pl.ds(pl.multiple_of((1 - c) * hr, 8), hr), :], dst)]
            else:
                rs = fr // N_CHIPS
                hs = rs // 2
                pieces = [(src.at[pl.ds(pl.multiple_of(k * rs + (1 - c) * hs, 8), hs), :],
                           dst.at[pl.ds(k * hs, hs), :]) for k in range(N_CHIPS)]
            for s_ref, d_ref in pieces:
                out.append(pltpu.make_async_remote_copy(
                    src_ref=s_ref, dst_ref=d_ref, send_sem=send_sems.at[len(out)], recv_sem=recv_sems.at[len(out)],
                    device_id=(x, y, 1 - c), device_id_type=MESH))
        return out

    def start(ins, outs, sems):
        for cp in copies(ins, outs, sems):
            cp.start()

    def finish(ins, outs, sems):
        for cp in copies(ins, outs, sems):
            cp.wait()

    return _Comm(grads, [jax.ShapeDtypeStruct((g.shape[0] // 2, g.shape[1]), F32) for g in grads],
                 [pltpu.SemaphoreType.DMA((n_copies,)), pltpu.SemaphoreType.DMA((n_copies,))], start, finish)


def _pair_sum(grad, other, col, core, name):
    fr, fc = grad.shape
    pieces = 1 if col else N_CHIPS
    pr = fr // 2 // pieces
    gview = grad.reshape(pieces * 2, pr, fc)
    oview = other.reshape(pieces, pr, fc)
    tr = _tile(pr, 256, 16)

    def body(c_ref, g_ref, o_ref, out_ref):
        out_ref[...] = (g_ref[...] + o_ref[...]).astype(out_ref.dtype)

    out = pl.pallas_call(
        body, name=name,
        grid_spec=pltpu.PrefetchScalarGridSpec(
            num_scalar_prefetch=1, grid=(pieces, pr // tr),
            in_specs=[pl.BlockSpec((1, tr, fc), lambda p, i, cref: (p * 2 + cref[0], i, 0)),
                      pl.BlockSpec((1, tr, fc), lambda p, i, cref: (p, i, 0))],
            out_specs=pl.BlockSpec((1, tr, fc), lambda p, i, cref: (p, i, 0))),
        out_shape=jax.ShapeDtypeStruct((pieces, pr, fc), BF16),
        compiler_params=_params("parallel", "parallel"),
    )(core, gview, oview)
    return out.reshape(fr // 2, fc)


def _chip_exchange_comm(psums, cols):
    nw = len(psums)
    out_shapes = [(N_CHIPS, p.shape[0], p.shape[1] // N_CHIPS) if col else (N_CHIPS, p.shape[0] // N_CHIPS, p.shape[1])
                  for p, col in zip(psums, cols)]

    def copies(ins, outs, sems):
        send_sems, recv_sems, local_sems = sems
        x, y, c = _my_place()
        me = 2 * x + y
        out = []
        for wi in range(nw):
            src = ins[wi]
            mine = outs[wi].at[me]

            def piece(chip, src=src, col=cols[wi]):
                if col:
                    cs = src.shape[1] // N_CHIPS
                    return src.at[:, pl.ds(chip * cs, cs)]
                ps = src.shape[0] // N_CHIPS
                return src.at[pl.ds(pl.multiple_of(chip * ps, 8), ps), :]

            out.append(pltpu.make_async_copy(piece(me), mine, local_sems.at[wi]))
            for j, (px, py) in enumerate(_other_chips(x, y)):
                out.append(pltpu.make_async_remote_copy(
                    src_ref=piece(2 * px + py), dst_ref=mine,
                    send_sem=send_sems.at[wi * 3 + j], recv_sem=recv_sems.at[wi * 3 + j],
                    device_id=(px, py, c), device_id_type=MESH))
        return out

    def start(ins, outs, sems):
        for cp in copies(ins, outs, sems):
            cp.start()

    def finish(ins, outs, sems):
        for cp in copies(ins, outs, sems):
            cp.wait()

    return _Comm(psums, [jax.ShapeDtypeStruct(s, BF16) for s in out_shapes],
                 [pltpu.SemaphoreType.DMA((3 * nw,)), pltpu.SemaphoreType.DMA((3 * nw,)),
                  pltpu.SemaphoreType.DMA((nw,))], start, finish)


def _chip_sum(slots, core, layer, layers, into, name):
    _, hr, cs = slots.shape
    tr = _tile(hr, 256, 16)

    def body(c_ref, s_ref, *rest):
        out_ref = rest[-1]
        acc = s_ref[0].astype(F32)
        for i in range(1, N_CHIPS):
            acc = acc + s_ref[i].astype(F32)
        out_ref[0] = acc

    in_specs = [pl.BlockSpec((N_CHIPS, tr, cs), lambda i, cref: (0, i, 0))]
    args = [core, slots]
    aliases = {}
    if into is not None:
        in_specs.append(pl.BlockSpec(memory_space=pl.ANY))
        args.append(into.reshape(layers * 2, hr, cs))
        aliases = {2: 0}
    out = pl.pallas_call(
        body, name=name,
        grid_spec=pltpu.PrefetchScalarGridSpec(
            num_scalar_prefetch=1, grid=(hr // tr,), in_specs=in_specs,
            out_specs=pl.BlockSpec((1, tr, cs), lambda i, cref: (layer * 2 + cref[0], i, 0))),
        out_shape=jax.ShapeDtypeStruct((layers * 2, hr, cs), F32),
        input_output_aliases=aliases,
        compiler_params=_params("parallel"),
    )(*args)
    return out.reshape(layers, 2 * hr, cs)


def _pair_share_comm(reduced):
    nw = len(reduced)

    def copies(ins, outs, sems):
        send_sems, recv_sems = sems
        x, y, c = _my_place()
        out = []
        for wi in range(nw):
            hs = outs[wi].shape[1] // 2
            mine = outs[wi].at[:, pl.ds(pl.multiple_of(c * hs, 8), hs), :]
            out.append(pltpu.make_async_remote_copy(
                src_ref=mine, dst_ref=mine, send_sem=send_sems.at[wi], recv_sem=recv_sems.at[wi],
                device_id=(x, y, 1 - c), device_id_type=MESH))
        return out

    def start(ins, outs, sems):
        for cp in copies(ins, outs, sems):
            cp.start()

    def finish(ins, outs, sems):
        for cp in copies(ins, outs, sems):
            cp.wait()

    return _Comm(reduced, [jax.ShapeDtypeStruct(r.shape, F32) for r in reduced],
                 [pltpu.SemaphoreType.DMA((nw,)), pltpu.SemaphoreType.DMA((nw,))], start, finish,
                 alias={i: i for i in range(nw)})


def _all_reduce_small(flat):
    rows, lanes = flat.shape
    seg = rows // N_DEV

    def body(in_ref, out_ref, recv_ref, send_sems, recv_sems):
        x, y, c = _my_place()
        me = 4 * x + 2 * y + c

        def peer(r):
            fx, fy, fc = (r >> 2) & 1, (r >> 1) & 1, r & 1
            px = jnp.where(fx == 1, 1 - x, x)
            py = jnp.where(fy == 1, 1 - y, y)
            pc = jnp.where(fc == 1, 1 - c, c)
            return px, py, pc

        first = []
        for r in range(1, N_DEV):
            px, py, pc = peer(r)
            theirs = in_ref.at[pl.ds(pl.multiple_of((4 * px + 2 * py + pc) * seg, 8), seg), :]
            cp = pltpu.make_async_remote_copy(
                src_ref=theirs, dst_ref=recv_ref.at[r], send_sem=send_sems.at[r - 1], recv_sem=recv_sems.at[r - 1],
                device_id=(px, py, pc), device_id_type=MESH)
            cp.start()
            first.append(cp)
        for cp in first:
            cp.wait()
        my_rows = pl.ds(pl.multiple_of(me * seg, 8), seg)
        acc = in_ref[my_rows, :]
        for r in range(1, N_DEV):
            acc = acc + recv_ref[r]
        out_ref[my_rows, :] = acc
        second = []
        for r in range(1, N_DEV):
            px, py, pc = peer(r)
            cp = pltpu.make_async_remote_copy(
                src_ref=out_ref.at[my_rows, :], dst_ref=out_ref.at[my_rows, :],
                send_sem=send_sems.at[6 + r], recv_sem=recv_sems.at[6 + r],
                device_id=(px, py, pc), device_id_type=MESH)
            cp.start()
            second.append(cp)
        for r in range(1, N_DEV):
            px, py, pc = peer(r)
            theirs = out_ref.at[pl.ds(pl.multiple_of((4 * px + 2 * py + pc) * seg, 8), seg), :]
            pltpu.make_async_remote_copy(
                src_ref=theirs, dst_ref=theirs, send_sem=send_sems.at[6 + r], recv_sem=recv_sems.at[6 + r],
                device_id=(px, py, pc), device_id_type=MESH).wait_recv()
        for cp in second:
            cp.wait_send()

    vm = pl.BlockSpec(memory_space=pltpu.VMEM)
    return pl.pallas_call(
        body, name="all_reduce_small",
        in_specs=[vm], out_specs=vm,
        out_shape=jax.ShapeDtypeStruct((rows, lanes), F32),
        scratch_shapes=[pltpu.VMEM((N_DEV, seg, lanes), F32),
                        pltpu.SemaphoreType.DMA((2 * (N_DEV - 1),)), pltpu.SemaphoreType.DMA((2 * (N_DEV - 1),))],
        compiler_params=pltpu.CompilerParams(vmem_limit_bytes=VMEM_LIMIT),
    )(flat)


def _adamw(w, g, m, v, name):
    rows, cols = w.shape
    tr = _tile(rows, 256, 8)
    c1 = 1.0 - ADAM_B1 ** ADAM_STEP
    c2 = 1.0 - ADAM_B2 ** ADAM_STEP

    def body(w_ref, g_ref, m_ref, v_ref, d_ref, nm_ref, nv_ref):
        gv = g_ref[...]
        nm = ADAM_B1 * m_ref[...] + (1.0 - ADAM_B1) * gv
        nv = ADAM_B2 * v_ref[...] + (1.0 - ADAM_B2) * (gv * gv)
        d_ref[...] = -ADAM_LR * ((nm / c1) / (jnp.sqrt(nv / c2) + ADAM_EPS) + ADAM_WD * w_ref[...])
        nm_ref[...] = nm
        nv_ref[...] = nv

    blk = pl.BlockSpec((tr, cols), lambda i: (i, 0))
    sds = jax.ShapeDtypeStruct((rows, cols), F32)
    return pl.pallas_call(
        body, name=name, grid=(rows // tr,),
        in_specs=[blk] * 4, out_specs=[blk] * 3, out_shape=[sds] * 3,
        compiler_params=_params("parallel"),
    )(w, g, m, v)


SMALL = ("norm_ffn1", "norm_mix", "ssm_a_re", "ssm_a_im", "ssm_log_dt", "ssm_b_re", "ssm_b_im", "ssm_c_re",
         "ssm_c_im", "ssm_d", "ssm_glu_w", "ssm_glu_b", "gm_v_gain", "gm_w_s", "gm_b_s", "gain_ssm_out",
         "gain_gm_out", "norm_ffn2", "norm_final")
WEIGHTS = ("norm_ffn1", "ffn1_w_in", "ffn1_w_out", "norm_mix", "mix_w_in", "ssm_a_re", "ssm_a_im", "ssm_log_dt",
           "ssm_b_re", "ssm_b_im", "ssm_c_re", "ssm_c_im", "ssm_d", "ssm_glu_w", "ssm_glu_b", "gm_v_gain", "gm_w_s",
           "gm_b_s", "gain_ssm_out", "gain_gm_out", "mix_w_out", "norm_ffn2", "ffn2_w_in", "ffn2_w_out", "norm_final")


def _ffn_fwd(x, gain, w_in, w_out, tag, hosted=None):
    if hosted is None:
        h, g, u, a = _ffn_in_fwd(x, gain, w_in, f"{tag}_in")
    else:
        (h, g, u, a), got = _ffn_in_fwd(x, gain, w_in, f"{tag}_in_hosting", comm=hosted[0]())
        hosted[1](got)
    if callable(w_out):
        w_out = w_out()
    out = _matmul(a, w_out, "nn", scale=0.5, res=x, tm=512, tn=1024, tk=4096, name=f"{tag}_out")
    return out, (x, h, g, u, a)


def _ffn_bwd(dout, saved, gain, w_in, w_out, tag, hooks=None, publish=None):
    x, h, g, u, a = saved
    f = g.shape[1]
    hooks = hooks or {}

    def hosted(key, fn, *args, name, **kw):
        if key not in hooks:
            return fn(*args, name=name, **kw)
        make, take = hooks[key]
        *res, got = fn(*args, name=f"{name}_hosting", comm=make(), **kw)
        take(got)
        return res[0] if len(res) == 1 else tuple(res)

    dg, du = hosted("out_dx", _ffn_out_bwd, dout, w_out, g, u, name=f"{tag}_out_dx")
    dw_out = hosted("out_dw", _matmul, a, dout, "tn", scale=0.5, tm=1536, tn=1024, tk=2048, name=f"{tag}_out_dw")
    if publish is not None:
        publish("out", dw_out)
    dw_in = hosted("in_dw_g", _matmul, h, dg, "tn", tm=1024, tn=1536, tk=2048, name=f"{tag}_in_dw_g",
                   out_cols=2 * f)
    dw_in = hosted("in_dw_u", _matmul, h, du, "tn", tm=1024, tn=1536, tk=2048, name=f"{tag}_in_dw_u",
                   out_cols=2 * f, col_off=f, into=dw_in)
    if publish is not None:
        publish("in", dw_in)
    dx, dgain = hosted("in_dx", _proj_in_bwd, [(dg, 0), (du, f)], w_in, x, gain, dout, name=f"{tag}_in_dx")
    return dx, dgain, dw_in, dw_out


def kernel(x, norm_ffn1, ffn1_w_in, ffn1_w_out, norm_mix, mix_w_in, ssm_a_re, ssm_a_im, ssm_log_dt, ssm_b_re, ssm_b_im, ssm_c_re, ssm_c_im, ssm_d, ssm_glu_w, ssm_glu_b, gm_v_gain, gm_w_s, gm_b_s, gain_ssm_out, gain_gm_out, mix_w_out, norm_ffn2, ffn2_w_in, ffn2_w_out, norm_final, loss_target, m_norm_ffn1, m_ffn1_w_in, m_ffn1_w_out, m_norm_mix, m_mix_w_in, m_ssm_a_re, m_ssm_a_im, m_ssm_log_dt, m_ssm_b_re, m_ssm_b_im, m_ssm_c_re, m_ssm_c_im, m_ssm_d, m_ssm_glu_w, m_ssm_glu_b, m_gm_v_gain, m_gm_w_s, m_gm_b_s, m_gain_ssm_out, m_gain_gm_out, m_mix_w_out, m_norm_ffn2, m_ffn2_w_in, m_ffn2_w_out, m_norm_final, v_norm_ffn1, v_ffn1_w_in, v_ffn1_w_out, v_norm_mix, v_mix_w_in, v_ssm_a_re, v_ssm_a_im, v_ssm_log_dt, v_ssm_b_re, v_ssm_b_im, v_ssm_c_re, v_ssm_c_im, v_ssm_d, v_ssm_glu_w, v_ssm_glu_b, v_gm_v_gain, v_gm_w_s, v_gm_b_s, v_gain_ssm_out, v_gain_gm_out, v_mix_w_out, v_norm_ffn2, v_ffn2_w_in, v_ffn2_w_out, v_norm_final):
    wts = dict(norm_ffn1=norm_ffn1, ffn1_w_in=ffn1_w_in, ffn1_w_out=ffn1_w_out, norm_mix=norm_mix, mix_w_in=mix_w_in,
               ssm_a_re=ssm_a_re, ssm_a_im=ssm_a_im, ssm_log_dt=ssm_log_dt, ssm_b_re=ssm_b_re, ssm_b_im=ssm_b_im,
               ssm_c_re=ssm_c_re, ssm_c_im=ssm_c_im, ssm_d=ssm_d, ssm_glu_w=ssm_glu_w, ssm_glu_b=ssm_glu_b,
               gm_v_gain=gm_v_gain, gm_w_s=gm_w_s, gm_b_s=gm_b_s, gain_ssm_out=gain_ssm_out, gain_gm_out=gain_gm_out,
               mix_w_out=mix_w_out, norm_ffn2=norm_ffn2, ffn2_w_in=ffn2_w_in, ffn2_w_out=ffn2_w_out,
               norm_final=norm_final)
    mom = dict(norm_ffn1=m_norm_ffn1, ffn1_w_in=m_ffn1_w_in, ffn1_w_out=m_ffn1_w_out, norm_mix=m_norm_mix,
               mix_w_in=m_mix_w_in, ssm_a_re=m_ssm_a_re, ssm_a_im=m_ssm_a_im, ssm_log_dt=m_ssm_log_dt,
               ssm_b_re=m_ssm_b_re, ssm_b_im=m_ssm_b_im, ssm_c_re=m_ssm_c_re, ssm_c_im=m_ssm_c_im, ssm_d=m_ssm_d,
               ssm_glu_w=m_ssm_glu_w, ssm_glu_b=m_ssm_glu_b, gm_v_gain=m_gm_v_gain, gm_w_s=m_gm_w_s, gm_b_s=m_gm_b_s,
               gain_ssm_out=m_gain_ssm_out, gain_gm_out=m_gain_gm_out, mix_w_out=m_mix_w_out, norm_ffn2=m_norm_ffn2,
               ffn2_w_in=m_ffn2_w_in, ffn2_w_out=m_ffn2_w_out, norm_final=m_norm_final)
    var = dict(norm_ffn1=v_norm_ffn1, ffn1_w_in=v_ffn1_w_in, ffn1_w_out=v_ffn1_w_out, norm_mix=v_norm_mix,
               mix_w_in=v_mix_w_in, ssm_a_re=v_ssm_a_re, ssm_a_im=v_ssm_a_im, ssm_log_dt=v_ssm_log_dt,
               ssm_b_re=v_ssm_b_re, ssm_b_im=v_ssm_b_im, ssm_c_re=v_ssm_c_re, ssm_c_im=v_ssm_c_im, ssm_d=v_ssm_d,
               ssm_glu_w=v_ssm_glu_w, ssm_glu_b=v_ssm_glu_b, gm_v_gain=v_gm_v_gain, gm_w_s=v_gm_w_s, gm_b_s=v_gm_b_s,
               gain_ssm_out=v_gain_ssm_out, gain_gm_out=v_gain_gm_out, mix_w_out=v_mix_w_out, norm_ffn2=v_norm_ffn2,
               ffn2_w_in=v_ffn2_w_in, ffn2_w_out=v_ffn2_w_out, norm_final=v_norm_final)

    nseq, seq, d = x.shape
    n = nseq * seq
    depth = norm_ffn1.shape[0]
    width = gain_ssm_out.shape[1]
    groups = ssm_a_re.shape[1]
    heads = gm_w_s.shape[1]
    core = lax.axis_index("c").astype(jnp.int32).reshape(1)

    is_col = dict(BIG)
    full = {name: [None] * depth for name, _ in BIG}

    def gather_comm(pairs):
        return _gather_comm([wts[nm][l].astype(BF16) for nm, l in pairs], [is_col[nm] for nm, _ in pairs])

    def store(pairs, arrays):
        for (nm, l), w in zip(pairs, arrays):
            full[nm][l] = w

    pairs = [("ffn1_w_in", 0)]
    store(pairs, _run_comm(gather_comm(pairs), "all_gather_first"))

    xs = x.reshape(n, d)
    saved = []
    for l in range(depth):
        pairs = [("ffn1_w_out", l)] + ([("mix_w_in", l), ("mix_w_out", l)] if l == 0 else [])
        x1, s_ffn1 = _ffn_fwd(xs, norm_ffn1[l], full["ffn1_w_in"][l], lambda l=l: full["ffn1_w_out"][l], "ffn1",
                              hosted=(functools.partial(gather_comm, pairs), functools.partial(store, pairs)))
        hm, u_ssm, zgm = _mix_in_fwd(x1, norm_mix[l], full["mix_w_in"][l], width, "mix_in")
        bt_re = jnp.swapaxes(ssm_b_re[l], 1, 2)
        bt_im = jnp.swapaxes(ssm_b_im[l], 1, 2)
        disc_in = (ssm_a_re[l], ssm_a_im[l], ssm_log_dt[l].reshape(groups, 1), bt_re, bt_im)
        lr, li, bbr, bbi = _disc_fwd(*disc_in)
        ops = _ssm_operands(lr, li, bbr, bbi, ssm_c_re[l], ssm_c_im[l], ssm_d[l], ssm_glu_w[l], ssm_glu_b[l])
        u8 = _to_scan_rows(u_ssm, nseq, seq)
        pairs = [("ffn2_w_in", l), ("ffn2_w_out", l)] + ([("ffn1_w_in", l + 1)] if l + 1 < depth else [])
        y8, hb, got = _ssm_fwd(u8, ops, nseq, "s5_fwd", comm=gather_comm(pairs))
        store(pairs, got)
        y_ssm = _from_scan_rows(y8, nseq, seq)
        bias_tile = jnp.broadcast_to(gm_b_s[l].T[:, :, None], (GM_CHUNK, heads, GM_HEAD_DIM)).reshape(GM_CHUNK, width)
        y_gm = _gmlp_fwd(zgm, gm_v_gain[l], gm_w_s[l], bias_tile, "gmlp_fwd")
        ycat = _mixnorm_fwd(y_ssm, y_gm, gain_ssm_out[l], gain_gm_out[l], "mix_out_norm")
        x2 = _matmul(ycat, full["mix_w_out"][l], "nn", res=x1, tm=512, tn=1024, tk=1024, name="mix_out")
        hosted = None
        if l + 1 < depth:
            pairs = [("mix_w_in", l + 1), ("mix_w_out", l + 1)]
            hosted = (functools.partial(gather_comm, pairs), functools.partial(store, pairs))
        x3, s_ffn2 = _ffn_fwd(x2, norm_ffn2[l], full["ffn2_w_in"][l], full["ffn2_w_out"][l], "ffn2", hosted=hosted)
        saved.append(dict(ffn1=s_ffn1, x1=x1, hm=hm, zgm=zgm, disc_in=disc_in, ops=ops, u8=u8, hb=hb, y_ssm=y_ssm,
                          bias_tile=bias_tile, y_gm=y_gm, ycat=ycat, ffn2=s_ffn2))
        xs = x3

    dx, g_norm_final, loss_part = _loss_head(xs, norm_final, loss_target.reshape(n, d))
    big = {name: [None] * depth for name, _ in BIG}
    small = {name: [None] * depth for name in SMALL if name != "norm_final"}
    gpb = GROUPS_PER_BLOCK
    s_blk = STATES_PER_BLOCK
    psum_of, reduced = {}, {}

    def swap_comm(pairs):
        return _pair_exchange_comm([big[nm][l] for nm, l in pairs], [is_col[nm] for nm, _ in pairs])

    def take_swapped(pairs, others):
        for (nm, l), other in zip(pairs, others):
            psum_of[nm, l] = _pair_sum(big[nm][l], other, is_col[nm], core, f"grad_pair_sum_{nm}")

    def send_comm(pairs):
        return _chip_exchange_comm([psum_of[p] for p in pairs], [is_col[nm] for nm, _ in pairs])

    def take_sent(pairs, slots):
        for (nm, l), s in zip(pairs, slots):
            reduced[nm] = _chip_sum(s, core, l, depth, reduced.get(nm), f"grad_chip_sum_{nm}")

    def hosting(make, take, pairs):
        return functools.partial(make, pairs), functools.partial(take, pairs)

    for l in reversed(range(depth)):
        sv = saved[l]
        above = [(nm, l + 1) for nm in ("mix_w_in", "mix_w_out", "ffn1_w_in", "ffn1_w_out")] if l + 1 < depth else []
        dx, small["norm_ffn2"][l], big["ffn2_w_in"][l], big["ffn2_w_out"][l] = _ffn_bwd(
            dx, sv["ffn2"], norm_ffn2[l], full["ffn2_w_in"][l], full["ffn2_w_out"][l], "ffn2",
            hooks={"out_dx": hosting(swap_comm, take_swapped, above)} if above else None)
        mine = [("ffn2_w_in", l), ("ffn2_w_out", l)]
        dycat, got = _matmul(dx, full["mix_w_out"][l], "nt", tm=512, tn=1024, tk=1024, name="mix_out_dx",
                             comm=swap_comm(mine))
        take_swapped(mine, got)
        big["mix_w_out"][l] = _matmul(sv["ycat"], dx, "tn", tm=1024, tn=1024, tk=2048, name="mix_out_dw")
        dy_ssm, dy_gm, small["gain_ssm_out"][l], small["gain_gm_out"][l] = _mixnorm_bwd(
            sv["y_ssm"], sv["y_gm"], gain_ssm_out[l], gain_gm_out[l], dycat, "mix_out_norm_bwd")
        dzgm, small["gm_w_s"][l], dbias_tile, small["gm_v_gain"][l] = _gmlp_bwd(
            sv["zgm"], dy_gm, gm_v_gain[l], gm_w_s[l], sv["bias_tile"], "gmlp_bwd")
        small["gm_b_s"][l] = dbias_tile.reshape(GM_CHUNK, heads, GM_HEAD_DIM).sum(-1).T
        dy8 = _to_scan_rows(dy_ssm, nseq, seq)
        (du8, dlam, db_bd, dct_bd, dd, dw_bd, dbias), got = _ssm_bwd(
            sv["u8"], dy8, sv["hb"], sv["ops"], nseq, "s5_bwd", comm=send_comm(mine + above))
        take_sent(mine + above, got)
        du_ssm = _from_scan_rows(du8, nseq, seq)
        dlr = dlam[:, 0, :s_blk].reshape(groups, SSM_STATE)
        dli = dlam[:, 0, s_blk:].reshape(groups, SSM_STATE)
        dbbr = _block_diag_extract(db_bd[:, :, :s_blk], gpb)
        dbbi = _block_diag_extract(db_bd[:, :, s_blk:], gpb)
        da_re, da_im, dldt, dbt_re, dbt_im = _disc_bwd(*sv["disc_in"], dlr, dli, dbbr, dbbi)
        small["ssm_a_re"][l], small["ssm_a_im"][l], small["ssm_log_dt"][l] = da_re, da_im, dldt.reshape(groups)
        small["ssm_b_re"][l] = jnp.swapaxes(dbt_re, 1, 2)
        small["ssm_b_im"][l] = jnp.swapaxes(dbt_im, 1, 2)
        small["ssm_c_re"][l] = _block_diag_extract(dct_bd[:, :, :s_blk], gpb)
        small["ssm_c_im"][l] = -_block_diag_extract(dct_bd[:, :, s_blk:], gpb)
        small["ssm_d"][l] = dd.reshape(groups, SSM_CH)
        small["ssm_glu_w"][l] = jnp.concatenate(
            [_block_diag_extract(dw_bd[:, :, :LANES], gpb), _block_diag_extract(dw_bd[:, :, LANES:], gpb)], axis=-1)
        small["ssm_glu_b"][l] = jnp.concatenate(
            [dbias[:, 0, :LANES].reshape(groups, SSM_CH), dbias[:, 0, LANES:].reshape(groups, SSM_CH)], axis=-1)
        cols_mi = 3 * width
        dw_mi = _matmul(sv["hm"], du_ssm, "tn", tm=1024, tn=width, tk=2048, name="mix_in_dw_ssm", out_cols=cols_mi)
        big["mix_w_in"][l] = _matmul(sv["hm"], dzgm, "tn", tm=1024, tn=width, tk=2048, name="mix_in_dw_gm",
                                     out_cols=cols_mi, col_off=width, into=dw_mi)
        dx, small["norm_mix"][l] = _proj_in_bwd([(du_ssm, 0), (dzgm, width)], full["mix_w_in"][l], sv["x1"],
                                                norm_mix[l], dx, "mix_in_dx")
        hooks = None
        if l == 0:
            mix, w_out_0, w_in_0 = [("mix_w_in", 0), ("mix_w_out", 0)], [("ffn1_w_out", 0)], [("ffn1_w_in", 0)]
            hooks = {"out_dx": hosting(swap_comm, take_swapped, mix), "out_dw": hosting(send_comm, take_sent, mix),
                     "in_dw_g": hosting(swap_comm, take_swapped, w_out_0),
                     "in_dw_u": hosting(send_comm, take_sent, w_out_0),
                     "in_dx": hosting(swap_comm, take_swapped, w_in_0)}

        def publish(which, dw, l=l):
            big[f"ffn1_w_{which}"][l] = dw

        dx, small["norm_ffn1"][l], big["ffn1_w_in"][l], big["ffn1_w_out"][l] = _ffn_bwd(
            dx, sv["ffn1"], norm_ffn1[l], full["ffn1_w_in"][l], full["ffn1_w_out"][l], "ffn1",
            hooks=hooks, publish=publish)
    grad_x = dx.reshape(nseq, seq, d)

    tail = [("ffn1_w_in", 0)]
    take_sent(tail, _run_comm(send_comm(tail), "grad_chip_exchange_tail"))
    names = [name for name, _ in BIG]
    grads = dict(zip(names, _run_comm(_pair_share_comm([reduced[nm] for nm in names]), "grad_pair_share")))

    pieces = [jnp.stack(small[name]).reshape(-1) for name in SMALL if name != "norm_final"]
    pieces += [g_norm_final.reshape(-1), loss_part.reshape(1)]
    sizes = [p.shape[0] for p in pieces]
    total = sum(sizes)
    rows = -(-total // (LANES * N_DEV * SUBLANES)) * N_DEV * SUBLANES
    pad = rows * LANES - total

    def pack(parts, fill):
        return jnp.concatenate(parts + [jnp.full((pad,), fill, F32)]).reshape(rows, LANES)

    flat_g = _all_reduce_small(pack(pieces, 0.0))
    loss = flat_g.reshape(-1)[total - 1]

    delta, new_m, new_v = {}, {}, {}
    for name, _ in BIG:
        shape = wts[name].shape
        two_d = lambda a: a.reshape(shape[0] * shape[1], shape[2])
        dl, nm, nv = _adamw(two_d(wts[name]), two_d(grads[name]), two_d(mom[name]), two_d(var[name]), f"adamw_{name}")
        delta[name], new_m[name], new_v[name] = dl.reshape(shape), nm.reshape(shape), nv.reshape(shape)
    one = [jnp.zeros((1,), F32)]
    dl, nm, nv = _adamw(pack([wts[k].reshape(-1) for k in SMALL] + one, 0.0), flat_g,
                        pack([mom[k].reshape(-1) for k in SMALL] + one, 0.0),
                        pack([var[k].reshape(-1) for k in SMALL] + one, 1.0), "adamw_small")
    offs = 0
    for name, size in zip(SMALL, sizes[:-1]):
        shape = wts[name].shape
        grads[name] = flat_g.reshape(-1)[offs:offs + size].reshape(shape)
        delta[name] = dl.reshape(-1)[offs:offs + size].reshape(shape)
        new_m[name] = nm.reshape(-1)[offs:offs + size].reshape(shape)
        new_v[name] = nv.reshape(-1)[offs:offs + size].reshape(shape)
        offs += size

    return (loss, grad_x, *[grads[k] for k in WEIGHTS], *[delta[k] for k in WEIGHTS],
            *[new_m[k] for k in WEIGHTS], *[new_v[k] for k in WEIGHTS])
```

```python
import functools
import math

import jax
import jax.numpy as jnp
from jax import lax
from jax.experimental import pallas as pl
from jax.experimental.pallas import tpu as pltpu

F32 = jnp.float32
BF16 = jnp.bfloat16
MESH = pl.DeviceIdType.MESH

EPS = 1e-6
SSM_CH = 16
SSM_STATE = 64
GM_CHUNK = 128
GM_HEAD_DIM = 128
SUBLANES = 8
LANES = 128
GROUPS_PER_BLOCK = LANES // SSM_CH
STATES_PER_BLOCK = GROUPS_PER_BLOCK * SSM_STATE
SSM_TIME_CHUNK = 128
N_CHIPS = 4
N_DEV = 8

ADAM_LR = 0.001
ADAM_B1 = 0.9
ADAM_B2 = 0.999
ADAM_EPS = 1e-08
ADAM_WD = 0.01
ADAM_STEP = 10

VMEM_LIMIT = 56 * 1024 * 1024


def _tile(dim, pref, align):
    best = None
    t = align
    while t <= min(dim, pref):
        if dim % t == 0:
            best = t
        t += align
    return best if best is not None else dim


def _params(*sem):
    return pltpu.CompilerParams(dimension_semantics=sem, vmem_limit_bytes=VMEM_LIMIT)


def _gelu(x):
    c = math.sqrt(2.0 / math.pi)
    return 0.5 * x * (1.0 + jnp.tanh(c * (x + 0.044715 * x * x * x)))


def _gelu_and_grad(x):
    c = math.sqrt(2.0 / math.pi)
    t = jnp.tanh(c * (x + 0.044715 * x * x * x))
    g = 0.5 * x * (1.0 + t)
    dg = 0.5 * (1.0 + t) + 0.5 * x * (1.0 - t * t) * c * (1.0 + 3.0 * 0.044715 * x * x)
    return g, dg


def _sigmoid(x):
    return 0.5 * jnp.tanh(0.5 * x) + 0.5


def _matmul(a, b, mode, *, out_dtype=F32, scale=1.0, res=None, tm=512, tn=1024, tk=1024, name="mm",
            out_cols=None, col_off=0, into=None, comm=None):
    if mode == "nn":
        (m, k), (k2, n) = a.shape, b.shape
    elif mode == "nt":
        (m, k), (n, k2) = a.shape, b.shape
    else:
        (k, m), (k2, n) = a.shape, b.shape
    assert k == k2, (a.shape, b.shape, mode)
    tm = _tile(m, tm, 16 if mode != "tn" else LANES)
    tn = _tile(n, tn, LANES)
    tk = _tile(k, tk, LANES if mode != "tn" else 16)
    nk = k // tk
    grid = (m // tm, n // tn, nk)
    if mode == "nn":
        a_spec = pl.BlockSpec((tm, tk), lambda i, j, kk: (i, kk))
        b_spec = pl.BlockSpec((tk, tn), lambda i, j, kk: (kk, j))
        dims = (((1,), (0,)), ((), ()))
    elif mode == "nt":
        a_spec = pl.BlockSpec((tm, tk), lambda i, j, kk: (i, kk))
        b_spec = pl.BlockSpec((tn, tk), lambda i, j, kk: (j, kk))
        dims = (((1,), (1,)), ((), ()))
    else:
        a_spec = pl.BlockSpec((tk, tm), lambda i, j, kk: (kk, i))
        b_spec = pl.BlockSpec((tk, tn), lambda i, j, kk: (kk, j))
        dims = (((0,), (0,)), ((), ()))
    assert col_off % tn == 0
    off = col_off // tn
    r_spec = pl.BlockSpec((tm, tn), lambda i, j, kk: (i, j))
    o_spec = pl.BlockSpec((tm, tn), lambda i, j, kk: (i, j + off))
    has_res = res is not None
    has_into = into is not None

    def body(*refs):
        refs = list(refs)
        a_ref, b_ref = refs[:2]
        pos = 2
        r_ref = None
        if has_res:
            r_ref = refs[pos]
            pos += 1
        if has_into:
            pos += 1
        o_ref = refs[pos]
        acc_ref = refs[pos + 1] if nk > 1 else None
        part = lax.dot_general(a_ref[...].astype(BF16), b_ref[...].astype(BF16), dims,
                               preferred_element_type=F32)

        def finish(r):
            if scale != 1.0:
                r = r * scale
            if has_res:
                r = r + r_ref[...].astype(F32)
            o_ref[...] = r.astype(o_ref.dtype)

        if nk == 1:
            finish(part)
        else:
            kk = pl.program_id(2)

            @pl.when(kk == 0)
            def _():
                acc_ref[...] = part

            @pl.when(kk > 0)
            def _():
                acc_ref[...] += part

            @pl.when(kk == nk - 1)
            def _():
                finish(acc_ref[...])

    in_specs = [a_spec, b_spec]
    args = [a, b]
    if has_res:
        in_specs.append(r_spec)
        args.append(res)
    aliases = {}
    if has_into:
        in_specs.append(pl.BlockSpec(memory_space=pl.ANY))
        args.append(into)
        aliases = {len(args) - 1: 0}
    (out,), comm_outs = _call(
        body, name=name, grid=grid, in_specs=in_specs, out_specs=[o_spec],
        out_shape=[jax.ShapeDtypeStruct((m, n if out_cols is None else out_cols), out_dtype)],
        scratch_shapes=[pltpu.VMEM((tm, tn), F32)] if nk > 1 else [],
        aliases=aliases, semantics=("parallel", "parallel", "arbitrary"), args=args, comm=comm)
    return out if comm is None else (out, comm_outs)


class _Comm:
    def __init__(self, ins, outs, sems, start, finish, alias=None, middle=None):
        self.ins, self.outs, self.sems, self.start, self.finish = list(ins), list(outs), list(sems), start, finish
        self.alias = dict(alias or {})
        self.middle = middle


def _call(body, *, name, grid, in_specs, out_specs, out_shape, args, scratch_shapes=(), semantics=(), aliases=None,
          comm=None):
    in_specs, out_specs, out_shape = list(in_specs), list(out_specs), list(out_shape)
    scratch_shapes = list(scratch_shapes)
    aliases = dict(aliases or {})
    if comm is None:
        outs = pl.pallas_call(
            body, name=name, grid=grid, in_specs=in_specs, out_specs=out_specs, out_shape=out_shape,
            scratch_shapes=scratch_shapes, input_output_aliases=aliases, compiler_params=_params(*semantics),
        )(*args)
        return list(outs), []
    n_in, n_out, n_scr = len(in_specs), len(out_specs), len(scratch_shapes)
    c_in, c_out = len(comm.ins), len(comm.outs)
    for ci, co in comm.alias.items():
        aliases[n_in + ci] = n_out + co

    def hosted(*refs):
        refs = list(refs)
        ins, cins = refs[:n_in], refs[n_in:n_in + c_in]
        p = n_in + c_in
        outs, couts = refs[p:p + n_out], refs[p + n_out:p + n_out + c_out]
        p += n_out + c_out
        scr, sems = refs[p:p + n_scr], refs[p + n_scr:]
        ids = [pl.program_id(a) for a in range(len(grid))]
        first = functools.reduce(jnp.logical_and, [i == 0 for i in ids])
        last = functools.reduce(jnp.logical_and, [i == g - 1 for i, g in zip(ids, grid)])

        total = math.prod(grid)
        late = comm.middle is not None and total >= 4

        @pl.when(first)
        def _():
            comm.start(cins, couts, sems)

        if late:
            flat = functools.reduce(lambda acc, ig: acc * ig[1] + ig[0], zip(ids, grid), 0)

            @pl.when(flat == (3 * total) // 4)
            def _():
                comm.middle(cins, couts, sems)

        body(*ins, *outs, *scr)

        @pl.when(last)
        def _():
            if comm.middle is not None and not late:
                comm.middle(cins, couts, sems)
            comm.finish(cins, couts, sems)

    any_spec = pl.BlockSpec(memory_space=pl.ANY)
    outs = pl.pallas_call(
        hosted, name=name, grid=grid, in_specs=in_specs + [any_spec] * c_in, out_specs=out_specs + [any_spec] * c_out,
        out_shape=out_shape + comm.outs, scratch_shapes=scratch_shapes + comm.sems, input_output_aliases=aliases,
        compiler_params=_params(*(["arbitrary"] * len(grid))),
    )(*args, *comm.ins)
    return list(outs[:n_out]), list(outs[n_out:])


def _run_comm(comm, name):
    c_in, c_out = len(comm.ins), len(comm.outs)

    def body(*refs):
        refs = list(refs)
        cins, couts, sems = refs[:c_in], refs[c_in:c_in + c_out], refs[c_in + c_out:]
        comm.start(cins, couts, sems)
        if comm.middle is not None:
            comm.middle(cins, couts, sems)
        comm.finish(cins, couts, sems)

    any_spec = pl.BlockSpec(memory_space=pl.ANY)
    return list(pl.pallas_call(
        body, name=name, in_specs=[any_spec] * c_in, out_specs=[any_spec] * c_out, out_shape=comm.outs,
        scratch_shapes=comm.sems, input_output_aliases=comm.alias,
    )(*comm.ins))


def _loss_head(x, gain, target):
    n, d = x.shape
    tm = _tile(n, 512, 8)
    steps = n // tm

    def body(x_ref, g_ref, t_ref, dx_ref, dg_ref, loss_ref, acc_ref, lacc_ref):
        i = pl.program_id(0)
        xv = x_ref[...]
        g = g_ref[...]
        r = lax.rsqrt(jnp.mean(xv * xv, axis=-1, keepdims=True) + EPS)
        xh = xv * r
        err = xh * g - t_ref[...]
        dy = err * (1.0 / d)
        dyg = dy * g
        mean = jnp.mean(dyg * xh, axis=-1, keepdims=True)
        dx_ref[...] = r * (dyg - xh * mean)
        part = jnp.sum((dy * xh).reshape(tm // SUBLANES, SUBLANES, d), axis=0)
        lpart = jnp.sum((err * err).reshape(tm // SUBLANES, SUBLANES, d), axis=0)

        @pl.when(i == 0)
        def _():
            acc_ref[...] = part
            lacc_ref[...] = lpart

        @pl.when(i > 0)
        def _():
            acc_ref[...] += part
            lacc_ref[...] += lpart

        @pl.when(i == steps - 1)
        def _():
            dg_ref[...] = jnp.sum(acc_ref[...], axis=0, keepdims=True)
            tot = jnp.sum(jnp.sum(lacc_ref[...], axis=0, keepdims=True), axis=1, keepdims=True)
            loss_ref[...] = jnp.broadcast_to(tot * (0.5 / d), loss_ref.shape)

    row = pl.BlockSpec((tm, d), lambda i: (i, 0))
    vec = pl.BlockSpec((1, d), lambda i: (0, 0))
    dx, dg, loss = pl.pallas_call(
        body, name="loss_head", grid=(steps,),
        in_specs=[row, vec, row],
        out_specs=[row, vec, pl.BlockSpec((1, LANES), lambda i: (0, 0))],
        out_shape=[jax.ShapeDtypeStruct((n, d), F32), jax.ShapeDtypeStruct((1, d), F32),
                   jax.ShapeDtypeStruct((1, LANES), F32)],
        scratch_shapes=[pltpu.VMEM((SUBLANES, d), F32), pltpu.VMEM((SUBLANES, d), F32)],
        compiler_params=_params("arbitrary"),
    )(x, gain.reshape(1, d), target)
    return dx, dg.reshape(d), loss[0, 0]


def _rms_rows(xv):
    return lax.rsqrt(jnp.mean(xv * xv, axis=-1, keepdims=True) + EPS)


def _ffn_in_fwd(x, gain, w_in, name, comm=None):
    n, d = x.shape
    f = w_in.shape[1] // 2
    tm = _tile(n, 256, 16)
    tn = _tile(f, 4096, LANES)
    nj = f // tn

    def body(x_ref, gain_ref, wg_ref, wu_ref, h_ref, g_ref, u_ref, a_ref):
        @pl.when(pl.program_id(1) == 0)
        def _():
            xv = x_ref[...]
            h_ref[...] = (xv * _rms_rows(xv) * gain_ref[...]).astype(h_ref.dtype)

        h = h_ref[...]
        g = jnp.dot(h, wg_ref[...], preferred_element_type=F32)
        u = jnp.dot(h, wu_ref[...], preferred_element_type=F32)
        g_ref[...] = g.astype(g_ref.dtype)
        u_ref[...] = u.astype(u_ref.dtype)
        a_ref[...] = (g * _sigmoid(g) * u).astype(a_ref.dtype)

    row = pl.BlockSpec((tm, d), lambda i, j: (i, 0))
    tile = pl.BlockSpec((tm, tn), lambda i, j: (i, j))
    act = jax.ShapeDtypeStruct((n, f), BF16)
    outs, comm_outs = _call(
        body, name=name, grid=(n // tm, nj),
        in_specs=[row, pl.BlockSpec((1, d), lambda i, j: (0, 0)),
                  pl.BlockSpec((d, tn), lambda i, j: (0, j)), pl.BlockSpec((d, tn), lambda i, j: (0, j + nj))],
        out_specs=[row, tile, tile, tile],
        out_shape=[jax.ShapeDtypeStruct((n, d), BF16), act, act, act],
        semantics=("parallel", "arbitrary"), args=(x, gain.reshape(1, d), w_in, w_in), comm=comm)
    return outs if comm is None else (outs, comm_outs)


def _ffn_out_bwd(dout, w_out, g, u, name, comm=None):
    n, d = dout.shape
    f = w_out.shape[0]
    tm = _tile(n, 256, 16)
    tn = _tile(f, 4096, LANES)

    def body(d_ref, w_ref, g_ref, u_ref, dg_ref, du_ref):
        da = 0.5 * lax.dot_general(d_ref[...].astype(BF16), w_ref[...], (((1,), (1,)), ((), ())),
                                   preferred_element_type=F32)
        gv = g_ref[...].astype(F32)
        s = _sigmoid(gv)
        dg_ref[...] = (da * u_ref[...].astype(F32) * (s * (1.0 + gv * (1.0 - s)))).astype(dg_ref.dtype)
        du_ref[...] = (da * gv * s).astype(du_ref.dtype)

    tile = pl.BlockSpec((tm, tn), lambda i, j: (i, j))
    act = jax.ShapeDtypeStruct((n, f), BF16)
    outs, comm_outs = _call(
        body, name=name, grid=(n // tm, f // tn),
        in_specs=[pl.BlockSpec((tm, d), lambda i, j: (i, 0)), pl.BlockSpec((tn, d), lambda i, j: (j, 0)), tile, tile],
        out_specs=[tile, tile], out_shape=[act, act],
        semantics=("parallel", "parallel"), args=(dout, w_out, g, u), comm=comm)
    return outs if comm is None else (outs, comm_outs)


def _proj_in_bwd(parts, w, x, gain, dres, name, comm=None):
    n, d = x.shape
    tm = _tile(n, 256, 8)
    steps = n // tm
    np_ = len(parts)
    offs = [off for _, off in parts]
    widths = [a.shape[1] for a, _ in parts]

    def body(*refs):
        a_refs = refs[:np_]
        w_ref, x_ref, g_ref, dr_ref, dx_ref, dg_ref, acc_ref = refs[np_:]
        i = pl.program_id(0)
        dh = None
        for a_ref, off, kp in zip(a_refs, offs, widths):
            part = lax.dot_general(a_ref[...].astype(BF16), w_ref[:, off:off + kp], (((1,), (1,)), ((), ())),
                                   preferred_element_type=F32)
            dh = part if dh is None else dh + part
        xv = x_ref[...]
        r = _rms_rows(xv)
        xh = xv * r
        dyg = dh * g_ref[...]
        mean = jnp.mean(dyg * xh, axis=-1, keepdims=True)
        dx_ref[...] = dr_ref[...] + r * (dyg - xh * mean)
        part = jnp.sum((dh * xh).reshape(tm // SUBLANES, SUBLANES, d), axis=0)

        @pl.when(i == 0)
        def _():
            acc_ref[...] = part

        @pl.when(i > 0)
        def _():
            acc_ref[...] += part

        @pl.when(i == steps - 1)
        def _():
            dg_ref[...] = jnp.sum(acc_ref[...], axis=0, keepdims=True)

    row = pl.BlockSpec((tm, d), lambda i: (i, 0))
    vec = pl.BlockSpec((1, d), lambda i: (0, 0))
    (dx, dg), comm_outs = _call(
        body, name=name, grid=(steps,),
        in_specs=[pl.BlockSpec((tm, kp), lambda i: (i, 0)) for kp in widths]
        + [pl.BlockSpec(w.shape, lambda i: (0, 0)), row, vec, row],
        out_specs=[row, vec],
        out_shape=[jax.ShapeDtypeStruct((n, d), F32), jax.ShapeDtypeStruct((1, d), F32)],
        scratch_shapes=[pltpu.VMEM((SUBLANES, d), F32)],
        semantics=("arbitrary",), args=(*[a for a, _ in parts], w, x, gain.reshape(1, d), dres), comm=comm)
    return (dx, dg.reshape(d)) if comm is None else (dx, dg.reshape(d), comm_outs)


def _mix_in_fwd(x, gain, w, width, name):
    n, d = x.shape
    cols = w.shape[1]
    tm = _tile(n, 512, 16)

    def body(x_ref, gain_ref, w_ref, h_ref, u_ref, z_ref):
        xv = x_ref[...]
        h = (xv * _rms_rows(xv) * gain_ref[...]).astype(h_ref.dtype)
        h_ref[...] = h
        z = jnp.dot(h, w_ref[...], preferred_element_type=F32)
        u_ref[...] = z[:, 0:width]
        z_ref[...] = z[:, width:cols]

    row = pl.BlockSpec((tm, d), lambda i: (i, 0))
    return pl.pallas_call(
        body, name=name, grid=(n // tm,),
        in_specs=[row, pl.BlockSpec((1, d), lambda i: (0, 0)), pl.BlockSpec((d, cols), lambda i: (0, 0))],
        out_specs=[row, pl.BlockSpec((tm, width), lambda i: (i, 0)), pl.BlockSpec((tm, cols - width), lambda i: (i, 0))],
        out_shape=[jax.ShapeDtypeStruct((n, d), BF16), jax.ShapeDtypeStruct((n, width), F32),
                   jax.ShapeDtypeStruct((n, cols - width), F32)],
        compiler_params=_params("parallel"),
    )(x, gain.reshape(1, d), w)


def _tril_mask():
    t = lax.broadcasted_iota(jnp.int32, (GM_CHUNK, GM_CHUNK), 0)
    s = lax.broadcasted_iota(jnp.int32, (GM_CHUNK, GM_CHUNK), 1)
    return s <= t


def _gmlp_fwd(zgm, v_gain, w_s, bias_tile, name):
    n, w2 = zgm.shape
    w = w2 // 2
    heads = w // GM_HEAD_DIM
    tm = _tile(n, 512, GM_CHUNK)
    nq = tm // GM_CHUNK

    def body(u_ref, v_ref, gain_ref, w_ref, b_ref, o_ref):
        mask = _tril_mask()
        ug = _gelu(u_ref[...])
        vg = _gelu(v_ref[...])
        for h in range(heads):
            cols = slice(h * GM_HEAD_DIM, (h + 1) * GM_HEAD_DIM)
            vh = vg[:, cols]
            r = lax.rsqrt(jnp.mean(vh * vh, axis=-1, keepdims=True) + EPS)
            vn = (vh * r * gain_ref[:, cols]).astype(BF16)
            wm = jnp.where(mask, w_ref[h], 0.0).astype(BF16)
            for q in range(nq):
                rows = slice(q * GM_CHUNK, (q + 1) * GM_CHUNK)
                s = jnp.dot(wm, vn[rows], preferred_element_type=F32) + b_ref[:, cols]
                o_ref[rows, cols] = ug[rows, cols] * s

    return pl.pallas_call(
        body, name=name, grid=(n // tm,),
        in_specs=[pl.BlockSpec((tm, w), lambda i: (i, 0)), pl.BlockSpec((tm, w), lambda i: (i, 1)),
                  pl.BlockSpec((1, w), lambda i: (0, 0)),
                  pl.BlockSpec((heads, GM_CHUNK, GM_CHUNK), lambda i: (0, 0, 0)),
                  pl.BlockSpec((GM_CHUNK, w), lambda i: (0, 0))],
        out_specs=pl.BlockSpec((tm, w), lambda i: (i, 0)),
        out_shape=jax.ShapeDtypeStruct((n, w), F32),
        compiler_params=_params("parallel"),
    )(zgm, zgm, v_gain.reshape(1, w), w_s, bias_tile)


def _gmlp_bwd(zgm, dy, v_gain, w_s, bias_tile, name):
    n, w2 = zgm.shape
    w = w2 // 2
    heads = w // GM_HEAD_DIM
    tm = _tile(n, 512, GM_CHUNK)
    nq = tm // GM_CHUNK
    steps = n // tm

    def body(z_ref, dy_ref, gain_ref, w_ref, b_ref, dz_ref, dw_ref, db_ref, dgain_ref):
        i = pl.program_id(0)
        mask = _tril_mask()

        @pl.when(i == 0)
        def _():
            dw_ref[...] = jnp.zeros_like(dw_ref)
            db_ref[...] = jnp.zeros_like(db_ref)
            dgain_ref[...] = jnp.zeros_like(dgain_ref)

        ug, dug_du = _gelu_and_grad(z_ref[:, 0:w])
        vg, dvg_dv = _gelu_and_grad(z_ref[:, w:w2])
        dyv = dy_ref[...]
        for h in range(heads):
            cols = slice(h * GM_HEAD_DIM, (h + 1) * GM_HEAD_DIM)
            vh = vg[:, cols]
            r = lax.rsqrt(jnp.mean(vh * vh, axis=-1, keepdims=True) + EPS)
            vhat = vh * r
            gain = gain_ref[:, cols]
            vn = (vhat * gain).astype(BF16)
            wm = jnp.where(mask, w_ref[h], 0.0).astype(BF16)
            dvn_parts = []
            for q in range(nq):
                rows = slice(q * GM_CHUNK, (q + 1) * GM_CHUNK)
                s = jnp.dot(wm, vn[rows], preferred_element_type=F32) + b_ref[:, cols]
                dyq = dyv[rows, cols]
                dz_ref[rows, cols] = dyq * s * dug_du[rows, cols]
                ds = dyq * ug[rows, cols]
                db_ref[:, cols] += ds
                dsb = ds.astype(BF16)
                dw_ref[h] += lax.dot_general(dsb, vn[rows], (((1,), (1,)), ((), ())), preferred_element_type=F32)
                dvn_parts.append(lax.dot_general(wm, dsb, (((0,), (0,)), ((), ())), preferred_element_type=F32))
            dvn = jnp.concatenate(dvn_parts, axis=0) if nq > 1 else dvn_parts[0]
            dgain_ref[:, cols] += jnp.sum(dvn * vhat, axis=0, keepdims=True)
            dvhat = dvn * gain
            mean = jnp.mean(dvhat * vhat, axis=-1, keepdims=True)
            dz_ref[:, w + h * GM_HEAD_DIM:w + (h + 1) * GM_HEAD_DIM] = r * (dvhat - vhat * mean) * dvg_dv[:, cols]

        @pl.when(i == steps - 1)
        def _():
            for h in range(heads):
                dw_ref[h] = jnp.where(mask, dw_ref[h], 0.0)

    dz, dw, db, dgain = pl.pallas_call(
        body, name=name, grid=(steps,),
        in_specs=[pl.BlockSpec((tm, w2), lambda i: (i, 0)), pl.BlockSpec((tm, w), lambda i: (i, 0)),
                  pl.BlockSpec((1, w), lambda i: (0, 0)),
                  pl.BlockSpec((heads, GM_CHUNK, GM_CHUNK), lambda i: (0, 0, 0)),
                  pl.BlockSpec((GM_CHUNK, w), lambda i: (0, 0))],
        out_specs=[pl.BlockSpec((tm, w2), lambda i: (i, 0)),
                   pl.BlockSpec((heads, GM_CHUNK, GM_CHUNK), lambda i: (0, 0, 0)),
                   pl.BlockSpec((GM_CHUNK, w), lambda i: (0, 0)),
                   pl.BlockSpec((1, w), lambda i: (0, 0))],
        out_shape=[jax.ShapeDtypeStruct((n, w2), F32), jax.ShapeDtypeStruct((heads, GM_CHUNK, GM_CHUNK), F32),
                   jax.ShapeDtypeStruct((GM_CHUNK, w), F32), jax.ShapeDtypeStruct((1, w), F32)],
        compiler_params=_params("arbitrary"),
    )(zgm, dy, v_gain.reshape(1, w), w_s, bias_tile)
    return dz, dw, db, dgain.reshape(w)


def _mixnorm_fwd(y_ssm, y_gm, g1, g2, name):
    n, w = y_ssm.shape
    tm = _tile(n, 512, 16)

    def body(a_ref, b_ref, g1_ref, g2_ref, o_ref):
        for src, g_ref, lo in ((a_ref, g1_ref, 0), (b_ref, g2_ref, w)):
            v = src[...]
            r = lax.rsqrt(jnp.mean(v * v, axis=-1, keepdims=True) + EPS)
            o_ref[:, lo:lo + w] = (v * r * g_ref[...]).astype(o_ref.dtype)

    row = pl.BlockSpec((tm, w), lambda i: (i, 0))
    vec = pl.BlockSpec((1, w), lambda i: (0, 0))
    return pl.pallas_call(
        body, name=name, grid=(n // tm,),
        in_specs=[row, row, vec, vec], out_specs=pl.BlockSpec((tm, 2 * w), lambda i: (i, 0)),
        out_shape=jax.ShapeDtypeStruct((n, 2 * w), BF16),
        compiler_params=_params("parallel"),
    )(y_ssm, y_gm, g1.reshape(1, w), g2.reshape(1, w))


def _mixnorm_bwd(y_ssm, y_gm, g1, g2, dycat, name):
    n, w = y_ssm.shape
    tm = _tile(n, 512, 8)
    steps = n // tm

    def body(a_ref, b_ref, g1_ref, g2_ref, d_ref, da_ref, db_ref, dg1_ref, dg2_ref):
        i = pl.program_id(0)

        @pl.when(i == 0)
        def _():
            dg1_ref[...] = jnp.zeros_like(dg1_ref)
            dg2_ref[...] = jnp.zeros_like(dg2_ref)

        for src, g_ref, lo, dst, dg_ref in ((a_ref, g1_ref, 0, da_ref, dg1_ref), (b_ref, g2_ref, w, db_ref, dg2_ref)):
            v = src[...]
            dh = d_ref[:, lo:lo + w]
            r = lax.rsqrt(jnp.mean(v * v, axis=-1, keepdims=True) + EPS)
            vh = v * r
            dyg = dh * g_ref[...]
            mean = jnp.mean(dyg * vh, axis=-1, keepdims=True)
            dst[...] = r * (dyg - vh * mean)
            dg_ref[...] += jnp.sum(dh * vh, axis=0, keepdims=True)

    row = pl.BlockSpec((tm, w), lambda i: (i, 0))
    vec = pl.BlockSpec((1, w), lambda i: (0, 0))
    da, db, dg1, dg2 = pl.pallas_call(
        body, name=name, grid=(steps,),
        in_specs=[row, row, vec, vec, pl.BlockSpec((tm, 2 * w), lambda i: (i, 0))],
        out_specs=[row, row, vec, vec],
        out_shape=[jax.ShapeDtypeStruct((n, w), F32), jax.ShapeDtypeStruct((n, w), F32),
                   jax.ShapeDtypeStruct((1, w), F32), jax.ShapeDtypeStruct((1, w), F32)],
        compiler_params=_params("arbitrary"),
    )(y_ssm, y_gm, g1.reshape(1, w), g2.reshape(1, w), dycat)
    return da, db, dg1.reshape(w), dg2.reshape(w)


def _discretise(a_re, a_im, log_dt, bt_re, bt_im):
    dt = jnp.exp(log_dt)
    e = jnp.exp(a_re * dt)
    ang = a_im * dt
    lr = e * jnp.cos(ang)
    li = e * jnp.sin(ang)
    den = a_re * a_re + a_im * a_im
    cr = ((lr - 1.0) * a_re + li * a_im) / den
    ci = (li * a_re - (lr - 1.0) * a_im) / den
    cr3 = cr[:, None, :]
    ci3 = ci[:, None, :]
    return lr, li, cr3 * bt_re - ci3 * bt_im, cr3 * bt_im + ci3 * bt_re


def _disc_fwd(a_re, a_im, log_dt, bt_re, bt_im):
    g, p = a_re.shape
    c = bt_re.shape[1]

    def body(are_ref, aim_ref, ldt_ref, bre_ref, bim_ref, lr_ref, li_ref, bbr_ref, bbi_ref):
        lr, li, bbr, bbi = _discretise(are_ref[...], aim_ref[...], ldt_ref[...], bre_ref[...], bim_ref[...])
        lr_ref[...] = lr
        li_ref[...] = li
        bbr_ref[...] = bbr
        bbi_ref[...] = bbi

    return pl.pallas_call(
        body, name="s5_discretise",
        out_shape=[jax.ShapeDtypeStruct((g, p), F32), jax.ShapeDtypeStruct((g, p), F32),
                   jax.ShapeDtypeStruct((g, c, p), F32), jax.ShapeDtypeStruct((g, c, p), F32)],
    )(a_re, a_im, log_dt, bt_re, bt_im)


def _disc_bwd(a_re, a_im, log_dt, bt_re, bt_im, dlr, dli, dbbr, dbbi):
    g, p = a_re.shape
    c = bt_re.shape[1]

    def body(are_ref, aim_ref, ldt_ref, bre_ref, bim_ref, dlr_ref, dli_ref, dbbr_ref, dbbi_ref,
             dare_ref, daim_ref, dldt_ref, dbre_ref, dbim_ref):
        _, vjp = jax.vjp(_discretise, are_ref[...], aim_ref[...], ldt_ref[...], bre_ref[...], bim_ref[...])
        dare, daim, dldt, dbre, dbim = vjp((dlr_ref[...], dli_ref[...], dbbr_ref[...], dbbi_ref[...]))
        dare_ref[...] = dare
        daim_ref[...] = daim
        dldt_ref[...] = dldt
        dbre_ref[...] = dbre
        dbim_ref[...] = dbim

    return pl.pallas_call(
        body, name="s5_discretise_bwd",
        out_shape=[jax.ShapeDtypeStruct((g, p), F32), jax.ShapeDtypeStruct((g, p), F32),
                   jax.ShapeDtypeStruct((g, 1), F32),
                   jax.ShapeDtypeStruct((g, c, p), F32), jax.ShapeDtypeStruct((g, c, p), F32)],
    )(a_re, a_im, log_dt, bt_re, bt_im, dlr, dli, dbbr, dbbi)


def _block_diag(w, nb):
    g, a, b = w.shape
    gpb = g // nb
    eye = jnp.eye(gpb, dtype=w.dtype)
    w4 = w.reshape(nb, gpb, a, b)
    return jnp.einsum("ngab,gh->ngahb", w4, eye).reshape(nb, gpb * a, gpb * b)


def _block_diag_extract(m, gpb):
    nb, ga, gb = m.shape
    a, b = ga // gpb, gb // gpb
    m5 = m.reshape(nb, gpb, a, gpb, b)
    idx = jnp.arange(gpb)
    return m5[:, idx, :, idx, :].transpose(1, 0, 2, 3).reshape(nb * gpb, a, b)


def _ssm_operands(lr, li, bbr, bbi, c_re, c_im, d_skip, glu_w, glu_b):
    g = lr.shape[0]
    nb = g // GROUPS_PER_BLOCK
    s = STATES_PER_BLOCK
    lam = jnp.concatenate([lr.reshape(nb, 1, s), li.reshape(nb, 1, s)], axis=-1)
    b_bd = jnp.concatenate([_block_diag(bbr, nb), _block_diag(bbi, nb)], axis=-1)
    ct_re = jnp.swapaxes(c_re, 1, 2)
    ct_im = jnp.swapaxes(c_im, 1, 2)
    c_bd = jnp.concatenate([_block_diag(ct_re, nb), -_block_diag(ct_im, nb)], axis=1)
    dsk = d_skip.reshape(nb, 1, LANES)
    w_bd = jnp.concatenate([_block_diag(glu_w[:, :, :SSM_CH], nb), _block_diag(glu_w[:, :, SSM_CH:], nb)], axis=-1)
    bias = jnp.concatenate([glu_b[:, :SSM_CH].reshape(nb, 1, LANES), glu_b[:, SSM_CH:].reshape(nb, 1, LANES)], axis=-1)
    return lam, b_bd.astype(BF16), c_bd.astype(BF16), dsk, w_bd.astype(BF16), bias


def _roll_rows(v, shift):
    return v if shift % SUBLANES == 0 else pltpu.roll(v, shift % SUBLANES, 0)


def _scan_chunk_rows(seq, nseq):
    return _tile(seq, max(8 * SSM_TIME_CHUNK // nseq, 8), max(SUBLANES // nseq, 1) * 8)


def _ssm_fwd(u8, ops, nseq, name, comm=None):
    lam, b_bd, c_bd, dsk, w_bd, bias = ops
    rows_total, w = u8.shape
    seq = rows_total // nseq
    nb = w // LANES
    s = STATES_PER_BLOCK
    tc = _scan_chunk_rows(seq, nseq)
    nk = seq // tc
    rows = tc * nseq
    stages = SUBLANES // nseq

    def body(u_ref, lam_ref, b_ref, c_ref, d_ref, w_ref, bias_ref, y_ref, hb_ref, buf, st):
        k = pl.program_id(1)

        @pl.when(k == 0)
        def _():
            st[...] = jnp.zeros_like(st)

        hb_ref[...] = st[...]
        u = u_ref[...]
        buf[...] = jnp.dot(u.astype(BF16), b_ref[0], preferred_element_type=F32)
        lr = jnp.broadcast_to(lam_ref[0, :, 0:s], (SUBLANES, s))
        li = jnp.broadcast_to(lam_ref[0, :, s:2 * s], (SUBLANES, s))
        row = lax.broadcasted_iota(jnp.int32, (SUBLANES, s), 0)

        def step(i, carry):
            pr, pi = carry
            r0 = pl.multiple_of(i * SUBLANES, SUBLANES)
            br = buf[pl.ds(r0, SUBLANES), 0:s]
            bi = buf[pl.ds(r0, SUBLANES), s:2 * s]
            outr = outi = None
            for j in range(stages):
                rr = _roll_rows(pr, nseq)
                ri = _roll_rows(pi, nseq)
                pr = lr * rr - li * ri + br
                pi = lr * ri + li * rr + bi
                outr = pr if j == 0 else jnp.where(row >= j * nseq, pr, outr)
                outi = pi if j == 0 else jnp.where(row >= j * nseq, pi, outi)
            buf[pl.ds(r0, SUBLANES), 0:s] = outr
            buf[pl.ds(r0, SUBLANES), s:2 * s] = outi
            return outr, outi

        hr, hi = lax.fori_loop(0, rows // SUBLANES, step, (st[:, 0:s], st[:, s:2 * s]), unroll=2)
        st[:, 0:s] = hr
        st[:, s:2 * s] = hi
        y = jnp.dot(buf[...].astype(BF16), c_ref[0], preferred_element_type=F32) + d_ref[0] * u
        z = jnp.dot(_gelu(y).astype(BF16), w_ref[0], preferred_element_type=F32) + bias_ref[0]
        y_ref[...] = z[:, 0:LANES] * _sigmoid(z[:, LANES:2 * LANES])

    blk = lambda shape: pl.BlockSpec(shape, lambda b, k: (b, 0, 0))
    (y8, hb), comm_outs = _call(
        body, name=name, grid=(nb, nk),
        in_specs=[pl.BlockSpec((rows, LANES), lambda b, k: (k, b)),
                  blk((1, 1, 2 * s)), blk((1, LANES, 2 * s)), blk((1, 2 * s, LANES)),
                  blk((1, 1, LANES)), blk((1, LANES, 2 * LANES)), blk((1, 1, 2 * LANES))],
        out_specs=[pl.BlockSpec((rows, LANES), lambda b, k: (k, b)),
                   pl.BlockSpec((SUBLANES, 2 * s), lambda b, k: (k, b))],
        out_shape=[jax.ShapeDtypeStruct((rows_total, w), F32),
                   jax.ShapeDtypeStruct((nk * SUBLANES, nb * 2 * s), F32)],
        scratch_shapes=[pltpu.VMEM((rows, 2 * s), F32), pltpu.VMEM((SUBLANES, 2 * s), F32)],
        semantics=("parallel", "arbitrary"), args=(u8, lam, b_bd, c_bd, dsk, w_bd, bias), comm=comm)
    return (y8, hb) if comm is None else (y8, hb, comm_outs)


def _ssm_bwd(u8, dy8, hb, ops, nseq, name, comm=None):
    lam, b_bd, c_bd, dsk, w_bd, bias = ops
    rows_total, w = u8.shape
    seq = rows_total // nseq
    nb = w // LANES
    s = STATES_PER_BLOCK
    tc = _scan_chunk_rows(seq, nseq)
    nk = seq // tc
    rows = tc * nseq
    nblk = rows // SUBLANES
    stages = SUBLANES // nseq
    tn_dims = (((0,), (0,)), ((), ()))
    nt_dims = (((1,), (1,)), ((), ()))

    def body(u_ref, dy_ref, hb_ref, lam_ref, b_ref, c_ref, d_ref, w_ref, bias_ref,
             du_ref, dlam_ref, db_ref, dct_ref, dd_ref, dw_ref, dbias_ref, hbuf, gbuf, gst, lacc):
        k = pl.program_id(1)

        @pl.when(k == 0)
        def _():
            gst[...] = jnp.zeros_like(gst)
            lacc[...] = jnp.zeros_like(lacc)
            db_ref[...] = jnp.zeros_like(db_ref)
            dct_ref[...] = jnp.zeros_like(dct_ref)
            dd_ref[...] = jnp.zeros_like(dd_ref)
            dw_ref[...] = jnp.zeros_like(dw_ref)
            dbias_ref[...] = jnp.zeros_like(dbias_ref)

        u = u_ref[...]
        ub = u.astype(BF16)
        lr = jnp.broadcast_to(lam_ref[0, :, 0:s], (SUBLANES, s))
        li = jnp.broadcast_to(lam_ref[0, :, s:2 * s], (SUBLANES, s))
        row = lax.broadcasted_iota(jnp.int32, (SUBLANES, s), 0)
        hbuf[...] = jnp.dot(ub, b_ref[0], preferred_element_type=F32)

        def fstep(i, carry):
            pr, pi = carry
            r0 = pl.multiple_of(i * SUBLANES, SUBLANES)
            br = hbuf[pl.ds(r0, SUBLANES), 0:s]
            bi = hbuf[pl.ds(r0, SUBLANES), s:2 * s]
            outr = outi = None
            for j in range(stages):
                rr = _roll_rows(pr, nseq)
                ri = _roll_rows(pi, nseq)
                pr = lr * rr - li * ri + br
                pi = lr * ri + li * rr + bi
                outr = pr if j == 0 else jnp.where(row >= j * nseq, pr, outr)
                outi = pi if j == 0 else jnp.where(row >= j * nseq, pi, outi)
            hbuf[pl.ds(r0, SUBLANES), 0:s] = outr
            hbuf[pl.ds(r0, SUBLANES), s:2 * s] = outi
            return outr, outi

        lax.fori_loop(0, nblk, fstep, (hb_ref[:, 0:s], hb_ref[:, s:2 * s]), unroll=2)
        hb16 = hbuf[...].astype(BF16)
        y = jnp.dot(hb16, c_ref[0], preferred_element_type=F32) + d_ref[0] * u
        yg, dyg_dy = _gelu_and_grad(y)
        yg16 = yg.astype(BF16)
        z = jnp.dot(yg16, w_ref[0], preferred_element_type=F32) + bias_ref[0]
        z1 = z[:, 0:LANES]
        sg = _sigmoid(z[:, LANES:2 * LANES])
        dout = dy_ref[...]
        dz = jnp.concatenate([dout * sg, dout * z1 * sg * (1.0 - sg)], axis=-1)
        dz16 = dz.astype(BF16)
        dw_ref[0] += lax.dot_general(yg16, dz16, tn_dims, preferred_element_type=F32)
        dbias_ref[0] += jnp.sum(dz, axis=0, keepdims=True)
        dy = lax.dot_general(dz16, w_ref[0], nt_dims, preferred_element_type=F32) * dyg_dy
        dy16 = dy.astype(BF16)
        dd_ref[0] += jnp.sum(dy * u, axis=0, keepdims=True)
        dct_ref[0] += lax.dot_general(dy16, hb16, tn_dims, preferred_element_type=F32)
        gbuf[...] = lax.dot_general(dy16, c_ref[0], nt_dims, preferred_element_type=F32)

        def bstep(i, carry):
            pr, pi, ar, ai = carry
            blk = nblk - 1 - i
            r0 = pl.multiple_of(blk * SUBLANES, SUBLANES)
            dr = gbuf[pl.ds(r0, SUBLANES), 0:s]
            di = gbuf[pl.ds(r0, SUBLANES), s:2 * s]
            outr = outi = None
            for j in reversed(range(stages)):
                rr = _roll_rows(pr, SUBLANES - nseq)
                ri = _roll_rows(pi, SUBLANES - nseq)
                pr = dr + lr * rr + li * ri
                pi = di - li * rr + lr * ri
                outr = pr if j == stages - 1 else jnp.where(row < (j + 1) * nseq, pr, outr)
                outi = pi if j == stages - 1 else jnp.where(row < (j + 1) * nseq, pi, outi)
            gbuf[pl.ds(r0, SUBLANES), 0:s] = outr
            gbuf[pl.ds(r0, SUBLANES), s:2 * s] = outi
            p0 = pl.multiple_of(jnp.maximum(blk - 1, 0) * SUBLANES, SUBLANES)
            first = blk == 0
            before_r = jnp.where(first, hb_ref[:, 0:s], hbuf[pl.ds(p0, SUBLANES), 0:s])
            before_i = jnp.where(first, hb_ref[:, s:2 * s], hbuf[pl.ds(p0, SUBLANES), s:2 * s])
            if stages > 1:
                last_rows = row >= SUBLANES - nseq
                before_r = _roll_rows(jnp.where(last_rows, before_r, hbuf[pl.ds(r0, SUBLANES), 0:s]), nseq)
                before_i = _roll_rows(jnp.where(last_rows, before_i, hbuf[pl.ds(r0, SUBLANES), s:2 * s]), nseq)
            return (outr, outi, ar + outr * before_r + outi * before_i, ai - outr * before_i + outi * before_r)

        gr, gi, ar, ai = lax.fori_loop(
            0, nblk, bstep, (gst[:, 0:s], gst[:, s:2 * s], lacc[:, 0:s], lacc[:, s:2 * s]))
        gst[:, 0:s] = gr
        gst[:, s:2 * s] = gi
        lacc[:, 0:s] = ar
        lacc[:, s:2 * s] = ai
        g16 = gbuf[...].astype(BF16)
        du_ref[...] = dy * d_ref[0] + lax.dot_general(g16, b_ref[0], nt_dims, preferred_element_type=F32)
        db_ref[0] += lax.dot_general(ub, g16, tn_dims, preferred_element_type=F32)

        @pl.when(k == nk - 1)
        def _():
            dlam_ref[0] = jnp.sum(lacc[...], axis=0, keepdims=True)

    blk = lambda shape: pl.BlockSpec(shape, lambda b, k: (b, 0, 0))
    rev = lambda b, k: (nk - 1 - k, b)
    outs, comm_outs = _call(
        body, name=name, grid=(nb, nk),
        in_specs=[pl.BlockSpec((rows, LANES), rev), pl.BlockSpec((rows, LANES), rev),
                  pl.BlockSpec((SUBLANES, 2 * s), rev),
                  blk((1, 1, 2 * s)), blk((1, LANES, 2 * s)), blk((1, 2 * s, LANES)),
                  blk((1, 1, LANES)), blk((1, LANES, 2 * LANES)), blk((1, 1, 2 * LANES))],
        out_specs=[pl.BlockSpec((rows, LANES), rev),
                   blk((1, 1, 2 * s)), blk((1, LANES, 2 * s)), blk((1, LANES, 2 * s)),
                   blk((1, 1, LANES)), blk((1, LANES, 2 * LANES)), blk((1, 1, 2 * LANES))],
        out_shape=[jax.ShapeDtypeStruct((rows_total, w), F32),
                   jax.ShapeDtypeStruct((nb, 1, 2 * s), F32), jax.ShapeDtypeStruct((nb, LANES, 2 * s), F32),
                   jax.ShapeDtypeStruct((nb, LANES, 2 * s), F32), jax.ShapeDtypeStruct((nb, 1, LANES), F32),
                   jax.ShapeDtypeStruct((nb, LANES, 2 * LANES), F32), jax.ShapeDtypeStruct((nb, 1, 2 * LANES), F32)],
        scratch_shapes=[pltpu.VMEM((rows, 2 * s), F32), pltpu.VMEM((rows, 2 * s), F32),
                        pltpu.VMEM((SUBLANES, 2 * s), F32), pltpu.VMEM((SUBLANES, 2 * s), F32)],
        semantics=("parallel", "arbitrary"), args=(u8, dy8, hb, lam, b_bd, c_bd, dsk, w_bd, bias), comm=comm)
    return outs if comm is None else (outs, comm_outs)


def _to_scan_rows(a, nseq, seq):
    w = a.shape[-1]
    return jnp.swapaxes(a.reshape(nseq, seq, w), 0, 1).reshape(seq * nseq, w)


def _from_scan_rows(a8, nseq, seq):
    w = a8.shape[-1]
    return jnp.swapaxes(a8.reshape(seq, nseq, w), 0, 1).reshape(nseq * seq, w)


ANY = pl.BlockSpec(memory_space=pl.ANY)

BIG = (("ffn1_w_in", True), ("ffn1_w_out", False), ("mix_w_in", True), ("mix_w_out", False),
       ("ffn2_w_in", True), ("ffn2_w_out", False))


def _my_place():
    return lax.axis_index("x"), lax.axis_index("y"), lax.axis_index("c")


def _other_chips(x, y):
    return [(1 - x, y), (x, 1 - y), (1 - x, 1 - y)]


def _half_of_shard(ref, col_sharded, chip, core):
    full_rows, full_cols = ref.shape
    if col_sharded:
        hr, cs = full_rows // 2, full_cols // N_CHIPS
        return ref.at[pl.ds(pl.multiple_of(core * hr, 8), hr), pl.ds(chip * cs, cs)]
    rs = full_rows // N_CHIPS
    return ref.at[pl.ds(pl.multiple_of(chip * rs + core * (rs // 2), 8), rs // 2), :]


def _gather_comm(shards, cols):
    full_shapes = [(sh.shape[0], sh.shape[1] * N_CHIPS) if col else (sh.shape[0] * N_CHIPS, sh.shape[1])
                   for sh, col in zip(shards, cols)]
    nw = len(shards)

    def first_copies(ins, outs, sems):
        send_sems, recv_sems, local_sems = sems
        x, y, c = _my_place()
        me = 2 * x + y
        locals_, sends = [], []
        for wi in range(nw):
            src, dst = ins[wi], outs[wi]
            rs, cs = src.shape
            hs = rs // 2
            if cols[wi]:
                place = dst.at[:, pl.ds(me * cs, cs)]
            else:
                place = dst.at[pl.ds(pl.multiple_of(me * rs, 8), rs), :]
            locals_.append(pltpu.make_async_copy(src, place, local_sems.at[wi]))
            my_half = src.at[pl.ds(pl.multiple_of(c * hs, 8), hs), :]
            for j, (px, py) in enumerate(_other_chips(x, y)):
                sends.append(pltpu.make_async_remote_copy(
                    src_ref=my_half, dst_ref=_half_of_shard(dst, cols[wi], me, c),
                    send_sem=send_sems.at[wi * 6 + j], recv_sem=recv_sems.at[wi * 6 + j],
                    device_id=(px, py, c), device_id_type=MESH))
        return locals_, sends

    def start(ins, outs, sems):
        locals_, sends = first_copies(ins, outs, sems)
        for cp in locals_ + sends:
            cp.start()

    def forwards(outs, sems, wait_landed):
        send_sems, recv_sems, _ = sems
        x, y, c = _my_place()
        out = []
        for wi in range(nw):
            dst = outs[wi]
            for j, (px, py) in enumerate(_other_chips(x, y)):
                got = _half_of_shard(dst, cols[wi], 2 * px + py, c)
                if wait_landed:
                    pltpu.make_async_remote_copy(
                        src_ref=got, dst_ref=got, send_sem=send_sems.at[wi * 6 + j], recv_sem=recv_sems.at[wi * 6 + j],
                        device_id=(px, py, c), device_id_type=MESH).wait_recv()
                out.append(pltpu.make_async_remote_copy(
                    src_ref=got, dst_ref=got, send_sem=send_sems.at[wi * 6 + 3 + j], recv_sem=recv_sems.at[wi * 6 + 3 + j],
                    device_id=(x, y, 1 - c), device_id_type=MESH))
                if wait_landed:
                    out[-1].start()
        return out

    def middle(ins, outs, sems):
        forwards(outs, sems, True)

    def finish(ins, outs, sems):
        send_sems, recv_sems, _ = sems
        x, y, c = _my_place()
        locals_, sends = first_copies(ins, outs, sems)
        for wi in range(nw):
            dst = outs[wi]
            for j, (px, py) in enumerate(_other_chips(x, y)):
                theirs = _half_of_shard(dst, cols[wi], 2 * px + py, 1 - c)
                pltpu.make_async_remote_copy(
                    src_ref=theirs, dst_ref=theirs, send_sem=send_sems.at[wi * 6 + 3 + j],
                    recv_sem=recv_sems.at[wi * 6 + 3 + j], device_id=(x, y, 1 - c), device_id_type=MESH).wait_recv()
        for cp in sends + forwards(outs, sems, False):
            cp.wait_send()
        for cp in locals_:
            cp.wait()

    return _Comm(shards, [jax.ShapeDtypeStruct(s, BF16) for s in full_shapes],
                 [pltpu.SemaphoreType.DMA((6 * nw,)), pltpu.SemaphoreType.DMA((6 * nw,)),
                  pltpu.SemaphoreType.DMA((nw,))], start, finish, middle=middle)


def _pair_exchange_comm(grads, cols):
    nw = len(grads)
    n_copies = sum(1 if col else N_CHIPS for col in cols)

    def copies(ins, outs, sems):
        send_sems, recv_sems = sems
        x, y, c = _my_place()
        out = []
        for wi in range(nw):
            src, dst = ins[wi], outs[wi]
            fr = src.shape[0]
            if cols[wi]:
                hr = fr // 2
                pieces = [(src.at[pl.ds(pl.multiple_of((1 - c) * hr, 8), hr), :], dst)]
            else:
                rs = fr // N_CHIPS
                hs = rs // 2
                pieces = [(src.at[pl.ds(pl.multiple_of(k * rs + (1 - c) * hs, 8), hs), :],
                           dst.at[pl.ds(k * hs, hs), :]) for k in range(N_CHIPS)]
            for s_ref, d_ref in pieces:
                out.append(pltpu.make_async_remote_copy(
                    src_ref=s_ref, dst_ref=d_ref, send_sem=send_sems.at[len(out)], recv_sem=recv_sems.at[len(out)],
                    device_id=(x, y, 1 - c), device_id_type=MESH))
        return out

    def start(ins, outs, sems):
        for cp in copies(ins, outs, sems):
            cp.start()

    def finish(ins, outs, sems):
        for cp in copies(ins, outs, sems):
            cp.wait()

    return _Comm(grads, [jax.ShapeDtypeStruct((g.shape[0] // 2, g.shape[1]), F32) for g in grads],
                 [pltpu.SemaphoreType.DMA((n_copies,)), pltpu.SemaphoreType.DMA((n_copies,))], start, finish)


def _pair_sum(grad, other, col, core, name):
    fr, fc = grad.shape
    pieces = 1 if col else N_CHIPS
    pr = fr // 2 // pieces
    gview = grad.reshape(pieces * 2, pr, fc)
    oview = other.reshape(pieces, pr, fc)
    tr = _tile(pr, 256, 16)

    def body(c_ref, g_ref, o_ref, out_ref):
        out_ref[...] = (g_ref[...] + o_ref[...]).astype(out_ref.dtype)

    out = pl.pallas_call(
        body, name=name,
        grid_spec=pltpu.PrefetchScalarGridSpec(
            num_scalar_prefetch=1, grid=(pieces, pr // tr),
            in_specs=[pl.BlockSpec((1, tr, fc), lambda p, i, cref: (p * 2 + cref[0], i, 0)),
                      pl.BlockSpec((1, tr, fc), lambda p, i, cref: (p, i, 0))],
            out_specs=pl.BlockSpec((1, tr, fc), lambda p, i, cref: (p, i, 0))),
        out_shape=jax.ShapeDtypeStruct((pieces, pr, fc), BF16),
        compiler_params=_params("parallel", "parallel"),
    )(core, gview, oview)
    return out.reshape(fr // 2, fc)


def _chip_exchange_comm(psums, cols):
    nw = len(psums)
    out_shapes = [(N_CHIPS, p.shape[0], p.shape[1] // N_CHIPS) if col else (N_CHIPS, p.shape[0] // N_CHIPS, p.shape[1])
                  for p, col in zip(psums, cols)]

    def copies(ins, outs, sems):
        send_sems, recv_sems, local_sems = sems
        x, y, c = _my_place()
        me = 2 * x + y
        out = []
        for wi in range(nw):
            src = ins[wi]
            mine = outs[wi].at[me]

            def piece(chip, src=src, col=cols[wi]):
                if col:
                    cs = src.shape[1] // N_CHIPS
                    return src.at[:, pl.ds(chip * cs, cs)]
                ps = src.shape[0] // N_CHIPS
                return src.at[pl.ds(pl.multiple_of(chip * ps, 8), ps), :]

            out.append(pltpu.make_async_copy(piece(me), mine, local_sems.at[wi]))
            for j, (px, py) in enumerate(_other_chips(x, y)):
                out.append(pltpu.make_async_remote_copy(
                    src_ref=piece(2 * px + py), dst_ref=mine,
                    send_sem=send_sems.at[wi * 3 + j], recv_sem=recv_sems.at[wi * 3 + j],
                    device_id=(px, py, c), device_id_type=MESH))
        return out

    def start(ins, outs, sems):
        for cp in copies(ins, outs, sems):
            cp.start()

    def finish(ins, outs, sems):
        for cp in copies(ins, outs, sems):
            cp.wait()

    return _Comm(psums, [jax.ShapeDtypeStruct(s, BF16) for s in out_shapes],
                 [pltpu.SemaphoreType.DMA((3 * nw,)), pltpu.SemaphoreType.DMA((3 * nw,)),
                  pltpu.SemaphoreType.DMA((nw,))], start, finish)


def _chip_sum(slots, core, layer, layers, into, name):
    _, hr, cs = slots.shape
    tr = _tile(hr, 256, 16)

    def body(c_ref, s_ref, *rest):
        out_ref = rest[-1]
        acc = s_ref[0].astype(F32)
        for i in range(1, N_CHIPS):
            acc = acc + s_ref[i].astype(F32)
        out_ref[0] = acc

    in_specs = [pl.BlockSpec((N_CHIPS, tr, cs), lambda i, cref: (0, i, 0))]
    args = [core, slots]
    aliases = {}
    if into is not None:
        in_specs.append(pl.BlockSpec(memory_space=pl.ANY))
        args.append(into.reshape(layers * 2, hr, cs))
        aliases = {2: 0}
    out = pl.pallas_call(
        body, name=name,
        grid_spec=pltpu.PrefetchScalarGridSpec(
            num_scalar_prefetch=1, grid=(hr // tr,), in_specs=in_specs,
            out_specs=pl.BlockSpec((1, tr, cs), lambda i, cref: (layer * 2 + cref[0], i, 0))),
        out_shape=jax.ShapeDtypeStruct((layers * 2, hr, cs), F32),
        input_output_aliases=aliases,
        compiler_params=_params("parallel"),
    )(*args)
    return out.reshape(layers, 2 * hr, cs)


def _pair_share_comm(reduced):
    nw = len(reduced)

    def copies(ins, outs, sems):
        send_sems, recv_sems = sems
        x, y, c = _my_place()
        out = []
        for wi in range(nw):
            hs = outs[wi].shape[1] // 2
            mine = outs[wi].at[:, pl.ds(pl.multiple_of(c * hs, 8), hs), :]
            out.append(pltpu.make_async_remote_copy(
                src_ref=mine, dst_ref=mine, send_sem=send_sems.at[wi], recv_sem=recv_sems.at[wi],
                device_id=(x, y, 1 - c), device_id_type=MESH))
        return out

    def start(ins, outs, sems):
        for cp in copies(ins, outs, sems):
            cp.start()

    def finish(ins, outs, sems):
        for cp in copies(ins, outs, sems):
            cp.wait()

    return _Comm(reduced, [jax.ShapeDtypeStruct(r.shape, F32) for r in reduced],
                 [pltpu.SemaphoreType.DMA((nw,)), pltpu.SemaphoreType.DMA((nw,))], start, finish,
                 alias={i: i for i in range(nw)})


def _all_reduce_small(flat):
    rows, lanes = flat.shape
    seg = rows // N_DEV

    def body(in_ref, out_ref, recv_ref, send_sems, recv_sems):
        x, y, c = _my_place()
        me = 4 * x + 2 * y + c

        def peer(r):
            fx, fy, fc = (r >> 2) & 1, (r >> 1) & 1, r & 1
            px = jnp.where(fx == 1, 1 - x, x)
            py = jnp.where(fy == 1, 1 - y, y)
            pc = jnp.where(fc == 1, 1 - c, c)
            return px, py, pc

        first = []
        for r in range(1, N_DEV):
            px, py, pc = peer(r)
            theirs = in_ref.at[pl.ds(pl.multiple_of((4 * px + 2 * py + pc) * seg, 8), seg), :]
            cp = pltpu.make_async_remote_copy(
                src_ref=theirs, dst_ref=recv_ref.at[r], send_sem=send_sems.at[r - 1], recv_sem=recv_sems.at[r - 1],
                device_id=(px, py, pc), device_id_type=MESH)
            cp.start()
            first.append(cp)
        for cp in first:
            cp.wait()
        my_rows = pl.ds(pl.multiple_of(me * seg, 8), seg)
        acc = in_ref[my_rows, :]
        for r in range(1, N_DEV):
            acc = acc + recv_ref[r]
        out_ref[my_rows, :] = acc
        second = []
        for r in range(1, N_DEV):
            px, py, pc = peer(r)
            cp = pltpu.make_async_remote_copy(
                src_ref=out_ref.at[my_rows, :], dst_ref=out_ref.at[my_rows, :],
                send_sem=send_sems.at[6 + r], recv_sem=recv_sems.at[6 + r],
                device_id=(px, py, pc), device_id_type=MESH)
            cp.start()
            second.append(cp)
        for r in range(1, N_DEV):
            px, py, pc = peer(r)
            theirs = out_ref.at[pl.ds(pl.multiple_of((4 * px + 2 * py + pc) * seg, 8), seg), :]
            pltpu.make_async_remote_copy(
                src_ref=theirs, dst_ref=theirs, send_sem=send_sems.at[6 + r], recv_sem=recv_sems.at[6 + r],
                device_id=(px, py, pc), device_id_type=MESH).wait_recv()
        for cp in second:
            cp.wait_send()

    vm = pl.BlockSpec(memory_space=pltpu.VMEM)
    return pl.pallas_call(
        body, name="all_reduce_small",
        in_specs=[vm], out_specs=vm,
        out_shape=jax.ShapeDtypeStruct((rows, lanes), F32),
        scratch_shapes=[pltpu.VMEM((N_DEV, seg, lanes), F32),
                        pltpu.SemaphoreType.DMA((2 * (N_DEV - 1),)), pltpu.SemaphoreType.DMA((2 * (N_DEV - 1),))],
        compiler_params=pltpu.CompilerParams(vmem_limit_bytes=VMEM_LIMIT),
    )(flat)


def _adamw_update(w_ref, g_ref, m_ref, v_ref, d_ref, nm_ref, nv_ref):
    c1 = 1.0 - ADAM_B1 ** ADAM_STEP
    c2 = 1.0 - ADAM_B2 ** ADAM_STEP
    gv = g_ref[...]
    nm = ADAM_B1 * m_ref[...] + (1.0 - ADAM_B1) * gv
    nv = ADAM_B2 * v_ref[...] + (1.0 - ADAM_B2) * (gv * gv)
    d_ref[...] = -ADAM_LR * ((nm / c1) / (jnp.sqrt(nv / c2) + ADAM_EPS) + ADAM_WD * w_ref[...])
    nm_ref[...] = nm
    nv_ref[...] = nv


def _adamw_many(ws, gs, ms, vs, name):
    n = len(ws)

    def body(*refs):
        for i in range(n):
            _adamw_update(*[refs[k * n + i] for k in range(7)])

    shapes = [jax.ShapeDtypeStruct(w.shape, F32) for w in ws]
    outs = pl.pallas_call(
        body, name=name, out_shape=shapes * 3,
        compiler_params=pltpu.CompilerParams(vmem_limit_bytes=VMEM_LIMIT),
    )(*ws, *gs, *ms, *vs)
    return outs[:n], outs[n:2 * n], outs[2 * n:]


def _adamw(w, g, m, v, name):
    rows, cols = w.shape
    tr = _tile(rows, 256, 8)
    body = functools.partial(_adamw_update)

    blk = pl.BlockSpec((tr, cols), lambda i: (i, 0))
    sds = jax.ShapeDtypeStruct((rows, cols), F32)
    return pl.pallas_call(
        body, name=name, grid=(rows // tr,),
        in_specs=[blk] * 4, out_specs=[blk] * 3, out_shape=[sds] * 3,
        compiler_params=_params("parallel"),
    )(w, g, m, v)


SMALL = ("norm_ffn1", "norm_mix", "ssm_a_re", "ssm_a_im", "ssm_log_dt", "ssm_b_re", "ssm_b_im", "ssm_c_re",
         "ssm_c_im", "ssm_d", "ssm_glu_w", "ssm_glu_b", "gm_v_gain", "gm_w_s", "gm_b_s", "gain_ssm_out",
         "gain_gm_out", "norm_ffn2", "norm_final")
WEIGHTS = ("norm_ffn1", "ffn1_w_in", "ffn1_w_out", "norm_mix", "mix_w_in", "ssm_a_re", "ssm_a_im", "ssm_log_dt",
           "ssm_b_re", "ssm_b_im", "ssm_c_re", "ssm_c_im", "ssm_d", "ssm_glu_w", "ssm_glu_b", "gm_v_gain", "gm_w_s",
           "gm_b_s", "gain_ssm_out", "gain_gm_out", "mix_w_out", "norm_ffn2", "ffn2_w_in", "ffn2_w_out", "norm_final")


def _ffn_fwd(x, gain, w_in, w_out, tag, hosted=None):
    if hosted is None:
        h, g, u, a = _ffn_in_fwd(x, gain, w_in, f"{tag}_in")
    else:
        (h, g, u, a), got = _ffn_in_fwd(x, gain, w_in, f"{tag}_in_hosting", comm=hosted[0]())
        hosted[1](got)
    if callable(w_out):
        w_out = w_out()
    out = _matmul(a, w_out, "nn", scale=0.5, res=x, tm=512, tn=1024, tk=4096, name=f"{tag}_out")
    return out, (x, h, g, u, a)


def _ffn_bwd(dout, saved, gain, w_in, w_out, tag, hooks=None, publish=None):
    x, h, g, u, a = saved
    f = g.shape[1]
    hooks = hooks or {}

    def hosted(key, fn, *args, name, **kw):
        if key not in hooks:
            return fn(*args, name=name, **kw)
        make, take = hooks[key]
        *res, got = fn(*args, name=f"{name}_hosting", comm=make(), **kw)
        take(got)
        return res[0] if len(res) == 1 else tuple(res)

    dg, du = hosted("out_dx", _ffn_out_bwd, dout, w_out, g, u, name=f"{tag}_out_dx")
    dw_out = hosted("out_dw", _matmul, a, dout, "tn", scale=0.5, tm=1536, tn=1024, tk=2048, name=f"{tag}_out_dw")
    if publish is not None:
        publish("out", dw_out)
    dw_in = hosted("in_dw_g", _matmul, h, dg, "tn", tm=1024, tn=1536, tk=2048, name=f"{tag}_in_dw_g",
                   out_cols=2 * f)
    dw_in = hosted("in_dw_u", _matmul, h, du, "tn", tm=1024, tn=1536, tk=2048, name=f"{tag}_in_dw_u",
                   out_cols=2 * f, col_off=f, into=dw_in)
    if publish is not None:
        publish("in", dw_in)
    dx, dgain = hosted("in_dx", _proj_in_bwd, [(dg, 0), (du, f)], w_in, x, gain, dout, name=f"{tag}_in_dx")
    return dx, dgain, dw_in, dw_out


def kernel(x, norm_ffn1, ffn1_w_in, ffn1_w_out, norm_mix, mix_w_in, ssm_a_re, ssm_a_im, ssm_log_dt, ssm_b_re, ssm_b_im, ssm_c_re, ssm_c_im, ssm_d, ssm_glu_w, ssm_glu_b, gm_v_gain, gm_w_s, gm_b_s, gain_ssm_out, gain_gm_out, mix_w_out, norm_ffn2, ffn2_w_in, ffn2_w_out, norm_final, loss_target, m_norm_ffn1, m_ffn1_w_in, m_ffn1_w_out, m_norm_mix, m_mix_w_in, m_ssm_a_re, m_ssm_a_im, m_ssm_log_dt, m_ssm_b_re, m_ssm_b_im, m_ssm_c_re, m_ssm_c_im, m_ssm_d, m_ssm_glu_w, m_ssm_glu_b, m_gm_v_gain, m_gm_w_s, m_gm_b_s, m_gain_ssm_out, m_gain_gm_out, m_mix_w_out, m_norm_ffn2, m_ffn2_w_in, m_ffn2_w_out, m_norm_final, v_norm_ffn1, v_ffn1_w_in, v_ffn1_w_out, v_norm_mix, v_mix_w_in, v_ssm_a_re, v_ssm_a_im, v_ssm_log_dt, v_ssm_b_re, v_ssm_b_im, v_ssm_c_re, v_ssm_c_im, v_ssm_d, v_ssm_glu_w, v_ssm_glu_b, v_gm_v_gain, v_gm_w_s, v_gm_b_s, v_gain_ssm_out, v_gain_gm_out, v_mix_w_out, v_norm_ffn2, v_ffn2_w_in, v_ffn2_w_out, v_norm_final):
    wts = dict(norm_ffn1=norm_ffn1, ffn1_w_in=ffn1_w_in, ffn1_w_out=ffn1_w_out, norm_mix=norm_mix, mix_w_in=mix_w_in,
               ssm_a_re=ssm_a_re, ssm_a_im=ssm_a_im, ssm_log_dt=ssm_log_dt, ssm_b_re=ssm_b_re, ssm_b_im=ssm_b_im,
               ssm_c_re=ssm_c_re, ssm_c_im=ssm_c_im, ssm_d=ssm_d, ssm_glu_w=ssm_glu_w, ssm_glu_b=ssm_glu_b,
               gm_v_gain=gm_v_gain, gm_w_s=gm_w_s, gm_b_s=gm_b_s, gain_ssm_out=gain_ssm_out, gain_gm_out=gain_gm_out,
               mix_w_out=mix_w_out, norm_ffn2=norm_ffn2, ffn2_w_in=ffn2_w_in, ffn2_w_out=ffn2_w_out,
               norm_final=norm_final)
    mom = dict(norm_ffn1=m_norm_ffn1, ffn1_w_in=m_ffn1_w_in, ffn1_w_out=m_ffn1_w_out, norm_mix=m_norm_mix,
               mix_w_in=m_mix_w_in, ssm_a_re=m_ssm_a_re, ssm_a_im=m_ssm_a_im, ssm_log_dt=m_ssm_log_dt,
               ssm_b_re=m_ssm_b_re, ssm_b_im=m_ssm_b_im, ssm_c_re=m_ssm_c_re, ssm_c_im=m_ssm_c_im, ssm_d=m_ssm_d,
               ssm_glu_w=m_ssm_glu_w, ssm_glu_b=m_ssm_glu_b, gm_v_gain=m_gm_v_gain, gm_w_s=m_gm_w_s, gm_b_s=m_gm_b_s,
               gain_ssm_out=m_gain_ssm_out, gain_gm_out=m_gain_gm_out, mix_w_out=m_mix_w_out, norm_ffn2=m_norm_ffn2,
               ffn2_w_in=m_ffn2_w_in, ffn2_w_out=m_ffn2_w_out, norm_final=m_norm_final)
    var = dict(norm_ffn1=v_norm_ffn1, ffn1_w_in=v_ffn1_w_in, ffn1_w_out=v_ffn1_w_out, norm_mix=v_norm_mix,
               mix_w_in=v_mix_w_in, ssm_a_re=v_ssm_a_re, ssm_a_im=v_ssm_a_im, ssm_log_dt=v_ssm_log_dt,
               ssm_b_re=v_ssm_b_re, ssm_b_im=v_ssm_b_im, ssm_c_re=v_ssm_c_re, ssm_c_im=v_ssm_c_im, ssm_d=v_ssm_d,
               ssm_glu_w=v_ssm_glu_w, ssm_glu_b=v_ssm_glu_b, gm_v_gain=v_gm_v_gain, gm_w_s=v_gm_w_s, gm_b_s=v_gm_b_s,
               gain_ssm_out=v_gain_ssm_out, gain_gm_out=v_gain_gm_out, mix_w_out=v_mix_w_out, norm_ffn2=v_norm_ffn2,
               ffn2_w_in=v_ffn2_w_in, ffn2_w_out=v_ffn2_w_out, norm_final=v_norm_final)

    nseq, seq, d = x.shape
    n = nseq * seq
    depth = norm_ffn1.shape[0]
    width = gain_ssm_out.shape[1]
    groups = ssm_a_re.shape[1]
    heads = gm_w_s.shape[1]
    core = lax.axis_index("c").astype(jnp.int32).reshape(1)

    is_col = dict(BIG)
    full = {name: [None] * depth for name, _ in BIG}

    def gather_comm(pairs):
        return _gather_comm([wts[nm][l].astype(BF16) for nm, l in pairs], [is_col[nm] for nm, _ in pairs])

    def store(pairs, arrays):
        for (nm, l), w in zip(pairs, arrays):
            full[nm][l] = w

    pairs = [("ffn1_w_in", 0)]
    store(pairs, _run_comm(gather_comm(pairs), "all_gather_first"))

    xs = x.reshape(n, d)
    saved = []
    for l in range(depth):
        pairs = [("ffn1_w_out", l)] + ([("mix_w_in", l), ("mix_w_out", l)] if l == 0 else [])
        x1, s_ffn1 = _ffn_fwd(xs, norm_ffn1[l], full["ffn1_w_in"][l], lambda l=l: full["ffn1_w_out"][l], "ffn1",
                              hosted=(functools.partial(gather_comm, pairs), functools.partial(store, pairs)))
        hm, u_ssm, zgm = _mix_in_fwd(x1, norm_mix[l], full["mix_w_in"][l], width, "mix_in")
        bt_re = jnp.swapaxes(ssm_b_re[l], 1, 2)
        bt_im = jnp.swapaxes(ssm_b_im[l], 1, 2)
        disc_in = (ssm_a_re[l], ssm_a_im[l], ssm_log_dt[l].reshape(groups, 1), bt_re, bt_im)
        lr, li, bbr, bbi = _disc_fwd(*disc_in)
        ops = _ssm_operands(lr, li, bbr, bbi, ssm_c_re[l], ssm_c_im[l], ssm_d[l], ssm_glu_w[l], ssm_glu_b[l])
        u8 = _to_scan_rows(u_ssm, nseq, seq)
        pairs = [("ffn2_w_in", l), ("ffn2_w_out", l)]
        y8, hb, got = _ssm_fwd(u8, ops, nseq, "s5_fwd", comm=gather_comm(pairs))
        store(pairs, got)
        y_ssm = _from_scan_rows(y8, nseq, seq)
        bias_tile = jnp.broadcast_to(gm_b_s[l].T[:, :, None], (GM_CHUNK, heads, GM_HEAD_DIM)).reshape(GM_CHUNK, width)
        y_gm = _gmlp_fwd(zgm, gm_v_gain[l], gm_w_s[l], bias_tile, "gmlp_fwd")
        ycat = _mixnorm_fwd(y_ssm, y_gm, gain_ssm_out[l], gain_gm_out[l], "mix_out_norm")
        x2 = _matmul(ycat, full["mix_w_out"][l], "nn", res=x1, tm=512, tn=1024, tk=1024, name="mix_out")
        hosted = None
        if l + 1 < depth:
            pairs = [("ffn1_w_in", l + 1), ("mix_w_in", l + 1), ("mix_w_out", l + 1)]
            hosted = (functools.partial(gather_comm, pairs), functools.partial(store, pairs))
        x3, s_ffn2 = _ffn_fwd(x2, norm_ffn2[l], full["ffn2_w_in"][l], full["ffn2_w_out"][l], "ffn2", hosted=hosted)
        saved.append(dict(ffn1=s_ffn1, x1=x1, hm=hm, zgm=zgm, disc_in=disc_in, ops=ops, u8=u8, hb=hb, y_ssm=y_ssm,
                          bias_tile=bias_tile, y_gm=y_gm, ycat=ycat, ffn2=s_ffn2))
        xs = x3

    dx, g_norm_final, loss_part = _loss_head(xs, norm_final, loss_target.reshape(n, d))
    big = {name: [None] * depth for name, _ in BIG}
    small = {name: [None] * depth for name in SMALL if name != "norm_final"}
    gpb = GROUPS_PER_BLOCK
    s_blk = STATES_PER_BLOCK
    psum_of, reduced = {}, {}

    def swap_comm(pairs):
        return _pair_exchange_comm([big[nm][l] for nm, l in pairs], [is_col[nm] for nm, _ in pairs])

    def take_swapped(pairs, others):
        for (nm, l), other in zip(pairs, others):
            psum_of[nm, l] = _pair_sum(big[nm][l], other, is_col[nm], core, f"grad_pair_sum_{nm}")

    def send_comm(pairs):
        return _chip_exchange_comm([psum_of[p] for p in pairs], [is_col[nm] for nm, _ in pairs])

    def take_sent(pairs, slots):
        for (nm, l), s in zip(pairs, slots):
            reduced[nm] = _chip_sum(s, core, l, depth, reduced.get(nm), f"grad_chip_sum_{nm}")

    def hosting(make, take, pairs):
        return functools.partial(make, pairs), functools.partial(take, pairs)

    for l in reversed(range(depth)):
        sv = saved[l]
        above = [(nm, l + 1) for nm in ("mix_w_in", "mix_w_out", "ffn1_w_in", "ffn1_w_out")] if l + 1 < depth else []
        dx, small["norm_ffn2"][l], big["ffn2_w_in"][l], big["ffn2_w_out"][l] = _ffn_bwd(
            dx, sv["ffn2"], norm_ffn2[l], full["ffn2_w_in"][l], full["ffn2_w_out"][l], "ffn2",
            hooks={"out_dx": hosting(swap_comm, take_swapped, above)} if above else None)
        mine = [("ffn2_w_in", l), ("ffn2_w_out", l)]
        dycat, got = _matmul(dx, full["mix_w_out"][l], "nt", tm=512, tn=1024, tk=1024, name="mix_out_dx",
                             comm=swap_comm(mine))
        take_swapped(mine, got)
        big["mix_w_out"][l] = _matmul(sv["ycat"], dx, "tn", tm=1024, tn=1024, tk=2048, name="mix_out_dw")
        dy_ssm, dy_gm, small["gain_ssm_out"][l], small["gain_gm_out"][l] = _mixnorm_bwd(
            sv["y_ssm"], sv["y_gm"], gain_ssm_out[l], gain_gm_out[l], dycat, "mix_out_norm_bwd")
        dzgm, small["gm_w_s"][l], dbias_tile, small["gm_v_gain"][l] = _gmlp_bwd(
            sv["zgm"], dy_gm, gm_v_gain[l], gm_w_s[l], sv["bias_tile"], "gmlp_bwd")
        small["gm_b_s"][l] = dbias_tile.reshape(GM_CHUNK, heads, GM_HEAD_DIM).sum(-1).T
        dy8 = _to_scan_rows(dy_ssm, nseq, seq)
        (du8, dlam, db_bd, dct_bd, dd, dw_bd, dbias), got = _ssm_bwd(
            sv["u8"], dy8, sv["hb"], sv["ops"], nseq, "s5_bwd", comm=send_comm(mine + above))
        take_sent(mine + above, got)
        du_ssm = _from_scan_rows(du8, nseq, seq)
        dlr = dlam[:, 0, :s_blk].reshape(groups, SSM_STATE)
        dli = dlam[:, 0, s_blk:].reshape(groups, SSM_STATE)
        dbbr = _block_diag_extract(db_bd[:, :, :s_blk], gpb)
        dbbi = _block_diag_extract(db_bd[:, :, s_blk:], gpb)
        da_re, da_im, dldt, dbt_re, dbt_im = _disc_bwd(*sv["disc_in"], dlr, dli, dbbr, dbbi)
        small["ssm_a_re"][l], small["ssm_a_im"][l], small["ssm_log_dt"][l] = da_re, da_im, dldt.reshape(groups)
        small["ssm_b_re"][l] = jnp.swapaxes(dbt_re, 1, 2)
        small["ssm_b_im"][l] = jnp.swapaxes(dbt_im, 1, 2)
        small["ssm_c_re"][l] = _block_diag_extract(dct_bd[:, :, :s_blk], gpb)
        small["ssm_c_im"][l] = -_block_diag_extract(dct_bd[:, :, s_blk:], gpb)
        small["ssm_d"][l] = dd.reshape(groups, SSM_CH)
        small["ssm_glu_w"][l] = jnp.concatenate(
            [_block_diag_extract(dw_bd[:, :, :LANES], gpb), _block_diag_extract(dw_bd[:, :, LANES:], gpb)], axis=-1)
        small["ssm_glu_b"][l] = jnp.concatenate(
            [dbias[:, 0, :LANES].reshape(groups, SSM_CH), dbias[:, 0, LANES:].reshape(groups, SSM_CH)], axis=-1)
        cols_mi = 3 * width
        dw_mi = _matmul(sv["hm"], du_ssm, "tn", tm=1024, tn=width, tk=2048, name="mix_in_dw_ssm", out_cols=cols_mi)
        big["mix_w_in"][l] = _matmul(sv["hm"], dzgm, "tn", tm=1024, tn=width, tk=2048, name="mix_in_dw_gm",
                                     out_cols=cols_mi, col_off=width, into=dw_mi)
        dx, small["norm_mix"][l] = _proj_in_bwd([(du_ssm, 0), (dzgm, width)], full["mix_w_in"][l], sv["x1"],
                                                norm_mix[l], dx, "mix_in_dx")
        hooks = None
        if l == 0:
            mix, w_out_0, w_in_0 = [("mix_w_in", 0), ("mix_w_out", 0)], [("ffn1_w_out", 0)], [("ffn1_w_in", 0)]
            hooks = {"out_dx": hosting(swap_comm, take_swapped, mix), "out_dw": hosting(send_comm, take_sent, mix),
                     "in_dw_g": hosting(swap_comm, take_swapped, w_out_0),
                     "in_dw_u": hosting(send_comm, take_sent, w_out_0),
                     "in_dx": hosting(swap_comm, take_swapped, w_in_0)}

        def publish(which, dw, l=l):
            big[f"ffn1_w_{which}"][l] = dw

        dx, small["norm_ffn1"][l], big["ffn1_w_in"][l], big["ffn1_w_out"][l] = _ffn_bwd(
            dx, sv["ffn1"], norm_ffn1[l], full["ffn1_w_in"][l], full["ffn1_w_out"][l], "ffn1",
            hooks=hooks, publish=publish)
    grad_x = dx.reshape(nseq, seq, d)

    tail = [("ffn1_w_in", 0)]
    take_sent(tail, _run_comm(send_comm(tail), "grad_chip_exchange_tail"))
    names = [name for name, _ in BIG]
    grads = dict(zip(names, _run_comm(_pair_share_comm([reduced[nm] for nm in names]), "grad_pair_share")))

    pieces = [jnp.stack(small[name]).reshape(-1) for name in SMALL if name != "norm_final"]
    pieces += [g_norm_final.reshape(-1), loss_part.reshape(1)]
    sizes = [p.shape[0] for p in pieces]
    total = sum(sizes)
    rows = -(-total // (LANES * N_DEV * SUBLANES)) * N_DEV * SUBLANES
    pad = rows * LANES - total

    flat_g = _all_reduce_small(jnp.concatenate(pieces + [jnp.zeros((pad,), F32)]).reshape(rows, LANES)).reshape(-1)
    loss = flat_g[total - 1]
    offs = 0
    for name, size in zip(SMALL, sizes[:-1]):
        grads[name] = flat_g[offs:offs + size].reshape(wts[name].shape)
        offs += size

    delta, new_m, new_v = {}, {}, {}
    for name, _ in BIG:
        shape = wts[name].shape
        two_d = lambda a: a.reshape(shape[0] * shape[1], shape[2])
        dl, nm, nv = _adamw(two_d(wts[name]), two_d(grads[name]), two_d(mom[name]), two_d(var[name]), f"adamw_{name}")
        delta[name], new_m[name], new_v[name] = dl.reshape(shape), nm.reshape(shape), nv.reshape(shape)
    at_least_2d = lambda a: a.reshape(1, -1) if a.ndim == 1 else a
    dls, nms, nvs = _adamw_many(*[[at_least_2d(tree[k]) for k in SMALL] for tree in (wts, grads, mom, var)],
                                "adamw_small")
    for name, dl, nm, nv in zip(SMALL, dls, nms, nvs):
        shape = wts[name].shape
        delta[name], new_m[name], new_v[name] = dl.reshape(shape), nm.reshape(shape), nv.reshape(shape)

    return (loss, grad_x, *[grads[k] for k in WEIGHTS], *[delta[k] for k in WEIGHTS],
            *[new_m[k] for k in WEIGHTS], *[new_v[k] for k in WEIGHTS])
```

```python
import functools
import math

import jax
import jax.numpy as jnp
from jax import lax
from jax.experimental import pallas as pl
from jax.experimental.pallas import tpu as pltpu

F32 = jnp.float32
BF16 = jnp.bfloat16
MESH = pl.DeviceIdType.MESH

EPS = 1e-6
SSM_CH = 16
SSM_STATE = 64
GM_CHUNK = 128
GM_HEAD_DIM = 128
SUBLANES = 8
LANES = 128
GROUPS_PER_BLOCK = LANES // SSM_CH
STATES_PER_BLOCK = GROUPS_PER_BLOCK * SSM_STATE
SSM_TIME_CHUNK = 128
N_CHIPS = 4
N_DEV = 8

ADAM_LR = 0.001
ADAM_B1 = 0.9
ADAM_B2 = 0.999
ADAM_EPS = 1e-08
ADAM_WD = 0.01
ADAM_STEP = 10

VMEM_LIMIT = 56 * 1024 * 1024


def _tile(dim, pref, align):
    best = None
    t = align
    while t <= min(dim, pref):
        if dim % t == 0:
            best = t
        t += align
    return best if best is not None else dim


def _params(*sem):
    return pltpu.CompilerParams(dimension_semantics=sem, vmem_limit_bytes=VMEM_LIMIT)


def _gelu(x):
    c = math.sqrt(2.0 / math.pi)
    return 0.5 * x * (1.0 + jnp.tanh(c * (x + 0.044715 * x * x * x)))


def _gelu_and_grad(x):
    c = math.sqrt(2.0 / math.pi)
    t = jnp.tanh(c * (x + 0.044715 * x * x * x))
    g = 0.5 * x * (1.0 + t)
    dg = 0.5 * (1.0 + t) + 0.5 * x * (1.0 - t * t) * c * (1.0 + 3.0 * 0.044715 * x * x)
    return g, dg


def _sigmoid(x):
    return 0.5 * jnp.tanh(0.5 * x) + 0.5


def _matmul(a, b, mode, *, out_dtype=F32, scale=1.0, res=None, tm=512, tn=1024, tk=1024, name="mm",
            out_cols=None, col_off=0, into=None, comm=None):
    if mode == "nn":
        (m, k), (k2, n) = a.shape, b.shape
    elif mode == "nt":
        (m, k), (n, k2) = a.shape, b.shape
    else:
        (k, m), (k2, n) = a.shape, b.shape
    assert k == k2, (a.shape, b.shape, mode)
    tm = _tile(m, tm, 16 if mode != "tn" else LANES)
    tn = _tile(n, tn, LANES)
    tk = _tile(k, tk, LANES if mode != "tn" else 16)
    nk = k // tk
    grid = (m // tm, n // tn, nk)
    if mode == "nn":
        a_spec = pl.BlockSpec((tm, tk), lambda i, j, kk: (i, kk))
        b_spec = pl.BlockSpec((tk, tn), lambda i, j, kk: (kk, j))
        dims = (((1,), (0,)), ((), ()))
    elif mode == "nt":
        a_spec = pl.BlockSpec((tm, tk), lambda i, j, kk: (i, kk))
        b_spec = pl.BlockSpec((tn, tk), lambda i, j, kk: (j, kk))
        dims = (((1,), (1,)), ((), ()))
    else:
        a_spec = pl.BlockSpec((tk, tm), lambda i, j, kk: (kk, i))
        b_spec = pl.BlockSpec((tk, tn), lambda i, j, kk: (kk, j))
        dims = (((0,), (0,)), ((), ()))
    assert col_off % tn == 0
    off = col_off // tn
    r_spec = pl.BlockSpec((tm, tn), lambda i, j, kk: (i, j))
    o_spec = pl.BlockSpec((tm, tn), lambda i, j, kk: (i, j + off))
    has_res = res is not None
    has_into = into is not None

    def body(*refs):
        refs = list(refs)
        a_ref, b_ref = refs[:2]
        pos = 2
        r_ref = None
        if has_res:
            r_ref = refs[pos]
            pos += 1
        if has_into:
            pos += 1
        o_ref = refs[pos]
        acc_ref = refs[pos + 1] if nk > 1 else None
        part = lax.dot_general(a_ref[...].astype(BF16), b_ref[...].astype(BF16), dims,
                               preferred_element_type=F32)

        def finish(r):
            if scale != 1.0:
                r = r * scale
            if has_res:
                r = r + r_ref[...].astype(F32)
            o_ref[...] = r.astype(o_ref.dtype)

        if nk == 1:
            finish(part)
        else:
            kk = pl.program_id(2)

            @pl.when(kk == 0)
            def _():
                acc_ref[...] = part

            @pl.when(kk > 0)
            def _():
                acc_ref[...] += part

            @pl.when(kk == nk - 1)
            def _():
                finish(acc_ref[...])

    in_specs = [a_spec, b_spec]
    args = [a, b]
    if has_res:
        in_specs.append(r_spec)
        args.append(res)
    aliases = {}
    if has_into:
        in_specs.append(pl.BlockSpec(memory_space=pl.ANY))
        args.append(into)
        aliases = {len(args) - 1: 0}
    (out,), comm_outs = _call(
        body, name=name, grid=grid, in_specs=in_specs, out_specs=[o_spec],
        out_shape=[jax.ShapeDtypeStruct((m, n if out_cols is None else out_cols), out_dtype)],
        scratch_shapes=[pltpu.VMEM((tm, tn), F32)] if nk > 1 else [],
        aliases=aliases, semantics=("parallel", "parallel", "arbitrary"), args=args, comm=comm)
    return out if comm is None else (out, comm_outs)


class _Comm:
    def __init__(self, ins, outs, sems, start, finish, alias=None, middle=None):
        self.ins, self.outs, self.sems, self.start, self.finish = list(ins), list(outs), list(sems), start, finish
        self.alias = dict(alias or {})
        self.middle = middle


def _call(body, *, name, grid, in_specs, out_specs, out_shape, args, scratch_shapes=(), semantics=(), aliases=None,
          comm=None):
    in_specs, out_specs, out_shape = list(in_specs), list(out_specs), list(out_shape)
    scratch_shapes = list(scratch_shapes)
    aliases = dict(aliases or {})
    if comm is None:
        outs = pl.pallas_call(
            body, name=name, grid=grid, in_specs=in_specs, out_specs=out_specs, out_shape=out_shape,
            scratch_shapes=scratch_shapes, input_output_aliases=aliases, compiler_params=_params(*semantics),
        )(*args)
        return list(outs), []
    n_in, n_out, n_scr = len(in_specs), len(out_specs), len(scratch_shapes)
    c_in, c_out = len(comm.ins), len(comm.outs)
    for ci, co in comm.alias.items():
        aliases[n_in + ci] = n_out + co

    def hosted(*refs):
        refs = list(refs)
        ins, cins = refs[:n_in], refs[n_in:n_in + c_in]
        p = n_in + c_in
        outs, couts = refs[p:p + n_out], refs[p + n_out:p + n_out + c_out]
        p += n_out + c_out
        scr, sems = refs[p:p + n_scr], refs[p + n_scr:]
        ids = [pl.program_id(a) for a in range(len(grid))]
        first = functools.reduce(jnp.logical_and, [i == 0 for i in ids])
        last = functools.reduce(jnp.logical_and, [i == g - 1 for i, g in zip(ids, grid)])

        total = math.prod(grid)
        late = comm.middle is not None and total >= 4

        @pl.when(first)
        def _():
            comm.start(cins, couts, sems)

        if late:
            flat = functools.reduce(lambda acc, ig: acc * ig[1] + ig[0], zip(ids, grid), 0)

            @pl.when(flat == (3 * total) // 4)
            def _():
                comm.middle(cins, couts, sems)

        body(*ins, *outs, *scr)

        @pl.when(last)
        def _():
            if comm.middle is not None and not late:
                comm.middle(cins, couts, sems)
            comm.finish(cins, couts, sems)

    any_spec = pl.BlockSpec(memory_space=pl.ANY)
    outs = pl.pallas_call(
        hosted, name=name, grid=grid, in_specs=in_specs + [any_spec] * c_in, out_specs=out_specs + [any_spec] * c_out,
        out_shape=out_shape + comm.outs, scratch_shapes=scratch_shapes + comm.sems, input_output_aliases=aliases,
        compiler_params=_params(*(["arbitrary"] * len(grid))),
    )(*args, *comm.ins)
    return list(outs[:n_out]), list(outs[n_out:])


def _run_comm(comm, name):
    c_in, c_out = len(comm.ins), len(comm.outs)

    def body(*refs):
        refs = list(refs)
        cins, couts, sems = refs[:c_in], refs[c_in:c_in + c_out], refs[c_in + c_out:]
        comm.start(cins, couts, sems)
        if comm.middle is not None:
            comm.middle(cins, couts, sems)
        comm.finish(cins, couts, sems)

    any_spec = pl.BlockSpec(memory_space=pl.ANY)
    return list(pl.pallas_call(
        body, name=name, in_specs=[any_spec] * c_in, out_specs=[any_spec] * c_out, out_shape=comm.outs,
        scratch_shapes=comm.sems, input_output_aliases=comm.alias,
    )(*comm.ins))


def _loss_head(x, gain, target):
    n, d = x.shape
    tm = _tile(n, 512, 8)
    steps = n // tm

    def body(x_ref, g_ref, t_ref, dx_ref, dg_ref, loss_ref, acc_ref, lacc_ref):
        i = pl.program_id(0)
        xv = x_ref[...]
        g = g_ref[...]
        r = lax.rsqrt(jnp.mean(xv * xv, axis=-1, keepdims=True) + EPS)
        xh = xv * r
        err = xh * g - t_ref[...]
        dy = err * (1.0 / d)
        dyg = dy * g
        mean = jnp.mean(dyg * xh, axis=-1, keepdims=True)
        dx_ref[...] = r * (dyg - xh * mean)
        part = jnp.sum((dy * xh).reshape(tm // SUBLANES, SUBLANES, d), axis=0)
        lpart = jnp.sum((err * err).reshape(tm // SUBLANES, SUBLANES, d), axis=0)

        @pl.when(i == 0)
        def _():
            acc_ref[...] = part
            lacc_ref[...] = lpart

        @pl.when(i > 0)
        def _():
            acc_ref[...] += part
            lacc_ref[...] += lpart

        @pl.when(i == steps - 1)
        def _():
            dg_ref[...] = jnp.sum(acc_ref[...], axis=0, keepdims=True)
            tot = jnp.sum(jnp.sum(lacc_ref[...], axis=0, keepdims=True), axis=1, keepdims=True)
            loss_ref[...] = jnp.broadcast_to(tot * (0.5 / d), loss_ref.shape)

    row = pl.BlockSpec((tm, d), lambda i: (i, 0))
    vec = pl.BlockSpec((1, d), lambda i: (0, 0))
    dx, dg, loss = pl.pallas_call(
        body, name="loss_head", grid=(steps,),
        in_specs=[row, vec, row],
        out_specs=[row, vec, pl.BlockSpec((1, LANES), lambda i: (0, 0))],
        out_shape=[jax.ShapeDtypeStruct((n, d), F32), jax.ShapeDtypeStruct((1, d), F32),
                   jax.ShapeDtypeStruct((1, LANES), F32)],
        scratch_shapes=[pltpu.VMEM((SUBLANES, d), F32), pltpu.VMEM((SUBLANES, d), F32)],
        compiler_params=_params("arbitrary"),
    )(x, gain.reshape(1, d), target)
    return dx, dg.reshape(d), loss[0, 0]


def _rms_rows(xv):
    return lax.rsqrt(jnp.mean(xv * xv, axis=-1, keepdims=True) + EPS)


def _ffn_in_fwd(x, gain, w_in, name, comm=None):
    n, d = x.shape
    f = w_in.shape[1] // 2
    tm = _tile(n, 256, 16)
    tn = _tile(f, 4096, LANES)
    nj = f // tn

    def body(x_ref, gain_ref, wg_ref, wu_ref, h_ref, t_ref, q_ref, a_ref):
        @pl.when(pl.program_id(1) == 0)
        def _():
            xv = x_ref[...]
            h_ref[...] = (xv * _rms_rows(xv) * gain_ref[...]).astype(h_ref.dtype)

        h = h_ref[...]
        g = jnp.dot(h, wg_ref[...], preferred_element_type=F32)
        u = jnp.dot(h, wu_ref[...], preferred_element_type=F32)
        s = _sigmoid(g)
        t = g * s
        t_ref[...] = t.astype(t_ref.dtype)
        q_ref[...] = (u * (s + t * (1.0 - s))).astype(q_ref.dtype)
        a_ref[...] = (t * u).astype(a_ref.dtype)

    row = pl.BlockSpec((tm, d), lambda i, j: (i, 0))
    tile = pl.BlockSpec((tm, tn), lambda i, j: (i, j))
    act = jax.ShapeDtypeStruct((n, f), BF16)
    outs, comm_outs = _call(
        body, name=name, grid=(n // tm, nj),
        in_specs=[row, pl.BlockSpec((1, d), lambda i, j: (0, 0)),
                  pl.BlockSpec((d, tn), lambda i, j: (0, j)), pl.BlockSpec((d, tn), lambda i, j: (0, j + nj))],
        out_specs=[row, tile, tile, tile],
        out_shape=[jax.ShapeDtypeStruct((n, d), BF16), act, act, act],
        semantics=("parallel", "arbitrary"), args=(x, gain.reshape(1, d), w_in, w_in), comm=comm)
    return outs if comm is None else (outs, comm_outs)


def _ffn_out_bwd(dout, w_out, t, q, name, comm=None):
    n, d = dout.shape
    f = w_out.shape[0]
    tm = _tile(n, 256, 16)
    tn = _tile(f, 4096, LANES)

    def body(d_ref, w_ref, t_ref, q_ref, dg_ref, du_ref):
        da = 0.5 * lax.dot_general(d_ref[...].astype(BF16), w_ref[...], (((1,), (1,)), ((), ())),
                                   preferred_element_type=F32)
        dg_ref[...] = (da * q_ref[...].astype(F32)).astype(dg_ref.dtype)
        du_ref[...] = (da * t_ref[...].astype(F32)).astype(du_ref.dtype)

    tile = pl.BlockSpec((tm, tn), lambda i, j: (i, j))
    act = jax.ShapeDtypeStruct((n, f), BF16)
    outs, comm_outs = _call(
        body, name=name, grid=(n // tm, f // tn),
        in_specs=[pl.BlockSpec((tm, d), lambda i, j: (i, 0)), pl.BlockSpec((tn, d), lambda i, j: (j, 0)), tile, tile],
        out_specs=[tile, tile], out_shape=[act, act],
        semantics=("parallel", "parallel"), args=(dout, w_out, t, q), comm=comm)
    return outs if comm is None else (outs, comm_outs)


def _proj_in_bwd(parts, w, x, gain, dres, name, comm=None):
    n, d = x.shape
    tm = _tile(n, 256, 8)
    steps = n // tm
    np_ = len(parts)
    offs = [off for _, off in parts]
    widths = [a.shape[1] for a, _ in parts]

    def body(*refs):
        a_refs = refs[:np_]
        w_ref, x_ref, g_ref, dr_ref, dx_ref, dg_ref, acc_ref = refs[np_:]
        i = pl.program_id(0)
        dh = None
        for a_ref, off, kp in zip(a_refs, offs, widths):
            part = lax.dot_general(a_ref[...].astype(BF16), w_ref[:, off:off + kp], (((1,), (1,)), ((), ())),
                                   preferred_element_type=F32)
            dh = part if dh is None else dh + part
        xv = x_ref[...]
        r = _rms_rows(xv)
        xh = xv * r
        dyg = dh * g_ref[...]
        mean = jnp.mean(dyg * xh, axis=-1, keepdims=True)
        dx_ref[...] = dr_ref[...] + r * (dyg - xh * mean)
        part = jnp.sum((dh * xh).reshape(tm // SUBLANES, SUBLANES, d), axis=0)

        @pl.when(i == 0)
        def _():
            acc_ref[...] = part

        @pl.when(i > 0)
        def _():
            acc_ref[...] += part

        @pl.when(i == steps - 1)
        def _():
            dg_ref[...] = jnp.sum(acc_ref[...], axis=0, keepdims=True)

    row = pl.BlockSpec((tm, d), lambda i: (i, 0))
    vec = pl.BlockSpec((1, d), lambda i: (0, 0))
    (dx, dg), comm_outs = _call(
        body, name=name, grid=(steps,),
        in_specs=[pl.BlockSpec((tm, kp), lambda i: (i, 0)) for kp in widths]
        + [pl.BlockSpec(w.shape, lambda i: (0, 0)), row, vec, row],
        out_specs=[row, vec],
        out_shape=[jax.ShapeDtypeStruct((n, d), F32), jax.ShapeDtypeStruct((1, d), F32)],
        scratch_shapes=[pltpu.VMEM((SUBLANES, d), F32)],
        semantics=("arbitrary",), args=(*[a for a, _ in parts], w, x, gain.reshape(1, d), dres), comm=comm)
    return (dx, dg.reshape(d)) if comm is None else (dx, dg.reshape(d), comm_outs)


def _mix_in_fwd(x, gain, w, width, name):
    n, d = x.shape
    cols = w.shape[1]
    tm = _tile(n, 512, 16)

    def body(x_ref, gain_ref, w_ref, h_ref, u_ref, z_ref):
        xv = x_ref[...]
        h = (xv * _rms_rows(xv) * gain_ref[...]).astype(h_ref.dtype)
        h_ref[...] = h
        z = jnp.dot(h, w_ref[...], preferred_element_type=F32)
        u_ref[...] = z[:, 0:width]
        z_ref[...] = z[:, width:cols]

    row = pl.BlockSpec((tm, d), lambda i: (i, 0))
    return pl.pallas_call(
        body, name=name, grid=(n // tm,),
        in_specs=[row, pl.BlockSpec((1, d), lambda i: (0, 0)), pl.BlockSpec((d, cols), lambda i: (0, 0))],
        out_specs=[row, pl.BlockSpec((tm, width), lambda i: (i, 0)), pl.BlockSpec((tm, cols - width), lambda i: (i, 0))],
        out_shape=[jax.ShapeDtypeStruct((n, d), BF16), jax.ShapeDtypeStruct((n, width), F32),
                   jax.ShapeDtypeStruct((n, cols - width), F32)],
        compiler_params=_params("parallel"),
    )(x, gain.reshape(1, d), w)


def _tril_mask():
    t = lax.broadcasted_iota(jnp.int32, (GM_CHUNK, GM_CHUNK), 0)
    s = lax.broadcasted_iota(jnp.int32, (GM_CHUNK, GM_CHUNK), 1)
    return s <= t


def _gmlp_fwd(zgm, v_gain, w_s, bias_tile, name):
    n, w2 = zgm.shape
    w = w2 // 2
    heads = w // GM_HEAD_DIM
    tm = _tile(n, 512, GM_CHUNK)
    nq = tm // GM_CHUNK

    def body(u_ref, v_ref, gain_ref, w_ref, b_ref, o_ref):
        mask = _tril_mask()
        ug = _gelu(u_ref[...])
        vg = _gelu(v_ref[...])
        for h in range(heads):
            cols = slice(h * GM_HEAD_DIM, (h + 1) * GM_HEAD_DIM)
            vh = vg[:, cols]
            r = lax.rsqrt(jnp.mean(vh * vh, axis=-1, keepdims=True) + EPS)
            vn = (vh * r * gain_ref[:, cols]).astype(BF16)
            wm = jnp.where(mask, w_ref[h], 0.0).astype(BF16)
            for q in range(nq):
                rows = slice(q * GM_CHUNK, (q + 1) * GM_CHUNK)
                s = jnp.dot(wm, vn[rows], preferred_element_type=F32) + b_ref[:, cols]
                o_ref[rows, cols] = ug[rows, cols] * s

    return pl.pallas_call(
        body, name=name, grid=(n // tm,),
        in_specs=[pl.BlockSpec((tm, w), lambda i: (i, 0)), pl.BlockSpec((tm, w), lambda i: (i, 1)),
                  pl.BlockSpec((1, w), lambda i: (0, 0)),
                  pl.BlockSpec((heads, GM_CHUNK, GM_CHUNK), lambda i: (0, 0, 0)),
                  pl.BlockSpec((GM_CHUNK, w), lambda i: (0, 0))],
        out_specs=pl.BlockSpec((tm, w), lambda i: (i, 0)),
        out_shape=jax.ShapeDtypeStruct((n, w), F32),
        compiler_params=_params("parallel"),
    )(zgm, zgm, v_gain.reshape(1, w), w_s, bias_tile)


def _gmlp_bwd(zgm, dy, v_gain, w_s, bias_tile, name):
    n, w2 = zgm.shape
    w = w2 // 2
    heads = w // GM_HEAD_DIM
    tm = _tile(n, 512, GM_CHUNK)
    nq = tm // GM_CHUNK
    steps = n // tm

    def body(z_ref, dy_ref, gain_ref, w_ref, b_ref, dz_ref, dw_ref, db_ref, dgain_ref):
        i = pl.program_id(0)
        mask = _tril_mask()

        @pl.when(i == 0)
        def _():
            dw_ref[...] = jnp.zeros_like(dw_ref)
            db_ref[...] = jnp.zeros_like(db_ref)
            dgain_ref[...] = jnp.zeros_like(dgain_ref)

        ug, dug_du = _gelu_and_grad(z_ref[:, 0:w])
        vg, dvg_dv = _gelu_and_grad(z_ref[:, w:w2])
        dyv = dy_ref[...]
        for h in range(heads):
            cols = slice(h * GM_HEAD_DIM, (h + 1) * GM_HEAD_DIM)
            vh = vg[:, cols]
            r = lax.rsqrt(jnp.mean(vh * vh, axis=-1, keepdims=True) + EPS)
            vhat = vh * r
            gain = gain_ref[:, cols]
            vn = (vhat * gain).astype(BF16)
            wm = jnp.where(mask, w_ref[h], 0.0).astype(BF16)
            dvn_parts = []
            for q in range(nq):
                rows = slice(q * GM_CHUNK, (q + 1) * GM_CHUNK)
                s = jnp.dot(wm, vn[rows], preferred_element_type=F32) + b_ref[:, cols]
                dyq = dyv[rows, cols]
                dz_ref[rows, cols] = dyq * s * dug_du[rows, cols]
                ds = dyq * ug[rows, cols]
                db_ref[:, cols] += ds
                dsb = ds.astype(BF16)
                dw_ref[h] += lax.dot_general(dsb, vn[rows], (((1,), (1,)), ((), ())), preferred_element_type=F32)
                dvn_parts.append(lax.dot_general(wm, dsb, (((0,), (0,)), ((), ())), preferred_element_type=F32))
            dvn = jnp.concatenate(dvn_parts, axis=0) if nq > 1 else dvn_parts[0]
            dgain_ref[:, cols] += jnp.sum(dvn * vhat, axis=0, keepdims=True)
            dvhat = dvn * gain
            mean = jnp.mean(dvhat * vhat, axis=-1, keepdims=True)
            dz_ref[:, w + h * GM_HEAD_DIM:w + (h + 1) * GM_HEAD_DIM] = r * (dvhat - vhat * mean) * dvg_dv[:, cols]

        @pl.when(i == steps - 1)
        def _():
            for h in range(heads):
                dw_ref[h] = jnp.where(mask, dw_ref[h], 0.0)

    dz, dw, db, dgain = pl.pallas_call(
        body, name=name, grid=(steps,),
        in_specs=[pl.BlockSpec((tm, w2), lambda i: (i, 0)), pl.BlockSpec((tm, w), lambda i: (i, 0)),
                  pl.BlockSpec((1, w), lambda i: (0, 0)),
                  pl.BlockSpec((heads, GM_CHUNK, GM_CHUNK), lambda i: (0, 0, 0)),
                  pl.BlockSpec((GM_CHUNK, w), lambda i: (0, 0))],
        out_specs=[pl.BlockSpec((tm, w2), lambda i: (i, 0)),
                   pl.BlockSpec((heads, GM_CHUNK, GM_CHUNK), lambda i: (0, 0, 0)),
                   pl.BlockSpec((GM_CHUNK, w), lambda i: (0, 0)),
                   pl.BlockSpec((1, w), lambda i: (0, 0))],
        out_shape=[jax.ShapeDtypeStruct((n, w2), F32), jax.ShapeDtypeStruct((heads, GM_CHUNK, GM_CHUNK), F32),
                   jax.ShapeDtypeStruct((GM_CHUNK, w), F32), jax.ShapeDtypeStruct((1, w), F32)],
        compiler_params=_params("arbitrary"),
    )(zgm, dy, v_gain.reshape(1, w), w_s, bias_tile)
    return dz, dw, db, dgain.reshape(w)


def _mixnorm_fwd(y_ssm, y_gm, g1, g2, name):
    n, w = y_ssm.shape
    tm = _tile(n, 512, 16)

    def body(a_ref, b_ref, g1_ref, g2_ref, o_ref):
        for src, g_ref, lo in ((a_ref, g1_ref, 0), (b_ref, g2_ref, w)):
            v = src[...]
            r = lax.rsqrt(jnp.mean(v * v, axis=-1, keepdims=True) + EPS)
            o_ref[:, lo:lo + w] = (v * r * g_ref[...]).astype(o_ref.dtype)

    row = pl.BlockSpec((tm, w), lambda i: (i, 0))
    vec = pl.BlockSpec((1, w), lambda i: (0, 0))
    return pl.pallas_call(
        body, name=name, grid=(n // tm,),
        in_specs=[row, row, vec, vec], out_specs=pl.BlockSpec((tm, 2 * w), lambda i: (i, 0)),
        out_shape=jax.ShapeDtypeStruct((n, 2 * w), BF16),
        compiler_params=_params("parallel"),
    )(y_ssm, y_gm, g1.reshape(1, w), g2.reshape(1, w))


def _mixnorm_bwd(y_ssm, y_gm, g1, g2, dycat, name):
    n, w = y_ssm.shape
    tm = _tile(n, 512, 8)
    steps = n // tm

    def body(a_ref, b_ref, g1_ref, g2_ref, d_ref, da_ref, db_ref, dg1_ref, dg2_ref):
        i = pl.program_id(0)

        @pl.when(i == 0)
        def _():
            dg1_ref[...] = jnp.zeros_like(dg1_ref)
            dg2_ref[...] = jnp.zeros_like(dg2_ref)

        for src, g_ref, lo, dst, dg_ref in ((a_ref, g1_ref, 0, da_ref, dg1_ref), (b_ref, g2_ref, w, db_ref, dg2_ref)):
            v = src[...]
            dh = d_ref[:, lo:lo + w]
            r = lax.rsqrt(jnp.mean(v * v, axis=-1, keepdims=True) + EPS)
            vh = v * r
            dyg = dh * g_ref[...]
            mean = jnp.mean(dyg * vh, axis=-1, keepdims=True)
            dst[...] = r * (dyg - vh * mean)
            dg_ref[...] += jnp.sum(dh * vh, axis=0, keepdims=True)

    row = pl.BlockSpec((tm, w), lambda i: (i, 0))
    vec = pl.BlockSpec((1, w), lambda i: (0, 0))
    da, db, dg1, dg2 = pl.pallas_call(
        body, name=name, grid=(steps,),
        in_specs=[row, row, vec, vec, pl.BlockSpec((tm, 2 * w), lambda i: (i, 0))],
        out_specs=[row, row, vec, vec],
        out_shape=[jax.ShapeDtypeStruct((n, w), F32), jax.ShapeDtypeStruct((n, w), F32),
                   jax.ShapeDtypeStruct((1, w), F32), jax.ShapeDtypeStruct((1, w), F32)],
        compiler_params=_params("arbitrary"),
    )(y_ssm, y_gm, g1.reshape(1, w), g2.reshape(1, w), dycat)
    return da, db, dg1.reshape(w), dg2.reshape(w)


def _discretise(a_re, a_im, log_dt, bt_re, bt_im):
    dt = jnp.exp(log_dt)
    e = jnp.exp(a_re * dt)
    ang = a_im * dt
    lr = e * jnp.cos(ang)
    li = e * jnp.sin(ang)
    den = a_re * a_re + a_im * a_im
    cr = ((lr - 1.0) * a_re + li * a_im) / den
    ci = (li * a_re - (lr - 1.0) * a_im) / den
    cr3 = cr[:, None, :]
    ci3 = ci[:, None, :]
    return lr, li, cr3 * bt_re - ci3 * bt_im, cr3 * bt_im + ci3 * bt_re


def _disc_fwd(a_re, a_im, log_dt, bt_re, bt_im):
    g, p = a_re.shape
    c = bt_re.shape[1]

    def body(are_ref, aim_ref, ldt_ref, bre_ref, bim_ref, lr_ref, li_ref, bbr_ref, bbi_ref):
        lr, li, bbr, bbi = _discretise(are_ref[...], aim_ref[...], ldt_ref[...], bre_ref[...], bim_ref[...])
        lr_ref[...] = lr
        li_ref[...] = li
        bbr_ref[...] = bbr
        bbi_ref[...] = bbi

    return pl.pallas_call(
        body, name="s5_discretise",
        out_shape=[jax.ShapeDtypeStruct((g, p), F32), jax.ShapeDtypeStruct((g, p), F32),
                   jax.ShapeDtypeStruct((g, c, p), F32), jax.ShapeDtypeStruct((g, c, p), F32)],
    )(a_re, a_im, log_dt, bt_re, bt_im)


def _disc_bwd(a_re, a_im, log_dt, bt_re, bt_im, dlr, dli, dbbr, dbbi):
    g, p = a_re.shape
    c = bt_re.shape[1]

    def body(are_ref, aim_ref, ldt_ref, bre_ref, bim_ref, dlr_ref, dli_ref, dbbr_ref, dbbi_ref,
             dare_ref, daim_ref, dldt_ref, dbre_ref, dbim_ref):
        _, vjp = jax.vjp(_discretise, are_ref[...], aim_ref[...], ldt_ref[...], bre_ref[...], bim_ref[...])
        dare, daim, dldt, dbre, dbim = vjp((dlr_ref[...], dli_ref[...], dbbr_ref[...], dbbi_ref[...]))
        dare_ref[...] = dare
        daim_ref[...] = daim
        dldt_ref[...] = dldt
        dbre_ref[...] = dbre
        dbim_ref[...] = dbim

    return pl.pallas_call(
        body, name="s5_discretise_bwd",
        out_shape=[jax.ShapeDtypeStruct((g, p), F32), jax.ShapeDtypeStruct((g, p), F32),
                   jax.ShapeDtypeStruct((g, 1), F32),
                   jax.ShapeDtypeStruct((g, c, p), F32), jax.ShapeDtypeStruct((g, c, p), F32)],
    )(a_re, a_im, log_dt, bt_re, bt_im, dlr, dli, dbbr, dbbi)


def _block_diag(w, nb):
    g, a, b = w.shape
    gpb = g // nb
    eye = jnp.eye(gpb, dtype=w.dtype)
    w4 = w.reshape(nb, gpb, a, b)
    return jnp.einsum("ngab,gh->ngahb", w4, eye).reshape(nb, gpb * a, gpb * b)


def _block_diag_extract(m, gpb):
    nb, ga, gb = m.shape
    a, b = ga // gpb, gb // gpb
    m5 = m.reshape(nb, gpb, a, gpb, b)
    idx = jnp.arange(gpb)
    return m5[:, idx, :, idx, :].transpose(1, 0, 2, 3).reshape(nb * gpb, a, b)


def _ssm_operands(lr, li, bbr, bbi, c_re, c_im, d_skip, glu_w, glu_b):
    g = lr.shape[0]
    nb = g // GROUPS_PER_BLOCK
    s = STATES_PER_BLOCK
    lam = jnp.concatenate([lr.reshape(nb, 1, s), li.reshape(nb, 1, s)], axis=-1)
    b_bd = jnp.concatenate([_block_diag(bbr, nb), _block_diag(bbi, nb)], axis=-1)
    ct_re = jnp.swapaxes(c_re, 1, 2)
    ct_im = jnp.swapaxes(c_im, 1, 2)
    c_bd = jnp.concatenate([_block_diag(ct_re, nb), -_block_diag(ct_im, nb)], axis=1)
    dsk = d_skip.reshape(nb, 1, LANES)
    w_bd = jnp.concatenate([_block_diag(glu_w[:, :, :SSM_CH], nb), _block_diag(glu_w[:, :, SSM_CH:], nb)], axis=-1)
    bias = jnp.concatenate([glu_b[:, :SSM_CH].reshape(nb, 1, LANES), glu_b[:, SSM_CH:].reshape(nb, 1, LANES)], axis=-1)
    return lam, b_bd.astype(BF16), c_bd.astype(BF16), dsk, w_bd.astype(BF16), bias


def _roll_rows(v, shift):
    return v if shift % SUBLANES == 0 else pltpu.roll(v, shift % SUBLANES, 0)


def _scan_chunk_rows(seq, nseq):
    return _tile(seq, max(8 * SSM_TIME_CHUNK // nseq, 8), max(SUBLANES // nseq, 1) * 8)


def _ssm_fwd(u8, ops, nseq, name, comm=None):
    lam, b_bd, c_bd, dsk, w_bd, bias = ops
    rows_total, w = u8.shape
    seq = rows_total // nseq
    nb = w // LANES
    s = STATES_PER_BLOCK
    tc = _scan_chunk_rows(seq, nseq)
    nk = seq // tc
    rows = tc * nseq
    stages = SUBLANES // nseq

    def body(u_ref, lam_ref, b_ref, c_ref, d_ref, w_ref, bias_ref, y_ref, hb_ref, buf, st):
        k = pl.program_id(1)

        @pl.when(k == 0)
        def _():
            st[...] = jnp.zeros_like(st)

        hb_ref[...] = st[...]
        u = u_ref[...]
        buf[...] = jnp.dot(u.astype(BF16), b_ref[0], preferred_element_type=F32)
        lr = jnp.broadcast_to(lam_ref[0, :, 0:s], (SUBLANES, s))
        li = jnp.broadcast_to(lam_ref[0, :, s:2 * s], (SUBLANES, s))
        row = lax.broadcasted_iota(jnp.int32, (SUBLANES, s), 0)

        def step(i, carry):
            pr, pi = carry
            r0 = pl.multiple_of(i * SUBLANES, SUBLANES)
            br = buf[pl.ds(r0, SUBLANES), 0:s]
            bi = buf[pl.ds(r0, SUBLANES), s:2 * s]
            outr = outi = None
            for j in range(stages):
                rr = _roll_rows(pr, nseq)
                ri = _roll_rows(pi, nseq)
                pr = lr * rr - li * ri + br
                pi = lr * ri + li * rr + bi
                outr = pr if j == 0 else jnp.where(row >= j * nseq, pr, outr)
                outi = pi if j == 0 else jnp.where(row >= j * nseq, pi, outi)
            buf[pl.ds(r0, SUBLANES), 0:s] = outr
            buf[pl.ds(r0, SUBLANES), s:2 * s] = outi
            return outr, outi

        hr, hi = lax.fori_loop(0, rows // SUBLANES, step, (st[:, 0:s], st[:, s:2 * s]), unroll=2)
        st[:, 0:s] = hr
        st[:, s:2 * s] = hi
        y = jnp.dot(buf[...].astype(BF16), c_ref[0], preferred_element_type=F32) + d_ref[0] * u
        z = jnp.dot(_gelu(y).astype(BF16), w_ref[0], preferred_element_type=F32) + bias_ref[0]
        y_ref[...] = z[:, 0:LANES] * _sigmoid(z[:, LANES:2 * LANES])

    blk = lambda shape: pl.BlockSpec(shape, lambda b, k: (b, 0, 0))
    (y8, hb), comm_outs = _call(
        body, name=name, grid=(nb, nk),
        in_specs=[pl.BlockSpec((rows, LANES), lambda b, k: (k, b)),
                  blk((1, 1, 2 * s)), blk((1, LANES, 2 * s)), blk((1, 2 * s, LANES)),
                  blk((1, 1, LANES)), blk((1, LANES, 2 * LANES)), blk((1, 1, 2 * LANES))],
        out_specs=[pl.BlockSpec((rows, LANES), lambda b, k: (k, b)),
                   pl.BlockSpec((SUBLANES, 2 * s), lambda b, k: (k, b))],
        out_shape=[jax.ShapeDtypeStruct((rows_total, w), F32),
                   jax.ShapeDtypeStruct((nk * SUBLANES, nb * 2 * s), F32)],
        scratch_shapes=[pltpu.VMEM((rows, 2 * s), F32), pltpu.VMEM((SUBLANES, 2 * s), F32)],
        semantics=("parallel", "arbitrary"), args=(u8, lam, b_bd, c_bd, dsk, w_bd, bias), comm=comm)
    return (y8, hb) if comm is None else (y8, hb, comm_outs)


def _ssm_bwd(u8, dy8, hb, ops, nseq, name, comm=None):
    lam, b_bd, c_bd, dsk, w_bd, bias = ops
    rows_total, w = u8.shape
    seq = rows_total // nseq
    nb = w // LANES
    s = STATES_PER_BLOCK
    tc = _scan_chunk_rows(seq, nseq)
    nk = seq // tc
    rows = tc * nseq
    nblk = rows // SUBLANES
    stages = SUBLANES // nseq
    tn_dims = (((0,), (0,)), ((), ()))
    nt_dims = (((1,), (1,)), ((), ()))

    def body(u_ref, dy_ref, hb_ref, lam_ref, b_ref, c_ref, d_ref, w_ref, bias_ref,
             du_ref, dlam_ref, db_ref, dct_ref, dd_ref, dw_ref, dbias_ref, hbuf, gbuf, gst, lacc):
        k = pl.program_id(1)

        @pl.when(k == 0)
        def _():
            gst[...] = jnp.zeros_like(gst)
            lacc[...] = jnp.zeros_like(lacc)
            db_ref[...] = jnp.zeros_like(db_ref)
            dct_ref[...] = jnp.zeros_like(dct_ref)
            dd_ref[...] = jnp.zeros_like(dd_ref)
            dw_ref[...] = jnp.zeros_like(dw_ref)
            dbias_ref[...] = jnp.zeros_like(dbias_ref)

        u = u_ref[...]
        ub = u.astype(BF16)
        lr = jnp.broadcast_to(lam_ref[0, :, 0:s], (SUBLANES, s))
        li = jnp.broadcast_to(lam_ref[0, :, s:2 * s], (SUBLANES, s))
        row = lax.broadcasted_iota(jnp.int32, (SUBLANES, s), 0)
        hbuf[...] = jnp.dot(ub, b_ref[0], preferred_element_type=F32)

        def fstep(i, carry):
            pr, pi = carry
            r0 = pl.multiple_of(i * SUBLANES, SUBLANES)
            br = hbuf[pl.ds(r0, SUBLANES), 0:s]
            bi = hbuf[pl.ds(r0, SUBLANES), s:2 * s]
            outr = outi = None
            for j in range(stages):
                rr = _roll_rows(pr, nseq)
                ri = _roll_rows(pi, nseq)
                pr = lr * rr - li * ri + br
                pi = lr * ri + li * rr + bi
                outr = pr if j == 0 else jnp.where(row >= j * nseq, pr, outr)
                outi = pi if j == 0 else jnp.where(row >= j * nseq, pi, outi)
            hbuf[pl.ds(r0, SUBLANES), 0:s] = outr
            hbuf[pl.ds(r0, SUBLANES), s:2 * s] = outi
            return outr, outi

        lax.fori_loop(0, nblk, fstep, (hb_ref[:, 0:s], hb_ref[:, s:2 * s]), unroll=2)
        hb16 = hbuf[...].astype(BF16)
        y = jnp.dot(hb16, c_ref[0], preferred_element_type=F32) + d_ref[0] * u
        yg, dyg_dy = _gelu_and_grad(y)
        yg16 = yg.astype(BF16)
        z = jnp.dot(yg16, w_ref[0], preferred_element_type=F32) + bias_ref[0]
        z1 = z[:, 0:LANES]
        sg = _sigmoid(z[:, LANES:2 * LANES])
        dout = dy_ref[...]
        dz = jnp.concatenate([dout * sg, dout * z1 * sg * (1.0 - sg)], axis=-1)
        dz16 = dz.astype(BF16)
        dw_ref[0] += lax.dot_general(yg16, dz16, tn_dims, preferred_element_type=F32)
        dbias_ref[0] += jnp.sum(dz, axis=0, keepdims=True)
        dy = lax.dot_general(dz16, w_ref[0], nt_dims, preferred_element_type=F32) * dyg_dy
        dy16 = dy.astype(BF16)
        dd_ref[0] += jnp.sum(dy * u, axis=0, keepdims=True)
        dct_ref[0] += lax.dot_general(dy16, hb16, tn_dims, preferred_element_type=F32)
        gbuf[...] = lax.dot_general(dy16, c_ref[0], nt_dims, preferred_element_type=F32)

        def bstep(i, carry):
            pr, pi, ar, ai = carry
            blk = nblk - 1 - i
            r0 = pl.multiple_of(blk * SUBLANES, SUBLANES)
            dr = gbuf[pl.ds(r0, SUBLANES), 0:s]
            di = gbuf[pl.ds(r0, SUBLANES), s:2 * s]
            outr = outi = None
            for j in reversed(range(stages)):
                rr = _roll_rows(pr, SUBLANES - nseq)
                ri = _roll_rows(pi, SUBLANES - nseq)
                pr = dr + lr * rr + li * ri
                pi = di - li * rr + lr * ri
                outr = pr if j == stages - 1 else jnp.where(row < (j + 1) * nseq, pr, outr)
                outi = pi if j == stages - 1 else jnp.where(row < (j + 1) * nseq, pi, outi)
            gbuf[pl.ds(r0, SUBLANES), 0:s] = outr
            gbuf[pl.ds(r0, SUBLANES), s:2 * s] = outi
            p0 = pl.multiple_of(jnp.maximum(blk - 1, 0) * SUBLANES, SUBLANES)
            first = blk == 0
            before_r = jnp.where(first, hb_ref[:, 0:s], hbuf[pl.ds(p0, SUBLANES), 0:s])
            before_i = jnp.where(first, hb_ref[:, s:2 * s], hbuf[pl.ds(p0, SUBLANES), s:2 * s])
            if stages > 1:
                last_rows = row >= SUBLANES - nseq
                before_r = _roll_rows(jnp.where(last_rows, before_r, hbuf[pl.ds(r0, SUBLANES), 0:s]), nseq)
                before_i = _roll_rows(jnp.where(last_rows, before_i, hbuf[pl.ds(r0, SUBLANES), s:2 * s]), nseq)
            return (outr, outi, ar + outr * before_r + outi * before_i, ai - outr * before_i + outi * before_r)

        gr, gi, ar, ai = lax.fori_loop(
            0, nblk, bstep, (gst[:, 0:s], gst[:, s:2 * s], lacc[:, 0:s], lacc[:, s:2 * s]))
        gst[:, 0:s] = gr
        gst[:, s:2 * s] = gi
        lacc[:, 0:s] = ar
        lacc[:, s:2 * s] = ai
        g16 = gbuf[...].astype(BF16)
        du_ref[...] = dy * d_ref[0] + lax.dot_general(g16, b_ref[0], nt_dims, preferred_element_type=F32)
        db_ref[0] += lax.dot_general(ub, g16, tn_dims, preferred_element_type=F32)

        @pl.when(k == nk - 1)
        def _():
            dlam_ref[0] = jnp.sum(lacc[...], axis=0, keepdims=True)

    blk = lambda shape: pl.BlockSpec(shape, lambda b, k: (b, 0, 0))
    rev = lambda b, k: (nk - 1 - k, b)
    outs, comm_outs = _call(
        body, name=name, grid=(nb, nk),
        in_specs=[pl.BlockSpec((rows, LANES), rev), pl.BlockSpec((rows, LANES), rev),
                  pl.BlockSpec((SUBLANES, 2 * s), rev),
                  blk((1, 1, 2 * s)), blk((1, LANES, 2 * s)), blk((1, 2 * s, LANES)),
                  blk((1, 1, LANES)), blk((1, LANES, 2 * LANES)), blk((1, 1, 2 * LANES))],
        out_specs=[pl.BlockSpec((rows, LANES), rev),
                   blk((1, 1, 2 * s)), blk((1, LANES, 2 * s)), blk((1, LANES, 2 * s)),
                   blk((1, 1, LANES)), blk((1, LANES, 2 * LANES)), blk((1, 1, 2 * LANES))],
        out_shape=[jax.ShapeDtypeStruct((rows_total, w), F32),
                   jax.ShapeDtypeStruct((nb, 1, 2 * s), F32), jax.ShapeDtypeStruct((nb, LANES, 2 * s), F32),
                   jax.ShapeDtypeStruct((nb, LANES, 2 * s), F32), jax.ShapeDtypeStruct((nb, 1, LANES), F32),
                   jax.ShapeDtypeStruct((nb, LANES, 2 * LANES), F32), jax.ShapeDtypeStruct((nb, 1, 2 * LANES), F32)],
        scratch_shapes=[pltpu.VMEM((rows, 2 * s), F32), pltpu.VMEM((rows, 2 * s), F32),
                        pltpu.VMEM((SUBLANES, 2 * s), F32), pltpu.VMEM((SUBLANES, 2 * s), F32)],
        semantics=("parallel", "arbitrary"), args=(u8, dy8, hb, lam, b_bd, c_bd, dsk, w_bd, bias), comm=comm)
    return outs if comm is None else (outs, comm_outs)


def _to_scan_rows(a, nseq, seq):
    w = a.shape[-1]
    return jnp.swapaxes(a.reshape(nseq, seq, w), 0, 1).reshape(seq * nseq, w)


def _from_scan_rows(a8, nseq, seq):
    w = a8.shape[-1]
    return jnp.swapaxes(a8.reshape(seq, nseq, w), 0, 1).reshape(nseq * seq, w)


ANY = pl.BlockSpec(memory_space=pl.ANY)

BIG = (("ffn1_w_in", True), ("ffn1_w_out", False), ("mix_w_in", True), ("mix_w_out", False),
       ("ffn2_w_in", True), ("ffn2_w_out", False))


def _my_place():
    return lax.axis_index("x"), lax.axis_index("y"), lax.axis_index("c")


def _other_chips(x, y):
    return [(1 - x, y), (x, 1 - y), (1 - x, 1 - y)]


def _half_of_shard(ref, col_sharded, chip, core):
    full_rows, full_cols = ref.shape
    if col_sharded:
        hr, cs = full_rows // 2, full_cols // N_CHIPS
        return ref.at[pl.ds(pl.multiple_of(core * hr, 8), hr), pl.ds(chip * cs, cs)]
    rs = full_rows // N_CHIPS
    return ref.at[pl.ds(pl.multiple_of(chip * rs + core * (rs // 2), 8), rs // 2), :]


def _gather_comm(shards, cols):
    full_shapes = [(sh.shape[0], sh.shape[1] * N_CHIPS) if col else (sh.shape[0] * N_CHIPS, sh.shape[1])
                   for sh, col in zip(shards, cols)]
    nw = len(shards)

    def first_copies(ins, outs, sems):
        send_sems, recv_sems, local_sems = sems
        x, y, c = _my_place()
        me = 2 * x + y
        locals_, sends = [], []
        for wi in range(nw):
            src, dst = ins[wi], outs[wi]
            rs, cs = src.shape
            hs = rs // 2
            if cols[wi]:
                place = dst.at[:, pl.ds(me * cs, cs)]
            else:
                place = dst.at[pl.ds(pl.multiple_of(me * rs, 8), rs), :]
            locals_.append(pltpu.make_async_copy(src, place, local_sems.at[wi]))
            my_half = src.at[pl.ds(pl.multiple_of(c * hs, 8), hs), :]
            for j, (px, py) in enumerate(_other_chips(x, y)):
                sends.append(pltpu.make_async_remote_copy(
                    src_ref=my_half, dst_ref=_half_of_shard(dst, cols[wi], me, c),
                    send_sem=send_sems.at[wi * 6 + j], recv_sem=recv_sems.at[wi * 6 + j],
                    device_id=(px, py, c), device_id_type=MESH))
        return locals_, sends

    def start(ins, outs, sems):
        locals_, sends = first_copies(ins, outs, sems)
        for cp in locals_ + sends:
            cp.start()

    def forwards(outs, sems, wait_landed):
        send_sems, recv_sems, _ = sems
        x, y, c = _my_place()
        out = []
        for wi in range(nw):
            dst = outs[wi]
            for j, (px, py) in enumerate(_other_chips(x, y)):
                got = _half_of_shard(dst, cols[wi], 2 * px + py, c)
                if wait_landed:
                    pltpu.make_async_remote_copy(
                        src_ref=got, dst_ref=got, send_sem=send_sems.at[wi * 6 + j], recv_sem=recv_sems.at[wi * 6 + j],
                        device_id=(px, py, c), device_id_type=MESH).wait_recv()
                out.append(pltpu.make_async_remote_copy(
                    src_ref=got, dst_ref=got, send_sem=send_sems.at[wi * 6 + 3 + j], recv_sem=recv_sems.at[wi * 6 + 3 + j],
                    device_id=(x, y, 1 - c), device_id_type=MESH))
                if wait_landed:
                    out[-1].start()
        return out

    def middle(ins, outs, sems):
        forwards(outs, sems, True)

    def finish(ins, outs, sems):
        send_sems, recv_sems, _ = sems
        x, y, c = _my_place()
        locals_, sends = first_copies(ins, outs, sems)
        for wi in range(nw):
            dst = outs[wi]
            for j, (px, py) in enumerate(_other_chips(x, y)):
                theirs = _half_of_shard(dst, cols[wi], 2 * px + py, 1 - c)
                pltpu.make_async_remote_copy(
                    src_ref=theirs, dst_ref=theirs, send_sem=send_sems.at[wi * 6 + 3 + j],
                    recv_sem=recv_sems.at[wi * 6 + 3 + j], device_id=(x, y, 1 - c), device_id_type=MESH).wait_recv()
        for cp in sends + forwards(outs, sems, False):
            cp.wait_send()
        for cp in locals_:
            cp.wait()

    return _Comm(shards, [jax.ShapeDtypeStruct(s, BF16) for s in full_shapes],
                 [pltpu.SemaphoreType.DMA((6 * nw,)), pltpu.SemaphoreType.DMA((6 * nw,)),
                  pltpu.SemaphoreType.DMA((nw,))], start, finish, middle=middle)


def _pair_exchange_comm(grads, cols):
    nw = len(grads)
    n_copies = sum(1 if col else N_CHIPS for col in cols)

    def copies(ins, outs, sems):
        send_sems, recv_sems = sems
        x, y, c = _my_place()
        out = []
        for wi in range(nw):
            src, dst = ins[wi], outs[wi]
            fr = src.shape[0]
            if cols[wi]:
                hr = fr // 2
                pieces = [(src.at[pl.ds(pl.multiple_of((1 - c) * hr, 8), hr), :], dst)]
            else:
                rs = fr // N_CHIPS
                hs = rs // 2
                pieces = [(src.at[pl.ds(pl.multiple_of(k * rs + (1 - c) * hs, 8), hs), :],
                           dst.at[pl.ds(k * hs, hs), :]) for k in range(N_CHIPS)]
            for s_ref, d_ref in pieces:
                out.append(pltpu.make_async_remote_copy(
                    src_ref=s_ref, dst_ref=d_ref, send_sem=send_sems.at[len(out)], recv_sem=recv_sems.at[len(out)],
                    device_id=(x, y, 1 - c), device_id_type=MESH))
        return out

    def start(ins, outs, sems):
        for cp in copies(ins, outs, sems):
            cp.start()

    def finish(ins, outs, sems):
        for cp in copies(ins, outs, sems):
            cp.wait()

    return _Comm(grads, [jax.ShapeDtypeStruct((g.shape[0] // 2, g.shape[1]), F32) for g in grads],
                 [pltpu.SemaphoreType.DMA((n_copies,)), pltpu.SemaphoreType.DMA((n_copies,))], start, finish)


def _pair_sum(grad, other, col, core, name):
    fr, fc = grad.shape
    pieces = 1 if col else N_CHIPS
    pr = fr // 2 // pieces
    gview = grad.reshape(pieces * 2, pr, fc)
    oview = other.reshape(pieces, pr, fc)
    tr = _tile(pr, 256, 16)

    def body(c_ref, g_ref, o_ref, out_ref):
        out_ref[...] = (g_ref[...] + o_ref[...]).astype(out_ref.dtype)

    out = pl.pallas_call(
        body, name=name,
        grid_spec=pltpu.PrefetchScalarGridSpec(
            num_scalar_prefetch=1, grid=(pieces, pr // tr),
            in_specs=[pl.BlockSpec((1, tr, fc), lambda p, i, cref: (p * 2 + cref[0], i, 0)),
                      pl.BlockSpec((1, tr, fc), lambda p, i, cref: (p, i, 0))],
            out_specs=pl.BlockSpec((1, tr, fc), lambda p, i, cref: (p, i, 0))),
        out_shape=jax.ShapeDtypeStruct((pieces, pr, fc), BF16),
        compiler_params=_params("parallel", "parallel"),
    )(core, gview, oview)
    return out.reshape(fr // 2, fc)


def _chip_exchange_comm(psums, cols):
    nw = len(psums)
    out_shapes = [(N_CHIPS, p.shape[0], p.shape[1] // N_CHIPS) if col else (N_CHIPS, p.shape[0] // N_CHIPS, p.shape[1])
                  for p, col in zip(psums, cols)]

    def copies(ins, outs, sems):
        send_sems, recv_sems, local_sems = sems
        x, y, c = _my_place()
        me = 2 * x + y
        out = []
        for wi in range(nw):
            src = ins[wi]
            mine = outs[wi].at[me]

            def piece(chip, src=src, col=cols[wi]):
                if col:
                    cs = src.shape[1] // N_CHIPS
                    return src.at[:, pl.ds(chip * cs, cs)]
                ps = src.shape[0] // N_CHIPS
                return src.at[pl.ds(pl.multiple_of(chip * ps, 8), ps), :]

            out.append(pltpu.make_async_copy(piece(me), mine, local_sems.at[wi]))
            for j, (px, py) in enumerate(_other_chips(x, y)):
                out.append(pltpu.make_async_remote_copy(
                    src_ref=piece(2 * px + py), dst_ref=mine,
                    send_sem=send_sems.at[wi * 3 + j], recv_sem=recv_sems.at[wi * 3 + j],
                    device_id=(px, py, c), device_id_type=MESH))
        return out

    def start(ins, outs, sems):
        for cp in copies(ins, outs, sems):
            cp.start()

    def finish(ins, outs, sems):
        for cp in copies(ins, outs, sems):
            cp.wait()

    return _Comm(psums, [jax.ShapeDtypeStruct(s, BF16) for s in out_shapes],
                 [pltpu.SemaphoreType.DMA((3 * nw,)), pltpu.SemaphoreType.DMA((3 * nw,)),
                  pltpu.SemaphoreType.DMA((nw,))], start, finish)


def _chip_sum(slots, core, layer, layers, into, name):
    _, hr, cs = slots.shape
    tr = _tile(hr, 256, 16)

    def body(c_ref, s_ref, *rest):
        out_ref = rest[-1]
        acc = s_ref[0].astype(F32)
        for i in range(1, N_CHIPS):
            acc = acc + s_ref[i].astype(F32)
        out_ref[0] = acc

    in_specs = [pl.BlockSpec((N_CHIPS, tr, cs), lambda i, cref: (0, i, 0))]
    args = [core, slots]
    aliases = {}
    if into is not None:
        in_specs.append(pl.BlockSpec(memory_space=pl.ANY))
        args.append(into.reshape(layers * 2, hr, cs))
        aliases = {2: 0}
    out = pl.pallas_call(
        body, name=name,
        grid_spec=pltpu.PrefetchScalarGridSpec(
            num_scalar_prefetch=1, grid=(hr // tr,), in_specs=in_specs,
            out_specs=pl.BlockSpec((1, tr, cs), lambda i, cref: (layer * 2 + cref[0], i, 0))),
        out_shape=jax.ShapeDtypeStruct((layers * 2, hr, cs), F32),
        input_output_aliases=aliases,
        compiler_params=_params("parallel"),
    )(*args)
    return out.reshape(layers, 2 * hr, cs)


def _pair_share_comm(reduced):
    nw = len(reduced)

    def copies(ins, outs, sems):
        send_sems, recv_sems = sems
        x, y, c = _my_place()
        out = []
        for wi in range(nw):
            hs = outs[wi].shape[1] // 2
            mine = outs[wi].at[:, pl.ds(pl.multiple_of(c * hs, 8), hs), :]
            out.append(pltpu.make_async_remote_copy(
                src_ref=mine, dst_ref=mine, send_sem=send_sems.at[wi], recv_sem=recv_sems.at[wi],
                device_id=(x, y, 1 - c), device_id_type=MESH))
        return out

    def start(ins, outs, sems):
        for cp in copies(ins, outs, sems):
            cp.start()

    def finish(ins, outs, sems):
        for cp in copies(ins, outs, sems):
            cp.wait()

    return _Comm(reduced, [jax.ShapeDtypeStruct(r.shape, F32) for r in reduced],
                 [pltpu.SemaphoreType.DMA((nw,)), pltpu.SemaphoreType.DMA((nw,))], start, finish,
                 alias={i: i for i in range(nw)})


def _all_reduce_small(flat, comm):
    rows, lanes = flat.shape
    seg = rows // N_DEV
    c_in, c_out = len(comm.ins), len(comm.outs)

    def body(*refs):
        refs = list(refs)
        in_ref, cins = refs[0], refs[1:1 + c_in]
        out_ref, couts = refs[1 + c_in], refs[2 + c_in:2 + c_in + c_out]
        recv_ref, send_sems, recv_sems = refs[2 + c_in + c_out:5 + c_in + c_out]
        csems = refs[5 + c_in + c_out:]
        comm.start(cins, couts, csems)
        x, y, c = _my_place()
        me = 4 * x + 2 * y + c

        def peer(r):
            fx, fy, fc = (r >> 2) & 1, (r >> 1) & 1, r & 1
            px = jnp.where(fx == 1, 1 - x, x)
            py = jnp.where(fy == 1, 1 - y, y)
            pc = jnp.where(fc == 1, 1 - c, c)
            return px, py, pc

        first = []
        for r in range(1, N_DEV):
            px, py, pc = peer(r)
            theirs = in_ref.at[pl.ds(pl.multiple_of((4 * px + 2 * py + pc) * seg, 8), seg), :]
            cp = pltpu.make_async_remote_copy(
                src_ref=theirs, dst_ref=recv_ref.at[r], send_sem=send_sems.at[r - 1], recv_sem=recv_sems.at[r - 1],
                device_id=(px, py, pc), device_id_type=MESH)
            cp.start()
            first.append(cp)
        for cp in first:
            cp.wait()
        my_rows = pl.ds(pl.multiple_of(me * seg, 8), seg)
        acc = in_ref[my_rows, :]
        for r in range(1, N_DEV):
            acc = acc + recv_ref[r]
        out_ref[my_rows, :] = acc
        second = []
        for r in range(1, N_DEV):
            px, py, pc = peer(r)
            cp = pltpu.make_async_remote_copy(
                src_ref=out_ref.at[my_rows, :], dst_ref=out_ref.at[my_rows, :],
                send_sem=send_sems.at[6 + r], recv_sem=recv_sems.at[6 + r],
                device_id=(px, py, pc), device_id_type=MESH)
            cp.start()
            second.append(cp)
        for r in range(1, N_DEV):
            px, py, pc = peer(r)
            theirs = out_ref.at[pl.ds(pl.multiple_of((4 * px + 2 * py + pc) * seg, 8), seg), :]
            pltpu.make_async_remote_copy(
                src_ref=theirs, dst_ref=theirs, send_sem=send_sems.at[6 + r], recv_sem=recv_sems.at[6 + r],
                device_id=(px, py, pc), device_id_type=MESH).wait_recv()
        for cp in second:
            cp.wait_send()
        comm.finish(cins, couts, csems)

    vm = pl.BlockSpec(memory_space=pltpu.VMEM)
    any_spec = pl.BlockSpec(memory_space=pl.ANY)
    outs = pl.pallas_call(
        body, name="all_reduce_small",
        in_specs=[vm] + [any_spec] * c_in, out_specs=[vm] + [any_spec] * c_out,
        out_shape=[jax.ShapeDtypeStruct((rows, lanes), F32)] + comm.outs,
        scratch_shapes=[pltpu.VMEM((N_DEV, seg, lanes), F32),
                        pltpu.SemaphoreType.DMA((2 * (N_DEV - 1),)), pltpu.SemaphoreType.DMA((2 * (N_DEV - 1),))]
        + comm.sems,
        input_output_aliases={1 + ci: 1 + co for ci, co in comm.alias.items()},
        compiler_params=pltpu.CompilerParams(vmem_limit_bytes=VMEM_LIMIT),
    )(flat, *comm.ins)
    return outs[0], list(outs[1:])


def _adamw_update(w_ref, g_ref, m_ref, v_ref, d_ref, nm_ref, nv_ref):
    c1 = 1.0 - ADAM_B1 ** ADAM_STEP
    c2 = 1.0 - ADAM_B2 ** ADAM_STEP
    gv = g_ref[...]
    nm = ADAM_B1 * m_ref[...] + (1.0 - ADAM_B1) * gv
    nv = ADAM_B2 * v_ref[...] + (1.0 - ADAM_B2) * (gv * gv)
    d_ref[...] = -ADAM_LR * ((nm / c1) / (jnp.sqrt(nv / c2) + ADAM_EPS) + ADAM_WD * w_ref[...])
    nm_ref[...] = nm
    nv_ref[...] = nv


def _adamw_many(ws, gs, ms, vs, name):
    n = len(ws)

    def body(*refs):
        for i in range(n):
            _adamw_update(*[refs[k * n + i] for k in range(7)])

    shapes = [jax.ShapeDtypeStruct(w.shape, F32) for w in ws]
    outs = pl.pallas_call(
        body, name=name, out_shape=shapes * 3,
        compiler_params=pltpu.CompilerParams(vmem_limit_bytes=VMEM_LIMIT),
    )(*ws, *gs, *ms, *vs)
    return outs[:n], outs[n:2 * n], outs[2 * n:]


def _adamw(w, g, m, v, name):
    rows, cols = w.shape
    tr = _tile(rows, 256, 8)

    def body(w_ref, g_ref, m_ref, v_ref, go_ref, d_ref, nm_ref, nv_ref):
        go_ref[...] = g_ref[...]
        _adamw_update(w_ref, g_ref, m_ref, v_ref, d_ref, nm_ref, nv_ref)

    blk = pl.BlockSpec((tr, cols), lambda i: (i, 0))
    sds = jax.ShapeDtypeStruct((rows, cols), F32)
    return pl.pallas_call(
        body, name=name, grid=(rows // tr,),
        in_specs=[blk] * 4, out_specs=[blk] * 4, out_shape=[sds] * 4,
        compiler_params=_params("parallel"),
    )(w, g, m, v)


SMALL = ("norm_ffn1", "norm_mix", "ssm_a_re", "ssm_a_im", "ssm_log_dt", "ssm_b_re", "ssm_b_im", "ssm_c_re",
         "ssm_c_im", "ssm_d", "ssm_glu_w", "ssm_glu_b", "gm_v_gain", "gm_w_s", "gm_b_s", "gain_ssm_out",
         "gain_gm_out", "norm_ffn2", "norm_final")
WEIGHTS = ("norm_ffn1", "ffn1_w_in", "ffn1_w_out", "norm_mix", "mix_w_in", "ssm_a_re", "ssm_a_im", "ssm_log_dt",
           "ssm_b_re", "ssm_b_im", "ssm_c_re", "ssm_c_im", "ssm_d", "ssm_glu_w", "ssm_glu_b", "gm_v_gain", "gm_w_s",
           "gm_b_s", "gain_ssm_out", "gain_gm_out", "mix_w_out", "norm_ffn2", "ffn2_w_in", "ffn2_w_out", "norm_final")


def _ffn_fwd(x, gain, w_in, w_out, tag, hosted=None):
    if hosted is None:
        h, t, q, a = _ffn_in_fwd(x, gain, w_in, f"{tag}_in")
    else:
        (h, t, q, a), got = _ffn_in_fwd(x, gain, w_in, f"{tag}_in_hosting", comm=hosted[0]())
        hosted[1](got)
    if callable(w_out):
        w_out = w_out()
    out = _matmul(a, w_out, "nn", scale=0.5, res=x, tm=512, tn=1024, tk=4096, name=f"{tag}_out")
    return out, (x, h, t, q, a)


def _ffn_bwd(dout, saved, gain, w_in, w_out, tag, hooks=None, publish=None):
    x, h, t, q, a = saved
    f = t.shape[1]
    hooks = hooks or {}

    def hosted(key, fn, *args, name, **kw):
        if key not in hooks:
            return fn(*args, name=name, **kw)
        make, take = hooks[key]
        *res, got = fn(*args, name=f"{name}_hosting", comm=make(), **kw)
        take(got)
        return res[0] if len(res) == 1 else tuple(res)

    dg, du = hosted("out_dx", _ffn_out_bwd, dout, w_out, t, q, name=f"{tag}_out_dx")
    dw_out = hosted("out_dw", _matmul, a, dout, "tn", scale=0.5, tm=1536, tn=1024, tk=2048, name=f"{tag}_out_dw")
    if publish is not None:
        publish("out", dw_out)
    dw_in = hosted("in_dw_g", _matmul, h, dg, "tn", tm=1024, tn=1536, tk=2048, name=f"{tag}_in_dw_g",
                   out_cols=2 * f)
    dw_in = hosted("in_dw_u", _matmul, h, du, "tn", tm=1024, tn=1536, tk=2048, name=f"{tag}_in_dw_u",
                   out_cols=2 * f, col_off=f, into=dw_in)
    if publish is not None:
        publish("in", dw_in)
    dx, dgain = hosted("in_dx", _proj_in_bwd, [(dg, 0), (du, f)], w_in, x, gain, dout, name=f"{tag}_in_dx")
    return dx, dgain, dw_in, dw_out


def kernel(x, norm_ffn1, ffn1_w_in, ffn1_w_out, norm_mix, mix_w_in, ssm_a_re, ssm_a_im, ssm_log_dt, ssm_b_re, ssm_b_im, ssm_c_re, ssm_c_im, ssm_d, ssm_glu_w, ssm_glu_b, gm_v_gain, gm_w_s, gm_b_s, gain_ssm_out, gain_gm_out, mix_w_out, norm_ffn2, ffn2_w_in, ffn2_w_out, norm_final, loss_target, m_norm_ffn1, m_ffn1_w_in, m_ffn1_w_out, m_norm_mix, m_mix_w_in, m_ssm_a_re, m_ssm_a_im, m_ssm_log_dt, m_ssm_b_re, m_ssm_b_im, m_ssm_c_re, m_ssm_c_im, m_ssm_d, m_ssm_glu_w, m_ssm_glu_b, m_gm_v_gain, m_gm_w_s, m_gm_b_s, m_gain_ssm_out, m_gain_gm_out, m_mix_w_out, m_norm_ffn2, m_ffn2_w_in, m_ffn2_w_out, m_norm_final, v_norm_ffn1, v_ffn1_w_in, v_ffn1_w_out, v_norm_mix, v_mix_w_in, v_ssm_a_re, v_ssm_a_im, v_ssm_log_dt, v_ssm_b_re, v_ssm_b_im, v_ssm_c_re, v_ssm_c_im, v_ssm_d, v_ssm_glu_w, v_ssm_glu_b, v_gm_v_gain, v_gm_w_s, v_gm_b_s, v_gain_ssm_out, v_gain_gm_out, v_mix_w_out, v_norm_ffn2, v_ffn2_w_in, v_ffn2_w_out, v_norm_final):
    wts = dict(norm_ffn1=norm_ffn1, ffn1_w_in=ffn1_w_in, ffn1_w_out=ffn1_w_out, norm_mix=norm_mix, mix_w_in=mix_w_in,
               ssm_a_re=ssm_a_re, ssm_a_im=ssm_a_im, ssm_log_dt=ssm_log_dt, ssm_b_re=ssm_b_re, ssm_b_im=ssm_b_im,
               ssm_c_re=ssm_c_re, ssm_c_im=ssm_c_im, ssm_d=ssm_d, ssm_glu_w=ssm_glu_w, ssm_glu_b=ssm_glu_b,
               gm_v_gain=gm_v_gain, gm_w_s=gm_w_s, gm_b_s=gm_b_s, gain_ssm_out=gain_ssm_out, gain_gm_out=gain_gm_out,
               mix_w_out=mix_w_out, norm_ffn2=norm_ffn2, ffn2_w_in=ffn2_w_in, ffn2_w_out=ffn2_w_out,
               norm_final=norm_final)
    mom = dict(norm_ffn1=m_norm_ffn1, ffn1_w_in=m_ffn1_w_in, ffn1_w_out=m_ffn1_w_out, norm_mix=m_norm_mix,
               mix_w_in=m_mix_w_in, ssm_a_re=m_ssm_a_re, ssm_a_im=m_ssm_a_im, ssm_log_dt=m_ssm_log_dt,
               ssm_b_re=m_ssm_b_re, ssm_b_im=m_ssm_b_im, ssm_c_re=m_ssm_c_re, ssm_c_im=m_ssm_c_im, ssm_d=m_ssm_d,
               ssm_glu_w=m_ssm_glu_w, ssm_glu_b=m_ssm_glu_b, gm_v_gain=m_gm_v_gain, gm_w_s=m_gm_w_s, gm_b_s=m_gm_b_s,
               gain_ssm_out=m_gain_ssm_out, gain_gm_out=m_gain_gm_out, mix_w_out=m_mix_w_out, norm_ffn2=m_norm_ffn2,
               ffn2_w_in=m_ffn2_w_in, ffn2_w_out=m_ffn2_w_out, norm_final=m_norm_final)
    var = dict(norm_ffn1=v_norm_ffn1, ffn1_w_in=v_ffn1_w_in, ffn1_w_out=v_ffn1_w_out, norm_mix=v_norm_mix,
               mix_w_in=v_mix_w_in, ssm_a_re=v_ssm_a_re, ssm_a_im=v_ssm_a_im, ssm_log_dt=v_ssm_log_dt,
               ssm_b_re=v_ssm_b_re, ssm_b_im=v_ssm_b_im, ssm_c_re=v_ssm_c_re, ssm_c_im=v_ssm_c_im, ssm_d=v_ssm_d,
               ssm_glu_w=v_ssm_glu_w, ssm_glu_b=v_ssm_glu_b, gm_v_gain=v_gm_v_gain, gm_w_s=v_gm_w_s, gm_b_s=v_gm_b_s,
               gain_ssm_out=v_gain_ssm_out, gain_gm_out=v_gain_gm_out, mix_w_out=v_mix_w_out, norm_ffn2=v_norm_ffn2,
               ffn2_w_in=v_ffn2_w_in, ffn2_w_out=v_ffn2_w_out, norm_final=v_norm_final)

    nseq, seq, d = x.shape
    n = nseq * seq
    depth = norm_ffn1.shape[0]
    width = gain_ssm_out.shape[1]
    groups = ssm_a_re.shape[1]
    heads = gm_w_s.shape[1]
    core = lax.axis_index("c").astype(jnp.int32).reshape(1)

    is_col = dict(BIG)
    full = {name: [None] * depth for name, _ in BIG}

    def gather_comm(pairs):
        return _gather_comm([wts[nm][l].astype(BF16) for nm, l in pairs], [is_col[nm] for nm, _ in pairs])

    def store(pairs, arrays):
        for (nm, l), w in zip(pairs, arrays):
            full[nm][l] = w

    pairs = [("ffn1_w_in", 0)]
    store(pairs, _run_comm(gather_comm(pairs), "all_gather_first"))

    xs = x.reshape(n, d)
    saved = []
    for l in range(depth):
        pairs = [("ffn1_w_out", l)] + ([("mix_w_in", l), ("mix_w_out", l)] if l == 0 else [])
        x1, s_ffn1 = _ffn_fwd(xs, norm_ffn1[l], full["ffn1_w_in"][l], lambda l=l: full["ffn1_w_out"][l], "ffn1",
                              hosted=(functools.partial(gather_comm, pairs), functools.partial(store, pairs)))
        hm, u_ssm, zgm = _mix_in_fwd(x1, norm_mix[l], full["mix_w_in"][l], width, "mix_in")
        bt_re = jnp.swapaxes(ssm_b_re[l], 1, 2)
        bt_im = jnp.swapaxes(ssm_b_im[l], 1, 2)
        disc_in = (ssm_a_re[l], ssm_a_im[l], ssm_log_dt[l].reshape(groups, 1), bt_re, bt_im)
        lr, li, bbr, bbi = _disc_fwd(*disc_in)
        ops = _ssm_operands(lr, li, bbr, bbi, ssm_c_re[l], ssm_c_im[l], ssm_d[l], ssm_glu_w[l], ssm_glu_b[l])
        u8 = _to_scan_rows(u_ssm, nseq, seq)
        pairs = [("ffn2_w_in", l), ("ffn2_w_out", l)]
        y8, hb, got = _ssm_fwd(u8, ops, nseq, "s5_fwd", comm=gather_comm(pairs))
        store(pairs, got)
        y_ssm = _from_scan_rows(y8, nseq, seq)
        bias_tile = jnp.broadcast_to(gm_b_s[l].T[:, :, None], (GM_CHUNK, heads, GM_HEAD_DIM)).reshape(GM_CHUNK, width)
        y_gm = _gmlp_fwd(zgm, gm_v_gain[l], gm_w_s[l], bias_tile, "gmlp_fwd")
        ycat = _mixnorm_fwd(y_ssm, y_gm, gain_ssm_out[l], gain_gm_out[l], "mix_out_norm")
        x2 = _matmul(ycat, full["mix_w_out"][l], "nn", res=x1, tm=512, tn=1024, tk=1024, name="mix_out")
        hosted = None
        if l + 1 < depth:
            pairs = [("ffn1_w_in", l + 1), ("mix_w_in", l + 1), ("mix_w_out", l + 1)]
            hosted = (functools.partial(gather_comm, pairs), functools.partial(store, pairs))
        x3, s_ffn2 = _ffn_fwd(x2, norm_ffn2[l], full["ffn2_w_in"][l], full["ffn2_w_out"][l], "ffn2", hosted=hosted)
        saved.append(dict(ffn1=s_ffn1, x1=x1, hm=hm, zgm=zgm, disc_in=disc_in, ops=ops, u8=u8, hb=hb, y_ssm=y_ssm,
                          bias_tile=bias_tile, y_gm=y_gm, ycat=ycat, ffn2=s_ffn2))
        xs = x3

    dx, g_norm_final, loss_part = _loss_head(xs, norm_final, loss_target.reshape(n, d))
    big = {name: [None] * depth for name, _ in BIG}
    small = {name: [None] * depth for name in SMALL if name != "norm_final"}
    gpb = GROUPS_PER_BLOCK
    s_blk = STATES_PER_BLOCK
    psum_of, reduced = {}, {}

    def swap_comm(pairs):
        return _pair_exchange_comm([big[nm][l] for nm, l in pairs], [is_col[nm] for nm, _ in pairs])

    def take_swapped(pairs, others):
        for (nm, l), other in zip(pairs, others):
            psum_of[nm, l] = _pair_sum(big[nm][l], other, is_col[nm], core, f"grad_pair_sum_{nm}")

    def send_comm(pairs):
        return _chip_exchange_comm([psum_of[p] for p in pairs], [is_col[nm] for nm, _ in pairs])

    def take_sent(pairs, slots):
        for (nm, l), s in zip(pairs, slots):
            reduced[nm] = _chip_sum(s, core, l, depth, reduced.get(nm), f"grad_chip_sum_{nm}")

    def hosting(make, take, pairs):
        return functools.partial(make, pairs), functools.partial(take, pairs)

    for l in reversed(range(depth)):
        sv = saved[l]
        above = [(nm, l + 1) for nm in ("mix_w_in", "mix_w_out", "ffn1_w_in", "ffn1_w_out")] if l + 1 < depth else []
        dx, small["norm_ffn2"][l], big["ffn2_w_in"][l], big["ffn2_w_out"][l] = _ffn_bwd(
            dx, sv["ffn2"], norm_ffn2[l], full["ffn2_w_in"][l], full["ffn2_w_out"][l], "ffn2",
            hooks={"out_dx": hosting(swap_comm, take_swapped, above)} if above else None)
        mine = [("ffn2_w_in", l), ("ffn2_w_out", l)]
        dycat, got = _matmul(dx, full["mix_w_out"][l], "nt", tm=512, tn=1024, tk=1024, name="mix_out_dx",
                             comm=swap_comm(mine))
        take_swapped(mine, got)
        big["mix_w_out"][l] = _matmul(sv["ycat"], dx, "tn", tm=1024, tn=1024, tk=2048, name="mix_out_dw")
        dy_ssm, dy_gm, small["gain_ssm_out"][l], small["gain_gm_out"][l] = _mixnorm_bwd(
            sv["y_ssm"], sv["y_gm"], gain_ssm_out[l], gain_gm_out[l], dycat, "mix_out_norm_bwd")
        dzgm, small["gm_w_s"][l], dbias_tile, small["gm_v_gain"][l] = _gmlp_bwd(
            sv["zgm"], dy_gm, gm_v_gain[l], gm_w_s[l], sv["bias_tile"], "gmlp_bwd")
        small["gm_b_s"][l] = dbias_tile.reshape(GM_CHUNK, heads, GM_HEAD_DIM).sum(-1).T
        dy8 = _to_scan_rows(dy_ssm, nseq, seq)
        (du8, dlam, db_bd, dct_bd, dd, dw_bd, dbias), got = _ssm_bwd(
            sv["u8"], dy8, sv["hb"], sv["ops"], nseq, "s5_bwd", comm=send_comm(mine + above))
        take_sent(mine + above, got)
        du_ssm = _from_scan_rows(du8, nseq, seq)
        dlr = dlam[:, 0, :s_blk].reshape(groups, SSM_STATE)
        dli = dlam[:, 0, s_blk:].reshape(groups, SSM_STATE)
        dbbr = _block_diag_extract(db_bd[:, :, :s_blk], gpb)
        dbbi = _block_diag_extract(db_bd[:, :, s_blk:], gpb)
        da_re, da_im, dldt, dbt_re, dbt_im = _disc_bwd(*sv["disc_in"], dlr, dli, dbbr, dbbi)
        small["ssm_a_re"][l], small["ssm_a_im"][l], small["ssm_log_dt"][l] = da_re, da_im, dldt.reshape(groups)
        small["ssm_b_re"][l] = jnp.swapaxes(dbt_re, 1, 2)
        small["ssm_b_im"][l] = jnp.swapaxes(dbt_im, 1, 2)
        small["ssm_c_re"][l] = _block_diag_extract(dct_bd[:, :, :s_blk], gpb)
        small["ssm_c_im"][l] = -_block_diag_extract(dct_bd[:, :, s_blk:], gpb)
        small["ssm_d"][l] = dd.reshape(groups, SSM_CH)
        small["ssm_glu_w"][l] = jnp.concatenate(
            [_block_diag_extract(dw_bd[:, :, :LANES], gpb), _block_diag_extract(dw_bd[:, :, LANES:], gpb)], axis=-1)
        small["ssm_glu_b"][l] = jnp.concatenate(
            [dbias[:, 0, :LANES].reshape(groups, SSM_CH), dbias[:, 0, LANES:].reshape(groups, SSM_CH)], axis=-1)
        cols_mi = 3 * width
        dw_mi = _matmul(sv["hm"], du_ssm, "tn", tm=1024, tn=width, tk=2048, name="mix_in_dw_ssm", out_cols=cols_mi)
        big["mix_w_in"][l] = _matmul(sv["hm"], dzgm, "tn", tm=1024, tn=width, tk=2048, name="mix_in_dw_gm",
                                     out_cols=cols_mi, col_off=width, into=dw_mi)
        dx, small["norm_mix"][l] = _proj_in_bwd([(du_ssm, 0), (dzgm, width)], full["mix_w_in"][l], sv["x1"],
                                                norm_mix[l], dx, "mix_in_dx")
        hooks = None
        if l == 0:
            mix, w_out_0, w_in_0 = [("mix_w_in", 0), ("mix_w_out", 0)], [("ffn1_w_out", 0)], [("ffn1_w_in", 0)]
            hooks = {"out_dx": hosting(swap_comm, take_swapped, mix), "out_dw": hosting(send_comm, take_sent, mix),
                     "in_dw_g": hosting(swap_comm, take_swapped, w_out_0),
                     "in_dw_u": hosting(send_comm, take_sent, w_out_0),
                     "in_dx": hosting(swap_comm, take_swapped, w_in_0)}

        def publish(which, dw, l=l):
            big[f"ffn1_w_{which}"][l] = dw

        dx, small["norm_ffn1"][l], big["ffn1_w_in"][l], big["ffn1_w_out"][l] = _ffn_bwd(
            dx, sv["ffn1"], norm_ffn1[l], full["ffn1_w_in"][l], full["ffn1_w_out"][l], "ffn1",
            hooks=hooks, publish=publish)
    grad_x = dx.reshape(nseq, seq, d)

    pieces = [jnp.stack(small[name]).reshape(-1) for name in SMALL if name != "norm_final"]
    pieces += [g_norm_final.reshape(-1), loss_part.reshape(1)]
    sizes = [p.shape[0] for p in pieces]
    total = sum(sizes)
    rows = -(-total // (LANES * N_DEV * SUBLANES)) * N_DEV * SUBLANES
    pad = rows * LANES - total
    tail = [("ffn1_w_in", 0)]
    flat_g, got = _all_reduce_small(
        jnp.concatenate(pieces + [jnp.zeros((pad,), F32)]).reshape(rows, LANES), send_comm(tail))
    take_sent(tail, got)
    flat_g = flat_g.reshape(-1)
    loss = flat_g[total - 1]

    names = [name for name, _ in BIG]
    grads = dict(zip(names, _run_comm(_pair_share_comm([reduced[nm] for nm in names]), "grad_pair_share")))
    offs = 0
    for name, size in zip(SMALL, sizes[:-1]):
        grads[name] = flat_g[offs:offs + size].reshape(wts[name].shape)
        offs += size

    delta, new_m, new_v = {}, {}, {}
    for name, _ in BIG:
        shape = wts[name].shape
        two_d = lambda a: a.reshape(shape[0] * shape[1], shape[2])
        go, dl, nm, nv = _adamw(two_d(wts[name]), two_d(grads[name]), two_d(mom[name]), two_d(var[name]),
                                f"adamw_{name}")
        grads[name] = go.reshape(shape)
        delta[name], new_m[name], new_v[name] = dl.reshape(shape), nm.reshape(shape), nv.reshape(shape)
    at_least_2d = lambda a: a.reshape(1, -1) if a.ndim == 1 else a
    dls, nms, nvs = _adamw_many(*[[at_least_2d(tree[k]) for k in SMALL] for tree in (wts, grads, mom, var)],
                                "adamw_small")
    for name, dl, nm, nv in zip(SMALL, dls, nms, nvs):
        shape = wts[name].shape
        delta[name], new_m[name], new_v[name] = dl.reshape(shape), nm.reshape(shape), nv.reshape(shape)

    return (loss, grad_x, *[grads[k] for k in WEIGHTS], *[delta[k] for k in WEIGHTS],
            *[new_m[k] for k in WEIGHTS], *[new_v[k] for k in WEIGHTS])
```

```python
import functools
import math

import jax
import jax.numpy as jnp
from jax import lax
from jax.experimental import pallas as pl
from jax.experimental.pallas import tpu as pltpu

F32 = jnp.float32
BF16 = jnp.bfloat16
MESH = pl.DeviceIdType.MESH

EPS = 1e-6
SSM_CH = 16
SSM_STATE = 64
GM_CHUNK = 128
GM_HEAD_DIM = 128
SUBLANES = 8
LANES = 128
GROUPS_PER_BLOCK = LANES // SSM_CH
STATES_PER_BLOCK = GROUPS_PER_BLOCK * SSM_STATE
SSM_TIME_CHUNK = 128
N_CHIPS = 4
N_DEV = 8

ADAM_LR = 0.001
ADAM_B1 = 0.9
ADAM_B2 = 0.999
ADAM_EPS = 1e-08
ADAM_WD = 0.01
ADAM_STEP = 10

VMEM_LIMIT = 56 * 1024 * 1024


def _tile(dim, pref, align):
    best = None
    t = align
    while t <= min(dim, pref):
        if dim % t == 0:
            best = t
        t += align
    return best if best is not None else dim


def _params(*sem):
    return pltpu.CompilerParams(dimension_semantics=sem, vmem_limit_bytes=VMEM_LIMIT)


def _gelu(x):
    c = math.sqrt(2.0 / math.pi)
    return 0.5 * x * (1.0 + jnp.tanh(c * (x + 0.044715 * x * x * x)))


def _gelu_and_grad(x):
    c = math.sqrt(2.0 / math.pi)
    t = jnp.tanh(c * (x + 0.044715 * x * x * x))
    g = 0.5 * x * (1.0 + t)
    dg = 0.5 * (1.0 + t) + 0.5 * x * (1.0 - t * t) * c * (1.0 + 3.0 * 0.044715 * x * x)
    return g, dg


def _sigmoid(x):
    return 0.5 * jnp.tanh(0.5 * x) + 0.5


def _matmul(a, b, mode, *, out_dtype=F32, scale=1.0, res=None, tm=512, tn=1024, tk=1024, name="mm",
            out_cols=None, col_off=0, into=None, comm=None):
    if mode == "nn":
        (m, k), (k2, n) = a.shape, b.shape
    elif mode == "nt":
        (m, k), (n, k2) = a.shape, b.shape
    else:
        (k, m), (k2, n) = a.shape, b.shape
    assert k == k2, (a.shape, b.shape, mode)
    tm = _tile(m, tm, 16 if mode != "tn" else LANES)
    tn = _tile(n, tn, LANES)
    tk = _tile(k, tk, LANES if mode != "tn" else 16)
    nk = k // tk
    grid = (m // tm, n // tn, nk)
    if mode == "nn":
        a_spec = pl.BlockSpec((tm, tk), lambda i, j, kk: (i, kk))
        b_spec = pl.BlockSpec((tk, tn), lambda i, j, kk: (kk, j))
        dims = (((1,), (0,)), ((), ()))
    elif mode == "nt":
        a_spec = pl.BlockSpec((tm, tk), lambda i, j, kk: (i, kk))
        b_spec = pl.BlockSpec((tn, tk), lambda i, j, kk: (j, kk))
        dims = (((1,), (1,)), ((), ()))
    else:
        a_spec = pl.BlockSpec((tk, tm), lambda i, j, kk: (kk, i))
        b_spec = pl.BlockSpec((tk, tn), lambda i, j, kk: (kk, j))
        dims = (((0,), (0,)), ((), ()))
    assert col_off % tn == 0
    off = col_off // tn
    r_spec = pl.BlockSpec((tm, tn), lambda i, j, kk: (i, j))
    o_spec = pl.BlockSpec((tm, tn), lambda i, j, kk: (i, j + off))
    has_res = res is not None
    has_into = into is not None

    def body(*refs):
        refs = list(refs)
        a_ref, b_ref = refs[:2]
        pos = 2
        r_ref = None
        if has_res:
            r_ref = refs[pos]
            pos += 1
        if has_into:
            pos += 1
        o_ref = refs[pos]
        acc_ref = refs[pos + 1] if nk > 1 else None
        part = lax.dot_general(a_ref[...].astype(BF16), b_ref[...].astype(BF16), dims,
                               preferred_element_type=F32)

        def finish(r):
            if scale != 1.0:
                r = r * scale
            if has_res:
                r = r + r_ref[...].astype(F32)
            o_ref[...] = r.astype(o_ref.dtype)

        if nk == 1:
            finish(part)
        else:
            kk = pl.program_id(2)

            @pl.when(kk == 0)
            def _():
                acc_ref[...] = part

            @pl.when(kk > 0)
            def _():
                acc_ref[...] += part

            @pl.when(kk == nk - 1)
            def _():
                finish(acc_ref[...])

    in_specs = [a_spec, b_spec]
    args = [a, b]
    if has_res:
        in_specs.append(r_spec)
        args.append(res)
    aliases = {}
    if has_into:
        in_specs.append(pl.BlockSpec(memory_space=pl.ANY))
        args.append(into)
        aliases = {len(args) - 1: 0}
    (out,), comm_outs = _call(
        body, name=name, grid=grid, in_specs=in_specs, out_specs=[o_spec],
        out_shape=[jax.ShapeDtypeStruct((m, n if out_cols is None else out_cols), out_dtype)],
        scratch_shapes=[pltpu.VMEM((tm, tn), F32)] if nk > 1 else [],
        aliases=aliases, semantics=("parallel", "parallel", "arbitrary"), args=args, comm=comm)
    return out if comm is None else (out, comm_outs)


class _Comm:
    def __init__(self, ins, outs, sems, start, finish, alias=None, middle=None):
        self.ins, self.outs, self.sems, self.start, self.finish = list(ins), list(outs), list(sems), start, finish
        self.alias = dict(alias or {})
        self.middle = middle


def _call(body, *, name, grid, in_specs, out_specs, out_shape, args, scratch_shapes=(), semantics=(), aliases=None,
          comm=None):
    in_specs, out_specs, out_shape = list(in_specs), list(out_specs), list(out_shape)
    scratch_shapes = list(scratch_shapes)
    aliases = dict(aliases or {})
    if comm is None:
        outs = pl.pallas_call(
            body, name=name, grid=grid, in_specs=in_specs, out_specs=out_specs, out_shape=out_shape,
            scratch_shapes=scratch_shapes, input_output_aliases=aliases, compiler_params=_params(*semantics),
        )(*args)
        return list(outs), []
    n_in, n_out, n_scr = len(in_specs), len(out_specs), len(scratch_shapes)
    c_in, c_out = len(comm.ins), len(comm.outs)
    for ci, co in comm.alias.items():
        aliases[n_in + ci] = n_out + co

    def hosted(*refs):
        refs = list(refs)
        ins, cins = refs[:n_in], refs[n_in:n_in + c_in]
        p = n_in + c_in
        outs, couts = refs[p:p + n_out], refs[p + n_out:p + n_out + c_out]
        p += n_out + c_out
        scr, sems = refs[p:p + n_scr], refs[p + n_scr:]
        ids = [pl.program_id(a) for a in range(len(grid))]
        first = functools.reduce(jnp.logical_and, [i == 0 for i in ids])
        last = functools.reduce(jnp.logical_and, [i == g - 1 for i, g in zip(ids, grid)])

        total = math.prod(grid)
        late = comm.middle is not None and total >= 4

        @pl.when(first)
        def _():
            comm.start(cins, couts, sems)

        if late:
            flat = functools.reduce(lambda acc, ig: acc * ig[1] + ig[0], zip(ids, grid), 0)

            @pl.when(flat == (3 * total) // 4)
            def _():
                comm.middle(cins, couts, sems)

        body(*ins, *outs, *scr)

        @pl.when(last)
        def _():
            if comm.middle is not None and not late:
                comm.middle(cins, couts, sems)
            comm.finish(cins, couts, sems)

    any_spec = pl.BlockSpec(memory_space=pl.ANY)
    outs = pl.pallas_call(
        hosted, name=name, grid=grid, in_specs=in_specs + [any_spec] * c_in, out_specs=out_specs + [any_spec] * c_out,
        out_shape=out_shape + comm.outs, scratch_shapes=scratch_shapes + comm.sems, input_output_aliases=aliases,
        compiler_params=_params(*(["arbitrary"] * len(grid))),
    )(*args, *comm.ins)
    return list(outs[:n_out]), list(outs[n_out:])


def _run_comm(comm, name):
    c_in, c_out = len(comm.ins), len(comm.outs)

    def body(*refs):
        refs = list(refs)
        cins, couts, sems = refs[:c_in], refs[c_in:c_in + c_out], refs[c_in + c_out:]
        comm.start(cins, couts, sems)
        if comm.middle is not None:
            comm.middle(cins, couts, sems)
        comm.finish(cins, couts, sems)

    any_spec = pl.BlockSpec(memory_space=pl.ANY)
    return list(pl.pallas_call(
        body, name=name, in_specs=[any_spec] * c_in, out_specs=[any_spec] * c_out, out_shape=comm.outs,
        scratch_shapes=comm.sems, input_output_aliases=comm.alias,
    )(*comm.ins))


def _loss_head(x, gain, target):
    n, d = x.shape
    tm = _tile(n, 512, 8)
    steps = n // tm

    def body(x_ref, g_ref, t_ref, dx_ref, dg_ref, loss_ref, acc_ref, lacc_ref):
        i = pl.program_id(0)
        xv = x_ref[...]
        g = g_ref[...]
        r = lax.rsqrt(jnp.mean(xv * xv, axis=-1, keepdims=True) + EPS)
        xh = xv * r
        err = xh * g - t_ref[...]
        dy = err * (1.0 / d)
        dyg = dy * g
        mean = jnp.mean(dyg * xh, axis=-1, keepdims=True)
        dx_ref[...] = r * (dyg - xh * mean)
        part = jnp.sum((dy * xh).reshape(tm // SUBLANES, SUBLANES, d), axis=0)
        lpart = jnp.sum((err * err).reshape(tm // SUBLANES, SUBLANES, d), axis=0)

        @pl.when(i == 0)
        def _():
            acc_ref[...] = part
            lacc_ref[...] = lpart

        @pl.when(i > 0)
        def _():
            acc_ref[...] += part
            lacc_ref[...] += lpart

        @pl.when(i == steps - 1)
        def _():
            dg_ref[...] = jnp.sum(acc_ref[...], axis=0, keepdims=True)
            tot = jnp.sum(jnp.sum(lacc_ref[...], axis=0, keepdims=True), axis=1, keepdims=True)
            loss_ref[...] = jnp.broadcast_to(tot * (0.5 / d), loss_ref.shape)

    row = pl.BlockSpec((tm, d), lambda i: (i, 0))
    vec = pl.BlockSpec((1, d), lambda i: (0, 0))
    dx, dg, loss = pl.pallas_call(
        body, name="loss_head", grid=(steps,),
        in_specs=[row, vec, row],
        out_specs=[row, vec, pl.BlockSpec((1, LANES), lambda i: (0, 0))],
        out_shape=[jax.ShapeDtypeStruct((n, d), F32), jax.ShapeDtypeStruct((1, d), F32),
                   jax.ShapeDtypeStruct((1, LANES), F32)],
        scratch_shapes=[pltpu.VMEM((SUBLANES, d), F32), pltpu.VMEM((SUBLANES, d), F32)],
        compiler_params=_params("arbitrary"),
    )(x, gain.reshape(1, d), target)
    return dx, dg.reshape(d), loss[0, 0]


def _rms_rows(xv):
    return lax.rsqrt(jnp.mean(xv * xv, axis=-1, keepdims=True) + EPS)


def _ffn_in_fwd(x, gain, w_in, name, comm=None):
    n, d = x.shape
    f = w_in.shape[1] // 2
    tm = _tile(n, 256, 16)
    tn = _tile(f, 4096, LANES)
    nj = f // tn

    def body(x_ref, gain_ref, wg_ref, wu_ref, h_ref, t_ref, q_ref, a_ref):
        @pl.when(pl.program_id(1) == 0)
        def _():
            xv = x_ref[...]
            h_ref[...] = (xv * _rms_rows(xv) * gain_ref[...]).astype(h_ref.dtype)

        h = h_ref[...]
        g = jnp.dot(h, wg_ref[...], preferred_element_type=F32)
        u = jnp.dot(h, wu_ref[...], preferred_element_type=F32)
        s = _sigmoid(g)
        t = g * s
        t_ref[...] = t.astype(t_ref.dtype)
        q_ref[...] = (u * (s + t * (1.0 - s))).astype(q_ref.dtype)
        a_ref[...] = (t * u).astype(a_ref.dtype)

    row = pl.BlockSpec((tm, d), lambda i, j: (i, 0))
    tile = pl.BlockSpec((tm, tn), lambda i, j: (i, j))
    act = jax.ShapeDtypeStruct((n, f), BF16)
    outs, comm_outs = _call(
        body, name=name, grid=(n // tm, nj),
        in_specs=[row, pl.BlockSpec((1, d), lambda i, j: (0, 0)),
                  pl.BlockSpec((d, tn), lambda i, j: (0, j)), pl.BlockSpec((d, tn), lambda i, j: (0, j + nj))],
        out_specs=[row, tile, tile, tile],
        out_shape=[jax.ShapeDtypeStruct((n, d), BF16), act, act, act],
        semantics=("parallel", "arbitrary"), args=(x, gain.reshape(1, d), w_in, w_in), comm=comm)
    return outs if comm is None else (outs, comm_outs)


def _ffn_out_bwd(dout, w_out, t, q, name, comm=None):
    n, d = dout.shape
    f = w_out.shape[0]
    tm = _tile(n, 256, 16)
    tn = _tile(f, 4096, LANES)

    def body(d_ref, w_ref, t_ref, q_ref, dg_ref, du_ref):
        da = 0.5 * lax.dot_general(d_ref[...].astype(BF16), w_ref[...], (((1,), (1,)), ((), ())),
                                   preferred_element_type=F32)
        dg_ref[...] = (da * q_ref[...].astype(F32)).astype(dg_ref.dtype)
        du_ref[...] = (da * t_ref[...].astype(F32)).astype(du_ref.dtype)

    tile = pl.BlockSpec((tm, tn), lambda i, j: (i, j))
    act = jax.ShapeDtypeStruct((n, f), BF16)
    outs, comm_outs = _call(
        body, name=name, grid=(n // tm, f // tn),
        in_specs=[pl.BlockSpec((tm, d), lambda i, j: (i, 0)), pl.BlockSpec((tn, d), lambda i, j: (j, 0)), tile, tile],
        out_specs=[tile, tile], out_shape=[act, act],
        semantics=("parallel", "parallel"), args=(dout, w_out, t, q), comm=comm)
    return outs if comm is None else (outs, comm_outs)


def _proj_in_bwd(parts, w, x, gain, dres, name, comm=None):
    n, d = x.shape
    tm = _tile(n, 256, 8)
    steps = n // tm
    np_ = len(parts)
    offs = [off for _, off in parts]
    widths = [a.shape[1] for a, _ in parts]

    def body(*refs):
        a_refs = refs[:np_]
        w_ref, x_ref, g_ref, dr_ref, dx_ref, dg_ref, acc_ref = refs[np_:]
        i = pl.program_id(0)
        dh = None
        for a_ref, off, kp in zip(a_refs, offs, widths):
            part = lax.dot_general(a_ref[...].astype(BF16), w_ref[:, off:off + kp], (((1,), (1,)), ((), ())),
                                   preferred_element_type=F32)
            dh = part if dh is None else dh + part
        xv = x_ref[...]
        r = _rms_rows(xv)
        xh = xv * r
        dyg = dh * g_ref[...]
        mean = jnp.mean(dyg * xh, axis=-1, keepdims=True)
        dx_ref[...] = dr_ref[...] + r * (dyg - xh * mean)
        part = jnp.sum((dh * xh).reshape(tm // SUBLANES, SUBLANES, d), axis=0)

        @pl.when(i == 0)
        def _():
            acc_ref[...] = part

        @pl.when(i > 0)
        def _():
            acc_ref[...] += part

        @pl.when(i == steps - 1)
        def _():
            dg_ref[...] = jnp.sum(acc_ref[...], axis=0, keepdims=True)

    row = pl.BlockSpec((tm, d), lambda i: (i, 0))
    vec = pl.BlockSpec((1, d), lambda i: (0, 0))
    (dx, dg), comm_outs = _call(
        body, name=name, grid=(steps,),
        in_specs=[pl.BlockSpec((tm, kp), lambda i: (i, 0)) for kp in widths]
        + [pl.BlockSpec(w.shape, lambda i: (0, 0)), row, vec, row],
        out_specs=[row, vec],
        out_shape=[jax.ShapeDtypeStruct((n, d), F32), jax.ShapeDtypeStruct((1, d), F32)],
        scratch_shapes=[pltpu.VMEM((SUBLANES, d), F32)],
        semantics=("arbitrary",), args=(*[a for a, _ in parts], w, x, gain.reshape(1, d), dres), comm=comm)
    return (dx, dg.reshape(d)) if comm is None else (dx, dg.reshape(d), comm_outs)


def _mix_in_fwd(x, gain, w, width, name):
    n, d = x.shape
    cols = w.shape[1]
    tm = _tile(n, 512, 16)

    def body(x_ref, gain_ref, w_ref, h_ref, u_ref, z_ref):
        xv = x_ref[...]
        h = (xv * _rms_rows(xv) * gain_ref[...]).astype(h_ref.dtype)
        h_ref[...] = h
        z = jnp.dot(h, w_ref[...], preferred_element_type=F32)
        u_ref[...] = z[:, 0:width]
        z_ref[...] = z[:, width:cols]

    row = pl.BlockSpec((tm, d), lambda i: (i, 0))
    return pl.pallas_call(
        body, name=name, grid=(n // tm,),
        in_specs=[row, pl.BlockSpec((1, d), lambda i: (0, 0)), pl.BlockSpec((d, cols), lambda i: (0, 0))],
        out_specs=[row, pl.BlockSpec((tm, width), lambda i: (i, 0)), pl.BlockSpec((tm, cols - width), lambda i: (i, 0))],
        out_shape=[jax.ShapeDtypeStruct((n, d), BF16), jax.ShapeDtypeStruct((n, width), F32),
                   jax.ShapeDtypeStruct((n, cols - width), F32)],
        compiler_params=_params("parallel"),
    )(x, gain.reshape(1, d), w)


def _tril_mask():
    t = lax.broadcasted_iota(jnp.int32, (GM_CHUNK, GM_CHUNK), 0)
    s = lax.broadcasted_iota(jnp.int32, (GM_CHUNK, GM_CHUNK), 1)
    return s <= t


def _gmlp_fwd(zgm, v_gain, w_s, bias_tile, name):
    n, w2 = zgm.shape
    w = w2 // 2
    heads = w // GM_HEAD_DIM
    tm = _tile(n, 512, GM_CHUNK)
    nq = tm // GM_CHUNK

    def body(u_ref, v_ref, gain_ref, w_ref, b_ref, o_ref):
        mask = _tril_mask()
        ug = _gelu(u_ref[...])
        vg = _gelu(v_ref[...])
        for h in range(heads):
            cols = slice(h * GM_HEAD_DIM, (h + 1) * GM_HEAD_DIM)
            vh = vg[:, cols]
            r = lax.rsqrt(jnp.mean(vh * vh, axis=-1, keepdims=True) + EPS)
            vn = (vh * r * gain_ref[:, cols]).astype(BF16)
            wm = jnp.where(mask, w_ref[h], 0.0).astype(BF16)
            for q in range(nq):
                rows = slice(q * GM_CHUNK, (q + 1) * GM_CHUNK)
                s = jnp.dot(wm, vn[rows], preferred_element_type=F32) + b_ref[:, cols]
                o_ref[rows, cols] = ug[rows, cols] * s

    return pl.pallas_call(
        body, name=name, grid=(n // tm,),
        in_specs=[pl.BlockSpec((tm, w), lambda i: (i, 0)), pl.BlockSpec((tm, w), lambda i: (i, 1)),
                  pl.BlockSpec((1, w), lambda i: (0, 0)),
                  pl.BlockSpec((heads, GM_CHUNK, GM_CHUNK), lambda i: (0, 0, 0)),
                  pl.BlockSpec((GM_CHUNK, w), lambda i: (0, 0))],
        out_specs=pl.BlockSpec((tm, w), lambda i: (i, 0)),
        out_shape=jax.ShapeDtypeStruct((n, w), F32),
        compiler_params=_params("parallel"),
    )(zgm, zgm, v_gain.reshape(1, w), w_s, bias_tile)


def _gmlp_bwd(zgm, dy, v_gain, w_s, bias_tile, name):
    n, w2 = zgm.shape
    w = w2 // 2
    heads = w // GM_HEAD_DIM
    tm = _tile(n, 512, GM_CHUNK)
    nq = tm // GM_CHUNK
    steps = n // tm

    def body(z_ref, dy_ref, gain_ref, w_ref, b_ref, dz_ref, dw_ref, db_ref, dgain_ref):
        i = pl.program_id(0)
        mask = _tril_mask()

        @pl.when(i == 0)
        def _():
            dw_ref[...] = jnp.zeros_like(dw_ref)
            db_ref[...] = jnp.zeros_like(db_ref)
            dgain_ref[...] = jnp.zeros_like(dgain_ref)

        ug, dug_du = _gelu_and_grad(z_ref[:, 0:w])
        vg, dvg_dv = _gelu_and_grad(z_ref[:, w:w2])
        dyv = dy_ref[...]
        for h in range(heads):
            cols = slice(h * GM_HEAD_DIM, (h + 1) * GM_HEAD_DIM)
            vh = vg[:, cols]
            r = lax.rsqrt(jnp.mean(vh * vh, axis=-1, keepdims=True) + EPS)
            vhat = vh * r
            gain = gain_ref[:, cols]
            vn = (vhat * gain).astype(BF16)
            wm = jnp.where(mask, w_ref[h], 0.0).astype(BF16)
            dvn_parts = []
            for q in range(nq):
                rows = slice(q * GM_CHUNK, (q + 1) * GM_CHUNK)
                s = jnp.dot(wm, vn[rows], preferred_element_type=F32) + b_ref[:, cols]
                dyq = dyv[rows, cols]
                dz_ref[rows, cols] = dyq * s * dug_du[rows, cols]
                ds = dyq * ug[rows, cols]
                db_ref[:, cols] += ds
                dsb = ds.astype(BF16)
                dw_ref[h] += lax.dot_general(dsb, vn[rows], (((1,), (1,)), ((), ())), preferred_element_type=F32)
                dvn_parts.append(lax.dot_general(wm, dsb, (((0,), (0,)), ((), ())), preferred_element_type=F32))
            dvn = jnp.concatenate(dvn_parts, axis=0) if nq > 1 else dvn_parts[0]
            dgain_ref[:, cols] += jnp.sum(dvn * vhat, axis=0, keepdims=True)
            dvhat = dvn * gain
            mean = jnp.mean(dvhat * vhat, axis=-1, keepdims=True)
            dz_ref[:, w + h * GM_HEAD_DIM:w + (h + 1) * GM_HEAD_DIM] = r * (dvhat - vhat * mean) * dvg_dv[:, cols]

        @pl.when(i == steps - 1)
        def _():
            for h in range(heads):
                dw_ref[h] = jnp.where(mask, dw_ref[h], 0.0)

    dz, dw, db, dgain = pl.pallas_call(
        body, name=name, grid=(steps,),
        in_specs=[pl.BlockSpec((tm, w2), lambda i: (i, 0)), pl.BlockSpec((tm, w), lambda i: (i, 0)),
                  pl.BlockSpec((1, w), lambda i: (0, 0)),
                  pl.BlockSpec((heads, GM_CHUNK, GM_CHUNK), lambda i: (0, 0, 0)),
                  pl.BlockSpec((GM_CHUNK, w), lambda i: (0, 0))],
        out_specs=[pl.BlockSpec((tm, w2), lambda i: (i, 0)),
                   pl.BlockSpec((heads, GM_CHUNK, GM_CHUNK), lambda i: (0, 0, 0)),
                   pl.BlockSpec((GM_CHUNK, w), lambda i: (0, 0)),
                   pl.BlockSpec((1, w), lambda i: (0, 0))],
        out_shape=[jax.ShapeDtypeStruct((n, w2), F32), jax.ShapeDtypeStruct((heads, GM_CHUNK, GM_CHUNK), F32),
                   jax.ShapeDtypeStruct((GM_CHUNK, w), F32), jax.ShapeDtypeStruct((1, w), F32)],
        compiler_params=_params("arbitrary"),
    )(zgm, dy, v_gain.reshape(1, w), w_s, bias_tile)
    return dz, dw, db, dgain.reshape(w)


def _mix_out_fwd(y_ssm, y_gm, g1, g2, w_out, x, name):
    n, w = y_ssm.shape
    d = w_out.shape[1]
    tm = _tile(n, 512, 16)

    def body(a_ref, b_ref, g1_ref, g2_ref, w_ref, x_ref, ycat_ref, o_ref):
        for src, g_ref, lo in ((a_ref, g1_ref, 0), (b_ref, g2_ref, w)):
            v = src[...]
            ycat_ref[:, lo:lo + w] = (v * _rms_rows(v) * g_ref[...]).astype(ycat_ref.dtype)
        o_ref[...] = x_ref[...] + jnp.dot(ycat_ref[...], w_ref[...], preferred_element_type=F32)

    row = pl.BlockSpec((tm, w), lambda i: (i, 0))
    vec = pl.BlockSpec((1, w), lambda i: (0, 0))
    return pl.pallas_call(
        body, name=name, grid=(n // tm,),
        in_specs=[row, row, vec, vec, pl.BlockSpec((2 * w, d), lambda i: (0, 0)), pl.BlockSpec((tm, d), lambda i: (i, 0))],
        out_specs=[pl.BlockSpec((tm, 2 * w), lambda i: (i, 0)), pl.BlockSpec((tm, d), lambda i: (i, 0))],
        out_shape=[jax.ShapeDtypeStruct((n, 2 * w), BF16), jax.ShapeDtypeStruct((n, d), F32)],
        compiler_params=_params("parallel"),
    )(y_ssm, y_gm, g1.reshape(1, w), g2.reshape(1, w), w_out, x)


def _mix_out_bwd(dx, w_out, y_ssm, y_gm, g1, g2, name, comm=None):
    n, w = y_ssm.shape
    d = w_out.shape[1]
    tm = _tile(n, 512, 8)
    steps = n // tm

    def body(dx_ref, w_ref, a_ref, b_ref, g1_ref, g2_ref, da_ref, db_ref, dg1_ref, dg2_ref):
        i = pl.program_id(0)

        @pl.when(i == 0)
        def _():
            dg1_ref[...] = jnp.zeros_like(dg1_ref)
            dg2_ref[...] = jnp.zeros_like(dg2_ref)

        dycat = lax.dot_general(dx_ref[...].astype(BF16), w_ref[...], (((1,), (1,)), ((), ())),
                                preferred_element_type=F32)
        for src, g_ref, lo, dst, dg_ref in ((a_ref, g1_ref, 0, da_ref, dg1_ref), (b_ref, g2_ref, w, db_ref, dg2_ref)):
            v = src[...]
            dh = dycat[:, lo:lo + w]
            r = _rms_rows(v)
            vh = v * r
            dyg = dh * g_ref[...]
            mean = jnp.mean(dyg * vh, axis=-1, keepdims=True)
            dst[...] = r * (dyg - vh * mean)
            dg_ref[...] += jnp.sum(dh * vh, axis=0, keepdims=True)

    row = pl.BlockSpec((tm, w), lambda i: (i, 0))
    vec = pl.BlockSpec((1, w), lambda i: (0, 0))
    (da, db, dg1, dg2), comm_outs = _call(
        body, name=name, grid=(steps,),
        in_specs=[pl.BlockSpec((tm, d), lambda i: (i, 0)), pl.BlockSpec((2 * w, d), lambda i: (0, 0)), row, row, vec, vec],
        out_specs=[row, row, vec, vec],
        out_shape=[jax.ShapeDtypeStruct((n, w), F32), jax.ShapeDtypeStruct((n, w), F32),
                   jax.ShapeDtypeStruct((1, w), F32), jax.ShapeDtypeStruct((1, w), F32)],
        semantics=("arbitrary",), args=(dx, w_out, y_ssm, y_gm, g1.reshape(1, w), g2.reshape(1, w)), comm=comm)
    res = (da, db, dg1.reshape(w), dg2.reshape(w))
    return res if comm is None else (*res, comm_outs)


def _discretise(a_re, a_im, log_dt, bt_re, bt_im):
    dt = jnp.exp(log_dt)
    e = jnp.exp(a_re * dt)
    ang = a_im * dt
    lr = e * jnp.cos(ang)
    li = e * jnp.sin(ang)
    den = a_re * a_re + a_im * a_im
    cr = ((lr - 1.0) * a_re + li * a_im) / den
    ci = (li * a_re - (lr - 1.0) * a_im) / den
    cr3 = cr[:, None, :]
    ci3 = ci[:, None, :]
    return lr, li, cr3 * bt_re - ci3 * bt_im, cr3 * bt_im + ci3 * bt_re


def _disc_fwd(a_re, a_im, log_dt, bt_re, bt_im):
    g, p = a_re.shape
    c = bt_re.shape[1]

    def body(are_ref, aim_ref, ldt_ref, bre_ref, bim_ref, lr_ref, li_ref, bbr_ref, bbi_ref):
        lr, li, bbr, bbi = _discretise(are_ref[...], aim_ref[...], ldt_ref[...], bre_ref[...], bim_ref[...])
        lr_ref[...] = lr
        li_ref[...] = li
        bbr_ref[...] = bbr
        bbi_ref[...] = bbi

    return pl.pallas_call(
        body, name="s5_discretise",
        out_shape=[jax.ShapeDtypeStruct((g, p), F32), jax.ShapeDtypeStruct((g, p), F32),
                   jax.ShapeDtypeStruct((g, c, p), F32), jax.ShapeDtypeStruct((g, c, p), F32)],
    )(a_re, a_im, log_dt, bt_re, bt_im)


def _disc_bwd(a_re, a_im, log_dt, bt_re, bt_im, dlr, dli, dbbr, dbbi):
    g, p = a_re.shape
    c = bt_re.shape[1]

    def body(are_ref, aim_ref, ldt_ref, bre_ref, bim_ref, dlr_ref, dli_ref, dbbr_ref, dbbi_ref,
             dare_ref, daim_ref, dldt_ref, dbre_ref, dbim_ref):
        _, vjp = jax.vjp(_discretise, are_ref[...], aim_ref[...], ldt_ref[...], bre_ref[...], bim_ref[...])
        dare, daim, dldt, dbre, dbim = vjp((dlr_ref[...], dli_ref[...], dbbr_ref[...], dbbi_ref[...]))
        dare_ref[...] = dare
        daim_ref[...] = daim
        dldt_ref[...] = dldt
        dbre_ref[...] = dbre
        dbim_ref[...] = dbim

    return pl.pallas_call(
        body, name="s5_discretise_bwd",
        out_shape=[jax.ShapeDtypeStruct((g, p), F32), jax.ShapeDtypeStruct((g, p), F32),
                   jax.ShapeDtypeStruct((g, 1), F32),
                   jax.ShapeDtypeStruct((g, c, p), F32), jax.ShapeDtypeStruct((g, c, p), F32)],
    )(a_re, a_im, log_dt, bt_re, bt_im, dlr, dli, dbbr, dbbi)


def _block_diag(w, nb):
    g, a, b = w.shape
    gpb = g // nb
    eye = jnp.eye(gpb, dtype=w.dtype)
    w4 = w.reshape(nb, gpb, a, b)
    return jnp.einsum("ngab,gh->ngahb", w4, eye).reshape(nb, gpb * a, gpb * b)


def _block_diag_extract(m, gpb):
    nb, ga, gb = m.shape
    a, b = ga // gpb, gb // gpb
    m5 = m.reshape(nb, gpb, a, gpb, b)
    idx = jnp.arange(gpb)
    return m5[:, idx, :, idx, :].transpose(1, 0, 2, 3).reshape(nb * gpb, a, b)


def _ssm_operands(lr, li, bbr, bbi, c_re, c_im, d_skip, glu_w, glu_b):
    g = lr.shape[0]
    nb = g // GROUPS_PER_BLOCK
    s = STATES_PER_BLOCK
    lam = jnp.concatenate([lr.reshape(nb, 1, s), li.reshape(nb, 1, s)], axis=-1)
    b_bd = jnp.concatenate([_block_diag(bbr, nb), _block_diag(bbi, nb)], axis=-1)
    ct_re = jnp.swapaxes(c_re, 1, 2)
    ct_im = jnp.swapaxes(c_im, 1, 2)
    c_bd = jnp.concatenate([_block_diag(ct_re, nb), -_block_diag(ct_im, nb)], axis=1)
    dsk = d_skip.reshape(nb, 1, LANES)
    w_bd = jnp.concatenate([_block_diag(glu_w[:, :, :SSM_CH], nb), _block_diag(glu_w[:, :, SSM_CH:], nb)], axis=-1)
    bias = jnp.concatenate([glu_b[:, :SSM_CH].reshape(nb, 1, LANES), glu_b[:, SSM_CH:].reshape(nb, 1, LANES)], axis=-1)
    return lam, b_bd.astype(BF16), c_bd.astype(BF16), dsk, w_bd.astype(BF16), bias


def _roll_rows(v, shift):
    return v if shift % SUBLANES == 0 else pltpu.roll(v, shift % SUBLANES, 0)


def _scan_chunk_rows(seq, nseq):
    return _tile(seq, max(8 * SSM_TIME_CHUNK // nseq, 8), max(SUBLANES // nseq, 1) * 8)


def _ssm_fwd(u, ops, nseq, name, comm=None):
    lam, b_bd, c_bd, dsk, w_bd, bias = ops
    rows_total, w = u.shape
    seq = rows_total // nseq
    nb = w // LANES
    s = STATES_PER_BLOCK
    tc = _scan_chunk_rows(seq, nseq)
    nk = seq // tc
    rows = tc * nseq
    stages = SUBLANES // nseq

    def body(u_ref, lam_ref, b_ref, c_ref, d_ref, w_ref, bias_ref, y_ref, hb_ref, buf, st, rbuf):
        k = pl.program_id(1)

        @pl.when(k == 0)
        def _():
            st[...] = jnp.zeros_like(st)

        hb_ref[...] = st[...]
        for q in range(nseq):
            rbuf[pl.ds(q, tc, stride=nseq), :] = u_ref[q]
        u = rbuf[...]
        buf[...] = jnp.dot(u.astype(BF16), b_ref[0], preferred_element_type=F32)
        lr = jnp.broadcast_to(lam_ref[0, :, 0:s], (SUBLANES, s))
        li = jnp.broadcast_to(lam_ref[0, :, s:2 * s], (SUBLANES, s))
        row = lax.broadcasted_iota(jnp.int32, (SUBLANES, s), 0)

        def step(i, carry):
            pr, pi = carry
            r0 = pl.multiple_of(i * SUBLANES, SUBLANES)
            br = buf[pl.ds(r0, SUBLANES), 0:s]
            bi = buf[pl.ds(r0, SUBLANES), s:2 * s]
            outr = outi = None
            for j in range(stages):
                rr = _roll_rows(pr, nseq)
                ri = _roll_rows(pi, nseq)
                pr = lr * rr - li * ri + br
                pi = lr * ri + li * rr + bi
                outr = pr if j == 0 else jnp.where(row >= j * nseq, pr, outr)
                outi = pi if j == 0 else jnp.where(row >= j * nseq, pi, outi)
            buf[pl.ds(r0, SUBLANES), 0:s] = outr
            buf[pl.ds(r0, SUBLANES), s:2 * s] = outi
            return outr, outi

        hr, hi = lax.fori_loop(0, rows // SUBLANES, step, (st[:, 0:s], st[:, s:2 * s]), unroll=2)
        st[:, 0:s] = hr
        st[:, s:2 * s] = hi
        y = jnp.dot(buf[...].astype(BF16), c_ref[0], preferred_element_type=F32) + d_ref[0] * u
        z = jnp.dot(_gelu(y).astype(BF16), w_ref[0], preferred_element_type=F32) + bias_ref[0]
        rbuf[...] = z[:, 0:LANES] * _sigmoid(z[:, LANES:2 * LANES])
        for q in range(nseq):
            y_ref[q] = rbuf[pl.ds(q, tc, stride=nseq), :]

    blk = lambda shape: pl.BlockSpec(shape, lambda b, k: (b, 0, 0))
    tok = pl.BlockSpec((nseq, tc, LANES), lambda b, k: (0, k, b))
    (y, hb), comm_outs = _call(
        body, name=name, grid=(nb, nk),
        in_specs=[tok, blk((1, 1, 2 * s)), blk((1, LANES, 2 * s)), blk((1, 2 * s, LANES)),
                  blk((1, 1, LANES)), blk((1, LANES, 2 * LANES)), blk((1, 1, 2 * LANES))],
        out_specs=[tok, pl.BlockSpec((SUBLANES, 2 * s), lambda b, k: (k, b))],
        out_shape=[jax.ShapeDtypeStruct((nseq, seq, w), F32),
                   jax.ShapeDtypeStruct((nk * SUBLANES, nb * 2 * s), F32)],
        scratch_shapes=[pltpu.VMEM((rows, 2 * s), F32), pltpu.VMEM((SUBLANES, 2 * s), F32),
                        pltpu.VMEM((rows, LANES), F32)],
        semantics=("parallel", "arbitrary"),
        args=(u.reshape(nseq, seq, w), lam, b_bd, c_bd, dsk, w_bd, bias), comm=comm)
    y = y.reshape(nseq * seq, w)
    return (y, hb) if comm is None else (y, hb, comm_outs)


def _ssm_bwd(u, dout, hb, ops, nseq, name, comm=None):
    lam, b_bd, c_bd, dsk, w_bd, bias = ops
    rows_total, w = u.shape
    seq = rows_total // nseq
    nb = w // LANES
    s = STATES_PER_BLOCK
    tc = _scan_chunk_rows(seq, nseq)
    nk = seq // tc
    rows = tc * nseq
    nblk = rows // SUBLANES
    stages = SUBLANES // nseq
    tn_dims = (((0,), (0,)), ((), ()))
    nt_dims = (((1,), (1,)), ((), ()))

    def body(u_ref, dy_ref, hb_ref, lam_ref, b_ref, c_ref, d_ref, w_ref, bias_ref,
             du_ref, dlam_ref, db_ref, dct_ref, dd_ref, dw_ref, dbias_ref, hbuf, gbuf, gst, lacc, rbuf, rbuf2):
        k = pl.program_id(1)

        @pl.when(k == 0)
        def _():
            gst[...] = jnp.zeros_like(gst)
            lacc[...] = jnp.zeros_like(lacc)
            db_ref[...] = jnp.zeros_like(db_ref)
            dct_ref[...] = jnp.zeros_like(dct_ref)
            dd_ref[...] = jnp.zeros_like(dd_ref)
            dw_ref[...] = jnp.zeros_like(dw_ref)
            dbias_ref[...] = jnp.zeros_like(dbias_ref)

        for q in range(nseq):
            rbuf[pl.ds(q, tc, stride=nseq), :] = u_ref[q]
            rbuf2[pl.ds(q, tc, stride=nseq), :] = dy_ref[q]
        u = rbuf[...]
        ub = u.astype(BF16)
        lr = jnp.broadcast_to(lam_ref[0, :, 0:s], (SUBLANES, s))
        li = jnp.broadcast_to(lam_ref[0, :, s:2 * s], (SUBLANES, s))
        row = lax.broadcasted_iota(jnp.int32, (SUBLANES, s), 0)
        hbuf[...] = jnp.dot(ub, b_ref[0], preferred_element_type=F32)

        def fstep(i, carry):
            pr, pi = carry
            r0 = pl.multiple_of(i * SUBLANES, SUBLANES)
            br = hbuf[pl.ds(r0, SUBLANES), 0:s]
            bi = hbuf[pl.ds(r0, SUBLANES), s:2 * s]
            outr = outi = None
            for j in range(stages):
                rr = _roll_rows(pr, nseq)
                ri = _roll_rows(pi, nseq)
                pr = lr * rr - li * ri + br
                pi = lr * ri + li * rr + bi
                outr = pr if j == 0 else jnp.where(row >= j * nseq, pr, outr)
                outi = pi if j == 0 else jnp.where(row >= j * nseq, pi, outi)
            hbuf[pl.ds(r0, SUBLANES), 0:s] = outr
            hbuf[pl.ds(r0, SUBLANES), s:2 * s] = outi
            return outr, outi

        lax.fori_loop(0, nblk, fstep, (hb_ref[:, 0:s], hb_ref[:, s:2 * s]), unroll=2)
        hb16 = hbuf[...].astype(BF16)
        y = jnp.dot(hb16, c_ref[0], preferred_element_type=F32) + d_ref[0] * u
        yg, dyg_dy = _gelu_and_grad(y)
        yg16 = yg.astype(BF16)
        z = jnp.dot(yg16, w_ref[0], preferred_element_type=F32) + bias_ref[0]
        z1 = z[:, 0:LANES]
        sg = _sigmoid(z[:, LANES:2 * LANES])
        dout = rbuf2[...]
        dz = jnp.concatenate([dout * sg, dout * z1 * sg * (1.0 - sg)], axis=-1)
        dz16 = dz.astype(BF16)
        dw_ref[0] += lax.dot_general(yg16, dz16, tn_dims, preferred_element_type=F32)
        dbias_ref[0] += jnp.sum(dz, axis=0, keepdims=True)
        dy = lax.dot_general(dz16, w_ref[0], nt_dims, preferred_element_type=F32) * dyg_dy
        dy16 = dy.astype(BF16)
        dd_ref[0] += jnp.sum(dy * u, axis=0, keepdims=True)
        dct_ref[0] += lax.dot_general(dy16, hb16, tn_dims, preferred_element_type=F32)
        gbuf[...] = lax.dot_general(dy16, c_ref[0], nt_dims, preferred_element_type=F32)

        def bstep(i, carry):
            pr, pi, ar, ai = carry
            blk = nblk - 1 - i
            r0 = pl.multiple_of(blk * SUBLANES, SUBLANES)
            dr = gbuf[pl.ds(r0, SUBLANES), 0:s]
            di = gbuf[pl.ds(r0, SUBLANES), s:2 * s]
            outr = outi = None
            for j in reversed(range(stages)):
                rr = _roll_rows(pr, SUBLANES - nseq)
                ri = _roll_rows(pi, SUBLANES - nseq)
                pr = dr + lr * rr + li * ri
                pi = di - li * rr + lr * ri
                outr = pr if j == stages - 1 else jnp.where(row < (j + 1) * nseq, pr, outr)
                outi = pi if j == stages - 1 else jnp.where(row < (j + 1) * nseq, pi, outi)
            gbuf[pl.ds(r0, SUBLANES), 0:s] = outr
            gbuf[pl.ds(r0, SUBLANES), s:2 * s] = outi
            p0 = pl.multiple_of(jnp.maximum(blk - 1, 0) * SUBLANES, SUBLANES)
            first = blk == 0
            before_r = jnp.where(first, hb_ref[:, 0:s], hbuf[pl.ds(p0, SUBLANES), 0:s])
            before_i = jnp.where(first, hb_ref[:, s:2 * s], hbuf[pl.ds(p0, SUBLANES), s:2 * s])
            if stages > 1:
                last_rows = row >= SUBLANES - nseq
                before_r = _roll_rows(jnp.where(last_rows, before_r, hbuf[pl.ds(r0, SUBLANES), 0:s]), nseq)
                before_i = _roll_rows(jnp.where(last_rows, before_i, hbuf[pl.ds(r0, SUBLANES), s:2 * s]), nseq)
            return (outr, outi, ar + outr * before_r + outi * before_i, ai - outr * before_i + outi * before_r)

        gr, gi, ar, ai = lax.fori_loop(
            0, nblk, bstep, (gst[:, 0:s], gst[:, s:2 * s], lacc[:, 0:s], lacc[:, s:2 * s]))
        gst[:, 0:s] = gr
        gst[:, s:2 * s] = gi
        lacc[:, 0:s] = ar
        lacc[:, s:2 * s] = ai
        g16 = gbuf[...].astype(BF16)
        rbuf[...] = dy * d_ref[0] + lax.dot_general(g16, b_ref[0], nt_dims, preferred_element_type=F32)
        for q in range(nseq):
            du_ref[q] = rbuf[pl.ds(q, tc, stride=nseq), :]
        db_ref[0] += lax.dot_general(ub, g16, tn_dims, preferred_element_type=F32)

        @pl.when(k == nk - 1)
        def _():
            dlam_ref[0] = jnp.sum(lacc[...], axis=0, keepdims=True)

    blk = lambda shape: pl.BlockSpec(shape, lambda b, k: (b, 0, 0))
    rev = lambda b, k: (nk - 1 - k, b)
    tok = pl.BlockSpec((nseq, tc, LANES), lambda b, k: (0, nk - 1 - k, b))
    outs, comm_outs = _call(
        body, name=name, grid=(nb, nk),
        in_specs=[tok, tok, pl.BlockSpec((SUBLANES, 2 * s), rev),
                  blk((1, 1, 2 * s)), blk((1, LANES, 2 * s)), blk((1, 2 * s, LANES)),
                  blk((1, 1, LANES)), blk((1, LANES, 2 * LANES)), blk((1, 1, 2 * LANES))],
        out_specs=[tok, blk((1, 1, 2 * s)), blk((1, LANES, 2 * s)), blk((1, LANES, 2 * s)),
                   blk((1, 1, LANES)), blk((1, LANES, 2 * LANES)), blk((1, 1, 2 * LANES))],
        out_shape=[jax.ShapeDtypeStruct((nseq, seq, w), F32),
                   jax.ShapeDtypeStruct((nb, 1, 2 * s), F32), jax.ShapeDtypeStruct((nb, LANES, 2 * s), F32),
                   jax.ShapeDtypeStruct((nb, LANES, 2 * s), F32), jax.ShapeDtypeStruct((nb, 1, LANES), F32),
                   jax.ShapeDtypeStruct((nb, LANES, 2 * LANES), F32), jax.ShapeDtypeStruct((nb, 1, 2 * LANES), F32)],
        scratch_shapes=[pltpu.VMEM((rows, 2 * s), F32), pltpu.VMEM((rows, 2 * s), F32),
                        pltpu.VMEM((SUBLANES, 2 * s), F32), pltpu.VMEM((SUBLANES, 2 * s), F32),
                        pltpu.VMEM((rows, LANES), F32), pltpu.VMEM((rows, LANES), F32)],
        semantics=("parallel", "arbitrary"),
        args=(u.reshape(nseq, seq, w), dout.reshape(nseq, seq, w), hb, lam, b_bd, c_bd, dsk, w_bd, bias), comm=comm)
    outs[0] = outs[0].reshape(nseq * seq, w)
    return outs if comm is None else (outs, comm_outs)


ANY = pl.BlockSpec(memory_space=pl.ANY)

BIG = (("ffn1_w_in", True), ("ffn1_w_out", False), ("mix_w_in", True), ("mix_w_out", False),
       ("ffn2_w_in", True), ("ffn2_w_out", False))


def _my_place():
    return lax.axis_index("x"), lax.axis_index("y"), lax.axis_index("c")


def _other_chips(x, y):
    return [(1 - x, y), (x, 1 - y), (1 - x, 1 - y)]


def _half_of_shard(ref, col_sharded, chip, core):
    full_rows, full_cols = ref.shape
    if col_sharded:
        hr, cs = full_rows // 2, full_cols // N_CHIPS
        return ref.at[pl.ds(pl.multiple_of(core * hr, 8), hr), pl.ds(chip * cs, cs)]
    rs = full_rows // N_CHIPS
    return ref.at[pl.ds(pl.multiple_of(chip * rs + core * (rs // 2), 8), rs // 2), :]


def _gather_comm(shards, cols):
    full_shapes = [(sh.shape[0], sh.shape[1] * N_CHIPS) if col else (sh.shape[0] * N_CHIPS, sh.shape[1])
                   for sh, col in zip(shards, cols)]
    nw = len(shards)

    def first_copies(ins, outs, sems):
        send_sems, recv_sems, local_sems = sems
        x, y, c = _my_place()
        me = 2 * x + y
        locals_, sends = [], []
        for wi in range(nw):
            src, dst = ins[wi], outs[wi]
            rs, cs = src.shape
            hs = rs // 2
            if cols[wi]:
                place = dst.at[:, pl.ds(me * cs, cs)]
            else:
                place = dst.at[pl.ds(pl.multiple_of(me * rs, 8), rs), :]
            locals_.append(pltpu.make_async_copy(src, place, local_sems.at[wi]))
            my_half = src.at[pl.ds(pl.multiple_of(c * hs, 8), hs), :]
            for j, (px, py) in enumerate(_other_chips(x, y)):
                sends.append(pltpu.make_async_remote_copy(
                    src_ref=my_half, dst_ref=_half_of_shard(dst, cols[wi], me, c),
                    send_sem=send_sems.at[wi * 6 + j], recv_sem=recv_sems.at[wi * 6 + j],
                    device_id=(px, py, c), device_id_type=MESH))
        return locals_, sends

    def start(ins, outs, sems):
        locals_, sends = first_copies(ins, outs, sems)
        for cp in locals_ + sends:
            cp.start()

    def forwards(outs, sems, wait_landed):
        send_sems, recv_sems, _ = sems
        x, y, c = _my_place()
        out = []
        for wi in range(nw):
            dst = outs[wi]
            for j, (px, py) in enumerate(_other_chips(x, y)):
                got = _half_of_shard(dst, cols[wi], 2 * px + py, c)
                if wait_landed:
                    pltpu.make_async_remote_copy(
                        src_ref=got, dst_ref=got, send_sem=send_sems.at[wi * 6 + j], recv_sem=recv_sems.at[wi * 6 + j],
                        device_id=(px, py, c), device_id_type=MESH).wait_recv()
                out.append(pltpu.make_async_remote_copy(
                    src_ref=got, dst_ref=got, send_sem=send_sems.at[wi * 6 + 3 + j], recv_sem=recv_sems.at[wi * 6 + 3 + j],
                    device_id=(x, y, 1 - c), device_id_type=MESH))
                if wait_landed:
                    out[-1].start()
        return out

    def middle(ins, outs, sems):
        forwards(outs, sems, True)

    def finish(ins, outs, sems):
        send_sems, recv_sems, _ = sems
        x, y, c = _my_place()
        locals_, sends = first_copies(ins, outs, sems)
        for wi in range(nw):
            dst = outs[wi]
            for j, (px, py) in enumerate(_other_chips(x, y)):
                theirs = _half_of_shard(dst, cols[wi], 2 * px + py, 1 - c)
                pltpu.make_async_remote_copy(
                    src_ref=theirs, dst_ref=theirs, send_sem=send_sems.at[wi * 6 + 3 + j],
                    recv_sem=recv_sems.at[wi * 6 + 3 + j], device_id=(x, y, 1 - c), device_id_type=MESH).wait_recv()
        for cp in sends + forwards(outs, sems, False):
            cp.wait_send()
        for cp in locals_:
            cp.wait()

    return _Comm(shards, [jax.ShapeDtypeStruct(s, BF16) for s in full_shapes],
                 [pltpu.SemaphoreType.DMA((6 * nw,)), pltpu.SemaphoreType.DMA((6 * nw,)),
                  pltpu.SemaphoreType.DMA((nw,))], start, finish, middle=middle)


def _pair_exchange_comm(grads, cols):
    nw = len(grads)
    n_copies = sum(1 if col else N_CHIPS for col in cols)

    def copies(ins, outs, sems):
        send_sems, recv_sems = sems
        x, y, c = _my_place()
        out = []
        for wi in range(nw):
            src, dst = ins[wi], outs[wi]
            fr = src.shape[0]
            if cols[wi]:
                hr = fr // 2
                pieces = [(src.at[pl.ds(pl.multiple_of((1 - c) * hr, 8), hr), :], dst)]
            else:
                rs = fr // N_CHIPS
                hs = rs // 2
                pieces = [(src.at[pl.ds(pl.multiple_of(k * rs + (1 - c) * hs, 8), hs), :],
                           dst.at[pl.ds(k * hs, hs), :]) for k in range(N_CHIPS)]
            for s_ref, d_ref in pieces:
                out.append(pltpu.make_async_remote_copy(
                    src_ref=s_ref, dst_ref=d_ref, send_sem=send_sems.at[len(out)], recv_sem=recv_sems.at[len(out)],
                    device_id=(x, y, 1 - c), device_id_type=MESH))
        return out

    def start(ins, outs, sems):
        for cp in copies(ins, outs, sems):
            cp.start()

    def finish(ins, outs, sems):
        for cp in copies(ins, outs, sems):
            cp.wait()

    return _Comm(grads, [jax.ShapeDtypeStruct((g.shape[0] // 2, g.shape[1]), F32) for g in grads],
                 [pltpu.SemaphoreType.DMA((n_copies,)), pltpu.SemaphoreType.DMA((n_copies,))], start, finish)


def _pair_sum(grad, other, col, core, name):
    fr, fc = grad.shape
    pieces = 1 if col else N_CHIPS
    pr = fr // 2 // pieces
    gview = grad.reshape(pieces * 2, pr, fc)
    oview = other.reshape(pieces, pr, fc)
    tr = _tile(pr, 256, 16)

    def body(c_ref, g_ref, o_ref, out_ref):
        out_ref[...] = (g_ref[...] + o_ref[...]).astype(out_ref.dtype)

    out = pl.pallas_call(
        body, name=name,
        grid_spec=pltpu.PrefetchScalarGridSpec(
            num_scalar_prefetch=1, grid=(pieces, pr // tr),
            in_specs=[pl.BlockSpec((1, tr, fc), lambda p, i, cref: (p * 2 + cref[0], i, 0)),
                      pl.BlockSpec((1, tr, fc), lambda p, i, cref: (p, i, 0))],
            out_specs=pl.BlockSpec((1, tr, fc), lambda p, i, cref: (p, i, 0))),
        out_shape=jax.ShapeDtypeStruct((pieces, pr, fc), BF16),
        compiler_params=_params("parallel", "parallel"),
    )(core, gview, oview)
    return out.reshape(fr // 2, fc)


def _chip_exchange_comm(psums, cols):
    nw = len(psums)
    out_shapes = [(N_CHIPS, p.shape[0], p.shape[1] // N_CHIPS) if col else (N_CHIPS, p.shape[0] // N_CHIPS, p.shape[1])
                  for p, col in zip(psums, cols)]

    def copies(ins, outs, sems):
        send_sems, recv_sems, local_sems = sems
        x, y, c = _my_place()
        me = 2 * x + y
        out = []
        for wi in range(nw):
            src = ins[wi]
            mine = outs[wi].at[me]

            def piece(chip, src=src, col=cols[wi]):
                if col:
                    cs = src.shape[1] // N_CHIPS
                    return src.at[:, pl.ds(chip * cs, cs)]
                ps = src.shape[0] // N_CHIPS
                return src.at[pl.ds(pl.multiple_of(chip * ps, 8), ps), :]

            out.append(pltpu.make_async_copy(piece(me), mine, local_sems.at[wi]))
            for j, (px, py) in enumerate(_other_chips(x, y)):
                out.append(pltpu.make_async_remote_copy(
                    src_ref=piece(2 * px + py), dst_ref=mine,
                    send_sem=send_sems.at[wi * 3 + j], recv_sem=recv_sems.at[wi * 3 + j],
                    device_id=(px, py, c), device_id_type=MESH))
        return out

    def start(ins, outs, sems):
        for cp in copies(ins, outs, sems):
            cp.start()

    def finish(ins, outs, sems):
        for cp in copies(ins, outs, sems):
            cp.wait()

    return _Comm(psums, [jax.ShapeDtypeStruct(s, BF16) for s in out_shapes],
                 [pltpu.SemaphoreType.DMA((3 * nw,)), pltpu.SemaphoreType.DMA((3 * nw,)),
                  pltpu.SemaphoreType.DMA((nw,))], start, finish)


def _chip_sum(slots, core, layer, layers, into, name):
    _, hr, cs = slots.shape
    tr = _tile(hr, 256, 16)

    def body(c_ref, s_ref, *rest):
        out_ref = rest[-1]
        acc = s_ref[0].astype(F32)
        for i in range(1, N_CHIPS):
            acc = acc + s_ref[i].astype(F32)
        out_ref[0] = acc

    in_specs = [pl.BlockSpec((N_CHIPS, tr, cs), lambda i, cref: (0, i, 0))]
    args = [core, slots]
    aliases = {}
    if into is not None:
        in_specs.append(pl.BlockSpec(memory_space=pl.ANY))
        args.append(into.reshape(layers * 2, hr, cs))
        aliases = {2: 0}
    out = pl.pallas_call(
        body, name=name,
        grid_spec=pltpu.PrefetchScalarGridSpec(
            num_scalar_prefetch=1, grid=(hr // tr,), in_specs=in_specs,
            out_specs=pl.BlockSpec((1, tr, cs), lambda i, cref: (layer * 2 + cref[0], i, 0))),
        out_shape=jax.ShapeDtypeStruct((layers * 2, hr, cs), F32),
        input_output_aliases=aliases,
        compiler_params=_params("parallel"),
    )(*args)
    return out.reshape(layers, 2 * hr, cs)


def _pair_share_comm(reduced):
    nw = len(reduced)

    def copies(ins, outs, sems):
        send_sems, recv_sems = sems
        x, y, c = _my_place()
        out = []
        for wi in range(nw):
            hs = outs[wi].shape[1] // 2
            mine = outs[wi].at[:, pl.ds(pl.multiple_of(c * hs, 8), hs), :]
            out.append(pltpu.make_async_remote_copy(
                src_ref=mine, dst_ref=mine, send_sem=send_sems.at[wi], recv_sem=recv_sems.at[wi],
                device_id=(x, y, 1 - c), device_id_type=MESH))
        return out

    def start(ins, outs, sems):
        for cp in copies(ins, outs, sems):
            cp.start()

    def finish(ins, outs, sems):
        for cp in copies(ins, outs, sems):
            cp.wait()

    return _Comm(reduced, [jax.ShapeDtypeStruct(r.shape, F32) for r in reduced],
                 [pltpu.SemaphoreType.DMA((nw,)), pltpu.SemaphoreType.DMA((nw,))], start, finish,
                 alias={i: i for i in range(nw)})


def _all_reduce_small(flat, comm):
    rows, lanes = flat.shape
    seg = rows // N_DEV
    c_in, c_out = len(comm.ins), len(comm.outs)

    def body(*refs):
        refs = list(refs)
        in_ref, cins = refs[0], refs[1:1 + c_in]
        out_ref, couts = refs[1 + c_in], refs[2 + c_in:2 + c_in + c_out]
        recv_ref, send_sems, recv_sems = refs[2 + c_in + c_out:5 + c_in + c_out]
        csems = refs[5 + c_in + c_out:]
        comm.start(cins, couts, csems)
        x, y, c = _my_place()
        me = 4 * x + 2 * y + c

        def peer(r):
            fx, fy, fc = (r >> 2) & 1, (r >> 1) & 1, r & 1
            px = jnp.where(fx == 1, 1 - x, x)
            py = jnp.where(fy == 1, 1 - y, y)
            pc = jnp.where(fc == 1, 1 - c, c)
            return px, py, pc

        first = []
        for r in range(1, N_DEV):
            px, py, pc = peer(r)
            theirs = in_ref.at[pl.ds(pl.multiple_of((4 * px + 2 * py + pc) * seg, 8), seg), :]
            cp = pltpu.make_async_remote_copy(
                src_ref=theirs, dst_ref=recv_ref.at[r], send_sem=send_sems.at[r - 1], recv_sem=recv_sems.at[r - 1],
                device_id=(px, py, pc), device_id_type=MESH)
            cp.start()
            first.append(cp)
        for cp in first:
            cp.wait()
        my_rows = pl.ds(pl.multiple_of(me * seg, 8), seg)
        acc = in_ref[my_rows, :]
        for r in range(1, N_DEV):
            acc = acc + recv_ref[r]
        out_ref[my_rows, :] = acc
        second = []
        for r in range(1, N_DEV):
            px, py, pc = peer(r)
            cp = pltpu.make_async_remote_copy(
                src_ref=out_ref.at[my_rows, :], dst_ref=out_ref.at[my_rows, :],
                send_sem=send_sems.at[6 + r], recv_sem=recv_sems.at[6 + r],
                device_id=(px, py, pc), device_id_type=MESH)
            cp.start()
            second.append(cp)
        for r in range(1, N_DEV):
            px, py, pc = peer(r)
            theirs = out_ref.at[pl.ds(pl.multiple_of((4 * px + 2 * py + pc) * seg, 8), seg), :]
            pltpu.make_async_remote_copy(
                src_ref=theirs, dst_ref=theirs, send_sem=send_sems.at[6 + r], recv_sem=recv_sems.at[6 + r],
                device_id=(px, py, pc), device_id_type=MESH).wait_recv()
        for cp in second:
            cp.wait_send()
        comm.finish(cins, couts, csems)

    vm = pl.BlockSpec(memory_space=pltpu.VMEM)
    any_spec = pl.BlockSpec(memory_space=pl.ANY)
    outs = pl.pallas_call(
        body, name="all_reduce_small",
        in_specs=[vm] + [any_spec] * c_in, out_specs=[vm] + [any_spec] * c_out,
        out_shape=[jax.ShapeDtypeStruct((rows, lanes), F32)] + comm.outs,
        scratch_shapes=[pltpu.VMEM((N_DEV, seg, lanes), F32),
                        pltpu.SemaphoreType.DMA((2 * (N_DEV - 1),)), pltpu.SemaphoreType.DMA((2 * (N_DEV - 1),))]
        + comm.sems,
        input_output_aliases={1 + ci: 1 + co for ci, co in comm.alias.items()},
        compiler_params=pltpu.CompilerParams(vmem_limit_bytes=VMEM_LIMIT),
    )(flat, *comm.ins)
    return outs[0], list(outs[1:])


def _adamw_update(w_ref, g_ref, m_ref, v_ref, d_ref, nm_ref, nv_ref):
    c1 = 1.0 - ADAM_B1 ** ADAM_STEP
    c2 = 1.0 - ADAM_B2 ** ADAM_STEP
    gv = g_ref[...]
    nm = ADAM_B1 * m_ref[...] + (1.0 - ADAM_B1) * gv
    nv = ADAM_B2 * v_ref[...] + (1.0 - ADAM_B2) * (gv * gv)
    d_ref[...] = -ADAM_LR * ((nm / c1) / (jnp.sqrt(nv / c2) + ADAM_EPS) + ADAM_WD * w_ref[...])
    nm_ref[...] = nm
    nv_ref[...] = nv


def _adamw_many(ws, gs, ms, vs, name):
    n = len(ws)

    def body(*refs):
        for i in range(n):
            _adamw_update(*[refs[k * n + i] for k in range(7)])

    shapes = [jax.ShapeDtypeStruct(w.shape, F32) for w in ws]
    outs = pl.pallas_call(
        body, name=name, out_shape=shapes * 3,
        compiler_params=pltpu.CompilerParams(vmem_limit_bytes=VMEM_LIMIT),
    )(*ws, *gs, *ms, *vs)
    return outs[:n], outs[n:2 * n], outs[2 * n:]


def _adamw(w, g, m, v, name):
    rows, cols = w.shape
    tr = _tile(rows, 256, 8)

    def body(w_ref, g_ref, m_ref, v_ref, go_ref, d_ref, nm_ref, nv_ref):
        go_ref[...] = g_ref[...]
        _adamw_update(w_ref, g_ref, m_ref, v_ref, d_ref, nm_ref, nv_ref)

    blk = pl.BlockSpec((tr, cols), lambda i: (i, 0))
    sds = jax.ShapeDtypeStruct((rows, cols), F32)
    return pl.pallas_call(
        body, name=name, grid=(rows // tr,),
        in_specs=[blk] * 4, out_specs=[blk] * 4, out_shape=[sds] * 4,
        compiler_params=_params("parallel"),
    )(w, g, m, v)


SMALL = ("norm_ffn1", "norm_mix", "ssm_a_re", "ssm_a_im", "ssm_log_dt", "ssm_b_re", "ssm_b_im", "ssm_c_re",
         "ssm_c_im", "ssm_d", "ssm_glu_w", "ssm_glu_b", "gm_v_gain", "gm_w_s", "gm_b_s", "gain_ssm_out",
         "gain_gm_out", "norm_ffn2", "norm_final")
WEIGHTS = ("norm_ffn1", "ffn1_w_in", "ffn1_w_out", "norm_mix", "mix_w_in", "ssm_a_re", "ssm_a_im", "ssm_log_dt",
           "ssm_b_re", "ssm_b_im", "ssm_c_re", "ssm_c_im", "ssm_d", "ssm_glu_w", "ssm_glu_b", "gm_v_gain", "gm_w_s",
           "gm_b_s", "gain_ssm_out", "gain_gm_out", "mix_w_out", "norm_ffn2", "ffn2_w_in", "ffn2_w_out", "norm_final")


def _ffn_fwd(x, gain, w_in, w_out, tag, hosted=None):
    if hosted is None:
        h, t, q, a = _ffn_in_fwd(x, gain, w_in, f"{tag}_in")
    else:
        (h, t, q, a), got = _ffn_in_fwd(x, gain, w_in, f"{tag}_in_hosting", comm=hosted[0]())
        hosted[1](got)
    if callable(w_out):
        w_out = w_out()
    out = _matmul(a, w_out, "nn", scale=0.5, res=x, tm=512, tn=1024, tk=4096, name=f"{tag}_out")
    return out, (x, h, t, q, a)


def _ffn_bwd(dout, saved, gain, w_in, w_out, tag, hooks=None, publish=None):
    x, h, t, q, a = saved
    f = t.shape[1]
    hooks = hooks or {}

    def hosted(key, fn, *args, name, **kw):
        if key not in hooks:
            return fn(*args, name=name, **kw)
        make, take = hooks[key]
        *res, got = fn(*args, name=f"{name}_hosting", comm=make(), **kw)
        take(got)
        return res[0] if len(res) == 1 else tuple(res)

    dg, du = hosted("out_dx", _ffn_out_bwd, dout, w_out, t, q, name=f"{tag}_out_dx")
    dw_out = hosted("out_dw", _matmul, a, dout, "tn", scale=0.5, tm=1536, tn=1024, tk=2048, name=f"{tag}_out_dw")
    if publish is not None:
        publish("out", dw_out)
    dw_in = hosted("in_dw_g", _matmul, h, dg, "tn", tm=1024, tn=1536, tk=2048, name=f"{tag}_in_dw_g",
                   out_cols=2 * f)
    dw_in = hosted("in_dw_u", _matmul, h, du, "tn", tm=1024, tn=1536, tk=2048, name=f"{tag}_in_dw_u",
                   out_cols=2 * f, col_off=f, into=dw_in)
    if publish is not None:
        publish("in", dw_in)
    dx, dgain = hosted("in_dx", _proj_in_bwd, [(dg, 0), (du, f)], w_in, x, gain, dout, name=f"{tag}_in_dx")
    return dx, dgain, dw_in, dw_out


def kernel(x, norm_ffn1, ffn1_w_in, ffn1_w_out, norm_mix, mix_w_in, ssm_a_re, ssm_a_im, ssm_log_dt, ssm_b_re, ssm_b_im, ssm_c_re, ssm_c_im, ssm_d, ssm_glu_w, ssm_glu_b, gm_v_gain, gm_w_s, gm_b_s, gain_ssm_out, gain_gm_out, mix_w_out, norm_ffn2, ffn2_w_in, ffn2_w_out, norm_final, loss_target, m_norm_ffn1, m_ffn1_w_in, m_ffn1_w_out, m_norm_mix, m_mix_w_in, m_ssm_a_re, m_ssm_a_im, m_ssm_log_dt, m_ssm_b_re, m_ssm_b_im, m_ssm_c_re, m_ssm_c_im, m_ssm_d, m_ssm_glu_w, m_ssm_glu_b, m_gm_v_gain, m_gm_w_s, m_gm_b_s, m_gain_ssm_out, m_gain_gm_out, m_mix_w_out, m_norm_ffn2, m_ffn2_w_in, m_ffn2_w_out, m_norm_final, v_norm_ffn1, v_ffn1_w_in, v_ffn1_w_out, v_norm_mix, v_mix_w_in, v_ssm_a_re, v_ssm_a_im, v_ssm_log_dt, v_ssm_b_re, v_ssm_b_im, v_ssm_c_re, v_ssm_c_im, v_ssm_d, v_ssm_glu_w, v_ssm_glu_b, v_gm_v_gain, v_gm_w_s, v_gm_b_s, v_gain_ssm_out, v_gain_gm_out, v_mix_w_out, v_norm_ffn2, v_ffn2_w_in, v_ffn2_w_out, v_norm_final):
    wts = dict(norm_ffn1=norm_ffn1, ffn1_w_in=ffn1_w_in, ffn1_w_out=ffn1_w_out, norm_mix=norm_mix, mix_w_in=mix_w_in,
               ssm_a_re=ssm_a_re, ssm_a_im=ssm_a_im, ssm_log_dt=ssm_log_dt, ssm_b_re=ssm_b_re, ssm_b_im=ssm_b_im,
               ssm_c_re=ssm_c_re, ssm_c_im=ssm_c_im, ssm_d=ssm_d, ssm_glu_w=ssm_glu_w, ssm_glu_b=ssm_glu_b,
               gm_v_gain=gm_v_gain, gm_w_s=gm_w_s, gm_b_s=gm_b_s, gain_ssm_out=gain_ssm_out, gain_gm_out=gain_gm_out,
               mix_w_out=mix_w_out, norm_ffn2=norm_ffn2, ffn2_w_in=ffn2_w_in, ffn2_w_out=ffn2_w_out,
               norm_final=norm_final)
    mom = dict(norm_ffn1=m_norm_ffn1, ffn1_w_in=m_ffn1_w_in, ffn1_w_out=m_ffn1_w_out, norm_mix=m_norm_mix,
               mix_w_in=m_mix_w_in, ssm_a_re=m_ssm_a_re, ssm_a_im=m_ssm_a_im, ssm_log_dt=m_ssm_log_dt,
               ssm_b_re=m_ssm_b_re, ssm_b_im=m_ssm_b_im, ssm_c_re=m_ssm_c_re, ssm_c_im=m_ssm_c_im, ssm_d=m_ssm_d,
               ssm_glu_w=m_ssm_glu_w, ssm_glu_b=m_ssm_glu_b, gm_v_gain=m_gm_v_gain, gm_w_s=m_gm_w_s, gm_b_s=m_gm_b_s,
               gain_ssm_out=m_gain_ssm_out, gain_gm_out=m_gain_gm_out, mix_w_out=m_mix_w_out, norm_ffn2=m_norm_ffn2,
               ffn2_w_in=m_ffn2_w_in, ffn2_w_out=m_ffn2_w_out, norm_final=m_norm_final)
    var = dict(norm_ffn1=v_norm_ffn1, ffn1_w_in=v_ffn1_w_in, ffn1_w_out=v_ffn1_w_out, norm_mix=v_norm_mix,
               mix_w_in=v_mix_w_in, ssm_a_re=v_ssm_a_re, ssm_a_im=v_ssm_a_im, ssm_log_dt=v_ssm_log_dt,
               ssm_b_re=v_ssm_b_re, ssm_b_im=v_ssm_b_im, ssm_c_re=v_ssm_c_re, ssm_c_im=v_ssm_c_im, ssm_d=v_ssm_d,
               ssm_glu_w=v_ssm_glu_w, ssm_glu_b=v_ssm_glu_b, gm_v_gain=v_gm_v_gain, gm_w_s=v_gm_w_s, gm_b_s=v_gm_b_s,
               gain_ssm_out=v_gain_ssm_out, gain_gm_out=v_gain_gm_out, mix_w_out=v_mix_w_out, norm_ffn2=v_norm_ffn2,
               ffn2_w_in=v_ffn2_w_in, ffn2_w_out=v_ffn2_w_out, norm_final=v_norm_final)

    nseq, seq, d = x.shape
    n = nseq * seq
    depth = norm_ffn1.shape[0]
    width = gain_ssm_out.shape[1]
    groups = ssm_a_re.shape[1]
    heads = gm_w_s.shape[1]
    core = lax.axis_index("c").astype(jnp.int32).reshape(1)

    is_col = dict(BIG)
    full = {name: [None] * depth for name, _ in BIG}

    def gather_comm(pairs):
        return _gather_comm([wts[nm][l].astype(BF16) for nm, l in pairs], [is_col[nm] for nm, _ in pairs])

    def store(pairs, arrays):
        for (nm, l), w in zip(pairs, arrays):
            full[nm][l] = w

    pairs = [("ffn1_w_in", 0)]
    store(pairs, _run_comm(gather_comm(pairs), "all_gather_first"))

    xs = x.reshape(n, d)
    saved = []
    for l in range(depth):
        pairs = [("ffn1_w_out", l)] + ([("mix_w_in", l), ("mix_w_out", l)] if l == 0 else [])
        x1, s_ffn1 = _ffn_fwd(xs, norm_ffn1[l], full["ffn1_w_in"][l], lambda l=l: full["ffn1_w_out"][l], "ffn1",
                              hosted=(functools.partial(gather_comm, pairs), functools.partial(store, pairs)))
        hm, u_ssm, zgm = _mix_in_fwd(x1, norm_mix[l], full["mix_w_in"][l], width, "mix_in")
        bt_re = jnp.swapaxes(ssm_b_re[l], 1, 2)
        bt_im = jnp.swapaxes(ssm_b_im[l], 1, 2)
        disc_in = (ssm_a_re[l], ssm_a_im[l], ssm_log_dt[l].reshape(groups, 1), bt_re, bt_im)
        lr, li, bbr, bbi = _disc_fwd(*disc_in)
        ops = _ssm_operands(lr, li, bbr, bbi, ssm_c_re[l], ssm_c_im[l], ssm_d[l], ssm_glu_w[l], ssm_glu_b[l])
        pairs = [("ffn2_w_in", l), ("ffn2_w_out", l)]
        y_ssm, hb, got = _ssm_fwd(u_ssm, ops, nseq, "s5_fwd", comm=gather_comm(pairs))
        store(pairs, got)
        bias_tile = jnp.broadcast_to(gm_b_s[l].T[:, :, None], (GM_CHUNK, heads, GM_HEAD_DIM)).reshape(GM_CHUNK, width)
        y_gm = _gmlp_fwd(zgm, gm_v_gain[l], gm_w_s[l], bias_tile, "gmlp_fwd")
        ycat, x2 = _mix_out_fwd(y_ssm, y_gm, gain_ssm_out[l], gain_gm_out[l], full["mix_w_out"][l], x1, "mix_out")
        hosted = None
        if l + 1 < depth:
            pairs = [("ffn1_w_in", l + 1), ("mix_w_in", l + 1), ("mix_w_out", l + 1)]
            hosted = (functools.partial(gather_comm, pairs), functools.partial(store, pairs))
        x3, s_ffn2 = _ffn_fwd(x2, norm_ffn2[l], full["ffn2_w_in"][l], full["ffn2_w_out"][l], "ffn2", hosted=hosted)
        saved.append(dict(ffn1=s_ffn1, x1=x1, hm=hm, zgm=zgm, disc_in=disc_in, ops=ops, u_ssm=u_ssm, hb=hb, y_ssm=y_ssm,
                          bias_tile=bias_tile, y_gm=y_gm, ycat=ycat, ffn2=s_ffn2))
        xs = x3

    dx, g_norm_final, loss_part = _loss_head(xs, norm_final, loss_target.reshape(n, d))
    big = {name: [None] * depth for name, _ in BIG}
    small = {name: [None] * depth for name in SMALL if name != "norm_final"}
    gpb = GROUPS_PER_BLOCK
    s_blk = STATES_PER_BLOCK
    psum_of, reduced = {}, {}

    def swap_comm(pairs):
        return _pair_exchange_comm([big[nm][l] for nm, l in pairs], [is_col[nm] for nm, _ in pairs])

    def take_swapped(pairs, others):
        for (nm, l), other in zip(pairs, others):
            psum_of[nm, l] = _pair_sum(big[nm][l], other, is_col[nm], core, f"grad_pair_sum_{nm}")

    def send_comm(pairs):
        return _chip_exchange_comm([psum_of[p] for p in pairs], [is_col[nm] for nm, _ in pairs])

    def take_sent(pairs, slots):
        for (nm, l), s in zip(pairs, slots):
            reduced[nm] = _chip_sum(s, core, l, depth, reduced.get(nm), f"grad_chip_sum_{nm}")

    def hosting(make, take, pairs):
        return functools.partial(make, pairs), functools.partial(take, pairs)

    for l in reversed(range(depth)):
        sv = saved[l]
        above = [(nm, l + 1) for nm in ("mix_w_in", "mix_w_out", "ffn1_w_in", "ffn1_w_out")] if l + 1 < depth else []
        dx, small["norm_ffn2"][l], big["ffn2_w_in"][l], big["ffn2_w_out"][l] = _ffn_bwd(
            dx, sv["ffn2"], norm_ffn2[l], full["ffn2_w_in"][l], full["ffn2_w_out"][l], "ffn2",
            hooks={"out_dx": hosting(swap_comm, take_swapped, above)} if above else None)
        mine = [("ffn2_w_in", l), ("ffn2_w_out", l)]
        dy_ssm, dy_gm, small["gain_ssm_out"][l], small["gain_gm_out"][l], got = _mix_out_bwd(
            dx, full["mix_w_out"][l], sv["y_ssm"], sv["y_gm"], gain_ssm_out[l], gain_gm_out[l], "mix_out_dx",
            comm=swap_comm(mine))
        take_swapped(mine, got)
        big["mix_w_out"][l] = _matmul(sv["ycat"], dx, "tn", tm=1024, tn=1024, tk=2048, name="mix_out_dw")
        dzgm, small["gm_w_s"][l], dbias_tile, small["gm_v_gain"][l] = _gmlp_bwd(
            sv["zgm"], dy_gm, gm_v_gain[l], gm_w_s[l], sv["bias_tile"], "gmlp_bwd")
        small["gm_b_s"][l] = dbias_tile.reshape(GM_CHUNK, heads, GM_HEAD_DIM).sum(-1).T
        (du_ssm, dlam, db_bd, dct_bd, dd, dw_bd, dbias), got = _ssm_bwd(
            sv["u_ssm"], dy_ssm, sv["hb"], sv["ops"], nseq, "s5_bwd", comm=send_comm(mine + above))
        take_sent(mine + above, got)
        dlr = dlam[:, 0, :s_blk].reshape(groups, SSM_STATE)
        dli = dlam[:, 0, s_blk:].reshape(groups, SSM_STATE)
        dbbr = _block_diag_extract(db_bd[:, :, :s_blk], gpb)
        dbbi = _block_diag_extract(db_bd[:, :, s_blk:], gpb)
        da_re, da_im, dldt, dbt_re, dbt_im = _disc_bwd(*sv["disc_in"], dlr, dli, dbbr, dbbi)
        small["ssm_a_re"][l], small["ssm_a_im"][l], small["ssm_log_dt"][l] = da_re, da_im, dldt.reshape(groups)
        small["ssm_b_re"][l] = jnp.swapaxes(dbt_re, 1, 2)
        small["ssm_b_im"][l] = jnp.swapaxes(dbt_im, 1, 2)
        small["ssm_c_re"][l] = _block_diag_extract(dct_bd[:, :, :s_blk], gpb)
        small["ssm_c_im"][l] = -_block_diag_extract(dct_bd[:, :, s_blk:], gpb)
        small["ssm_d"][l] = dd.reshape(groups, SSM_CH)
        small["ssm_glu_w"][l] = jnp.concatenate(
            [_block_diag_extract(dw_bd[:, :, :LANES], gpb), _block_diag_extract(dw_bd[:, :, LANES:], gpb)], axis=-1)
        small["ssm_glu_b"][l] = jnp.concatenate(
            [dbias[:, 0, :LANES].reshape(groups, SSM_CH), dbias[:, 0, LANES:].reshape(groups, SSM_CH)], axis=-1)
        cols_mi = 3 * width
        dw_mi = _matmul(sv["hm"], du_ssm, "tn", tm=1024, tn=width, tk=2048, name="mix_in_dw_ssm", out_cols=cols_mi)
        big["mix_w_in"][l] = _matmul(sv["hm"], dzgm, "tn", tm=1024, tn=width, tk=2048, name="mix_in_dw_gm",
                                     out_cols=cols_mi, col_off=width, into=dw_mi)
        dx, small["norm_mix"][l] = _proj_in_bwd([(du_ssm, 0), (dzgm, width)], full["mix_w_in"][l], sv["x1"],
                                                norm_mix[l], dx, "mix_in_dx")
        hooks = None
        if l == 0:
            mix, w_out_0, w_in_0 = [("mix_w_in", 0), ("mix_w_out", 0)], [("ffn1_w_out", 0)], [("ffn1_w_in", 0)]
            hooks = {"out_dx": hosting(swap_comm, take_swapped, mix), "out_dw": hosting(send_comm, take_sent, mix),
                     "in_dw_g": hosting(swap_comm, take_swapped, w_out_0),
                     "in_dw_u": hosting(send_comm, take_sent, w_out_0),
                     "in_dx": hosting(swap_comm, take_swapped, w_in_0)}

        def publish(which, dw, l=l):
            big[f"ffn1_w_{which}"][l] = dw

        dx, small["norm_ffn1"][l], big["ffn1_w_in"][l], big["ffn1_w_out"][l] = _ffn_bwd(
            dx, sv["ffn1"], norm_ffn1[l], full["ffn1_w_in"][l], full["ffn1_w_out"][l], "ffn1",
            hooks=hooks, publish=publish)
    grad_x = dx.reshape(nseq, seq, d)

    pieces = [jnp.stack(small[name]).reshape(-1) for name in SMALL if name != "norm_final"]
    pieces += [g_norm_final.reshape(-1), loss_part.reshape(1)]
    sizes = [p.shape[0] for p in pieces]
    total = sum(sizes)
    rows = -(-total // (LANES * N_DEV * SUBLANES)) * N_DEV * SUBLANES
    pad = rows * LANES - total
    tail = [("ffn1_w_in", 0)]
    flat_g, got = _all_reduce_small(
        jnp.concatenate(pieces + [jnp.zeros((pad,), F32)]).reshape(rows, LANES), send_comm(tail))
    take_sent(tail, got)
    flat_g = flat_g.reshape(-1)
    loss = flat_g[total - 1]

    names = [name for name, _ in BIG]
    grads = dict(zip(names, _run_comm(_pair_share_comm([reduced[nm] for nm in names]), "grad_pair_share")))
    offs = 0
    for name, size in zip(SMALL, sizes[:-1]):
        grads[name] = flat_g[offs:offs + size].reshape(wts[name].shape)
        offs += size

    delta, new_m, new_v = {}, {}, {}
    for name, _ in BIG:
        shape = wts[name].shape
        two_d = lambda a: a.reshape(shape[0] * shape[1], shape[2])
        go, dl, nm, nv = _adamw(two_d(wts[name]), two_d(grads[name]), two_d(mom[name]), two_d(var[name]),
                                f"adamw_{name}")
        grads[name] = go.reshape(shape)
        delta[name], new_m[name], new_v[name] = dl.reshape(shape), nm.reshape(shape), nv.reshape(shape)
    at_least_2d = lambda a: a.reshape(1, -1) if a.ndim == 1 else a
    dls, nms, nvs = _adamw_many(*[[at_least_2d(tree[k]) for k in SMALL] for tree in (wts, grads, mom, var)],
                                "adamw_small")
    for name, dl, nm, nv in zip(SMALL, dls, nms, nvs):
        shape = wts[name].shape
        delta[name], new_m[name], new_v[name] = dl.reshape(shape), nm.reshape(shape), nv.reshape(shape)

    return (loss, grad_x, *[grads[k] for k in WEIGHTS], *[delta[k] for k in WEIGHTS],
            *[new_m[k] for k in WEIGHTS], *[new_v[k] for k in WEIGHTS])
```

```python
import functools
import math

import jax
import jax.numpy as jnp
from jax import lax
from jax.experimental import pallas as pl
from jax.experimental.pallas import tpu as pltpu

F32 = jnp.float32
BF16 = jnp.bfloat16
MESH = pl.DeviceIdType.MESH

EPS = 1e-6
SSM_CH = 16
SSM_STATE = 64
GM_CHUNK = 128
GM_HEAD_DIM = 128
SUBLANES = 8
LANES = 128
GROUPS_PER_BLOCK = LANES // SSM_CH
STATES_PER_BLOCK = GROUPS_PER_BLOCK * SSM_STATE
SSM_TIME_CHUNK = 128
N_CHIPS = 4
N_DEV = 8

ADAM_LR = 0.001
ADAM_B1 = 0.9
ADAM_B2 = 0.999
ADAM_EPS = 1e-08
ADAM_WD = 0.01
ADAM_STEP = 10

VMEM_LIMIT = 56 * 1024 * 1024


def _tile(dim, pref, align):
    best = None
    t = align
    while t <= min(dim, pref):
        if dim % t == 0:
            best = t
        t += align
    return best if best is not None else dim


def _params(*sem):
    return pltpu.CompilerParams(dimension_semantics=sem, vmem_limit_bytes=VMEM_LIMIT)


def _gelu(x):
    c = math.sqrt(2.0 / math.pi)
    return 0.5 * x * (1.0 + jnp.tanh(c * (x + 0.044715 * x * x * x)))


def _gelu_and_grad(x):
    c = math.sqrt(2.0 / math.pi)
    t = jnp.tanh(c * (x + 0.044715 * x * x * x))
    g = 0.5 * x * (1.0 + t)
    dg = 0.5 * (1.0 + t) + 0.5 * x * (1.0 - t * t) * c * (1.0 + 3.0 * 0.044715 * x * x)
    return g, dg


def _sigmoid(x):
    return 0.5 * jnp.tanh(0.5 * x) + 0.5


def _matmul(a, b, mode, *, out_dtype=F32, scale=1.0, res=None, tm=512, tn=1024, tk=1024, name="mm",
            out_cols=None, col_off=0, into=None, comm=None):
    if mode == "nn":
        (m, k), (k2, n) = a.shape, b.shape
    elif mode == "nt":
        (m, k), (n, k2) = a.shape, b.shape
    else:
        (k, m), (k2, n) = a.shape, b.shape
    assert k == k2, (a.shape, b.shape, mode)
    tm = _tile(m, tm, 16 if mode != "tn" else LANES)
    tn = _tile(n, tn, LANES)
    tk = _tile(k, tk, LANES if mode != "tn" else 16)
    nk = k // tk
    grid = (m // tm, n // tn, nk)
    if mode == "nn":
        a_spec = pl.BlockSpec((tm, tk), lambda i, j, kk: (i, kk))
        b_spec = pl.BlockSpec((tk, tn), lambda i, j, kk: (kk, j))
        dims = (((1,), (0,)), ((), ()))
    elif mode == "nt":
        a_spec = pl.BlockSpec((tm, tk), lambda i, j, kk: (i, kk))
        b_spec = pl.BlockSpec((tn, tk), lambda i, j, kk: (j, kk))
        dims = (((1,), (1,)), ((), ()))
    else:
        a_spec = pl.BlockSpec((tk, tm), lambda i, j, kk: (kk, i))
        b_spec = pl.BlockSpec((tk, tn), lambda i, j, kk: (kk, j))
        dims = (((0,), (0,)), ((), ()))
    assert col_off % tn == 0
    off = col_off // tn
    r_spec = pl.BlockSpec((tm, tn), lambda i, j, kk: (i, j))
    o_spec = pl.BlockSpec((tm, tn), lambda i, j, kk: (i, j + off))
    has_res = res is not None
    has_into = into is not None

    def body(*refs):
        refs = list(refs)
        a_ref, b_ref = refs[:2]
        pos = 2
        r_ref = None
        if has_res:
            r_ref = refs[pos]
            pos += 1
        if has_into:
            pos += 1
        o_ref = refs[pos]
        acc_ref = refs[pos + 1] if nk > 1 else None
        part = lax.dot_general(a_ref[...].astype(BF16), b_ref[...].astype(BF16), dims,
                               preferred_element_type=F32)

        def finish(r):
            if scale != 1.0:
                r = r * scale
            if has_res:
                r = r + r_ref[...].astype(F32)
            o_ref[...] = r.astype(o_ref.dtype)

        if nk == 1:
            finish(part)
        else:
            kk = pl.program_id(2)

            @pl.when(kk == 0)
            def _():
                acc_ref[...] = part

            @pl.when(kk > 0)
            def _():
                acc_ref[...] += part

            @pl.when(kk == nk - 1)
            def _():
                finish(acc_ref[...])

    in_specs = [a_spec, b_spec]
    args = [a, b]
    if has_res:
        in_specs.append(r_spec)
        args.append(res)
    aliases = {}
    if has_into:
        in_specs.append(pl.BlockSpec(memory_space=pl.ANY))
        args.append(into)
        aliases = {len(args) - 1: 0}
    (out,), comm_outs = _call(
        body, name=name, grid=grid, in_specs=in_specs, out_specs=[o_spec],
        out_shape=[jax.ShapeDtypeStruct((m, n if out_cols is None else out_cols), out_dtype)],
        scratch_shapes=[pltpu.VMEM((tm, tn), F32)] if nk > 1 else [],
        aliases=aliases, semantics=("parallel", "parallel", "arbitrary"), args=args, comm=comm)
    return out if comm is None else (out, comm_outs)


class _Comm:
    def __init__(self, ins, outs, sems, start, finish, alias=None, middle=None):
        self.ins, self.outs, self.sems, self.start, self.finish = list(ins), list(outs), list(sems), start, finish
        self.alias = dict(alias or {})
        self.middle = middle


def _call(body, *, name, grid, in_specs, out_specs, out_shape, args, scratch_shapes=(), semantics=(), aliases=None,
          comm=None):
    in_specs, out_specs, out_shape = list(in_specs), list(out_specs), list(out_shape)
    scratch_shapes = list(scratch_shapes)
    aliases = dict(aliases or {})
    if comm is None:
        outs = pl.pallas_call(
            body, name=name, grid=grid, in_specs=in_specs, out_specs=out_specs, out_shape=out_shape,
            scratch_shapes=scratch_shapes, input_output_aliases=aliases, compiler_params=_params(*semantics),
        )(*args)
        return list(outs), []
    n_in, n_out, n_scr = len(in_specs), len(out_specs), len(scratch_shapes)
    c_in, c_out = len(comm.ins), len(comm.outs)
    for ci, co in comm.alias.items():
        aliases[n_in + ci] = n_out + co

    def hosted(*refs):
        refs = list(refs)
        ins, cins = refs[:n_in], refs[n_in:n_in + c_in]
        p = n_in + c_in
        outs, couts = refs[p:p + n_out], refs[p + n_out:p + n_out + c_out]
        p += n_out + c_out
        scr, sems = refs[p:p + n_scr], refs[p + n_scr:]
        ids = [pl.program_id(a) for a in range(len(grid))]
        first = functools.reduce(jnp.logical_and, [i == 0 for i in ids])
        last = functools.reduce(jnp.logical_and, [i == g - 1 for i, g in zip(ids, grid)])

        total = math.prod(grid)
        late = comm.middle is not None and total >= 4

        @pl.when(first)
        def _():
            comm.start(cins, couts, sems)

        if late:
            flat = functools.reduce(lambda acc, ig: acc * ig[1] + ig[0], zip(ids, grid), 0)

            @pl.when(flat == (3 * total) // 4)
            def _():
                comm.middle(cins, couts, sems)

        body(*ins, *outs, *scr)

        @pl.when(last)
        def _():
            if comm.middle is not None and not late:
                comm.middle(cins, couts, sems)
            comm.finish(cins, couts, sems)

    any_spec = pl.BlockSpec(memory_space=pl.ANY)
    outs = pl.pallas_call(
        hosted, name=name, grid=grid, in_specs=in_specs + [any_spec] * c_in, out_specs=out_specs + [any_spec] * c_out,
        out_shape=out_shape + comm.outs, scratch_shapes=scratch_shapes + comm.sems, input_output_aliases=aliases,
        compiler_params=_params(*(["arbitrary"] * len(grid))),
    )(*args, *comm.ins)
    return list(outs[:n_out]), list(outs[n_out:])


def _run_comm(comm, name):
    c_in, c_out = len(comm.ins), len(comm.outs)

    def body(*refs):
        refs = list(refs)
        cins, couts, sems = refs[:c_in], refs[c_in:c_in + c_out], refs[c_in + c_out:]
        comm.start(cins, couts, sems)
        if comm.middle is not None:
            comm.middle(cins, couts, sems)
        comm.finish(cins, couts, sems)

    any_spec = pl.BlockSpec(memory_space=pl.ANY)
    return list(pl.pallas_call(
        body, name=name, in_specs=[any_spec] * c_in, out_specs=[any_spec] * c_out, out_shape=comm.outs,
        scratch_shapes=comm.sems, input_output_aliases=comm.alias,
    )(*comm.ins))


def _loss_head(x, gain, target):
    n, d = x.shape
    tm = _tile(n, 512, 8)
    steps = n // tm

    def body(x_ref, g_ref, t_ref, dx_ref, dg_ref, loss_ref, acc_ref, lacc_ref):
        i = pl.program_id(0)
        xv = x_ref[...]
        g = g_ref[...]
        r = lax.rsqrt(jnp.mean(xv * xv, axis=-1, keepdims=True) + EPS)
        xh = xv * r
        err = xh * g - t_ref[...]
        dy = err * (1.0 / d)
        dyg = dy * g
        mean = jnp.mean(dyg * xh, axis=-1, keepdims=True)
        dx_ref[...] = r * (dyg - xh * mean)
        part = jnp.sum((dy * xh).reshape(tm // SUBLANES, SUBLANES, d), axis=0)
        lpart = jnp.sum((err * err).reshape(tm // SUBLANES, SUBLANES, d), axis=0)

        @pl.when(i == 0)
        def _():
            acc_ref[...] = part
            lacc_ref[...] = lpart

        @pl.when(i > 0)
        def _():
            acc_ref[...] += part
            lacc_ref[...] += lpart

        @pl.when(i == steps - 1)
        def _():
            dg_ref[...] = jnp.sum(acc_ref[...], axis=0, keepdims=True)
            tot = jnp.sum(jnp.sum(lacc_ref[...], axis=0, keepdims=True), axis=1, keepdims=True)
            loss_ref[...] = jnp.broadcast_to(tot * (0.5 / d), loss_ref.shape)

    row = pl.BlockSpec((tm, d), lambda i: (i, 0))
    vec = pl.BlockSpec((1, d), lambda i: (0, 0))
    dx, dg, loss = pl.pallas_call(
        body, name="loss_head", grid=(steps,),
        in_specs=[row, vec, row],
        out_specs=[row, vec, pl.BlockSpec((1, LANES), lambda i: (0, 0))],
        out_shape=[jax.ShapeDtypeStruct((n, d), F32), jax.ShapeDtypeStruct((1, d), F32),
                   jax.ShapeDtypeStruct((1, LANES), F32)],
        scratch_shapes=[pltpu.VMEM((SUBLANES, d), F32), pltpu.VMEM((SUBLANES, d), F32)],
        compiler_params=_params("arbitrary"),
    )(x, gain.reshape(1, d), target)
    return dx, dg.reshape(d), loss[0, 0]


def _rms_rows(xv):
    return lax.rsqrt(jnp.mean(xv * xv, axis=-1, keepdims=True) + EPS)


def _ffn_in_fwd(x, gain, w_in, name, comm=None):
    n, d = x.shape
    f = w_in.shape[1] // 2
    tm = _tile(n, 256, 16)
    tn = _tile(f, 4096, LANES)
    nj = f // tn

    def body(x_ref, gain_ref, wg_ref, wu_ref, h_ref, t_ref, q_ref, a_ref):
        @pl.when(pl.program_id(1) == 0)
        def _():
            xv = x_ref[...]
            h_ref[...] = (xv * _rms_rows(xv) * gain_ref[...]).astype(h_ref.dtype)

        h = h_ref[...]
        g = jnp.dot(h, wg_ref[...], preferred_element_type=F32)
        u = jnp.dot(h, wu_ref[...], preferred_element_type=F32)
        s = _sigmoid(g)
        t = g * s
        t_ref[...] = t.astype(t_ref.dtype)
        q_ref[...] = (u * (s + t * (1.0 - s))).astype(q_ref.dtype)
        a_ref[...] = (t * u).astype(a_ref.dtype)

    row = pl.BlockSpec((tm, d), lambda i, j: (i, 0))
    tile = pl.BlockSpec((tm, tn), lambda i, j: (i, j))
    act = jax.ShapeDtypeStruct((n, f), BF16)
    outs, comm_outs = _call(
        body, name=name, grid=(n // tm, nj),
        in_specs=[row, pl.BlockSpec((1, d), lambda i, j: (0, 0)),
                  pl.BlockSpec((d, tn), lambda i, j: (0, j)), pl.BlockSpec((d, tn), lambda i, j: (0, j + nj))],
        out_specs=[row, tile, tile, tile],
        out_shape=[jax.ShapeDtypeStruct((n, d), BF16), act, act, act],
        semantics=("parallel", "arbitrary"), args=(x, gain.reshape(1, d), w_in, w_in), comm=comm)
    return outs if comm is None else (outs, comm_outs)


def _ffn_out_bwd(dout, w_out, t, q, name, comm=None):
    n, d = dout.shape
    f = w_out.shape[0]
    tm = _tile(n, 256, 16)
    tn = _tile(f, 4096, LANES)

    def body(d_ref, w_ref, t_ref, q_ref, dg_ref, du_ref):
        da = 0.5 * lax.dot_general(d_ref[...].astype(BF16), w_ref[...], (((1,), (1,)), ((), ())),
                                   preferred_element_type=F32)
        dg_ref[...] = (da * q_ref[...].astype(F32)).astype(dg_ref.dtype)
        du_ref[...] = (da * t_ref[...].astype(F32)).astype(du_ref.dtype)

    tile = pl.BlockSpec((tm, tn), lambda i, j: (i, j))
    act = jax.ShapeDtypeStruct((n, f), BF16)
    outs, comm_outs = _call(
        body, name=name, grid=(n // tm, f // tn),
        in_specs=[pl.BlockSpec((tm, d), lambda i, j: (i, 0)), pl.BlockSpec((tn, d), lambda i, j: (j, 0)), tile, tile],
        out_specs=[tile, tile], out_shape=[act, act],
        semantics=("parallel", "parallel"), args=(dout, w_out, t, q), comm=comm)
    return outs if comm is None else (outs, comm_outs)


def _proj_in_bwd(parts, w, x, gain, dres, name, comm=None):
    n, d = x.shape
    tm = _tile(n, 256, 8)
    steps = n // tm
    np_ = len(parts)
    offs = [off for _, off in parts]
    widths = [a.shape[1] for a, _ in parts]

    def body(*refs):
        a_refs = refs[:np_]
        w_ref, x_ref, g_ref, dr_ref, dx_ref, dg_ref, acc_ref = refs[np_:]
        i = pl.program_id(0)
        dh = None
        for a_ref, off, kp in zip(a_refs, offs, widths):
            part = lax.dot_general(a_ref[...].astype(BF16), w_ref[:, off:off + kp], (((1,), (1,)), ((), ())),
                                   preferred_element_type=F32)
            dh = part if dh is None else dh + part
        xv = x_ref[...]
        r = _rms_rows(xv)
        xh = xv * r
        dyg = dh * g_ref[...]
        mean = jnp.mean(dyg * xh, axis=-1, keepdims=True)
        dx_ref[...] = dr_ref[...] + r * (dyg - xh * mean)
        part = jnp.sum((dh * xh).reshape(tm // SUBLANES, SUBLANES, d), axis=0)

        @pl.when(i == 0)
        def _():
            acc_ref[...] = part

        @pl.when(i > 0)
        def _():
            acc_ref[...] += part

        @pl.when(i == steps - 1)
        def _():
            dg_ref[...] = jnp.sum(acc_ref[...], axis=0, keepdims=True)

    row = pl.BlockSpec((tm, d), lambda i: (i, 0))
    vec = pl.BlockSpec((1, d), lambda i: (0, 0))
    (dx, dg), comm_outs = _call(
        body, name=name, grid=(steps,),
        in_specs=[pl.BlockSpec((tm, kp), lambda i: (i, 0)) for kp in widths]
        + [pl.BlockSpec(w.shape, lambda i: (0, 0)), row, vec, row],
        out_specs=[row, vec],
        out_shape=[jax.ShapeDtypeStruct((n, d), F32), jax.ShapeDtypeStruct((1, d), F32)],
        scratch_shapes=[pltpu.VMEM((SUBLANES, d), F32)],
        semantics=("arbitrary",), args=(*[a for a, _ in parts], w, x, gain.reshape(1, d), dres), comm=comm)
    return (dx, dg.reshape(d)) if comm is None else (dx, dg.reshape(d), comm_outs)


def _mix_in_fwd(x, gain, w, width, name):
    n, d = x.shape
    cols = w.shape[1]
    tm = _tile(n, 512, 16)

    def body(x_ref, gain_ref, w_ref, h_ref, u_ref, z_ref):
        xv = x_ref[...]
        h = (xv * _rms_rows(xv) * gain_ref[...]).astype(h_ref.dtype)
        h_ref[...] = h
        z = jnp.dot(h, w_ref[...], preferred_element_type=F32)
        u_ref[...] = z[:, 0:width]
        z_ref[...] = z[:, width:cols]

    row = pl.BlockSpec((tm, d), lambda i: (i, 0))
    return pl.pallas_call(
        body, name=name, grid=(n // tm,),
        in_specs=[row, pl.BlockSpec((1, d), lambda i: (0, 0)), pl.BlockSpec((d, cols), lambda i: (0, 0))],
        out_specs=[row, pl.BlockSpec((tm, width), lambda i: (i, 0)), pl.BlockSpec((tm, cols - width), lambda i: (i, 0))],
        out_shape=[jax.ShapeDtypeStruct((n, d), BF16), jax.ShapeDtypeStruct((n, width), F32),
                   jax.ShapeDtypeStruct((n, cols - width), F32)],
        compiler_params=_params("parallel"),
    )(x, gain.reshape(1, d), w)


def _tril_mask():
    t = lax.broadcasted_iota(jnp.int32, (GM_CHUNK, GM_CHUNK), 0)
    s = lax.broadcasted_iota(jnp.int32, (GM_CHUNK, GM_CHUNK), 1)
    return s <= t


def _gmlp_fwd(zgm, v_gain, w_s, bias_tile, name):
    n, w2 = zgm.shape
    w = w2 // 2
    heads = w // GM_HEAD_DIM
    tm = _tile(n, 512, GM_CHUNK)
    nq = tm // GM_CHUNK

    def body(u_ref, v_ref, gain_ref, w_ref, b_ref, o_ref):
        mask = _tril_mask()
        ug = _gelu(u_ref[...])
        vg = _gelu(v_ref[...])
        for h in range(heads):
            cols = slice(h * GM_HEAD_DIM, (h + 1) * GM_HEAD_DIM)
            vh = vg[:, cols]
            r = lax.rsqrt(jnp.mean(vh * vh, axis=-1, keepdims=True) + EPS)
            vn = (vh * r * gain_ref[:, cols]).astype(BF16)
            wm = jnp.where(mask, w_ref[h], 0.0).astype(BF16)
            for q in range(nq):
                rows = slice(q * GM_CHUNK, (q + 1) * GM_CHUNK)
                s = jnp.dot(wm, vn[rows], preferred_element_type=F32) + b_ref[:, cols]
                o_ref[rows, cols] = ug[rows, cols] * s

    return pl.pallas_call(
        body, name=name, grid=(n // tm,),
        in_specs=[pl.BlockSpec((tm, w), lambda i: (i, 0)), pl.BlockSpec((tm, w), lambda i: (i, 1)),
                  pl.BlockSpec((1, w), lambda i: (0, 0)),
                  pl.BlockSpec((heads, GM_CHUNK, GM_CHUNK), lambda i: (0, 0, 0)),
                  pl.BlockSpec((GM_CHUNK, w), lambda i: (0, 0))],
        out_specs=pl.BlockSpec((tm, w), lambda i: (i, 0)),
        out_shape=jax.ShapeDtypeStruct((n, w), F32),
        compiler_params=_params("parallel"),
    )(zgm, zgm, v_gain.reshape(1, w), w_s, bias_tile)


def _gmlp_bwd(zgm, dy, v_gain, w_s, bias_tile, name):
    n, w2 = zgm.shape
    w = w2 // 2
    heads = w // GM_HEAD_DIM
    tm = _tile(n, 512, GM_CHUNK)
    nq = tm // GM_CHUNK
    steps = n // tm

    def body(z_ref, dy_ref, gain_ref, w_ref, b_ref, dz_ref, dw_ref, db_ref, dgain_ref):
        i = pl.program_id(0)
        mask = _tril_mask()

        @pl.when(i == 0)
        def _():
            dw_ref[...] = jnp.zeros_like(dw_ref)
            db_ref[...] = jnp.zeros_like(db_ref)
            dgain_ref[...] = jnp.zeros_like(dgain_ref)

        ug, dug_du = _gelu_and_grad(z_ref[:, 0:w])
        vg, dvg_dv = _gelu_and_grad(z_ref[:, w:w2])
        dyv = dy_ref[...]
        for h in range(heads):
            cols = slice(h * GM_HEAD_DIM, (h + 1) * GM_HEAD_DIM)
            vh = vg[:, cols]
            r = lax.rsqrt(jnp.mean(vh * vh, axis=-1, keepdims=True) + EPS)
            vhat = vh * r
            gain = gain_ref[:, cols]
            vn = (vhat * gain).astype(BF16)
            wm = jnp.where(mask, w_ref[h], 0.0).astype(BF16)
            dvn_parts = []
            for q in range(nq):
                rows = slice(q * GM_CHUNK, (q + 1) * GM_CHUNK)
                s = jnp.dot(wm, vn[rows], preferred_element_type=F32) + b_ref[:, cols]
                dyq = dyv[rows, cols]
                dz_ref[rows, cols] = dyq * s * dug_du[rows, cols]
                ds = dyq * ug[rows, cols]
                db_ref[:, cols] += ds
                dsb = ds.astype(BF16)
                dw_ref[h] += lax.dot_general(dsb, vn[rows], (((1,), (1,)), ((), ())), preferred_element_type=F32)
                dvn_parts.append(lax.dot_general(wm, dsb, (((0,), (0,)), ((), ())), preferred_element_type=F32))
            dvn = jnp.concatenate(dvn_parts, axis=0) if nq > 1 else dvn_parts[0]
            dgain_ref[:, cols] += jnp.sum(dvn * vhat, axis=0, keepdims=True)
            dvhat = dvn * gain
            mean = jnp.mean(dvhat * vhat, axis=-1, keepdims=True)
            dz_ref[:, w + h * GM_HEAD_DIM:w + (h + 1) * GM_HEAD_DIM] = r * (dvhat - vhat * mean) * dvg_dv[:, cols]

        @pl.when(i == steps - 1)
        def _():
            for h in range(heads):
                dw_ref[h] = jnp.where(mask, dw_ref[h], 0.0)

    dz, dw, db, dgain = pl.pallas_call(
        body, name=name, grid=(steps,),
        in_specs=[pl.BlockSpec((tm, w2), lambda i: (i, 0)), pl.BlockSpec((tm, w), lambda i: (i, 0)),
                  pl.BlockSpec((1, w), lambda i: (0, 0)),
                  pl.BlockSpec((heads, GM_CHUNK, GM_CHUNK), lambda i: (0, 0, 0)),
                  pl.BlockSpec((GM_CHUNK, w), lambda i: (0, 0))],
        out_specs=[pl.BlockSpec((tm, w2), lambda i: (i, 0)),
                   pl.BlockSpec((heads, GM_CHUNK, GM_CHUNK), lambda i: (0, 0, 0)),
                   pl.BlockSpec((GM_CHUNK, w), lambda i: (0, 0)),
                   pl.BlockSpec((1, w), lambda i: (0, 0))],
        out_shape=[jax.ShapeDtypeStruct((n, w2), F32), jax.ShapeDtypeStruct((heads, GM_CHUNK, GM_CHUNK), F32),
                   jax.ShapeDtypeStruct((GM_CHUNK, w), F32), jax.ShapeDtypeStruct((1, w), F32)],
        compiler_params=_params("arbitrary"),
    )(zgm, dy, v_gain.reshape(1, w), w_s, bias_tile)
    return dz, dw, db, dgain.reshape(w)


def _mix_out_fwd(y_ssm, y_gm, g1, g2, w_out, x, name):
    n, w = y_ssm.shape
    d = w_out.shape[1]
    tm = _tile(n, 512, 16)

    def body(a_ref, b_ref, g1_ref, g2_ref, w_ref, x_ref, ycat_ref, o_ref):
        for src, g_ref, lo in ((a_ref, g1_ref, 0), (b_ref, g2_ref, w)):
            v = src[...]
            ycat_ref[:, lo:lo + w] = (v * _rms_rows(v) * g_ref[...]).astype(ycat_ref.dtype)
        o_ref[...] = x_ref[...] + jnp.dot(ycat_ref[...], w_ref[...], preferred_element_type=F32)

    row = pl.BlockSpec((tm, w), lambda i: (i, 0))
    vec = pl.BlockSpec((1, w), lambda i: (0, 0))
    return pl.pallas_call(
        body, name=name, grid=(n // tm,),
        in_specs=[row, row, vec, vec, pl.BlockSpec((2 * w, d), lambda i: (0, 0)), pl.BlockSpec((tm, d), lambda i: (i, 0))],
        out_specs=[pl.BlockSpec((tm, 2 * w), lambda i: (i, 0)), pl.BlockSpec((tm, d), lambda i: (i, 0))],
        out_shape=[jax.ShapeDtypeStruct((n, 2 * w), BF16), jax.ShapeDtypeStruct((n, d), F32)],
        compiler_params=_params("parallel"),
    )(y_ssm, y_gm, g1.reshape(1, w), g2.reshape(1, w), w_out, x)


def _mix_out_bwd(dx, w_out, y_ssm, y_gm, g1, g2, name, comm=None):
    n, w = y_ssm.shape
    d = w_out.shape[1]
    tm = _tile(n, 512, 8)
    steps = n // tm

    def body(dx_ref, w_ref, a_ref, b_ref, g1_ref, g2_ref, da_ref, db_ref, dg1_ref, dg2_ref):
        i = pl.program_id(0)

        @pl.when(i == 0)
        def _():
            dg1_ref[...] = jnp.zeros_like(dg1_ref)
            dg2_ref[...] = jnp.zeros_like(dg2_ref)

        dycat = lax.dot_general(dx_ref[...].astype(BF16), w_ref[...], (((1,), (1,)), ((), ())),
                                preferred_element_type=F32)
        for src, g_ref, lo, dst, dg_ref in ((a_ref, g1_ref, 0, da_ref, dg1_ref), (b_ref, g2_ref, w, db_ref, dg2_ref)):
            v = src[...]
            dh = dycat[:, lo:lo + w]
            r = _rms_rows(v)
            vh = v * r
            dyg = dh * g_ref[...]
            mean = jnp.mean(dyg * vh, axis=-1, keepdims=True)
            dst[...] = r * (dyg - vh * mean)
            dg_ref[...] += jnp.sum(dh * vh, axis=0, keepdims=True)

    row = pl.BlockSpec((tm, w), lambda i: (i, 0))
    vec = pl.BlockSpec((1, w), lambda i: (0, 0))
    (da, db, dg1, dg2), comm_outs = _call(
        body, name=name, grid=(steps,),
        in_specs=[pl.BlockSpec((tm, d), lambda i: (i, 0)), pl.BlockSpec((2 * w, d), lambda i: (0, 0)), row, row, vec, vec],
        out_specs=[row, row, vec, vec],
        out_shape=[jax.ShapeDtypeStruct((n, w), F32), jax.ShapeDtypeStruct((n, w), F32),
                   jax.ShapeDtypeStruct((1, w), F32), jax.ShapeDtypeStruct((1, w), F32)],
        semantics=("arbitrary",), args=(dx, w_out, y_ssm, y_gm, g1.reshape(1, w), g2.reshape(1, w)), comm=comm)
    res = (da, db, dg1.reshape(w), dg2.reshape(w))
    return res if comm is None else (*res, comm_outs)


def _discretise(a_re, a_im, log_dt, bt_re, bt_im):
    dt = jnp.exp(log_dt)
    e = jnp.exp(a_re * dt)
    ang = a_im * dt
    lr = e * jnp.cos(ang)
    li = e * jnp.sin(ang)
    den = a_re * a_re + a_im * a_im
    cr = ((lr - 1.0) * a_re + li * a_im) / den
    ci = (li * a_re - (lr - 1.0) * a_im) / den
    cr3 = cr[:, None, :]
    ci3 = ci[:, None, :]
    return lr, li, cr3 * bt_re - ci3 * bt_im, cr3 * bt_im + ci3 * bt_re


def _disc_fwd(a_re, a_im, log_dt, bt_re, bt_im):
    g, p = a_re.shape
    c = bt_re.shape[1]

    def body(are_ref, aim_ref, ldt_ref, bre_ref, bim_ref, lr_ref, li_ref, bbr_ref, bbi_ref):
        lr, li, bbr, bbi = _discretise(are_ref[...], aim_ref[...], ldt_ref[...], bre_ref[...], bim_ref[...])
        lr_ref[...] = lr
        li_ref[...] = li
        bbr_ref[...] = bbr
        bbi_ref[...] = bbi

    return pl.pallas_call(
        body, name="s5_discretise",
        out_shape=[jax.ShapeDtypeStruct((g, p), F32), jax.ShapeDtypeStruct((g, p), F32),
                   jax.ShapeDtypeStruct((g, c, p), F32), jax.ShapeDtypeStruct((g, c, p), F32)],
    )(a_re, a_im, log_dt, bt_re, bt_im)


def _disc_bwd(a_re, a_im, log_dt, bt_re, bt_im, dlr, dli, dbbr, dbbi):
    g, p = a_re.shape
    c = bt_re.shape[1]

    def body(are_ref, aim_ref, ldt_ref, bre_ref, bim_ref, dlr_ref, dli_ref, dbbr_ref, dbbi_ref,
             dare_ref, daim_ref, dldt_ref, dbre_ref, dbim_ref):
        _, vjp = jax.vjp(_discretise, are_ref[...], aim_ref[...], ldt_ref[...], bre_ref[...], bim_ref[...])
        dare, daim, dldt, dbre, dbim = vjp((dlr_ref[...], dli_ref[...], dbbr_ref[...], dbbi_ref[...]))
        dare_ref[...] = dare
        daim_ref[...] = daim
        dldt_ref[...] = dldt
        dbre_ref[...] = dbre
        dbim_ref[...] = dbim

    return pl.pallas_call(
        body, name="s5_discretise_bwd",
        out_shape=[jax.ShapeDtypeStruct((g, p), F32), jax.ShapeDtypeStruct((g, p), F32),
                   jax.ShapeDtypeStruct((g, 1), F32),
                   jax.ShapeDtypeStruct((g, c, p), F32), jax.ShapeDtypeStruct((g, c, p), F32)],
    )(a_re, a_im, log_dt, bt_re, bt_im, dlr, dli, dbbr, dbbi)


def _block_diag(w, nb):
    g, a, b = w.shape
    gpb = g // nb
    eye = jnp.eye(gpb, dtype=w.dtype)
    w4 = w.reshape(nb, gpb, a, b)
    return jnp.einsum("ngab,gh->ngahb", w4, eye).reshape(nb, gpb * a, gpb * b)


def _block_diag_extract(m, gpb):
    nb, ga, gb = m.shape
    a, b = ga // gpb, gb // gpb
    m5 = m.reshape(nb, gpb, a, gpb, b)
    idx = jnp.arange(gpb)
    return m5[:, idx, :, idx, :].transpose(1, 0, 2, 3).reshape(nb * gpb, a, b)


def _ssm_operands(lr, li, bbr, bbi, c_re, c_im, d_skip, glu_w, glu_b):
    g = lr.shape[0]
    nb = g // GROUPS_PER_BLOCK
    s = STATES_PER_BLOCK
    lam = jnp.concatenate([lr.reshape(nb, 1, s), li.reshape(nb, 1, s)], axis=-1)
    b_bd = jnp.concatenate([_block_diag(bbr, nb), _block_diag(bbi, nb)], axis=-1)
    ct_re = jnp.swapaxes(c_re, 1, 2)
    ct_im = jnp.swapaxes(c_im, 1, 2)
    c_bd = jnp.concatenate([_block_diag(ct_re, nb), -_block_diag(ct_im, nb)], axis=1)
    dsk = d_skip.reshape(nb, 1, LANES)
    w_bd = jnp.concatenate([_block_diag(glu_w[:, :, :SSM_CH], nb), _block_diag(glu_w[:, :, SSM_CH:], nb)], axis=-1)
    bias = jnp.concatenate([glu_b[:, :SSM_CH].reshape(nb, 1, LANES), glu_b[:, SSM_CH:].reshape(nb, 1, LANES)], axis=-1)
    return lam, b_bd.astype(BF16), c_bd.astype(BF16), dsk, w_bd.astype(BF16), bias


def _roll_rows(v, shift):
    return v if shift % SUBLANES == 0 else pltpu.roll(v, shift % SUBLANES, 0)


def _scan_chunk_rows(seq, nseq):
    return _tile(seq, max(8 * SSM_TIME_CHUNK // nseq, 8), max(SUBLANES // nseq, 1) * 8)


def _ssm_fwd(u, ops, nseq, name, comm=None):
    lam, b_bd, c_bd, dsk, w_bd, bias = ops
    rows_total, w = u.shape
    seq = rows_total // nseq
    nb = w // LANES
    s = STATES_PER_BLOCK
    tc = _scan_chunk_rows(seq, nseq)
    nk = seq // tc
    rows = tc * nseq
    stages = SUBLANES // nseq

    def body(u_ref, lam_ref, b_ref, c_ref, d_ref, w_ref, bias_ref, y_ref, hb_ref, buf, st, rbuf):
        k = pl.program_id(1)

        @pl.when(k == 0)
        def _():
            st[...] = jnp.zeros_like(st)

        hb_ref[...] = st[...]
        for q in range(nseq):
            rbuf[pl.ds(q, tc, stride=nseq), :] = u_ref[q]
        u = rbuf[...]
        buf[...] = jnp.dot(u.astype(BF16), b_ref[0], preferred_element_type=F32)
        lr = jnp.broadcast_to(lam_ref[0, :, 0:s], (SUBLANES, s))
        li = jnp.broadcast_to(lam_ref[0, :, s:2 * s], (SUBLANES, s))
        row = lax.broadcasted_iota(jnp.int32, (SUBLANES, s), 0)

        def step(i, carry):
            pr, pi = carry
            r0 = pl.multiple_of(i * SUBLANES, SUBLANES)
            br = buf[pl.ds(r0, SUBLANES), 0:s]
            bi = buf[pl.ds(r0, SUBLANES), s:2 * s]
            outr = outi = None
            for j in range(stages):
                rr = _roll_rows(pr, nseq)
                ri = _roll_rows(pi, nseq)
                pr = lr * rr - li * ri + br
                pi = lr * ri + li * rr + bi
                outr = pr if j == 0 else jnp.where(row >= j * nseq, pr, outr)
                outi = pi if j == 0 else jnp.where(row >= j * nseq, pi, outi)
            buf[pl.ds(r0, SUBLANES), 0:s] = outr
            buf[pl.ds(r0, SUBLANES), s:2 * s] = outi
            return outr, outi

        hr, hi = lax.fori_loop(0, rows // SUBLANES, step, (st[:, 0:s], st[:, s:2 * s]), unroll=2)
        st[:, 0:s] = hr
        st[:, s:2 * s] = hi
        y = jnp.dot(buf[...].astype(BF16), c_ref[0], preferred_element_type=F32) + d_ref[0] * u
        z = jnp.dot(_gelu(y).astype(BF16), w_ref[0], preferred_element_type=F32) + bias_ref[0]
        rbuf[...] = z[:, 0:LANES] * _sigmoid(z[:, LANES:2 * LANES])
        for q in range(nseq):
            y_ref[q] = rbuf[pl.ds(q, tc, stride=nseq), :]

    blk = lambda shape: pl.BlockSpec(shape, lambda b, k: (b, 0, 0))
    tok = pl.BlockSpec((nseq, tc, LANES), lambda b, k: (0, k, b))
    (y, hb), comm_outs = _call(
        body, name=name, grid=(nb, nk),
        in_specs=[tok, blk((1, 1, 2 * s)), blk((1, LANES, 2 * s)), blk((1, 2 * s, LANES)),
                  blk((1, 1, LANES)), blk((1, LANES, 2 * LANES)), blk((1, 1, 2 * LANES))],
        out_specs=[tok, pl.BlockSpec((SUBLANES, 2 * s), lambda b, k: (k, b))],
        out_shape=[jax.ShapeDtypeStruct((nseq, seq, w), F32),
                   jax.ShapeDtypeStruct((nk * SUBLANES, nb * 2 * s), F32)],
        scratch_shapes=[pltpu.VMEM((rows, 2 * s), F32), pltpu.VMEM((SUBLANES, 2 * s), F32),
                        pltpu.VMEM((rows, LANES), F32)],
        semantics=("parallel", "arbitrary"),
        args=(u.reshape(nseq, seq, w), lam, b_bd, c_bd, dsk, w_bd, bias), comm=comm)
    y = y.reshape(nseq * seq, w)
    return (y, hb) if comm is None else (y, hb, comm_outs)


def _scan_with(nblk, step, carry, between):
    runs = len(between)
    per = nblk // runs
    for i in range(runs):
        hi = nblk if i == runs - 1 else (i + 1) * per
        carry = lax.fori_loop(i * per, hi, step, carry, unroll=True)
        between[i]()
    return carry


def _ssm_fwd_pair(u, ops, nseq, name, comm=None):
    lam, b_bd, c_bd, dsk, w_bd, bias = ops
    rows_total, w = u.shape
    seq = rows_total // nseq
    nb = w // LANES
    s = STATES_PER_BLOCK
    tc = _scan_chunk_rows(seq, nseq)
    nk = seq // tc
    rows = tc * nseq
    nblk = rows // SUBLANES
    stages = SUBLANES // nseq
    two = 2 * LANES
    ncol = 4

    def body(u_ref, lam_ref, b_ref, c_ref, d_ref, w_ref, bias_ref, y_ref, hb_ref, buf_a, buf_b, st, rbuf_a, rbuf_b):
        k = pl.program_id(1)

        @pl.when(k == 0)
        def _():
            st[...] = jnp.zeros_like(st)

        hb_ref[...] = st[...]
        rbufs = (rbuf_a, rbuf_b)
        for q in range(nseq):
            for e in range(2):
                rbufs[e][pl.ds(q, tc, stride=nseq), :] = u_ref[q, :, e * LANES:(e + 1) * LANES]
        row = lax.broadcasted_iota(jnp.int32, (SUBLANES, s), 0)
        bufs = (buf_a, buf_b)

        def u_of(e):
            return rbufs[e][...]

        def project_in(e, j):
            cols = slice(j * (2 * s // ncol), (j + 1) * (2 * s // ncol))
            bufs[e][:, cols] = jnp.dot(u_of(e).astype(BF16), b_ref[e, :, cols], preferred_element_type=F32)

        def scan(e, between):
            buf = bufs[e]
            lr = jnp.broadcast_to(lam_ref[e, :, 0:s], (SUBLANES, s))
            li = jnp.broadcast_to(lam_ref[e, :, s:2 * s], (SUBLANES, s))

            def step(i, carry):
                pr, pi = carry
                r0 = pl.multiple_of(i * SUBLANES, SUBLANES)
                br = buf[pl.ds(r0, SUBLANES), 0:s]
                bi = buf[pl.ds(r0, SUBLANES), s:2 * s]
                outr = outi = None
                for j in range(stages):
                    rr = _roll_rows(pr, nseq)
                    ri = _roll_rows(pi, nseq)
                    pr = lr * rr - li * ri + br
                    pi = lr * ri + li * rr + bi
                    outr = pr if j == 0 else jnp.where(row >= j * nseq, pr, outr)
                    outi = pi if j == 0 else jnp.where(row >= j * nseq, pi, outi)
                buf[pl.ds(r0, SUBLANES), 0:s] = outr
                buf[pl.ds(r0, SUBLANES), s:2 * s] = outi
                return outr, outi

            lo = e * 2 * s
            hr, hi = _scan_with(nblk, step, (st[:, lo:lo + s], st[:, lo + s:lo + 2 * s]), between)
            st[:, lo:lo + s] = hr
            st[:, lo + s:lo + 2 * s] = hi

        part = {}

        def project_out(e, j):
            ks = slice(j * (2 * s // ncol), (j + 1) * (2 * s // ncol))
            p = jnp.dot(bufs[e][:, ks].astype(BF16), c_ref[e, ks, :], preferred_element_type=F32)
            part[e] = p if j == 0 else part[e] + p

        def finish(e):
            y = part[e] + d_ref[e] * u_of(e)
            z = jnp.dot(_gelu(y).astype(BF16), w_ref[e], preferred_element_type=F32) + bias_ref[e]
            part[e] = z[:, 0:LANES] * _sigmoid(z[:, LANES:two])

        nothing = lambda: None
        for j in range(ncol):
            project_in(0, j)
        scan(0, [functools.partial(project_in, 1, j) for j in range(ncol)])
        scan(1, [functools.partial(project_out, 0, j) for j in range(ncol)] + [functools.partial(finish, 0), nothing,
                                                                                nothing, nothing])
        for j in range(ncol):
            project_out(1, j)
        finish(1)
        for e in range(2):
            rbufs[e][...] = part[e]
            for q in range(nseq):
                y_ref[q, :, e * LANES:(e + 1) * LANES] = rbufs[e][pl.ds(q, tc, stride=nseq), :]

    blk = lambda shape: pl.BlockSpec(shape, lambda b, k: (b, 0, 0))
    tok = pl.BlockSpec((nseq, tc, two), lambda b, k: (0, k, b))
    (y, hb), comm_outs = _call(
        body, name=name, grid=(nb // 2, nk),
        in_specs=[tok, blk((2, 1, 2 * s)), blk((2, LANES, 2 * s)), blk((2, 2 * s, LANES)),
                  blk((2, 1, LANES)), blk((2, LANES, two)), blk((2, 1, two))],
        out_specs=[tok, pl.BlockSpec((SUBLANES, 4 * s), lambda b, k: (k, b))],
        out_shape=[jax.ShapeDtypeStruct((nseq, seq, w), F32),
                   jax.ShapeDtypeStruct((nk * SUBLANES, nb * 2 * s), F32)],
        scratch_shapes=[pltpu.VMEM((rows, 2 * s), F32), pltpu.VMEM((rows, 2 * s), F32),
                        pltpu.VMEM((SUBLANES, 4 * s), F32), pltpu.VMEM((rows, LANES), F32),
                        pltpu.VMEM((rows, LANES), F32)],
        semantics=("parallel", "arbitrary"),
        args=(u.reshape(nseq, seq, w), lam, b_bd, c_bd, dsk, w_bd, bias), comm=comm)
    y = y.reshape(nseq * seq, w)
    return (y, hb) if comm is None else (y, hb, comm_outs)


def _ssm_bwd_pair(u, dout, hb, ops, nseq, name, comm=None):
    lam, b_bd, c_bd, dsk, w_bd, bias = ops
    rows_total, w = u.shape
    seq = rows_total // nseq
    nb = w // LANES
    s = STATES_PER_BLOCK
    tc = _scan_chunk_rows(seq, nseq)
    nk = seq // tc
    rows = tc * nseq
    nblk = rows // SUBLANES
    stages = SUBLANES // nseq
    two = 2 * LANES
    ncol = 4
    cw = 2 * s // ncol
    tn_dims = (((0,), (0,)), ((), ()))
    nt_dims = (((1,), (1,)), ((), ()))

    def body(u_ref, dy_ref, hb_ref, lam_ref, b_ref, c_ref, d_ref, w_ref, bias_ref,
             du_ref, dlam_ref, db_ref, dct_ref, dd_ref, dw_ref, dbias_ref,
             hbuf_a, hbuf_b, gbuf_a, gbuf_b, gst, lacc, ru_a, ru_b, rd_a, rd_b):
        k = pl.program_id(1)

        @pl.when(k == 0)
        def _():
            gst[...] = jnp.zeros_like(gst)
            lacc[...] = jnp.zeros_like(lacc)
            db_ref[...] = jnp.zeros_like(db_ref)
            dct_ref[...] = jnp.zeros_like(dct_ref)
            dd_ref[...] = jnp.zeros_like(dd_ref)
            dw_ref[...] = jnp.zeros_like(dw_ref)
            dbias_ref[...] = jnp.zeros_like(dbias_ref)

        hbufs, gbufs, rus, rds = (hbuf_a, hbuf_b), (gbuf_a, gbuf_b), (ru_a, ru_b), (rd_a, rd_b)
        for q in range(nseq):
            for e in range(2):
                rus[e][pl.ds(q, tc, stride=nseq), :] = u_ref[q, :, e * LANES:(e + 1) * LANES]
                rds[e][pl.ds(q, tc, stride=nseq), :] = dy_ref[q, :, e * LANES:(e + 1) * LANES]
        row = lax.broadcasted_iota(jnp.int32, (SUBLANES, s), 0)
        cols = [slice(j * cw, (j + 1) * cw) for j in range(ncol)]
        val = [{}, {}]

        def lam_of(e):
            return (jnp.broadcast_to(lam_ref[e, :, 0:s], (SUBLANES, s)),
                    jnp.broadcast_to(lam_ref[e, :, s:2 * s], (SUBLANES, s)))

        def project_in(e, j):
            hbufs[e][:, cols[j]] = jnp.dot(rus[e][...].astype(BF16), b_ref[e, :, cols[j]], preferred_element_type=F32)

        def scan_fwd(e, between):
            buf = hbufs[e]
            lr, li = lam_of(e)

            def step(i, carry):
                pr, pi = carry
                r0 = pl.multiple_of(i * SUBLANES, SUBLANES)
                br = buf[pl.ds(r0, SUBLANES), 0:s]
                bi = buf[pl.ds(r0, SUBLANES), s:2 * s]
                outr = outi = None
                for j in range(stages):
                    rr = _roll_rows(pr, nseq)
                    ri = _roll_rows(pi, nseq)
                    pr = lr * rr - li * ri + br
                    pi = lr * ri + li * rr + bi
                    outr = pr if j == 0 else jnp.where(row >= j * nseq, pr, outr)
                    outi = pi if j == 0 else jnp.where(row >= j * nseq, pi, outi)
                buf[pl.ds(r0, SUBLANES), 0:s] = outr
                buf[pl.ds(r0, SUBLANES), s:2 * s] = outi
                return outr, outi

            lo = e * 2 * s
            _scan_with(nblk, step, (hb_ref[:, lo:lo + s], hb_ref[:, lo + s:lo + 2 * s]), between)

        def y_part(e, j):
            p = jnp.dot(hbufs[e][:, cols[j]].astype(BF16), c_ref[e, cols[j], :], preferred_element_type=F32)
            val[e]["y"] = p if j == 0 else val[e]["y"] + p

        def gate(e):
            v = val[e]
            uu = rus[e][...]
            yg, dyg_dy = _gelu_and_grad(v.pop("y") + d_ref[e] * uu)
            yg16 = yg.astype(BF16)
            z = jnp.dot(yg16, w_ref[e], preferred_element_type=F32) + bias_ref[e]
            sg = _sigmoid(z[:, LANES:two])
            dout_e = rds[e][...]
            dz = jnp.concatenate([dout_e * sg, dout_e * z[:, 0:LANES] * sg * (1.0 - sg)], axis=-1)
            dz16 = dz.astype(BF16)
            dw_ref[e] += lax.dot_general(yg16, dz16, tn_dims, preferred_element_type=F32)
            dbias_ref[e] += jnp.sum(dz, axis=0, keepdims=True)
            dy = lax.dot_general(dz16, w_ref[e], nt_dims, preferred_element_type=F32) * dyg_dy
            dd_ref[e] += jnp.sum(dy * uu, axis=0, keepdims=True)
            v["dy"] = dy
            v["dy16"] = dy.astype(BF16)

        def dc_part(e, j):
            dct_ref[e, :, cols[j]] += lax.dot_general(val[e]["dy16"], hbufs[e][:, cols[j]].astype(BF16), tn_dims,
                                                      preferred_element_type=F32)

        def dh_part(e, j):
            gbufs[e][:, cols[j]] = lax.dot_general(val[e]["dy16"], c_ref[e, cols[j], :], nt_dims,
                                                   preferred_element_type=F32)

        def scan_bwd(e, between):
            hbuf, gbuf = hbufs[e], gbufs[e]
            lr, li = lam_of(e)
            lo = e * 2 * s

            def step(i, carry):
                pr, pi, ar, ai = carry
                blk = nblk - 1 - i
                r0 = pl.multiple_of(blk * SUBLANES, SUBLANES)
                dr = gbuf[pl.ds(r0, SUBLANES), 0:s]
                di = gbuf[pl.ds(r0, SUBLANES), s:2 * s]
                outr = outi = None
                for j in reversed(range(stages)):
                    rr = _roll_rows(pr, SUBLANES - nseq)
                    ri = _roll_rows(pi, SUBLANES - nseq)
                    pr = dr + lr * rr + li * ri
                    pi = di - li * rr + lr * ri
                    outr = pr if j == stages - 1 else jnp.where(row < (j + 1) * nseq, pr, outr)
                    outi = pi if j == stages - 1 else jnp.where(row < (j + 1) * nseq, pi, outi)
                gbuf[pl.ds(r0, SUBLANES), 0:s] = outr
                gbuf[pl.ds(r0, SUBLANES), s:2 * s] = outi
                p0 = pl.multiple_of(jnp.maximum(blk - 1, 0) * SUBLANES, SUBLANES)
                first = blk == 0
                before_r = jnp.where(first, hb_ref[:, lo:lo + s], hbuf[pl.ds(p0, SUBLANES), 0:s])
                before_i = jnp.where(first, hb_ref[:, lo + s:lo + 2 * s], hbuf[pl.ds(p0, SUBLANES), s:2 * s])
                if stages > 1:
                    last_rows = row >= SUBLANES - nseq
                    before_r = _roll_rows(jnp.where(last_rows, before_r, hbuf[pl.ds(r0, SUBLANES), 0:s]), nseq)
                    before_i = _roll_rows(jnp.where(last_rows, before_i, hbuf[pl.ds(r0, SUBLANES), s:2 * s]), nseq)
                return (outr, outi, ar + outr * before_r + outi * before_i, ai - outr * before_i + outi * before_r)

            gr, gi, ar, ai = _scan_with(
                nblk, step, (gst[:, lo:lo + s], gst[:, lo + s:lo + 2 * s], lacc[:, lo:lo + s], lacc[:, lo + s:lo + 2 * s]),
                between)
            gst[:, lo:lo + s] = gr
            gst[:, lo + s:lo + 2 * s] = gi
            lacc[:, lo:lo + s] = ar
            lacc[:, lo + s:lo + 2 * s] = ai

        def du_part(e, j):
            p = lax.dot_general(gbufs[e][:, cols[j]].astype(BF16), b_ref[e, :, cols[j]], nt_dims,
                                preferred_element_type=F32)
            val[e]["du"] = (val[e].pop("dy") * d_ref[e] + p) if j == 0 else val[e]["du"] + p

        def db_part(e, j):
            db_ref[e, :, cols[j]] += lax.dot_general(rus[e][...].astype(BF16), gbufs[e][:, cols[j]].astype(BF16),
                                                     tn_dims, preferred_element_type=F32)

        def parts(fn, e):
            return [functools.partial(fn, e, j) for j in range(ncol)]

        nothing = lambda: None
        middle_of = lambda e: parts(y_part, e) + [functools.partial(gate, e)] + parts(dc_part, e) + parts(dh_part, e)
        last_of = lambda e: parts(du_part, e) + parts(db_part, e)
        for piece in parts(project_in, 0):
            piece()
        scan_fwd(0, parts(project_in, 1))
        scan_fwd(1, middle_of(0) + [nothing] * 3)
        scan_bwd(0, middle_of(1) + [nothing] * 3)
        scan_bwd(1, last_of(0))
        for piece in last_of(1):
            piece()
        for e in range(2):
            rus[e][...] = val[e]["du"]
            for q in range(nseq):
                du_ref[q, :, e * LANES:(e + 1) * LANES] = rus[e][pl.ds(q, tc, stride=nseq), :]

        @pl.when(k == nk - 1)
        def _():
            for e in range(2):
                dlam_ref[e] = jnp.sum(lacc[:, e * 2 * s:(e + 1) * 2 * s], axis=0, keepdims=True)

    blk = lambda shape: pl.BlockSpec(shape, lambda b, k: (b, 0, 0))
    tok = pl.BlockSpec((nseq, tc, two), lambda b, k: (0, nk - 1 - k, b))
    outs, comm_outs = _call(
        body, name=name, grid=(nb // 2, nk),
        in_specs=[tok, tok, pl.BlockSpec((SUBLANES, 4 * s), lambda b, k: (nk - 1 - k, b)),
                  blk((2, 1, 2 * s)), blk((2, LANES, 2 * s)), blk((2, 2 * s, LANES)),
                  blk((2, 1, LANES)), blk((2, LANES, two)), blk((2, 1, two))],
        out_specs=[tok, blk((2, 1, 2 * s)), blk((2, LANES, 2 * s)), blk((2, LANES, 2 * s)),
                   blk((2, 1, LANES)), blk((2, LANES, two)), blk((2, 1, two))],
        out_shape=[jax.ShapeDtypeStruct((nseq, seq, w), F32),
                   jax.ShapeDtypeStruct((nb, 1, 2 * s), F32), jax.ShapeDtypeStruct((nb, LANES, 2 * s), F32),
                   jax.ShapeDtypeStruct((nb, LANES, 2 * s), F32), jax.ShapeDtypeStruct((nb, 1, LANES), F32),
                   jax.ShapeDtypeStruct((nb, LANES, two), F32), jax.ShapeDtypeStruct((nb, 1, two), F32)],
        scratch_shapes=[pltpu.VMEM((rows, 2 * s), F32)] * 4
        + [pltpu.VMEM((SUBLANES, 4 * s), F32), pltpu.VMEM((SUBLANES, 4 * s), F32)]
        + [pltpu.VMEM((rows, LANES), F32)] * 4,
        semantics=("parallel", "arbitrary"),
        args=(u.reshape(nseq, seq, w), dout.reshape(nseq, seq, w), hb, lam, b_bd, c_bd, dsk, w_bd, bias), comm=comm)
    outs[0] = outs[0].reshape(nseq * seq, w)
    return outs if comm is None else (outs, comm_outs)


def _ssm_bwd(u, dout, hb, ops, nseq, name, comm=None):
    lam, b_bd, c_bd, dsk, w_bd, bias = ops
    rows_total, w = u.shape
    seq = rows_total // nseq
    nb = w // LANES
    s = STATES_PER_BLOCK
    tc = _scan_chunk_rows(seq, nseq)
    nk = seq // tc
    rows = tc * nseq
    nblk = rows // SUBLANES
    stages = SUBLANES // nseq
    tn_dims = (((0,), (0,)), ((), ()))
    nt_dims = (((1,), (1,)), ((), ()))

    def body(u_ref, dy_ref, hb_ref, lam_ref, b_ref, c_ref, d_ref, w_ref, bias_ref,
             du_ref, dlam_ref, db_ref, dct_ref, dd_ref, dw_ref, dbias_ref, hbuf, gbuf, gst, lacc, rbuf, rbuf2):
        k = pl.program_id(1)

        @pl.when(k == 0)
        def _():
            gst[...] = jnp.zeros_like(gst)
            lacc[...] = jnp.zeros_like(lacc)
            db_ref[...] = jnp.zeros_like(db_ref)
            dct_ref[...] = jnp.zeros_like(dct_ref)
            dd_ref[...] = jnp.zeros_like(dd_ref)
            dw_ref[...] = jnp.zeros_like(dw_ref)
            dbias_ref[...] = jnp.zeros_like(dbias_ref)

        for q in range(nseq):
            rbuf[pl.ds(q, tc, stride=nseq), :] = u_ref[q]
            rbuf2[pl.ds(q, tc, stride=nseq), :] = dy_ref[q]
        u = rbuf[...]
        ub = u.astype(BF16)
        lr = jnp.broadcast_to(lam_ref[0, :, 0:s], (SUBLANES, s))
        li = jnp.broadcast_to(lam_ref[0, :, s:2 * s], (SUBLANES, s))
        row = lax.broadcasted_iota(jnp.int32, (SUBLANES, s), 0)
        hbuf[...] = jnp.dot(ub, b_ref[0], preferred_element_type=F32)

        def fstep(i, carry):
            pr, pi = carry
            r0 = pl.multiple_of(i * SUBLANES, SUBLANES)
            br = hbuf[pl.ds(r0, SUBLANES), 0:s]
            bi = hbuf[pl.ds(r0, SUBLANES), s:2 * s]
            outr = outi = None
            for j in range(stages):
                rr = _roll_rows(pr, nseq)
                ri = _roll_rows(pi, nseq)
                pr = lr * rr - li * ri + br
                pi = lr * ri + li * rr + bi
                outr = pr if j == 0 else jnp.where(row >= j * nseq, pr, outr)
                outi = pi if j == 0 else jnp.where(row >= j * nseq, pi, outi)
            hbuf[pl.ds(r0, SUBLANES), 0:s] = outr
            hbuf[pl.ds(r0, SUBLANES), s:2 * s] = outi
            return outr, outi

        lax.fori_loop(0, nblk, fstep, (hb_ref[:, 0:s], hb_ref[:, s:2 * s]), unroll=2)
        hb16 = hbuf[...].astype(BF16)
        y = jnp.dot(hb16, c_ref[0], preferred_element_type=F32) + d_ref[0] * u
        yg, dyg_dy = _gelu_and_grad(y)
        yg16 = yg.astype(BF16)
        z = jnp.dot(yg16, w_ref[0], preferred_element_type=F32) + bias_ref[0]
        z1 = z[:, 0:LANES]
        sg = _sigmoid(z[:, LANES:2 * LANES])
        dout = rbuf2[...]
        dz = jnp.concatenate([dout * sg, dout * z1 * sg * (1.0 - sg)], axis=-1)
        dz16 = dz.astype(BF16)
        dw_ref[0] += lax.dot_general(yg16, dz16, tn_dims, preferred_element_type=F32)
        dbias_ref[0] += jnp.sum(dz, axis=0, keepdims=True)
        dy = lax.dot_general(dz16, w_ref[0], nt_dims, preferred_element_type=F32) * dyg_dy
        dy16 = dy.astype(BF16)
        dd_ref[0] += jnp.sum(dy * u, axis=0, keepdims=True)
        dct_ref[0] += lax.dot_general(dy16, hb16, tn_dims, preferred_element_type=F32)
        gbuf[...] = lax.dot_general(dy16, c_ref[0], nt_dims, preferred_element_type=F32)

        def bstep(i, carry):
            pr, pi, ar, ai = carry
            blk = nblk - 1 - i
            r0 = pl.multiple_of(blk * SUBLANES, SUBLANES)
            dr = gbuf[pl.ds(r0, SUBLANES), 0:s]
            di = gbuf[pl.ds(r0, SUBLANES), s:2 * s]
            outr = outi = None
            for j in reversed(range(stages)):
                rr = _roll_rows(pr, SUBLANES - nseq)
                ri = _roll_rows(pi, SUBLANES - nseq)
                pr = dr + lr * rr + li * ri
                pi = di - li * rr + lr * ri
                outr = pr if j == stages - 1 else jnp.where(row < (j + 1) * nseq, pr, outr)
                outi = pi if j == stages - 1 else jnp.where(row < (j + 1) * nseq, pi, outi)
            gbuf[pl.ds(r0, SUBLANES), 0:s] = outr
            gbuf[pl.ds(r0, SUBLANES), s:2 * s] = outi
            p0 = pl.multiple_of(jnp.maximum(blk - 1, 0) * SUBLANES, SUBLANES)
            first = blk == 0
            before_r = jnp.where(first, hb_ref[:, 0:s], hbuf[pl.ds(p0, SUBLANES), 0:s])
            before_i = jnp.where(first, hb_ref[:, s:2 * s], hbuf[pl.ds(p0, SUBLANES), s:2 * s])
            if stages > 1:
                last_rows = row >= SUBLANES - nseq
                before_r = _roll_rows(jnp.where(last_rows, before_r, hbuf[pl.ds(r0, SUBLANES), 0:s]), nseq)
                before_i = _roll_rows(jnp.where(last_rows, before_i, hbuf[pl.ds(r0, SUBLANES), s:2 * s]), nseq)
            return (outr, outi, ar + outr * before_r + outi * before_i, ai - outr * before_i + outi * before_r)

        gr, gi, ar, ai = lax.fori_loop(
            0, nblk, bstep, (gst[:, 0:s], gst[:, s:2 * s], lacc[:, 0:s], lacc[:, s:2 * s]))
        gst[:, 0:s] = gr
        gst[:, s:2 * s] = gi
        lacc[:, 0:s] = ar
        lacc[:, s:2 * s] = ai
        g16 = gbuf[...].astype(BF16)
        rbuf[...] = dy * d_ref[0] + lax.dot_general(g16, b_ref[0], nt_dims, preferred_element_type=F32)
        for q in range(nseq):
            du_ref[q] = rbuf[pl.ds(q, tc, stride=nseq), :]
        db_ref[0] += lax.dot_general(ub, g16, tn_dims, preferred_element_type=F32)

        @pl.when(k == nk - 1)
        def _():
            dlam_ref[0] = jnp.sum(lacc[...], axis=0, keepdims=True)

    blk = lambda shape: pl.BlockSpec(shape, lambda b, k: (b, 0, 0))
    rev = lambda b, k: (nk - 1 - k, b)
    tok = pl.BlockSpec((nseq, tc, LANES), lambda b, k: (0, nk - 1 - k, b))
    outs, comm_outs = _call(
        body, name=name, grid=(nb, nk),
        in_specs=[tok, tok, pl.BlockSpec((SUBLANES, 2 * s), rev),
                  blk((1, 1, 2 * s)), blk((1, LANES, 2 * s)), blk((1, 2 * s, LANES)),
                  blk((1, 1, LANES)), blk((1, LANES, 2 * LANES)), blk((1, 1, 2 * LANES))],
        out_specs=[tok, blk((1, 1, 2 * s)), blk((1, LANES, 2 * s)), blk((1, LANES, 2 * s)),
                   blk((1, 1, LANES)), blk((1, LANES, 2 * LANES)), blk((1, 1, 2 * LANES))],
        out_shape=[jax.ShapeDtypeStruct((nseq, seq, w), F32),
                   jax.ShapeDtypeStruct((nb, 1, 2 * s), F32), jax.ShapeDtypeStruct((nb, LANES, 2 * s), F32),
                   jax.ShapeDtypeStruct((nb, LANES, 2 * s), F32), jax.ShapeDtypeStruct((nb, 1, LANES), F32),
                   jax.ShapeDtypeStruct((nb, LANES, 2 * LANES), F32), jax.ShapeDtypeStruct((nb, 1, 2 * LANES), F32)],
        scratch_shapes=[pltpu.VMEM((rows, 2 * s), F32), pltpu.VMEM((rows, 2 * s), F32),
                        pltpu.VMEM((SUBLANES, 2 * s), F32), pltpu.VMEM((SUBLANES, 2 * s), F32),
                        pltpu.VMEM((rows, LANES), F32), pltpu.VMEM((rows, LANES), F32)],
        semantics=("parallel", "arbitrary"),
        args=(u.reshape(nseq, seq, w), dout.reshape(nseq, seq, w), hb, lam, b_bd, c_bd, dsk, w_bd, bias), comm=comm)
    outs[0] = outs[0].reshape(nseq * seq, w)
    return outs if comm is None else (outs, comm_outs)


ANY = pl.BlockSpec(memory_space=pl.ANY)

BIG = (("ffn1_w_in", True), ("ffn1_w_out", False), ("mix_w_in", True), ("mix_w_out", False),
       ("ffn2_w_in", True), ("ffn2_w_out", False))


def _my_place():
    return lax.axis_index("x"), lax.axis_index("y"), lax.axis_index("c")


def _other_chips(x, y):
    return [(1 - x, y), (x, 1 - y), (1 - x, 1 - y)]


def _half_of_shard(ref, col_sharded, chip, core):
    full_rows, full_cols = ref.shape
    if col_sharded:
        hr, cs = full_rows // 2, full_cols // N_CHIPS
        return ref.at[pl.ds(pl.multiple_of(core * hr, 8), hr), pl.ds(chip * cs, cs)]
    rs = full_rows // N_CHIPS
    return ref.at[pl.ds(pl.multiple_of(chip * rs + core * (rs // 2), 8), rs // 2), :]


def _gather_comm(shards, cols):
    full_shapes = [(sh.shape[0], sh.shape[1] * N_CHIPS) if col else (sh.shape[0] * N_CHIPS, sh.shape[1])
                   for sh, col in zip(shards, cols)]
    nw = len(shards)

    def first_copies(ins, outs, sems):
        send_sems, recv_sems, local_sems = sems
        x, y, c = _my_place()
        me = 2 * x + y
        locals_, sends = [], []
        for wi in range(nw):
            src, dst = ins[wi], outs[wi]
            rs, cs = src.shape
            hs = rs // 2
            if cols[wi]:
                place = dst.at[:, pl.ds(me * cs, cs)]
            else:
                place = dst.at[pl.ds(pl.multiple_of(me * rs, 8), rs), :]
            locals_.append(pltpu.make_async_copy(src, place, local_sems.at[wi]))
            my_half = src.at[pl.ds(pl.multiple_of(c * hs, 8), hs), :]
            for j, (px, py) in enumerate(_other_chips(x, y)):
                sends.append(pltpu.make_async_remote_copy(
                    src_ref=my_half, dst_ref=_half_of_shard(dst, cols[wi], me, c),
                    send_sem=send_sems.at[wi * 6 + j], recv_sem=recv_sems.at[wi * 6 + j],
                    device_id=(px, py, c), device_id_type=MESH))
        return locals_, sends

    def start(ins, outs, sems):
        locals_, sends = first_copies(ins, outs, sems)
        for cp in locals_ + sends:
            cp.start()

    def forwards(outs, sems, wait_landed):
        send_sems, recv_sems, _ = sems
        x, y, c = _my_place()
        out = []
        for wi in range(nw):
            dst = outs[wi]
            for j, (px, py) in enumerate(_other_chips(x, y)):
                got = _half_of_shard(dst, cols[wi], 2 * px + py, c)
                if wait_landed:
                    pltpu.make_async_remote_copy(
                        src_ref=got, dst_ref=got, send_sem=send_sems.at[wi * 6 + j], recv_sem=recv_sems.at[wi * 6 + j],
                        device_id=(px, py, c), device_id_type=MESH).wait_recv()
                out.append(pltpu.make_async_remote_copy(
                    src_ref=got, dst_ref=got, send_sem=send_sems.at[wi * 6 + 3 + j], recv_sem=recv_sems.at[wi * 6 + 3 + j],
                    device_id=(x, y, 1 - c), device_id_type=MESH))
                if wait_landed:
                    out[-1].start()
        return out

    def middle(ins, outs, sems):
        forwards(outs, sems, True)

    def finish(ins, outs, sems):
        send_sems, recv_sems, _ = sems
        x, y, c = _my_place()
        locals_, sends = first_copies(ins, outs, sems)
        for wi in range(nw):
            dst = outs[wi]
            for j, (px, py) in enumerate(_other_chips(x, y)):
                theirs = _half_of_shard(dst, cols[wi], 2 * px + py, 1 - c)
                pltpu.make_async_remote_copy(
                    src_ref=theirs, dst_ref=theirs, send_sem=send_sems.at[wi * 6 + 3 + j],
                    recv_sem=recv_sems.at[wi * 6 + 3 + j], device_id=(x, y, 1 - c), device_id_type=MESH).wait_recv()
        for cp in sends + forwards(outs, sems, False):
            cp.wait_send()
        for cp in locals_:
            cp.wait()

    return _Comm(shards, [jax.ShapeDtypeStruct(s, BF16) for s in full_shapes],
                 [pltpu.SemaphoreType.DMA((6 * nw,)), pltpu.SemaphoreType.DMA((6 * nw,)),
                  pltpu.SemaphoreType.DMA((nw,))], start, finish, middle=middle)


def _pair_exchange_comm(grads, cols):
    nw = len(grads)
    n_copies = sum(1 if col else N_CHIPS for col in cols)

    def copies(ins, outs, sems):
        send_sems, recv_sems = sems
        x, y, c = _my_place()
        out = []
        for wi in range(nw):
            src, dst = ins[wi], outs[wi]
            fr = src.shape[0]
            if cols[wi]:
                hr = fr // 2
                pieces = [(src.at[pl.ds(pl.multiple_of((1 - c) * hr, 8), hr), :], dst)]
            else:
                rs = fr // N_CHIPS
                hs = rs // 2
                pieces = [(src.at[pl.ds(pl.multiple_of(k * rs + (1 - c) * hs, 8), hs), :],
                           dst.at[pl.ds(k * hs, hs), :]) for k in range(N_CHIPS)]
            for s_ref, d_ref in pieces:
                out.append(pltpu.make_async_remote_copy(
                    src_ref=s_ref, dst_ref=d_ref, send_sem=send_sems.at[len(out)], recv_sem=recv_sems.at[len(out)],
                    device_id=(x, y, 1 - c), device_id_type=MESH))
        return out

    def start(ins, outs, sems):
        for cp in copies(ins, outs, sems):
            cp.start()

    def finish(ins, outs, sems):
        for cp in copies(ins, outs, sems):
            cp.wait()

    return _Comm(grads, [jax.ShapeDtypeStruct((g.shape[0] // 2, g.shape[1]), F32) for g in grads],
                 [pltpu.SemaphoreType.DMA((n_copies,)), pltpu.SemaphoreType.DMA((n_copies,))], start, finish)


def _pair_sum(grad, other, col, core, name):
    fr, fc = grad.shape
    pieces = 1 if col else N_CHIPS
    pr = fr // 2 // pieces
    gview = grad.reshape(pieces * 2, pr, fc)
    oview = other.reshape(pieces, pr, fc)
    tr = _tile(pr, 256, 16)

    def body(c_ref, g_ref, o_ref, out_ref):
        out_ref[...] = (g_ref[...] + o_ref[...]).astype(out_ref.dtype)

    out = pl.pallas_call(
        body, name=name,
        grid_spec=pltpu.PrefetchScalarGridSpec(
            num_scalar_prefetch=1, grid=(pieces, pr // tr),
            in_specs=[pl.BlockSpec((1, tr, fc), lambda p, i, cref: (p * 2 + cref[0], i, 0)),
                      pl.BlockSpec((1, tr, fc), lambda p, i, cref: (p, i, 0))],
            out_specs=pl.BlockSpec((1, tr, fc), lambda p, i, cref: (p, i, 0))),
        out_shape=jax.ShapeDtypeStruct((pieces, pr, fc), BF16),
        compiler_params=_params("parallel", "parallel"),
    )(core, gview, oview)
    return out.reshape(fr // 2, fc)


def _chip_exchange_comm(psums, cols):
    nw = len(psums)
    out_shapes = [(N_CHIPS, p.shape[0], p.shape[1] // N_CHIPS) if col else (N_CHIPS, p.shape[0] // N_CHIPS, p.shape[1])
                  for p, col in zip(psums, cols)]

    def copies(ins, outs, sems):
        send_sems, recv_sems, local_sems = sems
        x, y, c = _my_place()
        me = 2 * x + y
        out = []
        for wi in range(nw):
            src = ins[wi]
            mine = outs[wi].at[me]

            def piece(chip, src=src, col=cols[wi]):
                if col:
                    cs = src.shape[1] // N_CHIPS
                    return src.at[:, pl.ds(chip * cs, cs)]
                ps = src.shape[0] // N_CHIPS
                return src.at[pl.ds(pl.multiple_of(chip * ps, 8), ps), :]

            out.append(pltpu.make_async_copy(piece(me), mine, local_sems.at[wi]))
            for j, (px, py) in enumerate(_other_chips(x, y)):
                out.append(pltpu.make_async_remote_copy(
                    src_ref=piece(2 * px + py), dst_ref=mine,
                    send_sem=send_sems.at[wi * 3 + j], recv_sem=recv_sems.at[wi * 3 + j],
                    device_id=(px, py, c), device_id_type=MESH))
        return out

    def start(ins, outs, sems):
        for cp in copies(ins, outs, sems):
            cp.start()

    def finish(ins, outs, sems):
        for cp in copies(ins, outs, sems):
            cp.wait()

    return _Comm(psums, [jax.ShapeDtypeStruct(s, BF16) for s in out_shapes],
                 [pltpu.SemaphoreType.DMA((3 * nw,)), pltpu.SemaphoreType.DMA((3 * nw,)),
                  pltpu.SemaphoreType.DMA((nw,))], start, finish)


def _chip_sum(slots, core, layer, layers, into, name):
    _, hr, cs = slots.shape
    tr = _tile(hr, 256, 16)

    def body(c_ref, s_ref, *rest):
        out_ref = rest[-1]
        acc = s_ref[0].astype(F32)
        for i in range(1, N_CHIPS):
            acc = acc + s_ref[i].astype(F32)
        out_ref[0] = acc

    in_specs = [pl.BlockSpec((N_CHIPS, tr, cs), lambda i, cref: (0, i, 0))]
    args = [core, slots]
    aliases = {}
    if into is not None:
        in_specs.append(pl.BlockSpec(memory_space=pl.ANY))
        args.append(into.reshape(layers * 2, hr, cs))
        aliases = {2: 0}
    out = pl.pallas_call(
        body, name=name,
        grid_spec=pltpu.PrefetchScalarGridSpec(
            num_scalar_prefetch=1, grid=(hr // tr,), in_specs=in_specs,
            out_specs=pl.BlockSpec((1, tr, cs), lambda i, cref: (layer * 2 + cref[0], i, 0))),
        out_shape=jax.ShapeDtypeStruct((layers * 2, hr, cs), F32),
        input_output_aliases=aliases,
        compiler_params=_params("parallel"),
    )(*args)
    return out.reshape(layers, 2 * hr, cs)


def _pair_share_comm(reduced):
    nw = len(reduced)

    def copies(ins, outs, sems):
        send_sems, recv_sems = sems
        x, y, c = _my_place()
        out = []
        for wi in range(nw):
            hs = outs[wi].shape[1] // 2
            mine = outs[wi].at[:, pl.ds(pl.multiple_of(c * hs, 8), hs), :]
            out.append(pltpu.make_async_remote_copy(
                src_ref=mine, dst_ref=mine, send_sem=send_sems.at[wi], recv_sem=recv_sems.at[wi],
                device_id=(x, y, 1 - c), device_id_type=MESH))
        return out

    def start(ins, outs, sems):
        for cp in copies(ins, outs, sems):
            cp.start()

    def finish(ins, outs, sems):
        for cp in copies(ins, outs, sems):
            cp.wait()

    return _Comm(reduced, [jax.ShapeDtypeStruct(r.shape, F32) for r in reduced],
                 [pltpu.SemaphoreType.DMA((nw,)), pltpu.SemaphoreType.DMA((nw,))], start, finish,
                 alias={i: i for i in range(nw)})


def _all_reduce_small(flat, comm):
    rows, lanes = flat.shape
    seg = rows // N_DEV
    c_in, c_out = len(comm.ins), len(comm.outs)

    def body(*refs):
        refs = list(refs)
        in_ref, cins = refs[0], refs[1:1 + c_in]
        out_ref, couts = refs[1 + c_in], refs[2 + c_in:2 + c_in + c_out]
        recv_ref, send_sems, recv_sems = refs[2 + c_in + c_out:5 + c_in + c_out]
        csems = refs[5 + c_in + c_out:]
        comm.start(cins, couts, csems)
        x, y, c = _my_place()
        me = 4 * x + 2 * y + c

        def peer(r):
            fx, fy, fc = (r >> 2) & 1, (r >> 1) & 1, r & 1
            px = jnp.where(fx == 1, 1 - x, x)
            py = jnp.where(fy == 1, 1 - y, y)
            pc = jnp.where(fc == 1, 1 - c, c)
            return px, py, pc

        first = []
        for r in range(1, N_DEV):
            px, py, pc = peer(r)
            theirs = in_ref.at[pl.ds(pl.multiple_of((4 * px + 2 * py + pc) * seg, 8), seg), :]
            cp = pltpu.make_async_remote_copy(
                src_ref=theirs, dst_ref=recv_ref.at[r], send_sem=send_sems.at[r - 1], recv_sem=recv_sems.at[r - 1],
                device_id=(px, py, pc), device_id_type=MESH)
            cp.start()
            first.append(cp)
        for cp in first:
            cp.wait()
        my_rows = pl.ds(pl.multiple_of(me * seg, 8), seg)
        acc = in_ref[my_rows, :]
        for r in range(1, N_DEV):
            acc = acc + recv_ref[r]
        out_ref[my_rows, :] = acc
        second = []
        for r in range(1, N_DEV):
            px, py, pc = peer(r)
            cp = pltpu.make_async_remote_copy(
                src_ref=out_ref.at[my_rows, :], dst_ref=out_ref.at[my_rows, :],
                send_sem=send_sems.at[6 + r], recv_sem=recv_sems.at[6 + r],
                device_id=(px, py, pc), device_id_type=MESH)
            cp.start()
            second.append(cp)
        for r in range(1, N_DEV):
            px, py, pc = peer(r)
            theirs = out_ref.at[pl.ds(pl.multiple_of((4 * px + 2 * py + pc) * seg, 8), seg), :]
            pltpu.make_async_remote_copy(
                src_ref=theirs, dst_ref=theirs, send_sem=send_sems.at[6 + r], recv_sem=recv_sems.at[6 + r],
                device_id=(px, py, pc), device_id_type=MESH).wait_recv()
        for cp in second:
            cp.wait_send()
        comm.finish(cins, couts, csems)

    vm = pl.BlockSpec(memory_space=pltpu.VMEM)
    any_spec = pl.BlockSpec(memory_space=pl.ANY)
    outs = pl.pallas_call(
        body, name="all_reduce_small",
        in_specs=[vm] + [any_spec] * c_in, out_specs=[vm] + [any_spec] * c_out,
        out_shape=[jax.ShapeDtypeStruct((rows, lanes), F32)] + comm.outs,
        scratch_shapes=[pltpu.VMEM((N_DEV, seg, lanes), F32),
                        pltpu.SemaphoreType.DMA((2 * (N_DEV - 1),)), pltpu.SemaphoreType.DMA((2 * (N_DEV - 1),))]
        + comm.sems,
        input_output_aliases={1 + ci: 1 + co for ci, co in comm.alias.items()},
        compiler_params=pltpu.CompilerParams(vmem_limit_bytes=VMEM_LIMIT),
    )(flat, *comm.ins)
    return outs[0], list(outs[1:])


def _adamw_update(w_ref, g_ref, m_ref, v_ref, d_ref, nm_ref, nv_ref):
    c1 = 1.0 - ADAM_B1 ** ADAM_STEP
    c2 = 1.0 - ADAM_B2 ** ADAM_STEP
    gv = g_ref[...]
    nm = ADAM_B1 * m_ref[...] + (1.0 - ADAM_B1) * gv
    nv = ADAM_B2 * v_ref[...] + (1.0 - ADAM_B2) * (gv * gv)
    d_ref[...] = -ADAM_LR * ((nm / c1) / (jnp.sqrt(nv / c2) + ADAM_EPS) + ADAM_WD * w_ref[...])
    nm_ref[...] = nm
    nv_ref[...] = nv


def _adamw_many(ws, gs, ms, vs, name):
    n = len(ws)

    def body(*refs):
        for i in range(n):
            _adamw_update(*[refs[k * n + i] for k in range(7)])

    shapes = [jax.ShapeDtypeStruct(w.shape, F32) for w in ws]
    outs = pl.pallas_call(
        body, name=name, out_shape=shapes * 3,
        compiler_params=pltpu.CompilerParams(vmem_limit_bytes=VMEM_LIMIT),
    )(*ws, *gs, *ms, *vs)
    return outs[:n], outs[n:2 * n], outs[2 * n:]


def _adamw(w, g, m, v, name):
    rows, cols = w.shape
    tr = _tile(rows, 256, 8)

    def body(w_ref, g_ref, m_ref, v_ref, go_ref, d_ref, nm_ref, nv_ref):
        go_ref[...] = g_ref[...]
        _adamw_update(w_ref, g_ref, m_ref, v_ref, d_ref, nm_ref, nv_ref)

    blk = pl.BlockSpec((tr, cols), lambda i: (i, 0))
    sds = jax.ShapeDtypeStruct((rows, cols), F32)
    return pl.pallas_call(
        body, name=name, grid=(rows // tr,),
        in_specs=[blk] * 4, out_specs=[blk] * 4, out_shape=[sds] * 4,
        compiler_params=_params("parallel"),
    )(w, g, m, v)


SMALL = ("norm_ffn1", "norm_mix", "ssm_a_re", "ssm_a_im", "ssm_log_dt", "ssm_b_re", "ssm_b_im", "ssm_c_re",
         "ssm_c_im", "ssm_d", "ssm_glu_w", "ssm_glu_b", "gm_v_gain", "gm_w_s", "gm_b_s", "gain_ssm_out",
         "gain_gm_out", "norm_ffn2", "norm_final")
WEIGHTS = ("norm_ffn1", "ffn1_w_in", "ffn1_w_out", "norm_mix", "mix_w_in", "ssm_a_re", "ssm_a_im", "ssm_log_dt",
           "ssm_b_re", "ssm_b_im", "ssm_c_re", "ssm_c_im", "ssm_d", "ssm_glu_w", "ssm_glu_b", "gm_v_gain", "gm_w_s",
           "gm_b_s", "gain_ssm_out", "gain_gm_out", "mix_w_out", "norm_ffn2", "ffn2_w_in", "ffn2_w_out", "norm_final")


def _ffn_fwd(x, gain, w_in, w_out, tag, hosted=None):
    if hosted is None:
        h, t, q, a = _ffn_in_fwd(x, gain, w_in, f"{tag}_in")
    else:
        (h, t, q, a), got = _ffn_in_fwd(x, gain, w_in, f"{tag}_in_hosting", comm=hosted[0]())
        hosted[1](got)
    if callable(w_out):
        w_out = w_out()
    out = _matmul(a, w_out, "nn", scale=0.5, res=x, tm=512, tn=1024, tk=4096, name=f"{tag}_out")
    return out, (x, h, t, q, a)


def _ffn_bwd(dout, saved, gain, w_in, w_out, tag, hooks=None, publish=None):
    x, h, t, q, a = saved
    f = t.shape[1]
    hooks = hooks or {}

    def hosted(key, fn, *args, name, **kw):
        if key not in hooks:
            return fn(*args, name=name, **kw)
        make, take = hooks[key]
        *res, got = fn(*args, name=f"{name}_hosting", comm=make(), **kw)
        take(got)
        return res[0] if len(res) == 1 else tuple(res)

    dg, du = hosted("out_dx", _ffn_out_bwd, dout, w_out, t, q, name=f"{tag}_out_dx")
    dw_out = hosted("out_dw", _matmul, a, dout, "tn", scale=0.5, tm=1536, tn=1024, tk=2048, name=f"{tag}_out_dw")
    if publish is not None:
        publish("out", dw_out)
    dw_in = hosted("in_dw_g", _matmul, h, dg, "tn", tm=1024, tn=1536, tk=2048, name=f"{tag}_in_dw_g",
                   out_cols=2 * f)
    dw_in = hosted("in_dw_u", _matmul, h, du, "tn", tm=1024, tn=1536, tk=2048, name=f"{tag}_in_dw_u",
                   out_cols=2 * f, col_off=f, into=dw_in)
    if publish is not None:
        publish("in", dw_in)
    dx, dgain = hosted("in_dx", _proj_in_bwd, [(dg, 0), (du, f)], w_in, x, gain, dout, name=f"{tag}_in_dx")
    return dx, dgain, dw_in, dw_out


def kernel(x, norm_ffn1, ffn1_w_in, ffn1_w_out, norm_mix, mix_w_in, ssm_a_re, ssm_a_im, ssm_log_dt, ssm_b_re, ssm_b_im, ssm_c_re, ssm_c_im, ssm_d, ssm_glu_w, ssm_glu_b, gm_v_gain, gm_w_s, gm_b_s, gain_ssm_out, gain_gm_out, mix_w_out, norm_ffn2, ffn2_w_in, ffn2_w_out, norm_final, loss_target, m_norm_ffn1, m_ffn1_w_in, m_ffn1_w_out, m_norm_mix, m_mix_w_in, m_ssm_a_re, m_ssm_a_im, m_ssm_log_dt, m_ssm_b_re, m_ssm_b_im, m_ssm_c_re, m_ssm_c_im, m_ssm_d, m_ssm_glu_w, m_ssm_glu_b, m_gm_v_gain, m_gm_w_s, m_gm_b_s, m_gain_ssm_out, m_gain_gm_out, m_mix_w_out, m_norm_ffn2, m_ffn2_w_in, m_ffn2_w_out, m_norm_final, v_norm_ffn1, v_ffn1_w_in, v_ffn1_w_out, v_norm_mix, v_mix_w_in, v_ssm_a_re, v_ssm_a_im, v_ssm_log_dt, v_ssm_b_re, v_ssm_b_im, v_ssm_c_re, v_ssm_c_im, v_ssm_d, v_ssm_glu_w, v_ssm_glu_b, v_gm_v_gain, v_gm_w_s, v_gm_b_s, v_gain_ssm_out, v_gain_gm_out, v_mix_w_out, v_norm_ffn2, v_ffn2_w_in, v_ffn2_w_out, v_norm_final):
    wts = dict(norm_ffn1=norm_ffn1, ffn1_w_in=ffn1_w_in, ffn1_w_out=ffn1_w_out, norm_mix=norm_mix, mix_w_in=mix_w_in,
               ssm_a_re=ssm_a_re, ssm_a_im=ssm_a_im, ssm_log_dt=ssm_log_dt, ssm_b_re=ssm_b_re, ssm_b_im=ssm_b_im,
               ssm_c_re=ssm_c_re, ssm_c_im=ssm_c_im, ssm_d=ssm_d, ssm_glu_w=ssm_glu_w, ssm_glu_b=ssm_glu_b,
               gm_v_gain=gm_v_gain, gm_w_s=gm_w_s, gm_b_s=gm_b_s, gain_ssm_out=gain_ssm_out, gain_gm_out=gain_gm_out,
               mix_w_out=mix_w_out, norm_ffn2=norm_ffn2, ffn2_w_in=ffn2_w_in, ffn2_w_out=ffn2_w_out,
               norm_final=norm_final)
    mom = dict(norm_ffn1=m_norm_ffn1, ffn1_w_in=m_ffn1_w_in, ffn1_w_out=m_ffn1_w_out, norm_mix=m_norm_mix,
               mix_w_in=m_mix_w_in, ssm_a_re=m_ssm_a_re, ssm_a_im=m_ssm_a_im, ssm_log_dt=m_ssm_log_dt,
               ssm_b_re=m_ssm_b_re, ssm_b_im=m_ssm_b_im, ssm_c_re=m_ssm_c_re, ssm_c_im=m_ssm_c_im, ssm_d=m_ssm_d,
               ssm_glu_w=m_ssm_glu_w, ssm_glu_b=m_ssm_glu_b, gm_v_gain=m_gm_v_gain, gm_w_s=m_gm_w_s, gm_b_s=m_gm_b_s,
               gain_ssm_out=m_gain_ssm_out, gain_gm_out=m_gain_gm_out, mix_w_out=m_mix_w_out, norm_ffn2=m_norm_ffn2,
               ffn2_w_in=m_ffn2_w_in, ffn2_w_out=m_ffn2_w_out, norm_final=m_norm_final)
    var = dict(norm_ffn1=v_norm_ffn1, ffn1_w_in=v_ffn1_w_in, ffn1_w_out=v_ffn1_w_out, norm_mix=v_norm_mix,
               mix_w_in=v_mix_w_in, ssm_a_re=v_ssm_a_re, ssm_a_im=v_ssm_a_im, ssm_log_dt=v_ssm_log_dt,
               ssm_b_re=v_ssm_b_re, ssm_b_im=v_ssm_b_im, ssm_c_re=v_ssm_c_re, ssm_c_im=v_ssm_c_im, ssm_d=v_ssm_d,
               ssm_glu_w=v_ssm_glu_w, ssm_glu_b=v_ssm_glu_b, gm_v_gain=v_gm_v_gain, gm_w_s=v_gm_w_s, gm_b_s=v_gm_b_s,
               gain_ssm_out=v_gain_ssm_out, gain_gm_out=v_gain_gm_out, mix_w_out=v_mix_w_out, norm_ffn2=v_norm_ffn2,
               ffn2_w_in=v_ffn2_w_in, ffn2_w_out=v_ffn2_w_out, norm_final=v_norm_final)

    nseq, seq, d = x.shape
    n = nseq * seq
    depth = norm_ffn1.shape[0]
    width = gain_ssm_out.shape[1]
    groups = ssm_a_re.shape[1]
    heads = gm_w_s.shape[1]
    core = lax.axis_index("c").astype(jnp.int32).reshape(1)

    is_col = dict(BIG)
    full = {name: [None] * depth for name, _ in BIG}

    def gather_comm(pairs):
        return _gather_comm([wts[nm][l].astype(BF16) for nm, l in pairs], [is_col[nm] for nm, _ in pairs])

    def store(pairs, arrays):
        for (nm, l), w in zip(pairs, arrays):
            full[nm][l] = w

    pairs = [("ffn1_w_in", 0)]
    store(pairs, _run_comm(gather_comm(pairs), "all_gather_first"))

    xs = x.reshape(n, d)
    saved = []
    for l in range(depth):
        pairs = [("ffn1_w_out", l)] + ([("mix_w_in", l), ("mix_w_out", l)] if l == 0 else [])
        x1, s_ffn1 = _ffn_fwd(xs, norm_ffn1[l], full["ffn1_w_in"][l], lambda l=l: full["ffn1_w_out"][l], "ffn1",
                              hosted=(functools.partial(gather_comm, pairs), functools.partial(store, pairs)))
        hm, u_ssm, zgm = _mix_in_fwd(x1, norm_mix[l], full["mix_w_in"][l], width, "mix_in")
        bt_re = jnp.swapaxes(ssm_b_re[l], 1, 2)
        bt_im = jnp.swapaxes(ssm_b_im[l], 1, 2)
        disc_in = (ssm_a_re[l], ssm_a_im[l], ssm_log_dt[l].reshape(groups, 1), bt_re, bt_im)
        lr, li, bbr, bbi = _disc_fwd(*disc_in)
        ops = _ssm_operands(lr, li, bbr, bbi, ssm_c_re[l], ssm_c_im[l], ssm_d[l], ssm_glu_w[l], ssm_glu_b[l])
        pairs = [("ffn2_w_in", l), ("ffn2_w_out", l)]
        y_ssm, hb, got = _ssm_fwd_pair(u_ssm, ops, nseq, "s5_fwd", comm=gather_comm(pairs))
        store(pairs, got)
        bias_tile = jnp.broadcast_to(gm_b_s[l].T[:, :, None], (GM_CHUNK, heads, GM_HEAD_DIM)).reshape(GM_CHUNK, width)
        y_gm = _gmlp_fwd(zgm, gm_v_gain[l], gm_w_s[l], bias_tile, "gmlp_fwd")
        ycat, x2 = _mix_out_fwd(y_ssm, y_gm, gain_ssm_out[l], gain_gm_out[l], full["mix_w_out"][l], x1, "mix_out")
        hosted = None
        if l + 1 < depth:
            pairs = [("ffn1_w_in", l + 1), ("mix_w_in", l + 1), ("mix_w_out", l + 1)]
            hosted = (functools.partial(gather_comm, pairs), functools.partial(store, pairs))
        x3, s_ffn2 = _ffn_fwd(x2, norm_ffn2[l], full["ffn2_w_in"][l], full["ffn2_w_out"][l], "ffn2", hosted=hosted)
        saved.append(dict(ffn1=s_ffn1, x1=x1, hm=hm, zgm=zgm, disc_in=disc_in, ops=ops, u_ssm=u_ssm, hb=hb, y_ssm=y_ssm,
                          bias_tile=bias_tile, y_gm=y_gm, ycat=ycat, ffn2=s_ffn2))
        xs = x3

    dx, g_norm_final, loss_part = _loss_head(xs, norm_final, loss_target.reshape(n, d))
    big = {name: [None] * depth for name, _ in BIG}
    small = {name: [None] * depth for name in SMALL if name != "norm_final"}
    gpb = GROUPS_PER_BLOCK
    s_blk = STATES_PER_BLOCK
    psum_of, reduced = {}, {}

    def swap_comm(pairs):
        return _pair_exchange_comm([big[nm][l] for nm, l in pairs], [is_col[nm] for nm, _ in pairs])

    def take_swapped(pairs, others):
        for (nm, l), other in zip(pairs, others):
            psum_of[nm, l] = _pair_sum(big[nm][l], other, is_col[nm], core, f"grad_pair_sum_{nm}")

    def send_comm(pairs):
        return _chip_exchange_comm([psum_of[p] for p in pairs], [is_col[nm] for nm, _ in pairs])

    def take_sent(pairs, slots):
        for (nm, l), s in zip(pairs, slots):
            reduced[nm] = _chip_sum(s, core, l, depth, reduced.get(nm), f"grad_chip_sum_{nm}")

    def hosting(make, take, pairs):
        return functools.partial(make, pairs), functools.partial(take, pairs)

    for l in reversed(range(depth)):
        sv = saved[l]
        above = [(nm, l + 1) for nm in ("mix_w_in", "mix_w_out", "ffn1_w_in", "ffn1_w_out")] if l + 1 < depth else []
        dx, small["norm_ffn2"][l], big["ffn2_w_in"][l], big["ffn2_w_out"][l] = _ffn_bwd(
            dx, sv["ffn2"], norm_ffn2[l], full["ffn2_w_in"][l], full["ffn2_w_out"][l], "ffn2",
            hooks={"out_dx": hosting(swap_comm, take_swapped, above)} if above else None)
        mine = [("ffn2_w_in", l), ("ffn2_w_out", l)]
        dy_ssm, dy_gm, small["gain_ssm_out"][l], small["gain_gm_out"][l], got = _mix_out_bwd(
            dx, full["mix_w_out"][l], sv["y_ssm"], sv["y_gm"], gain_ssm_out[l], gain_gm_out[l], "mix_out_dx",
            comm=swap_comm(mine))
        take_swapped(mine, got)
        big["mix_w_out"][l] = _matmul(sv["ycat"], dx, "tn", tm=1024, tn=1024, tk=2048, name="mix_out_dw")
        dzgm, small["gm_w_s"][l], dbias_tile, small["gm_v_gain"][l] = _gmlp_bwd(
            sv["zgm"], dy_gm, gm_v_gain[l], gm_w_s[l], sv["bias_tile"], "gmlp_bwd")
        small["gm_b_s"][l] = dbias_tile.reshape(GM_CHUNK, heads, GM_HEAD_DIM).sum(-1).T
        (du_ssm, dlam, db_bd, dct_bd, dd, dw_bd, dbias), got = _ssm_bwd_pair(
            sv["u_ssm"], dy_ssm, sv["hb"], sv["ops"], nseq, "s5_bwd", comm=send_comm(mine + above))
        take_sent(mine + above, got)
        dlr = dlam[:, 0, :s_blk].reshape(groups, SSM_STATE)
        dli = dlam[:, 0, s_blk:].reshape(groups, SSM_STATE)
        dbbr = _block_diag_extract(db_bd[:, :, :s_blk], gpb)
        dbbi = _block_diag_extract(db_bd[:, :, s_blk:], gpb)
        da_re, da_im, dldt, dbt_re, dbt_im = _disc_bwd(*sv["disc_in"], dlr, dli, dbbr, dbbi)
        small["ssm_a_re"][l], small["ssm_a_im"][l], small["ssm_log_dt"][l] = da_re, da_im, dldt.reshape(groups)
        small["ssm_b_re"][l] = jnp.swapaxes(dbt_re, 1, 2)
        small["ssm_b_im"][l] = jnp.swapaxes(dbt_im, 1, 2)
        small["ssm_c_re"][l] = _block_diag_extract(dct_bd[:, :, :s_blk], gpb)
        small["ssm_c_im"][l] = -_block_diag_extract(dct_bd[:, :, s_blk:], gpb)
        small["ssm_d"][l] = dd.reshape(groups, SSM_CH)
        small["ssm_glu_w"][l] = jnp.concatenate(
            [_block_diag_extract(dw_bd[:, :, :LANES], gpb), _block_diag_extract(dw_bd[:, :, LANES:], gpb)], axis=-1)
        small["ssm_glu_b"][l] = jnp.concatenate(
            [dbias[:, 0, :LANES].reshape(groups, SSM_CH), dbias[:, 0, LANES:].reshape(groups, SSM_CH)], axis=-1)
        cols_mi = 3 * width
        dw_mi = _matmul(sv["hm"], du_ssm, "tn", tm=1024, tn=width, tk=2048, name="mix_in_dw_ssm", out_cols=cols_mi)
        big["mix_w_in"][l] = _matmul(sv["hm"], dzgm, "tn", tm=1024, tn=width, tk=2048, name="mix_in_dw_gm",
                                     out_cols=cols_mi, col_off=width, into=dw_mi)
        dx, small["norm_mix"][l] = _proj_in_bwd([(du_ssm, 0), (dzgm, width)], full["mix_w_in"][l], sv["x1"],
                                                norm_mix[l], dx, "mix_in_dx")
        hooks = None
        if l == 0:
            mix, w_out_0, w_in_0 = [("mix_w_in", 0), ("mix_w_out", 0)], [("ffn1_w_out", 0)], [("ffn1_w_in", 0)]
            hooks = {"out_dx": hosting(swap_comm, take_swapped, mix), "out_dw": hosting(send_comm, take_sent, mix),
                     "in_dw_g": hosting(swap_comm, take_swapped, w_out_0),
                     "in_dw_u": hosting(send_comm, take_sent, w_out_0),
                     "in_dx": hosting(swap_comm, take_swapped, w_in_0)}

        def publish(which, dw, l=l):
            big[f"ffn1_w_{which}"][l] = dw

        dx, small["norm_ffn1"][l], big["ffn1_w_in"][l], big["ffn1_w_out"][l] = _ffn_bwd(
            dx, sv["ffn1"], norm_ffn1[l], full["ffn1_w_in"][l], full["ffn1_w_out"][l], "ffn1",
            hooks=hooks, publish=publish)
    grad_x = dx.reshape(nseq, seq, d)

    pieces = [jnp.stack(small[name]).reshape(-1) for name in SMALL if name != "norm_final"]
    pieces += [g_norm_final.reshape(-1), loss_part.reshape(1)]
    sizes = [p.shape[0] for p in pieces]
    total = sum(sizes)
    rows = -(-total // (LANES * N_DEV * SUBLANES)) * N_DEV * SUBLANES
    pad = rows * LANES - total
    tail = [("ffn1_w_in", 0)]
    flat_g, got = _all_reduce_small(
        jnp.concatenate(pieces + [jnp.zeros((pad,), F32)]).reshape(rows, LANES), send_comm(tail))
    take_sent(tail, got)
    flat_g = flat_g.reshape(-1)
    loss = flat_g[total - 1]

    names = [name for name, _ in BIG]
    grads = dict(zip(names, _run_comm(_pair_share_comm([reduced[nm] for nm in names]), "grad_pair_share")))
    offs = 0
    for name, size in zip(SMALL, sizes[:-1]):
        grads[name] = flat_g[offs:offs + size].reshape(wts[name].shape)
        offs += size

    delta, new_m, new_v = {}, {}, {}
    for name, _ in BIG:
        shape = wts[name].shape
        two_d = lambda a: a.reshape(shape[0] * shape[1], shape[2])
        go, dl, nm, nv = _adamw(two_d(wts[name]), two_d(grads[name]), two_d(mom[name]), two_d(var[name]),
                                f"adamw_{name}")
        grads[name] = go.reshape(shape)
        delta[name], new_m[name], new_v[name] = dl.reshape(shape), nm.reshape(shape), nv.reshape(shape)
    at_least_2d = lambda a: a.reshape(1, -1) if a.ndim == 1 else a
    dls, nms, nvs = _adamw_many(*[[at_least_2d(tree[k]) for k in SMALL] for tree in (wts, grads, mom, var)],
                                "adamw_small")
    for name, dl, nm, nv in zip(SMALL, dls, nms, nvs):
        shape = wts[name].shape
        delta[name], new_m[name], new_v[name] = dl.reshape(shape), nm.reshape(shape), nv.reshape(shape)

    return (loss, grad_x, *[grads[k] for k in WEIGHTS], *[delta[k] for k in WEIGHTS],
            *[new_m[k] for k in WEIGHTS], *[new_v[k] for k in WEIGHTS])
```

```python
import functools
import math

import jax
import jax.numpy as jnp
from jax import lax
from jax.experimental import pallas as pl
from jax.experimental.pallas import tpu as pltpu

F32 = jnp.float32
BF16 = jnp.bfloat16
MESH = pl.DeviceIdType.MESH

EPS = 1e-6
SSM_CH = 16
SSM_STATE = 64
GM_CHUNK = 128
GM_HEAD_DIM = 128
SUBLANES = 8
LANES = 128
GROUPS_PER_BLOCK = LANES // SSM_CH
STATES_PER_BLOCK = GROUPS_PER_BLOCK * SSM_STATE
SSM_TIME_CHUNK = 128
N_CHIPS = 4
N_DEV = 8

ADAM_LR = 0.001
ADAM_B1 = 0.9
ADAM_B2 = 0.999
ADAM_EPS = 1e-08
ADAM_WD = 0.01
ADAM_STEP = 10

VMEM_LIMIT = 56 * 1024 * 1024


def _tile(dim, pref, align):
    best = None
    t = align
    while t <= min(dim, pref):
        if dim % t == 0:
            best = t
        t += align
    return best if best is not None else dim


def _params(*sem):
    return pltpu.CompilerParams(dimension_semantics=sem, vmem_limit_bytes=VMEM_LIMIT)


def _gelu(x):
    c = math.sqrt(2.0 / math.pi)
    return 0.5 * x * (1.0 + jnp.tanh(c * (x + 0.044715 * x * x * x)))


def _gelu_and_grad(x):
    c = math.sqrt(2.0 / math.pi)
    t = jnp.tanh(c * (x + 0.044715 * x * x * x))
    g = 0.5 * x * (1.0 + t)
    dg = 0.5 * (1.0 + t) + 0.5 * x * (1.0 - t * t) * c * (1.0 + 3.0 * 0.044715 * x * x)
    return g, dg


def _sigmoid(x):
    return 0.5 * jnp.tanh(0.5 * x) + 0.5


def _matmul(a, b, mode, *, out_dtype=F32, scale=1.0, res=None, tm=512, tn=1024, tk=1024, name="mm",
            out_cols=None, col_off=0, into=None, comm=None):
    if mode == "nn":
        (m, k), (k2, n) = a.shape, b.shape
    elif mode == "nt":
        (m, k), (n, k2) = a.shape, b.shape
    else:
        (k, m), (k2, n) = a.shape, b.shape
    assert k == k2, (a.shape, b.shape, mode)
    tm = _tile(m, tm, 16 if mode != "tn" else LANES)
    tn = _tile(n, tn, LANES)
    tk = _tile(k, tk, LANES if mode != "tn" else 16)
    nk = k // tk
    grid = (m // tm, n // tn, nk)
    if mode == "nn":
        a_spec = pl.BlockSpec((tm, tk), lambda i, j, kk: (i, kk))
        b_spec = pl.BlockSpec((tk, tn), lambda i, j, kk: (kk, j))
        dims = (((1,), (0,)), ((), ()))
    elif mode == "nt":
        a_spec = pl.BlockSpec((tm, tk), lambda i, j, kk: (i, kk))
        b_spec = pl.BlockSpec((tn, tk), lambda i, j, kk: (j, kk))
        dims = (((1,), (1,)), ((), ()))
    else:
        a_spec = pl.BlockSpec((tk, tm), lambda i, j, kk: (kk, i))
        b_spec = pl.BlockSpec((tk, tn), lambda i, j, kk: (kk, j))
        dims = (((0,), (0,)), ((), ()))
    assert col_off % tn == 0
    off = col_off // tn
    r_spec = pl.BlockSpec((tm, tn), lambda i, j, kk: (i, j))
    o_spec = pl.BlockSpec((tm, tn), lambda i, j, kk: (i, j + off))
    has_res = res is not None
    has_into = into is not None

    def body(*refs):
        refs = list(refs)
        a_ref, b_ref = refs[:2]
        pos = 2
        r_ref = None
        if has_res:
            r_ref = refs[pos]
            pos += 1
        if has_into:
            pos += 1
        o_ref = refs[pos]
        acc_ref = refs[pos + 1] if nk > 1 else None
        part = lax.dot_general(a_ref[...].astype(BF16), b_ref[...].astype(BF16), dims,
                               preferred_element_type=F32)

        def finish(r):
            if scale != 1.0:
                r = r * scale
            if has_res:
                r = r + r_ref[...].astype(F32)
            o_ref[...] = r.astype(o_ref.dtype)

        if nk == 1:
            finish(part)
        else:
            kk = pl.program_id(2)

            @pl.when(kk == 0)
            def _():
                acc_ref[...] = part

            @pl.when(kk > 0)
            def _():
                acc_ref[...] += part

            @pl.when(kk == nk - 1)
            def _():
                finish(acc_ref[...])

    in_specs = [a_spec, b_spec]
    args = [a, b]
    if has_res:
        in_specs.append(r_spec)
        args.append(res)
    aliases = {}
    if has_into:
        in_specs.append(pl.BlockSpec(memory_space=pl.ANY))
        args.append(into)
        aliases = {len(args) - 1: 0}
    (out,), comm_outs = _call(
        body, name=name, grid=grid, in_specs=in_specs, out_specs=[o_spec],
        out_shape=[jax.ShapeDtypeStruct((m, n if out_cols is None else out_cols), out_dtype)],
        scratch_shapes=[pltpu.VMEM((tm, tn), F32)] if nk > 1 else [],
        aliases=aliases, semantics=("parallel", "parallel", "arbitrary"), args=args, comm=comm)
    return out if comm is None else (out, comm_outs)


class _Comm:
    def __init__(self, ins, outs, sems, start, finish, alias=None, middle=None):
        self.ins, self.outs, self.sems, self.start, self.finish = list(ins), list(outs), list(sems), start, finish
        self.alias = dict(alias or {})
        self.middle = middle


def _merge_comms(a, b):
    assert a.middle is None and b.middle is None
    cut = (len(a.ins), len(a.outs), len(a.sems))

    def both(which):
        def run(ins, outs, sems):
            getattr(a, which)(ins[:cut[0]], outs[:cut[1]], sems[:cut[2]])
            getattr(b, which)(ins[cut[0]:], outs[cut[1]:], sems[cut[2]:])
        return run

    alias = dict(a.alias)
    alias.update({cut[0] + ci: cut[1] + co for ci, co in b.alias.items()})
    return _Comm(a.ins + b.ins, a.outs + b.outs, a.sems + b.sems, both("start"), both("finish"), alias=alias)


def _call(body, *, name, grid, in_specs, out_specs, out_shape, args, scratch_shapes=(), semantics=(), aliases=None,
          comm=None):
    in_specs, out_specs, out_shape = list(in_specs), list(out_specs), list(out_shape)
    scratch_shapes = list(scratch_shapes)
    aliases = dict(aliases or {})
    if comm is None:
        outs = pl.pallas_call(
            body, name=name, grid=grid, in_specs=in_specs, out_specs=out_specs, out_shape=out_shape,
            scratch_shapes=scratch_shapes, input_output_aliases=aliases, compiler_params=_params(*semantics),
        )(*args)
        return list(outs), []
    n_in, n_out, n_scr = len(in_specs), len(out_specs), len(scratch_shapes)
    c_in, c_out = len(comm.ins), len(comm.outs)
    for ci, co in comm.alias.items():
        aliases[n_in + ci] = n_out + co

    def hosted(*refs):
        refs = list(refs)
        ins, cins = refs[:n_in], refs[n_in:n_in + c_in]
        p = n_in + c_in
        outs, couts = refs[p:p + n_out], refs[p + n_out:p + n_out + c_out]
        p += n_out + c_out
        scr, sems = refs[p:p + n_scr], refs[p + n_scr:]
        ids = [pl.program_id(a) for a in range(len(grid))]
        first = functools.reduce(jnp.logical_and, [i == 0 for i in ids])
        last = functools.reduce(jnp.logical_and, [i == g - 1 for i, g in zip(ids, grid)])

        total = math.prod(grid)
        late = comm.middle is not None and total >= 4

        @pl.when(first)
        def _():
            comm.start(cins, couts, sems)

        if late:
            flat = functools.reduce(lambda acc, ig: acc * ig[1] + ig[0], zip(ids, grid), 0)

            @pl.when(flat == (3 * total) // 4)
            def _():
                comm.middle(cins, couts, sems)

        body(*ins, *outs, *scr)

        @pl.when(last)
        def _():
            if comm.middle is not None and not late:
                comm.middle(cins, couts, sems)
            comm.finish(cins, couts, sems)

    any_spec = pl.BlockSpec(memory_space=pl.ANY)
    outs = pl.pallas_call(
        hosted, name=name, grid=grid, in_specs=in_specs + [any_spec] * c_in, out_specs=out_specs + [any_spec] * c_out,
        out_shape=out_shape + comm.outs, scratch_shapes=scratch_shapes + comm.sems, input_output_aliases=aliases,
        compiler_params=_params(*(["arbitrary"] * len(grid))),
    )(*args, *comm.ins)
    return list(outs[:n_out]), list(outs[n_out:])


def _run_comm(comm, name):
    c_in, c_out = len(comm.ins), len(comm.outs)

    def body(*refs):
        refs = list(refs)
        cins, couts, sems = refs[:c_in], refs[c_in:c_in + c_out], refs[c_in + c_out:]
        comm.start(cins, couts, sems)
        if comm.middle is not None:
            comm.middle(cins, couts, sems)
        comm.finish(cins, couts, sems)

    any_spec = pl.BlockSpec(memory_space=pl.ANY)
    return list(pl.pallas_call(
        body, name=name, in_specs=[any_spec] * c_in, out_specs=[any_spec] * c_out, out_shape=comm.outs,
        scratch_shapes=comm.sems, input_output_aliases=comm.alias,
    )(*comm.ins))


def _loss_head(x, gain, target):
    n, d = x.shape
    tm = _tile(n, 512, 8)
    steps = n // tm

    def body(x_ref, g_ref, t_ref, dx_ref, dg_ref, loss_ref, acc_ref, lacc_ref):
        i = pl.program_id(0)
        xv = x_ref[...]
        g = g_ref[...]
        r = lax.rsqrt(jnp.mean(xv * xv, axis=-1, keepdims=True) + EPS)
        xh = xv * r
        err = xh * g - t_ref[...]
        dy = err * (1.0 / d)
        dyg = dy * g
        mean = jnp.mean(dyg * xh, axis=-1, keepdims=True)
        dx_ref[...] = r * (dyg - xh * mean)
        part = jnp.sum((dy * xh).reshape(tm // SUBLANES, SUBLANES, d), axis=0)
        lpart = jnp.sum((err * err).reshape(tm // SUBLANES, SUBLANES, d), axis=0)

        @pl.when(i == 0)
        def _():
            acc_ref[...] = part
            lacc_ref[...] = lpart

        @pl.when(i > 0)
        def _():
            acc_ref[...] += part
            lacc_ref[...] += lpart

        @pl.when(i == steps - 1)
        def _():
            dg_ref[...] = jnp.sum(acc_ref[...], axis=0, keepdims=True)
            tot = jnp.sum(jnp.sum(lacc_ref[...], axis=0, keepdims=True), axis=1, keepdims=True)
            loss_ref[...] = jnp.broadcast_to(tot * (0.5 / d), loss_ref.shape)

    row = pl.BlockSpec((tm, d), lambda i: (i, 0))
    vec = pl.BlockSpec((1, d), lambda i: (0, 0))
    dx, dg, loss = pl.pallas_call(
        body, name="loss_head", grid=(steps,),
        in_specs=[row, vec, row],
        out_specs=[row, vec, pl.BlockSpec((1, LANES), lambda i: (0, 0))],
        out_shape=[jax.ShapeDtypeStruct((n, d), F32), jax.ShapeDtypeStruct((1, d), F32),
                   jax.ShapeDtypeStruct((1, LANES), F32)],
        scratch_shapes=[pltpu.VMEM((SUBLANES, d), F32), pltpu.VMEM((SUBLANES, d), F32)],
        compiler_params=_params("arbitrary"),
    )(x, gain.reshape(1, d), target)
    return dx, dg.reshape(d), loss[0, 0]


def _rms_rows(xv):
    return lax.rsqrt(jnp.mean(xv * xv, axis=-1, keepdims=True) + EPS)


def _ffn_in_fwd(x, gain, w_in, name, comm=None):
    n, d = x.shape
    f = w_in.shape[1] // 2
    tm = _tile(n, 256, 16)
    tn = _tile(f, 4096, LANES)
    nj = f // tn

    def body(x_ref, gain_ref, wg_ref, wu_ref, h_ref, t_ref, q_ref, a_ref):
        @pl.when(pl.program_id(1) == 0)
        def _():
            xv = x_ref[...]
            h_ref[...] = (xv * _rms_rows(xv) * gain_ref[...]).astype(h_ref.dtype)

        h = h_ref[...]
        g = jnp.dot(h, wg_ref[...], preferred_element_type=F32)
        u = jnp.dot(h, wu_ref[...], preferred_element_type=F32)
        s = _sigmoid(g)
        t = g * s
        t_ref[...] = t.astype(t_ref.dtype)
        q_ref[...] = (u * (s + t * (1.0 - s))).astype(q_ref.dtype)
        a_ref[...] = (t * u).astype(a_ref.dtype)

    row = pl.BlockSpec((tm, d), lambda i, j: (i, 0))
    tile = pl.BlockSpec((tm, tn), lambda i, j: (i, j))
    act = jax.ShapeDtypeStruct((n, f), BF16)
    outs, comm_outs = _call(
        body, name=name, grid=(n // tm, nj),
        in_specs=[row, pl.BlockSpec((1, d), lambda i, j: (0, 0)),
                  pl.BlockSpec((d, tn), lambda i, j: (0, j)), pl.BlockSpec((d, tn), lambda i, j: (0, j + nj))],
        out_specs=[row, tile, tile, tile],
        out_shape=[jax.ShapeDtypeStruct((n, d), BF16), act, act, act],
        semantics=("parallel", "arbitrary"), args=(x, gain.reshape(1, d), w_in, w_in), comm=comm)
    return outs if comm is None else (outs, comm_outs)


def _ffn_out_bwd(dout, w_out, t, q, name, comm=None):
    n, d = dout.shape
    f = w_out.shape[0]
    tm = _tile(n, 256, 16)
    tn = _tile(f, 4096, LANES)

    def body(d_ref, w_ref, t_ref, q_ref, dg_ref, du_ref):
        da = 0.5 * lax.dot_general(d_ref[...].astype(BF16), w_ref[...], (((1,), (1,)), ((), ())),
                                   preferred_element_type=F32)
        dg_ref[...] = (da * q_ref[...].astype(F32)).astype(dg_ref.dtype)
        du_ref[...] = (da * t_ref[...].astype(F32)).astype(du_ref.dtype)

    tile = pl.BlockSpec((tm, tn), lambda i, j: (i, j))
    act = jax.ShapeDtypeStruct((n, f), BF16)
    outs, comm_outs = _call(
        body, name=name, grid=(n // tm, f // tn),
        in_specs=[pl.BlockSpec((tm, d), lambda i, j: (i, 0)), pl.BlockSpec((tn, d), lambda i, j: (j, 0)), tile, tile],
        out_specs=[tile, tile], out_shape=[act, act],
        semantics=("parallel", "parallel"), args=(dout, w_out, t, q), comm=comm)
    return outs if comm is None else (outs, comm_outs)


def _proj_in_bwd(parts, w, x, gain, dres, name, comm=None):
    n, d = x.shape
    tm = _tile(n, 256, 8)
    steps = n // tm
    np_ = len(parts)
    offs = [off for _, off in parts]
    widths = [a.shape[1] for a, _ in parts]

    def body(*refs):
        a_refs = refs[:np_]
        w_ref, x_ref, g_ref, dr_ref, dx_ref, dg_ref, acc_ref = refs[np_:]
        i = pl.program_id(0)
        dh = None
        for a_ref, off, kp in zip(a_refs, offs, widths):
            part = lax.dot_general(a_ref[...].astype(BF16), w_ref[:, off:off + kp], (((1,), (1,)), ((), ())),
                                   preferred_element_type=F32)
            dh = part if dh is None else dh + part
        xv = x_ref[...]
        r = _rms_rows(xv)
        xh = xv * r
        dyg = dh * g_ref[...]
        mean = jnp.mean(dyg * xh, axis=-1, keepdims=True)
        dx_ref[...] = dr_ref[...] + r * (dyg - xh * mean)
        part = jnp.sum((dh * xh).reshape(tm // SUBLANES, SUBLANES, d), axis=0)

        @pl.when(i == 0)
        def _():
            acc_ref[...] = part

        @pl.when(i > 0)
        def _():
            acc_ref[...] += part

        @pl.when(i == steps - 1)
        def _():
            dg_ref[...] = jnp.sum(acc_ref[...], axis=0, keepdims=True)

    row = pl.BlockSpec((tm, d), lambda i: (i, 0))
    vec = pl.BlockSpec((1, d), lambda i: (0, 0))
    (dx, dg), comm_outs = _call(
        body, name=name, grid=(steps,),
        in_specs=[pl.BlockSpec((tm, kp), lambda i: (i, 0)) for kp in widths]
        + [pl.BlockSpec(w.shape, lambda i: (0, 0)), row, vec, row],
        out_specs=[row, vec],
        out_shape=[jax.ShapeDtypeStruct((n, d), F32), jax.ShapeDtypeStruct((1, d), F32)],
        scratch_shapes=[pltpu.VMEM((SUBLANES, d), F32)],
        semantics=("arbitrary",), args=(*[a for a, _ in parts], w, x, gain.reshape(1, d), dres), comm=comm)
    return (dx, dg.reshape(d)) if comm is None else (dx, dg.reshape(d), comm_outs)


def _mix_in_fwd(x, gain, w, width, name, comm=None):
    n, d = x.shape
    cols = w.shape[1]
    tm = _tile(n, 512, 16)

    def body(x_ref, gain_ref, w_ref, h_ref, u_ref, z_ref):
        xv = x_ref[...]
        h = (xv * _rms_rows(xv) * gain_ref[...]).astype(h_ref.dtype)
        h_ref[...] = h
        z = jnp.dot(h, w_ref[...], preferred_element_type=F32)
        u_ref[...] = z[:, 0:width]
        z_ref[...] = z[:, width:cols]

    row = pl.BlockSpec((tm, d), lambda i: (i, 0))
    outs, comm_outs = _call(
        body, name=name, grid=(n // tm,),
        in_specs=[row, pl.BlockSpec((1, d), lambda i: (0, 0)), pl.BlockSpec((d, cols), lambda i: (0, 0))],
        out_specs=[row, pl.BlockSpec((tm, width), lambda i: (i, 0)), pl.BlockSpec((tm, cols - width), lambda i: (i, 0))],
        out_shape=[jax.ShapeDtypeStruct((n, d), BF16), jax.ShapeDtypeStruct((n, width), F32),
                   jax.ShapeDtypeStruct((n, cols - width), F32)],
        semantics=("parallel",), args=(x, gain.reshape(1, d), w), comm=comm)
    return outs if comm is None else (*outs, comm_outs)


def _tril_mask():
    t = lax.broadcasted_iota(jnp.int32, (GM_CHUNK, GM_CHUNK), 0)
    s = lax.broadcasted_iota(jnp.int32, (GM_CHUNK, GM_CHUNK), 1)
    return s <= t


def _gmlp_fwd(zgm, v_gain, w_s, bias_tile, name):
    n, w2 = zgm.shape
    w = w2 // 2
    heads = w // GM_HEAD_DIM
    tm = _tile(n, 512, GM_CHUNK)
    nq = tm // GM_CHUNK

    def body(u_ref, v_ref, gain_ref, w_ref, b_ref, o_ref):
        mask = _tril_mask()
        ug = _gelu(u_ref[...])
        vg = _gelu(v_ref[...])
        for h in range(heads):
            cols = slice(h * GM_HEAD_DIM, (h + 1) * GM_HEAD_DIM)
            vh = vg[:, cols]
            r = lax.rsqrt(jnp.mean(vh * vh, axis=-1, keepdims=True) + EPS)
            vn = (vh * r * gain_ref[:, cols]).astype(BF16)
            wm = jnp.where(mask, w_ref[h], 0.0).astype(BF16)
            for q in range(nq):
                rows = slice(q * GM_CHUNK, (q + 1) * GM_CHUNK)
                s = jnp.dot(wm, vn[rows], preferred_element_type=F32) + b_ref[:, cols]
                o_ref[rows, cols] = ug[rows, cols] * s

    return pl.pallas_call(
        body, name=name, grid=(n // tm,),
        in_specs=[pl.BlockSpec((tm, w), lambda i: (i, 0)), pl.BlockSpec((tm, w), lambda i: (i, 1)),
                  pl.BlockSpec((1, w), lambda i: (0, 0)),
                  pl.BlockSpec((heads, GM_CHUNK, GM_CHUNK), lambda i: (0, 0, 0)),
                  pl.BlockSpec((GM_CHUNK, w), lambda i: (0, 0))],
        out_specs=pl.BlockSpec((tm, w), lambda i: (i, 0)),
        out_shape=jax.ShapeDtypeStruct((n, w), F32),
        compiler_params=_params("parallel"),
    )(zgm, zgm, v_gain.reshape(1, w), w_s, bias_tile)


def _gmlp_bwd(zgm, dy, v_gain, w_s, bias_tile, name):
    n, w2 = zgm.shape
    w = w2 // 2
    heads = w // GM_HEAD_DIM
    tm = _tile(n, 512, GM_CHUNK)
    nq = tm // GM_CHUNK
    steps = n // tm

    def body(z_ref, dy_ref, gain_ref, w_ref, b_ref, dz_ref, dw_ref, db_ref, dgain_ref):
        i = pl.program_id(0)
        mask = _tril_mask()

        @pl.when(i == 0)
        def _():
            dw_ref[...] = jnp.zeros_like(dw_ref)
            db_ref[...] = jnp.zeros_like(db_ref)
            dgain_ref[...] = jnp.zeros_like(dgain_ref)

        ug, dug_du = _gelu_and_grad(z_ref[:, 0:w])
        vg, dvg_dv = _gelu_and_grad(z_ref[:, w:w2])
        dyv = dy_ref[...]
        for h in range(heads):
            cols = slice(h * GM_HEAD_DIM, (h + 1) * GM_HEAD_DIM)
            vh = vg[:, cols]
            r = lax.rsqrt(jnp.mean(vh * vh, axis=-1, keepdims=True) + EPS)
            vhat = vh * r
            gain = gain_ref[:, cols]
            vn = (vhat * gain).astype(BF16)
            wm = jnp.where(mask, w_ref[h], 0.0).astype(BF16)
            dvn_parts = []
            for q in range(nq):
                rows = slice(q * GM_CHUNK, (q + 1) * GM_CHUNK)
                s = jnp.dot(wm, vn[rows], preferred_element_type=F32) + b_ref[:, cols]
                dyq = dyv[rows, cols]
                dz_ref[rows, cols] = dyq * s * dug_du[rows, cols]
                ds = dyq * ug[rows, cols]
                db_ref[:, cols] += ds
                dsb = ds.astype(BF16)
                dw_ref[h] += lax.dot_general(dsb, vn[rows], (((1,), (1,)), ((), ())), preferred_element_type=F32)
                dvn_parts.append(lax.dot_general(wm, dsb, (((0,), (0,)), ((), ())), preferred_element_type=F32))
            dvn = jnp.concatenate(dvn_parts, axis=0) if nq > 1 else dvn_parts[0]
            dgain_ref[:, cols] += jnp.sum(dvn * vhat, axis=0, keepdims=True)
            dvhat = dvn * gain
            mean = jnp.mean(dvhat * vhat, axis=-1, keepdims=True)
            dz_ref[:, w + h * GM_HEAD_DIM:w + (h + 1) * GM_HEAD_DIM] = r * (dvhat - vhat * mean) * dvg_dv[:, cols]

        @pl.when(i == steps - 1)
        def _():
            for h in range(heads):
                dw_ref[h] = jnp.where(mask, dw_ref[h], 0.0)

    dz, dw, db, dgain = pl.pallas_call(
        body, name=name, grid=(steps,),
        in_specs=[pl.BlockSpec((tm, w2), lambda i: (i, 0)), pl.BlockSpec((tm, w), lambda i: (i, 0)),
                  pl.BlockSpec((1, w), lambda i: (0, 0)),
                  pl.BlockSpec((heads, GM_CHUNK, GM_CHUNK), lambda i: (0, 0, 0)),
                  pl.BlockSpec((GM_CHUNK, w), lambda i: (0, 0))],
        out_specs=[pl.BlockSpec((tm, w2), lambda i: (i, 0)),
                   pl.BlockSpec((heads, GM_CHUNK, GM_CHUNK), lambda i: (0, 0, 0)),
                   pl.BlockSpec((GM_CHUNK, w), lambda i: (0, 0)),
                   pl.BlockSpec((1, w), lambda i: (0, 0))],
        out_shape=[jax.ShapeDtypeStruct((n, w2), F32), jax.ShapeDtypeStruct((heads, GM_CHUNK, GM_CHUNK), F32),
                   jax.ShapeDtypeStruct((GM_CHUNK, w), F32), jax.ShapeDtypeStruct((1, w), F32)],
        compiler_params=_params("arbitrary"),
    )(zgm, dy, v_gain.reshape(1, w), w_s, bias_tile)
    return dz, dw, db, dgain.reshape(w)


def _mix_out_fwd(y_ssm, y_gm, g1, g2, w_out, x, name, comm=None):
    n, w = y_ssm.shape
    d = w_out.shape[1]
    tm = _tile(n, 512, 16)

    def body(a_ref, b_ref, g1_ref, g2_ref, w_ref, x_ref, ycat_ref, o_ref):
        for src, g_ref, lo in ((a_ref, g1_ref, 0), (b_ref, g2_ref, w)):
            v = src[...]
            ycat_ref[:, lo:lo + w] = (v * _rms_rows(v) * g_ref[...]).astype(ycat_ref.dtype)
        o_ref[...] = x_ref[...] + jnp.dot(ycat_ref[...], w_ref[...], preferred_element_type=F32)

    row = pl.BlockSpec((tm, w), lambda i: (i, 0))
    vec = pl.BlockSpec((1, w), lambda i: (0, 0))
    outs, comm_outs = _call(
        body, name=name, grid=(n // tm,),
        in_specs=[row, row, vec, vec, pl.BlockSpec((2 * w, d), lambda i: (0, 0)), pl.BlockSpec((tm, d), lambda i: (i, 0))],
        out_specs=[pl.BlockSpec((tm, 2 * w), lambda i: (i, 0)), pl.BlockSpec((tm, d), lambda i: (i, 0))],
        out_shape=[jax.ShapeDtypeStruct((n, 2 * w), BF16), jax.ShapeDtypeStruct((n, d), F32)],
        semantics=("parallel",), args=(y_ssm, y_gm, g1.reshape(1, w), g2.reshape(1, w), w_out, x), comm=comm)
    return outs if comm is None else (*outs, comm_outs)


def _mix_out_bwd(dx, w_out, y_ssm, y_gm, g1, g2, name, comm=None):
    n, w = y_ssm.shape
    d = w_out.shape[1]
    tm = _tile(n, 512, 8)
    steps = n // tm

    def body(dx_ref, w_ref, a_ref, b_ref, g1_ref, g2_ref, da_ref, db_ref, dg1_ref, dg2_ref):
        i = pl.program_id(0)

        @pl.when(i == 0)
        def _():
            dg1_ref[...] = jnp.zeros_like(dg1_ref)
            dg2_ref[...] = jnp.zeros_like(dg2_ref)

        dycat = lax.dot_general(dx_ref[...].astype(BF16), w_ref[...], (((1,), (1,)), ((), ())),
                                preferred_element_type=F32)
        for src, g_ref, lo, dst, dg_ref in ((a_ref, g1_ref, 0, da_ref, dg1_ref), (b_ref, g2_ref, w, db_ref, dg2_ref)):
            v = src[...]
            dh = dycat[:, lo:lo + w]
            r = _rms_rows(v)
            vh = v * r
            dyg = dh * g_ref[...]
            mean = jnp.mean(dyg * vh, axis=-1, keepdims=True)
            dst[...] = r * (dyg - vh * mean)
            dg_ref[...] += jnp.sum(dh * vh, axis=0, keepdims=True)

    row = pl.BlockSpec((tm, w), lambda i: (i, 0))
    vec = pl.BlockSpec((1, w), lambda i: (0, 0))
    (da, db, dg1, dg2), comm_outs = _call(
        body, name=name, grid=(steps,),
        in_specs=[pl.BlockSpec((tm, d), lambda i: (i, 0)), pl.BlockSpec((2 * w, d), lambda i: (0, 0)), row, row, vec, vec],
        out_specs=[row, row, vec, vec],
        out_shape=[jax.ShapeDtypeStruct((n, w), F32), jax.ShapeDtypeStruct((n, w), F32),
                   jax.ShapeDtypeStruct((1, w), F32), jax.ShapeDtypeStruct((1, w), F32)],
        semantics=("arbitrary",), args=(dx, w_out, y_ssm, y_gm, g1.reshape(1, w), g2.reshape(1, w)), comm=comm)
    res = (da, db, dg1.reshape(w), dg2.reshape(w))
    return res if comm is None else (*res, comm_outs)


def _discretise(a_re, a_im, log_dt, bt_re, bt_im):
    dt = jnp.exp(log_dt)
    e = jnp.exp(a_re * dt)
    ang = a_im * dt
    lr = e * jnp.cos(ang)
    li = e * jnp.sin(ang)
    den = a_re * a_re + a_im * a_im
    cr = ((lr - 1.0) * a_re + li * a_im) / den
    ci = (li * a_re - (lr - 1.0) * a_im) / den
    cr3 = cr[:, None, :]
    ci3 = ci[:, None, :]
    return lr, li, cr3 * bt_re - ci3 * bt_im, cr3 * bt_im + ci3 * bt_re


def _disc_fwd(a_re, a_im, log_dt, bt_re, bt_im):
    g, p = a_re.shape
    c = bt_re.shape[1]

    def body(are_ref, aim_ref, ldt_ref, bre_ref, bim_ref, lr_ref, li_ref, bbr_ref, bbi_ref):
        lr, li, bbr, bbi = _discretise(are_ref[...], aim_ref[...], ldt_ref[...], bre_ref[...], bim_ref[...])
        lr_ref[...] = lr
        li_ref[...] = li
        bbr_ref[...] = bbr
        bbi_ref[...] = bbi

    return pl.pallas_call(
        body, name="s5_discretise",
        out_shape=[jax.ShapeDtypeStruct((g, p), F32), jax.ShapeDtypeStruct((g, p), F32),
                   jax.ShapeDtypeStruct((g, c, p), F32), jax.ShapeDtypeStruct((g, c, p), F32)],
    )(a_re, a_im, log_dt, bt_re, bt_im)


def _disc_bwd(a_re, a_im, log_dt, bt_re, bt_im, dlr, dli, dbbr, dbbi):
    g, p = a_re.shape
    c = bt_re.shape[1]

    def body(are_ref, aim_ref, ldt_ref, bre_ref, bim_ref, dlr_ref, dli_ref, dbbr_ref, dbbi_ref,
             dare_ref, daim_ref, dldt_ref, dbre_ref, dbim_ref):
        _, vjp = jax.vjp(_discretise, are_ref[...], aim_ref[...], ldt_ref[...], bre_ref[...], bim_ref[...])
        dare, daim, dldt, dbre, dbim = vjp((dlr_ref[...], dli_ref[...], dbbr_ref[...], dbbi_ref[...]))
        dare_ref[...] = dare
        daim_ref[...] = daim
        dldt_ref[...] = dldt
        dbre_ref[...] = dbre
        dbim_ref[...] = dbim

    return pl.pallas_call(
        body, name="s5_discretise_bwd",
        out_shape=[jax.ShapeDtypeStruct((g, p), F32), jax.ShapeDtypeStruct((g, p), F32),
                   jax.ShapeDtypeStruct((g, 1), F32),
                   jax.ShapeDtypeStruct((g, c, p), F32), jax.ShapeDtypeStruct((g, c, p), F32)],
    )(a_re, a_im, log_dt, bt_re, bt_im, dlr, dli, dbbr, dbbi)


def _block_diag(w, nb):
    g, a, b = w.shape
    gpb = g // nb
    eye = jnp.eye(gpb, dtype=w.dtype)
    w4 = w.reshape(nb, gpb, a, b)
    return jnp.einsum("ngab,gh->ngahb", w4, eye).reshape(nb, gpb * a, gpb * b)


def _block_diag_extract(m, gpb):
    nb, ga, gb = m.shape
    a, b = ga // gpb, gb // gpb
    m5 = m.reshape(nb, gpb, a, gpb, b)
    idx = jnp.arange(gpb)
    return m5[:, idx, :, idx, :].transpose(1, 0, 2, 3).reshape(nb * gpb, a, b)


def _ssm_operands(lr, li, bbr, bbi, c_re, c_im, d_skip, glu_w, glu_b):
    g = lr.shape[0]
    nb = g // GROUPS_PER_BLOCK
    s = STATES_PER_BLOCK
    lam = jnp.concatenate([lr.reshape(nb, 1, s), li.reshape(nb, 1, s)], axis=-1)
    b_bd = jnp.concatenate([_block_diag(bbr, nb), _block_diag(bbi, nb)], axis=-1)
    ct_re = jnp.swapaxes(c_re, 1, 2)
    ct_im = jnp.swapaxes(c_im, 1, 2)
    c_bd = jnp.concatenate([_block_diag(ct_re, nb), -_block_diag(ct_im, nb)], axis=1)
    dsk = d_skip.reshape(nb, 1, LANES)
    w_bd = jnp.concatenate([_block_diag(glu_w[:, :, :SSM_CH], nb), _block_diag(glu_w[:, :, SSM_CH:], nb)], axis=-1)
    bias = jnp.concatenate([glu_b[:, :SSM_CH].reshape(nb, 1, LANES), glu_b[:, SSM_CH:].reshape(nb, 1, LANES)], axis=-1)
    return lam, b_bd.astype(BF16), c_bd.astype(BF16), dsk, w_bd.astype(BF16), bias


def _roll_rows(v, shift):
    return v if shift % SUBLANES == 0 else pltpu.roll(v, shift % SUBLANES, 0)


def _scan_chunk_rows(seq, nseq):
    return _tile(seq, max(8 * SSM_TIME_CHUNK // nseq, 8), max(SUBLANES // nseq, 1) * 8)


def _ssm_fwd(u, ops, nseq, name, comm=None):
    lam, b_bd, c_bd, dsk, w_bd, bias = ops
    rows_total, w = u.shape
    seq = rows_total // nseq
    nb = w // LANES
    s = STATES_PER_BLOCK
    tc = _scan_chunk_rows(seq, nseq)
    nk = seq // tc
    rows = tc * nseq
    stages = SUBLANES // nseq

    def body(u_ref, lam_ref, b_ref, c_ref, d_ref, w_ref, bias_ref, y_ref, hb_ref, buf, st, rbuf):
        k = pl.program_id(1)

        @pl.when(k == 0)
        def _():
            st[...] = jnp.zeros_like(st)

        hb_ref[...] = st[...]
        for q in range(nseq):
            rbuf[pl.ds(q, tc, stride=nseq), :] = u_ref[q]
        u = rbuf[...]
        buf[...] = jnp.dot(u.astype(BF16), b_ref[0], preferred_element_type=F32)
        lr = jnp.broadcast_to(lam_ref[0, :, 0:s], (SUBLANES, s))
        li = jnp.broadcast_to(lam_ref[0, :, s:2 * s], (SUBLANES, s))
        row = lax.broadcasted_iota(jnp.int32, (SUBLANES, s), 0)

        def step(i, carry):
            pr, pi = carry
            r0 = pl.multiple_of(i * SUBLANES, SUBLANES)
            br = buf[pl.ds(r0, SUBLANES), 0:s]
            bi = buf[pl.ds(r0, SUBLANES), s:2 * s]
            outr = outi = None
            for j in range(stages):
                rr = _roll_rows(pr, nseq)
                ri = _roll_rows(pi, nseq)
                pr = lr * rr - li * ri + br
                pi = lr * ri + li * rr + bi
                outr = pr if j == 0 else jnp.where(row >= j * nseq, pr, outr)
                outi = pi if j == 0 else jnp.where(row >= j * nseq, pi, outi)
            buf[pl.ds(r0, SUBLANES), 0:s] = outr
            buf[pl.ds(r0, SUBLANES), s:2 * s] = outi
            return outr, outi

        hr, hi = lax.fori_loop(0, rows // SUBLANES, step, (st[:, 0:s], st[:, s:2 * s]), unroll=2)
        st[:, 0:s] = hr
        st[:, s:2 * s] = hi
        y = jnp.dot(buf[...].astype(BF16), c_ref[0], preferred_element_type=F32) + d_ref[0] * u
        z = jnp.dot(_gelu(y).astype(BF16), w_ref[0], preferred_element_type=F32) + bias_ref[0]
        rbuf[...] = z[:, 0:LANES] * _sigmoid(z[:, LANES:2 * LANES])
        for q in range(nseq):
            y_ref[q] = rbuf[pl.ds(q, tc, stride=nseq), :]

    blk = lambda shape: pl.BlockSpec(shape, lambda b, k: (b, 0, 0))
    tok = pl.BlockSpec((nseq, tc, LANES), lambda b, k: (0, k, b))
    (y, hb), comm_outs = _call(
        body, name=name, grid=(nb, nk),
        in_specs=[tok, blk((1, 1, 2 * s)), blk((1, LANES, 2 * s)), blk((1, 2 * s, LANES)),
                  blk((1, 1, LANES)), blk((1, LANES, 2 * LANES)), blk((1, 1, 2 * LANES))],
        out_specs=[tok, pl.BlockSpec((SUBLANES, 2 * s), lambda b, k: (k, b))],
        out_shape=[jax.ShapeDtypeStruct((nseq, seq, w), F32),
                   jax.ShapeDtypeStruct((nk * SUBLANES, nb * 2 * s), F32)],
        scratch_shapes=[pltpu.VMEM((rows, 2 * s), F32), pltpu.VMEM((SUBLANES, 2 * s), F32),
                        pltpu.VMEM((rows, LANES), F32)],
        semantics=("parallel", "arbitrary"),
        args=(u.reshape(nseq, seq, w), lam, b_bd, c_bd, dsk, w_bd, bias), comm=comm)
    y = y.reshape(nseq * seq, w)
    return (y, hb) if comm is None else (y, hb, comm_outs)


def _scan_with(nblk, step, carry, between):
    runs = len(between)
    per = nblk // runs
    for i in range(runs):
        hi = nblk if i == runs - 1 else (i + 1) * per
        carry = lax.fori_loop(i * per, hi, step, carry, unroll=True)
        between[i]()
    return carry


def _ssm_fwd_pair(u, ops, nseq, name, comm=None):
    lam, b_bd, c_bd, dsk, w_bd, bias = ops
    rows_total, w = u.shape
    seq = rows_total // nseq
    nb = w // LANES
    s = STATES_PER_BLOCK
    tc = _scan_chunk_rows(seq, nseq)
    nk = seq // tc
    rows = tc * nseq
    nblk = rows // SUBLANES
    stages = SUBLANES // nseq
    two = 2 * LANES
    ncol = 4

    def body(u_ref, lam_ref, b_ref, c_ref, d_ref, w_ref, bias_ref, y_ref, hb_ref, buf_a, buf_b, st, rbuf_a, rbuf_b):
        k = pl.program_id(1)

        @pl.when(k == 0)
        def _():
            st[...] = jnp.zeros_like(st)

        hb_ref[...] = st[...]
        rbufs = (rbuf_a, rbuf_b)
        for q in range(nseq):
            for e in range(2):
                rbufs[e][pl.ds(q, tc, stride=nseq), :] = u_ref[q, :, e * LANES:(e + 1) * LANES]
        row = lax.broadcasted_iota(jnp.int32, (SUBLANES, s), 0)
        bufs = (buf_a, buf_b)

        def u_of(e):
            return rbufs[e][...]

        def project_in(e, j):
            cols = slice(j * (2 * s // ncol), (j + 1) * (2 * s // ncol))
            bufs[e][:, cols] = jnp.dot(u_of(e).astype(BF16), b_ref[e, :, cols], preferred_element_type=F32)

        def scan(e, between):
            buf = bufs[e]
            lr = jnp.broadcast_to(lam_ref[e, :, 0:s], (SUBLANES, s))
            li = jnp.broadcast_to(lam_ref[e, :, s:2 * s], (SUBLANES, s))

            def step(i, carry):
                pr, pi = carry
                r0 = pl.multiple_of(i * SUBLANES, SUBLANES)
                br = buf[pl.ds(r0, SUBLANES), 0:s]
                bi = buf[pl.ds(r0, SUBLANES), s:2 * s]
                outr = outi = None
                for j in range(stages):
                    rr = _roll_rows(pr, nseq)
                    ri = _roll_rows(pi, nseq)
                    pr = lr * rr - li * ri + br
                    pi = lr * ri + li * rr + bi
                    outr = pr if j == 0 else jnp.where(row >= j * nseq, pr, outr)
                    outi = pi if j == 0 else jnp.where(row >= j * nseq, pi, outi)
                buf[pl.ds(r0, SUBLANES), 0:s] = outr
                buf[pl.ds(r0, SUBLANES), s:2 * s] = outi
                return outr, outi

            lo = e * 2 * s
            hr, hi = _scan_with(nblk, step, (st[:, lo:lo + s], st[:, lo + s:lo + 2 * s]), between)
            st[:, lo:lo + s] = hr
            st[:, lo + s:lo + 2 * s] = hi

        part = {}

        def project_out(e, j):
            ks = slice(j * (2 * s // ncol), (j + 1) * (2 * s // ncol))
            p = jnp.dot(bufs[e][:, ks].astype(BF16), c_ref[e, ks, :], preferred_element_type=F32)
            part[e] = p if j == 0 else part[e] + p

        def finish(e):
            y = part[e] + d_ref[e] * u_of(e)
            z = jnp.dot(_gelu(y).astype(BF16), w_ref[e], preferred_element_type=F32) + bias_ref[e]
            part[e] = z[:, 0:LANES] * _sigmoid(z[:, LANES:two])

        nothing = lambda: None
        for j in range(ncol):
            project_in(0, j)
        scan(0, [functools.partial(project_in, 1, j) for j in range(ncol)])
        scan(1, [functools.partial(project_out, 0, j) for j in range(ncol)] + [functools.partial(finish, 0), nothing,
                                                                                nothing, nothing])
        for j in range(ncol):
            project_out(1, j)
        finish(1)
        for e in range(2):
            rbufs[e][...] = part[e]
            for q in range(nseq):
                y_ref[q, :, e * LANES:(e + 1) * LANES] = rbufs[e][pl.ds(q, tc, stride=nseq), :]

    blk = lambda shape: pl.BlockSpec(shape, lambda b, k: (b, 0, 0))
    tok = pl.BlockSpec((nseq, tc, two), lambda b, k: (0, k, b))
    (y, hb), comm_outs = _call(
        body, name=name, grid=(nb // 2, nk),
        in_specs=[tok, blk((2, 1, 2 * s)), blk((2, LANES, 2 * s)), blk((2, 2 * s, LANES)),
                  blk((2, 1, LANES)), blk((2, LANES, two)), blk((2, 1, two))],
        out_specs=[tok, pl.BlockSpec((SUBLANES, 4 * s), lambda b, k: (k, b))],
        out_shape=[jax.ShapeDtypeStruct((nseq, seq, w), F32),
                   jax.ShapeDtypeStruct((nk * SUBLANES, nb * 2 * s), F32)],
        scratch_shapes=[pltpu.VMEM((rows, 2 * s), F32), pltpu.VMEM((rows, 2 * s), F32),
                        pltpu.VMEM((SUBLANES, 4 * s), F32), pltpu.VMEM((rows, LANES), F32),
                        pltpu.VMEM((rows, LANES), F32)],
        semantics=("parallel", "arbitrary"),
        args=(u.reshape(nseq, seq, w), lam, b_bd, c_bd, dsk, w_bd, bias), comm=comm)
    y = y.reshape(nseq * seq, w)
    return (y, hb) if comm is None else (y, hb, comm_outs)


def _ssm_bwd_pair(u, dout, hb, ops, nseq, name, comm=None):
    lam, b_bd, c_bd, dsk, w_bd, bias = ops
    rows_total, w = u.shape
    seq = rows_total // nseq
    nb = w // LANES
    s = STATES_PER_BLOCK
    tc = _scan_chunk_rows(seq, nseq)
    nk = seq // tc
    rows = tc * nseq
    nblk = rows // SUBLANES
    stages = SUBLANES // nseq
    two = 2 * LANES
    ncol = 4
    cw = 2 * s // ncol
    tn_dims = (((0,), (0,)), ((), ()))
    nt_dims = (((1,), (1,)), ((), ()))

    def body(u_ref, dy_ref, hb_ref, lam_ref, b_ref, c_ref, d_ref, w_ref, bias_ref,
             du_ref, dlam_ref, db_ref, dct_ref, dd_ref, dw_ref, dbias_ref,
             hbuf_a, hbuf_b, gbuf_a, gbuf_b, gst, lacc, ru_a, ru_b, rd_a, rd_b):
        k = pl.program_id(1)

        @pl.when(k == 0)
        def _():
            gst[...] = jnp.zeros_like(gst)
            lacc[...] = jnp.zeros_like(lacc)
            db_ref[...] = jnp.zeros_like(db_ref)
            dct_ref[...] = jnp.zeros_like(dct_ref)
            dd_ref[...] = jnp.zeros_like(dd_ref)
            dw_ref[...] = jnp.zeros_like(dw_ref)
            dbias_ref[...] = jnp.zeros_like(dbias_ref)

        hbufs, gbufs, rus, rds = (hbuf_a, hbuf_b), (gbuf_a, gbuf_b), (ru_a, ru_b), (rd_a, rd_b)
        for q in range(nseq):
            for e in range(2):
                rus[e][pl.ds(q, tc, stride=nseq), :] = u_ref[q, :, e * LANES:(e + 1) * LANES]
                rds[e][pl.ds(q, tc, stride=nseq), :] = dy_ref[q, :, e * LANES:(e + 1) * LANES]
        row = lax.broadcasted_iota(jnp.int32, (SUBLANES, s), 0)
        cols = [slice(j * cw, (j + 1) * cw) for j in range(ncol)]
        val = [{}, {}]

        def lam_of(e):
            return (jnp.broadcast_to(lam_ref[e, :, 0:s], (SUBLANES, s)),
                    jnp.broadcast_to(lam_ref[e, :, s:2 * s], (SUBLANES, s)))

        def project_in(e, j):
            hbufs[e][:, cols[j]] = jnp.dot(rus[e][...].astype(BF16), b_ref[e, :, cols[j]], preferred_element_type=F32)

        def scan_fwd(e, between):
            buf = hbufs[e]
            lr, li = lam_of(e)

            def step(i, carry):
                pr, pi = carry
                r0 = pl.multiple_of(i * SUBLANES, SUBLANES)
                br = buf[pl.ds(r0, SUBLANES), 0:s]
                bi = buf[pl.ds(r0, SUBLANES), s:2 * s]
                outr = outi = None
                for j in range(stages):
                    rr = _roll_rows(pr, nseq)
                    ri = _roll_rows(pi, nseq)
                    pr = lr * rr - li * ri + br
                    pi = lr * ri + li * rr + bi
                    outr = pr if j == 0 else jnp.where(row >= j * nseq, pr, outr)
                    outi = pi if j == 0 else jnp.where(row >= j * nseq, pi, outi)
                buf[pl.ds(r0, SUBLANES), 0:s] = outr
                buf[pl.ds(r0, SUBLANES), s:2 * s] = outi
                return outr, outi

            lo = e * 2 * s
            _scan_with(nblk, step, (hb_ref[:, lo:lo + s], hb_ref[:, lo + s:lo + 2 * s]), between)

        def y_part(e, j):
            p = jnp.dot(hbufs[e][:, cols[j]].astype(BF16), c_ref[e, cols[j], :], preferred_element_type=F32)
            val[e]["y"] = p if j == 0 else val[e]["y"] + p

        def gate(e):
            v = val[e]
            uu = rus[e][...]
            yg, dyg_dy = _gelu_and_grad(v.pop("y") + d_ref[e] * uu)
            yg16 = yg.astype(BF16)
            z = jnp.dot(yg16, w_ref[e], preferred_element_type=F32) + bias_ref[e]
            sg = _sigmoid(z[:, LANES:two])
            dout_e = rds[e][...]
            dz = jnp.concatenate([dout_e * sg, dout_e * z[:, 0:LANES] * sg * (1.0 - sg)], axis=-1)
            dz16 = dz.astype(BF16)
            dw_ref[e] += lax.dot_general(yg16, dz16, tn_dims, preferred_element_type=F32)
            dbias_ref[e] += jnp.sum(dz, axis=0, keepdims=True)
            dy = lax.dot_general(dz16, w_ref[e], nt_dims, preferred_element_type=F32) * dyg_dy
            dd_ref[e] += jnp.sum(dy * uu, axis=0, keepdims=True)
            v["dy"] = dy
            v["dy16"] = dy.astype(BF16)

        def dc_part(e, j):
            dct_ref[e, :, cols[j]] += lax.dot_general(val[e]["dy16"], hbufs[e][:, cols[j]].astype(BF16), tn_dims,
                                                      preferred_element_type=F32)

        def dh_part(e, j):
            gbufs[e][:, cols[j]] = lax.dot_general(val[e]["dy16"], c_ref[e, cols[j], :], nt_dims,
                                                   preferred_element_type=F32)

        def scan_bwd(e, between):
            hbuf, gbuf = hbufs[e], gbufs[e]
            lr, li = lam_of(e)
            lo = e * 2 * s

            def step(i, carry):
                pr, pi, ar, ai = carry
                blk = nblk - 1 - i
                r0 = pl.multiple_of(blk * SUBLANES, SUBLANES)
                dr = gbuf[pl.ds(r0, SUBLANES), 0:s]
                di = gbuf[pl.ds(r0, SUBLANES), s:2 * s]
                outr = outi = None
                for j in reversed(range(stages)):
                    rr = _roll_rows(pr, SUBLANES - nseq)
                    ri = _roll_rows(pi, SUBLANES - nseq)
                    pr = dr + lr * rr + li * ri
                    pi = di - li * rr + lr * ri
                    outr = pr if j == stages - 1 else jnp.where(row < (j + 1) * nseq, pr, outr)
                    outi = pi if j == stages - 1 else jnp.where(row < (j + 1) * nseq, pi, outi)
                gbuf[pl.ds(r0, SUBLANES), 0:s] = outr
                gbuf[pl.ds(r0, SUBLANES), s:2 * s] = outi
                p0 = pl.multiple_of(jnp.maximum(blk - 1, 0) * SUBLANES, SUBLANES)
                first = blk == 0
                before_r = jnp.where(first, hb_ref[:, lo:lo + s], hbuf[pl.ds(p0, SUBLANES), 0:s])
                before_i = jnp.where(first, hb_ref[:, lo + s:lo + 2 * s], hbuf[pl.ds(p0, SUBLANES), s:2 * s])
                if stages > 1:
                    last_rows = row >= SUBLANES - nseq
                    before_r = _roll_rows(jnp.where(last_rows, before_r, hbuf[pl.ds(r0, SUBLANES), 0:s]), nseq)
                    before_i = _roll_rows(jnp.where(last_rows, before_i, hbuf[pl.ds(r0, SUBLANES), s:2 * s]), nseq)
                return (outr, outi, ar + outr * before_r + outi * before_i, ai - outr * before_i + outi * before_r)

            gr, gi, ar, ai = _scan_with(
                nblk, step, (gst[:, lo:lo + s], gst[:, lo + s:lo + 2 * s], lacc[:, lo:lo + s], lacc[:, lo + s:lo + 2 * s]),
                between)
            gst[:, lo:lo + s] = gr
            gst[:, lo + s:lo + 2 * s] = gi
            lacc[:, lo:lo + s] = ar
            lacc[:, lo + s:lo + 2 * s] = ai

        def du_part(e, j):
            p = lax.dot_general(gbufs[e][:, cols[j]].astype(BF16), b_ref[e, :, cols[j]], nt_dims,
                                preferred_element_type=F32)
            val[e]["du"] = (val[e].pop("dy") * d_ref[e] + p) if j == 0 else val[e]["du"] + p

        def db_part(e, j):
            db_ref[e, :, cols[j]] += lax.dot_general(rus[e][...].astype(BF16), gbufs[e][:, cols[j]].astype(BF16),
                                                     tn_dims, preferred_element_type=F32)

        def parts(fn, e):
            return [functools.partial(fn, e, j) for j in range(ncol)]

        nothing = lambda: None
        middle_of = lambda e: parts(y_part, e) + [functools.partial(gate, e)] + parts(dc_part, e) + parts(dh_part, e)
        last_of = lambda e: parts(du_part, e) + parts(db_part, e)
        for piece in parts(project_in, 0):
            piece()
        scan_fwd(0, parts(project_in, 1))
        scan_fwd(1, middle_of(0) + [nothing] * 3)
        scan_bwd(0, middle_of(1) + [nothing] * 3)
        scan_bwd(1, last_of(0))
        for piece in last_of(1):
            piece()
        for e in range(2):
            rus[e][...] = val[e]["du"]
            for q in range(nseq):
                du_ref[q, :, e * LANES:(e + 1) * LANES] = rus[e][pl.ds(q, tc, stride=nseq), :]

        @pl.when(k == nk - 1)
        def _():
            for e in range(2):
                dlam_ref[e] = jnp.sum(lacc[:, e * 2 * s:(e + 1) * 2 * s], axis=0, keepdims=True)

    blk = lambda shape: pl.BlockSpec(shape, lambda b, k: (b, 0, 0))
    tok = pl.BlockSpec((nseq, tc, two), lambda b, k: (0, nk - 1 - k, b))
    outs, comm_outs = _call(
        body, name=name, grid=(nb // 2, nk),
        in_specs=[tok, tok, pl.BlockSpec((SUBLANES, 4 * s), lambda b, k: (nk - 1 - k, b)),
                  blk((2, 1, 2 * s)), blk((2, LANES, 2 * s)), blk((2, 2 * s, LANES)),
                  blk((2, 1, LANES)), blk((2, LANES, two)), blk((2, 1, two))],
        out_specs=[tok, blk((2, 1, 2 * s)), blk((2, LANES, 2 * s)), blk((2, LANES, 2 * s)),
                   blk((2, 1, LANES)), blk((2, LANES, two)), blk((2, 1, two))],
        out_shape=[jax.ShapeDtypeStruct((nseq, seq, w), F32),
                   jax.ShapeDtypeStruct((nb, 1, 2 * s), F32), jax.ShapeDtypeStruct((nb, LANES, 2 * s), F32),
                   jax.ShapeDtypeStruct((nb, LANES, 2 * s), F32), jax.ShapeDtypeStruct((nb, 1, LANES), F32),
                   jax.ShapeDtypeStruct((nb, LANES, two), F32), jax.ShapeDtypeStruct((nb, 1, two), F32)],
        scratch_shapes=[pltpu.VMEM((rows, 2 * s), F32)] * 4
        + [pltpu.VMEM((SUBLANES, 4 * s), F32), pltpu.VMEM((SUBLANES, 4 * s), F32)]
        + [pltpu.VMEM((rows, LANES), F32)] * 4,
        semantics=("parallel", "arbitrary"),
        args=(u.reshape(nseq, seq, w), dout.reshape(nseq, seq, w), hb, lam, b_bd, c_bd, dsk, w_bd, bias), comm=comm)
    outs[0] = outs[0].reshape(nseq * seq, w)
    return outs if comm is None else (outs, comm_outs)


def _ssm_bwd(u, dout, hb, ops, nseq, name, comm=None):
    lam, b_bd, c_bd, dsk, w_bd, bias = ops
    rows_total, w = u.shape
    seq = rows_total // nseq
    nb = w // LANES
    s = STATES_PER_BLOCK
    tc = _scan_chunk_rows(seq, nseq)
    nk = seq // tc
    rows = tc * nseq
    nblk = rows // SUBLANES
    stages = SUBLANES // nseq
    tn_dims = (((0,), (0,)), ((), ()))
    nt_dims = (((1,), (1,)), ((), ()))

    def body(u_ref, dy_ref, hb_ref, lam_ref, b_ref, c_ref, d_ref, w_ref, bias_ref,
             du_ref, dlam_ref, db_ref, dct_ref, dd_ref, dw_ref, dbias_ref, hbuf, gbuf, gst, lacc, rbuf, rbuf2):
        k = pl.program_id(1)

        @pl.when(k == 0)
        def _():
            gst[...] = jnp.zeros_like(gst)
            lacc[...] = jnp.zeros_like(lacc)
            db_ref[...] = jnp.zeros_like(db_ref)
            dct_ref[...] = jnp.zeros_like(dct_ref)
            dd_ref[...] = jnp.zeros_like(dd_ref)
            dw_ref[...] = jnp.zeros_like(dw_ref)
            dbias_ref[...] = jnp.zeros_like(dbias_ref)

        for q in range(nseq):
            rbuf[pl.ds(q, tc, stride=nseq), :] = u_ref[q]
            rbuf2[pl.ds(q, tc, stride=nseq), :] = dy_ref[q]
        u = rbuf[...]
        ub = u.astype(BF16)
        lr = jnp.broadcast_to(lam_ref[0, :, 0:s], (SUBLANES, s))
        li = jnp.broadcast_to(lam_ref[0, :, s:2 * s], (SUBLANES, s))
        row = lax.broadcasted_iota(jnp.int32, (SUBLANES, s), 0)
        hbuf[...] = jnp.dot(ub, b_ref[0], preferred_element_type=F32)

        def fstep(i, carry):
            pr, pi = carry
            r0 = pl.multiple_of(i * SUBLANES, SUBLANES)
            br = hbuf[pl.ds(r0, SUBLANES), 0:s]
            bi = hbuf[pl.ds(r0, SUBLANES), s:2 * s]
            outr = outi = None
            for j in range(stages):
                rr = _roll_rows(pr, nseq)
                ri = _roll_rows(pi, nseq)
                pr = lr * rr - li * ri + br
                pi = lr * ri + li * rr + bi
                outr = pr if j == 0 else jnp.where(row >= j * nseq, pr, outr)
                outi = pi if j == 0 else jnp.where(row >= j * nseq, pi, outi)
            hbuf[pl.ds(r0, SUBLANES), 0:s] = outr
            hbuf[pl.ds(r0, SUBLANES), s:2 * s] = outi
            return outr, outi

        lax.fori_loop(0, nblk, fstep, (hb_ref[:, 0:s], hb_ref[:, s:2 * s]), unroll=2)
        hb16 = hbuf[...].astype(BF16)
        y = jnp.dot(hb16, c_ref[0], preferred_element_type=F32) + d_ref[0] * u
        yg, dyg_dy = _gelu_and_grad(y)
        yg16 = yg.astype(BF16)
        z = jnp.dot(yg16, w_ref[0], preferred_element_type=F32) + bias_ref[0]
        z1 = z[:, 0:LANES]
        sg = _sigmoid(z[:, LANES:2 * LANES])
        dout = rbuf2[...]
        dz = jnp.concatenate([dout * sg, dout * z1 * sg * (1.0 - sg)], axis=-1)
        dz16 = dz.astype(BF16)
        dw_ref[0] += lax.dot_general(yg16, dz16, tn_dims, preferred_element_type=F32)
        dbias_ref[0] += jnp.sum(dz, axis=0, keepdims=True)
        dy = lax.dot_general(dz16, w_ref[0], nt_dims, preferred_element_type=F32) * dyg_dy
        dy16 = dy.astype(BF16)
        dd_ref[0] += jnp.sum(dy * u, axis=0, keepdims=True)
        dct_ref[0] += lax.dot_general(dy16, hb16, tn_dims, preferred_element_type=F32)
        gbuf[...] = lax.dot_general(dy16, c_ref[0], nt_dims, preferred_element_type=F32)

        def bstep(i, carry):
            pr, pi, ar, ai = carry
            blk = nblk - 1 - i
            r0 = pl.multiple_of(blk * SUBLANES, SUBLANES)
            dr = gbuf[pl.ds(r0, SUBLANES), 0:s]
            di = gbuf[pl.ds(r0, SUBLANES), s:2 * s]
            outr = outi = None
            for j in reversed(range(stages)):
                rr = _roll_rows(pr, SUBLANES - nseq)
                ri = _roll_rows(pi, SUBLANES - nseq)
                pr = dr + lr * rr + li * ri
                pi = di - li * rr + lr * ri
                outr = pr if j == stages - 1 else jnp.where(row < (j + 1) * nseq, pr, outr)
                outi = pi if j == stages - 1 else jnp.where(row < (j + 1) * nseq, pi, outi)
            gbuf[pl.ds(r0, SUBLANES), 0:s] = outr
            gbuf[pl.ds(r0, SUBLANES), s:2 * s] = outi
            p0 = pl.multiple_of(jnp.maximum(blk - 1, 0) * SUBLANES, SUBLANES)
            first = blk == 0
            before_r = jnp.where(first, hb_ref[:, 0:s], hbuf[pl.ds(p0, SUBLANES), 0:s])
            before_i = jnp.where(first, hb_ref[:, s:2 * s], hbuf[pl.ds(p0, SUBLANES), s:2 * s])
            if stages > 1:
                last_rows = row >= SUBLANES - nseq
                before_r = _roll_rows(jnp.where(last_rows, before_r, hbuf[pl.ds(r0, SUBLANES), 0:s]), nseq)
                before_i = _roll_rows(jnp.where(last_rows, before_i, hbuf[pl.ds(r0, SUBLANES), s:2 * s]), nseq)
            return (outr, outi, ar + outr * before_r + outi * before_i, ai - outr * before_i + outi * before_r)

        gr, gi, ar, ai = lax.fori_loop(
            0, nblk, bstep, (gst[:, 0:s], gst[:, s:2 * s], lacc[:, 0:s], lacc[:, s:2 * s]))
        gst[:, 0:s] = gr
        gst[:, s:2 * s] = gi
        lacc[:, 0:s] = ar
        lacc[:, s:2 * s] = ai
        g16 = gbuf[...].astype(BF16)
        rbuf[...] = dy * d_ref[0] + lax.dot_general(g16, b_ref[0], nt_dims, preferred_element_type=F32)
        for q in range(nseq):
            du_ref[q] = rbuf[pl.ds(q, tc, stride=nseq), :]
        db_ref[0] += lax.dot_general(ub, g16, tn_dims, preferred_element_type=F32)

        @pl.when(k == nk - 1)
        def _():
            dlam_ref[0] = jnp.sum(lacc[...], axis=0, keepdims=True)

    blk = lambda shape: pl.BlockSpec(shape, lambda b, k: (b, 0, 0))
    rev = lambda b, k: (nk - 1 - k, b)
    tok = pl.BlockSpec((nseq, tc, LANES), lambda b, k: (0, nk - 1 - k, b))
    outs, comm_outs = _call(
        body, name=name, grid=(nb, nk),
        in_specs=[tok, tok, pl.BlockSpec((SUBLANES, 2 * s), rev),
                  blk((1, 1, 2 * s)), blk((1, LANES, 2 * s)), blk((1, 2 * s, LANES)),
                  blk((1, 1, LANES)), blk((1, LANES, 2 * LANES)), blk((1, 1, 2 * LANES))],
        out_specs=[tok, blk((1, 1, 2 * s)), blk((1, LANES, 2 * s)), blk((1, LANES, 2 * s)),
                   blk((1, 1, LANES)), blk((1, LANES, 2 * LANES)), blk((1, 1, 2 * LANES))],
        out_shape=[jax.ShapeDtypeStruct((nseq, seq, w), F32),
                   jax.ShapeDtypeStruct((nb, 1, 2 * s), F32), jax.ShapeDtypeStruct((nb, LANES, 2 * s), F32),
                   jax.ShapeDtypeStruct((nb, LANES, 2 * s), F32), jax.ShapeDtypeStruct((nb, 1, LANES), F32),
                   jax.ShapeDtypeStruct((nb, LANES, 2 * LANES), F32), jax.ShapeDtypeStruct((nb, 1, 2 * LANES), F32)],
        scratch_shapes=[pltpu.VMEM((rows, 2 * s), F32), pltpu.VMEM((rows, 2 * s), F32),
                        pltpu.VMEM((SUBLANES, 2 * s), F32), pltpu.VMEM((SUBLANES, 2 * s), F32),
                        pltpu.VMEM((rows, LANES), F32), pltpu.VMEM((rows, LANES), F32)],
        semantics=("parallel", "arbitrary"),
        args=(u.reshape(nseq, seq, w), dout.reshape(nseq, seq, w), hb, lam, b_bd, c_bd, dsk, w_bd, bias), comm=comm)
    outs[0] = outs[0].reshape(nseq * seq, w)
    return outs if comm is None else (outs, comm_outs)


ANY = pl.BlockSpec(memory_space=pl.ANY)

BIG = (("ffn1_w_in", True), ("ffn1_w_out", False), ("mix_w_in", True), ("mix_w_out", False),
       ("ffn2_w_in", True), ("ffn2_w_out", False))


def _my_place():
    return lax.axis_index("x"), lax.axis_index("y"), lax.axis_index("c")


def _other_chips(x, y):
    return [(1 - x, y), (x, 1 - y), (1 - x, 1 - y)]


def _half_of_shard(ref, col_sharded, chip, core):
    full_rows, full_cols = ref.shape
    if col_sharded:
        hr, cs = full_rows // 2, full_cols // N_CHIPS
        return ref.at[pl.ds(pl.multiple_of(core * hr, 8), hr), pl.ds(chip * cs, cs)]
    rs = full_rows // N_CHIPS
    return ref.at[pl.ds(pl.multiple_of(chip * rs + core * (rs // 2), 8), rs // 2), :]


def _gather_comm(shards, cols):
    full_shapes = [(sh.shape[0], sh.shape[1] * N_CHIPS) if col else (sh.shape[0] * N_CHIPS, sh.shape[1])
                   for sh, col in zip(shards, cols)]
    nw = len(shards)

    def first_copies(ins, outs, sems):
        send_sems, recv_sems, local_sems = sems
        x, y, c = _my_place()
        me = 2 * x + y
        locals_, sends = [], []
        for wi in range(nw):
            src, dst = ins[wi], outs[wi]
            rs, cs = src.shape
            hs = rs // 2
            if cols[wi]:
                place = dst.at[:, pl.ds(me * cs, cs)]
            else:
                place = dst.at[pl.ds(pl.multiple_of(me * rs, 8), rs), :]
            locals_.append(pltpu.make_async_copy(src, place, local_sems.at[wi]))
            my_half = src.at[pl.ds(pl.multiple_of(c * hs, 8), hs), :]
            for j, (px, py) in enumerate(_other_chips(x, y)):
                sends.append(pltpu.make_async_remote_copy(
                    src_ref=my_half, dst_ref=_half_of_shard(dst, cols[wi], me, c),
                    send_sem=send_sems.at[wi * 6 + j], recv_sem=recv_sems.at[wi * 6 + j],
                    device_id=(px, py, c), device_id_type=MESH))
        return locals_, sends

    def start(ins, outs, sems):
        locals_, sends = first_copies(ins, outs, sems)
        for cp in locals_ + sends:
            cp.start()

    def forwards(outs, sems, wait_landed):
        send_sems, recv_sems, _ = sems
        x, y, c = _my_place()
        out = []
        for wi in range(nw):
            dst = outs[wi]
            for j, (px, py) in enumerate(_other_chips(x, y)):
                got = _half_of_shard(dst, cols[wi], 2 * px + py, c)
                if wait_landed:
                    pltpu.make_async_remote_copy(
                        src_ref=got, dst_ref=got, send_sem=send_sems.at[wi * 6 + j], recv_sem=recv_sems.at[wi * 6 + j],
                        device_id=(px, py, c), device_id_type=MESH).wait_recv()
                out.append(pltpu.make_async_remote_copy(
                    src_ref=got, dst_ref=got, send_sem=send_sems.at[wi * 6 + 3 + j], recv_sem=recv_sems.at[wi * 6 + 3 + j],
                    device_id=(x, y, 1 - c), device_id_type=MESH))
                if wait_landed:
                    out[-1].start()
        return out

    def middle(ins, outs, sems):
        forwards(outs, sems, True)

    def finish(ins, outs, sems):
        send_sems, recv_sems, _ = sems
        x, y, c = _my_place()
        locals_, sends = first_copies(ins, outs, sems)
        for wi in range(nw):
            dst = outs[wi]
            for j, (px, py) in enumerate(_other_chips(x, y)):
                theirs = _half_of_shard(dst, cols[wi], 2 * px + py, 1 - c)
                pltpu.make_async_remote_copy(
                    src_ref=theirs, dst_ref=theirs, send_sem=send_sems.at[wi * 6 + 3 + j],
                    recv_sem=recv_sems.at[wi * 6 + 3 + j], device_id=(x, y, 1 - c), device_id_type=MESH).wait_recv()
        for cp in sends + forwards(outs, sems, False):
            cp.wait_send()
        for cp in locals_:
            cp.wait()

    return _Comm(shards, [jax.ShapeDtypeStruct(s, BF16) for s in full_shapes],
                 [pltpu.SemaphoreType.DMA((6 * nw,)), pltpu.SemaphoreType.DMA((6 * nw,)),
                  pltpu.SemaphoreType.DMA((nw,))], start, finish, middle=middle)


def _pair_exchange_comm(grads, cols):
    nw = len(grads)
    n_copies = sum(1 if col else N_CHIPS for col in cols)

    def copies(ins, outs, sems):
        send_sems, recv_sems = sems
        x, y, c = _my_place()
        out = []
        for wi in range(nw):
            src, dst = ins[wi], outs[wi]
            fr = src.shape[0]
            if cols[wi]:
                hr = fr // 2
                pieces = [(src.at[pl.ds(pl.multiple_of((1 - c) * hr, 8), hr), :], dst)]
            else:
                rs = fr // N_CHIPS
                hs = rs // 2
                pieces = [(src.at[pl.ds(pl.multiple_of(k * rs + (1 - c) * hs, 8), hs), :],
                           dst.at[pl.ds(k * hs, hs), :]) for k in range(N_CHIPS)]
            for s_ref, d_ref in pieces:
                out.append(pltpu.make_async_remote_copy(
                    src_ref=s_ref, dst_ref=d_ref, send_sem=send_sems.at[len(out)], recv_sem=recv_sems.at[len(out)],
                    device_id=(x, y, 1 - c), device_id_type=MESH))
        return out

    def start(ins, outs, sems):
        for cp in copies(ins, outs, sems):
            cp.start()

    def finish(ins, outs, sems):
        for cp in copies(ins, outs, sems):
            cp.wait()

    return _Comm(grads, [jax.ShapeDtypeStruct((g.shape[0] // 2, g.shape[1]), F32) for g in grads],
                 [pltpu.SemaphoreType.DMA((n_copies,)), pltpu.SemaphoreType.DMA((n_copies,))], start, finish)


def _pair_sum(grad, other, col, core, name):
    fr, fc = grad.shape
    pieces = 1 if col else N_CHIPS
    pr = fr // 2 // pieces
    gview = grad.reshape(pieces * 2, pr, fc)
    oview = other.reshape(pieces, pr, fc)
    tr = _tile(pr, 256, 16)

    def body(c_ref, g_ref, o_ref, out_ref):
        out_ref[...] = (g_ref[...] + o_ref[...]).astype(out_ref.dtype)

    out = pl.pallas_call(
        body, name=name,
        grid_spec=pltpu.PrefetchScalarGridSpec(
            num_scalar_prefetch=1, grid=(pieces, pr // tr),
            in_specs=[pl.BlockSpec((1, tr, fc), lambda p, i, cref: (p * 2 + cref[0], i, 0)),
                      pl.BlockSpec((1, tr, fc), lambda p, i, cref: (p, i, 0))],
            out_specs=pl.BlockSpec((1, tr, fc), lambda p, i, cref: (p, i, 0))),
        out_shape=jax.ShapeDtypeStruct((pieces, pr, fc), BF16),
        compiler_params=_params("parallel", "parallel"),
    )(core, gview, oview)
    return out.reshape(fr // 2, fc)


def _chip_exchange_comm(psums, cols):
    nw = len(psums)
    out_shapes = [(N_CHIPS, p.shape[0], p.shape[1] // N_CHIPS) if col else (N_CHIPS, p.shape[0] // N_CHIPS, p.shape[1])
                  for p, col in zip(psums, cols)]

    def copies(ins, outs, sems):
        send_sems, recv_sems, local_sems = sems
        x, y, c = _my_place()
        me = 2 * x + y
        out = []
        for wi in range(nw):
            src = ins[wi]
            mine = outs[wi].at[me]

            def piece(chip, src=src, col=cols[wi]):
                if col:
                    cs = src.shape[1] // N_CHIPS
                    return src.at[:, pl.ds(chip * cs, cs)]
                ps = src.shape[0] // N_CHIPS
                return src.at[pl.ds(pl.multiple_of(chip * ps, 8), ps), :]

            out.append(pltpu.make_async_copy(piece(me), mine, local_sems.at[wi]))
            for j, (px, py) in enumerate(_other_chips(x, y)):
                out.append(pltpu.make_async_remote_copy(
                    src_ref=piece(2 * px + py), dst_ref=mine,
                    send_sem=send_sems.at[wi * 3 + j], recv_sem=recv_sems.at[wi * 3 + j],
                    device_id=(px, py, c), device_id_type=MESH))
        return out

    def start(ins, outs, sems):
        for cp in copies(ins, outs, sems):
            cp.start()

    def finish(ins, outs, sems):
        for cp in copies(ins, outs, sems):
            cp.wait()

    return _Comm(psums, [jax.ShapeDtypeStruct(s, BF16) for s in out_shapes],
                 [pltpu.SemaphoreType.DMA((3 * nw,)), pltpu.SemaphoreType.DMA((3 * nw,)),
                  pltpu.SemaphoreType.DMA((nw,))], start, finish)


def _chip_sum(slots, core, layer, layers, into, name):
    _, hr, cs = slots.shape
    tr = _tile(hr, 256, 16)

    def body(c_ref, s_ref, *rest):
        out_ref = rest[-1]
        acc = s_ref[0].astype(F32)
        for i in range(1, N_CHIPS):
            acc = acc + s_ref[i].astype(F32)
        out_ref[0] = acc

    in_specs = [pl.BlockSpec((N_CHIPS, tr, cs), lambda i, cref: (0, i, 0))]
    args = [core, slots]
    aliases = {}
    if into is not None:
        in_specs.append(pl.BlockSpec(memory_space=pl.ANY))
        args.append(into.reshape(layers * 2, hr, cs))
        aliases = {2: 0}
    out = pl.pallas_call(
        body, name=name,
        grid_spec=pltpu.PrefetchScalarGridSpec(
            num_scalar_prefetch=1, grid=(hr // tr,), in_specs=in_specs,
            out_specs=pl.BlockSpec((1, tr, cs), lambda i, cref: (layer * 2 + cref[0], i, 0))),
        out_shape=jax.ShapeDtypeStruct((layers * 2, hr, cs), F32),
        input_output_aliases=aliases,
        compiler_params=_params("parallel"),
    )(*args)
    return out.reshape(layers, 2 * hr, cs)


def _pair_share_comm(reduced):
    nw = len(reduced)

    def copies(ins, outs, sems):
        send_sems, recv_sems = sems
        x, y, c = _my_place()
        out = []
        for wi in range(nw):
            hs = outs[wi].shape[1] // 2
            mine = outs[wi].at[:, pl.ds(pl.multiple_of(c * hs, 8), hs), :]
            out.append(pltpu.make_async_remote_copy(
                src_ref=mine, dst_ref=mine, send_sem=send_sems.at[wi], recv_sem=recv_sems.at[wi],
                device_id=(x, y, 1 - c), device_id_type=MESH))
        return out

    def start(ins, outs, sems):
        for cp in copies(ins, outs, sems):
            cp.start()

    def finish(ins, outs, sems):
        for cp in copies(ins, outs, sems):
            cp.wait()

    return _Comm(reduced, [jax.ShapeDtypeStruct(r.shape, F32) for r in reduced],
                 [pltpu.SemaphoreType.DMA((nw,)), pltpu.SemaphoreType.DMA((nw,))], start, finish,
                 alias={i: i for i in range(nw)})


def _all_reduce_small(flat, comm):
    rows, lanes = flat.shape
    seg = rows // N_DEV
    c_in, c_out = len(comm.ins), len(comm.outs)

    def body(*refs):
        refs = list(refs)
        in_ref, cins = refs[0], refs[1:1 + c_in]
        out_ref, couts = refs[1 + c_in], refs[2 + c_in:2 + c_in + c_out]
        recv_ref, send_sems, recv_sems = refs[2 + c_in + c_out:5 + c_in + c_out]
        csems = refs[5 + c_in + c_out:]
        comm.start(cins, couts, csems)
        x, y, c = _my_place()
        me = 4 * x + 2 * y + c

        def peer(r):
            fx, fy, fc = (r >> 2) & 1, (r >> 1) & 1, r & 1
            px = jnp.where(fx == 1, 1 - x, x)
            py = jnp.where(fy == 1, 1 - y, y)
            pc = jnp.where(fc == 1, 1 - c, c)
            return px, py, pc

        first = []
        for r in range(1, N_DEV):
            px, py, pc = peer(r)
            theirs = in_ref.at[pl.ds(pl.multiple_of((4 * px + 2 * py + pc) * seg, 8), seg), :]
            cp = pltpu.make_async_remote_copy(
                src_ref=theirs, dst_ref=recv_ref.at[r], send_sem=send_sems.at[r - 1], recv_sem=recv_sems.at[r - 1],
                device_id=(px, py, pc), device_id_type=MESH)
            cp.start()
            first.append(cp)
        for cp in first:
            cp.wait()
        my_rows = pl.ds(pl.multiple_of(me * seg, 8), seg)
        acc = in_ref[my_rows, :]
        for r in range(1, N_DEV):
            acc = acc + recv_ref[r]
        out_ref[my_rows, :] = acc
        second = []
        for r in range(1, N_DEV):
            px, py, pc = peer(r)
            cp = pltpu.make_async_remote_copy(
                src_ref=out_ref.at[my_rows, :], dst_ref=out_ref.at[my_rows, :],
                send_sem=send_sems.at[6 + r], recv_sem=recv_sems.at[6 + r],
                device_id=(px, py, pc), device_id_type=MESH)
            cp.start()
            second.append(cp)
        for r in range(1, N_DEV):
            px, py, pc = peer(r)
            theirs = out_ref.at[pl.ds(pl.multiple_of((4 * px + 2 * py + pc) * seg, 8), seg), :]
            pltpu.make_async_remote_copy(
                src_ref=theirs, dst_ref=theirs, send_sem=send_sems.at[6 + r], recv_sem=recv_sems.at[6 + r],
                device_id=(px, py, pc), device_id_type=MESH).wait_recv()
        for cp in second:
            cp.wait_send()
        comm.finish(cins, couts, csems)

    vm = pl.BlockSpec(memory_space=pltpu.VMEM)
    any_spec = pl.BlockSpec(memory_space=pl.ANY)
    outs = pl.pallas_call(
        body, name="all_reduce_small",
        in_specs=[vm] + [any_spec] * c_in, out_specs=[vm] + [any_spec] * c_out,
        out_shape=[jax.ShapeDtypeStruct((rows, lanes), F32)] + comm.outs,
        scratch_shapes=[pltpu.VMEM((N_DEV, seg, lanes), F32),
                        pltpu.SemaphoreType.DMA((2 * (N_DEV - 1),)), pltpu.SemaphoreType.DMA((2 * (N_DEV - 1),))]
        + comm.sems,
        input_output_aliases={1 + ci: 1 + co for ci, co in comm.alias.items()},
        compiler_params=pltpu.CompilerParams(vmem_limit_bytes=VMEM_LIMIT),
    )(flat, *comm.ins)
    return outs[0], list(outs[1:])


def _adamw_update(w_ref, g_ref, m_ref, v_ref, d_ref, nm_ref, nv_ref):
    c1 = 1.0 - ADAM_B1 ** ADAM_STEP
    c2 = 1.0 - ADAM_B2 ** ADAM_STEP
    gv = g_ref[...]
    nm = ADAM_B1 * m_ref[...] + (1.0 - ADAM_B1) * gv
    nv = ADAM_B2 * v_ref[...] + (1.0 - ADAM_B2) * (gv * gv)
    d_ref[...] = -ADAM_LR * ((nm / c1) / (jnp.sqrt(nv / c2) + ADAM_EPS) + ADAM_WD * w_ref[...])
    nm_ref[...] = nm
    nv_ref[...] = nv


def _adamw_many(ws, gs, ms, vs, name):
    n = len(ws)

    def body(*refs):
        for i in range(n):
            _adamw_update(*[refs[k * n + i] for k in range(7)])

    shapes = [jax.ShapeDtypeStruct(w.shape, F32) for w in ws]
    outs = pl.pallas_call(
        body, name=name, out_shape=shapes * 3,
        compiler_params=pltpu.CompilerParams(vmem_limit_bytes=VMEM_LIMIT),
    )(*ws, *gs, *ms, *vs)
    return outs[:n], outs[n:2 * n], outs[2 * n:]


def _adamw(w, g, m, v, name):
    rows, cols = w.shape
    tr = _tile(rows, 256, 8)

    def body(w_ref, g_ref, m_ref, v_ref, go_ref, d_ref, nm_ref, nv_ref):
        go_ref[...] = g_ref[...]
        _adamw_update(w_ref, g_ref, m_ref, v_ref, d_ref, nm_ref, nv_ref)

    blk = pl.BlockSpec((tr, cols), lambda i: (i, 0))
    sds = jax.ShapeDtypeStruct((rows, cols), F32)
    return pl.pallas_call(
        body, name=name, grid=(rows // tr,),
        in_specs=[blk] * 4, out_specs=[blk] * 4, out_shape=[sds] * 4,
        compiler_params=_params("parallel"),
    )(w, g, m, v)


SMALL = ("norm_ffn1", "norm_mix", "ssm_a_re", "ssm_a_im", "ssm_log_dt", "ssm_b_re", "ssm_b_im", "ssm_c_re",
         "ssm_c_im", "ssm_d", "ssm_glu_w", "ssm_glu_b", "gm_v_gain", "gm_w_s", "gm_b_s", "gain_ssm_out",
         "gain_gm_out", "norm_ffn2", "norm_final")
WEIGHTS = ("norm_ffn1", "ffn1_w_in", "ffn1_w_out", "norm_mix", "mix_w_in", "ssm_a_re", "ssm_a_im", "ssm_log_dt",
           "ssm_b_re", "ssm_b_im", "ssm_c_re", "ssm_c_im", "ssm_d", "ssm_glu_w", "ssm_glu_b", "gm_v_gain", "gm_w_s",
           "gm_b_s", "gain_ssm_out", "gain_gm_out", "mix_w_out", "norm_ffn2", "ffn2_w_in", "ffn2_w_out", "norm_final")


def _ffn_fwd(x, gain, w_in, w_out, tag, hosted=None):
    if hosted is None:
        h, t, q, a = _ffn_in_fwd(x, gain, w_in, f"{tag}_in")
    else:
        (h, t, q, a), got = _ffn_in_fwd(x, gain, w_in, f"{tag}_in_hosting", comm=hosted[0]())
        hosted[1](got)
    if callable(w_out):
        w_out = w_out()
    out = _matmul(a, w_out, "nn", scale=0.5, res=x, tm=512, tn=1024, tk=4096, name=f"{tag}_out")
    return out, (x, h, t, q, a)


def _ffn_bwd(dout, saved, gain, w_in, w_out, tag, hooks=None, publish=None, late_out_dw=False):
    x, h, t, q, a = saved
    f = t.shape[1]
    hooks = hooks or {}

    def hosted(key, fn, *args, name, **kw):
        if key not in hooks:
            return fn(*args, name=name, **kw)
        make, take = hooks[key]
        *res, got = fn(*args, name=f"{name}_hosting", comm=make(), **kw)
        take(got)
        return res[0] if len(res) == 1 else tuple(res)

    def out_dw():
        dw = hosted("out_dw", _matmul, a, dout, "tn", scale=0.5, tm=1536, tn=1024, tk=2048, name=f"{tag}_out_dw")
        if publish is not None:
            publish("out", dw)
        return dw

    dg, du = hosted("out_dx", _ffn_out_bwd, dout, w_out, t, q, name=f"{tag}_out_dx")
    if not late_out_dw:
        dw_out = out_dw()
    dw_in = hosted("in_dw_g", _matmul, h, dg, "tn", tm=1024, tn=1536, tk=2048, name=f"{tag}_in_dw_g",
                   out_cols=2 * f)
    dw_in = hosted("in_dw_u", _matmul, h, du, "tn", tm=1024, tn=1536, tk=2048, name=f"{tag}_in_dw_u",
                   out_cols=2 * f, col_off=f, into=dw_in)
    if publish is not None:
        publish("in", dw_in)
    if late_out_dw:
        dw_out = out_dw()
    dx, dgain = hosted("in_dx", _proj_in_bwd, [(dg, 0), (du, f)], w_in, x, gain, dout, name=f"{tag}_in_dx")
    return dx, dgain, dw_in, dw_out


def kernel(x, norm_ffn1, ffn1_w_in, ffn1_w_out, norm_mix, mix_w_in, ssm_a_re, ssm_a_im, ssm_log_dt, ssm_b_re, ssm_b_im, ssm_c_re, ssm_c_im, ssm_d, ssm_glu_w, ssm_glu_b, gm_v_gain, gm_w_s, gm_b_s, gain_ssm_out, gain_gm_out, mix_w_out, norm_ffn2, ffn2_w_in, ffn2_w_out, norm_final, loss_target, m_norm_ffn1, m_ffn1_w_in, m_ffn1_w_out, m_norm_mix, m_mix_w_in, m_ssm_a_re, m_ssm_a_im, m_ssm_log_dt, m_ssm_b_re, m_ssm_b_im, m_ssm_c_re, m_ssm_c_im, m_ssm_d, m_ssm_glu_w, m_ssm_glu_b, m_gm_v_gain, m_gm_w_s, m_gm_b_s, m_gain_ssm_out, m_gain_gm_out, m_mix_w_out, m_norm_ffn2, m_ffn2_w_in, m_ffn2_w_out, m_norm_final, v_norm_ffn1, v_ffn1_w_in, v_ffn1_w_out, v_norm_mix, v_mix_w_in, v_ssm_a_re, v_ssm_a_im, v_ssm_log_dt, v_ssm_b_re, v_ssm_b_im, v_ssm_c_re, v_ssm_c_im, v_ssm_d, v_ssm_glu_w, v_ssm_glu_b, v_gm_v_gain, v_gm_w_s, v_gm_b_s, v_gain_ssm_out, v_gain_gm_out, v_mix_w_out, v_norm_ffn2, v_ffn2_w_in, v_ffn2_w_out, v_norm_final):
    wts = dict(norm_ffn1=norm_ffn1, ffn1_w_in=ffn1_w_in, ffn1_w_out=ffn1_w_out, norm_mix=norm_mix, mix_w_in=mix_w_in,
               ssm_a_re=ssm_a_re, ssm_a_im=ssm_a_im, ssm_log_dt=ssm_log_dt, ssm_b_re=ssm_b_re, ssm_b_im=ssm_b_im,
               ssm_c_re=ssm_c_re, ssm_c_im=ssm_c_im, ssm_d=ssm_d, ssm_glu_w=ssm_glu_w, ssm_glu_b=ssm_glu_b,
               gm_v_gain=gm_v_gain, gm_w_s=gm_w_s, gm_b_s=gm_b_s, gain_ssm_out=gain_ssm_out, gain_gm_out=gain_gm_out,
               mix_w_out=mix_w_out, norm_ffn2=norm_ffn2, ffn2_w_in=ffn2_w_in, ffn2_w_out=ffn2_w_out,
               norm_final=norm_final)
    mom = dict(norm_ffn1=m_norm_ffn1, ffn1_w_in=m_ffn1_w_in, ffn1_w_out=m_ffn1_w_out, norm_mix=m_norm_mix,
               mix_w_in=m_mix_w_in, ssm_a_re=m_ssm_a_re, ssm_a_im=m_ssm_a_im, ssm_log_dt=m_ssm_log_dt,
               ssm_b_re=m_ssm_b_re, ssm_b_im=m_ssm_b_im, ssm_c_re=m_ssm_c_re, ssm_c_im=m_ssm_c_im, ssm_d=m_ssm_d,
               ssm_glu_w=m_ssm_glu_w, ssm_glu_b=m_ssm_glu_b, gm_v_gain=m_gm_v_gain, gm_w_s=m_gm_w_s, gm_b_s=m_gm_b_s,
               gain_ssm_out=m_gain_ssm_out, gain_gm_out=m_gain_gm_out, mix_w_out=m_mix_w_out, norm_ffn2=m_norm_ffn2,
               ffn2_w_in=m_ffn2_w_in, ffn2_w_out=m_ffn2_w_out, norm_final=m_norm_final)
    var = dict(norm_ffn1=v_norm_ffn1, ffn1_w_in=v_ffn1_w_in, ffn1_w_out=v_ffn1_w_out, norm_mix=v_norm_mix,
               mix_w_in=v_mix_w_in, ssm_a_re=v_ssm_a_re, ssm_a_im=v_ssm_a_im, ssm_log_dt=v_ssm_log_dt,
               ssm_b_re=v_ssm_b_re, ssm_b_im=v_ssm_b_im, ssm_c_re=v_ssm_c_re, ssm_c_im=v_ssm_c_im, ssm_d=v_ssm_d,
               ssm_glu_w=v_ssm_glu_w, ssm_glu_b=v_ssm_glu_b, gm_v_gain=v_gm_v_gain, gm_w_s=v_gm_w_s, gm_b_s=v_gm_b_s,
               gain_ssm_out=v_gain_ssm_out, gain_gm_out=v_gain_gm_out, mix_w_out=v_mix_w_out, norm_ffn2=v_norm_ffn2,
               ffn2_w_in=v_ffn2_w_in, ffn2_w_out=v_ffn2_w_out, norm_final=v_norm_final)

    nseq, seq, d = x.shape
    n = nseq * seq
    depth = norm_ffn1.shape[0]
    width = gain_ssm_out.shape[1]
    groups = ssm_a_re.shape[1]
    heads = gm_w_s.shape[1]
    core = lax.axis_index("c").astype(jnp.int32).reshape(1)

    is_col = dict(BIG)
    full = {name: [None] * depth for name, _ in BIG}

    def gather_comm(pairs):
        return _gather_comm([wts[nm][l].astype(BF16) for nm, l in pairs], [is_col[nm] for nm, _ in pairs])

    def store(pairs, arrays):
        for (nm, l), w in zip(pairs, arrays):
            full[nm][l] = w

    pairs = [("ffn1_w_in", 0)]
    store(pairs, _run_comm(gather_comm(pairs), "all_gather_first"))

    xs = x.reshape(n, d)
    saved = []
    for l in range(depth):
        pairs = [("ffn1_w_out", l)] + ([("mix_w_in", l), ("mix_w_out", l)] if l == 0 else [])
        x1, s_ffn1 = _ffn_fwd(xs, norm_ffn1[l], full["ffn1_w_in"][l], lambda l=l: full["ffn1_w_out"][l], "ffn1",
                              hosted=(functools.partial(gather_comm, pairs), functools.partial(store, pairs)))
        pairs = [("ffn2_w_out", l)]
        hm, u_ssm, zgm, got = _mix_in_fwd(x1, norm_mix[l], full["mix_w_in"][l], width, "mix_in",
                                          comm=gather_comm(pairs))
        store(pairs, got)
        bt_re = jnp.swapaxes(ssm_b_re[l], 1, 2)
        bt_im = jnp.swapaxes(ssm_b_im[l], 1, 2)
        disc_in = (ssm_a_re[l], ssm_a_im[l], ssm_log_dt[l].reshape(groups, 1), bt_re, bt_im)
        lr, li, bbr, bbi = _disc_fwd(*disc_in)
        ops = _ssm_operands(lr, li, bbr, bbi, ssm_c_re[l], ssm_c_im[l], ssm_d[l], ssm_glu_w[l], ssm_glu_b[l])
        pairs = [("ffn2_w_in", l)]
        y_ssm, hb, got = _ssm_fwd_pair(u_ssm, ops, nseq, "s5_fwd", comm=gather_comm(pairs))
        store(pairs, got)
        bias_tile = jnp.broadcast_to(gm_b_s[l].T[:, :, None], (GM_CHUNK, heads, GM_HEAD_DIM)).reshape(GM_CHUNK, width)
        y_gm = _gmlp_fwd(zgm, gm_v_gain[l], gm_w_s[l], bias_tile, "gmlp_fwd")
        if l + 1 < depth:
            pairs = [("mix_w_in", l + 1), ("mix_w_out", l + 1)]
            ycat, x2, got = _mix_out_fwd(y_ssm, y_gm, gain_ssm_out[l], gain_gm_out[l], full["mix_w_out"][l], x1,
                                         "mix_out_hosting", comm=gather_comm(pairs))
            store(pairs, got)
        else:
            ycat, x2 = _mix_out_fwd(y_ssm, y_gm, gain_ssm_out[l], gain_gm_out[l], full["mix_w_out"][l], x1, "mix_out")
        hosted = None
        if l + 1 < depth:
            pairs = [("ffn1_w_in", l + 1)]
            hosted = (functools.partial(gather_comm, pairs), functools.partial(store, pairs))
        x3, s_ffn2 = _ffn_fwd(x2, norm_ffn2[l], full["ffn2_w_in"][l], full["ffn2_w_out"][l], "ffn2", hosted=hosted)
        saved.append(dict(ffn1=s_ffn1, x1=x1, hm=hm, zgm=zgm, disc_in=disc_in, ops=ops, u_ssm=u_ssm, hb=hb, y_ssm=y_ssm,
                          bias_tile=bias_tile, y_gm=y_gm, ycat=ycat, ffn2=s_ffn2))
        xs = x3

    dx, g_norm_final, loss_part = _loss_head(xs, norm_final, loss_target.reshape(n, d))
    big = {name: [None] * depth for name, _ in BIG}
    small = {name: [None] * depth for name in SMALL if name != "norm_final"}
    gpb = GROUPS_PER_BLOCK
    s_blk = STATES_PER_BLOCK
    psum_of, reduced = {}, {}

    def swap_comm(pairs):
        return _pair_exchange_comm([big[nm][l] for nm, l in pairs], [is_col[nm] for nm, _ in pairs])

    def take_swapped(pairs, others):
        for (nm, l), other in zip(pairs, others):
            psum_of[nm, l] = _pair_sum(big[nm][l], other, is_col[nm], core, f"grad_pair_sum_{nm}")

    def send_comm(pairs):
        return _chip_exchange_comm([psum_of[p] for p in pairs], [is_col[nm] for nm, _ in pairs])

    def take_sent(pairs, slots):
        for (nm, l), s in zip(pairs, slots):
            reduced[nm] = _chip_sum(s, core, l, depth, reduced.get(nm), f"grad_chip_sum_{nm}")

    def hosting(make, take, pairs):
        return functools.partial(make, pairs), functools.partial(take, pairs)

    for l in reversed(range(depth)):
        sv = saved[l]
        above = [(nm, l + 1) for nm in ("mix_w_in", "mix_w_out", "ffn1_w_in", "ffn1_w_out")] if l + 1 < depth else []
        dx, small["norm_ffn2"][l], big["ffn2_w_in"][l], big["ffn2_w_out"][l] = _ffn_bwd(
            dx, sv["ffn2"], norm_ffn2[l], full["ffn2_w_in"][l], full["ffn2_w_out"][l], "ffn2",
            hooks={"out_dx": hosting(swap_comm, take_swapped, above)} if above else None)
        mine = [("ffn2_w_in", l), ("ffn2_w_out", l)]
        dy_ssm, dy_gm, small["gain_ssm_out"][l], small["gain_gm_out"][l], got = _mix_out_bwd(
            dx, full["mix_w_out"][l], sv["y_ssm"], sv["y_gm"], gain_ssm_out[l], gain_gm_out[l], "mix_out_dx",
            comm=swap_comm(mine))
        take_swapped(mine, got)
        big["mix_w_out"][l] = _matmul(sv["ycat"], dx, "tn", tm=1024, tn=1024, tk=2048, name="mix_out_dw")
        dzgm, small["gm_w_s"][l], dbias_tile, small["gm_v_gain"][l] = _gmlp_bwd(
            sv["zgm"], dy_gm, gm_v_gain[l], gm_w_s[l], sv["bias_tile"], "gmlp_bwd")
        small["gm_b_s"][l] = dbias_tile.reshape(GM_CHUNK, heads, GM_HEAD_DIM).sum(-1).T
        (du_ssm, dlam, db_bd, dct_bd, dd, dw_bd, dbias), got = _ssm_bwd_pair(
            sv["u_ssm"], dy_ssm, sv["hb"], sv["ops"], nseq, "s5_bwd", comm=send_comm(mine + above))
        take_sent(mine + above, got)
        dlr = dlam[:, 0, :s_blk].reshape(groups, SSM_STATE)
        dli = dlam[:, 0, s_blk:].reshape(groups, SSM_STATE)
        dbbr = _block_diag_extract(db_bd[:, :, :s_blk], gpb)
        dbbi = _block_diag_extract(db_bd[:, :, s_blk:], gpb)
        da_re, da_im, dldt, dbt_re, dbt_im = _disc_bwd(*sv["disc_in"], dlr, dli, dbbr, dbbi)
        small["ssm_a_re"][l], small["ssm_a_im"][l], small["ssm_log_dt"][l] = da_re, da_im, dldt.reshape(groups)
        small["ssm_b_re"][l] = jnp.swapaxes(dbt_re, 1, 2)
        small["ssm_b_im"][l] = jnp.swapaxes(dbt_im, 1, 2)
        small["ssm_c_re"][l] = _block_diag_extract(dct_bd[:, :, :s_blk], gpb)
        small["ssm_c_im"][l] = -_block_diag_extract(dct_bd[:, :, s_blk:], gpb)
        small["ssm_d"][l] = dd.reshape(groups, SSM_CH)
        small["ssm_glu_w"][l] = jnp.concatenate(
            [_block_diag_extract(dw_bd[:, :, :LANES], gpb), _block_diag_extract(dw_bd[:, :, LANES:], gpb)], axis=-1)
        small["ssm_glu_b"][l] = jnp.concatenate(
            [dbias[:, 0, :LANES].reshape(groups, SSM_CH), dbias[:, 0, LANES:].reshape(groups, SSM_CH)], axis=-1)
        cols_mi = 3 * width
        dw_mi = _matmul(sv["hm"], du_ssm, "tn", tm=1024, tn=width, tk=2048, name="mix_in_dw_ssm", out_cols=cols_mi)
        big["mix_w_in"][l] = _matmul(sv["hm"], dzgm, "tn", tm=1024, tn=width, tk=2048, name="mix_in_dw_gm",
                                     out_cols=cols_mi, col_off=width, into=dw_mi)
        dx, small["norm_mix"][l] = _proj_in_bwd([(du_ssm, 0), (dzgm, width)], full["mix_w_in"][l], sv["x1"],
                                                norm_mix[l], dx, "mix_in_dx")
        hooks = None
        if l == 0:
            mix, w_out_0, w_in_0 = [("mix_w_in", 0), ("mix_w_out", 0)], [("ffn1_w_out", 0)], [("ffn1_w_in", 0)]

            def last_make():
                return _merge_comms(send_comm(w_in_0), swap_comm(w_out_0))

            def last_take(got):
                take_sent(w_in_0, got[:1])
                take_swapped(w_out_0, got[1:])

            hooks = {"out_dx": hosting(swap_comm, take_swapped, mix), "in_dw_g": hosting(send_comm, take_sent, mix),
                     "out_dw": hosting(swap_comm, take_swapped, w_in_0), "in_dx": (last_make, last_take)}

        def publish(which, dw, l=l):
            big[f"ffn1_w_{which}"][l] = dw

        dx, small["norm_ffn1"][l], big["ffn1_w_in"][l], big["ffn1_w_out"][l] = _ffn_bwd(
            dx, sv["ffn1"], norm_ffn1[l], full["ffn1_w_in"][l], full["ffn1_w_out"][l], "ffn1",
            hooks=hooks, publish=publish, late_out_dw=(l == 0))
    grad_x = dx.reshape(nseq, seq, d)

    pieces = [jnp.stack(small[name]).reshape(-1) for name in SMALL if name != "norm_final"]
    pieces += [g_norm_final.reshape(-1), loss_part.reshape(1)]
    sizes = [p.shape[0] for p in pieces]
    total = sum(sizes)
    rows = -(-total // (LANES * N_DEV * SUBLANES)) * N_DEV * SUBLANES
    pad = rows * LANES - total
    tail = [("ffn1_w_out", 0)]
    flat_g, got = _all_reduce_small(
        jnp.concatenate(pieces + [jnp.zeros((pad,), F32)]).reshape(rows, LANES), send_comm(tail))
    take_sent(tail, got)
    flat_g = flat_g.reshape(-1)
    loss = flat_g[total - 1]

    names = [name for name, _ in BIG]
    grads = dict(zip(names, _run_comm(_pair_share_comm([reduced[nm] for nm in names]), "grad_pair_share")))
    offs = 0
    for name, size in zip(SMALL, sizes[:-1]):
        grads[name] = flat_g[offs:offs + size].reshape(wts[name].shape)
        offs += size

    delta, new_m, new_v = {}, {}, {}
    for name, _ in BIG:
        shape = wts[name].shape
        two_d = lambda a: a.reshape(shape[0] * shape[1], shape[2])
        go, dl, nm, nv = _adamw(two_d(wts[name]), two_d(grads[name]), two_d(mom[name]), two_d(var[name]),
                                f"adamw_{name}")
        grads[name] = go.reshape(shape)
        delta[name], new_m[name], new_v[name] = dl.reshape(shape), nm.reshape(shape), nv.reshape(shape)
    at_least_2d = lambda a: a.reshape(1, -1) if a.ndim == 1 else a
    dls, nms, nvs = _adamw_many(*[[at_least_2d(tree[k]) for k in SMALL] for tree in (wts, grads, mom, var)],
                                "adamw_small")
    for name, dl, nm, nv in zip(SMALL, dls, nms, nvs):
        shape = wts[name].shape
        delta[name], new_m[name], new_v[name] = dl.reshape(shape), nm.reshape(shape), nv.reshape(shape)

    return (loss, grad_x, *[grads[k] for k in WEIGHTS], *[delta[k] for k in WEIGHTS],
            *[new_m[k] for k in WEIGHTS], *[new_v[k] for k in WEIGHTS])
```

```python
import functools
import math

import jax
import jax.numpy as jnp
from jax import lax
from jax.experimental import pallas as pl
from jax.experimental.pallas import tpu as pltpu

F32 = jnp.float32
BF16 = jnp.bfloat16
MESH = pl.DeviceIdType.MESH

EPS = 1e-6
SSM_CH = 16
SSM_STATE = 64
GM_CHUNK = 128
GM_HEAD_DIM = 128
SUBLANES = 8
LANES = 128
GROUPS_PER_BLOCK = LANES // SSM_CH
STATES_PER_BLOCK = GROUPS_PER_BLOCK * SSM_STATE
SSM_TIME_CHUNK = 128
N_CHIPS = 4
N_DEV = 8

ADAM_LR = 0.001
ADAM_B1 = 0.9
ADAM_B2 = 0.999
ADAM_EPS = 1e-08
ADAM_WD = 0.01
ADAM_STEP = 10

VMEM_LIMIT = 56 * 1024 * 1024


def _tile(dim, pref, align):
    best = None
    t = align
    while t <= min(dim, pref):
        if dim % t == 0:
            best = t
        t += align
    return best if best is not None else dim


def _params(*sem):
    return pltpu.CompilerParams(dimension_semantics=sem, vmem_limit_bytes=VMEM_LIMIT)


def _gelu(x):
    c = math.sqrt(2.0 / math.pi)
    return 0.5 * x * (1.0 + jnp.tanh(c * (x + 0.044715 * x * x * x)))


def _gelu_and_grad(x):
    c = math.sqrt(2.0 / math.pi)
    t = jnp.tanh(c * (x + 0.044715 * x * x * x))
    g = 0.5 * x * (1.0 + t)
    dg = 0.5 * (1.0 + t) + 0.5 * x * (1.0 - t * t) * c * (1.0 + 3.0 * 0.044715 * x * x)
    return g, dg


def _sigmoid(x):
    return 0.5 * jnp.tanh(0.5 * x) + 0.5


def _matmul(a, b, mode, *, out_dtype=F32, scale=1.0, res=None, tm=512, tn=1024, tk=1024, name="mm",
            out_cols=None, col_off=0, into=None, comm=None):
    if mode == "nn":
        (m, k), (k2, n) = a.shape, b.shape
    elif mode == "nt":
        (m, k), (n, k2) = a.shape, b.shape
    else:
        (k, m), (k2, n) = a.shape, b.shape
    assert k == k2, (a.shape, b.shape, mode)
    tm = _tile(m, tm, 16 if mode != "tn" else LANES)
    tn = _tile(n, tn, LANES)
    tk = _tile(k, tk, LANES if mode != "tn" else 16)
    nk = k // tk
    grid = (m // tm, n // tn, nk)
    if mode == "nn":
        a_spec = pl.BlockSpec((tm, tk), lambda i, j, kk: (i, kk))
        b_spec = pl.BlockSpec((tk, tn), lambda i, j, kk: (kk, j))
        dims = (((1,), (0,)), ((), ()))
    elif mode == "nt":
        a_spec = pl.BlockSpec((tm, tk), lambda i, j, kk: (i, kk))
        b_spec = pl.BlockSpec((tn, tk), lambda i, j, kk: (j, kk))
        dims = (((1,), (1,)), ((), ()))
    else:
        a_spec = pl.BlockSpec((tk, tm), lambda i, j, kk: (kk, i))
        b_spec = pl.BlockSpec((tk, tn), lambda i, j, kk: (kk, j))
        dims = (((0,), (0,)), ((), ()))
    assert col_off % tn == 0
    off = col_off // tn
    r_spec = pl.BlockSpec((tm, tn), lambda i, j, kk: (i, j))
    o_spec = pl.BlockSpec((tm, tn), lambda i, j, kk: (i, j + off))
    has_res = res is not None
    has_into = into is not None

    def body(*refs):
        refs = list(refs)
        a_ref, b_ref = refs[:2]
        pos = 2
        r_ref = None
        if has_res:
            r_ref = refs[pos]
            pos += 1
        if has_into:
            pos += 1
        o_ref = refs[pos]
        acc_ref = refs[pos + 1] if nk > 1 else None
        part = lax.dot_general(a_ref[...].astype(BF16), b_ref[...].astype(BF16), dims,
                               preferred_element_type=F32)

        def finish(r):
            if scale != 1.0:
                r = r * scale
            if has_res:
                r = r + r_ref[...].astype(F32)
            o_ref[...] = r.astype(o_ref.dtype)

        if nk == 1:
            finish(part)
        else:
            kk = pl.program_id(2)

            @pl.when(kk == 0)
            def _():
                acc_ref[...] = part

            @pl.when(kk > 0)
            def _():
                acc_ref[...] += part

            @pl.when(kk == nk - 1)
            def _():
                finish(acc_ref[...])

    in_specs = [a_spec, b_spec]
    args = [a, b]
    if has_res:
        in_specs.append(r_spec)
        args.append(res)
    aliases = {}
    if has_into:
        in_specs.append(pl.BlockSpec(memory_space=pl.ANY))
        args.append(into)
        aliases = {len(args) - 1: 0}
    (out,), comm_outs = _call(
        body, name=name, grid=grid, in_specs=in_specs, out_specs=[o_spec],
        out_shape=[jax.ShapeDtypeStruct((m, n if out_cols is None else out_cols), out_dtype)],
        scratch_shapes=[pltpu.VMEM((tm, tn), F32)] if nk > 1 else [],
        aliases=aliases, semantics=("parallel", "parallel", "arbitrary"), args=args, comm=comm)
    return out if comm is None else (out, comm_outs)


class _Comm:
    def __init__(self, ins, outs, sems, start, finish, alias=None, middle=None):
        self.ins, self.outs, self.sems, self.start, self.finish = list(ins), list(outs), list(sems), start, finish
        self.alias = dict(alias or {})
        self.middle = middle


def _merge_comms(a, b):
    assert a.middle is None and b.middle is None
    cut = (len(a.ins), len(a.outs), len(a.sems))

    def both(which):
        def run(ins, outs, sems):
            getattr(a, which)(ins[:cut[0]], outs[:cut[1]], sems[:cut[2]])
            getattr(b, which)(ins[cut[0]:], outs[cut[1]:], sems[cut[2]:])
        return run

    alias = dict(a.alias)
    alias.update({cut[0] + ci: cut[1] + co for ci, co in b.alias.items()})
    return _Comm(a.ins + b.ins, a.outs + b.outs, a.sems + b.sems, both("start"), both("finish"), alias=alias)


def _call(body, *, name, grid, in_specs, out_specs, out_shape, args, scratch_shapes=(), semantics=(), aliases=None,
          comm=None):
    in_specs, out_specs, out_shape = list(in_specs), list(out_specs), list(out_shape)
    scratch_shapes = list(scratch_shapes)
    aliases = dict(aliases or {})
    if comm is None:
        outs = pl.pallas_call(
            body, name=name, grid=grid, in_specs=in_specs, out_specs=out_specs, out_shape=out_shape,
            scratch_shapes=scratch_shapes, input_output_aliases=aliases, compiler_params=_params(*semantics),
        )(*args)
        return list(outs), []
    n_in, n_out, n_scr = len(in_specs), len(out_specs), len(scratch_shapes)
    c_in, c_out = len(comm.ins), len(comm.outs)
    for ci, co in comm.alias.items():
        aliases[n_in + ci] = n_out + co

    def hosted(*refs):
        refs = list(refs)
        ins, cins = refs[:n_in], refs[n_in:n_in + c_in]
        p = n_in + c_in
        outs, couts = refs[p:p + n_out], refs[p + n_out:p + n_out + c_out]
        p += n_out + c_out
        scr, sems = refs[p:p + n_scr], refs[p + n_scr:]
        ids = [pl.program_id(a) for a in range(len(grid))]
        first = functools.reduce(jnp.logical_and, [i == 0 for i in ids])
        last = functools.reduce(jnp.logical_and, [i == g - 1 for i, g in zip(ids, grid)])

        total = math.prod(grid)
        late = comm.middle is not None and total >= 4

        @pl.when(first)
        def _():
            comm.start(cins, couts, sems)

        if late:
            flat = functools.reduce(lambda acc, ig: acc * ig[1] + ig[0], zip(ids, grid), 0)

            @pl.when(flat == (3 * total) // 4)
            def _():
                comm.middle(cins, couts, sems)

        body(*ins, *outs, *scr)

        @pl.when(last)
        def _():
            if comm.middle is not None and not late:
                comm.middle(cins, couts, sems)
            comm.finish(cins, couts, sems)

    any_spec = pl.BlockSpec(memory_space=pl.ANY)
    outs = pl.pallas_call(
        hosted, name=name, grid=grid, in_specs=in_specs + [any_spec] * c_in, out_specs=out_specs + [any_spec] * c_out,
        out_shape=out_shape + comm.outs, scratch_shapes=scratch_shapes + comm.sems, input_output_aliases=aliases,
        compiler_params=_params(*(["arbitrary"] * len(grid))),
    )(*args, *comm.ins)
    return list(outs[:n_out]), list(outs[n_out:])


def _run_comm(comm, name):
    c_in, c_out = len(comm.ins), len(comm.outs)

    def body(*refs):
        refs = list(refs)
        cins, couts, sems = refs[:c_in], refs[c_in:c_in + c_out], refs[c_in + c_out:]
        comm.start(cins, couts, sems)
        if comm.middle is not None:
            comm.middle(cins, couts, sems)
        comm.finish(cins, couts, sems)

    any_spec = pl.BlockSpec(memory_space=pl.ANY)
    return list(pl.pallas_call(
        body, name=name, in_specs=[any_spec] * c_in, out_specs=[any_spec] * c_out, out_shape=comm.outs,
        scratch_shapes=comm.sems, input_output_aliases=comm.alias,
    )(*comm.ins))


def _loss_head(x, gain, target):
    n, d = x.shape
    tm = _tile(n, 512, 8)
    steps = n // tm

    def body(x_ref, g_ref, t_ref, dx_ref, dg_ref, loss_ref, acc_ref, lacc_ref):
        i = pl.program_id(0)
        xv = x_ref[...]
        g = g_ref[...]
        r = lax.rsqrt(jnp.mean(xv * xv, axis=-1, keepdims=True) + EPS)
        xh = xv * r
        err = xh * g - t_ref[...]
        dy = err * (1.0 / d)
        dyg = dy * g
        mean = jnp.mean(dyg * xh, axis=-1, keepdims=True)
        dx_ref[...] = r * (dyg - xh * mean)
        part = jnp.sum((dy * xh).reshape(tm // SUBLANES, SUBLANES, d), axis=0)
        lpart = jnp.sum((err * err).reshape(tm // SUBLANES, SUBLANES, d), axis=0)

        @pl.when(i == 0)
        def _():
            acc_ref[...] = part
            lacc_ref[...] = lpart

        @pl.when(i > 0)
        def _():
            acc_ref[...] += part
            lacc_ref[...] += lpart

        @pl.when(i == steps - 1)
        def _():
            dg_ref[...] = jnp.sum(acc_ref[...], axis=0, keepdims=True)
            tot = jnp.sum(jnp.sum(lacc_ref[...], axis=0, keepdims=True), axis=1, keepdims=True)
            loss_ref[...] = jnp.broadcast_to(tot * (0.5 / d), loss_ref.shape)

    row = pl.BlockSpec((tm, d), lambda i: (i, 0))
    vec = pl.BlockSpec((1, d), lambda i: (0, 0))
    dx, dg, loss = pl.pallas_call(
        body, name="loss_head", grid=(steps,),
        in_specs=[row, vec, row],
        out_specs=[row, vec, pl.BlockSpec((1, LANES), lambda i: (0, 0))],
        out_shape=[jax.ShapeDtypeStruct((n, d), F32), jax.ShapeDtypeStruct((1, d), F32),
                   jax.ShapeDtypeStruct((1, LANES), F32)],
        scratch_shapes=[pltpu.VMEM((SUBLANES, d), F32), pltpu.VMEM((SUBLANES, d), F32)],
        compiler_params=_params("arbitrary"),
    )(x, gain.reshape(1, d), target)
    return dx, dg.reshape(d), loss[0, 0]


def _rms_rows(xv):
    return lax.rsqrt(jnp.mean(xv * xv, axis=-1, keepdims=True) + EPS)


def _ffn_in_fwd(x, gain, w_in, name, comm=None):
    n, d = x.shape
    f = w_in.shape[1] // 2
    tm = _tile(n, 256, 16)
    tn = _tile(f, 4096, LANES)
    nj = f // tn

    def body(x_ref, gain_ref, wg_ref, wu_ref, h_ref, t_ref, q_ref, a_ref):
        @pl.when(pl.program_id(1) == 0)
        def _():
            xv = x_ref[...]
            h_ref[...] = (xv * _rms_rows(xv) * gain_ref[...]).astype(h_ref.dtype)

        h = h_ref[...]
        g = jnp.dot(h, wg_ref[...], preferred_element_type=F32)
        u = jnp.dot(h, wu_ref[...], preferred_element_type=F32)
        s = _sigmoid(g)
        t = g * s
        t_ref[...] = t.astype(t_ref.dtype)
        q_ref[...] = (u * (s + t * (1.0 - s))).astype(q_ref.dtype)
        a_ref[...] = (t * u).astype(a_ref.dtype)

    row = pl.BlockSpec((tm, d), lambda i, j: (i, 0))
    tile = pl.BlockSpec((tm, tn), lambda i, j: (i, j))
    act = jax.ShapeDtypeStruct((n, f), BF16)
    outs, comm_outs = _call(
        body, name=name, grid=(n // tm, nj),
        in_specs=[row, pl.BlockSpec((1, d), lambda i, j: (0, 0)),
                  pl.BlockSpec((d, tn), lambda i, j: (0, j)), pl.BlockSpec((d, tn), lambda i, j: (0, j + nj))],
        out_specs=[row, tile, tile, tile],
        out_shape=[jax.ShapeDtypeStruct((n, d), BF16), act, act, act],
        semantics=("parallel", "arbitrary"), args=(x, gain.reshape(1, d), w_in, w_in), comm=comm)
    return outs if comm is None else (outs, comm_outs)


def _ffn_out_bwd(dout, w_out, t, q, name, comm=None):
    n, d = dout.shape
    f = w_out.shape[0]
    tm = _tile(n, 256, 16)
    tn = _tile(f, 4096, LANES)

    def body(d_ref, w_ref, t_ref, q_ref, dg_ref, du_ref):
        da = 0.5 * lax.dot_general(d_ref[...].astype(BF16), w_ref[...], (((1,), (1,)), ((), ())),
                                   preferred_element_type=F32)
        dg_ref[...] = (da * q_ref[...].astype(F32)).astype(dg_ref.dtype)
        du_ref[...] = (da * t_ref[...].astype(F32)).astype(du_ref.dtype)

    tile = pl.BlockSpec((tm, tn), lambda i, j: (i, j))
    act = jax.ShapeDtypeStruct((n, f), BF16)
    outs, comm_outs = _call(
        body, name=name, grid=(n // tm, f // tn),
        in_specs=[pl.BlockSpec((tm, d), lambda i, j: (i, 0)), pl.BlockSpec((tn, d), lambda i, j: (j, 0)), tile, tile],
        out_specs=[tile, tile], out_shape=[act, act],
        semantics=("parallel", "parallel"), args=(dout, w_out, t, q), comm=comm)
    return outs if comm is None else (outs, comm_outs)


def _proj_in_bwd(parts, w, x, gain, dres, name, comm=None):
    n, d = x.shape
    tm = _tile(n, 256, 8)
    steps = n // tm
    np_ = len(parts)
    offs = [off for _, off in parts]
    widths = [a.shape[1] for a, _ in parts]

    def body(*refs):
        a_refs = refs[:np_]
        w_ref, x_ref, g_ref, dr_ref, dx_ref, dg_ref, acc_ref = refs[np_:]
        i = pl.program_id(0)
        dh = None
        for a_ref, off, kp in zip(a_refs, offs, widths):
            part = lax.dot_general(a_ref[...].astype(BF16), w_ref[:, off:off + kp], (((1,), (1,)), ((), ())),
                                   preferred_element_type=F32)
            dh = part if dh is None else dh + part
        xv = x_ref[...]
        r = _rms_rows(xv)
        xh = xv * r
        dyg = dh * g_ref[...]
        mean = jnp.mean(dyg * xh, axis=-1, keepdims=True)
        dx_ref[...] = dr_ref[...] + r * (dyg - xh * mean)
        part = jnp.sum((dh * xh).reshape(tm // SUBLANES, SUBLANES, d), axis=0)

        @pl.when(i == 0)
        def _():
            acc_ref[...] = part

        @pl.when(i > 0)
        def _():
            acc_ref[...] += part

        @pl.when(i == steps - 1)
        def _():
            dg_ref[...] = jnp.sum(acc_ref[...], axis=0, keepdims=True)

    row = pl.BlockSpec((tm, d), lambda i: (i, 0))
    vec = pl.BlockSpec((1, d), lambda i: (0, 0))
    (dx, dg), comm_outs = _call(
        body, name=name, grid=(steps,),
        in_specs=[pl.BlockSpec((tm, kp), lambda i: (i, 0)) for kp in widths]
        + [pl.BlockSpec(w.shape, lambda i: (0, 0)), row, vec, row],
        out_specs=[row, vec],
        out_shape=[jax.ShapeDtypeStruct((n, d), F32), jax.ShapeDtypeStruct((1, d), F32)],
        scratch_shapes=[pltpu.VMEM((SUBLANES, d), F32)],
        semantics=("arbitrary",), args=(*[a for a, _ in parts], w, x, gain.reshape(1, d), dres), comm=comm)
    return (dx, dg.reshape(d)) if comm is None else (dx, dg.reshape(d), comm_outs)


def _mix_in_fwd(x, gain, w, width, name, comm=None):
    n, d = x.shape
    cols = w.shape[1]
    tm = _tile(n, 512, 16)

    def body(x_ref, gain_ref, w_ref, h_ref, u_ref, z_ref):
        xv = x_ref[...]
        h = (xv * _rms_rows(xv) * gain_ref[...]).astype(h_ref.dtype)
        h_ref[...] = h
        z = jnp.dot(h, w_ref[...], preferred_element_type=F32)
        u_ref[...] = z[:, 0:width]
        z_ref[...] = z[:, width:cols]

    row = pl.BlockSpec((tm, d), lambda i: (i, 0))
    outs, comm_outs = _call(
        body, name=name, grid=(n // tm,),
        in_specs=[row, pl.BlockSpec((1, d), lambda i: (0, 0)), pl.BlockSpec((d, cols), lambda i: (0, 0))],
        out_specs=[row, pl.BlockSpec((tm, width), lambda i: (i, 0)), pl.BlockSpec((tm, cols - width), lambda i: (i, 0))],
        out_shape=[jax.ShapeDtypeStruct((n, d), BF16), jax.ShapeDtypeStruct((n, width), F32),
                   jax.ShapeDtypeStruct((n, cols - width), F32)],
        semantics=("parallel",), args=(x, gain.reshape(1, d), w), comm=comm)
    return outs if comm is None else (*outs, comm_outs)


def _tril_mask():
    t = lax.broadcasted_iota(jnp.int32, (GM_CHUNK, GM_CHUNK), 0)
    s = lax.broadcasted_iota(jnp.int32, (GM_CHUNK, GM_CHUNK), 1)
    return s <= t


def _gmlp_fwd(zgm, v_gain, w_s, bias_tile, name):
    n, w2 = zgm.shape
    w = w2 // 2
    heads = w // GM_HEAD_DIM
    tm = _tile(n, 512, GM_CHUNK)
    nq = tm // GM_CHUNK

    def body(u_ref, v_ref, gain_ref, w_ref, b_ref, o_ref):
        mask = _tril_mask()
        ug = _gelu(u_ref[...])
        vg = _gelu(v_ref[...])
        for h in range(heads):
            cols = slice(h * GM_HEAD_DIM, (h + 1) * GM_HEAD_DIM)
            vh = vg[:, cols]
            r = lax.rsqrt(jnp.mean(vh * vh, axis=-1, keepdims=True) + EPS)
            vn = (vh * r * gain_ref[:, cols]).astype(BF16)
            wm = jnp.where(mask, w_ref[h], 0.0).astype(BF16)
            for q in range(nq):
                rows = slice(q * GM_CHUNK, (q + 1) * GM_CHUNK)
                s = jnp.dot(wm, vn[rows], preferred_element_type=F32) + b_ref[:, cols]
                o_ref[rows, cols] = ug[rows, cols] * s

    return pl.pallas_call(
        body, name=name, grid=(n // tm,),
        in_specs=[pl.BlockSpec((tm, w), lambda i: (i, 0)), pl.BlockSpec((tm, w), lambda i: (i, 1)),
                  pl.BlockSpec((1, w), lambda i: (0, 0)),
                  pl.BlockSpec((heads, GM_CHUNK, GM_CHUNK), lambda i: (0, 0, 0)),
                  pl.BlockSpec((GM_CHUNK, w), lambda i: (0, 0))],
        out_specs=pl.BlockSpec((tm, w), lambda i: (i, 0)),
        out_shape=jax.ShapeDtypeStruct((n, w), F32),
        compiler_params=_params("parallel"),
    )(zgm, zgm, v_gain.reshape(1, w), w_s, bias_tile)


def _gmlp_bwd(zgm, dy, v_gain, w_s, bias_tile, name):
    n, w2 = zgm.shape
    w = w2 // 2
    heads = w // GM_HEAD_DIM
    tm = _tile(n, 512, GM_CHUNK)
    nq = tm // GM_CHUNK
    steps = n // tm

    def body(z_ref, dy_ref, gain_ref, w_ref, b_ref, dz_ref, dw_ref, db_ref, dgain_ref):
        i = pl.program_id(0)
        mask = _tril_mask()

        @pl.when(i == 0)
        def _():
            dw_ref[...] = jnp.zeros_like(dw_ref)
            db_ref[...] = jnp.zeros_like(db_ref)
            dgain_ref[...] = jnp.zeros_like(dgain_ref)

        ug, dug_du = _gelu_and_grad(z_ref[:, 0:w])
        vg, dvg_dv = _gelu_and_grad(z_ref[:, w:w2])
        dyv = dy_ref[...]
        for h in range(heads):
            cols = slice(h * GM_HEAD_DIM, (h + 1) * GM_HEAD_DIM)
            vh = vg[:, cols]
            r = lax.rsqrt(jnp.mean(vh * vh, axis=-1, keepdims=True) + EPS)
            vhat = vh * r
            gain = gain_ref[:, cols]
            vn = (vhat * gain).astype(BF16)
            wm = jnp.where(mask, w_ref[h], 0.0).astype(BF16)
            dvn_parts = []
            for q in range(nq):
                rows = slice(q * GM_CHUNK, (q + 1) * GM_CHUNK)
                s = jnp.dot(wm, vn[rows], preferred_element_type=F32) + b_ref[:, cols]
                dyq = dyv[rows, cols]
                dz_ref[rows, cols] = (dyq * s * dug_du[rows, cols]).astype(dz_ref.dtype)
                ds = dyq * ug[rows, cols]
                db_ref[:, cols] += ds
                dsb = ds.astype(BF16)
                dw_ref[h] += lax.dot_general(dsb, vn[rows], (((1,), (1,)), ((), ())), preferred_element_type=F32)
                dvn_parts.append(lax.dot_general(wm, dsb, (((0,), (0,)), ((), ())), preferred_element_type=F32))
            dvn = jnp.concatenate(dvn_parts, axis=0) if nq > 1 else dvn_parts[0]
            dgain_ref[:, cols] += jnp.sum(dvn * vhat, axis=0, keepdims=True)
            dvhat = dvn * gain
            mean = jnp.mean(dvhat * vhat, axis=-1, keepdims=True)
            dz_ref[:, w + h * GM_HEAD_DIM:w + (h + 1) * GM_HEAD_DIM] = (
                r * (dvhat - vhat * mean) * dvg_dv[:, cols]).astype(dz_ref.dtype)

        @pl.when(i == steps - 1)
        def _():
            for h in range(heads):
                dw_ref[h] = jnp.where(mask, dw_ref[h], 0.0)

    dz, dw, db, dgain = pl.pallas_call(
        body, name=name, grid=(steps,),
        in_specs=[pl.BlockSpec((tm, w2), lambda i: (i, 0)), pl.BlockSpec((tm, w), lambda i: (i, 0)),
                  pl.BlockSpec((1, w), lambda i: (0, 0)),
                  pl.BlockSpec((heads, GM_CHUNK, GM_CHUNK), lambda i: (0, 0, 0)),
                  pl.BlockSpec((GM_CHUNK, w), lambda i: (0, 0))],
        out_specs=[pl.BlockSpec((tm, w2), lambda i: (i, 0)),
                   pl.BlockSpec((heads, GM_CHUNK, GM_CHUNK), lambda i: (0, 0, 0)),
                   pl.BlockSpec((GM_CHUNK, w), lambda i: (0, 0)),
                   pl.BlockSpec((1, w), lambda i: (0, 0))],
        out_shape=[jax.ShapeDtypeStruct((n, w2), BF16), jax.ShapeDtypeStruct((heads, GM_CHUNK, GM_CHUNK), F32),
                   jax.ShapeDtypeStruct((GM_CHUNK, w), F32), jax.ShapeDtypeStruct((1, w), F32)],
        compiler_params=_params("arbitrary"),
    )(zgm, dy, v_gain.reshape(1, w), w_s, bias_tile)
    return dz, dw, db, dgain.reshape(w)


def _mix_out_fwd(y_ssm, y_gm, g1, g2, w_out, x, name, comm=None):
    n, w = y_ssm.shape
    d = w_out.shape[1]
    tm = _tile(n, 512, 16)

    def body(a_ref, b_ref, g1_ref, g2_ref, w_ref, x_ref, ycat_ref, o_ref):
        for src, g_ref, lo in ((a_ref, g1_ref, 0), (b_ref, g2_ref, w)):
            v = src[...]
            ycat_ref[:, lo:lo + w] = (v * _rms_rows(v) * g_ref[...]).astype(ycat_ref.dtype)
        o_ref[...] = x_ref[...] + jnp.dot(ycat_ref[...], w_ref[...], preferred_element_type=F32)

    row = pl.BlockSpec((tm, w), lambda i: (i, 0))
    vec = pl.BlockSpec((1, w), lambda i: (0, 0))
    outs, comm_outs = _call(
        body, name=name, grid=(n // tm,),
        in_specs=[row, row, vec, vec, pl.BlockSpec((2 * w, d), lambda i: (0, 0)), pl.BlockSpec((tm, d), lambda i: (i, 0))],
        out_specs=[pl.BlockSpec((tm, 2 * w), lambda i: (i, 0)), pl.BlockSpec((tm, d), lambda i: (i, 0))],
        out_shape=[jax.ShapeDtypeStruct((n, 2 * w), BF16), jax.ShapeDtypeStruct((n, d), F32)],
        semantics=("parallel",), args=(y_ssm, y_gm, g1.reshape(1, w), g2.reshape(1, w), w_out, x), comm=comm)
    return outs if comm is None else (*outs, comm_outs)


def _mix_out_bwd(dx, w_out, y_ssm, y_gm, g1, g2, name, comm=None):
    n, w = y_ssm.shape
    d = w_out.shape[1]
    tm = _tile(n, 512, 8)
    steps = n // tm

    def body(dx_ref, w_ref, a_ref, b_ref, g1_ref, g2_ref, da_ref, db_ref, dg1_ref, dg2_ref):
        i = pl.program_id(0)

        @pl.when(i == 0)
        def _():
            dg1_ref[...] = jnp.zeros_like(dg1_ref)
            dg2_ref[...] = jnp.zeros_like(dg2_ref)

        dycat = lax.dot_general(dx_ref[...].astype(BF16), w_ref[...], (((1,), (1,)), ((), ())),
                                preferred_element_type=F32)
        for src, g_ref, lo, dst, dg_ref in ((a_ref, g1_ref, 0, da_ref, dg1_ref), (b_ref, g2_ref, w, db_ref, dg2_ref)):
            v = src[...]
            dh = dycat[:, lo:lo + w]
            r = _rms_rows(v)
            vh = v * r
            dyg = dh * g_ref[...]
            mean = jnp.mean(dyg * vh, axis=-1, keepdims=True)
            dst[...] = r * (dyg - vh * mean)
            dg_ref[...] += jnp.sum(dh * vh, axis=0, keepdims=True)

    row = pl.BlockSpec((tm, w), lambda i: (i, 0))
    vec = pl.BlockSpec((1, w), lambda i: (0, 0))
    (da, db, dg1, dg2), comm_outs = _call(
        body, name=name, grid=(steps,),
        in_specs=[pl.BlockSpec((tm, d), lambda i: (i, 0)), pl.BlockSpec((2 * w, d), lambda i: (0, 0)), row, row, vec, vec],
        out_specs=[row, row, vec, vec],
        out_shape=[jax.ShapeDtypeStruct((n, w), F32), jax.ShapeDtypeStruct((n, w), F32),
                   jax.ShapeDtypeStruct((1, w), F32), jax.ShapeDtypeStruct((1, w), F32)],
        semantics=("arbitrary",), args=(dx, w_out, y_ssm, y_gm, g1.reshape(1, w), g2.reshape(1, w)), comm=comm)
    res = (da, db, dg1.reshape(w), dg2.reshape(w))
    return res if comm is None else (*res, comm_outs)


def _discretise(a_re, a_im, log_dt, bt_re, bt_im):
    dt = jnp.exp(log_dt)
    e = jnp.exp(a_re * dt)
    ang = a_im * dt
    lr = e * jnp.cos(ang)
    li = e * jnp.sin(ang)
    den = a_re * a_re + a_im * a_im
    cr = ((lr - 1.0) * a_re + li * a_im) / den
    ci = (li * a_re - (lr - 1.0) * a_im) / den
    cr3 = cr[:, None, :]
    ci3 = ci[:, None, :]
    return lr, li, cr3 * bt_re - ci3 * bt_im, cr3 * bt_im + ci3 * bt_re


def _disc_fwd(a_re, a_im, log_dt, bt_re, bt_im):
    g, p = a_re.shape
    c = bt_re.shape[1]

    def body(are_ref, aim_ref, ldt_ref, bre_ref, bim_ref, lr_ref, li_ref, bbr_ref, bbi_ref):
        lr, li, bbr, bbi = _discretise(are_ref[...], aim_ref[...], ldt_ref[...], bre_ref[...], bim_ref[...])
        lr_ref[...] = lr
        li_ref[...] = li
        bbr_ref[...] = bbr
        bbi_ref[...] = bbi

    return pl.pallas_call(
        body, name="s5_discretise",
        out_shape=[jax.ShapeDtypeStruct((g, p), F32), jax.ShapeDtypeStruct((g, p), F32),
                   jax.ShapeDtypeStruct((g, c, p), F32), jax.ShapeDtypeStruct((g, c, p), F32)],
    )(a_re, a_im, log_dt, bt_re, bt_im)


def _disc_bwd(a_re, a_im, log_dt, bt_re, bt_im, dlr, dli, dbbr, dbbi):
    g, p = a_re.shape
    c = bt_re.shape[1]

    def body(are_ref, aim_ref, ldt_ref, bre_ref, bim_ref, dlr_ref, dli_ref, dbbr_ref, dbbi_ref,
             dare_ref, daim_ref, dldt_ref, dbre_ref, dbim_ref):
        _, vjp = jax.vjp(_discretise, are_ref[...], aim_ref[...], ldt_ref[...], bre_ref[...], bim_ref[...])
        dare, daim, dldt, dbre, dbim = vjp((dlr_ref[...], dli_ref[...], dbbr_ref[...], dbbi_ref[...]))
        dare_ref[...] = dare
        daim_ref[...] = daim
        dldt_ref[...] = dldt
        dbre_ref[...] = dbre
        dbim_ref[...] = dbim

    return pl.pallas_call(
        body, name="s5_discretise_bwd",
        out_shape=[jax.ShapeDtypeStruct((g, p), F32), jax.ShapeDtypeStruct((g, p), F32),
                   jax.ShapeDtypeStruct((g, 1), F32),
                   jax.ShapeDtypeStruct((g, c, p), F32), jax.ShapeDtypeStruct((g, c, p), F32)],
    )(a_re, a_im, log_dt, bt_re, bt_im, dlr, dli, dbbr, dbbi)


def _block_diag(w, nb):
    g, a, b = w.shape
    gpb = g // nb
    eye = jnp.eye(gpb, dtype=w.dtype)
    w4 = w.reshape(nb, gpb, a, b)
    return jnp.einsum("ngab,gh->ngahb", w4, eye).reshape(nb, gpb * a, gpb * b)


def _block_diag_extract(m, gpb):
    nb, ga, gb = m.shape
    a, b = ga // gpb, gb // gpb
    m5 = m.reshape(nb, gpb, a, gpb, b)
    idx = jnp.arange(gpb)
    return m5[:, idx, :, idx, :].transpose(1, 0, 2, 3).reshape(nb * gpb, a, b)


def _ssm_operands(lr, li, bbr, bbi, c_re, c_im, d_skip, glu_w, glu_b):
    g = lr.shape[0]
    nb = g // GROUPS_PER_BLOCK
    s = STATES_PER_BLOCK
    lam = jnp.concatenate([lr.reshape(nb, 1, s), li.reshape(nb, 1, s)], axis=-1)
    b_bd = jnp.concatenate([_block_diag(bbr, nb), _block_diag(bbi, nb)], axis=-1)
    ct_re = jnp.swapaxes(c_re, 1, 2)
    ct_im = jnp.swapaxes(c_im, 1, 2)
    c_bd = jnp.concatenate([_block_diag(ct_re, nb), -_block_diag(ct_im, nb)], axis=1)
    dsk = d_skip.reshape(nb, 1, LANES)
    w_bd = jnp.concatenate([_block_diag(glu_w[:, :, :SSM_CH], nb), _block_diag(glu_w[:, :, SSM_CH:], nb)], axis=-1)
    bias = jnp.concatenate([glu_b[:, :SSM_CH].reshape(nb, 1, LANES), glu_b[:, SSM_CH:].reshape(nb, 1, LANES)], axis=-1)
    return lam, b_bd.astype(BF16), c_bd.astype(BF16), dsk, w_bd.astype(BF16), bias


def _roll_rows(v, shift):
    return v if shift % SUBLANES == 0 else pltpu.roll(v, shift % SUBLANES, 0)


def _scan_chunk_rows(seq, nseq):
    return _tile(seq, max(8 * SSM_TIME_CHUNK // nseq, 8), max(SUBLANES // nseq, 1) * 8)


def _ssm_fwd(u, ops, nseq, name, comm=None):
    lam, b_bd, c_bd, dsk, w_bd, bias = ops
    rows_total, w = u.shape
    seq = rows_total // nseq
    nb = w // LANES
    s = STATES_PER_BLOCK
    tc = _scan_chunk_rows(seq, nseq)
    nk = seq // tc
    rows = tc * nseq
    stages = SUBLANES // nseq

    def body(u_ref, lam_ref, b_ref, c_ref, d_ref, w_ref, bias_ref, y_ref, hb_ref, buf, st, rbuf):
        k = pl.program_id(1)

        @pl.when(k == 0)
        def _():
            st[...] = jnp.zeros_like(st)

        hb_ref[...] = st[...]
        for q in range(nseq):
            rbuf[pl.ds(q, tc, stride=nseq), :] = u_ref[q]
        u = rbuf[...]
        buf[...] = jnp.dot(u.astype(BF16), b_ref[0], preferred_element_type=F32)
        lr = jnp.broadcast_to(lam_ref[0, :, 0:s], (SUBLANES, s))
        li = jnp.broadcast_to(lam_ref[0, :, s:2 * s], (SUBLANES, s))
        row = lax.broadcasted_iota(jnp.int32, (SUBLANES, s), 0)

        def step(i, carry):
            pr, pi = carry
            r0 = pl.multiple_of(i * SUBLANES, SUBLANES)
            br = buf[pl.ds(r0, SUBLANES), 0:s]
            bi = buf[pl.ds(r0, SUBLANES), s:2 * s]
            outr = outi = None
            for j in range(stages):
                rr = _roll_rows(pr, nseq)
                ri = _roll_rows(pi, nseq)
                pr = lr * rr - li * ri + br
                pi = lr * ri + li * rr + bi
                outr = pr if j == 0 else jnp.where(row >= j * nseq, pr, outr)
                outi = pi if j == 0 else jnp.where(row >= j * nseq, pi, outi)
            buf[pl.ds(r0, SUBLANES), 0:s] = outr
            buf[pl.ds(r0, SUBLANES), s:2 * s] = outi
            return outr, outi

        hr, hi = lax.fori_loop(0, rows // SUBLANES, step, (st[:, 0:s], st[:, s:2 * s]), unroll=2)
        st[:, 0:s] = hr
        st[:, s:2 * s] = hi
        y = jnp.dot(buf[...].astype(BF16), c_ref[0], preferred_element_type=F32) + d_ref[0] * u
        z = jnp.dot(_gelu(y).astype(BF16), w_ref[0], preferred_element_type=F32) + bias_ref[0]
        rbuf[...] = z[:, 0:LANES] * _sigmoid(z[:, LANES:2 * LANES])
        for q in range(nseq):
            y_ref[q] = rbuf[pl.ds(q, tc, stride=nseq), :]

    blk = lambda shape: pl.BlockSpec(shape, lambda b, k: (b, 0, 0))
    tok = pl.BlockSpec((nseq, tc, LANES), lambda b, k: (0, k, b))
    (y, hb), comm_outs = _call(
        body, name=name, grid=(nb, nk),
        in_specs=[tok, blk((1, 1, 2 * s)), blk((1, LANES, 2 * s)), blk((1, 2 * s, LANES)),
                  blk((1, 1, LANES)), blk((1, LANES, 2 * LANES)), blk((1, 1, 2 * LANES))],
        out_specs=[tok, pl.BlockSpec((SUBLANES, 2 * s), lambda b, k: (k, b))],
        out_shape=[jax.ShapeDtypeStruct((nseq, seq, w), F32),
                   jax.ShapeDtypeStruct((nk * SUBLANES, nb * 2 * s), F32)],
        scratch_shapes=[pltpu.VMEM((rows, 2 * s), F32), pltpu.VMEM((SUBLANES, 2 * s), F32),
                        pltpu.VMEM((rows, LANES), F32)],
        semantics=("parallel", "arbitrary"),
        args=(u.reshape(nseq, seq, w), lam, b_bd, c_bd, dsk, w_bd, bias), comm=comm)
    y = y.reshape(nseq * seq, w)
    return (y, hb) if comm is None else (y, hb, comm_outs)


def _scan_with(nblk, step, carry, between):
    runs = len(between)
    per = nblk // runs
    for i in range(runs):
        hi = nblk if i == runs - 1 else (i + 1) * per
        carry = lax.fori_loop(i * per, hi, step, carry, unroll=True)
        between[i]()
    return carry


def _ssm_fwd_pair(u, ops, nseq, name, comm=None):
    lam, b_bd, c_bd, dsk, w_bd, bias = ops
    rows_total, w = u.shape
    seq = rows_total // nseq
    nb = w // LANES
    s = STATES_PER_BLOCK
    tc = _scan_chunk_rows(seq, nseq)
    nk = seq // tc
    rows = tc * nseq
    nblk = rows // SUBLANES
    stages = SUBLANES // nseq
    two = 2 * LANES
    ncol = 4

    def body(u_ref, lam_ref, b_ref, c_ref, d_ref, w_ref, bias_ref, y_ref, hb_ref, buf_a, buf_b, st, rbuf_a, rbuf_b):
        k = pl.program_id(1)

        @pl.when(k == 0)
        def _():
            st[...] = jnp.zeros_like(st)

        hb_ref[...] = st[...]
        rbufs = (rbuf_a, rbuf_b)
        for q in range(nseq):
            for e in range(2):
                rbufs[e][pl.ds(q, tc, stride=nseq), :] = u_ref[q, :, e * LANES:(e + 1) * LANES]
        row = lax.broadcasted_iota(jnp.int32, (SUBLANES, s), 0)
        bufs = (buf_a, buf_b)

        def u_of(e):
            return rbufs[e][...]

        def project_in(e, j):
            cols = slice(j * (2 * s // ncol), (j + 1) * (2 * s // ncol))
            bufs[e][:, cols] = jnp.dot(u_of(e).astype(BF16), b_ref[e, :, cols], preferred_element_type=F32)

        def scan(e, between):
            buf = bufs[e]
            lr = jnp.broadcast_to(lam_ref[e, :, 0:s], (SUBLANES, s))
            li = jnp.broadcast_to(lam_ref[e, :, s:2 * s], (SUBLANES, s))

            def step(i, carry):
                pr, pi = carry
                r0 = pl.multiple_of(i * SUBLANES, SUBLANES)
                br = buf[pl.ds(r0, SUBLANES), 0:s]
                bi = buf[pl.ds(r0, SUBLANES), s:2 * s]
                outr = outi = None
                for j in range(stages):
                    rr = _roll_rows(pr, nseq)
                    ri = _roll_rows(pi, nseq)
                    pr = lr * rr - li * ri + br
                    pi = lr * ri + li * rr + bi
                    outr = pr if j == 0 else jnp.where(row >= j * nseq, pr, outr)
                    outi = pi if j == 0 else jnp.where(row >= j * nseq, pi, outi)
                buf[pl.ds(r0, SUBLANES), 0:s] = outr
                buf[pl.ds(r0, SUBLANES), s:2 * s] = outi
                return outr, outi

            lo = e * 2 * s
            hr, hi = _scan_with(nblk, step, (st[:, lo:lo + s], st[:, lo + s:lo + 2 * s]), between)
            st[:, lo:lo + s] = hr
            st[:, lo + s:lo + 2 * s] = hi

        part = {}

        def project_out(e, j):
            ks = slice(j * (2 * s // ncol), (j + 1) * (2 * s // ncol))
            p = jnp.dot(bufs[e][:, ks].astype(BF16), c_ref[e, ks, :], preferred_element_type=F32)
            part[e] = p if j == 0 else part[e] + p

        def finish(e):
            y = part[e] + d_ref[e] * u_of(e)
            z = jnp.dot(_gelu(y).astype(BF16), w_ref[e], preferred_element_type=F32) + bias_ref[e]
            part[e] = z[:, 0:LANES] * _sigmoid(z[:, LANES:two])

        nothing = lambda: None
        for j in range(ncol):
            project_in(0, j)
        scan(0, [functools.partial(project_in, 1, j) for j in range(ncol)])
        scan(1, [functools.partial(project_out, 0, j) for j in range(ncol)] + [functools.partial(finish, 0), nothing,
                                                                                nothing, nothing])
        for j in range(ncol):
            project_out(1, j)
        finish(1)
        for e in range(2):
            rbufs[e][...] = part[e]
            for q in range(nseq):
                y_ref[q, :, e * LANES:(e + 1) * LANES] = rbufs[e][pl.ds(q, tc, stride=nseq), :]

    blk = lambda shape: pl.BlockSpec(shape, lambda b, k: (b, 0, 0))
    tok = pl.BlockSpec((nseq, tc, two), lambda b, k: (0, k, b))
    (y, hb), comm_outs = _call(
        body, name=name, grid=(nb // 2, nk),
        in_specs=[tok, blk((2, 1, 2 * s)), blk((2, LANES, 2 * s)), blk((2, 2 * s, LANES)),
                  blk((2, 1, LANES)), blk((2, LANES, two)), blk((2, 1, two))],
        out_specs=[tok, pl.BlockSpec((SUBLANES, 4 * s), lambda b, k: (k, b))],
        out_shape=[jax.ShapeDtypeStruct((nseq, seq, w), F32),
                   jax.ShapeDtypeStruct((nk * SUBLANES, nb * 2 * s), F32)],
        scratch_shapes=[pltpu.VMEM((rows, 2 * s), F32), pltpu.VMEM((rows, 2 * s), F32),
                        pltpu.VMEM((SUBLANES, 4 * s), F32), pltpu.VMEM((rows, LANES), F32),
                        pltpu.VMEM((rows, LANES), F32)],
        semantics=("parallel", "arbitrary"),
        args=(u.reshape(nseq, seq, w), lam, b_bd, c_bd, dsk, w_bd, bias), comm=comm)
    y = y.reshape(nseq * seq, w)
    return (y, hb) if comm is None else (y, hb, comm_outs)


def _ssm_bwd_pair(u, dout, hb, ops, nseq, name, comm=None):
    lam, b_bd, c_bd, dsk, w_bd, bias = ops
    rows_total, w = u.shape
    seq = rows_total // nseq
    nb = w // LANES
    s = STATES_PER_BLOCK
    tc = _scan_chunk_rows(seq, nseq)
    nk = seq // tc
    rows = tc * nseq
    nblk = rows // SUBLANES
    stages = SUBLANES // nseq
    two = 2 * LANES
    ncol = 4
    cw = 2 * s // ncol
    tn_dims = (((0,), (0,)), ((), ()))
    nt_dims = (((1,), (1,)), ((), ()))

    def body(u_ref, dy_ref, hb_ref, lam_ref, b_ref, c_ref, d_ref, w_ref, bias_ref,
             du_ref, dlam_ref, db_ref, dct_ref, dd_ref, dw_ref, dbias_ref,
             hbuf_a, hbuf_b, gbuf_a, gbuf_b, gst, lacc, ru_a, ru_b, rd_a, rd_b):
        k = pl.program_id(1)

        @pl.when(k == 0)
        def _():
            gst[...] = jnp.zeros_like(gst)
            lacc[...] = jnp.zeros_like(lacc)
            db_ref[...] = jnp.zeros_like(db_ref)
            dct_ref[...] = jnp.zeros_like(dct_ref)
            dd_ref[...] = jnp.zeros_like(dd_ref)
            dw_ref[...] = jnp.zeros_like(dw_ref)
            dbias_ref[...] = jnp.zeros_like(dbias_ref)

        hbufs, gbufs, rus, rds = (hbuf_a, hbuf_b), (gbuf_a, gbuf_b), (ru_a, ru_b), (rd_a, rd_b)
        for q in range(nseq):
            for e in range(2):
                rus[e][pl.ds(q, tc, stride=nseq), :] = u_ref[q, :, e * LANES:(e + 1) * LANES]
                rds[e][pl.ds(q, tc, stride=nseq), :] = dy_ref[q, :, e * LANES:(e + 1) * LANES]
        row = lax.broadcasted_iota(jnp.int32, (SUBLANES, s), 0)
        cols = [slice(j * cw, (j + 1) * cw) for j in range(ncol)]
        val = [{}, {}]

        def lam_of(e):
            return (jnp.broadcast_to(lam_ref[e, :, 0:s], (SUBLANES, s)),
                    jnp.broadcast_to(lam_ref[e, :, s:2 * s], (SUBLANES, s)))

        def project_in(e, j):
            hbufs[e][:, cols[j]] = jnp.dot(rus[e][...].astype(BF16), b_ref[e, :, cols[j]], preferred_element_type=F32)

        def scan_fwd(e, between):
            buf = hbufs[e]
            lr, li = lam_of(e)

            def step(i, carry):
                pr, pi = carry
                r0 = pl.multiple_of(i * SUBLANES, SUBLANES)
                br = buf[pl.ds(r0, SUBLANES), 0:s]
                bi = buf[pl.ds(r0, SUBLANES), s:2 * s]
                outr = outi = None
                for j in range(stages):
                    rr = _roll_rows(pr, nseq)
                    ri = _roll_rows(pi, nseq)
                    pr = lr * rr - li * ri + br
                    pi = lr * ri + li * rr + bi
                    outr = pr if j == 0 else jnp.where(row >= j * nseq, pr, outr)
                    outi = pi if j == 0 else jnp.where(row >= j * nseq, pi, outi)
                buf[pl.ds(r0, SUBLANES), 0:s] = outr
                buf[pl.ds(r0, SUBLANES), s:2 * s] = outi
                return outr, outi

            lo = e * 2 * s
            _scan_with(nblk, step, (hb_ref[:, lo:lo + s], hb_ref[:, lo + s:lo + 2 * s]), between)

        def y_part(e, j):
            p = jnp.dot(hbufs[e][:, cols[j]].astype(BF16), c_ref[e, cols[j], :], preferred_element_type=F32)
            val[e]["y"] = p if j == 0 else val[e]["y"] + p

        def gate(e):
            v = val[e]
            uu = rus[e][...]
            yg, dyg_dy = _gelu_and_grad(v.pop("y") + d_ref[e] * uu)
            yg16 = yg.astype(BF16)
            z = jnp.dot(yg16, w_ref[e], preferred_element_type=F32) + bias_ref[e]
            sg = _sigmoid(z[:, LANES:two])
            dout_e = rds[e][...]
            dz = jnp.concatenate([dout_e * sg, dout_e * z[:, 0:LANES] * sg * (1.0 - sg)], axis=-1)
            dz16 = dz.astype(BF16)
            dw_ref[e] += lax.dot_general(yg16, dz16, tn_dims, preferred_element_type=F32)
            dbias_ref[e] += jnp.sum(dz, axis=0, keepdims=True)
            dy = lax.dot_general(dz16, w_ref[e], nt_dims, preferred_element_type=F32) * dyg_dy
            dd_ref[e] += jnp.sum(dy * uu, axis=0, keepdims=True)
            v["dy"] = dy
            v["dy16"] = dy.astype(BF16)

        def dc_part(e, j):
            dct_ref[e, :, cols[j]] += lax.dot_general(val[e]["dy16"], hbufs[e][:, cols[j]].astype(BF16), tn_dims,
                                                      preferred_element_type=F32)

        def dh_part(e, j):
            gbufs[e][:, cols[j]] = lax.dot_general(val[e]["dy16"], c_ref[e, cols[j], :], nt_dims,
                                                   preferred_element_type=F32)

        def scan_bwd(e, between):
            hbuf, gbuf = hbufs[e], gbufs[e]
            lr, li = lam_of(e)
            lo = e * 2 * s

            def step(i, carry):
                pr, pi, ar, ai = carry
                blk = nblk - 1 - i
                r0 = pl.multiple_of(blk * SUBLANES, SUBLANES)
                dr = gbuf[pl.ds(r0, SUBLANES), 0:s]
                di = gbuf[pl.ds(r0, SUBLANES), s:2 * s]
                outr = outi = None
                for j in reversed(range(stages)):
                    rr = _roll_rows(pr, SUBLANES - nseq)
                    ri = _roll_rows(pi, SUBLANES - nseq)
                    pr = dr + lr * rr + li * ri
                    pi = di - li * rr + lr * ri
                    outr = pr if j == stages - 1 else jnp.where(row < (j + 1) * nseq, pr, outr)
                    outi = pi if j == stages - 1 else jnp.where(row < (j + 1) * nseq, pi, outi)
                gbuf[pl.ds(r0, SUBLANES), 0:s] = outr
                gbuf[pl.ds(r0, SUBLANES), s:2 * s] = outi
                p0 = pl.multiple_of(jnp.maximum(blk - 1, 0) * SUBLANES, SUBLANES)
                first = blk == 0
                before_r = jnp.where(first, hb_ref[:, lo:lo + s], hbuf[pl.ds(p0, SUBLANES), 0:s])
                before_i = jnp.where(first, hb_ref[:, lo + s:lo + 2 * s], hbuf[pl.ds(p0, SUBLANES), s:2 * s])
                if stages > 1:
                    last_rows = row >= SUBLANES - nseq
                    before_r = _roll_rows(jnp.where(last_rows, before_r, hbuf[pl.ds(r0, SUBLANES), 0:s]), nseq)
                    before_i = _roll_rows(jnp.where(last_rows, before_i, hbuf[pl.ds(r0, SUBLANES), s:2 * s]), nseq)
                return (outr, outi, ar + outr * before_r + outi * before_i, ai - outr * before_i + outi * before_r)

            gr, gi, ar, ai = _scan_with(
                nblk, step, (gst[:, lo:lo + s], gst[:, lo + s:lo + 2 * s], lacc[:, lo:lo + s], lacc[:, lo + s:lo + 2 * s]),
                between)
            gst[:, lo:lo + s] = gr
            gst[:, lo + s:lo + 2 * s] = gi
            lacc[:, lo:lo + s] = ar
            lacc[:, lo + s:lo + 2 * s] = ai

        def du_part(e, j):
            p = lax.dot_general(gbufs[e][:, cols[j]].astype(BF16), b_ref[e, :, cols[j]], nt_dims,
                                preferred_element_type=F32)
            val[e]["du"] = (val[e].pop("dy") * d_ref[e] + p) if j == 0 else val[e]["du"] + p

        def db_part(e, j):
            db_ref[e, :, cols[j]] += lax.dot_general(rus[e][...].astype(BF16), gbufs[e][:, cols[j]].astype(BF16),
                                                     tn_dims, preferred_element_type=F32)

        def parts(fn, e):
            return [functools.partial(fn, e, j) for j in range(ncol)]

        nothing = lambda: None
        middle_of = lambda e: parts(y_part, e) + [functools.partial(gate, e)] + parts(dc_part, e) + parts(dh_part, e)
        last_of = lambda e: parts(du_part, e) + parts(db_part, e)
        for piece in parts(project_in, 0):
            piece()
        scan_fwd(0, parts(project_in, 1))
        scan_fwd(1, middle_of(0) + [nothing] * 3)
        scan_bwd(0, middle_of(1) + [nothing] * 3)
        scan_bwd(1, last_of(0))
        for piece in last_of(1):
            piece()
        for e in range(2):
            rus[e][...] = val[e]["du"]
            for q in range(nseq):
                du_ref[q, :, e * LANES:(e + 1) * LANES] = rus[e][pl.ds(q, tc, stride=nseq), :].astype(du_ref.dtype)

        @pl.when(k == nk - 1)
        def _():
            for e in range(2):
                dlam_ref[e] = jnp.sum(lacc[:, e * 2 * s:(e + 1) * 2 * s], axis=0, keepdims=True)

    blk = lambda shape: pl.BlockSpec(shape, lambda b, k: (b, 0, 0))
    tok = pl.BlockSpec((nseq, tc, two), lambda b, k: (0, nk - 1 - k, b))
    outs, comm_outs = _call(
        body, name=name, grid=(nb // 2, nk),
        in_specs=[tok, tok, pl.BlockSpec((SUBLANES, 4 * s), lambda b, k: (nk - 1 - k, b)),
                  blk((2, 1, 2 * s)), blk((2, LANES, 2 * s)), blk((2, 2 * s, LANES)),
                  blk((2, 1, LANES)), blk((2, LANES, two)), blk((2, 1, two))],
        out_specs=[tok, blk((2, 1, 2 * s)), blk((2, LANES, 2 * s)), blk((2, LANES, 2 * s)),
                   blk((2, 1, LANES)), blk((2, LANES, two)), blk((2, 1, two))],
        out_shape=[jax.ShapeDtypeStruct((nseq, seq, w), BF16),
                   jax.ShapeDtypeStruct((nb, 1, 2 * s), F32), jax.ShapeDtypeStruct((nb, LANES, 2 * s), F32),
                   jax.ShapeDtypeStruct((nb, LANES, 2 * s), F32), jax.ShapeDtypeStruct((nb, 1, LANES), F32),
                   jax.ShapeDtypeStruct((nb, LANES, two), F32), jax.ShapeDtypeStruct((nb, 1, two), F32)],
        scratch_shapes=[pltpu.VMEM((rows, 2 * s), F32)] * 4
        + [pltpu.VMEM((SUBLANES, 4 * s), F32), pltpu.VMEM((SUBLANES, 4 * s), F32)]
        + [pltpu.VMEM((rows, LANES), F32)] * 4,
        semantics=("parallel", "arbitrary"),
        args=(u.reshape(nseq, seq, w), dout.reshape(nseq, seq, w), hb, lam, b_bd, c_bd, dsk, w_bd, bias), comm=comm)
    outs[0] = outs[0].reshape(nseq * seq, w)
    return outs if comm is None else (outs, comm_outs)


def _ssm_bwd(u, dout, hb, ops, nseq, name, comm=None):
    lam, b_bd, c_bd, dsk, w_bd, bias = ops
    rows_total, w = u.shape
    seq = rows_total // nseq
    nb = w // LANES
    s = STATES_PER_BLOCK
    tc = _scan_chunk_rows(seq, nseq)
    nk = seq // tc
    rows = tc * nseq
    nblk = rows // SUBLANES
    stages = SUBLANES // nseq
    tn_dims = (((0,), (0,)), ((), ()))
    nt_dims = (((1,), (1,)), ((), ()))

    def body(u_ref, dy_ref, hb_ref, lam_ref, b_ref, c_ref, d_ref, w_ref, bias_ref,
             du_ref, dlam_ref, db_ref, dct_ref, dd_ref, dw_ref, dbias_ref, hbuf, gbuf, gst, lacc, rbuf, rbuf2):
        k = pl.program_id(1)

        @pl.when(k == 0)
        def _():
            gst[...] = jnp.zeros_like(gst)
            lacc[...] = jnp.zeros_like(lacc)
            db_ref[...] = jnp.zeros_like(db_ref)
            dct_ref[...] = jnp.zeros_like(dct_ref)
            dd_ref[...] = jnp.zeros_like(dd_ref)
            dw_ref[...] = jnp.zeros_like(dw_ref)
            dbias_ref[...] = jnp.zeros_like(dbias_ref)

        for q in range(nseq):
            rbuf[pl.ds(q, tc, stride=nseq), :] = u_ref[q]
            rbuf2[pl.ds(q, tc, stride=nseq), :] = dy_ref[q]
        u = rbuf[...]
        ub = u.astype(BF16)
        lr = jnp.broadcast_to(lam_ref[0, :, 0:s], (SUBLANES, s))
        li = jnp.broadcast_to(lam_ref[0, :, s:2 * s], (SUBLANES, s))
        row = lax.broadcasted_iota(jnp.int32, (SUBLANES, s), 0)
        hbuf[...] = jnp.dot(ub, b_ref[0], preferred_element_type=F32)

        def fstep(i, carry):
            pr, pi = carry
            r0 = pl.multiple_of(i * SUBLANES, SUBLANES)
            br = hbuf[pl.ds(r0, SUBLANES), 0:s]
            bi = hbuf[pl.ds(r0, SUBLANES), s:2 * s]
            outr = outi = None
            for j in range(stages):
                rr = _roll_rows(pr, nseq)
                ri = _roll_rows(pi, nseq)
                pr = lr * rr - li * ri + br
                pi = lr * ri + li * rr + bi
                outr = pr if j == 0 else jnp.where(row >= j * nseq, pr, outr)
                outi = pi if j == 0 else jnp.where(row >= j * nseq, pi, outi)
            hbuf[pl.ds(r0, SUBLANES), 0:s] = outr
            hbuf[pl.ds(r0, SUBLANES), s:2 * s] = outi
            return outr, outi

        lax.fori_loop(0, nblk, fstep, (hb_ref[:, 0:s], hb_ref[:, s:2 * s]), unroll=2)
        hb16 = hbuf[...].astype(BF16)
        y = jnp.dot(hb16, c_ref[0], preferred_element_type=F32) + d_ref[0] * u
        yg, dyg_dy = _gelu_and_grad(y)
        yg16 = yg.astype(BF16)
        z = jnp.dot(yg16, w_ref[0], preferred_element_type=F32) + bias_ref[0]
        z1 = z[:, 0:LANES]
        sg = _sigmoid(z[:, LANES:2 * LANES])
        dout = rbuf2[...]
        dz = jnp.concatenate([dout * sg, dout * z1 * sg * (1.0 - sg)], axis=-1)
        dz16 = dz.astype(BF16)
        dw_ref[0] += lax.dot_general(yg16, dz16, tn_dims, preferred_element_type=F32)
        dbias_ref[0] += jnp.sum(dz, axis=0, keepdims=True)
        dy = lax.dot_general(dz16, w_ref[0], nt_dims, preferred_element_type=F32) * dyg_dy
        dy16 = dy.astype(BF16)
        dd_ref[0] += jnp.sum(dy * u, axis=0, keepdims=True)
        dct_ref[0] += lax.dot_general(dy16, hb16, tn_dims, preferred_element_type=F32)
        gbuf[...] = lax.dot_general(dy16, c_ref[0], nt_dims, preferred_element_type=F32)

        def bstep(i, carry):
            pr, pi, ar, ai = carry
            blk = nblk - 1 - i
            r0 = pl.multiple_of(blk * SUBLANES, SUBLANES)
            dr = gbuf[pl.ds(r0, SUBLANES), 0:s]
            di = gbuf[pl.ds(r0, SUBLANES), s:2 * s]
            outr = outi = None
            for j in reversed(range(stages)):
                rr = _roll_rows(pr, SUBLANES - nseq)
                ri = _roll_rows(pi, SUBLANES - nseq)
                pr = dr + lr * rr + li * ri
                pi = di - li * rr + lr * ri
                outr = pr if j == stages - 1 else jnp.where(row < (j + 1) * nseq, pr, outr)
                outi = pi if j == stages - 1 else jnp.where(row < (j + 1) * nseq, pi, outi)
            gbuf[pl.ds(r0, SUBLANES), 0:s] = outr
            gbuf[pl.ds(r0, SUBLANES), s:2 * s] = outi
            p0 = pl.multiple_of(jnp.maximum(blk - 1, 0) * SUBLANES, SUBLANES)
            first = blk == 0
            before_r = jnp.where(first, hb_ref[:, 0:s], hbuf[pl.ds(p0, SUBLANES), 0:s])
            before_i = jnp.where(first, hb_ref[:, s:2 * s], hbuf[pl.ds(p0, SUBLANES), s:2 * s])
            if stages > 1:
                last_rows = row >= SUBLANES - nseq
                before_r = _roll_rows(jnp.where(last_rows, before_r, hbuf[pl.ds(r0, SUBLANES), 0:s]), nseq)
                before_i = _roll_rows(jnp.where(last_rows, before_i, hbuf[pl.ds(r0, SUBLANES), s:2 * s]), nseq)
            return (outr, outi, ar + outr * before_r + outi * before_i, ai - outr * before_i + outi * before_r)

        gr, gi, ar, ai = lax.fori_loop(
            0, nblk, bstep, (gst[:, 0:s], gst[:, s:2 * s], lacc[:, 0:s], lacc[:, s:2 * s]))
        gst[:, 0:s] = gr
        gst[:, s:2 * s] = gi
        lacc[:, 0:s] = ar
        lacc[:, s:2 * s] = ai
        g16 = gbuf[...].astype(BF16)
        rbuf[...] = dy * d_ref[0] + lax.dot_general(g16, b_ref[0], nt_dims, preferred_element_type=F32)
        for q in range(nseq):
            du_ref[q] = rbuf[pl.ds(q, tc, stride=nseq), :]
        db_ref[0] += lax.dot_general(ub, g16, tn_dims, preferred_element_type=F32)

        @pl.when(k == nk - 1)
        def _():
            dlam_ref[0] = jnp.sum(lacc[...], axis=0, keepdims=True)

    blk = lambda shape: pl.BlockSpec(shape, lambda b, k: (b, 0, 0))
    rev = lambda b, k: (nk - 1 - k, b)
    tok = pl.BlockSpec((nseq, tc, LANES), lambda b, k: (0, nk - 1 - k, b))
    outs, comm_outs = _call(
        body, name=name, grid=(nb, nk),
        in_specs=[tok, tok, pl.BlockSpec((SUBLANES, 2 * s), rev),
                  blk((1, 1, 2 * s)), blk((1, LANES, 2 * s)), blk((1, 2 * s, LANES)),
                  blk((1, 1, LANES)), blk((1, LANES, 2 * LANES)), blk((1, 1, 2 * LANES))],
        out_specs=[tok, blk((1, 1, 2 * s)), blk((1, LANES, 2 * s)), blk((1, LANES, 2 * s)),
                   blk((1, 1, LANES)), blk((1, LANES, 2 * LANES)), blk((1, 1, 2 * LANES))],
        out_shape=[jax.ShapeDtypeStruct((nseq, seq, w), F32),
                   jax.ShapeDtypeStruct((nb, 1, 2 * s), F32), jax.ShapeDtypeStruct((nb, LANES, 2 * s), F32),
                   jax.ShapeDtypeStruct((nb, LANES, 2 * s), F32), jax.ShapeDtypeStruct((nb, 1, LANES), F32),
                   jax.ShapeDtypeStruct((nb, LANES, 2 * LANES), F32), jax.ShapeDtypeStruct((nb, 1, 2 * LANES), F32)],
        scratch_shapes=[pltpu.VMEM((rows, 2 * s), F32), pltpu.VMEM((rows, 2 * s), F32),
                        pltpu.VMEM((SUBLANES, 2 * s), F32), pltpu.VMEM((SUBLANES, 2 * s), F32),
                        pltpu.VMEM((rows, LANES), F32), pltpu.VMEM((rows, LANES), F32)],
        semantics=("parallel", "arbitrary"),
        args=(u.reshape(nseq, seq, w), dout.reshape(nseq, seq, w), hb, lam, b_bd, c_bd, dsk, w_bd, bias), comm=comm)
    outs[0] = outs[0].reshape(nseq * seq, w)
    return outs if comm is None else (outs, comm_outs)


ANY = pl.BlockSpec(memory_space=pl.ANY)

BIG = (("ffn1_w_in", True), ("ffn1_w_out", False), ("mix_w_in", True), ("mix_w_out", False),
       ("ffn2_w_in", True), ("ffn2_w_out", False))


def _my_place():
    return lax.axis_index("x"), lax.axis_index("y"), lax.axis_index("c")


def _other_chips(x, y):
    return [(1 - x, y), (x, 1 - y), (1 - x, 1 - y)]


def _half_of_shard(ref, col_sharded, chip, core):
    full_rows, full_cols = ref.shape
    if col_sharded:
        hr, cs = full_rows // 2, full_cols // N_CHIPS
        return ref.at[pl.ds(pl.multiple_of(core * hr, 8), hr), pl.ds(chip * cs, cs)]
    rs = full_rows // N_CHIPS
    return ref.at[pl.ds(pl.multiple_of(chip * rs + core * (rs // 2), 8), rs // 2), :]


def _gather_comm(shards, cols):
    full_shapes = [(sh.shape[0], sh.shape[1] * N_CHIPS) if col else (sh.shape[0] * N_CHIPS, sh.shape[1])
                   for sh, col in zip(shards, cols)]
    nw = len(shards)

    def first_copies(ins, outs, sems):
        send_sems, recv_sems, local_sems = sems
        x, y, c = _my_place()
        me = 2 * x + y
        locals_, sends = [], []
        for wi in range(nw):
            src, dst = ins[wi], outs[wi]
            rs, cs = src.shape
            hs = rs // 2
            if cols[wi]:
                place = dst.at[:, pl.ds(me * cs, cs)]
            else:
                place = dst.at[pl.ds(pl.multiple_of(me * rs, 8), rs), :]
            locals_.append(pltpu.make_async_copy(src, place, local_sems.at[wi]))
            my_half = src.at[pl.ds(pl.multiple_of(c * hs, 8), hs), :]
            for j, (px, py) in enumerate(_other_chips(x, y)):
                sends.append(pltpu.make_async_remote_copy(
                    src_ref=my_half, dst_ref=_half_of_shard(dst, cols[wi], me, c),
                    send_sem=send_sems.at[wi * 6 + j], recv_sem=recv_sems.at[wi * 6 + j],
                    device_id=(px, py, c), device_id_type=MESH))
        return locals_, sends

    def start(ins, outs, sems):
        locals_, sends = first_copies(ins, outs, sems)
        for cp in locals_ + sends:
            cp.start()

    def forwards(outs, sems, wait_landed):
        send_sems, recv_sems, _ = sems
        x, y, c = _my_place()
        out = []
        for wi in range(nw):
            dst = outs[wi]
            for j, (px, py) in enumerate(_other_chips(x, y)):
                got = _half_of_shard(dst, cols[wi], 2 * px + py, c)
                if wait_landed:
                    pltpu.make_async_remote_copy(
                        src_ref=got, dst_ref=got, send_sem=send_sems.at[wi * 6 + j], recv_sem=recv_sems.at[wi * 6 + j],
                        device_id=(px, py, c), device_id_type=MESH).wait_recv()
                out.append(pltpu.make_async_remote_copy(
                    src_ref=got, dst_ref=got, send_sem=send_sems.at[wi * 6 + 3 + j], recv_sem=recv_sems.at[wi * 6 + 3 + j],
                    device_id=(x, y, 1 - c), device_id_type=MESH))
                if wait_landed:
                    out[-1].start()
        return out

    def middle(ins, outs, sems):
        forwards(outs, sems, True)

    def finish(ins, outs, sems):
        send_sems, recv_sems, _ = sems
        x, y, c = _my_place()
        locals_, sends = first_copies(ins, outs, sems)
        for wi in range(nw):
            dst = outs[wi]
            for j, (px, py) in enumerate(_other_chips(x, y)):
                theirs = _half_of_shard(dst, cols[wi], 2 * px + py, 1 - c)
                pltpu.make_async_remote_copy(
                    src_ref=theirs, dst_ref=theirs, send_sem=send_sems.at[wi * 6 + 3 + j],
                    recv_sem=recv_sems.at[wi * 6 + 3 + j], device_id=(x, y, 1 - c), device_id_type=MESH).wait_recv()
        for cp in sends + forwards(outs, sems, False):
            cp.wait_send()
        for cp in locals_:
            cp.wait()

    return _Comm(shards, [jax.ShapeDtypeStruct(s, BF16) for s in full_shapes],
                 [pltpu.SemaphoreType.DMA((6 * nw,)), pltpu.SemaphoreType.DMA((6 * nw,)),
                  pltpu.SemaphoreType.DMA((nw,))], start, finish, middle=middle)


def _pair_exchange_comm(grads, cols):
    nw = len(grads)
    n_copies = sum(1 if col else N_CHIPS for col in cols)

    def copies(ins, outs, sems):
        send_sems, recv_sems = sems
        x, y, c = _my_place()
        out = []
        for wi in range(nw):
            src, dst = ins[wi], outs[wi]
            fr = src.shape[0]
            if cols[wi]:
                hr = fr // 2
                pieces = [(src.at[pl.ds(pl.multiple_of((1 - c) * hr, 8), hr), :], dst)]
            else:
                rs = fr // N_CHIPS
                hs = rs // 2
                pieces = [(src.at[pl.ds(pl.multiple_of(k * rs + (1 - c) * hs, 8), hs), :],
                           dst.at[pl.ds(k * hs, hs), :]) for k in range(N_CHIPS)]
            for s_ref, d_ref in pieces:
                out.append(pltpu.make_async_remote_copy(
                    src_ref=s_ref, dst_ref=d_ref, send_sem=send_sems.at[len(out)], recv_sem=recv_sems.at[len(out)],
                    device_id=(x, y, 1 - c), device_id_type=MESH))
        return out

    def start(ins, outs, sems):
        for cp in copies(ins, outs, sems):
            cp.start()

    def finish(ins, outs, sems):
        for cp in copies(ins, outs, sems):
            cp.wait()

    return _Comm(grads, [jax.ShapeDtypeStruct((g.shape[0] // 2, g.shape[1]), F32) for g in grads],
                 [pltpu.SemaphoreType.DMA((n_copies,)), pltpu.SemaphoreType.DMA((n_copies,))], start, finish)


def _pair_sum(grad, other, col, core, name):
    fr, fc = grad.shape
    pieces = 1 if col else N_CHIPS
    pr = fr // 2 // pieces
    gview = grad.reshape(pieces * 2, pr, fc)
    oview = other.reshape(pieces, pr, fc)
    tr = _tile(pr, 256, 16)

    def body(c_ref, g_ref, o_ref, out_ref):
        out_ref[...] = (g_ref[...] + o_ref[...]).astype(out_ref.dtype)

    out = pl.pallas_call(
        body, name=name,
        grid_spec=pltpu.PrefetchScalarGridSpec(
            num_scalar_prefetch=1, grid=(pieces, pr // tr),
            in_specs=[pl.BlockSpec((1, tr, fc), lambda p, i, cref: (p * 2 + cref[0], i, 0)),
                      pl.BlockSpec((1, tr, fc), lambda p, i, cref: (p, i, 0))],
            out_specs=pl.BlockSpec((1, tr, fc), lambda p, i, cref: (p, i, 0))),
        out_shape=jax.ShapeDtypeStruct((pieces, pr, fc), BF16),
        compiler_params=_params("parallel", "parallel"),
    )(core, gview, oview)
    return out.reshape(fr // 2, fc)


def _chip_exchange_comm(psums, cols):
    nw = len(psums)
    out_shapes = [(N_CHIPS, p.shape[0], p.shape[1] // N_CHIPS) if col else (N_CHIPS, p.shape[0] // N_CHIPS, p.shape[1])
                  for p, col in zip(psums, cols)]

    def copies(ins, outs, sems):
        send_sems, recv_sems, local_sems = sems
        x, y, c = _my_place()
        me = 2 * x + y
        out = []
        for wi in range(nw):
            src = ins[wi]
            mine = outs[wi].at[me]

            def piece(chip, src=src, col=cols[wi]):
                if col:
                    cs = src.shape[1] // N_CHIPS
                    return src.at[:, pl.ds(chip * cs, cs)]
                ps = src.shape[0] // N_CHIPS
                return src.at[pl.ds(pl.multiple_of(chip * ps, 8), ps), :]

            out.append(pltpu.make_async_copy(piece(me), mine, local_sems.at[wi]))
            for j, (px, py) in enumerate(_other_chips(x, y)):
                out.append(pltpu.make_async_remote_copy(
                    src_ref=piece(2 * px + py), dst_ref=mine,
                    send_sem=send_sems.at[wi * 3 + j], recv_sem=recv_sems.at[wi * 3 + j],
                    device_id=(px, py, c), device_id_type=MESH))
        return out

    def start(ins, outs, sems):
        for cp in copies(ins, outs, sems):
            cp.start()

    def finish(ins, outs, sems):
        for cp in copies(ins, outs, sems):
            cp.wait()

    return _Comm(psums, [jax.ShapeDtypeStruct(s, BF16) for s in out_shapes],
                 [pltpu.SemaphoreType.DMA((3 * nw,)), pltpu.SemaphoreType.DMA((3 * nw,)),
                  pltpu.SemaphoreType.DMA((nw,))], start, finish)


def _chip_sum(slots, core, layer, layers, into, name):
    _, hr, cs = slots.shape
    tr = _tile(hr, 256, 16)

    def body(c_ref, s_ref, *rest):
        out_ref = rest[-1]
        acc = s_ref[0].astype(F32)
        for i in range(1, N_CHIPS):
            acc = acc + s_ref[i].astype(F32)
        out_ref[0] = acc

    in_specs = [pl.BlockSpec((N_CHIPS, tr, cs), lambda i, cref: (0, i, 0))]
    args = [core, slots]
    aliases = {}
    if into is not None:
        in_specs.append(pl.BlockSpec(memory_space=pl.ANY))
        args.append(into.reshape(layers * 2, hr, cs))
        aliases = {2: 0}
    out = pl.pallas_call(
        body, name=name,
        grid_spec=pltpu.PrefetchScalarGridSpec(
            num_scalar_prefetch=1, grid=(hr // tr,), in_specs=in_specs,
            out_specs=pl.BlockSpec((1, tr, cs), lambda i, cref: (layer * 2 + cref[0], i, 0))),
        out_shape=jax.ShapeDtypeStruct((layers * 2, hr, cs), F32),
        input_output_aliases=aliases,
        compiler_params=_params("parallel"),
    )(*args)
    return out.reshape(layers, 2 * hr, cs)


def _pair_share_comm(reduced):
    nw = len(reduced)

    def copies(ins, outs, sems):
        send_sems, recv_sems = sems
        x, y, c = _my_place()
        out = []
        for wi in range(nw):
            hs = outs[wi].shape[1] // 2
            mine = outs[wi].at[:, pl.ds(pl.multiple_of(c * hs, 8), hs), :]
            out.append(pltpu.make_async_remote_copy(
                src_ref=mine, dst_ref=mine, send_sem=send_sems.at[wi], recv_sem=recv_sems.at[wi],
                device_id=(x, y, 1 - c), device_id_type=MESH))
        return out

    def start(ins, outs, sems):
        for cp in copies(ins, outs, sems):
            cp.start()

    def finish(ins, outs, sems):
        for cp in copies(ins, outs, sems):
            cp.wait()

    return _Comm(reduced, [jax.ShapeDtypeStruct(r.shape, F32) for r in reduced],
                 [pltpu.SemaphoreType.DMA((nw,)), pltpu.SemaphoreType.DMA((nw,))], start, finish,
                 alias={i: i for i in range(nw)})


def _all_reduce_small(flat, comm):
    rows, lanes = flat.shape
    seg = rows // N_DEV
    c_in, c_out = len(comm.ins), len(comm.outs)

    def body(*refs):
        refs = list(refs)
        in_ref, cins = refs[0], refs[1:1 + c_in]
        out_ref, couts = refs[1 + c_in], refs[2 + c_in:2 + c_in + c_out]
        recv_ref, send_sems, recv_sems = refs[2 + c_in + c_out:5 + c_in + c_out]
        csems = refs[5 + c_in + c_out:]
        comm.start(cins, couts, csems)
        x, y, c = _my_place()
        me = 4 * x + 2 * y + c

        def peer(r):
            fx, fy, fc = (r >> 2) & 1, (r >> 1) & 1, r & 1
            px = jnp.where(fx == 1, 1 - x, x)
            py = jnp.where(fy == 1, 1 - y, y)
            pc = jnp.where(fc == 1, 1 - c, c)
            return px, py, pc

        first = []
        for r in range(1, N_DEV):
            px, py, pc = peer(r)
            theirs = in_ref.at[pl.ds(pl.multiple_of((4 * px + 2 * py + pc) * seg, 8), seg), :]
            cp = pltpu.make_async_remote_copy(
                src_ref=theirs, dst_ref=recv_ref.at[r], send_sem=send_sems.at[r - 1], recv_sem=recv_sems.at[r - 1],
                device_id=(px, py, pc), device_id_type=MESH)
            cp.start()
            first.append(cp)
        for cp in first:
            cp.wait()
        my_rows = pl.ds(pl.multiple_of(me * seg, 8), seg)
        acc = in_ref[my_rows, :]
        for r in range(1, N_DEV):
            acc = acc + recv_ref[r]
        out_ref[my_rows, :] = acc
        second = []
        for r in range(1, N_DEV):
            px, py, pc = peer(r)
            cp = pltpu.make_async_remote_copy(
                src_ref=out_ref.at[my_rows, :], dst_ref=out_ref.at[my_rows, :],
                send_sem=send_sems.at[6 + r], recv_sem=recv_sems.at[6 + r],
                device_id=(px, py, pc), device_id_type=MESH)
            cp.start()
            second.append(cp)
        for r in range(1, N_DEV):
            px, py, pc = peer(r)
            theirs = out_ref.at[pl.ds(pl.multiple_of((4 * px + 2 * py + pc) * seg, 8), seg), :]
            pltpu.make_async_remote_copy(
                src_ref=theirs, dst_ref=theirs, send_sem=send_sems.at[6 + r], recv_sem=recv_sems.at[6 + r],
                device_id=(px, py, pc), device_id_type=MESH).wait_recv()
        for cp in second:
            cp.wait_send()
        comm.finish(cins, couts, csems)

    vm = pl.BlockSpec(memory_space=pltpu.VMEM)
    any_spec = pl.BlockSpec(memory_space=pl.ANY)
    outs = pl.pallas_call(
        body, name="all_reduce_small",
        in_specs=[vm] + [any_spec] * c_in, out_specs=[vm] + [any_spec] * c_out,
        out_shape=[jax.ShapeDtypeStruct((rows, lanes), F32)] + comm.outs,
        scratch_shapes=[pltpu.VMEM((N_DEV, seg, lanes), F32),
                        pltpu.SemaphoreType.DMA((2 * (N_DEV - 1),)), pltpu.SemaphoreType.DMA((2 * (N_DEV - 1),))]
        + comm.sems,
        input_output_aliases={1 + ci: 1 + co for ci, co in comm.alias.items()},
        compiler_params=pltpu.CompilerParams(vmem_limit_bytes=VMEM_LIMIT),
    )(flat, *comm.ins)
    return outs[0], list(outs[1:])


def _adamw_update(w_ref, g_ref, m_ref, v_ref, d_ref, nm_ref, nv_ref):
    c1 = 1.0 - ADAM_B1 ** ADAM_STEP
    c2 = 1.0 - ADAM_B2 ** ADAM_STEP
    gv = g_ref[...]
    nm = ADAM_B1 * m_ref[...] + (1.0 - ADAM_B1) * gv
    nv = ADAM_B2 * v_ref[...] + (1.0 - ADAM_B2) * (gv * gv)
    d_ref[...] = -ADAM_LR * ((nm / c1) / (jnp.sqrt(nv / c2) + ADAM_EPS) + ADAM_WD * w_ref[...])
    nm_ref[...] = nm
    nv_ref[...] = nv


def _adamw_many(ws, gs, ms, vs, name):
    n = len(ws)

    def body(*refs):
        for i in range(n):
            _adamw_update(*[refs[k * n + i] for k in range(7)])

    shapes = [jax.ShapeDtypeStruct(w.shape, F32) for w in ws]
    outs = pl.pallas_call(
        body, name=name, out_shape=shapes * 3,
        compiler_params=pltpu.CompilerParams(vmem_limit_bytes=VMEM_LIMIT),
    )(*ws, *gs, *ms, *vs)
    return outs[:n], outs[n:2 * n], outs[2 * n:]


def _adamw(w, g, m, v, name):
    rows, cols = w.shape
    tr = _tile(rows, 256, 8)

    def body(w_ref, g_ref, m_ref, v_ref, go_ref, d_ref, nm_ref, nv_ref):
        go_ref[...] = g_ref[...]
        _adamw_update(w_ref, g_ref, m_ref, v_ref, d_ref, nm_ref, nv_ref)

    blk = pl.BlockSpec((tr, cols), lambda i: (i, 0))
    sds = jax.ShapeDtypeStruct((rows, cols), F32)
    return pl.pallas_call(
        body, name=name, grid=(rows // tr,),
        in_specs=[blk] * 4, out_specs=[blk] * 4, out_shape=[sds] * 4,
        compiler_params=_params("parallel"),
    )(w, g, m, v)


SMALL = ("norm_ffn1", "norm_mix", "ssm_a_re", "ssm_a_im", "ssm_log_dt", "ssm_b_re", "ssm_b_im", "ssm_c_re",
         "ssm_c_im", "ssm_d", "ssm_glu_w", "ssm_glu_b", "gm_v_gain", "gm_w_s", "gm_b_s", "gain_ssm_out",
         "gain_gm_out", "norm_ffn2", "norm_final")
WEIGHTS = ("norm_ffn1", "ffn1_w_in", "ffn1_w_out", "norm_mix", "mix_w_in", "ssm_a_re", "ssm_a_im", "ssm_log_dt",
           "ssm_b_re", "ssm_b_im", "ssm_c_re", "ssm_c_im", "ssm_d", "ssm_glu_w", "ssm_glu_b", "gm_v_gain", "gm_w_s",
           "gm_b_s", "gain_ssm_out", "gain_gm_out", "mix_w_out", "norm_ffn2", "ffn2_w_in", "ffn2_w_out", "norm_final")


def _ffn_fwd(x, gain, w_in, w_out, tag, hosted=None):
    if hosted is None:
        h, t, q, a = _ffn_in_fwd(x, gain, w_in, f"{tag}_in")
    else:
        (h, t, q, a), got = _ffn_in_fwd(x, gain, w_in, f"{tag}_in_hosting", comm=hosted[0]())
        hosted[1](got)
    if callable(w_out):
        w_out = w_out()
    out = _matmul(a, w_out, "nn", scale=0.5, res=x, tm=512, tn=1024, tk=4096, name=f"{tag}_out")
    return out, (x, h, t, q, a)


def _ffn_bwd(dout, saved, gain, w_in, w_out, tag, hooks=None, publish=None, late_out_dw=False):
    x, h, t, q, a = saved
    f = t.shape[1]
    hooks = hooks or {}

    def hosted(key, fn, *args, name, **kw):
        if key not in hooks:
            return fn(*args, name=name, **kw)
        make, take = hooks[key]
        *res, got = fn(*args, name=f"{name}_hosting", comm=make(), **kw)
        take(got)
        return res[0] if len(res) == 1 else tuple(res)

    def out_dw():
        dw = hosted("out_dw", _matmul, a, dout, "tn", scale=0.5, tm=1536, tn=1024, tk=2048, name=f"{tag}_out_dw")
        if publish is not None:
            publish("out", dw)
        return dw

    dg, du = hosted("out_dx", _ffn_out_bwd, dout, w_out, t, q, name=f"{tag}_out_dx")
    if not late_out_dw:
        dw_out = out_dw()
    dw_in = hosted("in_dw_g", _matmul, h, dg, "tn", tm=1024, tn=1536, tk=2048, name=f"{tag}_in_dw_g",
                   out_cols=2 * f)
    dw_in = hosted("in_dw_u", _matmul, h, du, "tn", tm=1024, tn=1536, tk=2048, name=f"{tag}_in_dw_u",
                   out_cols=2 * f, col_off=f, into=dw_in)
    if publish is not None:
        publish("in", dw_in)
    if late_out_dw:
        dw_out = out_dw()
    dx, dgain = hosted("in_dx", _proj_in_bwd, [(dg, 0), (du, f)], w_in, x, gain, dout, name=f"{tag}_in_dx")
    return dx, dgain, dw_in, dw_out


def kernel(x, norm_ffn1, ffn1_w_in, ffn1_w_out, norm_mix, mix_w_in, ssm_a_re, ssm_a_im, ssm_log_dt, ssm_b_re, ssm_b_im, ssm_c_re, ssm_c_im, ssm_d, ssm_glu_w, ssm_glu_b, gm_v_gain, gm_w_s, gm_b_s, gain_ssm_out, gain_gm_out, mix_w_out, norm_ffn2, ffn2_w_in, ffn2_w_out, norm_final, loss_target, m_norm_ffn1, m_ffn1_w_in, m_ffn1_w_out, m_norm_mix, m_mix_w_in, m_ssm_a_re, m_ssm_a_im, m_ssm_log_dt, m_ssm_b_re, m_ssm_b_im, m_ssm_c_re, m_ssm_c_im, m_ssm_d, m_ssm_glu_w, m_ssm_glu_b, m_gm_v_gain, m_gm_w_s, m_gm_b_s, m_gain_ssm_out, m_gain_gm_out, m_mix_w_out, m_norm_ffn2, m_ffn2_w_in, m_ffn2_w_out, m_norm_final, v_norm_ffn1, v_ffn1_w_in, v_ffn1_w_out, v_norm_mix, v_mix_w_in, v_ssm_a_re, v_ssm_a_im, v_ssm_log_dt, v_ssm_b_re, v_ssm_b_im, v_ssm_c_re, v_ssm_c_im, v_ssm_d, v_ssm_glu_w, v_ssm_glu_b, v_gm_v_gain, v_gm_w_s, v_gm_b_s, v_gain_ssm_out, v_gain_gm_out, v_mix_w_out, v_norm_ffn2, v_ffn2_w_in, v_ffn2_w_out, v_norm_final):
    wts = dict(norm_ffn1=norm_ffn1, ffn1_w_in=ffn1_w_in, ffn1_w_out=ffn1_w_out, norm_mix=norm_mix, mix_w_in=mix_w_in,
               ssm_a_re=ssm_a_re, ssm_a_im=ssm_a_im, ssm_log_dt=ssm_log_dt, ssm_b_re=ssm_b_re, ssm_b_im=ssm_b_im,
               ssm_c_re=ssm_c_re, ssm_c_im=ssm_c_im, ssm_d=ssm_d, ssm_glu_w=ssm_glu_w, ssm_glu_b=ssm_glu_b,
               gm_v_gain=gm_v_gain, gm_w_s=gm_w_s, gm_b_s=gm_b_s, gain_ssm_out=gain_ssm_out, gain_gm_out=gain_gm_out,
               mix_w_out=mix_w_out, norm_ffn2=norm_ffn2, ffn2_w_in=ffn2_w_in, ffn2_w_out=ffn2_w_out,
               norm_final=norm_final)
    mom = dict(norm_ffn1=m_norm_ffn1, ffn1_w_in=m_ffn1_w_in, ffn1_w_out=m_ffn1_w_out, norm_mix=m_norm_mix,
               mix_w_in=m_mix_w_in, ssm_a_re=m_ssm_a_re, ssm_a_im=m_ssm_a_im, ssm_log_dt=m_ssm_log_dt,
               ssm_b_re=m_ssm_b_re, ssm_b_im=m_ssm_b_im, ssm_c_re=m_ssm_c_re, ssm_c_im=m_ssm_c_im, ssm_d=m_ssm_d,
               ssm_glu_w=m_ssm_glu_w, ssm_glu_b=m_ssm_glu_b, gm_v_gain=m_gm_v_gain, gm_w_s=m_gm_w_s, gm_b_s=m_gm_b_s,
               gain_ssm_out=m_gain_ssm_out, gain_gm_out=m_gain_gm_out, mix_w_out=m_mix_w_out, norm_ffn2=m_norm_ffn2,
               ffn2_w_in=m_ffn2_w_in, ffn2_w_out=m_ffn2_w_out, norm_final=m_norm_final)
    var = dict(norm_ffn1=v_norm_ffn1, ffn1_w_in=v_ffn1_w_in, ffn1_w_out=v_ffn1_w_out, norm_mix=v_norm_mix,
               mix_w_in=v_mix_w_in, ssm_a_re=v_ssm_a_re, ssm_a_im=v_ssm_a_im, ssm_log_dt=v_ssm_log_dt,
               ssm_b_re=v_ssm_b_re, ssm_b_im=v_ssm_b_im, ssm_c_re=v_ssm_c_re, ssm_c_im=v_ssm_c_im, ssm_d=v_ssm_d,
               ssm_glu_w=v_ssm_glu_w, ssm_glu_b=v_ssm_glu_b, gm_v_gain=v_gm_v_gain, gm_w_s=v_gm_w_s, gm_b_s=v_gm_b_s,
               gain_ssm_out=v_gain_ssm_out, gain_gm_out=v_gain_gm_out, mix_w_out=v_mix_w_out, norm_ffn2=v_norm_ffn2,
               ffn2_w_in=v_ffn2_w_in, ffn2_w_out=v_ffn2_w_out, norm_final=v_norm_final)

    nseq, seq, d = x.shape
    n = nseq * seq
    depth = norm_ffn1.shape[0]
    width = gain_ssm_out.shape[1]
    groups = ssm_a_re.shape[1]
    heads = gm_w_s.shape[1]
    core = lax.axis_index("c").astype(jnp.int32).reshape(1)

    is_col = dict(BIG)
    full = {name: [None] * depth for name, _ in BIG}

    def gather_comm(pairs):
        return _gather_comm([wts[nm][l].astype(BF16) for nm, l in pairs], [is_col[nm] for nm, _ in pairs])

    def store(pairs, arrays):
        for (nm, l), w in zip(pairs, arrays):
            full[nm][l] = w

    pairs = [("ffn1_w_in", 0)]
    store(pairs, _run_comm(gather_comm(pairs), "all_gather_first"))

    xs = x.reshape(n, d)
    saved = []
    for l in range(depth):
        pairs = [("ffn1_w_out", l)] + ([("mix_w_in", l), ("mix_w_out", l)] if l == 0 else [])
        x1, s_ffn1 = _ffn_fwd(xs, norm_ffn1[l], full["ffn1_w_in"][l], lambda l=l: full["ffn1_w_out"][l], "ffn1",
                              hosted=(functools.partial(gather_comm, pairs), functools.partial(store, pairs)))
        pairs = [("ffn2_w_out", l)]
        hm, u_ssm, zgm, got = _mix_in_fwd(x1, norm_mix[l], full["mix_w_in"][l], width, "mix_in",
                                          comm=gather_comm(pairs))
        store(pairs, got)
        bt_re = jnp.swapaxes(ssm_b_re[l], 1, 2)
        bt_im = jnp.swapaxes(ssm_b_im[l], 1, 2)
        disc_in = (ssm_a_re[l], ssm_a_im[l], ssm_log_dt[l].reshape(groups, 1), bt_re, bt_im)
        lr, li, bbr, bbi = _disc_fwd(*disc_in)
        ops = _ssm_operands(lr, li, bbr, bbi, ssm_c_re[l], ssm_c_im[l], ssm_d[l], ssm_glu_w[l], ssm_glu_b[l])
        pairs = [("ffn2_w_in", l)]
        y_ssm, hb, got = _ssm_fwd_pair(u_ssm, ops, nseq, "s5_fwd", comm=gather_comm(pairs))
        store(pairs, got)
        bias_tile = jnp.broadcast_to(gm_b_s[l].T[:, :, None], (GM_CHUNK, heads, GM_HEAD_DIM)).reshape(GM_CHUNK, width)
        y_gm = _gmlp_fwd(zgm, gm_v_gain[l], gm_w_s[l], bias_tile, "gmlp_fwd")
        if l + 1 < depth:
            pairs = [("mix_w_in", l + 1), ("mix_w_out", l + 1)]
            ycat, x2, got = _mix_out_fwd(y_ssm, y_gm, gain_ssm_out[l], gain_gm_out[l], full["mix_w_out"][l], x1,
                                         "mix_out_hosting", comm=gather_comm(pairs))
            store(pairs, got)
        else:
            ycat, x2 = _mix_out_fwd(y_ssm, y_gm, gain_ssm_out[l], gain_gm_out[l], full["mix_w_out"][l], x1, "mix_out")
        hosted = None
        if l + 1 < depth:
            pairs = [("ffn1_w_in", l + 1)]
            hosted = (functools.partial(gather_comm, pairs), functools.partial(store, pairs))
        x3, s_ffn2 = _ffn_fwd(x2, norm_ffn2[l], full["ffn2_w_in"][l], full["ffn2_w_out"][l], "ffn2", hosted=hosted)
        saved.append(dict(ffn1=s_ffn1, x1=x1, hm=hm, zgm=zgm, disc_in=disc_in, ops=ops, u_ssm=u_ssm, hb=hb, y_ssm=y_ssm,
                          bias_tile=bias_tile, y_gm=y_gm, ycat=ycat, ffn2=s_ffn2))
        xs = x3

    dx, g_norm_final, loss_part = _loss_head(xs, norm_final, loss_target.reshape(n, d))
    big = {name: [None] * depth for name, _ in BIG}
    small = {name: [None] * depth for name in SMALL if name != "norm_final"}
    gpb = GROUPS_PER_BLOCK
    s_blk = STATES_PER_BLOCK
    psum_of, reduced, grads = {}, {}, {}
    shared_early = ["ffn2_w_in", "ffn2_w_out", "mix_w_in", "mix_w_out"]

    def swap_comm(pairs):
        return _pair_exchange_comm([big[nm][l] for nm, l in pairs], [is_col[nm] for nm, _ in pairs])

    def take_swapped(pairs, others):
        for (nm, l), other in zip(pairs, others):
            psum_of[nm, l] = _pair_sum(big[nm][l], other, is_col[nm], core, f"grad_pair_sum_{nm}")

    def send_comm(pairs):
        return _chip_exchange_comm([psum_of[p] for p in pairs], [is_col[nm] for nm, _ in pairs])

    def take_sent(pairs, slots):
        for (nm, l), s in zip(pairs, slots):
            reduced[nm] = _chip_sum(s, core, l, depth, reduced.get(nm), f"grad_chip_sum_{nm}")

    def hosting(make, take, pairs):
        return functools.partial(make, pairs), functools.partial(take, pairs)

    for l in reversed(range(depth)):
        sv = saved[l]
        above = [(nm, l + 1) for nm in ("mix_w_in", "mix_w_out", "ffn1_w_in", "ffn1_w_out")] if l + 1 < depth else []
        dx, small["norm_ffn2"][l], big["ffn2_w_in"][l], big["ffn2_w_out"][l] = _ffn_bwd(
            dx, sv["ffn2"], norm_ffn2[l], full["ffn2_w_in"][l], full["ffn2_w_out"][l], "ffn2",
            hooks={"out_dx": hosting(swap_comm, take_swapped, above)} if above else None)
        mine = [("ffn2_w_in", l), ("ffn2_w_out", l)]
        dy_ssm, dy_gm, small["gain_ssm_out"][l], small["gain_gm_out"][l], got = _mix_out_bwd(
            dx, full["mix_w_out"][l], sv["y_ssm"], sv["y_gm"], gain_ssm_out[l], gain_gm_out[l], "mix_out_dx",
            comm=swap_comm(mine))
        take_swapped(mine, got)
        big["mix_w_out"][l] = _matmul(sv["ycat"], dx, "tn", tm=1024, tn=1024, tk=2048, name="mix_out_dw")
        dzgm, small["gm_w_s"][l], dbias_tile, small["gm_v_gain"][l] = _gmlp_bwd(
            sv["zgm"], dy_gm, gm_v_gain[l], gm_w_s[l], sv["bias_tile"], "gmlp_bwd")
        small["gm_b_s"][l] = dbias_tile.reshape(GM_CHUNK, heads, GM_HEAD_DIM).sum(-1).T
        (du_ssm, dlam, db_bd, dct_bd, dd, dw_bd, dbias), got = _ssm_bwd_pair(
            sv["u_ssm"], dy_ssm, sv["hb"], sv["ops"], nseq, "s5_bwd", comm=send_comm(mine + above))
        take_sent(mine + above, got)
        dlr = dlam[:, 0, :s_blk].reshape(groups, SSM_STATE)
        dli = dlam[:, 0, s_blk:].reshape(groups, SSM_STATE)
        dbbr = _block_diag_extract(db_bd[:, :, :s_blk], gpb)
        dbbi = _block_diag_extract(db_bd[:, :, s_blk:], gpb)
        da_re, da_im, dldt, dbt_re, dbt_im = _disc_bwd(*sv["disc_in"], dlr, dli, dbbr, dbbi)
        small["ssm_a_re"][l], small["ssm_a_im"][l], small["ssm_log_dt"][l] = da_re, da_im, dldt.reshape(groups)
        small["ssm_b_re"][l] = jnp.swapaxes(dbt_re, 1, 2)
        small["ssm_b_im"][l] = jnp.swapaxes(dbt_im, 1, 2)
        small["ssm_c_re"][l] = _block_diag_extract(dct_bd[:, :, :s_blk], gpb)
        small["ssm_c_im"][l] = -_block_diag_extract(dct_bd[:, :, s_blk:], gpb)
        small["ssm_d"][l] = dd.reshape(groups, SSM_CH)
        small["ssm_glu_w"][l] = jnp.concatenate(
            [_block_diag_extract(dw_bd[:, :, :LANES], gpb), _block_diag_extract(dw_bd[:, :, LANES:], gpb)], axis=-1)
        small["ssm_glu_b"][l] = jnp.concatenate(
            [dbias[:, 0, :LANES].reshape(groups, SSM_CH), dbias[:, 0, LANES:].reshape(groups, SSM_CH)], axis=-1)
        cols_mi = 3 * width
        dw_mi = _matmul(sv["hm"], du_ssm, "tn", tm=1024, tn=width, tk=2048, name="mix_in_dw_ssm", out_cols=cols_mi)
        big["mix_w_in"][l] = _matmul(sv["hm"], dzgm, "tn", tm=1024, tn=width, tk=2048, name="mix_in_dw_gm",
                                     out_cols=cols_mi, col_off=width, into=dw_mi)
        dx, small["norm_mix"][l] = _proj_in_bwd([(du_ssm, 0), (dzgm, width)], full["mix_w_in"][l], sv["x1"],
                                                norm_mix[l], dx, "mix_in_dx")
        hooks = None
        if l == 0:
            mix, w_out_0, w_in_0 = [("mix_w_in", 0), ("mix_w_out", 0)], [("ffn1_w_out", 0)], [("ffn1_w_in", 0)]

            def last_make():
                return _merge_comms(_merge_comms(send_comm(w_in_0), swap_comm(w_out_0)),
                                    _pair_share_comm([reduced[nm] for nm in shared_early]))

            def last_take(got):
                take_sent(w_in_0, got[:1])
                take_swapped(w_out_0, got[1:2])
                grads.update(zip(shared_early, got[2:]))

            hooks = {"out_dx": hosting(swap_comm, take_swapped, mix), "in_dw_g": hosting(send_comm, take_sent, mix),
                     "out_dw": hosting(swap_comm, take_swapped, w_in_0), "in_dx": (last_make, last_take)}

        def publish(which, dw, l=l):
            big[f"ffn1_w_{which}"][l] = dw

        dx, small["norm_ffn1"][l], big["ffn1_w_in"][l], big["ffn1_w_out"][l] = _ffn_bwd(
            dx, sv["ffn1"], norm_ffn1[l], full["ffn1_w_in"][l], full["ffn1_w_out"][l], "ffn1",
            hooks=hooks, publish=publish, late_out_dw=(l == 0))
    grad_x = dx.reshape(nseq, seq, d)

    pieces = [jnp.stack(small[name]).reshape(-1) for name in SMALL if name != "norm_final"]
    pieces += [g_norm_final.reshape(-1), loss_part.reshape(1)]
    sizes = [p.shape[0] for p in pieces]
    total = sum(sizes)
    rows = -(-total // (LANES * N_DEV * SUBLANES)) * N_DEV * SUBLANES
    pad = rows * LANES - total
    tail = [("ffn1_w_out", 0)]
    flat_g, got = _all_reduce_small(
        jnp.concatenate(pieces + [jnp.zeros((pad,), F32)]).reshape(rows, LANES), send_comm(tail))
    take_sent(tail, got)
    flat_g = flat_g.reshape(-1)
    loss = flat_g[total - 1]

    names = [name for name, _ in BIG if name not in shared_early]
    grads.update(zip(names, _run_comm(_pair_share_comm([reduced[nm] for nm in names]), "grad_pair_share")))
    offs = 0
    for name, size in zip(SMALL, sizes[:-1]):
        grads[name] = flat_g[offs:offs + size].reshape(wts[name].shape)
        offs += size

    delta, new_m, new_v = {}, {}, {}
    for name, _ in BIG:
        shape = wts[name].shape
        two_d = lambda a: a.reshape(shape[0] * shape[1], shape[2])
        go, dl, nm, nv = _adamw(two_d(wts[name]), two_d(grads[name]), two_d(mom[name]), two_d(var[name]),
                                f"adamw_{name}")
        grads[name] = go.reshape(shape)
        delta[name], new_m[name], new_v[name] = dl.reshape(shape), nm.reshape(shape), nv.reshape(shape)
    at_least_2d = lambda a: a.reshape(1, -1) if a.ndim == 1 else a
    dls, nms, nvs = _adamw_many(*[[at_least_2d(tree[k]) for k in SMALL] for tree in (wts, grads, mom, var)],
                                "adamw_small")
    for name, dl, nm, nv in zip(SMALL, dls, nms, nvs):
        shape = wts[name].shape
        delta[name], new_m[name], new_v[name] = dl.reshape(shape), nm.reshape(shape), nv.reshape(shape)

    return (loss, grad_x, *[grads[k] for k in WEIGHTS], *[delta[k] for k in WEIGHTS],
            *[new_m[k] for k in WEIGHTS], *[new_v[k] for k in WEIGHTS])
```

```python
import functools
import math

import jax
import jax.numpy as jnp
from jax import lax
from jax.experimental import pallas as pl
from jax.experimental.pallas import tpu as pltpu

F32 = jnp.float32
BF16 = jnp.bfloat16
MESH = pl.DeviceIdType.MESH

EPS = 1e-6
SSM_CH = 16
SSM_STATE = 64
GM_CHUNK = 128
GM_HEAD_DIM = 128
SUBLANES = 8
LANES = 128
GROUPS_PER_BLOCK = LANES // SSM_CH
STATES_PER_BLOCK = GROUPS_PER_BLOCK * SSM_STATE
SSM_TIME_CHUNK = 128
N_CHIPS = 4
N_DEV = 8

ADAM_LR = 0.001
ADAM_B1 = 0.9
ADAM_B2 = 0.999
ADAM_EPS = 1e-08
ADAM_WD = 0.01
ADAM_STEP = 10

VMEM_LIMIT = 56 * 1024 * 1024


def _tile(dim, pref, align):
    best = None
    t = align
    while t <= min(dim, pref):
        if dim % t == 0:
            best = t
        t += align
    return best if best is not None else dim


def _params(*sem):
    return pltpu.CompilerParams(dimension_semantics=sem, vmem_limit_bytes=VMEM_LIMIT)


def _gelu(x):
    c = math.sqrt(2.0 / math.pi)
    return 0.5 * x * (1.0 + jnp.tanh(c * (x + 0.044715 * x * x * x)))


def _gelu_and_grad(x):
    c = math.sqrt(2.0 / math.pi)
    t = jnp.tanh(c * (x + 0.044715 * x * x * x))
    g = 0.5 * x * (1.0 + t)
    dg = 0.5 * (1.0 + t) + 0.5 * x * (1.0 - t * t) * c * (1.0 + 3.0 * 0.044715 * x * x)
    return g, dg


def _sigmoid(x):
    return 0.5 * jnp.tanh(0.5 * x) + 0.5


def _matmul(a, b, mode, *, out_dtype=F32, scale=1.0, res=None, tm=512, tn=1024, tk=1024, name="mm",
            out_cols=None, col_off=0, into=None, comm=None):
    if mode == "nn":
        (m, k), (k2, n) = a.shape, b.shape
    elif mode == "nt":
        (m, k), (n, k2) = a.shape, b.shape
    else:
        (k, m), (k2, n) = a.shape, b.shape
    assert k == k2, (a.shape, b.shape, mode)
    tm = _tile(m, tm, 16 if mode != "tn" else LANES)
    tn = _tile(n, tn, LANES)
    tk = _tile(k, tk, LANES if mode != "tn" else 16)
    nk = k // tk
    grid = (m // tm, n // tn, nk)
    if mode == "nn":
        a_spec = pl.BlockSpec((tm, tk), lambda i, j, kk: (i, kk))
        b_spec = pl.BlockSpec((tk, tn), lambda i, j, kk: (kk, j))
        dims = (((1,), (0,)), ((), ()))
    elif mode == "nt":
        a_spec = pl.BlockSpec((tm, tk), lambda i, j, kk: (i, kk))
        b_spec = pl.BlockSpec((tn, tk), lambda i, j, kk: (j, kk))
        dims = (((1,), (1,)), ((), ()))
    else:
        a_spec = pl.BlockSpec((tk, tm), lambda i, j, kk: (kk, i))
        b_spec = pl.BlockSpec((tk, tn), lambda i, j, kk: (kk, j))
        dims = (((0,), (0,)), ((), ()))
    assert col_off % tn == 0
    off = col_off // tn
    r_spec = pl.BlockSpec((tm, tn), lambda i, j, kk: (i, j))
    o_spec = pl.BlockSpec((tm, tn), lambda i, j, kk: (i, j + off))
    has_res = res is not None
    has_into = into is not None

    def body(*refs):
        refs = list(refs)
        a_ref, b_ref = refs[:2]
        pos = 2
        r_ref = None
        if has_res:
            r_ref = refs[pos]
            pos += 1
        if has_into:
            pos += 1
        o_ref = refs[pos]
        acc_ref = refs[pos + 1] if nk > 1 else None
        part = lax.dot_general(a_ref[...].astype(BF16), b_ref[...].astype(BF16), dims,
                               preferred_element_type=F32)

        def finish(r):
            if scale != 1.0:
                r = r * scale
            if has_res:
                r = r + r_ref[...].astype(F32)
            o_ref[...] = r.astype(o_ref.dtype)

        if nk == 1:
            finish(part)
        else:
            kk = pl.program_id(2)

            @pl.when(kk == 0)
            def _():
                acc_ref[...] = part

            @pl.when(kk > 0)
            def _():
                acc_ref[...] += part

            @pl.when(kk == nk - 1)
            def _():
                finish(acc_ref[...])

    in_specs = [a_spec, b_spec]
    args = [a, b]
    if has_res:
        in_specs.append(r_spec)
        args.append(res)
    aliases = {}
    if has_into:
        in_specs.append(pl.BlockSpec(memory_space=pl.ANY))
        args.append(into)
        aliases = {len(args) - 1: 0}
    (out,), comm_outs = _call(
        body, name=name, grid=grid, in_specs=in_specs, out_specs=[o_spec],
        out_shape=[jax.ShapeDtypeStruct((m, n if out_cols is None else out_cols), out_dtype)],
        scratch_shapes=[pltpu.VMEM((tm, tn), F32)] if nk > 1 else [],
        aliases=aliases, semantics=("parallel", "parallel", "arbitrary"), args=args, comm=comm)
    return out if comm is None else (out, comm_outs)


class _Comm:
    def __init__(self, ins, outs, sems, start, finish, alias=None, middle=None):
        self.ins, self.outs, self.sems, self.start, self.finish = list(ins), list(outs), list(sems), start, finish
        self.alias = dict(alias or {})
        self.middle = middle


def _merge_comms(a, b):
    assert a.middle is None and b.middle is None
    cut = (len(a.ins), len(a.outs), len(a.sems))

    def both(which):
        def run(ins, outs, sems):
            getattr(a, which)(ins[:cut[0]], outs[:cut[1]], sems[:cut[2]])
            getattr(b, which)(ins[cut[0]:], outs[cut[1]:], sems[cut[2]:])
        return run

    alias = dict(a.alias)
    alias.update({cut[0] + ci: cut[1] + co for ci, co in b.alias.items()})
    return _Comm(a.ins + b.ins, a.outs + b.outs, a.sems + b.sems, both("start"), both("finish"), alias=alias)


def _call(body, *, name, grid, in_specs, out_specs, out_shape, args, scratch_shapes=(), semantics=(), aliases=None,
          comm=None):
    in_specs, out_specs, out_shape = list(in_specs), list(out_specs), list(out_shape)
    scratch_shapes = list(scratch_shapes)
    aliases = dict(aliases or {})
    if comm is None:
        outs = pl.pallas_call(
            body, name=name, grid=grid, in_specs=in_specs, out_specs=out_specs, out_shape=out_shape,
            scratch_shapes=scratch_shapes, input_output_aliases=aliases, compiler_params=_params(*semantics),
        )(*args)
        return list(outs), []
    n_in, n_out, n_scr = len(in_specs), len(out_specs), len(scratch_shapes)
    c_in, c_out = len(comm.ins), len(comm.outs)
    for ci, co in comm.alias.items():
        aliases[n_in + ci] = n_out + co

    def hosted(*refs):
        refs = list(refs)
        ins, cins = refs[:n_in], refs[n_in:n_in + c_in]
        p = n_in + c_in
        outs, couts = refs[p:p + n_out], refs[p + n_out:p + n_out + c_out]
        p += n_out + c_out
        scr, sems = refs[p:p + n_scr], refs[p + n_scr:]
        ids = [pl.program_id(a) for a in range(len(grid))]
        first = functools.reduce(jnp.logical_and, [i == 0 for i in ids])
        last = functools.reduce(jnp.logical_and, [i == g - 1 for i, g in zip(ids, grid)])

        total = math.prod(grid)
        late = comm.middle is not None and total >= 4

        @pl.when(first)
        def _():
            comm.start(cins, couts, sems)

        if late:
            flat = functools.reduce(lambda acc, ig: acc * ig[1] + ig[0], zip(ids, grid), 0)

            @pl.when(flat == (3 * total) // 4)
            def _():
                comm.middle(cins, couts, sems)

        body(*ins, *outs, *scr)

        @pl.when(last)
        def _():
            if comm.middle is not None and not late:
                comm.middle(cins, couts, sems)
            comm.finish(cins, couts, sems)

    any_spec = pl.BlockSpec(memory_space=pl.ANY)
    outs = pl.pallas_call(
        hosted, name=name, grid=grid, in_specs=in_specs + [any_spec] * c_in, out_specs=out_specs + [any_spec] * c_out,
        out_shape=out_shape + comm.outs, scratch_shapes=scratch_shapes + comm.sems, input_output_aliases=aliases,
        compiler_params=_params(*(["arbitrary"] * len(grid))),
    )(*args, *comm.ins)
    return list(outs[:n_out]), list(outs[n_out:])


def _run_comm(comm, name):
    c_in, c_out = len(comm.ins), len(comm.outs)

    def body(*refs):
        refs = list(refs)
        cins, couts, sems = refs[:c_in], refs[c_in:c_in + c_out], refs[c_in + c_out:]
        comm.start(cins, couts, sems)
        if comm.middle is not None:
            comm.middle(cins, couts, sems)
        comm.finish(cins, couts, sems)

    any_spec = pl.BlockSpec(memory_space=pl.ANY)
    return list(pl.pallas_call(
        body, name=name, in_specs=[any_spec] * c_in, out_specs=[any_spec] * c_out, out_shape=comm.outs,
        scratch_shapes=comm.sems, input_output_aliases=comm.alias,
    )(*comm.ins))


def _loss_head(x, gain, target):
    n, d = x.shape
    tm = _tile(n, 512, 8)
    steps = n // tm

    def body(x_ref, g_ref, t_ref, dx_ref, dg_ref, loss_ref, acc_ref, lacc_ref):
        i = pl.program_id(0)
        xv = x_ref[...]
        g = g_ref[...]
        r = lax.rsqrt(jnp.mean(xv * xv, axis=-1, keepdims=True) + EPS)
        xh = xv * r
        err = xh * g - t_ref[...]
        dy = err * (1.0 / d)
        dyg = dy * g
        mean = jnp.mean(dyg * xh, axis=-1, keepdims=True)
        dx_ref[...] = r * (dyg - xh * mean)
        part = jnp.sum((dy * xh).reshape(tm // SUBLANES, SUBLANES, d), axis=0)
        lpart = jnp.sum((err * err).reshape(tm // SUBLANES, SUBLANES, d), axis=0)

        @pl.when(i == 0)
        def _():
            acc_ref[...] = part
            lacc_ref[...] = lpart

        @pl.when(i > 0)
        def _():
            acc_ref[...] += part
            lacc_ref[...] += lpart

        @pl.when(i == steps - 1)
        def _():
            dg_ref[...] = jnp.sum(acc_ref[...], axis=0, keepdims=True)
            tot = jnp.sum(jnp.sum(lacc_ref[...], axis=0, keepdims=True), axis=1, keepdims=True)
            loss_ref[...] = jnp.broadcast_to(tot * (0.5 / d), loss_ref.shape)

    row = pl.BlockSpec((tm, d), lambda i: (i, 0))
    vec = pl.BlockSpec((1, d), lambda i: (0, 0))
    dx, dg, loss = pl.pallas_call(
        body, name="loss_head", grid=(steps,),
        in_specs=[row, vec, row],
        out_specs=[row, vec, pl.BlockSpec((1, LANES), lambda i: (0, 0))],
        out_shape=[jax.ShapeDtypeStruct((n, d), F32), jax.ShapeDtypeStruct((1, d), F32),
                   jax.ShapeDtypeStruct((1, LANES), F32)],
        scratch_shapes=[pltpu.VMEM((SUBLANES, d), F32), pltpu.VMEM((SUBLANES, d), F32)],
        compiler_params=_params("arbitrary"),
    )(x, gain.reshape(1, d), target)
    return dx, dg.reshape(d), loss[0, 0]


def _rms_rows(xv):
    return lax.rsqrt(jnp.mean(xv * xv, axis=-1, keepdims=True) + EPS)


def _ffn_in_fwd(x, gain, w_in, name, comm=None):
    n, d = x.shape
    f = w_in.shape[1] // 2
    tm = _tile(n, 256, 16)
    tn = _tile(f, 4096, LANES)
    nj = f // tn

    def body(x_ref, gain_ref, wg_ref, wu_ref, h_ref, t_ref, q_ref, a_ref):
        @pl.when(pl.program_id(1) == 0)
        def _():
            xv = x_ref[...]
            h_ref[...] = (xv * _rms_rows(xv) * gain_ref[...]).astype(h_ref.dtype)

        h = h_ref[...]
        g = jnp.dot(h, wg_ref[...], preferred_element_type=F32)
        u = jnp.dot(h, wu_ref[...], preferred_element_type=F32)
        s = _sigmoid(g)
        t = g * s
        t_ref[...] = t.astype(t_ref.dtype)
        q_ref[...] = (u * (s + t * (1.0 - s))).astype(q_ref.dtype)
        a_ref[...] = (t * u).astype(a_ref.dtype)

    row = pl.BlockSpec((tm, d), lambda i, j: (i, 0))
    tile = pl.BlockSpec((tm, tn), lambda i, j: (i, j))
    act = jax.ShapeDtypeStruct((n, f), BF16)
    outs, comm_outs = _call(
        body, name=name, grid=(n // tm, nj),
        in_specs=[row, pl.BlockSpec((1, d), lambda i, j: (0, 0)),
                  pl.BlockSpec((d, tn), lambda i, j: (0, j)), pl.BlockSpec((d, tn), lambda i, j: (0, j + nj))],
        out_specs=[row, tile, tile, tile],
        out_shape=[jax.ShapeDtypeStruct((n, d), BF16), act, act, act],
        semantics=("parallel", "arbitrary"), args=(x, gain.reshape(1, d), w_in, w_in), comm=comm)
    return outs if comm is None else (outs, comm_outs)


def _ffn_out_bwd(dout, w_out, t, q, name, comm=None):
    n, d = dout.shape
    f = w_out.shape[0]
    tm = _tile(n, 256, 16)
    tn = _tile(f, 4096, LANES)

    def body(d_ref, w_ref, t_ref, q_ref, dg_ref, du_ref):
        da = 0.5 * lax.dot_general(d_ref[...].astype(BF16), w_ref[...], (((1,), (1,)), ((), ())),
                                   preferred_element_type=F32)
        dg_ref[...] = (da * q_ref[...].astype(F32)).astype(dg_ref.dtype)
        du_ref[...] = (da * t_ref[...].astype(F32)).astype(du_ref.dtype)

    tile = pl.BlockSpec((tm, tn), lambda i, j: (i, j))
    act = jax.ShapeDtypeStruct((n, f), BF16)
    outs, comm_outs = _call(
        body, name=name, grid=(n // tm, f // tn),
        in_specs=[pl.BlockSpec((tm, d), lambda i, j: (i, 0)), pl.BlockSpec((tn, d), lambda i, j: (j, 0)), tile, tile],
        out_specs=[tile, tile], out_shape=[act, act],
        semantics=("parallel", "parallel"), args=(dout, w_out, t, q), comm=comm)
    return outs if comm is None else (outs, comm_outs)


def _proj_in_bwd(parts, w, x, gain, dres, name, comm=None):
    n, d = x.shape
    tm = _tile(n, 256, 8)
    steps = n // tm
    np_ = len(parts)
    offs = [off for _, off in parts]
    widths = [a.shape[1] for a, _ in parts]

    def body(*refs):
        a_refs = refs[:np_]
        w_ref, x_ref, g_ref, dr_ref, dx_ref, dg_ref, acc_ref = refs[np_:]
        i = pl.program_id(0)
        dh = None
        for a_ref, off, kp in zip(a_refs, offs, widths):
            part = lax.dot_general(a_ref[...].astype(BF16), w_ref[:, off:off + kp], (((1,), (1,)), ((), ())),
                                   preferred_element_type=F32)
            dh = part if dh is None else dh + part
        xv = x_ref[...]
        r = _rms_rows(xv)
        xh = xv * r
        dyg = dh * g_ref[...]
        mean = jnp.mean(dyg * xh, axis=-1, keepdims=True)
        dx_ref[...] = dr_ref[...] + r * (dyg - xh * mean)
        part = jnp.sum((dh * xh).reshape(tm // SUBLANES, SUBLANES, d), axis=0)

        @pl.when(i == 0)
        def _():
            acc_ref[...] = part

        @pl.when(i > 0)
        def _():
            acc_ref[...] += part

        @pl.when(i == steps - 1)
        def _():
            dg_ref[...] = jnp.sum(acc_ref[...], axis=0, keepdims=True)

    row = pl.BlockSpec((tm, d), lambda i: (i, 0))
    vec = pl.BlockSpec((1, d), lambda i: (0, 0))
    (dx, dg), comm_outs = _call(
        body, name=name, grid=(steps,),
        in_specs=[pl.BlockSpec((tm, kp), lambda i: (i, 0)) for kp in widths]
        + [pl.BlockSpec(w.shape, lambda i: (0, 0)), row, vec, row],
        out_specs=[row, vec],
        out_shape=[jax.ShapeDtypeStruct((n, d), F32), jax.ShapeDtypeStruct((1, d), F32)],
        scratch_shapes=[pltpu.VMEM((SUBLANES, d), F32)],
        semantics=("arbitrary",), args=(*[a for a, _ in parts], w, x, gain.reshape(1, d), dres), comm=comm)
    return (dx, dg.reshape(d)) if comm is None else (dx, dg.reshape(d), comm_outs)


def _mix_in_fwd(x, gain, w, width, name, comm=None):
    n, d = x.shape
    cols = w.shape[1]
    tm = _tile(n, 512, 16)

    def body(x_ref, gain_ref, w_ref, h_ref, u_ref, z_ref):
        xv = x_ref[...]
        h = (xv * _rms_rows(xv) * gain_ref[...]).astype(h_ref.dtype)
        h_ref[...] = h
        z = jnp.dot(h, w_ref[...], preferred_element_type=F32)
        u_ref[...] = z[:, 0:width]
        z_ref[...] = z[:, width:cols]

    row = pl.BlockSpec((tm, d), lambda i: (i, 0))
    outs, comm_outs = _call(
        body, name=name, grid=(n // tm,),
        in_specs=[row, pl.BlockSpec((1, d), lambda i: (0, 0)), pl.BlockSpec((d, cols), lambda i: (0, 0))],
        out_specs=[row, pl.BlockSpec((tm, width), lambda i: (i, 0)), pl.BlockSpec((tm, cols - width), lambda i: (i, 0))],
        out_shape=[jax.ShapeDtypeStruct((n, d), BF16), jax.ShapeDtypeStruct((n, width), F32),
                   jax.ShapeDtypeStruct((n, cols - width), F32)],
        semantics=("parallel",), args=(x, gain.reshape(1, d), w), comm=comm)
    return outs if comm is None else (*outs, comm_outs)


def _tril_mask():
    t = lax.broadcasted_iota(jnp.int32, (GM_CHUNK, GM_CHUNK), 0)
    s = lax.broadcasted_iota(jnp.int32, (GM_CHUNK, GM_CHUNK), 1)
    return s <= t


def _gmlp_fwd(zgm, v_gain, w_s, bias_tile, name):
    n, w2 = zgm.shape
    w = w2 // 2
    heads = w // GM_HEAD_DIM
    tm = _tile(n, 512, GM_CHUNK)
    nq = tm // GM_CHUNK

    def body(u_ref, v_ref, gain_ref, w_ref, b_ref, o_ref):
        mask = _tril_mask()
        ug = _gelu(u_ref[...])
        vg = _gelu(v_ref[...])
        for h in range(heads):
            cols = slice(h * GM_HEAD_DIM, (h + 1) * GM_HEAD_DIM)
            vh = vg[:, cols]
            r = lax.rsqrt(jnp.mean(vh * vh, axis=-1, keepdims=True) + EPS)
            vn = (vh * r * gain_ref[:, cols]).astype(BF16)
            wm = jnp.where(mask, w_ref[h], 0.0).astype(BF16)
            for q in range(nq):
                rows = slice(q * GM_CHUNK, (q + 1) * GM_CHUNK)
                s = jnp.dot(wm, vn[rows], preferred_element_type=F32) + b_ref[:, cols]
                o_ref[rows, cols] = ug[rows, cols] * s

    return pl.pallas_call(
        body, name=name, grid=(n // tm,),
        in_specs=[pl.BlockSpec((tm, w), lambda i: (i, 0)), pl.BlockSpec((tm, w), lambda i: (i, 1)),
                  pl.BlockSpec((1, w), lambda i: (0, 0)),
                  pl.BlockSpec((heads, GM_CHUNK, GM_CHUNK), lambda i: (0, 0, 0)),
                  pl.BlockSpec((GM_CHUNK, w), lambda i: (0, 0))],
        out_specs=pl.BlockSpec((tm, w), lambda i: (i, 0)),
        out_shape=jax.ShapeDtypeStruct((n, w), F32),
        compiler_params=_params("parallel"),
    )(zgm, zgm, v_gain.reshape(1, w), w_s, bias_tile)


def _gmlp_bwd(zgm, dy, v_gain, w_s, bias_tile, name):
    n, w2 = zgm.shape
    w = w2 // 2
    heads = w // GM_HEAD_DIM
    tm = _tile(n, 512, GM_CHUNK)
    nq = tm // GM_CHUNK
    steps = n // tm

    def body(z_ref, dy_ref, gain_ref, w_ref, b_ref, dz_ref, dw_ref, db_ref, dgain_ref):
        i = pl.program_id(0)
        mask = _tril_mask()

        @pl.when(i == 0)
        def _():
            dw_ref[...] = jnp.zeros_like(dw_ref)
            db_ref[...] = jnp.zeros_like(db_ref)
            dgain_ref[...] = jnp.zeros_like(dgain_ref)

        ug, dug_du = _gelu_and_grad(z_ref[:, 0:w])
        vg, dvg_dv = _gelu_and_grad(z_ref[:, w:w2])
        dyv = dy_ref[...]
        for h in range(heads):
            cols = slice(h * GM_HEAD_DIM, (h + 1) * GM_HEAD_DIM)
            vh = vg[:, cols]
            r = lax.rsqrt(jnp.mean(vh * vh, axis=-1, keepdims=True) + EPS)
            vhat = vh * r
            gain = gain_ref[:, cols]
            vn = (vhat * gain).astype(BF16)
            wm = jnp.where(mask, w_ref[h], 0.0).astype(BF16)
            dvn_parts = []
            for q in range(nq):
                rows = slice(q * GM_CHUNK, (q + 1) * GM_CHUNK)
                s = jnp.dot(wm, vn[rows], preferred_element_type=F32) + b_ref[:, cols]
                dyq = dyv[rows, cols]
                dz_ref[rows, cols] = (dyq * s * dug_du[rows, cols]).astype(dz_ref.dtype)
                ds = dyq * ug[rows, cols]
                db_ref[:, cols] += ds
                dsb = ds.astype(BF16)
                dw_ref[h] += lax.dot_general(dsb, vn[rows], (((1,), (1,)), ((), ())), preferred_element_type=F32)
                dvn_parts.append(lax.dot_general(wm, dsb, (((0,), (0,)), ((), ())), preferred_element_type=F32))
            dvn = jnp.concatenate(dvn_parts, axis=0) if nq > 1 else dvn_parts[0]
            dgain_ref[:, cols] += jnp.sum(dvn * vhat, axis=0, keepdims=True)
            dvhat = dvn * gain
            mean = jnp.mean(dvhat * vhat, axis=-1, keepdims=True)
            dz_ref[:, w + h * GM_HEAD_DIM:w + (h + 1) * GM_HEAD_DIM] = (
                r * (dvhat - vhat * mean) * dvg_dv[:, cols]).astype(dz_ref.dtype)

        @pl.when(i == steps - 1)
        def _():
            for h in range(heads):
                dw_ref[h] = jnp.where(mask, dw_ref[h], 0.0)

    dz, dw, db, dgain = pl.pallas_call(
        body, name=name, grid=(steps,),
        in_specs=[pl.BlockSpec((tm, w2), lambda i: (i, 0)), pl.BlockSpec((tm, w), lambda i: (i, 0)),
                  pl.BlockSpec((1, w), lambda i: (0, 0)),
                  pl.BlockSpec((heads, GM_CHUNK, GM_CHUNK), lambda i: (0, 0, 0)),
                  pl.BlockSpec((GM_CHUNK, w), lambda i: (0, 0))],
        out_specs=[pl.BlockSpec((tm, w2), lambda i: (i, 0)),
                   pl.BlockSpec((heads, GM_CHUNK, GM_CHUNK), lambda i: (0, 0, 0)),
                   pl.BlockSpec((GM_CHUNK, w), lambda i: (0, 0)),
                   pl.BlockSpec((1, w), lambda i: (0, 0))],
        out_shape=[jax.ShapeDtypeStruct((n, w2), BF16), jax.ShapeDtypeStruct((heads, GM_CHUNK, GM_CHUNK), F32),
                   jax.ShapeDtypeStruct((GM_CHUNK, w), F32), jax.ShapeDtypeStruct((1, w), F32)],
        compiler_params=_params("arbitrary"),
    )(zgm, dy, v_gain.reshape(1, w), w_s, bias_tile)
    return dz, dw, db, dgain.reshape(w)


def _mix_out_fwd(y_ssm, y_gm, g1, g2, w_out, x, name, comm=None):
    n, w = y_ssm.shape
    d = w_out.shape[1]
    tm = _tile(n, 512, 16)

    def body(a_ref, b_ref, g1_ref, g2_ref, w_ref, x_ref, ycat_ref, o_ref):
        for src, g_ref, lo in ((a_ref, g1_ref, 0), (b_ref, g2_ref, w)):
            v = src[...]
            ycat_ref[:, lo:lo + w] = (v * _rms_rows(v) * g_ref[...]).astype(ycat_ref.dtype)
        o_ref[...] = x_ref[...] + jnp.dot(ycat_ref[...], w_ref[...], preferred_element_type=F32)

    row = pl.BlockSpec((tm, w), lambda i: (i, 0))
    vec = pl.BlockSpec((1, w), lambda i: (0, 0))
    outs, comm_outs = _call(
        body, name=name, grid=(n // tm,),
        in_specs=[row, row, vec, vec, pl.BlockSpec((2 * w, d), lambda i: (0, 0)), pl.BlockSpec((tm, d), lambda i: (i, 0))],
        out_specs=[pl.BlockSpec((tm, 2 * w), lambda i: (i, 0)), pl.BlockSpec((tm, d), lambda i: (i, 0))],
        out_shape=[jax.ShapeDtypeStruct((n, 2 * w), BF16), jax.ShapeDtypeStruct((n, d), F32)],
        semantics=("parallel",), args=(y_ssm, y_gm, g1.reshape(1, w), g2.reshape(1, w), w_out, x), comm=comm)
    return outs if comm is None else (*outs, comm_outs)


def _mix_out_bwd(dx, w_out, y_ssm, y_gm, g1, g2, name, comm=None):
    n, w = y_ssm.shape
    d = w_out.shape[1]
    tm = _tile(n, 512, 8)
    steps = n // tm

    def body(dx_ref, w_ref, a_ref, b_ref, g1_ref, g2_ref, da_ref, db_ref, dg1_ref, dg2_ref):
        i = pl.program_id(0)

        @pl.when(i == 0)
        def _():
            dg1_ref[...] = jnp.zeros_like(dg1_ref)
            dg2_ref[...] = jnp.zeros_like(dg2_ref)

        dycat = lax.dot_general(dx_ref[...].astype(BF16), w_ref[...], (((1,), (1,)), ((), ())),
                                preferred_element_type=F32)
        for src, g_ref, lo, dst, dg_ref in ((a_ref, g1_ref, 0, da_ref, dg1_ref), (b_ref, g2_ref, w, db_ref, dg2_ref)):
            v = src[...]
            dh = dycat[:, lo:lo + w]
            r = _rms_rows(v)
            vh = v * r
            dyg = dh * g_ref[...]
            mean = jnp.mean(dyg * vh, axis=-1, keepdims=True)
            dst[...] = r * (dyg - vh * mean)
            dg_ref[...] += jnp.sum(dh * vh, axis=0, keepdims=True)

    row = pl.BlockSpec((tm, w), lambda i: (i, 0))
    vec = pl.BlockSpec((1, w), lambda i: (0, 0))
    (da, db, dg1, dg2), comm_outs = _call(
        body, name=name, grid=(steps,),
        in_specs=[pl.BlockSpec((tm, d), lambda i: (i, 0)), pl.BlockSpec((2 * w, d), lambda i: (0, 0)), row, row, vec, vec],
        out_specs=[row, row, vec, vec],
        out_shape=[jax.ShapeDtypeStruct((n, w), F32), jax.ShapeDtypeStruct((n, w), F32),
                   jax.ShapeDtypeStruct((1, w), F32), jax.ShapeDtypeStruct((1, w), F32)],
        semantics=("arbitrary",), args=(dx, w_out, y_ssm, y_gm, g1.reshape(1, w), g2.reshape(1, w)), comm=comm)
    res = (da, db, dg1.reshape(w), dg2.reshape(w))
    return res if comm is None else (*res, comm_outs)


def _discretise(a_re, a_im, log_dt, bt_re, bt_im):
    dt = jnp.exp(log_dt)
    e = jnp.exp(a_re * dt)
    ang = a_im * dt
    lr = e * jnp.cos(ang)
    li = e * jnp.sin(ang)
    den = a_re * a_re + a_im * a_im
    cr = ((lr - 1.0) * a_re + li * a_im) / den
    ci = (li * a_re - (lr - 1.0) * a_im) / den
    cr3 = cr[:, None, :]
    ci3 = ci[:, None, :]
    return lr, li, cr3 * bt_re - ci3 * bt_im, cr3 * bt_im + ci3 * bt_re


def _disc_fwd(a_re, a_im, log_dt, bt_re, bt_im):
    g, p = a_re.shape
    c = bt_re.shape[1]

    def body(are_ref, aim_ref, ldt_ref, bre_ref, bim_ref, lr_ref, li_ref, bbr_ref, bbi_ref):
        lr, li, bbr, bbi = _discretise(are_ref[...], aim_ref[...], ldt_ref[...], bre_ref[...], bim_ref[...])
        lr_ref[...] = lr
        li_ref[...] = li
        bbr_ref[...] = bbr
        bbi_ref[...] = bbi

    return pl.pallas_call(
        body, name="s5_discretise",
        out_shape=[jax.ShapeDtypeStruct((g, p), F32), jax.ShapeDtypeStruct((g, p), F32),
                   jax.ShapeDtypeStruct((g, c, p), F32), jax.ShapeDtypeStruct((g, c, p), F32)],
    )(a_re, a_im, log_dt, bt_re, bt_im)


def _disc_bwd(a_re, a_im, log_dt, bt_re, bt_im, dlr, dli, dbbr, dbbi):
    g, p = a_re.shape
    c = bt_re.shape[1]

    def body(are_ref, aim_ref, ldt_ref, bre_ref, bim_ref, dlr_ref, dli_ref, dbbr_ref, dbbi_ref,
             dare_ref, daim_ref, dldt_ref, dbre_ref, dbim_ref):
        _, vjp = jax.vjp(_discretise, are_ref[...], aim_ref[...], ldt_ref[...], bre_ref[...], bim_ref[...])
        dare, daim, dldt, dbre, dbim = vjp((dlr_ref[...], dli_ref[...], dbbr_ref[...], dbbi_ref[...]))
        dare_ref[...] = dare
        daim_ref[...] = daim
        dldt_ref[...] = dldt
        dbre_ref[...] = dbre
        dbim_ref[...] = dbim

    return pl.pallas_call(
        body, name="s5_discretise_bwd",
        out_shape=[jax.ShapeDtypeStruct((g, p), F32), jax.ShapeDtypeStruct((g, p), F32),
                   jax.ShapeDtypeStruct((g, 1), F32),
                   jax.ShapeDtypeStruct((g, c, p), F32), jax.ShapeDtypeStruct((g, c, p), F32)],
    )(a_re, a_im, log_dt, bt_re, bt_im, dlr, dli, dbbr, dbbi)


def _block_diag(w, nb):
    g, a, b = w.shape
    gpb = g // nb
    eye = jnp.eye(gpb, dtype=w.dtype)
    w4 = w.reshape(nb, gpb, a, b)
    return jnp.einsum("ngab,gh->ngahb", w4, eye).reshape(nb, gpb * a, gpb * b)


def _block_diag_extract(m, gpb):
    nb, ga, gb = m.shape
    a, b = ga // gpb, gb // gpb
    m5 = m.reshape(nb, gpb, a, gpb, b)
    idx = jnp.arange(gpb)
    return m5[:, idx, :, idx, :].transpose(1, 0, 2, 3).reshape(nb * gpb, a, b)


def _ssm_operands(lr, li, bbr, bbi, c_re, c_im, d_skip, glu_w, glu_b):
    g = lr.shape[0]
    nb = g // GROUPS_PER_BLOCK
    s = STATES_PER_BLOCK
    lam = jnp.concatenate([lr.reshape(nb, 1, s), li.reshape(nb, 1, s)], axis=-1)
    b_bd = jnp.concatenate([_block_diag(bbr, nb), _block_diag(bbi, nb)], axis=-1)
    ct_re = jnp.swapaxes(c_re, 1, 2)
    ct_im = jnp.swapaxes(c_im, 1, 2)
    c_bd = jnp.concatenate([_block_diag(ct_re, nb), -_block_diag(ct_im, nb)], axis=1)
    dsk = d_skip.reshape(nb, 1, LANES)
    w_bd = jnp.concatenate([_block_diag(glu_w[:, :, :SSM_CH], nb), _block_diag(glu_w[:, :, SSM_CH:], nb)], axis=-1)
    bias = jnp.concatenate([glu_b[:, :SSM_CH].reshape(nb, 1, LANES), glu_b[:, SSM_CH:].reshape(nb, 1, LANES)], axis=-1)
    return lam, b_bd.astype(BF16), c_bd.astype(BF16), dsk, w_bd.astype(BF16), bias


def _roll_rows(v, shift):
    return v if shift % SUBLANES == 0 else pltpu.roll(v, shift % SUBLANES, 0)


def _scan_chunk_rows(seq, nseq):
    return _tile(seq, max(8 * SSM_TIME_CHUNK // nseq, 8), max(SUBLANES // nseq, 1) * 8)


def _stage_lams(lam_ref, e, nseq):
    s = STATES_PER_BLOCK
    lr = jnp.broadcast_to(lam_ref[e, :, 0:s], (SUBLANES, s))
    li = jnp.broadcast_to(lam_ref[e, :, s:2 * s], (SUBLANES, s))
    if nseq == SUBLANES:
        return [(lr, li)]
    row = lax.broadcasted_iota(jnp.int32, (SUBLANES, s), 0)
    out = []
    for j in range(SUBLANES // nseq):
        mine = jnp.logical_and(row >= j * nseq, row < (j + 1) * nseq)
        out.append((jnp.where(mine, lr, 0.0), jnp.where(mine, li, 0.0)))
    return out


def _scan_with(nblk, step, carry, between):
    runs = len(between)
    per = nblk // runs
    for i in range(runs):
        hi = nblk if i == runs - 1 else (i + 1) * per
        carry = lax.fori_loop(i * per, hi, step, carry, unroll=True)
        between[i]()
    return carry


def _ssm_fwd_pair(u, ops, nseq, name, comm=None):
    lam, b_bd, c_bd, dsk, w_bd, bias = ops
    rows_total, w = u.shape
    seq = rows_total // nseq
    nb = w // LANES
    s = STATES_PER_BLOCK
    tc = _scan_chunk_rows(seq, nseq)
    nk = seq // tc
    rows = tc * nseq
    nblk = rows // SUBLANES
    stages = SUBLANES // nseq
    two = 2 * LANES
    ncol = 4

    def body(u_ref, lam_ref, b_ref, c_ref, d_ref, w_ref, bias_ref, y_ref, hb_ref, buf_a, buf_b, st, rbuf_a, rbuf_b):
        k = pl.program_id(1)

        @pl.when(k == 0)
        def _():
            st[...] = jnp.zeros_like(st)

        hb_ref[...] = st[...]
        rbufs = (rbuf_a, rbuf_b)
        for q in range(nseq):
            for e in range(2):
                rbufs[e][pl.ds(q, tc, stride=nseq), :] = u_ref[q, :, e * LANES:(e + 1) * LANES]
        row = lax.broadcasted_iota(jnp.int32, (SUBLANES, s), 0)
        bufs = (buf_a, buf_b)

        def u_of(e):
            return rbufs[e][...]

        def project_in(e, j):
            cols = slice(j * (2 * s // ncol), (j + 1) * (2 * s // ncol))
            bufs[e][:, cols] = jnp.dot(u_of(e).astype(BF16), b_ref[e, :, cols], preferred_element_type=F32)

        def scan(e, between):
            buf = bufs[e]
            lams = _stage_lams(lam_ref, e, nseq)

            def step(i, carry):
                pr, pi = carry
                r0 = pl.multiple_of(i * SUBLANES, SUBLANES)
                outr = buf[pl.ds(r0, SUBLANES), 0:s]
                outi = buf[pl.ds(r0, SUBLANES), s:2 * s]
                for lr, li in lams:
                    rr = _roll_rows(pr, nseq)
                    ri = _roll_rows(pi, nseq)
                    outr = outr + (lr * rr - li * ri)
                    outi = outi + (lr * ri + li * rr)
                    pr, pi = outr, outi
                buf[pl.ds(r0, SUBLANES), 0:s] = outr
                buf[pl.ds(r0, SUBLANES), s:2 * s] = outi
                return outr, outi

            lo = e * 2 * s
            hr, hi = _scan_with(nblk, step, (st[:, lo:lo + s], st[:, lo + s:lo + 2 * s]), between)
            st[:, lo:lo + s] = hr
            st[:, lo + s:lo + 2 * s] = hi

        part = {}

        def project_out(e, j):
            ks = slice(j * (2 * s // ncol), (j + 1) * (2 * s // ncol))
            p = jnp.dot(bufs[e][:, ks].astype(BF16), c_ref[e, ks, :], preferred_element_type=F32)
            part[e] = p if j == 0 else part[e] + p

        def finish(e):
            y = part[e] + d_ref[e] * u_of(e)
            z = jnp.dot(_gelu(y).astype(BF16), w_ref[e], preferred_element_type=F32) + bias_ref[e]
            part[e] = z[:, 0:LANES] * _sigmoid(z[:, LANES:two])

        nothing = lambda: None
        for j in range(ncol):
            project_in(0, j)
        scan(0, [functools.partial(project_in, 1, j) for j in range(ncol)])
        scan(1, [functools.partial(project_out, 0, j) for j in range(ncol)] + [functools.partial(finish, 0), nothing,
                                                                                nothing, nothing])
        for j in range(ncol):
            project_out(1, j)
        finish(1)
        for e in range(2):
            rbufs[e][...] = part[e]
            for q in range(nseq):
                y_ref[q, :, e * LANES:(e + 1) * LANES] = rbufs[e][pl.ds(q, tc, stride=nseq), :]

    blk = lambda shape: pl.BlockSpec(shape, lambda b, k: (b, 0, 0))
    tok = pl.BlockSpec((nseq, tc, two), lambda b, k: (0, k, b))
    (y, hb), comm_outs = _call(
        body, name=name, grid=(nb // 2, nk),
        in_specs=[tok, blk((2, 1, 2 * s)), blk((2, LANES, 2 * s)), blk((2, 2 * s, LANES)),
                  blk((2, 1, LANES)), blk((2, LANES, two)), blk((2, 1, two))],
        out_specs=[tok, pl.BlockSpec((SUBLANES, 4 * s), lambda b, k: (k, b))],
        out_shape=[jax.ShapeDtypeStruct((nseq, seq, w), F32),
                   jax.ShapeDtypeStruct((nk * SUBLANES, nb * 2 * s), F32)],
        scratch_shapes=[pltpu.VMEM((rows, 2 * s), F32), pltpu.VMEM((rows, 2 * s), F32),
                        pltpu.VMEM((SUBLANES, 4 * s), F32), pltpu.VMEM((rows, LANES), F32),
                        pltpu.VMEM((rows, LANES), F32)],
        semantics=("parallel", "arbitrary"),
        args=(u.reshape(nseq, seq, w), lam, b_bd, c_bd, dsk, w_bd, bias), comm=comm)
    y = y.reshape(nseq * seq, w)
    return (y, hb) if comm is None else (y, hb, comm_outs)


def _ssm_bwd_pair(u, dout, hb, ops, nseq, name, comm=None):
    lam, b_bd, c_bd, dsk, w_bd, bias = ops
    rows_total, w = u.shape
    seq = rows_total // nseq
    nb = w // LANES
    s = STATES_PER_BLOCK
    tc = _scan_chunk_rows(seq, nseq)
    nk = seq // tc
    rows = tc * nseq
    nblk = rows // SUBLANES
    stages = SUBLANES // nseq
    two = 2 * LANES
    ncol = 4
    cw = 2 * s // ncol
    tn_dims = (((0,), (0,)), ((), ()))
    nt_dims = (((1,), (1,)), ((), ()))

    def body(u_ref, dy_ref, hb_ref, lam_ref, b_ref, c_ref, d_ref, w_ref, bias_ref,
             du_ref, dlam_ref, db_ref, dct_ref, dd_ref, dw_ref, dbias_ref,
             hbuf_a, hbuf_b, gbuf_a, gbuf_b, gst, lacc, ru_a, ru_b, rd_a, rd_b):
        k = pl.program_id(1)

        @pl.when(k == 0)
        def _():
            gst[...] = jnp.zeros_like(gst)
            lacc[...] = jnp.zeros_like(lacc)
            db_ref[...] = jnp.zeros_like(db_ref)
            dct_ref[...] = jnp.zeros_like(dct_ref)
            dd_ref[...] = jnp.zeros_like(dd_ref)
            dw_ref[...] = jnp.zeros_like(dw_ref)
            dbias_ref[...] = jnp.zeros_like(dbias_ref)

        hbufs, gbufs, rus, rds = (hbuf_a, hbuf_b), (gbuf_a, gbuf_b), (ru_a, ru_b), (rd_a, rd_b)
        for q in range(nseq):
            for e in range(2):
                rus[e][pl.ds(q, tc, stride=nseq), :] = u_ref[q, :, e * LANES:(e + 1) * LANES]
                rds[e][pl.ds(q, tc, stride=nseq), :] = dy_ref[q, :, e * LANES:(e + 1) * LANES]
        row = lax.broadcasted_iota(jnp.int32, (SUBLANES, s), 0)
        cols = [slice(j * cw, (j + 1) * cw) for j in range(ncol)]
        val = [{}, {}]

        def project_in(e, j):
            hbufs[e][:, cols[j]] = jnp.dot(rus[e][...].astype(BF16), b_ref[e, :, cols[j]], preferred_element_type=F32)

        def scan_fwd(e, between):
            buf = hbufs[e]
            lams = _stage_lams(lam_ref, e, nseq)

            def step(i, carry):
                pr, pi = carry
                r0 = pl.multiple_of(i * SUBLANES, SUBLANES)
                outr = buf[pl.ds(r0, SUBLANES), 0:s]
                outi = buf[pl.ds(r0, SUBLANES), s:2 * s]
                for lr, li in lams:
                    rr = _roll_rows(pr, nseq)
                    ri = _roll_rows(pi, nseq)
                    outr = outr + (lr * rr - li * ri)
                    outi = outi + (lr * ri + li * rr)
                    pr, pi = outr, outi
                buf[pl.ds(r0, SUBLANES), 0:s] = outr
                buf[pl.ds(r0, SUBLANES), s:2 * s] = outi
                return outr, outi

            lo = e * 2 * s
            _scan_with(nblk, step, (hb_ref[:, lo:lo + s], hb_ref[:, lo + s:lo + 2 * s]), between)

        def y_part(e, j):
            p = jnp.dot(hbufs[e][:, cols[j]].astype(BF16), c_ref[e, cols[j], :], preferred_element_type=F32)
            val[e]["y"] = p if j == 0 else val[e]["y"] + p

        def gate(e):
            v = val[e]
            uu = rus[e][...]
            yg, dyg_dy = _gelu_and_grad(v.pop("y") + d_ref[e] * uu)
            yg16 = yg.astype(BF16)
            z = jnp.dot(yg16, w_ref[e], preferred_element_type=F32) + bias_ref[e]
            sg = _sigmoid(z[:, LANES:two])
            dout_e = rds[e][...]
            dz = jnp.concatenate([dout_e * sg, dout_e * z[:, 0:LANES] * sg * (1.0 - sg)], axis=-1)
            dz16 = dz.astype(BF16)
            dw_ref[e] += lax.dot_general(yg16, dz16, tn_dims, preferred_element_type=F32)
            dbias_ref[e] += jnp.sum(dz, axis=0, keepdims=True)
            dy = lax.dot_general(dz16, w_ref[e], nt_dims, preferred_element_type=F32) * dyg_dy
            dd_ref[e] += jnp.sum(dy * uu, axis=0, keepdims=True)
            v["dy"] = dy
            v["dy16"] = dy.astype(BF16)

        def dc_part(e, j):
            dct_ref[e, :, cols[j]] += lax.dot_general(val[e]["dy16"], hbufs[e][:, cols[j]].astype(BF16), tn_dims,
                                                      preferred_element_type=F32)

        def dh_part(e, j):
            gbufs[e][:, cols[j]] = lax.dot_general(val[e]["dy16"], c_ref[e, cols[j], :], nt_dims,
                                                   preferred_element_type=F32)

        def scan_bwd(e, between):
            hbuf, gbuf = hbufs[e], gbufs[e]
            lams = _stage_lams(lam_ref, e, nseq)
            lo = e * 2 * s

            def step(i, carry):
                pr, pi, ar, ai = carry
                blk = nblk - 1 - i
                r0 = pl.multiple_of(blk * SUBLANES, SUBLANES)
                outr = gbuf[pl.ds(r0, SUBLANES), 0:s]
                outi = gbuf[pl.ds(r0, SUBLANES), s:2 * s]
                for lr, li in reversed(lams):
                    rr = _roll_rows(pr, SUBLANES - nseq)
                    ri = _roll_rows(pi, SUBLANES - nseq)
                    outr = outr + (lr * rr + li * ri)
                    outi = outi + (lr * ri - li * rr)
                    pr, pi = outr, outi
                gbuf[pl.ds(r0, SUBLANES), 0:s] = outr
                gbuf[pl.ds(r0, SUBLANES), s:2 * s] = outi
                p0 = pl.multiple_of(jnp.maximum(blk - 1, 0) * SUBLANES, SUBLANES)
                first = blk == 0
                before_r = jnp.where(first, hb_ref[:, lo:lo + s], hbuf[pl.ds(p0, SUBLANES), 0:s])
                before_i = jnp.where(first, hb_ref[:, lo + s:lo + 2 * s], hbuf[pl.ds(p0, SUBLANES), s:2 * s])
                if stages > 1:
                    last_rows = row >= SUBLANES - nseq
                    before_r = _roll_rows(jnp.where(last_rows, before_r, hbuf[pl.ds(r0, SUBLANES), 0:s]), nseq)
                    before_i = _roll_rows(jnp.where(last_rows, before_i, hbuf[pl.ds(r0, SUBLANES), s:2 * s]), nseq)
                return (outr, outi, ar + outr * before_r + outi * before_i, ai - outr * before_i + outi * before_r)

            gr, gi, ar, ai = _scan_with(
                nblk, step, (gst[:, lo:lo + s], gst[:, lo + s:lo + 2 * s], lacc[:, lo:lo + s], lacc[:, lo + s:lo + 2 * s]),
                between)
            gst[:, lo:lo + s] = gr
            gst[:, lo + s:lo + 2 * s] = gi
            lacc[:, lo:lo + s] = ar
            lacc[:, lo + s:lo + 2 * s] = ai

        def du_part(e, j):
            p = lax.dot_general(gbufs[e][:, cols[j]].astype(BF16), b_ref[e, :, cols[j]], nt_dims,
                                preferred_element_type=F32)
            val[e]["du"] = (val[e].pop("dy") * d_ref[e] + p) if j == 0 else val[e]["du"] + p

        def db_part(e, j):
            db_ref[e, :, cols[j]] += lax.dot_general(rus[e][...].astype(BF16), gbufs[e][:, cols[j]].astype(BF16),
                                                     tn_dims, preferred_element_type=F32)

        def parts(fn, e):
            return [functools.partial(fn, e, j) for j in range(ncol)]

        nothing = lambda: None
        middle_of = lambda e: parts(y_part, e) + [functools.partial(gate, e)] + parts(dc_part, e) + parts(dh_part, e)
        last_of = lambda e: parts(du_part, e) + parts(db_part, e)
        for piece in parts(project_in, 0):
            piece()
        scan_fwd(0, parts(project_in, 1))
        scan_fwd(1, middle_of(0) + [nothing] * 3)
        scan_bwd(0, middle_of(1) + [nothing] * 3)
        scan_bwd(1, last_of(0))
        for piece in last_of(1):
            piece()
        for e in range(2):
            rus[e][...] = val[e]["du"]
            for q in range(nseq):
                du_ref[q, :, e * LANES:(e + 1) * LANES] = rus[e][pl.ds(q, tc, stride=nseq), :].astype(du_ref.dtype)

        @pl.when(k == nk - 1)
        def _():
            for e in range(2):
                dlam_ref[e] = jnp.sum(lacc[:, e * 2 * s:(e + 1) * 2 * s], axis=0, keepdims=True)

    blk = lambda shape: pl.BlockSpec(shape, lambda b, k: (b, 0, 0))
    tok = pl.BlockSpec((nseq, tc, two), lambda b, k: (0, nk - 1 - k, b))
    outs, comm_outs = _call(
        body, name=name, grid=(nb // 2, nk),
        in_specs=[tok, tok, pl.BlockSpec((SUBLANES, 4 * s), lambda b, k: (nk - 1 - k, b)),
                  blk((2, 1, 2 * s)), blk((2, LANES, 2 * s)), blk((2, 2 * s, LANES)),
                  blk((2, 1, LANES)), blk((2, LANES, two)), blk((2, 1, two))],
        out_specs=[tok, blk((2, 1, 2 * s)), blk((2, LANES, 2 * s)), blk((2, LANES, 2 * s)),
                   blk((2, 1, LANES)), blk((2, LANES, two)), blk((2, 1, two))],
        out_shape=[jax.ShapeDtypeStruct((nseq, seq, w), BF16),
                   jax.ShapeDtypeStruct((nb, 1, 2 * s), F32), jax.ShapeDtypeStruct((nb, LANES, 2 * s), F32),
                   jax.ShapeDtypeStruct((nb, LANES, 2 * s), F32), jax.ShapeDtypeStruct((nb, 1, LANES), F32),
                   jax.ShapeDtypeStruct((nb, LANES, two), F32), jax.ShapeDtypeStruct((nb, 1, two), F32)],
        scratch_shapes=[pltpu.VMEM((rows, 2 * s), F32)] * 4
        + [pltpu.VMEM((SUBLANES, 4 * s), F32), pltpu.VMEM((SUBLANES, 4 * s), F32)]
        + [pltpu.VMEM((rows, LANES), F32)] * 4,
        semantics=("parallel", "arbitrary"),
        args=(u.reshape(nseq, seq, w), dout.reshape(nseq, seq, w), hb, lam, b_bd, c_bd, dsk, w_bd, bias), comm=comm)
    outs[0] = outs[0].reshape(nseq * seq, w)
    return outs if comm is None else (outs, comm_outs)


ANY = pl.BlockSpec(memory_space=pl.ANY)

BIG = (("ffn1_w_in", True), ("ffn1_w_out", False), ("mix_w_in", True), ("mix_w_out", False),
       ("ffn2_w_in", True), ("ffn2_w_out", False))


def _my_place():
    return lax.axis_index("x"), lax.axis_index("y"), lax.axis_index("c")


def _other_chips(x, y):
    return [(1 - x, y), (x, 1 - y), (1 - x, 1 - y)]


def _half_of_shard(ref, col_sharded, chip, core):
    full_rows, full_cols = ref.shape
    if col_sharded:
        hr, cs = full_rows // 2, full_cols // N_CHIPS
        return ref.at[pl.ds(pl.multiple_of(core * hr, 8), hr), pl.ds(chip * cs, cs)]
    rs = full_rows // N_CHIPS
    return ref.at[pl.ds(pl.multiple_of(chip * rs + core * (rs // 2), 8), rs // 2), :]


def _gather_comm(shards, cols):
    full_shapes = [(sh.shape[0], sh.shape[1] * N_CHIPS) if col else (sh.shape[0] * N_CHIPS, sh.shape[1])
                   for sh, col in zip(shards, cols)]
    nw = len(shards)

    def first_copies(ins, outs, sems):
        send_sems, recv_sems, local_sems = sems
        x, y, c = _my_place()
        me = 2 * x + y
        locals_, sends = [], []
        for wi in range(nw):
            src, dst = ins[wi], outs[wi]
            rs, cs = src.shape
            hs = rs // 2
            if cols[wi]:
                place = dst.at[:, pl.ds(me * cs, cs)]
            else:
                place = dst.at[pl.ds(pl.multiple_of(me * rs, 8), rs), :]
            locals_.append(pltpu.make_async_copy(src, place, local_sems.at[wi]))
            my_half = src.at[pl.ds(pl.multiple_of(c * hs, 8), hs), :]
            for j, (px, py) in enumerate(_other_chips(x, y)):
                sends.append(pltpu.make_async_remote_copy(
                    src_ref=my_half, dst_ref=_half_of_shard(dst, cols[wi], me, c),
                    send_sem=send_sems.at[wi * 6 + j], recv_sem=recv_sems.at[wi * 6 + j],
                    device_id=(px, py, c), device_id_type=MESH))
        return locals_, sends

    def start(ins, outs, sems):
        locals_, sends = first_copies(ins, outs, sems)
        for cp in locals_ + sends:
            cp.start()

    def forwards(outs, sems, wait_landed):
        send_sems, recv_sems, _ = sems
        x, y, c = _my_place()
        out = []
        for wi in range(nw):
            dst = outs[wi]
            for j, (px, py) in enumerate(_other_chips(x, y)):
                got = _half_of_shard(dst, cols[wi], 2 * px + py, c)
                if wait_landed:
                    pltpu.make_async_remote_copy(
                        src_ref=got, dst_ref=got, send_sem=send_sems.at[wi * 6 + j], recv_sem=recv_sems.at[wi * 6 + j],
                        device_id=(px, py, c), device_id_type=MESH).wait_recv()
                out.append(pltpu.make_async_remote_copy(
                    src_ref=got, dst_ref=got, send_sem=send_sems.at[wi * 6 + 3 + j], recv_sem=recv_sems.at[wi * 6 + 3 + j],
                    device_id=(x, y, 1 - c), device_id_type=MESH))
                if wait_landed:
                    out[-1].start()
        return out

    def middle(ins, outs, sems):
        forwards(outs, sems, True)

    def finish(ins, outs, sems):
        send_sems, recv_sems, _ = sems
        x, y, c = _my_place()
        locals_, sends = first_copies(ins, outs, sems)
        for wi in range(nw):
            dst = outs[wi]
            for j, (px, py) in enumerate(_other_chips(x, y)):
                theirs = _half_of_shard(dst, cols[wi], 2 * px + py, 1 - c)
                pltpu.make_async_remote_copy(
                    src_ref=theirs, dst_ref=theirs, send_sem=send_sems.at[wi * 6 + 3 + j],
                    recv_sem=recv_sems.at[wi * 6 + 3 + j], device_id=(x, y, 1 - c), device_id_type=MESH).wait_recv()
        for cp in sends + forwards(outs, sems, False):
            cp.wait_send()
        for cp in locals_:
            cp.wait()

    return _Comm(shards, [jax.ShapeDtypeStruct(s, BF16) for s in full_shapes],
                 [pltpu.SemaphoreType.DMA((6 * nw,)), pltpu.SemaphoreType.DMA((6 * nw,)),
                  pltpu.SemaphoreType.DMA((nw,))], start, finish, middle=middle)


def _pair_exchange_comm(grads, cols):
    nw = len(grads)
    n_copies = sum(1 if col else N_CHIPS for col in cols)

    def copies(ins, outs, sems):
        send_sems, recv_sems = sems
        x, y, c = _my_place()
        out = []
        for wi in range(nw):
            src, dst = ins[wi], outs[wi]
            fr = src.shape[0]
            if cols[wi]:
                hr = fr // 2
                pieces = [(src.at[pl.ds(pl.multiple_of((1 - c) * hr, 8), hr), :], dst)]
            else:
                rs = fr // N_CHIPS
                hs = rs // 2
                pieces = [(src.at[pl.ds(pl.multiple_of(k * rs + (1 - c) * hs, 8), hs), :],
                           dst.at[pl.ds(k * hs, hs), :]) for k in range(N_CHIPS)]
            for s_ref, d_ref in pieces:
                out.append(pltpu.make_async_remote_copy(
                    src_ref=s_ref, dst_ref=d_ref, send_sem=send_sems.at[len(out)], recv_sem=recv_sems.at[len(out)],
                    device_id=(x, y, 1 - c), device_id_type=MESH))
        return out

    def start(ins, outs, sems):
        for cp in copies(ins, outs, sems):
            cp.start()

    def finish(ins, outs, sems):
        for cp in copies(ins, outs, sems):
            cp.wait()

    return _Comm(grads, [jax.ShapeDtypeStruct((g.shape[0] // 2, g.shape[1]), F32) for g in grads],
                 [pltpu.SemaphoreType.DMA((n_copies,)), pltpu.SemaphoreType.DMA((n_copies,))], start, finish)


def _pair_sum(grad, other, col, core, name):
    fr, fc = grad.shape
    pieces = 1 if col else N_CHIPS
    pr = fr // 2 // pieces
    gview = grad.reshape(pieces * 2, pr, fc)
    oview = other.reshape(pieces, pr, fc)
    tr = _tile(pr, 256, 16)

    def body(c_ref, g_ref, o_ref, out_ref):
        out_ref[...] = (g_ref[...] + o_ref[...]).astype(out_ref.dtype)

    out = pl.pallas_call(
        body, name=name,
        grid_spec=pltpu.PrefetchScalarGridSpec(
            num_scalar_prefetch=1, grid=(pieces, pr // tr),
            in_specs=[pl.BlockSpec((1, tr, fc), lambda p, i, cref: (p * 2 + cref[0], i, 0)),
                      pl.BlockSpec((1, tr, fc), lambda p, i, cref: (p, i, 0))],
            out_specs=pl.BlockSpec((1, tr, fc), lambda p, i, cref: (p, i, 0))),
        out_shape=jax.ShapeDtypeStruct((pieces, pr, fc), BF16),
        compiler_params=_params("parallel", "parallel"),
    )(core, gview, oview)
    return out.reshape(fr // 2, fc)


def _chip_exchange_comm(psums, cols):
    nw = len(psums)
    out_shapes = [(N_CHIPS, p.shape[0], p.shape[1] // N_CHIPS) if col else (N_CHIPS, p.shape[0] // N_CHIPS, p.shape[1])
                  for p, col in zip(psums, cols)]

    def copies(ins, outs, sems):
        send_sems, recv_sems, local_sems = sems
        x, y, c = _my_place()
        me = 2 * x + y
        out = []
        for wi in range(nw):
            src = ins[wi]
            mine = outs[wi].at[me]

            def piece(chip, src=src, col=cols[wi]):
                if col:
                    cs = src.shape[1] // N_CHIPS
                    return src.at[:, pl.ds(chip * cs, cs)]
                ps = src.shape[0] // N_CHIPS
                return src.at[pl.ds(pl.multiple_of(chip * ps, 8), ps), :]

            out.append(pltpu.make_async_copy(piece(me), mine, local_sems.at[wi]))
            for j, (px, py) in enumerate(_other_chips(x, y)):
                out.append(pltpu.make_async_remote_copy(
                    src_ref=piece(2 * px + py), dst_ref=mine,
                    send_sem=send_sems.at[wi * 3 + j], recv_sem=recv_sems.at[wi * 3 + j],
                    device_id=(px, py, c), device_id_type=MESH))
        return out

    def start(ins, outs, sems):
        for cp in copies(ins, outs, sems):
            cp.start()

    def finish(ins, outs, sems):
        for cp in copies(ins, outs, sems):
            cp.wait()

    return _Comm(psums, [jax.ShapeDtypeStruct(s, BF16) for s in out_shapes],
                 [pltpu.SemaphoreType.DMA((3 * nw,)), pltpu.SemaphoreType.DMA((3 * nw,)),
                  pltpu.SemaphoreType.DMA((nw,))], start, finish)


def _chip_sum(slots, core, layer, layers, into, name):
    _, hr, cs = slots.shape
    tr = _tile(hr, 256, 16)

    def body(c_ref, s_ref, *rest):
        out_ref = rest[-1]
        acc = s_ref[0].astype(F32)
        for i in range(1, N_CHIPS):
            acc = acc + s_ref[i].astype(F32)
        out_ref[0] = acc

    in_specs = [pl.BlockSpec((N_CHIPS, tr, cs), lambda i, cref: (0, i, 0))]
    args = [core, slots]
    aliases = {}
    if into is not None:
        in_specs.append(pl.BlockSpec(memory_space=pl.ANY))
        args.append(into.reshape(layers * 2, hr, cs))
        aliases = {2: 0}
    out = pl.pallas_call(
        body, name=name,
        grid_spec=pltpu.PrefetchScalarGridSpec(
            num_scalar_prefetch=1, grid=(hr // tr,), in_specs=in_specs,
            out_specs=pl.BlockSpec((1, tr, cs), lambda i, cref: (layer * 2 + cref[0], i, 0))),
        out_shape=jax.ShapeDtypeStruct((layers * 2, hr, cs), F32),
        input_output_aliases=aliases,
        compiler_params=_params("parallel"),
    )(*args)
    return out.reshape(layers, 2 * hr, cs)


def _pair_share_comm(reduced):
    nw = len(reduced)

    def copies(ins, outs, sems):
        send_sems, recv_sems = sems
        x, y, c = _my_place()
        out = []
        for wi in range(nw):
            hs = outs[wi].shape[1] // 2
            mine = outs[wi].at[:, pl.ds(pl.multiple_of(c * hs, 8), hs), :]
            out.append(pltpu.make_async_remote_copy(
                src_ref=mine, dst_ref=mine, send_sem=send_sems.at[wi], recv_sem=recv_sems.at[wi],
                device_id=(x, y, 1 - c), device_id_type=MESH))
        return out

    def start(ins, outs, sems):
        for cp in copies(ins, outs, sems):
            cp.start()

    def finish(ins, outs, sems):
        for cp in copies(ins, outs, sems):
            cp.wait()

    return _Comm(reduced, [jax.ShapeDtypeStruct(r.shape, F32) for r in reduced],
                 [pltpu.SemaphoreType.DMA((nw,)), pltpu.SemaphoreType.DMA((nw,))], start, finish,
                 alias={i: i for i in range(nw)})


def _all_reduce_small(flat, comm):
    rows, lanes = flat.shape
    seg = rows // N_DEV
    c_in, c_out = len(comm.ins), len(comm.outs)

    def body(*refs):
        refs = list(refs)
        in_ref, cins = refs[0], refs[1:1 + c_in]
        out_ref, couts = refs[1 + c_in], refs[2 + c_in:2 + c_in + c_out]
        recv_ref, send_sems, recv_sems = refs[2 + c_in + c_out:5 + c_in + c_out]
        csems = refs[5 + c_in + c_out:]
        comm.start(cins, couts, csems)
        x, y, c = _my_place()
        me = 4 * x + 2 * y + c

        def peer(r):
            fx, fy, fc = (r >> 2) & 1, (r >> 1) & 1, r & 1
            px = jnp.where(fx == 1, 1 - x, x)
            py = jnp.where(fy == 1, 1 - y, y)
            pc = jnp.where(fc == 1, 1 - c, c)
            return px, py, pc

        first = []
        for r in range(1, N_DEV):
            px, py, pc = peer(r)
            theirs = in_ref.at[pl.ds(pl.multiple_of((4 * px + 2 * py + pc) * seg, 8), seg), :]
            cp = pltpu.make_async_remote_copy(
                src_ref=theirs, dst_ref=recv_ref.at[r], send_sem=send_sems.at[r - 1], recv_sem=recv_sems.at[r - 1],
                device_id=(px, py, pc), device_id_type=MESH)
            cp.start()
            first.append(cp)
        for cp in first:
            cp.wait()
        my_rows = pl.ds(pl.multiple_of(me * seg, 8), seg)
        acc = in_ref[my_rows, :]
        for r in range(1, N_DEV):
            acc = acc + recv_ref[r]
        out_ref[my_rows, :] = acc
        second = []
        for r in range(1, N_DEV):
            px, py, pc = peer(r)
            cp = pltpu.make_async_remote_copy(
                src_ref=out_ref.at[my_rows, :], dst_ref=out_ref.at[my_rows, :],
                send_sem=send_sems.at[6 + r], recv_sem=recv_sems.at[6 + r],
                device_id=(px, py, pc), device_id_type=MESH)
            cp.start()
            second.append(cp)
        for r in range(1, N_DEV):
            px, py, pc = peer(r)
            theirs = out_ref.at[pl.ds(pl.multiple_of((4 * px + 2 * py + pc) * seg, 8), seg), :]
            pltpu.make_async_remote_copy(
                src_ref=theirs, dst_ref=theirs, send_sem=send_sems.at[6 + r], recv_sem=recv_sems.at[6 + r],
                device_id=(px, py, pc), device_id_type=MESH).wait_recv()
        for cp in second:
            cp.wait_send()
        comm.finish(cins, couts, csems)

    vm = pl.BlockSpec(memory_space=pltpu.VMEM)
    any_spec = pl.BlockSpec(memory_space=pl.ANY)
    outs = pl.pallas_call(
        body, name="all_reduce_small",
        in_specs=[vm] + [any_spec] * c_in, out_specs=[vm] + [any_spec] * c_out,
        out_shape=[jax.ShapeDtypeStruct((rows, lanes), F32)] + comm.outs,
        scratch_shapes=[pltpu.VMEM((N_DEV, seg, lanes), F32),
                        pltpu.SemaphoreType.DMA((2 * (N_DEV - 1),)), pltpu.SemaphoreType.DMA((2 * (N_DEV - 1),))]
        + comm.sems,
        input_output_aliases={1 + ci: 1 + co for ci, co in comm.alias.items()},
        compiler_params=pltpu.CompilerParams(vmem_limit_bytes=VMEM_LIMIT),
    )(flat, *comm.ins)
    return outs[0], list(outs[1:])


def _adamw_update(w_ref, g_ref, m_ref, v_ref, d_ref, nm_ref, nv_ref):
    c1 = 1.0 - ADAM_B1 ** ADAM_STEP
    c2 = 1.0 - ADAM_B2 ** ADAM_STEP
    gv = g_ref[...]
    nm = ADAM_B1 * m_ref[...] + (1.0 - ADAM_B1) * gv
    nv = ADAM_B2 * v_ref[...] + (1.0 - ADAM_B2) * (gv * gv)
    d_ref[...] = -ADAM_LR * ((nm / c1) / (jnp.sqrt(nv / c2) + ADAM_EPS) + ADAM_WD * w_ref[...])
    nm_ref[...] = nm
    nv_ref[...] = nv


def _adamw_many(ws, gs, ms, vs, name):
    n = len(ws)

    def body(*refs):
        for i in range(n):
            _adamw_update(*[refs[k * n + i] for k in range(7)])

    shapes = [jax.ShapeDtypeStruct(w.shape, F32) for w in ws]
    outs = pl.pallas_call(
        body, name=name, out_shape=shapes * 3,
        compiler_params=pltpu.CompilerParams(vmem_limit_bytes=VMEM_LIMIT),
    )(*ws, *gs, *ms, *vs)
    return outs[:n], outs[n:2 * n], outs[2 * n:]


def _adamw(w, g, m, v, name):
    rows, cols = w.shape
    tr = _tile(rows, 256, 8)

    def body(w_ref, g_ref, m_ref, v_ref, go_ref, d_ref, nm_ref, nv_ref):
        go_ref[...] = g_ref[...]
        _adamw_update(w_ref, g_ref, m_ref, v_ref, d_ref, nm_ref, nv_ref)

    blk = pl.BlockSpec((tr, cols), lambda i: (i, 0))
    sds = jax.ShapeDtypeStruct((rows, cols), F32)
    return pl.pallas_call(
        body, name=name, grid=(rows // tr,),
        in_specs=[blk] * 4, out_specs=[blk] * 4, out_shape=[sds] * 4,
        compiler_params=_params("parallel"),
    )(w, g, m, v)


SMALL = ("norm_ffn1", "norm_mix", "ssm_a_re", "ssm_a_im", "ssm_log_dt", "ssm_b_re", "ssm_b_im", "ssm_c_re",
         "ssm_c_im", "ssm_d", "ssm_glu_w", "ssm_glu_b", "gm_v_gain", "gm_w_s", "gm_b_s", "gain_ssm_out",
         "gain_gm_out", "norm_ffn2", "norm_final")
WEIGHTS = ("norm_ffn1", "ffn1_w_in", "ffn1_w_out", "norm_mix", "mix_w_in", "ssm_a_re", "ssm_a_im", "ssm_log_dt",
           "ssm_b_re", "ssm_b_im", "ssm_c_re", "ssm_c_im", "ssm_d", "ssm_glu_w", "ssm_glu_b", "gm_v_gain", "gm_w_s",
           "gm_b_s", "gain_ssm_out", "gain_gm_out", "mix_w_out", "norm_ffn2", "ffn2_w_in", "ffn2_w_out", "norm_final")


def _ffn_fwd(x, gain, w_in, w_out, tag, hosted=None):
    if hosted is None:
        h, t, q, a = _ffn_in_fwd(x, gain, w_in, f"{tag}_in")
    else:
        (h, t, q, a), got = _ffn_in_fwd(x, gain, w_in, f"{tag}_in_hosting", comm=hosted[0]())
        hosted[1](got)
    if callable(w_out):
        w_out = w_out()
    out = _matmul(a, w_out, "nn", scale=0.5, res=x, tm=512, tn=1024, tk=4096, name=f"{tag}_out")
    return out, (x, h, t, q, a)


def _ffn_bwd(dout, saved, gain, w_in, w_out, tag, hooks=None, publish=None, late_out_dw=False):
    x, h, t, q, a = saved
    f = t.shape[1]
    hooks = hooks or {}

    def hosted(key, fn, *args, name, **kw):
        if key not in hooks:
            return fn(*args, name=name, **kw)
        make, take = hooks[key]
        *res, got = fn(*args, name=f"{name}_hosting", comm=make(), **kw)
        take(got)
        return res[0] if len(res) == 1 else tuple(res)

    def out_dw():
        dw = hosted("out_dw", _matmul, a, dout, "tn", scale=0.5, tm=1536, tn=1024, tk=2048, name=f"{tag}_out_dw")
        if publish is not None:
            publish("out", dw)
        return dw

    dg, du = hosted("out_dx", _ffn_out_bwd, dout, w_out, t, q, name=f"{tag}_out_dx")
    if not late_out_dw:
        dw_out = out_dw()
    dw_in = hosted("in_dw_g", _matmul, h, dg, "tn", tm=512, tn=1536, tk=4096, name=f"{tag}_in_dw_g",
                   out_cols=2 * f)
    dw_in = hosted("in_dw_u", _matmul, h, du, "tn", tm=512, tn=1536, tk=4096, name=f"{tag}_in_dw_u",
                   out_cols=2 * f, col_off=f, into=dw_in)
    if publish is not None:
        publish("in", dw_in)
    if late_out_dw:
        dw_out = out_dw()
    dx, dgain = hosted("in_dx", _proj_in_bwd, [(dg, 0), (du, f)], w_in, x, gain, dout, name=f"{tag}_in_dx")
    return dx, dgain, dw_in, dw_out


def kernel(x, norm_ffn1, ffn1_w_in, ffn1_w_out, norm_mix, mix_w_in, ssm_a_re, ssm_a_im, ssm_log_dt, ssm_b_re, ssm_b_im, ssm_c_re, ssm_c_im, ssm_d, ssm_glu_w, ssm_glu_b, gm_v_gain, gm_w_s, gm_b_s, gain_ssm_out, gain_gm_out, mix_w_out, norm_ffn2, ffn2_w_in, ffn2_w_out, norm_final, loss_target, m_norm_ffn1, m_ffn1_w_in, m_ffn1_w_out, m_norm_mix, m_mix_w_in, m_ssm_a_re, m_ssm_a_im, m_ssm_log_dt, m_ssm_b_re, m_ssm_b_im, m_ssm_c_re, m_ssm_c_im, m_ssm_d, m_ssm_glu_w, m_ssm_glu_b, m_gm_v_gain, m_gm_w_s, m_gm_b_s, m_gain_ssm_out, m_gain_gm_out, m_mix_w_out, m_norm_ffn2, m_ffn2_w_in, m_ffn2_w_out, m_norm_final, v_norm_ffn1, v_ffn1_w_in, v_ffn1_w_out, v_norm_mix, v_mix_w_in, v_ssm_a_re, v_ssm_a_im, v_ssm_log_dt, v_ssm_b_re, v_ssm_b_im, v_ssm_c_re, v_ssm_c_im, v_ssm_d, v_ssm_glu_w, v_ssm_glu_b, v_gm_v_gain, v_gm_w_s, v_gm_b_s, v_gain_ssm_out, v_gain_gm_out, v_mix_w_out, v_norm_ffn2, v_ffn2_w_in, v_ffn2_w_out, v_norm_final):
    wts = dict(norm_ffn1=norm_ffn1, ffn1_w_in=ffn1_w_in, ffn1_w_out=ffn1_w_out, norm_mix=norm_mix, mix_w_in=mix_w_in,
               ssm_a_re=ssm_a_re, ssm_a_im=ssm_a_im, ssm_log_dt=ssm_log_dt, ssm_b_re=ssm_b_re, ssm_b_im=ssm_b_im,
               ssm_c_re=ssm_c_re, ssm_c_im=ssm_c_im, ssm_d=ssm_d, ssm_glu_w=ssm_glu_w, ssm_glu_b=ssm_glu_b,
               gm_v_gain=gm_v_gain, gm_w_s=gm_w_s, gm_b_s=gm_b_s, gain_ssm_out=gain_ssm_out, gain_gm_out=gain_gm_out,
               mix_w_out=mix_w_out, norm_ffn2=norm_ffn2, ffn2_w_in=ffn2_w_in, ffn2_w_out=ffn2_w_out,
               norm_final=norm_final)
    mom = dict(norm_ffn1=m_norm_ffn1, ffn1_w_in=m_ffn1_w_in, ffn1_w_out=m_ffn1_w_out, norm_mix=m_norm_mix,
               mix_w_in=m_mix_w_in, ssm_a_re=m_ssm_a_re, ssm_a_im=m_ssm_a_im, ssm_log_dt=m_ssm_log_dt,
               ssm_b_re=m_ssm_b_re, ssm_b_im=m_ssm_b_im, ssm_c_re=m_ssm_c_re, ssm_c_im=m_ssm_c_im, ssm_d=m_ssm_d,
               ssm_glu_w=m_ssm_glu_w, ssm_glu_b=m_ssm_glu_b, gm_v_gain=m_gm_v_gain, gm_w_s=m_gm_w_s, gm_b_s=m_gm_b_s,
               gain_ssm_out=m_gain_ssm_out, gain_gm_out=m_gain_gm_out, mix_w_out=m_mix_w_out, norm_ffn2=m_norm_ffn2,
               ffn2_w_in=m_ffn2_w_in, ffn2_w_out=m_ffn2_w_out, norm_final=m_norm_final)
    var = dict(norm_ffn1=v_norm_ffn1, ffn1_w_in=v_ffn1_w_in, ffn1_w_out=v_ffn1_w_out, norm_mix=v_norm_mix,
               mix_w_in=v_mix_w_in, ssm_a_re=v_ssm_a_re, ssm_a_im=v_ssm_a_im, ssm_log_dt=v_ssm_log_dt,
               ssm_b_re=v_ssm_b_re, ssm_b_im=v_ssm_b_im, ssm_c_re=v_ssm_c_re, ssm_c_im=v_ssm_c_im, ssm_d=v_ssm_d,
               ssm_glu_w=v_ssm_glu_w, ssm_glu_b=v_ssm_glu_b, gm_v_gain=v_gm_v_gain, gm_w_s=v_gm_w_s, gm_b_s=v_gm_b_s,
               gain_ssm_out=v_gain_ssm_out, gain_gm_out=v_gain_gm_out, mix_w_out=v_mix_w_out, norm_ffn2=v_norm_ffn2,
               ffn2_w_in=v_ffn2_w_in, ffn2_w_out=v_ffn2_w_out, norm_final=v_norm_final)

    nseq, seq, d = x.shape
    n = nseq * seq
    depth = norm_ffn1.shape[0]
    width = gain_ssm_out.shape[1]
    groups = ssm_a_re.shape[1]
    heads = gm_w_s.shape[1]
    core = lax.axis_index("c").astype(jnp.int32).reshape(1)

    is_col = dict(BIG)
    full = {name: [None] * depth for name, _ in BIG}

    def gather_comm(pairs):
        return _gather_comm([wts[nm][l].astype(BF16) for nm, l in pairs], [is_col[nm] for nm, _ in pairs])

    def store(pairs, arrays):
        for (nm, l), w in zip(pairs, arrays):
            full[nm][l] = w

    pairs = [("ffn1_w_in", 0)]
    store(pairs, _run_comm(gather_comm(pairs), "all_gather_first"))

    xs = x.reshape(n, d)
    saved = []
    for l in range(depth):
        pairs = [("ffn1_w_out", l)] + ([("mix_w_in", l), ("mix_w_out", l)] if l == 0 else [])
        x1, s_ffn1 = _ffn_fwd(xs, norm_ffn1[l], full["ffn1_w_in"][l], lambda l=l: full["ffn1_w_out"][l], "ffn1",
                              hosted=(functools.partial(gather_comm, pairs), functools.partial(store, pairs)))
        pairs = [("ffn2_w_out", l)]
        hm, u_ssm, zgm, got = _mix_in_fwd(x1, norm_mix[l], full["mix_w_in"][l], width, "mix_in",
                                          comm=gather_comm(pairs))
        store(pairs, got)
        bt_re = jnp.swapaxes(ssm_b_re[l], 1, 2)
        bt_im = jnp.swapaxes(ssm_b_im[l], 1, 2)
        disc_in = (ssm_a_re[l], ssm_a_im[l], ssm_log_dt[l].reshape(groups, 1), bt_re, bt_im)
        lr, li, bbr, bbi = _disc_fwd(*disc_in)
        ops = _ssm_operands(lr, li, bbr, bbi, ssm_c_re[l], ssm_c_im[l], ssm_d[l], ssm_glu_w[l], ssm_glu_b[l])
        pairs = [("ffn2_w_in", l)]
        y_ssm, hb, got = _ssm_fwd_pair(u_ssm, ops, nseq, "s5_fwd", comm=gather_comm(pairs))
        store(pairs, got)
        bias_tile = jnp.broadcast_to(gm_b_s[l].T[:, :, None], (GM_CHUNK, heads, GM_HEAD_DIM)).reshape(GM_CHUNK, width)
        y_gm = _gmlp_fwd(zgm, gm_v_gain[l], gm_w_s[l], bias_tile, "gmlp_fwd")
        if l + 1 < depth:
            pairs = [("mix_w_in", l + 1), ("mix_w_out", l + 1)]
            ycat, x2, got = _mix_out_fwd(y_ssm, y_gm, gain_ssm_out[l], gain_gm_out[l], full["mix_w_out"][l], x1,
                                         "mix_out_hosting", comm=gather_comm(pairs))
            store(pairs, got)
        else:
            ycat, x2 = _mix_out_fwd(y_ssm, y_gm, gain_ssm_out[l], gain_gm_out[l], full["mix_w_out"][l], x1, "mix_out")
        hosted = None
        if l + 1 < depth:
            pairs = [("ffn1_w_in", l + 1)]
            hosted = (functools.partial(gather_comm, pairs), functools.partial(store, pairs))
        x3, s_ffn2 = _ffn_fwd(x2, norm_ffn2[l], full["ffn2_w_in"][l], full["ffn2_w_out"][l], "ffn2", hosted=hosted)
        saved.append(dict(ffn1=s_ffn1, x1=x1, hm=hm, zgm=zgm, disc_in=disc_in, ops=ops, u_ssm=u_ssm, hb=hb, y_ssm=y_ssm,
                          bias_tile=bias_tile, y_gm=y_gm, ycat=ycat, ffn2=s_ffn2))
        xs = x3

    dx, g_norm_final, loss_part = _loss_head(xs, norm_final, loss_target.reshape(n, d))
    big = {name: [None] * depth for name, _ in BIG}
    small = {name: [None] * depth for name in SMALL if name != "norm_final"}
    gpb = GROUPS_PER_BLOCK
    s_blk = STATES_PER_BLOCK
    psum_of, reduced, grads = {}, {}, {}
    shared_early = ["ffn2_w_in", "ffn2_w_out", "mix_w_in", "mix_w_out"]

    def swap_comm(pairs):
        return _pair_exchange_comm([big[nm][l] for nm, l in pairs], [is_col[nm] for nm, _ in pairs])

    def take_swapped(pairs, others):
        for (nm, l), other in zip(pairs, others):
            psum_of[nm, l] = _pair_sum(big[nm][l], other, is_col[nm], core, f"grad_pair_sum_{nm}")

    def send_comm(pairs):
        return _chip_exchange_comm([psum_of[p] for p in pairs], [is_col[nm] for nm, _ in pairs])

    def take_sent(pairs, slots):
        for (nm, l), s in zip(pairs, slots):
            reduced[nm] = _chip_sum(s, core, l, depth, reduced.get(nm), f"grad_chip_sum_{nm}")

    def hosting(make, take, pairs):
        return functools.partial(make, pairs), functools.partial(take, pairs)

    for l in reversed(range(depth)):
        sv = saved[l]
        above = [(nm, l + 1) for nm in ("mix_w_in", "mix_w_out", "ffn1_w_in", "ffn1_w_out")] if l + 1 < depth else []
        dx, small["norm_ffn2"][l], big["ffn2_w_in"][l], big["ffn2_w_out"][l] = _ffn_bwd(
            dx, sv["ffn2"], norm_ffn2[l], full["ffn2_w_in"][l], full["ffn2_w_out"][l], "ffn2",
            hooks={"out_dx": hosting(swap_comm, take_swapped, above)} if above else None)
        mine = [("ffn2_w_in", l), ("ffn2_w_out", l)]
        dy_ssm, dy_gm, small["gain_ssm_out"][l], small["gain_gm_out"][l], got = _mix_out_bwd(
            dx, full["mix_w_out"][l], sv["y_ssm"], sv["y_gm"], gain_ssm_out[l], gain_gm_out[l], "mix_out_dx",
            comm=swap_comm(mine))
        take_swapped(mine, got)
        big["mix_w_out"][l] = _matmul(sv["ycat"], dx, "tn", tm=1024, tn=1024, tk=2048, name="mix_out_dw")
        dzgm, small["gm_w_s"][l], dbias_tile, small["gm_v_gain"][l] = _gmlp_bwd(
            sv["zgm"], dy_gm, gm_v_gain[l], gm_w_s[l], sv["bias_tile"], "gmlp_bwd")
        small["gm_b_s"][l] = dbias_tile.reshape(GM_CHUNK, heads, GM_HEAD_DIM).sum(-1).T
        (du_ssm, dlam, db_bd, dct_bd, dd, dw_bd, dbias), got = _ssm_bwd_pair(
            sv["u_ssm"], dy_ssm, sv["hb"], sv["ops"], nseq, "s5_bwd", comm=send_comm(mine + above))
        take_sent(mine + above, got)
        dlr = dlam[:, 0, :s_blk].reshape(groups, SSM_STATE)
        dli = dlam[:, 0, s_blk:].reshape(groups, SSM_STATE)
        dbbr = _block_diag_extract(db_bd[:, :, :s_blk], gpb)
        dbbi = _block_diag_extract(db_bd[:, :, s_blk:], gpb)
        da_re, da_im, dldt, dbt_re, dbt_im = _disc_bwd(*sv["disc_in"], dlr, dli, dbbr, dbbi)
        small["ssm_a_re"][l], small["ssm_a_im"][l], small["ssm_log_dt"][l] = da_re, da_im, dldt.reshape(groups)
        small["ssm_b_re"][l] = jnp.swapaxes(dbt_re, 1, 2)
        small["ssm_b_im"][l] = jnp.swapaxes(dbt_im, 1, 2)
        small["ssm_c_re"][l] = _block_diag_extract(dct_bd[:, :, :s_blk], gpb)
        small["ssm_c_im"][l] = -_block_diag_extract(dct_bd[:, :, s_blk:], gpb)
        small["ssm_d"][l] = dd.reshape(groups, SSM_CH)
        small["ssm_glu_w"][l] = jnp.concatenate(
            [_block_diag_extract(dw_bd[:, :, :LANES], gpb), _block_diag_extract(dw_bd[:, :, LANES:], gpb)], axis=-1)
        small["ssm_glu_b"][l] = jnp.concatenate(
            [dbias[:, 0, :LANES].reshape(groups, SSM_CH), dbias[:, 0, LANES:].reshape(groups, SSM_CH)], axis=-1)
        cols_mi = 3 * width
        dw_mi = _matmul(sv["hm"], du_ssm, "tn", tm=1024, tn=width, tk=2048, name="mix_in_dw_ssm", out_cols=cols_mi)
        big["mix_w_in"][l] = _matmul(sv["hm"], dzgm, "tn", tm=1024, tn=width, tk=2048, name="mix_in_dw_gm",
                                     out_cols=cols_mi, col_off=width, into=dw_mi)
        dx, small["norm_mix"][l] = _proj_in_bwd([(du_ssm, 0), (dzgm, width)], full["mix_w_in"][l], sv["x1"],
                                                norm_mix[l], dx, "mix_in_dx")
        hooks = None
        if l == 0:
            mix, w_out_0, w_in_0 = [("mix_w_in", 0), ("mix_w_out", 0)], [("ffn1_w_out", 0)], [("ffn1_w_in", 0)]

            def last_make():
                return _merge_comms(_merge_comms(send_comm(w_in_0), swap_comm(w_out_0)),
                                    _pair_share_comm([reduced[nm] for nm in shared_early]))

            def last_take(got):
                take_sent(w_in_0, got[:1])
                take_swapped(w_out_0, got[1:2])
                grads.update(zip(shared_early, got[2:]))

            hooks = {"out_dx": hosting(swap_comm, take_swapped, mix), "in_dw_g": hosting(send_comm, take_sent, mix),
                     "out_dw": hosting(swap_comm, take_swapped, w_in_0), "in_dx": (last_make, last_take)}

        def publish(which, dw, l=l):
            big[f"ffn1_w_{which}"][l] = dw

        dx, small["norm_ffn1"][l], big["ffn1_w_in"][l], big["ffn1_w_out"][l] = _ffn_bwd(
            dx, sv["ffn1"], norm_ffn1[l], full["ffn1_w_in"][l], full["ffn1_w_out"][l], "ffn1",
            hooks=hooks, publish=publish, late_out_dw=(l == 0))
    grad_x = dx.reshape(nseq, seq, d)

    pieces = [jnp.stack(small[name]).reshape(-1) for name in SMALL if name != "norm_final"]
    pieces += [g_norm_final.reshape(-1), loss_part.reshape(1)]
    sizes = [p.shape[0] for p in pieces]
    total = sum(sizes)
    rows = -(-total // (LANES * N_DEV * SUBLANES)) * N_DEV * SUBLANES
    pad = rows * LANES - total
    tail = [("ffn1_w_out", 0)]
    flat_g, got = _all_reduce_small(
        jnp.concatenate(pieces + [jnp.zeros((pad,), F32)]).reshape(rows, LANES), send_comm(tail))
    take_sent(tail, got)
    flat_g = flat_g.reshape(-1)
    loss = flat_g[total - 1]

    names = [name for name, _ in BIG if name not in shared_early]
    grads.update(zip(names, _run_comm(_pair_share_comm([reduced[nm] for nm in names]), "grad_pair_share")))
    offs = 0
    for name, size in zip(SMALL, sizes[:-1]):
        grads[name] = flat_g[offs:offs + size].reshape(wts[name].shape)
        offs += size

    delta, new_m, new_v = {}, {}, {}
    for name, _ in BIG:
        shape = wts[name].shape
        two_d = lambda a: a.reshape(shape[0] * shape[1], shape[2])
        go, dl, nm, nv = _adamw(two_d(wts[name]), two_d(grads[name]), two_d(mom[name]), two_d(var[name]),
                                f"adamw_{name}")
        grads[name] = go.reshape(shape)
        delta[name], new_m[name], new_v[name] = dl.reshape(shape), nm.reshape(shape), nv.reshape(shape)
    at_least_2d = lambda a: a.reshape(1, -1) if a.ndim == 1 else a
    dls, nms, nvs = _adamw_many(*[[at_least_2d(tree[k]) for k in SMALL] for tree in (wts, grads, mom, var)],
                                "adamw_small")
    for name, dl, nm, nv in zip(SMALL, dls, nms, nvs):
        shape = wts[name].shape
        delta[name], new_m[name], new_v[name] = dl.reshape(shape), nm.reshape(shape), nv.reshape(shape)

    return (loss, grad_x, *[grads[k] for k in WEIGHTS], *[delta[k] for k in WEIGHTS],
            *[new_m[k] for k in WEIGHTS], *[new_v[k] for k in WEIGHTS])
```

```python
import functools
import math

import jax
import jax.numpy as jnp
from jax import lax
from jax.experimental import pallas as pl
from jax.experimental.pallas import tpu as pltpu

F32 = jnp.float32
BF16 = jnp.bfloat16
MESH = pl.DeviceIdType.MESH

EPS = 1e-6
SSM_CH = 16
SSM_STATE = 64
GM_CHUNK = 128
GM_HEAD_DIM = 128
SUBLANES = 8
LANES = 128
GROUPS_PER_BLOCK = LANES // SSM_CH
STATES_PER_BLOCK = GROUPS_PER_BLOCK * SSM_STATE
SSM_TIME_CHUNK = 128
N_CHIPS = 4
N_DEV = 8

ADAM_LR = 0.001
ADAM_B1 = 0.9
ADAM_B2 = 0.999
ADAM_EPS = 1e-08
ADAM_WD = 0.01
ADAM_STEP = 10

VMEM_LIMIT = 56 * 1024 * 1024


def _tile(dim, pref, align):
    best = None
    t = align
    while t <= min(dim, pref):
        if dim % t == 0:
            best = t
        t += align
    return best if best is not None else dim


def _params(*sem):
    return pltpu.CompilerParams(dimension_semantics=sem, vmem_limit_bytes=VMEM_LIMIT)


def _gelu(x):
    c = math.sqrt(2.0 / math.pi)
    return 0.5 * x * (1.0 + jnp.tanh(c * (x + 0.044715 * x * x * x)))


def _gelu_and_grad(x):
    c = math.sqrt(2.0 / math.pi)
    t = jnp.tanh(c * (x + 0.044715 * x * x * x))
    g = 0.5 * x * (1.0 + t)
    dg = 0.5 * (1.0 + t) + 0.5 * x * (1.0 - t * t) * c * (1.0 + 3.0 * 0.044715 * x * x)
    return g, dg


def _sigmoid(x):
    return 0.5 * jnp.tanh(0.5 * x) + 0.5


def _matmul(a, b, mode, *, out_dtype=F32, scale=1.0, res=None, tm=512, tn=1024, tk=1024, name="mm",
            out_cols=None, col_off=0, into=None, comm=None):
    if mode == "nn":
        (m, k), (k2, n) = a.shape, b.shape
    elif mode == "nt":
        (m, k), (n, k2) = a.shape, b.shape
    else:
        (k, m), (k2, n) = a.shape, b.shape
    assert k == k2, (a.shape, b.shape, mode)
    tm = _tile(m, tm, 16 if mode != "tn" else LANES)
    tn = _tile(n, tn, LANES)
    tk = _tile(k, tk, LANES if mode != "tn" else 16)
    nk = k // tk
    grid = (m // tm, n // tn, nk)
    if mode == "nn":
        a_spec = pl.BlockSpec((tm, tk), lambda i, j, kk: (i, kk))
        b_spec = pl.BlockSpec((tk, tn), lambda i, j, kk: (kk, j))
        dims = (((1,), (0,)), ((), ()))
    elif mode == "nt":
        a_spec = pl.BlockSpec((tm, tk), lambda i, j, kk: (i, kk))
        b_spec = pl.BlockSpec((tn, tk), lambda i, j, kk: (j, kk))
        dims = (((1,), (1,)), ((), ()))
    else:
        a_spec = pl.BlockSpec((tk, tm), lambda i, j, kk: (kk, i))
        b_spec = pl.BlockSpec((tk, tn), lambda i, j, kk: (kk, j))
        dims = (((0,), (0,)), ((), ()))
    assert col_off % tn == 0
    off = col_off // tn
    r_spec = pl.BlockSpec((tm, tn), lambda i, j, kk: (i, j))
    o_spec = pl.BlockSpec((tm, tn), lambda i, j, kk: (i, j + off))
    has_res = res is not None
    has_into = into is not None

    def body(*refs):
        refs = list(refs)
        a_ref, b_ref = refs[:2]
        pos = 2
        r_ref = None
        if has_res:
            r_ref = refs[pos]
            pos += 1
        if has_into:
            pos += 1
        o_ref = refs[pos]
        acc_ref = refs[pos + 1] if nk > 1 else None
        part = lax.dot_general(a_ref[...].astype(BF16), b_ref[...].astype(BF16), dims,
                               preferred_element_type=F32)

        def finish(r):
            if scale != 1.0:
                r = r * scale
            if has_res:
                r = r + r_ref[...].astype(F32)
            o_ref[...] = r.astype(o_ref.dtype)

        if nk == 1:
            finish(part)
        else:
            kk = pl.program_id(2)

            @pl.when(kk == 0)
            def _():
                acc_ref[...] = part

            @pl.when(kk > 0)
            def _():
                acc_ref[...] += part

            @pl.when(kk == nk - 1)
            def _():
                finish(acc_ref[...])

    in_specs = [a_spec, b_spec]
    args = [a, b]
    if has_res:
        in_specs.append(r_spec)
        args.append(res)
    aliases = {}
    if has_into:
        in_specs.append(pl.BlockSpec(memory_space=pl.ANY))
        args.append(into)
        aliases = {len(args) - 1: 0}
    (out,), comm_outs = _call(
        body, name=name, grid=grid, in_specs=in_specs, out_specs=[o_spec],
        out_shape=[jax.ShapeDtypeStruct((m, n if out_cols is None else out_cols), out_dtype)],
        scratch_shapes=[pltpu.VMEM((tm, tn), F32)] if nk > 1 else [],
        aliases=aliases, semantics=("parallel", "parallel", "arbitrary"), args=args, comm=comm)
    return out if comm is None else (out, comm_outs)


class _Comm:
    def __init__(self, ins, outs, sems, start, finish, alias=None, middle=None):
        self.ins, self.outs, self.sems, self.start, self.finish = list(ins), list(outs), list(sems), start, finish
        self.alias = dict(alias or {})
        self.middle = middle


def _merge_comms(a, b):
    assert a.middle is None and b.middle is None
    cut = (len(a.ins), len(a.outs), len(a.sems))

    def both(which):
        def run(ins, outs, sems):
            getattr(a, which)(ins[:cut[0]], outs[:cut[1]], sems[:cut[2]])
            getattr(b, which)(ins[cut[0]:], outs[cut[1]:], sems[cut[2]:])
        return run

    alias = dict(a.alias)
    alias.update({cut[0] + ci: cut[1] + co for ci, co in b.alias.items()})
    return _Comm(a.ins + b.ins, a.outs + b.outs, a.sems + b.sems, both("start"), both("finish"), alias=alias)


def _call(body, *, name, grid, in_specs, out_specs, out_shape, args, scratch_shapes=(), semantics=(), aliases=None,
          comm=None):
    in_specs, out_specs, out_shape = list(in_specs), list(out_specs), list(out_shape)
    scratch_shapes = list(scratch_shapes)
    aliases = dict(aliases or {})
    if comm is None:
        outs = pl.pallas_call(
            body, name=name, grid=grid, in_specs=in_specs, out_specs=out_specs, out_shape=out_shape,
            scratch_shapes=scratch_shapes, input_output_aliases=aliases, compiler_params=_params(*semantics),
        )(*args)
        return list(outs), []
    n_in, n_out, n_scr = len(in_specs), len(out_specs), len(scratch_shapes)
    c_in, c_out = len(comm.ins), len(comm.outs)
    for ci, co in comm.alias.items():
        aliases[n_in + ci] = n_out + co

    def hosted(*refs):
        refs = list(refs)
        ins, cins = refs[:n_in], refs[n_in:n_in + c_in]
        p = n_in + c_in
        outs, couts = refs[p:p + n_out], refs[p + n_out:p + n_out + c_out]
        p += n_out + c_out
        scr, sems = refs[p:p + n_scr], refs[p + n_scr:]
        ids = [pl.program_id(a) for a in range(len(grid))]
        first = functools.reduce(jnp.logical_and, [i == 0 for i in ids])
        last = functools.reduce(jnp.logical_and, [i == g - 1 for i, g in zip(ids, grid)])

        total = math.prod(grid)
        late = comm.middle is not None and total >= 4

        @pl.when(first)
        def _():
            comm.start(cins, couts, sems)

        if late:
            flat = functools.reduce(lambda acc, ig: acc * ig[1] + ig[0], zip(ids, grid), 0)

            @pl.when(flat == (3 * total) // 4)
            def _():
                comm.middle(cins, couts, sems)

        body(*ins, *outs, *scr)

        @pl.when(last)
        def _():
            if comm.middle is not None and not late:
                comm.middle(cins, couts, sems)
            comm.finish(cins, couts, sems)

    any_spec = pl.BlockSpec(memory_space=pl.ANY)
    outs = pl.pallas_call(
        hosted, name=name, grid=grid, in_specs=in_specs + [any_spec] * c_in, out_specs=out_specs + [any_spec] * c_out,
        out_shape=out_shape + comm.outs, scratch_shapes=scratch_shapes + comm.sems, input_output_aliases=aliases,
        compiler_params=_params(*(["arbitrary"] * len(grid))),
    )(*args, *comm.ins)
    return list(outs[:n_out]), list(outs[n_out:])


def _run_comm(comm, name):
    c_in, c_out = len(comm.ins), len(comm.outs)

    def body(*refs):
        refs = list(refs)
        cins, couts, sems = refs[:c_in], refs[c_in:c_in + c_out], refs[c_in + c_out:]
        comm.start(cins, couts, sems)
        if comm.middle is not None:
            comm.middle(cins, couts, sems)
        comm.finish(cins, couts, sems)

    any_spec = pl.BlockSpec(memory_space=pl.ANY)
    return list(pl.pallas_call(
        body, name=name, in_specs=[any_spec] * c_in, out_specs=[any_spec] * c_out, out_shape=comm.outs,
        scratch_shapes=comm.sems, input_output_aliases=comm.alias,
    )(*comm.ins))


def _loss_head(x, gain, target):
    n, d = x.shape
    tm = _tile(n, 512, 8)
    steps = n // tm

    def body(x_ref, g_ref, t_ref, dx_ref, dg_ref, loss_ref, acc_ref, lacc_ref):
        i = pl.program_id(0)
        xv = x_ref[...]
        g = g_ref[...]
        r = lax.rsqrt(jnp.mean(xv * xv, axis=-1, keepdims=True) + EPS)
        xh = xv * r
        err = xh * g - t_ref[...]
        dy = err * (1.0 / d)
        dyg = dy * g
        mean = jnp.mean(dyg * xh, axis=-1, keepdims=True)
        dx_ref[...] = r * (dyg - xh * mean)
        part = jnp.sum((dy * xh).reshape(tm // SUBLANES, SUBLANES, d), axis=0)
        lpart = jnp.sum((err * err).reshape(tm // SUBLANES, SUBLANES, d), axis=0)

        @pl.when(i == 0)
        def _():
            acc_ref[...] = part
            lacc_ref[...] = lpart

        @pl.when(i > 0)
        def _():
            acc_ref[...] += part
            lacc_ref[...] += lpart

        @pl.when(i == steps - 1)
        def _():
            dg_ref[...] = jnp.sum(acc_ref[...], axis=0, keepdims=True)
            tot = jnp.sum(jnp.sum(lacc_ref[...], axis=0, keepdims=True), axis=1, keepdims=True)
            loss_ref[...] = jnp.broadcast_to(tot * (0.5 / d), loss_ref.shape)

    row = pl.BlockSpec((tm, d), lambda i: (i, 0))
    vec = pl.BlockSpec((1, d), lambda i: (0, 0))
    dx, dg, loss = pl.pallas_call(
        body, name="loss_head", grid=(steps,),
        in_specs=[row, vec, row],
        out_specs=[row, vec, pl.BlockSpec((1, LANES), lambda i: (0, 0))],
        out_shape=[jax.ShapeDtypeStruct((n, d), F32), jax.ShapeDtypeStruct((1, d), F32),
                   jax.ShapeDtypeStruct((1, LANES), F32)],
        scratch_shapes=[pltpu.VMEM((SUBLANES, d), F32), pltpu.VMEM((SUBLANES, d), F32)],
        compiler_params=_params("arbitrary"),
    )(x, gain.reshape(1, d), target)
    return dx, dg.reshape(d), loss[0, 0]


def _rms_rows(xv):
    return lax.rsqrt(jnp.mean(xv * xv, axis=-1, keepdims=True) + EPS)


def _ffn_in_fwd(x, gain, w_in, name, comm=None):
    n, d = x.shape
    f = w_in.shape[1] // 2
    tm = _tile(n, 256, 16)
    tn = _tile(f, 4096, LANES)
    nj = f // tn

    def body(x_ref, gain_ref, wg_ref, wu_ref, h_ref, t_ref, q_ref, a_ref):
        @pl.when(pl.program_id(1) == 0)
        def _():
            xv = x_ref[...]
            h_ref[...] = (xv * _rms_rows(xv) * gain_ref[...]).astype(h_ref.dtype)

        h = h_ref[...]
        g = jnp.dot(h, wg_ref[...], preferred_element_type=F32)
        u = jnp.dot(h, wu_ref[...], preferred_element_type=F32)
        s = _sigmoid(g)
        t = g * s
        t_ref[...] = t.astype(t_ref.dtype)
        q_ref[...] = (u * (s + t * (1.0 - s))).astype(q_ref.dtype)
        a_ref[...] = (t * u).astype(a_ref.dtype)

    row = pl.BlockSpec((tm, d), lambda i, j: (i, 0))
    tile = pl.BlockSpec((tm, tn), lambda i, j: (i, j))
    act = jax.ShapeDtypeStruct((n, f), BF16)
    outs, comm_outs = _call(
        body, name=name, grid=(n // tm, nj),
        in_specs=[row, pl.BlockSpec((1, d), lambda i, j: (0, 0)),
                  pl.BlockSpec((d, tn), lambda i, j: (0, j)), pl.BlockSpec((d, tn), lambda i, j: (0, j + nj))],
        out_specs=[row, tile, tile, tile],
        out_shape=[jax.ShapeDtypeStruct((n, d), BF16), act, act, act],
        semantics=("parallel", "arbitrary"), args=(x, gain.reshape(1, d), w_in, w_in), comm=comm)
    return outs if comm is None else (outs, comm_outs)


def _ffn_out_bwd(dout, w_out, t, q, name, comm=None):
    n, d = dout.shape
    f = w_out.shape[0]
    tm = _tile(n, 256, 16)
    tn = _tile(f, 4096, LANES)

    def body(d_ref, w_ref, t_ref, q_ref, dg_ref, du_ref):
        da = 0.5 * lax.dot_general(d_ref[...].astype(BF16), w_ref[...], (((1,), (1,)), ((), ())),
                                   preferred_element_type=F32)
        dg_ref[...] = (da * q_ref[...].astype(F32)).astype(dg_ref.dtype)
        du_ref[...] = (da * t_ref[...].astype(F32)).astype(du_ref.dtype)

    tile = pl.BlockSpec((tm, tn), lambda i, j: (i, j))
    act = jax.ShapeDtypeStruct((n, f), BF16)
    outs, comm_outs = _call(
        body, name=name, grid=(n // tm, f // tn),
        in_specs=[pl.BlockSpec((tm, d), lambda i, j: (i, 0)), pl.BlockSpec((tn, d), lambda i, j: (j, 0)), tile, tile],
        out_specs=[tile, tile], out_shape=[act, act],
        semantics=("parallel", "parallel"), args=(dout, w_out, t, q), comm=comm)
    return outs if comm is None else (outs, comm_outs)


def _proj_in_bwd(parts, w, x, gain, dres, name, comm=None):
    n, d = x.shape
    tm = _tile(n, 256, 8)
    steps = n // tm
    np_ = len(parts)
    offs = [off for _, off in parts]
    widths = [a.shape[1] for a, _ in parts]

    def body(*refs):
        a_refs = refs[:np_]
        w_ref, x_ref, g_ref, dr_ref, dx_ref, dg_ref, acc_ref = refs[np_:]
        i = pl.program_id(0)
        dh = None
        for a_ref, off, kp in zip(a_refs, offs, widths):
            part = lax.dot_general(a_ref[...].astype(BF16), w_ref[:, off:off + kp], (((1,), (1,)), ((), ())),
                                   preferred_element_type=F32)
            dh = part if dh is None else dh + part
        xv = x_ref[...]
        r = _rms_rows(xv)
        xh = xv * r
        dyg = dh * g_ref[...]
        mean = jnp.mean(dyg * xh, axis=-1, keepdims=True)
        dx_ref[...] = dr_ref[...] + r * (dyg - xh * mean)
        part = jnp.sum((dh * xh).reshape(tm // SUBLANES, SUBLANES, d), axis=0)

        @pl.when(i == 0)
        def _():
            acc_ref[...] = part

        @pl.when(i > 0)
        def _():
            acc_ref[...] += part

        @pl.when(i == steps - 1)
        def _():
            dg_ref[...] = jnp.sum(acc_ref[...], axis=0, keepdims=True)

    row = pl.BlockSpec((tm, d), lambda i: (i, 0))
    vec = pl.BlockSpec((1, d), lambda i: (0, 0))
    (dx, dg), comm_outs = _call(
        body, name=name, grid=(steps,),
        in_specs=[pl.BlockSpec((tm, kp), lambda i: (i, 0)) for kp in widths]
        + [pl.BlockSpec(w.shape, lambda i: (0, 0)), row, vec, row],
        out_specs=[row, vec],
        out_shape=[jax.ShapeDtypeStruct((n, d), F32), jax.ShapeDtypeStruct((1, d), F32)],
        scratch_shapes=[pltpu.VMEM((SUBLANES, d), F32)],
        semantics=("arbitrary",), args=(*[a for a, _ in parts], w, x, gain.reshape(1, d), dres), comm=comm)
    return (dx, dg.reshape(d)) if comm is None else (dx, dg.reshape(d), comm_outs)


def _mix_in_fwd(x, gain, w, width, name, comm=None):
    n, d = x.shape
    cols = w.shape[1]
    tm = _tile(n, 512, 16)

    def body(x_ref, gain_ref, w_ref, h_ref, u_ref, z_ref):
        xv = x_ref[...]
        h = (xv * _rms_rows(xv) * gain_ref[...]).astype(h_ref.dtype)
        h_ref[...] = h
        z = jnp.dot(h, w_ref[...], preferred_element_type=F32)
        u_ref[...] = z[:, 0:width]
        z_ref[...] = z[:, width:cols]

    row = pl.BlockSpec((tm, d), lambda i: (i, 0))
    outs, comm_outs = _call(
        body, name=name, grid=(n // tm,),
        in_specs=[row, pl.BlockSpec((1, d), lambda i: (0, 0)), pl.BlockSpec((d, cols), lambda i: (0, 0))],
        out_specs=[row, pl.BlockSpec((tm, width), lambda i: (i, 0)), pl.BlockSpec((tm, cols - width), lambda i: (i, 0))],
        out_shape=[jax.ShapeDtypeStruct((n, d), BF16), jax.ShapeDtypeStruct((n, width), F32),
                   jax.ShapeDtypeStruct((n, cols - width), F32)],
        semantics=("parallel",), args=(x, gain.reshape(1, d), w), comm=comm)
    return outs if comm is None else (*outs, comm_outs)


def _tril_mask():
    t = lax.broadcasted_iota(jnp.int32, (GM_CHUNK, GM_CHUNK), 0)
    s = lax.broadcasted_iota(jnp.int32, (GM_CHUNK, GM_CHUNK), 1)
    return s <= t


def _gmlp_fwd(zgm, v_gain, w_s, bias_tile, name):
    n, w2 = zgm.shape
    w = w2 // 2
    heads = w // GM_HEAD_DIM
    tm = _tile(n, 512, GM_CHUNK)
    nq = tm // GM_CHUNK

    def body(u_ref, v_ref, gain_ref, w_ref, b_ref, o_ref):
        mask = _tril_mask()
        ug = _gelu(u_ref[...])
        vg = _gelu(v_ref[...])
        for h in range(heads):
            cols = slice(h * GM_HEAD_DIM, (h + 1) * GM_HEAD_DIM)
            vh = vg[:, cols]
            r = lax.rsqrt(jnp.mean(vh * vh, axis=-1, keepdims=True) + EPS)
            vn = (vh * r * gain_ref[:, cols]).astype(BF16)
            wm = jnp.where(mask, w_ref[h], 0.0).astype(BF16)
            for q in range(nq):
                rows = slice(q * GM_CHUNK, (q + 1) * GM_CHUNK)
                s = jnp.dot(wm, vn[rows], preferred_element_type=F32) + b_ref[:, cols]
                o_ref[rows, cols] = ug[rows, cols] * s

    return pl.pallas_call(
        body, name=name, grid=(n // tm,),
        in_specs=[pl.BlockSpec((tm, w), lambda i: (i, 0)), pl.BlockSpec((tm, w), lambda i: (i, 1)),
                  pl.BlockSpec((1, w), lambda i: (0, 0)),
                  pl.BlockSpec((heads, GM_CHUNK, GM_CHUNK), lambda i: (0, 0, 0)),
                  pl.BlockSpec((GM_CHUNK, w), lambda i: (0, 0))],
        out_specs=pl.BlockSpec((tm, w), lambda i: (i, 0)),
        out_shape=jax.ShapeDtypeStruct((n, w), F32),
        compiler_params=_params("parallel"),
    )(zgm, zgm, v_gain.reshape(1, w), w_s, bias_tile)


def _gmlp_bwd(zgm, dy, v_gain, w_s, bias_tile, name):
    n, w2 = zgm.shape
    w = w2 // 2
    heads = w // GM_HEAD_DIM
    tm = _tile(n, 512, GM_CHUNK)
    nq = tm // GM_CHUNK
    steps = n // tm

    def body(z_ref, dy_ref, gain_ref, w_ref, b_ref, dz_ref, dw_ref, db_ref, dgain_ref):
        i = pl.program_id(0)
        mask = _tril_mask()

        @pl.when(i == 0)
        def _():
            dw_ref[...] = jnp.zeros_like(dw_ref)
            db_ref[...] = jnp.zeros_like(db_ref)
            dgain_ref[...] = jnp.zeros_like(dgain_ref)

        ug, dug_du = _gelu_and_grad(z_ref[:, 0:w])
        vg, dvg_dv = _gelu_and_grad(z_ref[:, w:w2])
        dyv = dy_ref[...]
        for h in range(heads):
            cols = slice(h * GM_HEAD_DIM, (h + 1) * GM_HEAD_DIM)
            vh = vg[:, cols]
            r = lax.rsqrt(jnp.mean(vh * vh, axis=-1, keepdims=True) + EPS)
            vhat = vh * r
            gain = gain_ref[:, cols]
            vn = (vhat * gain).astype(BF16)
            wm = jnp.where(mask, w_ref[h], 0.0).astype(BF16)
            dvn_parts = []
            for q in range(nq):
                rows = slice(q * GM_CHUNK, (q + 1) * GM_CHUNK)
                s = jnp.dot(wm, vn[rows], preferred_element_type=F32) + b_ref[:, cols]
                dyq = dyv[rows, cols]
                dz_ref[rows, cols] = (dyq * s * dug_du[rows, cols]).astype(dz_ref.dtype)
                ds = dyq * ug[rows, cols]
                db_ref[:, cols] += ds
                dsb = ds.astype(BF16)
                dw_ref[h] += lax.dot_general(dsb, vn[rows], (((1,), (1,)), ((), ())), preferred_element_type=F32)
                dvn_parts.append(lax.dot_general(wm, dsb, (((0,), (0,)), ((), ())), preferred_element_type=F32))
            dvn = jnp.concatenate(dvn_parts, axis=0) if nq > 1 else dvn_parts[0]
            dgain_ref[:, cols] += jnp.sum(dvn * vhat, axis=0, keepdims=True)
            dvhat = dvn * gain
            mean = jnp.mean(dvhat * vhat, axis=-1, keepdims=True)
            dz_ref[:, w + h * GM_HEAD_DIM:w + (h + 1) * GM_HEAD_DIM] = (
                r * (dvhat - vhat * mean) * dvg_dv[:, cols]).astype(dz_ref.dtype)

        @pl.when(i == steps - 1)
        def _():
            for h in range(heads):
                dw_ref[h] = jnp.where(mask, dw_ref[h], 0.0)

    dz, dw, db, dgain = pl.pallas_call(
        body, name=name, grid=(steps,),
        in_specs=[pl.BlockSpec((tm, w2), lambda i: (i, 0)), pl.BlockSpec((tm, w), lambda i: (i, 0)),
                  pl.BlockSpec((1, w), lambda i: (0, 0)),
                  pl.BlockSpec((heads, GM_CHUNK, GM_CHUNK), lambda i: (0, 0, 0)),
                  pl.BlockSpec((GM_CHUNK, w), lambda i: (0, 0))],
        out_specs=[pl.BlockSpec((tm, w2), lambda i: (i, 0)),
                   pl.BlockSpec((heads, GM_CHUNK, GM_CHUNK), lambda i: (0, 0, 0)),
                   pl.BlockSpec((GM_CHUNK, w), lambda i: (0, 0)),
                   pl.BlockSpec((1, w), lambda i: (0, 0))],
        out_shape=[jax.ShapeDtypeStruct((n, w2), BF16), jax.ShapeDtypeStruct((heads, GM_CHUNK, GM_CHUNK), F32),
                   jax.ShapeDtypeStruct((GM_CHUNK, w), F32), jax.ShapeDtypeStruct((1, w), F32)],
        compiler_params=_params("arbitrary"),
    )(zgm, dy, v_gain.reshape(1, w), w_s, bias_tile)
    return dz, dw, db, dgain.reshape(w)


def _mix_out_fwd(y_ssm, y_gm, g1, g2, w_out, x, name, comm=None):
    n, w = y_ssm.shape
    d = w_out.shape[1]
    tm = _tile(n, 512, 16)

    def body(a_ref, b_ref, g1_ref, g2_ref, w_ref, x_ref, ycat_ref, o_ref):
        for src, g_ref, lo in ((a_ref, g1_ref, 0), (b_ref, g2_ref, w)):
            v = src[...]
            ycat_ref[:, lo:lo + w] = (v * _rms_rows(v) * g_ref[...]).astype(ycat_ref.dtype)
        o_ref[...] = x_ref[...] + jnp.dot(ycat_ref[...], w_ref[...], preferred_element_type=F32)

    row = pl.BlockSpec((tm, w), lambda i: (i, 0))
    vec = pl.BlockSpec((1, w), lambda i: (0, 0))
    outs, comm_outs = _call(
        body, name=name, grid=(n // tm,),
        in_specs=[row, row, vec, vec, pl.BlockSpec((2 * w, d), lambda i: (0, 0)), pl.BlockSpec((tm, d), lambda i: (i, 0))],
        out_specs=[pl.BlockSpec((tm, 2 * w), lambda i: (i, 0)), pl.BlockSpec((tm, d), lambda i: (i, 0))],
        out_shape=[jax.ShapeDtypeStruct((n, 2 * w), BF16), jax.ShapeDtypeStruct((n, d), F32)],
        semantics=("parallel",), args=(y_ssm, y_gm, g1.reshape(1, w), g2.reshape(1, w), w_out, x), comm=comm)
    return outs if comm is None else (*outs, comm_outs)


def _mix_out_bwd(dx, w_out, y_ssm, y_gm, g1, g2, name, comm=None):
    n, w = y_ssm.shape
    d = w_out.shape[1]
    tm = _tile(n, 512, 8)
    steps = n // tm

    def body(dx_ref, w_ref, a_ref, b_ref, g1_ref, g2_ref, da_ref, db_ref, dg1_ref, dg2_ref):
        i = pl.program_id(0)

        @pl.when(i == 0)
        def _():
            dg1_ref[...] = jnp.zeros_like(dg1_ref)
            dg2_ref[...] = jnp.zeros_like(dg2_ref)

        dycat = lax.dot_general(dx_ref[...].astype(BF16), w_ref[...], (((1,), (1,)), ((), ())),
                                preferred_element_type=F32)
        for src, g_ref, lo, dst, dg_ref in ((a_ref, g1_ref, 0, da_ref, dg1_ref), (b_ref, g2_ref, w, db_ref, dg2_ref)):
            v = src[...]
            dh = dycat[:, lo:lo + w]
            r = _rms_rows(v)
            vh = v * r
            dyg = dh * g_ref[...]
            mean = jnp.mean(dyg * vh, axis=-1, keepdims=True)
            dst[...] = r * (dyg - vh * mean)
            dg_ref[...] += jnp.sum(dh * vh, axis=0, keepdims=True)

    row = pl.BlockSpec((tm, w), lambda i: (i, 0))
    vec = pl.BlockSpec((1, w), lambda i: (0, 0))
    (da, db, dg1, dg2), comm_outs = _call(
        body, name=name, grid=(steps,),
        in_specs=[pl.BlockSpec((tm, d), lambda i: (i, 0)), pl.BlockSpec((2 * w, d), lambda i: (0, 0)), row, row, vec, vec],
        out_specs=[row, row, vec, vec],
        out_shape=[jax.ShapeDtypeStruct((n, w), F32), jax.ShapeDtypeStruct((n, w), F32),
                   jax.ShapeDtypeStruct((1, w), F32), jax.ShapeDtypeStruct((1, w), F32)],
        semantics=("arbitrary",), args=(dx, w_out, y_ssm, y_gm, g1.reshape(1, w), g2.reshape(1, w)), comm=comm)
    res = (da, db, dg1.reshape(w), dg2.reshape(w))
    return res if comm is None else (*res, comm_outs)


def _discretise(a_re, a_im, log_dt, bt_re, bt_im):
    dt = jnp.exp(log_dt)
    e = jnp.exp(a_re * dt)
    ang = a_im * dt
    lr = e * jnp.cos(ang)
    li = e * jnp.sin(ang)
    den = a_re * a_re + a_im * a_im
    cr = ((lr - 1.0) * a_re + li * a_im) / den
    ci = (li * a_re - (lr - 1.0) * a_im) / den
    cr3 = cr[:, None, :]
    ci3 = ci[:, None, :]
    return lr, li, cr3 * bt_re - ci3 * bt_im, cr3 * bt_im + ci3 * bt_re


def _disc_fwd(a_re, a_im, log_dt, bt_re, bt_im):
    g, p = a_re.shape
    c = bt_re.shape[1]

    def body(are_ref, aim_ref, ldt_ref, bre_ref, bim_ref, lr_ref, li_ref, bbr_ref, bbi_ref):
        lr, li, bbr, bbi = _discretise(are_ref[...], aim_ref[...], ldt_ref[...], bre_ref[...], bim_ref[...])
        lr_ref[...] = lr
        li_ref[...] = li
        bbr_ref[...] = bbr
        bbi_ref[...] = bbi

    return pl.pallas_call(
        body, name="s5_discretise",
        out_shape=[jax.ShapeDtypeStruct((g, p), F32), jax.ShapeDtypeStruct((g, p), F32),
                   jax.ShapeDtypeStruct((g, c, p), F32), jax.ShapeDtypeStruct((g, c, p), F32)],
    )(a_re, a_im, log_dt, bt_re, bt_im)


def _disc_bwd(a_re, a_im, log_dt, bt_re, bt_im, dlr, dli, dbbr, dbbi):
    g, p = a_re.shape
    c = bt_re.shape[1]

    def body(are_ref, aim_ref, ldt_ref, bre_ref, bim_ref, dlr_ref, dli_ref, dbbr_ref, dbbi_ref,
             dare_ref, daim_ref, dldt_ref, dbre_ref, dbim_ref):
        _, vjp = jax.vjp(_discretise, are_ref[...], aim_ref[...], ldt_ref[...], bre_ref[...], bim_ref[...])
        dare, daim, dldt, dbre, dbim = vjp((dlr_ref[...], dli_ref[...], dbbr_ref[...], dbbi_ref[...]))
        dare_ref[...] = dare
        daim_ref[...] = daim
        dldt_ref[...] = dldt
        dbre_ref[...] = dbre
        dbim_ref[...] = dbim

    return pl.pallas_call(
        body, name="s5_discretise_bwd",
        out_shape=[jax.ShapeDtypeStruct((g, p), F32), jax.ShapeDtypeStruct((g, p), F32),
                   jax.ShapeDtypeStruct((g, 1), F32),
                   jax.ShapeDtypeStruct((g, c, p), F32), jax.ShapeDtypeStruct((g, c, p), F32)],
    )(a_re, a_im, log_dt, bt_re, bt_im, dlr, dli, dbbr, dbbi)


def _block_diag(w, nb):
    g, a, b = w.shape
    gpb = g // nb
    eye = jnp.eye(gpb, dtype=w.dtype)
    w4 = w.reshape(nb, gpb, a, b)
    return jnp.einsum("ngab,gh->ngahb", w4, eye).reshape(nb, gpb * a, gpb * b)


def _block_diag_extract(m, gpb):
    nb, ga, gb = m.shape
    a, b = ga // gpb, gb // gpb
    m5 = m.reshape(nb, gpb, a, gpb, b)
    idx = jnp.arange(gpb)
    return m5[:, idx, :, idx, :].transpose(1, 0, 2, 3).reshape(nb * gpb, a, b)


def _ssm_operands(lr, li, bbr, bbi, c_re, c_im, d_skip, glu_w, glu_b):
    g = lr.shape[0]
    nb = g // GROUPS_PER_BLOCK
    s = STATES_PER_BLOCK
    lam = jnp.concatenate([lr.reshape(nb, 1, s), li.reshape(nb, 1, s)], axis=-1)
    b_bd = jnp.concatenate([_block_diag(bbr, nb), _block_diag(bbi, nb)], axis=-1)
    ct_re = jnp.swapaxes(c_re, 1, 2)
    ct_im = jnp.swapaxes(c_im, 1, 2)
    c_bd = jnp.concatenate([_block_diag(ct_re, nb), -_block_diag(ct_im, nb)], axis=1)
    dsk = d_skip.reshape(nb, 1, LANES)
    w_bd = jnp.concatenate([_block_diag(glu_w[:, :, :SSM_CH], nb), _block_diag(glu_w[:, :, SSM_CH:], nb)], axis=-1)
    bias = jnp.concatenate([glu_b[:, :SSM_CH].reshape(nb, 1, LANES), glu_b[:, SSM_CH:].reshape(nb, 1, LANES)], axis=-1)
    return lam, b_bd.astype(BF16), c_bd.astype(BF16), dsk, w_bd.astype(BF16), bias


def _roll_rows(v, shift):
    return v if shift % SUBLANES == 0 else pltpu.roll(v, shift % SUBLANES, 0)


def _scan_chunk_rows(seq, nseq):
    return _tile(seq, max(8 * SSM_TIME_CHUNK // nseq, 8), max(SUBLANES // nseq, 1) * 8)


def _stage_lams(lam_ref, e, nseq):
    s = STATES_PER_BLOCK
    lr = jnp.broadcast_to(lam_ref[e, :, 0:s], (SUBLANES, s))
    li = jnp.broadcast_to(lam_ref[e, :, s:2 * s], (SUBLANES, s))
    if nseq == SUBLANES:
        return [(lr, li)]
    row = lax.broadcasted_iota(jnp.int32, (SUBLANES, s), 0)
    out = []
    for j in range(SUBLANES // nseq):
        mine = jnp.logical_and(row >= j * nseq, row < (j + 1) * nseq)
        out.append((jnp.where(mine, lr, 0.0), jnp.where(mine, li, 0.0)))
    return out


def _scan_with(nblk, step, carry, between):
    runs = len(between)
    per = nblk // runs
    for i in range(runs):
        hi = nblk if i == runs - 1 else (i + 1) * per
        carry = lax.fori_loop(i * per, hi, step, carry, unroll=True)
        between[i]()
    return carry


def _ssm_fwd_pair(u, ops, nseq, name, comm=None):
    lam, b_bd, c_bd, dsk, w_bd, bias = ops
    rows_total, w = u.shape
    seq = rows_total // nseq
    nb = w // LANES
    s = STATES_PER_BLOCK
    tc = _scan_chunk_rows(seq, nseq)
    nk = seq // tc
    rows = tc * nseq
    nblk = rows // SUBLANES
    stages = SUBLANES // nseq
    two = 2 * LANES
    ncol = 4

    def body(u_ref, lam_ref, b_ref, c_ref, d_ref, w_ref, bias_ref, y_ref, hb_ref, buf_a, buf_b, st, rbuf_a, rbuf_b):
        k = pl.program_id(1)

        @pl.when(k == 0)
        def _():
            st[...] = jnp.zeros_like(st)

        hb_ref[...] = st[...]
        rbufs = (rbuf_a, rbuf_b)
        for q in range(nseq):
            for e in range(2):
                rbufs[e][pl.ds(q, tc, stride=nseq), :] = u_ref[q, :, e * LANES:(e + 1) * LANES]
        bufs = (buf_a, buf_b)

        def u_of(e):
            return rbufs[e][...]

        def project_in(e, j):
            cols = slice(j * (2 * s // ncol), (j + 1) * (2 * s // ncol))
            bufs[e][:, cols] = jnp.dot(u_of(e).astype(BF16), b_ref[e, :, cols], preferred_element_type=F32)

        def scan(e, between):
            buf = bufs[e]
            lams = _stage_lams(lam_ref, e, nseq)

            def step(i, carry):
                pr, pi = carry
                r0 = pl.multiple_of(i * SUBLANES, SUBLANES)
                outr = buf[pl.ds(r0, SUBLANES), 0:s]
                outi = buf[pl.ds(r0, SUBLANES), s:2 * s]
                for lr, li in lams:
                    rr = _roll_rows(pr, nseq)
                    ri = _roll_rows(pi, nseq)
                    outr = outr + (lr * rr - li * ri)
                    outi = outi + (lr * ri + li * rr)
                    pr, pi = outr, outi
                buf[pl.ds(r0, SUBLANES), 0:s] = outr
                buf[pl.ds(r0, SUBLANES), s:2 * s] = outi
                return outr, outi

            lo = e * 2 * s
            hr, hi = _scan_with(nblk, step, (st[:, lo:lo + s], st[:, lo + s:lo + 2 * s]), between)
            st[:, lo:lo + s] = hr
            st[:, lo + s:lo + 2 * s] = hi

        part = {}

        def project_out(e, j):
            ks = slice(j * (2 * s // ncol), (j + 1) * (2 * s // ncol))
            p = jnp.dot(bufs[e][:, ks].astype(BF16), c_ref[e, ks, :], preferred_element_type=F32)
            part[e] = p if j == 0 else part[e] + p

        def finish(e):
            y = part[e] + d_ref[e] * u_of(e)
            z = jnp.dot(_gelu(y).astype(BF16), w_ref[e], preferred_element_type=F32) + bias_ref[e]
            part[e] = z[:, 0:LANES] * _sigmoid(z[:, LANES:two])

        for j in range(ncol):
            project_in(0, j)
        scan(0, [functools.partial(project_in, 1, j) for j in range(ncol)])
        scan(1, [functools.partial(project_out, 0, j) for j in range(ncol)] + [functools.partial(finish, 0)])
        for j in range(ncol):
            project_out(1, j)
        finish(1)
        for e in range(2):
            rbufs[e][...] = part[e]
            for q in range(nseq):
                y_ref[q, :, e * LANES:(e + 1) * LANES] = rbufs[e][pl.ds(q, tc, stride=nseq), :]

    blk = lambda shape: pl.BlockSpec(shape, lambda b, k: (b, 0, 0))
    tok = pl.BlockSpec((nseq, tc, two), lambda b, k: (0, k, b))
    (y, hb), comm_outs = _call(
        body, name=name, grid=(nb // 2, nk),
        in_specs=[tok, blk((2, 1, 2 * s)), blk((2, LANES, 2 * s)), blk((2, 2 * s, LANES)),
                  blk((2, 1, LANES)), blk((2, LANES, two)), blk((2, 1, two))],
        out_specs=[tok, pl.BlockSpec((SUBLANES, 4 * s), lambda b, k: (k, b))],
        out_shape=[jax.ShapeDtypeStruct((nseq, seq, w), F32),
                   jax.ShapeDtypeStruct((nk * SUBLANES, nb * 2 * s), F32)],
        scratch_shapes=[pltpu.VMEM((rows, 2 * s), F32), pltpu.VMEM((rows, 2 * s), F32),
                        pltpu.VMEM((SUBLANES, 4 * s), F32), pltpu.VMEM((rows, LANES), F32),
                        pltpu.VMEM((rows, LANES), F32)],
        semantics=("parallel", "arbitrary"),
        args=(u.reshape(nseq, seq, w), lam, b_bd, c_bd, dsk, w_bd, bias), comm=comm)
    y = y.reshape(nseq * seq, w)
    return (y, hb) if comm is None else (y, hb, comm_outs)


def _ssm_bwd_pair(u, dout, hb, ops, nseq, name, comm=None):
    lam, b_bd, c_bd, dsk, w_bd, bias = ops
    rows_total, w = u.shape
    seq = rows_total // nseq
    nb = w // LANES
    s = STATES_PER_BLOCK
    tc = _scan_chunk_rows(seq, nseq)
    nk = seq // tc
    rows = tc * nseq
    nblk = rows // SUBLANES
    stages = SUBLANES // nseq
    two = 2 * LANES
    ncol = 4
    cw = 2 * s // ncol
    tn_dims = (((0,), (0,)), ((), ()))
    nt_dims = (((1,), (1,)), ((), ()))

    def body(u_ref, dy_ref, hb_ref, lam_ref, b_ref, c_ref, d_ref, w_ref, bias_ref,
             du_ref, dlam_ref, db_ref, dct_ref, dd_ref, dw_ref, dbias_ref,
             hbuf_a, hbuf_b, gbuf_a, gbuf_b, gst, lacc, ru_a, ru_b, rd_a, rd_b):
        k = pl.program_id(1)

        @pl.when(k == 0)
        def _():
            gst[...] = jnp.zeros_like(gst)
            lacc[...] = jnp.zeros_like(lacc)
            db_ref[...] = jnp.zeros_like(db_ref)
            dct_ref[...] = jnp.zeros_like(dct_ref)
            dd_ref[...] = jnp.zeros_like(dd_ref)
            dw_ref[...] = jnp.zeros_like(dw_ref)
            dbias_ref[...] = jnp.zeros_like(dbias_ref)

        hbufs, gbufs, rus, rds = (hbuf_a, hbuf_b), (gbuf_a, gbuf_b), (ru_a, ru_b), (rd_a, rd_b)
        for q in range(nseq):
            for e in range(2):
                rus[e][pl.ds(q, tc, stride=nseq), :] = u_ref[q, :, e * LANES:(e + 1) * LANES]
                rds[e][pl.ds(q, tc, stride=nseq), :] = dy_ref[q, :, e * LANES:(e + 1) * LANES]
        row = lax.broadcasted_iota(jnp.int32, (SUBLANES, s), 0)
        cols = [slice(j * cw, (j + 1) * cw) for j in range(ncol)]
        val = [{}, {}]

        def project_in(e, j):
            hbufs[e][:, cols[j]] = jnp.dot(rus[e][...].astype(BF16), b_ref[e, :, cols[j]], preferred_element_type=F32)

        def scan_fwd(e, between):
            buf = hbufs[e]
            lams = _stage_lams(lam_ref, e, nseq)

            def step(i, carry):
                pr, pi = carry
                r0 = pl.multiple_of(i * SUBLANES, SUBLANES)
                outr = buf[pl.ds(r0, SUBLANES), 0:s]
                outi = buf[pl.ds(r0, SUBLANES), s:2 * s]
                for lr, li in lams:
                    rr = _roll_rows(pr, nseq)
                    ri = _roll_rows(pi, nseq)
                    outr = outr + (lr * rr - li * ri)
                    outi = outi + (lr * ri + li * rr)
                    pr, pi = outr, outi
                buf[pl.ds(r0, SUBLANES), 0:s] = outr
                buf[pl.ds(r0, SUBLANES), s:2 * s] = outi
                return outr, outi

            lo = e * 2 * s
            _scan_with(nblk, step, (hb_ref[:, lo:lo + s], hb_ref[:, lo + s:lo + 2 * s]), between)

        def y_part(e, j):
            p = jnp.dot(hbufs[e][:, cols[j]].astype(BF16), c_ref[e, cols[j], :], preferred_element_type=F32)
            val[e]["y"] = p if j == 0 else val[e]["y"] + p

        def gate(e):
            v = val[e]
            uu = rus[e][...]
            yg, dyg_dy = _gelu_and_grad(v.pop("y") + d_ref[e] * uu)
            yg16 = yg.astype(BF16)
            z = jnp.dot(yg16, w_ref[e], preferred_element_type=F32) + bias_ref[e]
            sg = _sigmoid(z[:, LANES:two])
            dout_e = rds[e][...]
            dz = jnp.concatenate([dout_e * sg, dout_e * z[:, 0:LANES] * sg * (1.0 - sg)], axis=-1)
            dz16 = dz.astype(BF16)
            dw_ref[e] += lax.dot_general(yg16, dz16, tn_dims, preferred_element_type=F32)
            dbias_ref[e] += jnp.sum(dz, axis=0, keepdims=True)
            dy = lax.dot_general(dz16, w_ref[e], nt_dims, preferred_element_type=F32) * dyg_dy
            dd_ref[e] += jnp.sum(dy * uu, axis=0, keepdims=True)
            v["dy"] = dy
            v["dy16"] = dy.astype(BF16)

        def dc_part(e, j):
            dct_ref[e, :, cols[j]] += lax.dot_general(val[e]["dy16"], hbufs[e][:, cols[j]].astype(BF16), tn_dims,
                                                      preferred_element_type=F32)

        def dh_part(e, j):
            gbufs[e][:, cols[j]] = lax.dot_general(val[e]["dy16"], c_ref[e, cols[j], :], nt_dims,
                                                   preferred_element_type=F32)

        def scan_bwd(e, between):
            hbuf, gbuf = hbufs[e], gbufs[e]
            lams = _stage_lams(lam_ref, e, nseq)
            lo = e * 2 * s

            def step(i, carry):
                pr, pi, ar, ai = carry
                blk = nblk - 1 - i
                r0 = pl.multiple_of(blk * SUBLANES, SUBLANES)
                outr = gbuf[pl.ds(r0, SUBLANES), 0:s]
                outi = gbuf[pl.ds(r0, SUBLANES), s:2 * s]
                for lr, li in reversed(lams):
                    rr = _roll_rows(pr, SUBLANES - nseq)
                    ri = _roll_rows(pi, SUBLANES - nseq)
                    outr = outr + (lr * rr + li * ri)
                    outi = outi + (lr * ri - li * rr)
                    pr, pi = outr, outi
                gbuf[pl.ds(r0, SUBLANES), 0:s] = outr
                gbuf[pl.ds(r0, SUBLANES), s:2 * s] = outi
                p0 = pl.multiple_of(jnp.maximum(blk - 1, 0) * SUBLANES, SUBLANES)
                first = blk == 0
                before_r = jnp.where(first, hb_ref[:, lo:lo + s], hbuf[pl.ds(p0, SUBLANES), 0:s])
                before_i = jnp.where(first, hb_ref[:, lo + s:lo + 2 * s], hbuf[pl.ds(p0, SUBLANES), s:2 * s])
                if stages > 1:
                    last_rows = row >= SUBLANES - nseq
                    before_r = _roll_rows(jnp.where(last_rows, before_r, hbuf[pl.ds(r0, SUBLANES), 0:s]), nseq)
                    before_i = _roll_rows(jnp.where(last_rows, before_i, hbuf[pl.ds(r0, SUBLANES), s:2 * s]), nseq)
                return (outr, outi, ar + outr * before_r + outi * before_i, ai - outr * before_i + outi * before_r)

            gr, gi, ar, ai = _scan_with(
                nblk, step, (gst[:, lo:lo + s], gst[:, lo + s:lo + 2 * s], lacc[:, lo:lo + s], lacc[:, lo + s:lo + 2 * s]),
                between)
            gst[:, lo:lo + s] = gr
            gst[:, lo + s:lo + 2 * s] = gi
            lacc[:, lo:lo + s] = ar
            lacc[:, lo + s:lo + 2 * s] = ai

        def du_part(e, j):
            p = lax.dot_general(gbufs[e][:, cols[j]].astype(BF16), b_ref[e, :, cols[j]], nt_dims,
                                preferred_element_type=F32)
            val[e]["du"] = (val[e].pop("dy") * d_ref[e] + p) if j == 0 else val[e]["du"] + p

        def db_part(e, j):
            db_ref[e, :, cols[j]] += lax.dot_general(rus[e][...].astype(BF16), gbufs[e][:, cols[j]].astype(BF16),
                                                     tn_dims, preferred_element_type=F32)

        def parts(fn, e):
            return [functools.partial(fn, e, j) for j in range(ncol)]

        middle_of = lambda e: parts(y_part, e) + [functools.partial(gate, e)] + parts(dc_part, e) + parts(dh_part, e)
        last_of = lambda e: parts(du_part, e) + parts(db_part, e)
        for piece in parts(project_in, 0):
            piece()
        scan_fwd(0, parts(project_in, 1))
        scan_fwd(1, middle_of(0))
        scan_bwd(0, middle_of(1))
        scan_bwd(1, last_of(0))
        for piece in last_of(1):
            piece()
        for e in range(2):
            rus[e][...] = val[e]["du"]
            for q in range(nseq):
                du_ref[q, :, e * LANES:(e + 1) * LANES] = rus[e][pl.ds(q, tc, stride=nseq), :].astype(du_ref.dtype)

        @pl.when(k == nk - 1)
        def _():
            for e in range(2):
                dlam_ref[e] = jnp.sum(lacc[:, e * 2 * s:(e + 1) * 2 * s], axis=0, keepdims=True)

    blk = lambda shape: pl.BlockSpec(shape, lambda b, k: (b, 0, 0))
    tok = pl.BlockSpec((nseq, tc, two), lambda b, k: (0, nk - 1 - k, b))
    outs, comm_outs = _call(
        body, name=name, grid=(nb // 2, nk),
        in_specs=[tok, tok, pl.BlockSpec((SUBLANES, 4 * s), lambda b, k: (nk - 1 - k, b)),
                  blk((2, 1, 2 * s)), blk((2, LANES, 2 * s)), blk((2, 2 * s, LANES)),
                  blk((2, 1, LANES)), blk((2, LANES, two)), blk((2, 1, two))],
        out_specs=[tok, blk((2, 1, 2 * s)), blk((2, LANES, 2 * s)), blk((2, LANES, 2 * s)),
                   blk((2, 1, LANES)), blk((2, LANES, two)), blk((2, 1, two))],
        out_shape=[jax.ShapeDtypeStruct((nseq, seq, w), BF16),
                   jax.ShapeDtypeStruct((nb, 1, 2 * s), F32), jax.ShapeDtypeStruct((nb, LANES, 2 * s), F32),
                   jax.ShapeDtypeStruct((nb, LANES, 2 * s), F32), jax.ShapeDtypeStruct((nb, 1, LANES), F32),
                   jax.ShapeDtypeStruct((nb, LANES, two), F32), jax.ShapeDtypeStruct((nb, 1, two), F32)],
        scratch_shapes=[pltpu.VMEM((rows, 2 * s), F32)] * 4
        + [pltpu.VMEM((SUBLANES, 4 * s), F32), pltpu.VMEM((SUBLANES, 4 * s), F32)]
        + [pltpu.VMEM((rows, LANES), F32)] * 4,
        semantics=("parallel", "arbitrary"),
        args=(u.reshape(nseq, seq, w), dout.reshape(nseq, seq, w), hb, lam, b_bd, c_bd, dsk, w_bd, bias), comm=comm)
    outs[0] = outs[0].reshape(nseq * seq, w)
    return outs if comm is None else (outs, comm_outs)


ANY = pl.BlockSpec(memory_space=pl.ANY)

BIG = (("ffn1_w_in", True), ("ffn1_w_out", False), ("mix_w_in", True), ("mix_w_out", False),
       ("ffn2_w_in", True), ("ffn2_w_out", False))


def _my_place():
    return lax.axis_index("x"), lax.axis_index("y"), lax.axis_index("c")


def _other_chips(x, y):
    return [(1 - x, y), (x, 1 - y), (1 - x, 1 - y)]


def _half_of_shard(ref, col_sharded, chip, core):
    full_rows, full_cols = ref.shape
    if col_sharded:
        hr, cs = full_rows // 2, full_cols // N_CHIPS
        return ref.at[pl.ds(pl.multiple_of(core * hr, 8), hr), pl.ds(chip * cs, cs)]
    rs = full_rows // N_CHIPS
    return ref.at[pl.ds(pl.multiple_of(chip * rs + core * (rs // 2), 8), rs // 2), :]


def _gather_comm(shards, cols):
    full_shapes = [(sh.shape[0], sh.shape[1] * N_CHIPS) if col else (sh.shape[0] * N_CHIPS, sh.shape[1])
                   for sh, col in zip(shards, cols)]
    nw = len(shards)

    def first_copies(ins, outs, sems):
        send_sems, recv_sems, local_sems = sems
        x, y, c = _my_place()
        me = 2 * x + y
        locals_, sends = [], []
        for wi in range(nw):
            src, dst = ins[wi], outs[wi]
            rs, cs = src.shape
            hs = rs // 2
            if cols[wi]:
                place = dst.at[:, pl.ds(me * cs, cs)]
            else:
                place = dst.at[pl.ds(pl.multiple_of(me * rs, 8), rs), :]
            locals_.append(pltpu.make_async_copy(src, place, local_sems.at[wi]))
            my_half = src.at[pl.ds(pl.multiple_of(c * hs, 8), hs), :]
            for j, (px, py) in enumerate(_other_chips(x, y)):
                sends.append(pltpu.make_async_remote_copy(
                    src_ref=my_half, dst_ref=_half_of_shard(dst, cols[wi], me, c),
                    send_sem=send_sems.at[wi * 6 + j], recv_sem=recv_sems.at[wi * 6 + j],
                    device_id=(px, py, c), device_id_type=MESH))
        return locals_, sends

    def start(ins, outs, sems):
        locals_, sends = first_copies(ins, outs, sems)
        for cp in locals_ + sends:
            cp.start()

    def forwards(outs, sems, wait_landed):
        send_sems, recv_sems, _ = sems
        x, y, c = _my_place()
        out = []
        for wi in range(nw):
            dst = outs[wi]
            for j, (px, py) in enumerate(_other_chips(x, y)):
                got = _half_of_shard(dst, cols[wi], 2 * px + py, c)
                if wait_landed:
                    pltpu.make_async_remote_copy(
                        src_ref=got, dst_ref=got, send_sem=send_sems.at[wi * 6 + j], recv_sem=recv_sems.at[wi * 6 + j],
                        device_id=(px, py, c), device_id_type=MESH).wait_recv()
                out.append(pltpu.make_async_remote_copy(
                    src_ref=got, dst_ref=got, send_sem=send_sems.at[wi * 6 + 3 + j], recv_sem=recv_sems.at[wi * 6 + 3 + j],
                    device_id=(x, y, 1 - c), device_id_type=MESH))
                if wait_landed:
                    out[-1].start()
        return out

    def middle(ins, outs, sems):
        forwards(outs, sems, True)

    def finish(ins, outs, sems):
        send_sems, recv_sems, _ = sems
        x, y, c = _my_place()
        locals_, sends = first_copies(ins, outs, sems)
        for wi in range(nw):
            dst = outs[wi]
            for j, (px, py) in enumerate(_other_chips(x, y)):
                theirs = _half_of_shard(dst, cols[wi], 2 * px + py, 1 - c)
                pltpu.make_async_remote_copy(
                    src_ref=theirs, dst_ref=theirs, send_sem=send_sems.at[wi * 6 + 3 + j],
                    recv_sem=recv_sems.at[wi * 6 + 3 + j], device_id=(x, y, 1 - c), device_id_type=MESH).wait_recv()
        for cp in sends + forwards(outs, sems, False):
            cp.wait_send()
        for cp in locals_:
            cp.wait()

    return _Comm(shards, [jax.ShapeDtypeStruct(s, BF16) for s in full_shapes],
                 [pltpu.SemaphoreType.DMA((6 * nw,)), pltpu.SemaphoreType.DMA((6 * nw,)),
                  pltpu.SemaphoreType.DMA((nw,))], start, finish, middle=middle)


def _pair_exchange_comm(grads, cols):
    nw = len(grads)
    n_copies = sum(1 if col else N_CHIPS for col in cols)

    def copies(ins, outs, sems):
        send_sems, recv_sems = sems
        x, y, c = _my_place()
        out = []
        for wi in range(nw):
            src, dst = ins[wi], outs[wi]
            fr = src.shape[0]
            if cols[wi]:
                hr = fr // 2
                pieces = [(src.at[pl.ds(pl.multiple_of((1 - c) * hr, 8), hr), :], dst)]
            else:
                rs = fr // N_CHIPS
                hs = rs // 2
                pieces = [(src.at[pl.ds(pl.multiple_of(k * rs + (1 - c) * hs, 8), hs), :],
                           dst.at[pl.ds(k * hs, hs), :]) for k in range(N_CHIPS)]
            for s_ref, d_ref in pieces:
                out.append(pltpu.make_async_remote_copy(
                    src_ref=s_ref, dst_ref=d_ref, send_sem=send_sems.at[len(out)], recv_sem=recv_sems.at[len(out)],
                    device_id=(x, y, 1 - c), device_id_type=MESH))
        return out

    def start(ins, outs, sems):
        for cp in copies(ins, outs, sems):
            cp.start()

    def finish(ins, outs, sems):
        for cp in copies(ins, outs, sems):
            cp.wait()

    return _Comm(grads, [jax.ShapeDtypeStruct((g.shape[0] // 2, g.shape[1]), F32) for g in grads],
                 [pltpu.SemaphoreType.DMA((n_copies,)), pltpu.SemaphoreType.DMA((n_copies,))], start, finish)


def _pair_sum(grad, other, col, core, name):
    fr, fc = grad.shape
    pieces = 1 if col else N_CHIPS
    pr = fr // 2 // pieces
    gview = grad.reshape(pieces * 2, pr, fc)
    oview = other.reshape(pieces, pr, fc)
    tr = _tile(pr, 256, 16)

    def body(c_ref, g_ref, o_ref, out_ref):
        out_ref[...] = (g_ref[...] + o_ref[...]).astype(out_ref.dtype)

    out = pl.pallas_call(
        body, name=name,
        grid_spec=pltpu.PrefetchScalarGridSpec(
            num_scalar_prefetch=1, grid=(pieces, pr // tr),
            in_specs=[pl.BlockSpec((1, tr, fc), lambda p, i, cref: (p * 2 + cref[0], i, 0)),
                      pl.BlockSpec((1, tr, fc), lambda p, i, cref: (p, i, 0))],
            out_specs=pl.BlockSpec((1, tr, fc), lambda p, i, cref: (p, i, 0))),
        out_shape=jax.ShapeDtypeStruct((pieces, pr, fc), BF16),
        compiler_params=_params("parallel", "parallel"),
    )(core, gview, oview)
    return out.reshape(fr // 2, fc)


def _chip_exchange_comm(psums, cols):
    nw = len(psums)
    out_shapes = [(N_CHIPS, p.shape[0], p.shape[1] // N_CHIPS) if col else (N_CHIPS, p.shape[0] // N_CHIPS, p.shape[1])
                  for p, col in zip(psums, cols)]

    def copies(ins, outs, sems):
        send_sems, recv_sems, local_sems = sems
        x, y, c = _my_place()
        me = 2 * x + y
        out = []
        for wi in range(nw):
            src = ins[wi]
            mine = outs[wi].at[me]

            def piece(chip, src=src, col=cols[wi]):
                if col:
                    cs = src.shape[1] // N_CHIPS
                    return src.at[:, pl.ds(chip * cs, cs)]
                ps = src.shape[0] // N_CHIPS
                return src.at[pl.ds(pl.multiple_of(chip * ps, 8), ps), :]

            out.append(pltpu.make_async_copy(piece(me), mine, local_sems.at[wi]))
            for j, (px, py) in enumerate(_other_chips(x, y)):
                out.append(pltpu.make_async_remote_copy(
                    src_ref=piece(2 * px + py), dst_ref=mine,
                    send_sem=send_sems.at[wi * 3 + j], recv_sem=recv_sems.at[wi * 3 + j],
                    device_id=(px, py, c), device_id_type=MESH))
        return out

    def start(ins, outs, sems):
        for cp in copies(ins, outs, sems):
            cp.start()

    def finish(ins, outs, sems):
        for cp in copies(ins, outs, sems):
            cp.wait()

    return _Comm(psums, [jax.ShapeDtypeStruct(s, BF16) for s in out_shapes],
                 [pltpu.SemaphoreType.DMA((3 * nw,)), pltpu.SemaphoreType.DMA((3 * nw,)),
                  pltpu.SemaphoreType.DMA((nw,))], start, finish)


def _chip_sum(slots, core, layer, layers, into, name):
    _, hr, cs = slots.shape
    tr = _tile(hr, 256, 16)

    def body(c_ref, s_ref, *rest):
        out_ref = rest[-1]
        acc = s_ref[0].astype(F32)
        for i in range(1, N_CHIPS):
            acc = acc + s_ref[i].astype(F32)
        out_ref[0] = acc

    in_specs = [pl.BlockSpec((N_CHIPS, tr, cs), lambda i, cref: (0, i, 0))]
    args = [core, slots]
    aliases = {}
    if into is not None:
        in_specs.append(pl.BlockSpec(memory_space=pl.ANY))
        args.append(into.reshape(layers * 2, hr, cs))
        aliases = {2: 0}
    out = pl.pallas_call(
        body, name=name,
        grid_spec=pltpu.PrefetchScalarGridSpec(
            num_scalar_prefetch=1, grid=(hr // tr,), in_specs=in_specs,
            out_specs=pl.BlockSpec((1, tr, cs), lambda i, cref: (layer * 2 + cref[0], i, 0))),
        out_shape=jax.ShapeDtypeStruct((layers * 2, hr, cs), F32),
        input_output_aliases=aliases,
        compiler_params=_params("parallel"),
    )(*args)
    return out.reshape(layers, 2 * hr, cs)


def _pair_share_comm(reduced):
    nw = len(reduced)

    def copies(ins, outs, sems):
        send_sems, recv_sems = sems
        x, y, c = _my_place()
        out = []
        for wi in range(nw):
            hs = outs[wi].shape[1] // 2
            mine = outs[wi].at[:, pl.ds(pl.multiple_of(c * hs, 8), hs), :]
            out.append(pltpu.make_async_remote_copy(
                src_ref=mine, dst_ref=mine, send_sem=send_sems.at[wi], recv_sem=recv_sems.at[wi],
                device_id=(x, y, 1 - c), device_id_type=MESH))
        return out

    def start(ins, outs, sems):
        for cp in copies(ins, outs, sems):
            cp.start()

    def finish(ins, outs, sems):
        for cp in copies(ins, outs, sems):
            cp.wait()

    return _Comm(reduced, [jax.ShapeDtypeStruct(r.shape, F32) for r in reduced],
                 [pltpu.SemaphoreType.DMA((nw,)), pltpu.SemaphoreType.DMA((nw,))], start, finish,
                 alias={i: i for i in range(nw)})


def _all_reduce_small(flat, comm):
    rows, lanes = flat.shape
    seg = rows // N_DEV
    c_in, c_out = len(comm.ins), len(comm.outs)

    def body(*refs):
        refs = list(refs)
        in_ref, cins = refs[0], refs[1:1 + c_in]
        out_ref, couts = refs[1 + c_in], refs[2 + c_in:2 + c_in + c_out]
        recv_ref, send_sems, recv_sems = refs[2 + c_in + c_out:5 + c_in + c_out]
        csems = refs[5 + c_in + c_out:]
        comm.start(cins, couts, csems)
        x, y, c = _my_place()
        me = 4 * x + 2 * y + c

        def peer(r):
            fx, fy, fc = (r >> 2) & 1, (r >> 1) & 1, r & 1
            px = jnp.where(fx == 1, 1 - x, x)
            py = jnp.where(fy == 1, 1 - y, y)
            pc = jnp.where(fc == 1, 1 - c, c)
            return px, py, pc

        first = []
        for r in range(1, N_DEV):
            px, py, pc = peer(r)
            theirs = in_ref.at[pl.ds(pl.multiple_of((4 * px + 2 * py + pc) * seg, 8), seg), :]
            cp = pltpu.make_async_remote_copy(
                src_ref=theirs, dst_ref=recv_ref.at[r], send_sem=send_sems.at[r - 1], recv_sem=recv_sems.at[r - 1],
                device_id=(px, py, pc), device_id_type=MESH)
            cp.start()
            first.append(cp)
        for cp in first:
            cp.wait()
        my_rows = pl.ds(pl.multiple_of(me * seg, 8), seg)
        acc = in_ref[my_rows, :]
        for r in range(1, N_DEV):
            acc = acc + recv_ref[r]
        out_ref[my_rows, :] = acc
        second = []
        for r in range(1, N_DEV):
            px, py, pc = peer(r)
            cp = pltpu.make_async_remote_copy(
                src_ref=out_ref.at[my_rows, :], dst_ref=out_ref.at[my_rows, :],
                send_sem=send_sems.at[6 + r], recv_sem=recv_sems.at[6 + r],
                device_id=(px, py, pc), device_id_type=MESH)
            cp.start()
            second.append(cp)
        for r in range(1, N_DEV):
            px, py, pc = peer(r)
            theirs = out_ref.at[pl.ds(pl.multiple_of((4 * px + 2 * py + pc) * seg, 8), seg), :]
            pltpu.make_async_remote_copy(
                src_ref=theirs, dst_ref=theirs, send_sem=send_sems.at[6 + r], recv_sem=recv_sems.at[6 + r],
                device_id=(px, py, pc), device_id_type=MESH).wait_recv()
        for cp in second:
            cp.wait_send()
        comm.finish(cins, couts, csems)

    vm = pl.BlockSpec(memory_space=pltpu.VMEM)
    any_spec = pl.BlockSpec(memory_space=pl.ANY)
    outs = pl.pallas_call(
        body, name="all_reduce_small",
        in_specs=[vm] + [any_spec] * c_in, out_specs=[vm] + [any_spec] * c_out,
        out_shape=[jax.ShapeDtypeStruct((rows, lanes), F32)] + comm.outs,
        scratch_shapes=[pltpu.VMEM((N_DEV, seg, lanes), F32),
                        pltpu.SemaphoreType.DMA((2 * (N_DEV - 1),)), pltpu.SemaphoreType.DMA((2 * (N_DEV - 1),))]
        + comm.sems,
        input_output_aliases={1 + ci: 1 + co for ci, co in comm.alias.items()},
        compiler_params=pltpu.CompilerParams(vmem_limit_bytes=VMEM_LIMIT),
    )(flat, *comm.ins)
    return outs[0], list(outs[1:])


def _adamw_update(w_ref, g_ref, m_ref, v_ref, d_ref, nm_ref, nv_ref):
    c1 = 1.0 - ADAM_B1 ** ADAM_STEP
    c2 = 1.0 - ADAM_B2 ** ADAM_STEP
    gv = g_ref[...]
    nm = ADAM_B1 * m_ref[...] + (1.0 - ADAM_B1) * gv
    nv = ADAM_B2 * v_ref[...] + (1.0 - ADAM_B2) * (gv * gv)
    d_ref[...] = -ADAM_LR * ((nm / c1) / (jnp.sqrt(nv / c2) + ADAM_EPS) + ADAM_WD * w_ref[...])
    nm_ref[...] = nm
    nv_ref[...] = nv


def _adamw_many(ws, gs, ms, vs, name):
    n = len(ws)

    def body(*refs):
        for i in range(n):
            _adamw_update(*[refs[k * n + i] for k in range(7)])

    shapes = [jax.ShapeDtypeStruct(w.shape, F32) for w in ws]
    outs = pl.pallas_call(
        body, name=name, out_shape=shapes * 3,
        compiler_params=pltpu.CompilerParams(vmem_limit_bytes=VMEM_LIMIT),
    )(*ws, *gs, *ms, *vs)
    return outs[:n], outs[n:2 * n], outs[2 * n:]


def _adamw(w, g, m, v, name):
    rows, cols = w.shape
    tr = _tile(rows, 256, 8)

    def body(w_ref, g_ref, m_ref, v_ref, go_ref, d_ref, nm_ref, nv_ref):
        go_ref[...] = g_ref[...]
        _adamw_update(w_ref, g_ref, m_ref, v_ref, d_ref, nm_ref, nv_ref)

    blk = pl.BlockSpec((tr, cols), lambda i: (i, 0))
    sds = jax.ShapeDtypeStruct((rows, cols), F32)
    return pl.pallas_call(
        body, name=name, grid=(rows // tr,),
        in_specs=[blk] * 4, out_specs=[blk] * 4, out_shape=[sds] * 4,
        compiler_params=_params("parallel"),
    )(w, g, m, v)


SMALL = ("norm_ffn1", "norm_mix", "ssm_a_re", "ssm_a_im", "ssm_log_dt", "ssm_b_re", "ssm_b_im", "ssm_c_re",
         "ssm_c_im", "ssm_d", "ssm_glu_w", "ssm_glu_b", "gm_v_gain", "gm_w_s", "gm_b_s", "gain_ssm_out",
         "gain_gm_out", "norm_ffn2", "norm_final")
WEIGHTS = ("norm_ffn1", "ffn1_w_in", "ffn1_w_out", "norm_mix", "mix_w_in", "ssm_a_re", "ssm_a_im", "ssm_log_dt",
           "ssm_b_re", "ssm_b_im", "ssm_c_re", "ssm_c_im", "ssm_d", "ssm_glu_w", "ssm_glu_b", "gm_v_gain", "gm_w_s",
           "gm_b_s", "gain_ssm_out", "gain_gm_out", "mix_w_out", "norm_ffn2", "ffn2_w_in", "ffn2_w_out", "norm_final")


def _ffn_fwd(x, gain, w_in, w_out, tag, hosted=None):
    if hosted is None:
        h, t, q, a = _ffn_in_fwd(x, gain, w_in, f"{tag}_in")
    else:
        (h, t, q, a), got = _ffn_in_fwd(x, gain, w_in, f"{tag}_in_hosting", comm=hosted[0]())
        hosted[1](got)
    if callable(w_out):
        w_out = w_out()
    out = _matmul(a, w_out, "nn", scale=0.5, res=x, tm=512, tn=1024, tk=4096, name=f"{tag}_out")
    return out, (x, h, t, q, a)


def _ffn_bwd(dout, saved, gain, w_in, w_out, tag, hooks=None, publish=None, late_out_dw=False):
    x, h, t, q, a = saved
    f = t.shape[1]
    hooks = hooks or {}

    def hosted(key, fn, *args, name, **kw):
        if key not in hooks:
            return fn(*args, name=name, **kw)
        make, take = hooks[key]
        *res, got = fn(*args, name=f"{name}_hosting", comm=make(), **kw)
        take(got)
        return res[0] if len(res) == 1 else tuple(res)

    def out_dw():
        dw = hosted("out_dw", _matmul, a, dout, "tn", scale=0.5, tm=1536, tn=512, tk=4096, name=f"{tag}_out_dw")
        if publish is not None:
            publish("out", dw)
        return dw

    dg, du = hosted("out_dx", _ffn_out_bwd, dout, w_out, t, q, name=f"{tag}_out_dx")
    if not late_out_dw:
        dw_out = out_dw()
    dw_in = hosted("in_dw_g", _matmul, h, dg, "tn", tm=512, tn=1536, tk=4096, name=f"{tag}_in_dw_g",
                   out_cols=2 * f)
    dw_in = hosted("in_dw_u", _matmul, h, du, "tn", tm=512, tn=1536, tk=4096, name=f"{tag}_in_dw_u",
                   out_cols=2 * f, col_off=f, into=dw_in)
    if publish is not None:
        publish("in", dw_in)
    if late_out_dw:
        dw_out = out_dw()
    dx, dgain = hosted("in_dx", _proj_in_bwd, [(dg, 0), (du, f)], w_in, x, gain, dout, name=f"{tag}_in_dx")
    return dx, dgain, dw_in, dw_out


def kernel(x, norm_ffn1, ffn1_w_in, ffn1_w_out, norm_mix, mix_w_in, ssm_a_re, ssm_a_im, ssm_log_dt, ssm_b_re, ssm_b_im, ssm_c_re, ssm_c_im, ssm_d, ssm_glu_w, ssm_glu_b, gm_v_gain, gm_w_s, gm_b_s, gain_ssm_out, gain_gm_out, mix_w_out, norm_ffn2, ffn2_w_in, ffn2_w_out, norm_final, loss_target, m_norm_ffn1, m_ffn1_w_in, m_ffn1_w_out, m_norm_mix, m_mix_w_in, m_ssm_a_re, m_ssm_a_im, m_ssm_log_dt, m_ssm_b_re, m_ssm_b_im, m_ssm_c_re, m_ssm_c_im, m_ssm_d, m_ssm_glu_w, m_ssm_glu_b, m_gm_v_gain, m_gm_w_s, m_gm_b_s, m_gain_ssm_out, m_gain_gm_out, m_mix_w_out, m_norm_ffn2, m_ffn2_w_in, m_ffn2_w_out, m_norm_final, v_norm_ffn1, v_ffn1_w_in, v_ffn1_w_out, v_norm_mix, v_mix_w_in, v_ssm_a_re, v_ssm_a_im, v_ssm_log_dt, v_ssm_b_re, v_ssm_b_im, v_ssm_c_re, v_ssm_c_im, v_ssm_d, v_ssm_glu_w, v_ssm_glu_b, v_gm_v_gain, v_gm_w_s, v_gm_b_s, v_gain_ssm_out, v_gain_gm_out, v_mix_w_out, v_norm_ffn2, v_ffn2_w_in, v_ffn2_w_out, v_norm_final):
    wts = dict(norm_ffn1=norm_ffn1, ffn1_w_in=ffn1_w_in, ffn1_w_out=ffn1_w_out, norm_mix=norm_mix, mix_w_in=mix_w_in,
               ssm_a_re=ssm_a_re, ssm_a_im=ssm_a_im, ssm_log_dt=ssm_log_dt, ssm_b_re=ssm_b_re, ssm_b_im=ssm_b_im,
               ssm_c_re=ssm_c_re, ssm_c_im=ssm_c_im, ssm_d=ssm_d, ssm_glu_w=ssm_glu_w, ssm_glu_b=ssm_glu_b,
               gm_v_gain=gm_v_gain, gm_w_s=gm_w_s, gm_b_s=gm_b_s, gain_ssm_out=gain_ssm_out, gain_gm_out=gain_gm_out,
               mix_w_out=mix_w_out, norm_ffn2=norm_ffn2, ffn2_w_in=ffn2_w_in, ffn2_w_out=ffn2_w_out,
               norm_final=norm_final)
    mom = dict(norm_ffn1=m_norm_ffn1, ffn1_w_in=m_ffn1_w_in, ffn1_w_out=m_ffn1_w_out, norm_mix=m_norm_mix,
               mix_w_in=m_mix_w_in, ssm_a_re=m_ssm_a_re, ssm_a_im=m_ssm_a_im, ssm_log_dt=m_ssm_log_dt,
               ssm_b_re=m_ssm_b_re, ssm_b_im=m_ssm_b_im, ssm_c_re=m_ssm_c_re, ssm_c_im=m_ssm_c_im, ssm_d=m_ssm_d,
               ssm_glu_w=m_ssm_glu_w, ssm_glu_b=m_ssm_glu_b, gm_v_gain=m_gm_v_gain, gm_w_s=m_gm_w_s, gm_b_s=m_gm_b_s,
               gain_ssm_out=m_gain_ssm_out, gain_gm_out=m_gain_gm_out, mix_w_out=m_mix_w_out, norm_ffn2=m_norm_ffn2,
               ffn2_w_in=m_ffn2_w_in, ffn2_w_out=m_ffn2_w_out, norm_final=m_norm_final)
    var = dict(norm_ffn1=v_norm_ffn1, ffn1_w_in=v_ffn1_w_in, ffn1_w_out=v_ffn1_w_out, norm_mix=v_norm_mix,
               mix_w_in=v_mix_w_in, ssm_a_re=v_ssm_a_re, ssm_a_im=v_ssm_a_im, ssm_log_dt=v_ssm_log_dt,
               ssm_b_re=v_ssm_b_re, ssm_b_im=v_ssm_b_im, ssm_c_re=v_ssm_c_re, ssm_c_im=v_ssm_c_im, ssm_d=v_ssm_d,
               ssm_glu_w=v_ssm_glu_w, ssm_glu_b=v_ssm_glu_b, gm_v_gain=v_gm_v_gain, gm_w_s=v_gm_w_s, gm_b_s=v_gm_b_s,
               gain_ssm_out=v_gain_ssm_out, gain_gm_out=v_gain_gm_out, mix_w_out=v_mix_w_out, norm_ffn2=v_norm_ffn2,
               ffn2_w_in=v_ffn2_w_in, ffn2_w_out=v_ffn2_w_out, norm_final=v_norm_final)

    nseq, seq, d = x.shape
    n = nseq * seq
    depth = norm_ffn1.shape[0]
    width = gain_ssm_out.shape[1]
    groups = ssm_a_re.shape[1]
    heads = gm_w_s.shape[1]
    core = lax.axis_index("c").astype(jnp.int32).reshape(1)

    is_col = dict(BIG)
    full = {name: [None] * depth for name, _ in BIG}

    def gather_comm(pairs):
        return _gather_comm([wts[nm][l].astype(BF16) for nm, l in pairs], [is_col[nm] for nm, _ in pairs])

    def store(pairs, arrays):
        for (nm, l), w in zip(pairs, arrays):
            full[nm][l] = w

    pairs = [("ffn1_w_in", 0)]
    store(pairs, _run_comm(gather_comm(pairs), "all_gather_first"))

    xs = x.reshape(n, d)
    saved = []
    for l in range(depth):
        pairs = [("ffn1_w_out", l)] + ([("mix_w_in", l), ("mix_w_out", l)] if l == 0 else [])
        x1, s_ffn1 = _ffn_fwd(xs, norm_ffn1[l], full["ffn1_w_in"][l], lambda l=l: full["ffn1_w_out"][l], "ffn1",
                              hosted=(functools.partial(gather_comm, pairs), functools.partial(store, pairs)))
        pairs = [("ffn2_w_out", l)]
        hm, u_ssm, zgm, got = _mix_in_fwd(x1, norm_mix[l], full["mix_w_in"][l], width, "mix_in",
                                          comm=gather_comm(pairs))
        store(pairs, got)
        bt_re = jnp.swapaxes(ssm_b_re[l], 1, 2)
        bt_im = jnp.swapaxes(ssm_b_im[l], 1, 2)
        disc_in = (ssm_a_re[l], ssm_a_im[l], ssm_log_dt[l].reshape(groups, 1), bt_re, bt_im)
        lr, li, bbr, bbi = _disc_fwd(*disc_in)
        ops = _ssm_operands(lr, li, bbr, bbi, ssm_c_re[l], ssm_c_im[l], ssm_d[l], ssm_glu_w[l], ssm_glu_b[l])
        pairs = [("ffn2_w_in", l)]
        y_ssm, hb, got = _ssm_fwd_pair(u_ssm, ops, nseq, "s5_fwd", comm=gather_comm(pairs))
        store(pairs, got)
        bias_tile = jnp.broadcast_to(gm_b_s[l].T[:, :, None], (GM_CHUNK, heads, GM_HEAD_DIM)).reshape(GM_CHUNK, width)
        y_gm = _gmlp_fwd(zgm, gm_v_gain[l], gm_w_s[l], bias_tile, "gmlp_fwd")
        if l + 1 < depth:
            pairs = [("mix_w_in", l + 1), ("mix_w_out", l + 1)]
            ycat, x2, got = _mix_out_fwd(y_ssm, y_gm, gain_ssm_out[l], gain_gm_out[l], full["mix_w_out"][l], x1,
                                         "mix_out_hosting", comm=gather_comm(pairs))
            store(pairs, got)
        else:
            ycat, x2 = _mix_out_fwd(y_ssm, y_gm, gain_ssm_out[l], gain_gm_out[l], full["mix_w_out"][l], x1, "mix_out")
        hosted = None
        if l + 1 < depth:
            pairs = [("ffn1_w_in", l + 1)]
            hosted = (functools.partial(gather_comm, pairs), functools.partial(store, pairs))
        x3, s_ffn2 = _ffn_fwd(x2, norm_ffn2[l], full["ffn2_w_in"][l], full["ffn2_w_out"][l], "ffn2", hosted=hosted)
        saved.append(dict(ffn1=s_ffn1, x1=x1, hm=hm, zgm=zgm, disc_in=disc_in, ops=ops, u_ssm=u_ssm, hb=hb, y_ssm=y_ssm,
                          bias_tile=bias_tile, y_gm=y_gm, ycat=ycat, ffn2=s_ffn2))
        xs = x3

    dx, g_norm_final, loss_part = _loss_head(xs, norm_final, loss_target.reshape(n, d))
    big = {name: [None] * depth for name, _ in BIG}
    small = {name: [None] * depth for name in SMALL if name != "norm_final"}
    gpb = GROUPS_PER_BLOCK
    s_blk = STATES_PER_BLOCK
    psum_of, reduced, grads = {}, {}, {}
    shared_early = ["ffn2_w_in", "ffn2_w_out", "mix_w_in", "mix_w_out"]

    def swap_comm(pairs):
        return _pair_exchange_comm([big[nm][l] for nm, l in pairs], [is_col[nm] for nm, _ in pairs])

    def take_swapped(pairs, others):
        for (nm, l), other in zip(pairs, others):
            psum_of[nm, l] = _pair_sum(big[nm][l], other, is_col[nm], core, f"grad_pair_sum_{nm}")

    def send_comm(pairs):
        return _chip_exchange_comm([psum_of[p] for p in pairs], [is_col[nm] for nm, _ in pairs])

    def take_sent(pairs, slots):
        for (nm, l), s in zip(pairs, slots):
            reduced[nm] = _chip_sum(s, core, l, depth, reduced.get(nm), f"grad_chip_sum_{nm}")

    def hosting(make, take, pairs):
        return functools.partial(make, pairs), functools.partial(take, pairs)

    for l in reversed(range(depth)):
        sv = saved[l]
        above = [(nm, l + 1) for nm in ("mix_w_in", "mix_w_out", "ffn1_w_in", "ffn1_w_out")] if l + 1 < depth else []
        dx, small["norm_ffn2"][l], big["ffn2_w_in"][l], big["ffn2_w_out"][l] = _ffn_bwd(
            dx, sv["ffn2"], norm_ffn2[l], full["ffn2_w_in"][l], full["ffn2_w_out"][l], "ffn2",
            hooks={"out_dx": hosting(swap_comm, take_swapped, above)} if above else None)
        mine = [("ffn2_w_in", l), ("ffn2_w_out", l)]
        dy_ssm, dy_gm, small["gain_ssm_out"][l], small["gain_gm_out"][l], got = _mix_out_bwd(
            dx, full["mix_w_out"][l], sv["y_ssm"], sv["y_gm"], gain_ssm_out[l], gain_gm_out[l], "mix_out_dx",
            comm=swap_comm(mine))
        take_swapped(mine, got)
        big["mix_w_out"][l] = _matmul(sv["ycat"], dx, "tn", tm=1024, tn=512, tk=4096, name="mix_out_dw")
        dzgm, small["gm_w_s"][l], dbias_tile, small["gm_v_gain"][l] = _gmlp_bwd(
            sv["zgm"], dy_gm, gm_v_gain[l], gm_w_s[l], sv["bias_tile"], "gmlp_bwd")
        small["gm_b_s"][l] = dbias_tile.reshape(GM_CHUNK, heads, GM_HEAD_DIM).sum(-1).T
        (du_ssm, dlam, db_bd, dct_bd, dd, dw_bd, dbias), got = _ssm_bwd_pair(
            sv["u_ssm"], dy_ssm, sv["hb"], sv["ops"], nseq, "s5_bwd", comm=send_comm(mine + above))
        take_sent(mine + above, got)
        dlr = dlam[:, 0, :s_blk].reshape(groups, SSM_STATE)
        dli = dlam[:, 0, s_blk:].reshape(groups, SSM_STATE)
        dbbr = _block_diag_extract(db_bd[:, :, :s_blk], gpb)
        dbbi = _block_diag_extract(db_bd[:, :, s_blk:], gpb)
        da_re, da_im, dldt, dbt_re, dbt_im = _disc_bwd(*sv["disc_in"], dlr, dli, dbbr, dbbi)
        small["ssm_a_re"][l], small["ssm_a_im"][l], small["ssm_log_dt"][l] = da_re, da_im, dldt.reshape(groups)
        small["ssm_b_re"][l] = jnp.swapaxes(dbt_re, 1, 2)
        small["ssm_b_im"][l] = jnp.swapaxes(dbt_im, 1, 2)
        small["ssm_c_re"][l] = _block_diag_extract(dct_bd[:, :, :s_blk], gpb)
        small["ssm_c_im"][l] = -_block_diag_extract(dct_bd[:, :, s_blk:], gpb)
        small["ssm_d"][l] = dd.reshape(groups, SSM_CH)
        small["ssm_glu_w"][l] = jnp.concatenate(
            [_block_diag_extract(dw_bd[:, :, :LANES], gpb), _block_diag_extract(dw_bd[:, :, LANES:], gpb)], axis=-1)
        small["ssm_glu_b"][l] = jnp.concatenate(
            [dbias[:, 0, :LANES].reshape(groups, SSM_CH), dbias[:, 0, LANES:].reshape(groups, SSM_CH)], axis=-1)
        cols_mi = 3 * width
        dw_mi = _matmul(sv["hm"], du_ssm, "tn", tm=1024, tn=width, tk=2048, name="mix_in_dw_ssm", out_cols=cols_mi)
        big["mix_w_in"][l] = _matmul(sv["hm"], dzgm, "tn", tm=1024, tn=width, tk=2048, name="mix_in_dw_gm",
                                     out_cols=cols_mi, col_off=width, into=dw_mi)
        dx, small["norm_mix"][l] = _proj_in_bwd([(du_ssm, 0), (dzgm, width)], full["mix_w_in"][l], sv["x1"],
                                                norm_mix[l], dx, "mix_in_dx")
        hooks = None
        if l == 0:
            mix, w_out_0, w_in_0 = [("mix_w_in", 0), ("mix_w_out", 0)], [("ffn1_w_out", 0)], [("ffn1_w_in", 0)]

            def last_make():
                return _merge_comms(_merge_comms(send_comm(w_in_0), swap_comm(w_out_0)),
                                    _pair_share_comm([reduced[nm] for nm in shared_early]))

            def last_take(got):
                take_sent(w_in_0, got[:1])
                take_swapped(w_out_0, got[1:2])
                grads.update(zip(shared_early, got[2:]))

            hooks = {"out_dx": hosting(swap_comm, take_swapped, mix), "in_dw_g": hosting(send_comm, take_sent, mix),
                     "out_dw": hosting(swap_comm, take_swapped, w_in_0), "in_dx": (last_make, last_take)}

        def publish(which, dw, l=l):
            big[f"ffn1_w_{which}"][l] = dw

        dx, small["norm_ffn1"][l], big["ffn1_w_in"][l], big["ffn1_w_out"][l] = _ffn_bwd(
            dx, sv["ffn1"], norm_ffn1[l], full["ffn1_w_in"][l], full["ffn1_w_out"][l], "ffn1",
            hooks=hooks, publish=publish, late_out_dw=(l == 0))
    grad_x = dx.reshape(nseq, seq, d)

    pieces = [jnp.stack(small[name]).reshape(-1) for name in SMALL if name != "norm_final"]
    pieces += [g_norm_final.reshape(-1), loss_part.reshape(1)]
    sizes = [p.shape[0] for p in pieces]
    total = sum(sizes)
    rows = -(-total // (LANES * N_DEV * SUBLANES)) * N_DEV * SUBLANES
    pad = rows * LANES - total
    tail = [("ffn1_w_out", 0)]
    flat_g, got = _all_reduce_small(
        jnp.concatenate(pieces + [jnp.zeros((pad,), F32)]).reshape(rows, LANES), send_comm(tail))
    take_sent(tail, got)
    flat_g = flat_g.reshape(-1)
    loss = flat_g[total - 1]

    names = [name for name, _ in BIG if name not in shared_early]
    grads.update(zip(names, _run_comm(_pair_share_comm([reduced[nm] for nm in names]), "grad_pair_share")))
    offs = 0
    for name, size in zip(SMALL, sizes[:-1]):
        grads[name] = flat_g[offs:offs + size].reshape(wts[name].shape)
        offs += size

    delta, new_m, new_v = {}, {}, {}
    for name, _ in BIG:
        shape = wts[name].shape
        two_d = lambda a: a.reshape(shape[0] * shape[1], shape[2])
        go, dl, nm, nv = _adamw(two_d(wts[name]), two_d(grads[name]), two_d(mom[name]), two_d(var[name]),
                                f"adamw_{name}")
        grads[name] = go.reshape(shape)
        delta[name], new_m[name], new_v[name] = dl.reshape(shape), nm.reshape(shape), nv.reshape(shape)
    at_least_2d = lambda a: a.reshape(1, -1) if a.ndim == 1 else a
    dls, nms, nvs = _adamw_many(*[[at_least_2d(tree[k]) for k in SMALL] for tree in (wts, grads, mom, var)],
                                "adamw_small")
    for name, dl, nm, nv in zip(SMALL, dls, nms, nvs):
        shape = wts[name].shape
        delta[name], new_m[name], new_v[name] = dl.reshape(shape), nm.reshape(shape), nv.reshape(shape)

    return (loss, grad_x, *[grads[k] for k in WEIGHTS], *[delta[k] for k in WEIGHTS],
            *[new_m[k] for k in WEIGHTS], *[new_v[k] for k in WEIGHTS])
```

```python
import functools
import math

import jax
import jax.numpy as jnp
from jax import lax
from jax.experimental import pallas as pl
from jax.experimental.pallas import tpu as pltpu

F32 = jnp.float32
BF16 = jnp.bfloat16
MESH = pl.DeviceIdType.MESH

EPS = 1e-6
SSM_CH = 16
SSM_STATE = 64
GM_CHUNK = 128
GM_HEAD_DIM = 128
SUBLANES = 8
LANES = 128
GROUPS_PER_BLOCK = LANES // SSM_CH
STATES_PER_BLOCK = GROUPS_PER_BLOCK * SSM_STATE
SSM_TIME_CHUNK = 128
N_CHIPS = 4
N_DEV = 8

ADAM_LR = 0.001
ADAM_B1 = 0.9
ADAM_B2 = 0.999
ADAM_EPS = 1e-08
ADAM_WD = 0.01
ADAM_STEP = 10

VMEM_LIMIT = 56 * 1024 * 1024


def _tile(dim, pref, align):
    best = None
    t = align
    while t <= min(dim, pref):
        if dim % t == 0:
            best = t
        t += align
    return best if best is not None else dim


def _params(*sem):
    return pltpu.CompilerParams(dimension_semantics=sem, vmem_limit_bytes=VMEM_LIMIT)


def _gelu(x):
    c = math.sqrt(2.0 / math.pi)
    return 0.5 * x * (1.0 + jnp.tanh(c * (x + 0.044715 * x * x * x)))


def _gelu_and_grad(x):
    c = math.sqrt(2.0 / math.pi)
    t = jnp.tanh(c * (x + 0.044715 * x * x * x))
    g = 0.5 * x * (1.0 + t)
    dg = 0.5 * (1.0 + t) + 0.5 * x * (1.0 - t * t) * c * (1.0 + 3.0 * 0.044715 * x * x)
    return g, dg


def _sigmoid(x):
    return 0.5 * jnp.tanh(0.5 * x) + 0.5


def _matmul(a, b, mode, *, out_dtype=F32, scale=1.0, res=None, tm=512, tn=1024, tk=1024, name="mm",
            out_cols=None, col_off=0, into=None, comm=None):
    if mode == "nn":
        (m, k), (k2, n) = a.shape, b.shape
    elif mode == "nt":
        (m, k), (n, k2) = a.shape, b.shape
    else:
        (k, m), (k2, n) = a.shape, b.shape
    assert k == k2, (a.shape, b.shape, mode)
    tm = _tile(m, tm, 16 if mode != "tn" else LANES)
    tn = _tile(n, tn, LANES)
    tk = _tile(k, tk, LANES if mode != "tn" else 16)
    nk = k // tk
    grid = (m // tm, n // tn, nk)
    if mode == "nn":
        a_spec = pl.BlockSpec((tm, tk), lambda i, j, kk: (i, kk))
        b_spec = pl.BlockSpec((tk, tn), lambda i, j, kk: (kk, j))
        dims = (((1,), (0,)), ((), ()))
    elif mode == "nt":
        a_spec = pl.BlockSpec((tm, tk), lambda i, j, kk: (i, kk))
        b_spec = pl.BlockSpec((tn, tk), lambda i, j, kk: (j, kk))
        dims = (((1,), (1,)), ((), ()))
    else:
        a_spec = pl.BlockSpec((tk, tm), lambda i, j, kk: (kk, i))
        b_spec = pl.BlockSpec((tk, tn), lambda i, j, kk: (kk, j))
        dims = (((0,), (0,)), ((), ()))
    assert col_off % tn == 0
    off = col_off // tn
    r_spec = pl.BlockSpec((tm, tn), lambda i, j, kk: (i, j))
    o_spec = pl.BlockSpec((tm, tn), lambda i, j, kk: (i, j + off))
    has_res = res is not None
    has_into = into is not None

    def body(*refs):
        refs = list(refs)
        a_ref, b_ref = refs[:2]
        pos = 2
        r_ref = None
        if has_res:
            r_ref = refs[pos]
            pos += 1
        if has_into:
            pos += 1
        o_ref = refs[pos]
        acc_ref = refs[pos + 1] if nk > 1 else None
        part = lax.dot_general(a_ref[...].astype(BF16), b_ref[...].astype(BF16), dims,
                               preferred_element_type=F32)

        def finish(r):
            if scale != 1.0:
                r = r * scale
            if has_res:
                r = r + r_ref[...].astype(F32)
            o_ref[...] = r.astype(o_ref.dtype)

        if nk == 1:
            finish(part)
        else:
            kk = pl.program_id(2)

            @pl.when(kk == 0)
            def _():
                acc_ref[...] = part

            @pl.when(kk > 0)
            def _():
                acc_ref[...] += part

            @pl.when(kk == nk - 1)
            def _():
                finish(acc_ref[...])

    in_specs = [a_spec, b_spec]
    args = [a, b]
    if has_res:
        in_specs.append(r_spec)
        args.append(res)
    aliases = {}
    if has_into:
        in_specs.append(pl.BlockSpec(memory_space=pl.ANY))
        args.append(into)
        aliases = {len(args) - 1: 0}
    (out,), comm_outs = _call(
        body, name=name, grid=grid, in_specs=in_specs, out_specs=[o_spec],
        out_shape=[jax.ShapeDtypeStruct((m, n if out_cols is None else out_cols), out_dtype)],
        scratch_shapes=[pltpu.VMEM((tm, tn), F32)] if nk > 1 else [],
        aliases=aliases, semantics=("parallel", "parallel", "arbitrary"), args=args, comm=comm)
    return out if comm is None else (out, comm_outs)


class _Comm:
    def __init__(self, ins, outs, sems, start, finish, alias=None, middle=None):
        self.ins, self.outs, self.sems, self.start, self.finish = list(ins), list(outs), list(sems), start, finish
        self.alias = dict(alias or {})
        self.middle = middle


def _merge_comms(a, b):
    assert a.middle is None and b.middle is None
    cut = (len(a.ins), len(a.outs), len(a.sems))

    def both(which):
        def run(ins, outs, sems):
            getattr(a, which)(ins[:cut[0]], outs[:cut[1]], sems[:cut[2]])
            getattr(b, which)(ins[cut[0]:], outs[cut[1]:], sems[cut[2]:])
        return run

    alias = dict(a.alias)
    alias.update({cut[0] + ci: cut[1] + co for ci, co in b.alias.items()})
    return _Comm(a.ins + b.ins, a.outs + b.outs, a.sems + b.sems, both("start"), both("finish"), alias=alias)


def _call(body, *, name, grid, in_specs, out_specs, out_shape, args, scratch_shapes=(), semantics=(), aliases=None,
          comm=None):
    in_specs, out_specs, out_shape = list(in_specs), list(out_specs), list(out_shape)
    scratch_shapes = list(scratch_shapes)
    aliases = dict(aliases or {})
    if comm is None:
        outs = pl.pallas_call(
            body, name=name, grid=grid, in_specs=in_specs, out_specs=out_specs, out_shape=out_shape,
            scratch_shapes=scratch_shapes, input_output_aliases=aliases, compiler_params=_params(*semantics),
        )(*args)
        return list(outs), []
    n_in, n_out, n_scr = len(in_specs), len(out_specs), len(scratch_shapes)
    c_in, c_out = len(comm.ins), len(comm.outs)
    for ci, co in comm.alias.items():
        aliases[n_in + ci] = n_out + co

    def hosted(*refs):
        refs = list(refs)
        ins, cins = refs[:n_in], refs[n_in:n_in + c_in]
        p = n_in + c_in
        outs, couts = refs[p:p + n_out], refs[p + n_out:p + n_out + c_out]
        p += n_out + c_out
        scr, sems = refs[p:p + n_scr], refs[p + n_scr:]
        ids = [pl.program_id(a) for a in range(len(grid))]
        first = functools.reduce(jnp.logical_and, [i == 0 for i in ids])
        last = functools.reduce(jnp.logical_and, [i == g - 1 for i, g in zip(ids, grid)])

        total = math.prod(grid)
        late = comm.middle is not None and total >= 4

        @pl.when(first)
        def _():
            comm.start(cins, couts, sems)

        if late:
            flat = functools.reduce(lambda acc, ig: acc * ig[1] + ig[0], zip(ids, grid), 0)

            @pl.when(flat == (3 * total) // 4)
            def _():
                comm.middle(cins, couts, sems)

        body(*ins, *outs, *scr)

        @pl.when(last)
        def _():
            if comm.middle is not None and not late:
                comm.middle(cins, couts, sems)
            comm.finish(cins, couts, sems)

    any_spec = pl.BlockSpec(memory_space=pl.ANY)
    outs = pl.pallas_call(
        hosted, name=name, grid=grid, in_specs=in_specs + [any_spec] * c_in, out_specs=out_specs + [any_spec] * c_out,
        out_shape=out_shape + comm.outs, scratch_shapes=scratch_shapes + comm.sems, input_output_aliases=aliases,
        compiler_params=_params(*(["arbitrary"] * len(grid))),
    )(*args, *comm.ins)
    return list(outs[:n_out]), list(outs[n_out:])


def _run_comm(comm, name):
    c_in, c_out = len(comm.ins), len(comm.outs)

    def body(*refs):
        refs = list(refs)
        cins, couts, sems = refs[:c_in], refs[c_in:c_in + c_out], refs[c_in + c_out:]
        comm.start(cins, couts, sems)
        if comm.middle is not None:
            comm.middle(cins, couts, sems)
        comm.finish(cins, couts, sems)

    any_spec = pl.BlockSpec(memory_space=pl.ANY)
    return list(pl.pallas_call(
        body, name=name, in_specs=[any_spec] * c_in, out_specs=[any_spec] * c_out, out_shape=comm.outs,
        scratch_shapes=comm.sems, input_output_aliases=comm.alias,
    )(*comm.ins))


def _loss_head(x, gain, target):
    n, d = x.shape
    tm = _tile(n, 512, 8)
    steps = n // tm

    def body(x_ref, g_ref, t_ref, dx_ref, dg_ref, loss_ref, acc_ref, lacc_ref):
        i = pl.program_id(0)
        xv = x_ref[...]
        g = g_ref[...]
        r = lax.rsqrt(jnp.mean(xv * xv, axis=-1, keepdims=True) + EPS)
        xh = xv * r
        err = xh * g - t_ref[...]
        dy = err * (1.0 / d)
        dyg = dy * g
        mean = jnp.mean(dyg * xh, axis=-1, keepdims=True)
        dx_ref[...] = r * (dyg - xh * mean)
        part = jnp.sum((dy * xh).reshape(tm // SUBLANES, SUBLANES, d), axis=0)
        lpart = jnp.sum((err * err).reshape(tm // SUBLANES, SUBLANES, d), axis=0)

        @pl.when(i == 0)
        def _():
            acc_ref[...] = part
            lacc_ref[...] = lpart

        @pl.when(i > 0)
        def _():
            acc_ref[...] += part
            lacc_ref[...] += lpart

        @pl.when(i == steps - 1)
        def _():
            dg_ref[...] = jnp.sum(acc_ref[...], axis=0, keepdims=True)
            tot = jnp.sum(jnp.sum(lacc_ref[...], axis=0, keepdims=True), axis=1, keepdims=True)
            loss_ref[...] = jnp.broadcast_to(tot * (0.5 / d), loss_ref.shape)

    row = pl.BlockSpec((tm, d), lambda i: (i, 0))
    vec = pl.BlockSpec((1, d), lambda i: (0, 0))
    dx, dg, loss = pl.pallas_call(
        body, name="loss_head", grid=(steps,),
        in_specs=[row, vec, row],
        out_specs=[row, vec, pl.BlockSpec((1, LANES), lambda i: (0, 0))],
        out_shape=[jax.ShapeDtypeStruct((n, d), F32), jax.ShapeDtypeStruct((1, d), F32),
                   jax.ShapeDtypeStruct((1, LANES), F32)],
        scratch_shapes=[pltpu.VMEM((SUBLANES, d), F32), pltpu.VMEM((SUBLANES, d), F32)],
        compiler_params=_params("arbitrary"),
    )(x, gain.reshape(1, d), target)
    return dx, dg.reshape(d), loss[0, 0]


def _rms_rows(xv):
    return lax.rsqrt(jnp.mean(xv * xv, axis=-1, keepdims=True) + EPS)


def _ffn_in_fwd(x, gain, w_in, name, comm=None):
    n, d = x.shape
    f = w_in.shape[1] // 2
    tm = _tile(n, 256, 16)
    tn = _tile(f, 4096, LANES)
    nj = f // tn

    def body(x_ref, gain_ref, wg_ref, wu_ref, h_ref, t_ref, q_ref, a_ref):
        @pl.when(pl.program_id(1) == 0)
        def _():
            xv = x_ref[...]
            h_ref[...] = (xv * _rms_rows(xv) * gain_ref[...]).astype(h_ref.dtype)

        h = h_ref[...]
        g = jnp.dot(h, wg_ref[...], preferred_element_type=F32)
        u = jnp.dot(h, wu_ref[...], preferred_element_type=F32)
        s = _sigmoid(g)
        t = g * s
        t_ref[...] = t.astype(t_ref.dtype)
        q_ref[...] = (u * (s + t * (1.0 - s))).astype(q_ref.dtype)
        a_ref[...] = (t * u).astype(a_ref.dtype)

    row = pl.BlockSpec((tm, d), lambda i, j: (i, 0))
    tile = pl.BlockSpec((tm, tn), lambda i, j: (i, j))
    act = jax.ShapeDtypeStruct((n, f), BF16)
    outs, comm_outs = _call(
        body, name=name, grid=(n // tm, nj),
        in_specs=[row, pl.BlockSpec((1, d), lambda i, j: (0, 0)),
                  pl.BlockSpec((d, tn), lambda i, j: (0, j)), pl.BlockSpec((d, tn), lambda i, j: (0, j + nj))],
        out_specs=[row, tile, tile, tile],
        out_shape=[jax.ShapeDtypeStruct((n, d), BF16), act, act, act],
        semantics=("parallel", "arbitrary"), args=(x, gain.reshape(1, d), w_in, w_in), comm=comm)
    return outs if comm is None else (outs, comm_outs)


def _ffn_out_bwd(dout, w_out, t, q, name, comm=None):
    n, d = dout.shape
    f = w_out.shape[0]
    tm = _tile(n, 256, 16)
    tn = _tile(f, 4096, LANES)

    def body(d_ref, w_ref, t_ref, q_ref, dg_ref, du_ref):
        da = 0.5 * lax.dot_general(d_ref[...].astype(BF16), w_ref[...], (((1,), (1,)), ((), ())),
                                   preferred_element_type=F32)
        dg_ref[...] = (da * q_ref[...].astype(F32)).astype(dg_ref.dtype)
        du_ref[...] = (da * t_ref[...].astype(F32)).astype(du_ref.dtype)

    tile = pl.BlockSpec((tm, tn), lambda i, j: (i, j))
    act = jax.ShapeDtypeStruct((n, f), BF16)
    outs, comm_outs = _call(
        body, name=name, grid=(n // tm, f // tn),
        in_specs=[pl.BlockSpec((tm, d), lambda i, j: (i, 0)), pl.BlockSpec((tn, d), lambda i, j: (j, 0)), tile, tile],
        out_specs=[tile, tile], out_shape=[act, act],
        semantics=("parallel", "parallel"), args=(dout, w_out, t, q), comm=comm)
    return outs if comm is None else (outs, comm_outs)


def _proj_in_bwd(parts, w, x, gain, dres, name, comm=None):
    n, d = x.shape
    tm = _tile(n, 512, 8)
    steps = n // tm
    np_ = len(parts)
    offs = [off for _, off in parts]
    widths = [a.shape[1] for a, _ in parts]

    def body(*refs):
        a_refs = refs[:np_]
        w_ref, x_ref, g_ref, dr_ref, dx_ref, dg_ref, acc_ref = refs[np_:]
        i = pl.program_id(0)
        dh = None
        for a_ref, off, kp in zip(a_refs, offs, widths):
            part = lax.dot_general(a_ref[...].astype(BF16), w_ref[:, off:off + kp], (((1,), (1,)), ((), ())),
                                   preferred_element_type=F32)
            dh = part if dh is None else dh + part
        xv = x_ref[...]
        r = _rms_rows(xv)
        xh = xv * r
        dyg = dh * g_ref[...]
        mean = jnp.mean(dyg * xh, axis=-1, keepdims=True)
        dx_ref[...] = dr_ref[...] + r * (dyg - xh * mean)
        part = jnp.sum((dh * xh).reshape(tm // SUBLANES, SUBLANES, d), axis=0)

        @pl.when(i == 0)
        def _():
            acc_ref[...] = part

        @pl.when(i > 0)
        def _():
            acc_ref[...] += part

        @pl.when(i == steps - 1)
        def _():
            dg_ref[...] = jnp.sum(acc_ref[...], axis=0, keepdims=True)

    row = pl.BlockSpec((tm, d), lambda i: (i, 0))
    vec = pl.BlockSpec((1, d), lambda i: (0, 0))
    (dx, dg), comm_outs = _call(
        body, name=name, grid=(steps,),
        in_specs=[pl.BlockSpec((tm, kp), lambda i: (i, 0)) for kp in widths]
        + [pl.BlockSpec(w.shape, lambda i: (0, 0), pipeline_mode=pl.Buffered(1)), row, vec, row],
        out_specs=[row, vec],
        out_shape=[jax.ShapeDtypeStruct((n, d), F32), jax.ShapeDtypeStruct((1, d), F32)],
        scratch_shapes=[pltpu.VMEM((SUBLANES, d), F32)],
        semantics=("arbitrary",), args=(*[a for a, _ in parts], w, x, gain.reshape(1, d), dres), comm=comm)
    return (dx, dg.reshape(d)) if comm is None else (dx, dg.reshape(d), comm_outs)


def _mix_in_fwd(x, gain, w, width, name, comm=None):
    n, d = x.shape
    cols = w.shape[1]
    tm = _tile(n, 512, 16)

    def body(x_ref, gain_ref, w_ref, h_ref, u_ref, z_ref):
        xv = x_ref[...]
        h = (xv * _rms_rows(xv) * gain_ref[...]).astype(h_ref.dtype)
        h_ref[...] = h
        z = jnp.dot(h, w_ref[...], preferred_element_type=F32)
        u_ref[...] = z[:, 0:width]
        z_ref[...] = z[:, width:cols]

    row = pl.BlockSpec((tm, d), lambda i: (i, 0))
    outs, comm_outs = _call(
        body, name=name, grid=(n // tm,),
        in_specs=[row, pl.BlockSpec((1, d), lambda i: (0, 0)), pl.BlockSpec((d, cols), lambda i: (0, 0))],
        out_specs=[row, pl.BlockSpec((tm, width), lambda i: (i, 0)), pl.BlockSpec((tm, cols - width), lambda i: (i, 0))],
        out_shape=[jax.ShapeDtypeStruct((n, d), BF16), jax.ShapeDtypeStruct((n, width), F32),
                   jax.ShapeDtypeStruct((n, cols - width), F32)],
        semantics=("parallel",), args=(x, gain.reshape(1, d), w), comm=comm)
    return outs if comm is None else (*outs, comm_outs)


def _tril_mask():
    t = lax.broadcasted_iota(jnp.int32, (GM_CHUNK, GM_CHUNK), 0)
    s = lax.broadcasted_iota(jnp.int32, (GM_CHUNK, GM_CHUNK), 1)
    return s <= t


def _gmlp_fwd(zgm, v_gain, w_s, bias_tile, name):
    n, w2 = zgm.shape
    w = w2 // 2
    heads = w // GM_HEAD_DIM
    tm = _tile(n, 512, GM_CHUNK)
    nq = tm // GM_CHUNK

    def body(u_ref, v_ref, gain_ref, w_ref, b_ref, o_ref):
        mask = _tril_mask()
        ug = _gelu(u_ref[...])
        vg = _gelu(v_ref[...])
        for h in range(heads):
            cols = slice(h * GM_HEAD_DIM, (h + 1) * GM_HEAD_DIM)
            vh = vg[:, cols]
            r = lax.rsqrt(jnp.mean(vh * vh, axis=-1, keepdims=True) + EPS)
            vn = (vh * r * gain_ref[:, cols]).astype(BF16)
            wm = jnp.where(mask, w_ref[h], 0.0).astype(BF16)
            for q in range(nq):
                rows = slice(q * GM_CHUNK, (q + 1) * GM_CHUNK)
                s = jnp.dot(wm, vn[rows], preferred_element_type=F32) + b_ref[:, cols]
                o_ref[rows, cols] = ug[rows, cols] * s

    return pl.pallas_call(
        body, name=name, grid=(n // tm,),
        in_specs=[pl.BlockSpec((tm, w), lambda i: (i, 0)), pl.BlockSpec((tm, w), lambda i: (i, 1)),
                  pl.BlockSpec((1, w), lambda i: (0, 0)),
                  pl.BlockSpec((heads, GM_CHUNK, GM_CHUNK), lambda i: (0, 0, 0)),
                  pl.BlockSpec((GM_CHUNK, w), lambda i: (0, 0))],
        out_specs=pl.BlockSpec((tm, w), lambda i: (i, 0)),
        out_shape=jax.ShapeDtypeStruct((n, w), F32),
        compiler_params=_params("parallel"),
    )(zgm, zgm, v_gain.reshape(1, w), w_s, bias_tile)


def _gmlp_bwd(zgm, dy, v_gain, w_s, bias_tile, name):
    n, w2 = zgm.shape
    w = w2 // 2
    heads = w // GM_HEAD_DIM
    tm = _tile(n, 512, GM_CHUNK)
    nq = tm // GM_CHUNK
    steps = n // tm

    def body(z_ref, dy_ref, gain_ref, w_ref, b_ref, dz_ref, dw_ref, db_ref, dgain_ref):
        i = pl.program_id(0)
        mask = _tril_mask()

        @pl.when(i == 0)
        def _():
            dw_ref[...] = jnp.zeros_like(dw_ref)
            db_ref[...] = jnp.zeros_like(db_ref)
            dgain_ref[...] = jnp.zeros_like(dgain_ref)

        ug, dug_du = _gelu_and_grad(z_ref[:, 0:w])
        vg, dvg_dv = _gelu_and_grad(z_ref[:, w:w2])
        dyv = dy_ref[...]
        for h in range(heads):
            cols = slice(h * GM_HEAD_DIM, (h + 1) * GM_HEAD_DIM)
            vh = vg[:, cols]
            r = lax.rsqrt(jnp.mean(vh * vh, axis=-1, keepdims=True) + EPS)
            vhat = vh * r
            gain = gain_ref[:, cols]
            vn = (vhat * gain).astype(BF16)
            wm = jnp.where(mask, w_ref[h], 0.0).astype(BF16)
            dvn_parts = []
            for q in range(nq):
                rows = slice(q * GM_CHUNK, (q + 1) * GM_CHUNK)
                s = jnp.dot(wm, vn[rows], preferred_element_type=F32) + b_ref[:, cols]
                dyq = dyv[rows, cols]
                dz_ref[rows, cols] = (dyq * s * dug_du[rows, cols]).astype(dz_ref.dtype)
                ds = dyq * ug[rows, cols]
                db_ref[:, cols] += ds
                dsb = ds.astype(BF16)
                dw_ref[h] += lax.dot_general(dsb, vn[rows], (((1,), (1,)), ((), ())), preferred_element_type=F32)
                dvn_parts.append(lax.dot_general(wm, dsb, (((0,), (0,)), ((), ())), preferred_element_type=F32))
            dvn = jnp.concatenate(dvn_parts, axis=0) if nq > 1 else dvn_parts[0]
            dgain_ref[:, cols] += jnp.sum(dvn * vhat, axis=0, keepdims=True)
            dvhat = dvn * gain
            mean = jnp.mean(dvhat * vhat, axis=-1, keepdims=True)
            dz_ref[:, w + h * GM_HEAD_DIM:w + (h + 1) * GM_HEAD_DIM] = (
                r * (dvhat - vhat * mean) * dvg_dv[:, cols]).astype(dz_ref.dtype)

        @pl.when(i == steps - 1)
        def _():
            for h in range(heads):
                dw_ref[h] = jnp.where(mask, dw_ref[h], 0.0)

    dz, dw, db, dgain = pl.pallas_call(
        body, name=name, grid=(steps,),
        in_specs=[pl.BlockSpec((tm, w2), lambda i: (i, 0)), pl.BlockSpec((tm, w), lambda i: (i, 0)),
                  pl.BlockSpec((1, w), lambda i: (0, 0)),
                  pl.BlockSpec((heads, GM_CHUNK, GM_CHUNK), lambda i: (0, 0, 0)),
                  pl.BlockSpec((GM_CHUNK, w), lambda i: (0, 0))],
        out_specs=[pl.BlockSpec((tm, w2), lambda i: (i, 0)),
                   pl.BlockSpec((heads, GM_CHUNK, GM_CHUNK), lambda i: (0, 0, 0)),
                   pl.BlockSpec((GM_CHUNK, w), lambda i: (0, 0)),
                   pl.BlockSpec((1, w), lambda i: (0, 0))],
        out_shape=[jax.ShapeDtypeStruct((n, w2), BF16), jax.ShapeDtypeStruct((heads, GM_CHUNK, GM_CHUNK), F32),
                   jax.ShapeDtypeStruct((GM_CHUNK, w), F32), jax.ShapeDtypeStruct((1, w), F32)],
        compiler_params=_params("arbitrary"),
    )(zgm, dy, v_gain.reshape(1, w), w_s, bias_tile)
    return dz, dw, db, dgain.reshape(w)


def _mix_out_fwd(y_ssm, y_gm, g1, g2, w_out, x, name, comm=None):
    n, w = y_ssm.shape
    d = w_out.shape[1]
    tm = _tile(n, 512, 16)

    def body(a_ref, b_ref, g1_ref, g2_ref, w_ref, x_ref, ycat_ref, o_ref):
        for src, g_ref, lo in ((a_ref, g1_ref, 0), (b_ref, g2_ref, w)):
            v = src[...]
            ycat_ref[:, lo:lo + w] = (v * _rms_rows(v) * g_ref[...]).astype(ycat_ref.dtype)
        o_ref[...] = x_ref[...] + jnp.dot(ycat_ref[...], w_ref[...], preferred_element_type=F32)

    row = pl.BlockSpec((tm, w), lambda i: (i, 0))
    vec = pl.BlockSpec((1, w), lambda i: (0, 0))
    outs, comm_outs = _call(
        body, name=name, grid=(n // tm,),
        in_specs=[row, row, vec, vec, pl.BlockSpec((2 * w, d), lambda i: (0, 0)), pl.BlockSpec((tm, d), lambda i: (i, 0))],
        out_specs=[pl.BlockSpec((tm, 2 * w), lambda i: (i, 0)), pl.BlockSpec((tm, d), lambda i: (i, 0))],
        out_shape=[jax.ShapeDtypeStruct((n, 2 * w), BF16), jax.ShapeDtypeStruct((n, d), F32)],
        semantics=("parallel",), args=(y_ssm, y_gm, g1.reshape(1, w), g2.reshape(1, w), w_out, x), comm=comm)
    return outs if comm is None else (*outs, comm_outs)


def _mix_out_bwd(dx, w_out, y_ssm, y_gm, g1, g2, name, comm=None):
    n, w = y_ssm.shape
    d = w_out.shape[1]
    tm = _tile(n, 512, 8)
    steps = n // tm

    def body(dx_ref, w_ref, a_ref, b_ref, g1_ref, g2_ref, da_ref, db_ref, dg1_ref, dg2_ref):
        i = pl.program_id(0)

        @pl.when(i == 0)
        def _():
            dg1_ref[...] = jnp.zeros_like(dg1_ref)
            dg2_ref[...] = jnp.zeros_like(dg2_ref)

        dycat = lax.dot_general(dx_ref[...].astype(BF16), w_ref[...], (((1,), (1,)), ((), ())),
                                preferred_element_type=F32)
        for src, g_ref, lo, dst, dg_ref in ((a_ref, g1_ref, 0, da_ref, dg1_ref), (b_ref, g2_ref, w, db_ref, dg2_ref)):
            v = src[...]
            dh = dycat[:, lo:lo + w]
            r = _rms_rows(v)
            vh = v * r
            dyg = dh * g_ref[...]
            mean = jnp.mean(dyg * vh, axis=-1, keepdims=True)
            dst[...] = r * (dyg - vh * mean)
            dg_ref[...] += jnp.sum(dh * vh, axis=0, keepdims=True)

    row = pl.BlockSpec((tm, w), lambda i: (i, 0))
    vec = pl.BlockSpec((1, w), lambda i: (0, 0))
    (da, db, dg1, dg2), comm_outs = _call(
        body, name=name, grid=(steps,),
        in_specs=[pl.BlockSpec((tm, d), lambda i: (i, 0)), pl.BlockSpec((2 * w, d), lambda i: (0, 0)), row, row, vec, vec],
        out_specs=[row, row, vec, vec],
        out_shape=[jax.ShapeDtypeStruct((n, w), F32), jax.ShapeDtypeStruct((n, w), F32),
                   jax.ShapeDtypeStruct((1, w), F32), jax.ShapeDtypeStruct((1, w), F32)],
        semantics=("arbitrary",), args=(dx, w_out, y_ssm, y_gm, g1.reshape(1, w), g2.reshape(1, w)), comm=comm)
    res = (da, db, dg1.reshape(w), dg2.reshape(w))
    return res if comm is None else (*res, comm_outs)


def _discretise(a_re, a_im, log_dt, bt_re, bt_im):
    dt = jnp.exp(log_dt)
    e = jnp.exp(a_re * dt)
    ang = a_im * dt
    lr = e * jnp.cos(ang)
    li = e * jnp.sin(ang)
    den = a_re * a_re + a_im * a_im
    cr = ((lr - 1.0) * a_re + li * a_im) / den
    ci = (li * a_re - (lr - 1.0) * a_im) / den
    cr3 = cr[:, None, :]
    ci3 = ci[:, None, :]
    return lr, li, cr3 * bt_re - ci3 * bt_im, cr3 * bt_im + ci3 * bt_re


def _disc_fwd(a_re, a_im, log_dt, bt_re, bt_im):
    g, p = a_re.shape
    c = bt_re.shape[1]

    def body(are_ref, aim_ref, ldt_ref, bre_ref, bim_ref, lr_ref, li_ref, bbr_ref, bbi_ref):
        lr, li, bbr, bbi = _discretise(are_ref[...], aim_ref[...], ldt_ref[...], bre_ref[...], bim_ref[...])
        lr_ref[...] = lr
        li_ref[...] = li
        bbr_ref[...] = bbr
        bbi_ref[...] = bbi

    return pl.pallas_call(
        body, name="s5_discretise",
        out_shape=[jax.ShapeDtypeStruct((g, p), F32), jax.ShapeDtypeStruct((g, p), F32),
                   jax.ShapeDtypeStruct((g, c, p), F32), jax.ShapeDtypeStruct((g, c, p), F32)],
    )(a_re, a_im, log_dt, bt_re, bt_im)


def _disc_bwd(a_re, a_im, log_dt, bt_re, bt_im, dlr, dli, dbbr, dbbi):
    g, p = a_re.shape
    c = bt_re.shape[1]

    def body(are_ref, aim_ref, ldt_ref, bre_ref, bim_ref, dlr_ref, dli_ref, dbbr_ref, dbbi_ref,
             dare_ref, daim_ref, dldt_ref, dbre_ref, dbim_ref):
        _, vjp = jax.vjp(_discretise, are_ref[...], aim_ref[...], ldt_ref[...], bre_ref[...], bim_ref[...])
        dare, daim, dldt, dbre, dbim = vjp((dlr_ref[...], dli_ref[...], dbbr_ref[...], dbbi_ref[...]))
        dare_ref[...] = dare
        daim_ref[...] = daim
        dldt_ref[...] = dldt
        dbre_ref[...] = dbre
        dbim_ref[...] = dbim

    return pl.pallas_call(
        body, name="s5_discretise_bwd",
        out_shape=[jax.ShapeDtypeStruct((g, p), F32), jax.ShapeDtypeStruct((g, p), F32),
                   jax.ShapeDtypeStruct((g, 1), F32),
                   jax.ShapeDtypeStruct((g, c, p), F32), jax.ShapeDtypeStruct((g, c, p), F32)],
    )(a_re, a_im, log_dt, bt_re, bt_im, dlr, dli, dbbr, dbbi)


def _block_diag(w, nb):
    g, a, b = w.shape
    gpb = g // nb
    eye = jnp.eye(gpb, dtype=w.dtype)
    w4 = w.reshape(nb, gpb, a, b)
    return jnp.einsum("ngab,gh->ngahb", w4, eye).reshape(nb, gpb * a, gpb * b)


def _block_diag_extract(m, gpb):
    nb, ga, gb = m.shape
    a, b = ga // gpb, gb // gpb
    m5 = m.reshape(nb, gpb, a, gpb, b)
    idx = jnp.arange(gpb)
    return m5[:, idx, :, idx, :].transpose(1, 0, 2, 3).reshape(nb * gpb, a, b)


def _ssm_operands(lr, li, bbr, bbi, c_re, c_im, d_skip, glu_w, glu_b):
    g = lr.shape[0]
    nb = g // GROUPS_PER_BLOCK
    s = STATES_PER_BLOCK
    lam = jnp.concatenate([lr.reshape(nb, 1, s), li.reshape(nb, 1, s)], axis=-1)
    b_bd = jnp.concatenate([_block_diag(bbr, nb), _block_diag(bbi, nb)], axis=-1)
    ct_re = jnp.swapaxes(c_re, 1, 2)
    ct_im = jnp.swapaxes(c_im, 1, 2)
    c_bd = jnp.concatenate([_block_diag(ct_re, nb), -_block_diag(ct_im, nb)], axis=1)
    dsk = d_skip.reshape(nb, 1, LANES)
    w_bd = jnp.concatenate([_block_diag(glu_w[:, :, :SSM_CH], nb), _block_diag(glu_w[:, :, SSM_CH:], nb)], axis=-1)
    bias = jnp.concatenate([glu_b[:, :SSM_CH].reshape(nb, 1, LANES), glu_b[:, SSM_CH:].reshape(nb, 1, LANES)], axis=-1)
    return lam, b_bd.astype(BF16), c_bd.astype(BF16), dsk, w_bd.astype(BF16), bias


def _roll_rows(v, shift):
    return v if shift % SUBLANES == 0 else pltpu.roll(v, shift % SUBLANES, 0)


def _scan_chunk_rows(seq, nseq):
    return _tile(seq, max(8 * SSM_TIME_CHUNK // nseq, 8), max(SUBLANES // nseq, 1) * 8)


def _stage_lams(lam_ref, e, nseq):
    s = STATES_PER_BLOCK
    lr = jnp.broadcast_to(lam_ref[e, :, 0:s], (SUBLANES, s))
    li = jnp.broadcast_to(lam_ref[e, :, s:2 * s], (SUBLANES, s))
    if nseq == SUBLANES:
        return [(lr, li)]
    row = lax.broadcasted_iota(jnp.int32, (SUBLANES, s), 0)
    out = []
    for j in range(SUBLANES // nseq):
        mine = jnp.logical_and(row >= j * nseq, row < (j + 1) * nseq)
        out.append((jnp.where(mine, lr, 0.0), jnp.where(mine, li, 0.0)))
    return out


def _scan_with(nblk, step, carry, between):
    runs = len(between)
    per = nblk // runs
    for i in range(runs):
        hi = nblk if i == runs - 1 else (i + 1) * per
        carry = lax.fori_loop(i * per, hi, step, carry, unroll=True)
        between[i]()
    return carry


def _ssm_fwd_pair(u, ops, nseq, name, comm=None):
    lam, b_bd, c_bd, dsk, w_bd, bias = ops
    rows_total, w = u.shape
    seq = rows_total // nseq
    nb = w // LANES
    s = STATES_PER_BLOCK
    tc = _scan_chunk_rows(seq, nseq)
    nk = seq // tc
    rows = tc * nseq
    nblk = rows // SUBLANES
    stages = SUBLANES // nseq
    two = 2 * LANES
    ncol = 4

    def body(u_ref, lam_ref, b_ref, c_ref, d_ref, w_ref, bias_ref, y_ref, hb_ref, buf_a, buf_b, st, rbuf_a, rbuf_b):
        k = pl.program_id(1)

        @pl.when(k == 0)
        def _():
            st[...] = jnp.zeros_like(st)

        hb_ref[...] = st[...]
        rbufs = (rbuf_a, rbuf_b)
        for q in range(nseq):
            for e in range(2):
                rbufs[e][pl.ds(q, tc, stride=nseq), :] = u_ref[q, :, e * LANES:(e + 1) * LANES]
        bufs = (buf_a, buf_b)

        def u_of(e):
            return rbufs[e][...]

        def project_in(e, j):
            cols = slice(j * (2 * s // ncol), (j + 1) * (2 * s // ncol))
            bufs[e][:, cols] = jnp.dot(u_of(e).astype(BF16), b_ref[e, :, cols], preferred_element_type=F32)

        def scan(e, between):
            buf = bufs[e]
            lams = _stage_lams(lam_ref, e, nseq)

            def step(i, carry):
                pr, pi = carry
                r0 = pl.multiple_of(i * SUBLANES, SUBLANES)
                outr = buf[pl.ds(r0, SUBLANES), 0:s]
                outi = buf[pl.ds(r0, SUBLANES), s:2 * s]
                for lr, li in lams:
                    rr = _roll_rows(pr, nseq)
                    ri = _roll_rows(pi, nseq)
                    outr = outr + (lr * rr - li * ri)
                    outi = outi + (lr * ri + li * rr)
                    pr, pi = outr, outi
                buf[pl.ds(r0, SUBLANES), 0:s] = outr
                buf[pl.ds(r0, SUBLANES), s:2 * s] = outi
                return outr, outi

            lo = e * 2 * s
            hr, hi = _scan_with(nblk, step, (st[:, lo:lo + s], st[:, lo + s:lo + 2 * s]), between)
            st[:, lo:lo + s] = hr
            st[:, lo + s:lo + 2 * s] = hi

        part = {}

        def project_out(e, j):
            ks = slice(j * (2 * s // ncol), (j + 1) * (2 * s // ncol))
            p = jnp.dot(bufs[e][:, ks].astype(BF16), c_ref[e, ks, :], preferred_element_type=F32)
            part[e] = p if j == 0 else part[e] + p

        def finish(e):
            y = part[e] + d_ref[e] * u_of(e)
            z = jnp.dot(_gelu(y).astype(BF16), w_ref[e], preferred_element_type=F32) + bias_ref[e]
            part[e] = z[:, 0:LANES] * _sigmoid(z[:, LANES:two])

        for j in range(ncol):
            project_in(0, j)
        scan(0, [functools.partial(project_in, 1, j) for j in range(ncol)])
        scan(1, [functools.partial(project_out, 0, j) for j in range(ncol)] + [functools.partial(finish, 0)])
        for j in range(ncol):
            project_out(1, j)
        finish(1)
        for e in range(2):
            rbufs[e][...] = part[e]
            for q in range(nseq):
                y_ref[q, :, e * LANES:(e + 1) * LANES] = rbufs[e][pl.ds(q, tc, stride=nseq), :]

    blk = lambda shape: pl.BlockSpec(shape, lambda b, k: (b, 0, 0))
    tok = pl.BlockSpec((nseq, tc, two), lambda b, k: (0, k, b))
    (y, hb), comm_outs = _call(
        body, name=name, grid=(nb // 2, nk),
        in_specs=[tok, blk((2, 1, 2 * s)), blk((2, LANES, 2 * s)), blk((2, 2 * s, LANES)),
                  blk((2, 1, LANES)), blk((2, LANES, two)), blk((2, 1, two))],
        out_specs=[tok, pl.BlockSpec((SUBLANES, 4 * s), lambda b, k: (k, b))],
        out_shape=[jax.ShapeDtypeStruct((nseq, seq, w), F32),
                   jax.ShapeDtypeStruct((nk * SUBLANES, nb * 2 * s), F32)],
        scratch_shapes=[pltpu.VMEM((rows, 2 * s), F32), pltpu.VMEM((rows, 2 * s), F32),
                        pltpu.VMEM((SUBLANES, 4 * s), F32), pltpu.VMEM((rows, LANES), F32),
                        pltpu.VMEM((rows, LANES), F32)],
        semantics=("parallel", "arbitrary"),
        args=(u.reshape(nseq, seq, w), lam, b_bd, c_bd, dsk, w_bd, bias), comm=comm)
    y = y.reshape(nseq * seq, w)
    return (y, hb) if comm is None else (y, hb, comm_outs)


def _ssm_bwd_pair(u, dout, hb, ops, nseq, name, comm=None):
    lam, b_bd, c_bd, dsk, w_bd, bias = ops
    rows_total, w = u.shape
    seq = rows_total // nseq
    nb = w // LANES
    s = STATES_PER_BLOCK
    tc = _scan_chunk_rows(seq, nseq)
    nk = seq // tc
    rows = tc * nseq
    nblk = rows // SUBLANES
    stages = SUBLANES // nseq
    two = 2 * LANES
    ncol = 4
    cw = 2 * s // ncol
    tn_dims = (((0,), (0,)), ((), ()))
    nt_dims = (((1,), (1,)), ((), ()))

    def body(u_ref, dy_ref, hb_ref, lam_ref, b_ref, c_ref, d_ref, w_ref, bias_ref,
             du_ref, dlam_ref, db_ref, dct_ref, dd_ref, dw_ref, dbias_ref,
             hbuf_a, hbuf_b, gbuf_a, gbuf_b, gst, lacc, ru_a, ru_b, rd_a, rd_b):
        k = pl.program_id(1)

        @pl.when(k == 0)
        def _():
            gst[...] = jnp.zeros_like(gst)
            lacc[...] = jnp.zeros_like(lacc)
            db_ref[...] = jnp.zeros_like(db_ref)
            dct_ref[...] = jnp.zeros_like(dct_ref)
            dd_ref[...] = jnp.zeros_like(dd_ref)
            dw_ref[...] = jnp.zeros_like(dw_ref)
            dbias_ref[...] = jnp.zeros_like(dbias_ref)

        hbufs, gbufs, rus, rds = (hbuf_a, hbuf_b), (gbuf_a, gbuf_b), (ru_a, ru_b), (rd_a, rd_b)
        for q in range(nseq):
            for e in range(2):
                rus[e][pl.ds(q, tc, stride=nseq), :] = u_ref[q, :, e * LANES:(e + 1) * LANES]
                rds[e][pl.ds(q, tc, stride=nseq), :] = dy_ref[q, :, e * LANES:(e + 1) * LANES]
        row = lax.broadcasted_iota(jnp.int32, (SUBLANES, s), 0)
        cols = [slice(j * cw, (j + 1) * cw) for j in range(ncol)]
        val = [{}, {}]

        def project_in(e, j):
            hbufs[e][:, cols[j]] = jnp.dot(rus[e][...].astype(BF16), b_ref[e, :, cols[j]], preferred_element_type=F32)

        def scan_fwd(e, between):
            buf = hbufs[e]
            lams = _stage_lams(lam_ref, e, nseq)

            def step(i, carry):
                pr, pi = carry
                r0 = pl.multiple_of(i * SUBLANES, SUBLANES)
                outr = buf[pl.ds(r0, SUBLANES), 0:s]
                outi = buf[pl.ds(r0, SUBLANES), s:2 * s]
                for lr, li in lams:
                    rr = _roll_rows(pr, nseq)
                    ri = _roll_rows(pi, nseq)
                    outr = outr + (lr * rr - li * ri)
                    outi = outi + (lr * ri + li * rr)
                    pr, pi = outr, outi
                buf[pl.ds(r0, SUBLANES), 0:s] = outr
                buf[pl.ds(r0, SUBLANES), s:2 * s] = outi
                return outr, outi

            lo = e * 2 * s
            _scan_with(nblk, step, (hb_ref[:, lo:lo + s], hb_ref[:, lo + s:lo + 2 * s]), between)

        def y_part(e, j):
            p = jnp.dot(hbufs[e][:, cols[j]].astype(BF16), c_ref[e, cols[j], :], preferred_element_type=F32)
            val[e]["y"] = p if j == 0 else val[e]["y"] + p

        def gate(e):
            v = val[e]
            uu = rus[e][...]
            yg, dyg_dy = _gelu_and_grad(v.pop("y") + d_ref[e] * uu)
            yg16 = yg.astype(BF16)
            z = jnp.dot(yg16, w_ref[e], preferred_element_type=F32) + bias_ref[e]
            sg = _sigmoid(z[:, LANES:two])
            dout_e = rds[e][...]
            dz = jnp.concatenate([dout_e * sg, dout_e * z[:, 0:LANES] * sg * (1.0 - sg)], axis=-1)
            dz16 = dz.astype(BF16)
            dw_ref[e] += lax.dot_general(yg16, dz16, tn_dims, preferred_element_type=F32)
            dbias_ref[e] += jnp.sum(dz, axis=0, keepdims=True)
            dy = lax.dot_general(dz16, w_ref[e], nt_dims, preferred_element_type=F32) * dyg_dy
            dd_ref[e] += jnp.sum(dy * uu, axis=0, keepdims=True)
            v["dy"] = dy
            v["dy16"] = dy.astype(BF16)

        def dc_part(e, j):
            dct_ref[e, :, cols[j]] += lax.dot_general(val[e]["dy16"], hbufs[e][:, cols[j]].astype(BF16), tn_dims,
                                                      preferred_element_type=F32)

        def dh_part(e, j):
            gbufs[e][:, cols[j]] = lax.dot_general(val[e]["dy16"], c_ref[e, cols[j], :], nt_dims,
                                                   preferred_element_type=F32)

        def scan_bwd(e, between):
            hbuf, gbuf = hbufs[e], gbufs[e]
            lams = _stage_lams(lam_ref, e, nseq)
            lo = e * 2 * s

            def step(i, carry):
                pr, pi, ar, ai = carry
                blk = nblk - 1 - i
                r0 = pl.multiple_of(blk * SUBLANES, SUBLANES)
                outr = gbuf[pl.ds(r0, SUBLANES), 0:s]
                outi = gbuf[pl.ds(r0, SUBLANES), s:2 * s]
                for lr, li in reversed(lams):
                    rr = _roll_rows(pr, SUBLANES - nseq)
                    ri = _roll_rows(pi, SUBLANES - nseq)
                    outr = outr + (lr * rr + li * ri)
                    outi = outi + (lr * ri - li * rr)
                    pr, pi = outr, outi
                gbuf[pl.ds(r0, SUBLANES), 0:s] = outr
                gbuf[pl.ds(r0, SUBLANES), s:2 * s] = outi
                p0 = pl.multiple_of(jnp.maximum(blk - 1, 0) * SUBLANES, SUBLANES)
                first = blk == 0
                before_r = jnp.where(first, hb_ref[:, lo:lo + s], hbuf[pl.ds(p0, SUBLANES), 0:s])
                before_i = jnp.where(first, hb_ref[:, lo + s:lo + 2 * s], hbuf[pl.ds(p0, SUBLANES), s:2 * s])
                if stages > 1:
                    last_rows = row >= SUBLANES - nseq
                    before_r = _roll_rows(jnp.where(last_rows, before_r, hbuf[pl.ds(r0, SUBLANES), 0:s]), nseq)
                    before_i = _roll_rows(jnp.where(last_rows, before_i, hbuf[pl.ds(r0, SUBLANES), s:2 * s]), nseq)
                return (outr, outi, ar + outr * before_r + outi * before_i, ai - outr * before_i + outi * before_r)

            gr, gi, ar, ai = _scan_with(
                nblk, step, (gst[:, lo:lo + s], gst[:, lo + s:lo + 2 * s], lacc[:, lo:lo + s], lacc[:, lo + s:lo + 2 * s]),
                between)
            gst[:, lo:lo + s] = gr
            gst[:, lo + s:lo + 2 * s] = gi
            lacc[:, lo:lo + s] = ar
            lacc[:, lo + s:lo + 2 * s] = ai

        def du_part(e, j):
            p = lax.dot_general(gbufs[e][:, cols[j]].astype(BF16), b_ref[e, :, cols[j]], nt_dims,
                                preferred_element_type=F32)
            val[e]["du"] = (val[e].pop("dy") * d_ref[e] + p) if j == 0 else val[e]["du"] + p

        def db_part(e, j):
            db_ref[e, :, cols[j]] += lax.dot_general(rus[e][...].astype(BF16), gbufs[e][:, cols[j]].astype(BF16),
                                                     tn_dims, preferred_element_type=F32)

        def parts(fn, e):
            return [functools.partial(fn, e, j) for j in range(ncol)]

        middle_of = lambda e: parts(y_part, e) + [functools.partial(gate, e)] + parts(dc_part, e) + parts(dh_part, e)
        last_of = lambda e: parts(du_part, e) + parts(db_part, e)
        for piece in parts(project_in, 0):
            piece()
        scan_fwd(0, parts(project_in, 1))
        scan_fwd(1, middle_of(0))
        scan_bwd(0, middle_of(1))
        scan_bwd(1, last_of(0))
        for piece in last_of(1):
            piece()
        for e in range(2):
            rus[e][...] = val[e]["du"]
            for q in range(nseq):
                du_ref[q, :, e * LANES:(e + 1) * LANES] = rus[e][pl.ds(q, tc, stride=nseq), :].astype(du_ref.dtype)

        @pl.when(k == nk - 1)
        def _():
            for e in range(2):
                dlam_ref[e] = jnp.sum(lacc[:, e * 2 * s:(e + 1) * 2 * s], axis=0, keepdims=True)

    blk = lambda shape: pl.BlockSpec(shape, lambda b, k: (b, 0, 0))
    tok = pl.BlockSpec((nseq, tc, two), lambda b, k: (0, nk - 1 - k, b))
    outs, comm_outs = _call(
        body, name=name, grid=(nb // 2, nk),
        in_specs=[tok, tok, pl.BlockSpec((SUBLANES, 4 * s), lambda b, k: (nk - 1 - k, b)),
                  blk((2, 1, 2 * s)), blk((2, LANES, 2 * s)), blk((2, 2 * s, LANES)),
                  blk((2, 1, LANES)), blk((2, LANES, two)), blk((2, 1, two))],
        out_specs=[tok, blk((2, 1, 2 * s)), blk((2, LANES, 2 * s)), blk((2, LANES, 2 * s)),
                   blk((2, 1, LANES)), blk((2, LANES, two)), blk((2, 1, two))],
        out_shape=[jax.ShapeDtypeStruct((nseq, seq, w), BF16),
                   jax.ShapeDtypeStruct((nb, 1, 2 * s), F32), jax.ShapeDtypeStruct((nb, LANES, 2 * s), F32),
                   jax.ShapeDtypeStruct((nb, LANES, 2 * s), F32), jax.ShapeDtypeStruct((nb, 1, LANES), F32),
                   jax.ShapeDtypeStruct((nb, LANES, two), F32), jax.ShapeDtypeStruct((nb, 1, two), F32)],
        scratch_shapes=[pltpu.VMEM((rows, 2 * s), F32)] * 4
        + [pltpu.VMEM((SUBLANES, 4 * s), F32), pltpu.VMEM((SUBLANES, 4 * s), F32)]
        + [pltpu.VMEM((rows, LANES), F32)] * 4,
        semantics=("parallel", "arbitrary"),
        args=(u.reshape(nseq, seq, w), dout.reshape(nseq, seq, w), hb, lam, b_bd, c_bd, dsk, w_bd, bias), comm=comm)
    outs[0] = outs[0].reshape(nseq * seq, w)
    return outs if comm is None else (outs, comm_outs)


ANY = pl.BlockSpec(memory_space=pl.ANY)

BIG = (("ffn1_w_in", True), ("ffn1_w_out", False), ("mix_w_in", True), ("mix_w_out", False),
       ("ffn2_w_in", True), ("ffn2_w_out", False))


def _my_place():
    return lax.axis_index("x"), lax.axis_index("y"), lax.axis_index("c")


def _other_chips(x, y):
    return [(1 - x, y), (x, 1 - y), (1 - x, 1 - y)]


def _half_of_shard(ref, col_sharded, chip, core):
    full_rows, full_cols = ref.shape
    if col_sharded:
        hr, cs = full_rows // 2, full_cols // N_CHIPS
        return ref.at[pl.ds(pl.multiple_of(core * hr, 8), hr), pl.ds(chip * cs, cs)]
    rs = full_rows // N_CHIPS
    return ref.at[pl.ds(pl.multiple_of(chip * rs + core * (rs // 2), 8), rs // 2), :]


def _gather_comm(shards, cols):
    full_shapes = [(sh.shape[0], sh.shape[1] * N_CHIPS) if col else (sh.shape[0] * N_CHIPS, sh.shape[1])
                   for sh, col in zip(shards, cols)]
    nw = len(shards)

    def first_copies(ins, outs, sems):
        send_sems, recv_sems, local_sems = sems
        x, y, c = _my_place()
        me = 2 * x + y
        locals_, sends = [], []
        for wi in range(nw):
            src, dst = ins[wi], outs[wi]
            rs, cs = src.shape
            hs = rs // 2
            if cols[wi]:
                place = dst.at[:, pl.ds(me * cs, cs)]
            else:
                place = dst.at[pl.ds(pl.multiple_of(me * rs, 8), rs), :]
            locals_.append(pltpu.make_async_copy(src, place, local_sems.at[wi]))
            my_half = src.at[pl.ds(pl.multiple_of(c * hs, 8), hs), :]
            for j, (px, py) in enumerate(_other_chips(x, y)):
                sends.append(pltpu.make_async_remote_copy(
                    src_ref=my_half, dst_ref=_half_of_shard(dst, cols[wi], me, c),
                    send_sem=send_sems.at[wi * 6 + j], recv_sem=recv_sems.at[wi * 6 + j],
                    device_id=(px, py, c), device_id_type=MESH))
        return locals_, sends

    def start(ins, outs, sems):
        locals_, sends = first_copies(ins, outs, sems)
        for cp in locals_ + sends:
            cp.start()

    def forwards(outs, sems, wait_landed):
        send_sems, recv_sems, _ = sems
        x, y, c = _my_place()
        out = []
        for wi in range(nw):
            dst = outs[wi]
            for j, (px, py) in enumerate(_other_chips(x, y)):
                got = _half_of_shard(dst, cols[wi], 2 * px + py, c)
                if wait_landed:
                    pltpu.make_async_remote_copy(
                        src_ref=got, dst_ref=got, send_sem=send_sems.at[wi * 6 + j], recv_sem=recv_sems.at[wi * 6 + j],
                        device_id=(px, py, c), device_id_type=MESH).wait_recv()
                out.append(pltpu.make_async_remote_copy(
                    src_ref=got, dst_ref=got, send_sem=send_sems.at[wi * 6 + 3 + j], recv_sem=recv_sems.at[wi * 6 + 3 + j],
                    device_id=(x, y, 1 - c), device_id_type=MESH))
                if wait_landed:
                    out[-1].start()
        return out

    def middle(ins, outs, sems):
        forwards(outs, sems, True)

    def finish(ins, outs, sems):
        send_sems, recv_sems, _ = sems
        x, y, c = _my_place()
        locals_, sends = first_copies(ins, outs, sems)
        for wi in range(nw):
            dst = outs[wi]
            for j, (px, py) in enumerate(_other_chips(x, y)):
                theirs = _half_of_shard(dst, cols[wi], 2 * px + py, 1 - c)
                pltpu.make_async_remote_copy(
                    src_ref=theirs, dst_ref=theirs, send_sem=send_sems.at[wi * 6 + 3 + j],
                    recv_sem=recv_sems.at[wi * 6 + 3 + j], device_id=(x, y, 1 - c), device_id_type=MESH).wait_recv()
        for cp in sends + forwards(outs, sems, False):
            cp.wait_send()
        for cp in locals_:
            cp.wait()

    return _Comm(shards, [jax.ShapeDtypeStruct(s, BF16) for s in full_shapes],
                 [pltpu.SemaphoreType.DMA((6 * nw,)), pltpu.SemaphoreType.DMA((6 * nw,)),
                  pltpu.SemaphoreType.DMA((nw,))], start, finish, middle=middle)


def _pair_exchange_comm(grads, cols):
    nw = len(grads)
    n_copies = sum(1 if col else N_CHIPS for col in cols)

    def copies(ins, outs, sems):
        send_sems, recv_sems = sems
        x, y, c = _my_place()
        out = []
        for wi in range(nw):
            src, dst = ins[wi], outs[wi]
            fr = src.shape[0]
            if cols[wi]:
                hr = fr // 2
                pieces = [(src.at[pl.ds(pl.multiple_of((1 - c) * hr, 8), hr), :], dst)]
            else:
                rs = fr // N_CHIPS
                hs = rs // 2
                pieces = [(src.at[pl.ds(pl.multiple_of(k * rs + (1 - c) * hs, 8), hs), :],
                           dst.at[pl.ds(k * hs, hs), :]) for k in range(N_CHIPS)]
            for s_ref, d_ref in pieces:
                out.append(pltpu.make_async_remote_copy(
                    src_ref=s_ref, dst_ref=d_ref, send_sem=send_sems.at[len(out)], recv_sem=recv_sems.at[len(out)],
                    device_id=(x, y, 1 - c), device_id_type=MESH))
        return out

    def start(ins, outs, sems):
        for cp in copies(ins, outs, sems):
            cp.start()

    def finish(ins, outs, sems):
        for cp in copies(ins, outs, sems):
            cp.wait()

    return _Comm(grads, [jax.ShapeDtypeStruct((g.shape[0] // 2, g.shape[1]), F32) for g in grads],
                 [pltpu.SemaphoreType.DMA((n_copies,)), pltpu.SemaphoreType.DMA((n_copies,))], start, finish)


def _pair_sum(grad, other, col, core, name):
    fr, fc = grad.shape
    pieces = 1 if col else N_CHIPS
    pr = fr // 2 // pieces
    gview = grad.reshape(pieces * 2, pr, fc)
    oview = other.reshape(pieces, pr, fc)
    tr = _tile(pr, 256, 16)

    def body(c_ref, g_ref, o_ref, out_ref):
        out_ref[...] = (g_ref[...] + o_ref[...]).astype(out_ref.dtype)

    out = pl.pallas_call(
        body, name=name,
        grid_spec=pltpu.PrefetchScalarGridSpec(
            num_scalar_prefetch=1, grid=(pieces, pr // tr),
            in_specs=[pl.BlockSpec((1, tr, fc), lambda p, i, cref: (p * 2 + cref[0], i, 0)),
                      pl.BlockSpec((1, tr, fc), lambda p, i, cref: (p, i, 0))],
            out_specs=pl.BlockSpec((1, tr, fc), lambda p, i, cref: (p, i, 0))),
        out_shape=jax.ShapeDtypeStruct((pieces, pr, fc), BF16),
        compiler_params=_params("parallel", "parallel"),
    )(core, gview, oview)
    return out.reshape(fr // 2, fc)


def _chip_exchange_comm(psums, cols):
    nw = len(psums)
    out_shapes = [(N_CHIPS, p.shape[0], p.shape[1] // N_CHIPS) if col else (N_CHIPS, p.shape[0] // N_CHIPS, p.shape[1])
                  for p, col in zip(psums, cols)]

    def copies(ins, outs, sems):
        send_sems, recv_sems, local_sems = sems
        x, y, c = _my_place()
        me = 2 * x + y
        out = []
        for wi in range(nw):
            src = ins[wi]
            mine = outs[wi].at[me]

            def piece(chip, src=src, col=cols[wi]):
                if col:
                    cs = src.shape[1] // N_CHIPS
                    return src.at[:, pl.ds(chip * cs, cs)]
                ps = src.shape[0] // N_CHIPS
                return src.at[pl.ds(pl.multiple_of(chip * ps, 8), ps), :]

            out.append(pltpu.make_async_copy(piece(me), mine, local_sems.at[wi]))
            for j, (px, py) in enumerate(_other_chips(x, y)):
                out.append(pltpu.make_async_remote_copy(
                    src_ref=piece(2 * px + py), dst_ref=mine,
                    send_sem=send_sems.at[wi * 3 + j], recv_sem=recv_sems.at[wi * 3 + j],
                    device_id=(px, py, c), device_id_type=MESH))
        return out

    def start(ins, outs, sems):
        for cp in copies(ins, outs, sems):
            cp.start()

    def finish(ins, outs, sems):
        for cp in copies(ins, outs, sems):
            cp.wait()

    return _Comm(psums, [jax.ShapeDtypeStruct(s, BF16) for s in out_shapes],
                 [pltpu.SemaphoreType.DMA((3 * nw,)), pltpu.SemaphoreType.DMA((3 * nw,)),
                  pltpu.SemaphoreType.DMA((nw,))], start, finish)


def _chip_sum(slots, core, layer, layers, into, name):
    _, hr, cs = slots.shape
    tr = _tile(hr, 256, 16)

    def body(c_ref, s_ref, *rest):
        out_ref = rest[-1]
        acc = s_ref[0].astype(F32)
        for i in range(1, N_CHIPS):
            acc = acc + s_ref[i].astype(F32)
        out_ref[0] = acc

    in_specs = [pl.BlockSpec((N_CHIPS, tr, cs), lambda i, cref: (0, i, 0))]
    args = [core, slots]
    aliases = {}
    if into is not None:
        in_specs.append(pl.BlockSpec(memory_space=pl.ANY))
        args.append(into.reshape(layers * 2, hr, cs))
        aliases = {2: 0}
    out = pl.pallas_call(
        body, name=name,
        grid_spec=pltpu.PrefetchScalarGridSpec(
            num_scalar_prefetch=1, grid=(hr // tr,), in_specs=in_specs,
            out_specs=pl.BlockSpec((1, tr, cs), lambda i, cref: (layer * 2 + cref[0], i, 0))),
        out_shape=jax.ShapeDtypeStruct((layers * 2, hr, cs), F32),
        input_output_aliases=aliases,
        compiler_params=_params("parallel"),
    )(*args)
    return out.reshape(layers, 2 * hr, cs)


def _pair_share_comm(reduced):
    nw = len(reduced)

    def copies(ins, outs, sems):
        send_sems, recv_sems = sems
        x, y, c = _my_place()
        out = []
        for wi in range(nw):
            hs = outs[wi].shape[1] // 2
            mine = outs[wi].at[:, pl.ds(pl.multiple_of(c * hs, 8), hs), :]
            out.append(pltpu.make_async_remote_copy(
                src_ref=mine, dst_ref=mine, send_sem=send_sems.at[wi], recv_sem=recv_sems.at[wi],
                device_id=(x, y, 1 - c), device_id_type=MESH))
        return out

    def start(ins, outs, sems):
        for cp in copies(ins, outs, sems):
            cp.start()

    def finish(ins, outs, sems):
        for cp in copies(ins, outs, sems):
            cp.wait()

    return _Comm(reduced, [jax.ShapeDtypeStruct(r.shape, F32) for r in reduced],
                 [pltpu.SemaphoreType.DMA((nw,)), pltpu.SemaphoreType.DMA((nw,))], start, finish,
                 alias={i: i for i in range(nw)})


def _all_reduce_small(flat, comm):
    rows, lanes = flat.shape
    seg = rows // N_DEV
    c_in, c_out = len(comm.ins), len(comm.outs)

    def body(*refs):
        refs = list(refs)
        in_ref, cins = refs[0], refs[1:1 + c_in]
        out_ref, couts = refs[1 + c_in], refs[2 + c_in:2 + c_in + c_out]
        recv_ref, send_sems, recv_sems = refs[2 + c_in + c_out:5 + c_in + c_out]
        csems = refs[5 + c_in + c_out:]
        comm.start(cins, couts, csems)
        x, y, c = _my_place()
        me = 4 * x + 2 * y + c

        def peer(r):
            fx, fy, fc = (r >> 2) & 1, (r >> 1) & 1, r & 1
            px = jnp.where(fx == 1, 1 - x, x)
            py = jnp.where(fy == 1, 1 - y, y)
            pc = jnp.where(fc == 1, 1 - c, c)
            return px, py, pc

        first = []
        for r in range(1, N_DEV):
            px, py, pc = peer(r)
            theirs = in_ref.at[pl.ds(pl.multiple_of((4 * px + 2 * py + pc) * seg, 8), seg), :]
            cp = pltpu.make_async_remote_copy(
                src_ref=theirs, dst_ref=recv_ref.at[r], send_sem=send_sems.at[r - 1], recv_sem=recv_sems.at[r - 1],
                device_id=(px, py, pc), device_id_type=MESH)
            cp.start()
            first.append(cp)
        for cp in first:
            cp.wait()
        my_rows = pl.ds(pl.multiple_of(me * seg, 8), seg)
        acc = in_ref[my_rows, :]
        for r in range(1, N_DEV):
            acc = acc + recv_ref[r]
        out_ref[my_rows, :] = acc
        second = []
        for r in range(1, N_DEV):
            px, py, pc = peer(r)
            cp = pltpu.make_async_remote_copy(
                src_ref=out_ref.at[my_rows, :], dst_ref=out_ref.at[my_rows, :],
                send_sem=send_sems.at[6 + r], recv_sem=recv_sems.at[6 + r],
                device_id=(px, py, pc), device_id_type=MESH)
            cp.start()
            second.append(cp)
        for r in range(1, N_DEV):
            px, py, pc = peer(r)
            theirs = out_ref.at[pl.ds(pl.multiple_of((4 * px + 2 * py + pc) * seg, 8), seg), :]
            pltpu.make_async_remote_copy(
                src_ref=theirs, dst_ref=theirs, send_sem=send_sems.at[6 + r], recv_sem=recv_sems.at[6 + r],
                device_id=(px, py, pc), device_id_type=MESH).wait_recv()
        for cp in second:
            cp.wait_send()
        comm.finish(cins, couts, csems)

    vm = pl.BlockSpec(memory_space=pltpu.VMEM)
    any_spec = pl.BlockSpec(memory_space=pl.ANY)
    outs = pl.pallas_call(
        body, name="all_reduce_small",
        in_specs=[vm] + [any_spec] * c_in, out_specs=[vm] + [any_spec] * c_out,
        out_shape=[jax.ShapeDtypeStruct((rows, lanes), F32)] + comm.outs,
        scratch_shapes=[pltpu.VMEM((N_DEV, seg, lanes), F32),
                        pltpu.SemaphoreType.DMA((2 * (N_DEV - 1),)), pltpu.SemaphoreType.DMA((2 * (N_DEV - 1),))]
        + comm.sems,
        input_output_aliases={1 + ci: 1 + co for ci, co in comm.alias.items()},
        compiler_params=pltpu.CompilerParams(vmem_limit_bytes=VMEM_LIMIT),
    )(flat, *comm.ins)
    return outs[0], list(outs[1:])


def _adamw_update(w_ref, g_ref, m_ref, v_ref, d_ref, nm_ref, nv_ref):
    c1 = 1.0 - ADAM_B1 ** ADAM_STEP
    c2 = 1.0 - ADAM_B2 ** ADAM_STEP
    gv = g_ref[...]
    nm = ADAM_B1 * m_ref[...] + (1.0 - ADAM_B1) * gv
    nv = ADAM_B2 * v_ref[...] + (1.0 - ADAM_B2) * (gv * gv)
    d_ref[...] = -ADAM_LR * ((nm / c1) / (jnp.sqrt(nv / c2) + ADAM_EPS) + ADAM_WD * w_ref[...])
    nm_ref[...] = nm
    nv_ref[...] = nv


def _adamw_many(ws, gs, ms, vs, name):
    n = len(ws)

    def body(*refs):
        for i in range(n):
            _adamw_update(*[refs[k * n + i] for k in range(7)])

    shapes = [jax.ShapeDtypeStruct(w.shape, F32) for w in ws]
    outs = pl.pallas_call(
        body, name=name, out_shape=shapes * 3,
        compiler_params=pltpu.CompilerParams(vmem_limit_bytes=VMEM_LIMIT),
    )(*ws, *gs, *ms, *vs)
    return outs[:n], outs[n:2 * n], outs[2 * n:]


def _adamw(w, g, m, v, name):
    rows, cols = w.shape
    tr = _tile(rows, 256, 8)

    def body(w_ref, g_ref, m_ref, v_ref, go_ref, d_ref, nm_ref, nv_ref):
        go_ref[...] = g_ref[...]
        _adamw_update(w_ref, g_ref, m_ref, v_ref, d_ref, nm_ref, nv_ref)

    blk = pl.BlockSpec((tr, cols), lambda i: (i, 0))
    sds = jax.ShapeDtypeStruct((rows, cols), F32)
    return pl.pallas_call(
        body, name=name, grid=(rows // tr,),
        in_specs=[blk] * 4, out_specs=[blk] * 4, out_shape=[sds] * 4,
        compiler_params=_params("parallel"),
    )(w, g, m, v)


SMALL = ("norm_ffn1", "norm_mix", "ssm_a_re", "ssm_a_im", "ssm_log_dt", "ssm_b_re", "ssm_b_im", "ssm_c_re",
         "ssm_c_im", "ssm_d", "ssm_glu_w", "ssm_glu_b", "gm_v_gain", "gm_w_s", "gm_b_s", "gain_ssm_out",
         "gain_gm_out", "norm_ffn2", "norm_final")
WEIGHTS = ("norm_ffn1", "ffn1_w_in", "ffn1_w_out", "norm_mix", "mix_w_in", "ssm_a_re", "ssm_a_im", "ssm_log_dt",
           "ssm_b_re", "ssm_b_im", "ssm_c_re", "ssm_c_im", "ssm_d", "ssm_glu_w", "ssm_glu_b", "gm_v_gain", "gm_w_s",
           "gm_b_s", "gain_ssm_out", "gain_gm_out", "mix_w_out", "norm_ffn2", "ffn2_w_in", "ffn2_w_out", "norm_final")


def _ffn_fwd(x, gain, w_in, w_out, tag, hosted=None):
    if hosted is None:
        h, t, q, a = _ffn_in_fwd(x, gain, w_in, f"{tag}_in")
    else:
        (h, t, q, a), got = _ffn_in_fwd(x, gain, w_in, f"{tag}_in_hosting", comm=hosted[0]())
        hosted[1](got)
    if callable(w_out):
        w_out = w_out()
    out = _matmul(a, w_out, "nn", scale=0.5, res=x, tm=512, tn=1024, tk=4096, name=f"{tag}_out")
    return out, (x, h, t, q, a)


def _ffn_bwd(dout, saved, gain, w_in, w_out, tag, hooks=None, publish=None, late_out_dw=False):
    x, h, t, q, a = saved
    f = t.shape[1]
    hooks = hooks or {}

    def hosted(key, fn, *args, name, **kw):
        if key not in hooks:
            return fn(*args, name=name, **kw)
        make, take = hooks[key]
        *res, got = fn(*args, name=f"{name}_hosting", comm=make(), **kw)
        take(got)
        return res[0] if len(res) == 1 else tuple(res)

    def out_dw():
        dw = hosted("out_dw", _matmul, a, dout, "tn", scale=0.5, tm=1536, tn=1024, tk=2048, name=f"{tag}_out_dw")
        if publish is not None:
            publish("out", dw)
        return dw

    dg, du = hosted("out_dx", _ffn_out_bwd, dout, w_out, t, q, name=f"{tag}_out_dx")
    if not late_out_dw:
        dw_out = out_dw()
    dw_in = hosted("in_dw_g", _matmul, h, dg, "tn", tm=512, tn=1536, tk=4096, name=f"{tag}_in_dw_g",
                   out_cols=2 * f)
    dw_in = hosted("in_dw_u", _matmul, h, du, "tn", tm=512, tn=1536, tk=4096, name=f"{tag}_in_dw_u",
                   out_cols=2 * f, col_off=f, into=dw_in)
    if publish is not None:
        publish("in", dw_in)
    if late_out_dw:
        dw_out = out_dw()
    dx, dgain = hosted("in_dx", _proj_in_bwd, [(dg, 0), (du, f)], w_in, x, gain, dout, name=f"{tag}_in_dx")
    return dx, dgain, dw_in, dw_out


def kernel(x, norm_ffn1, ffn1_w_in, ffn1_w_out, norm_mix, mix_w_in, ssm_a_re, ssm_a_im, ssm_log_dt, ssm_b_re, ssm_b_im, ssm_c_re, ssm_c_im, ssm_d, ssm_glu_w, ssm_glu_b, gm_v_gain, gm_w_s, gm_b_s, gain_ssm_out, gain_gm_out, mix_w_out, norm_ffn2, ffn2_w_in, ffn2_w_out, norm_final, loss_target, m_norm_ffn1, m_ffn1_w_in, m_ffn1_w_out, m_norm_mix, m_mix_w_in, m_ssm_a_re, m_ssm_a_im, m_ssm_log_dt, m_ssm_b_re, m_ssm_b_im, m_ssm_c_re, m_ssm_c_im, m_ssm_d, m_ssm_glu_w, m_ssm_glu_b, m_gm_v_gain, m_gm_w_s, m_gm_b_s, m_gain_ssm_out, m_gain_gm_out, m_mix_w_out, m_norm_ffn2, m_ffn2_w_in, m_ffn2_w_out, m_norm_final, v_norm_ffn1, v_ffn1_w_in, v_ffn1_w_out, v_norm_mix, v_mix_w_in, v_ssm_a_re, v_ssm_a_im, v_ssm_log_dt, v_ssm_b_re, v_ssm_b_im, v_ssm_c_re, v_ssm_c_im, v_ssm_d, v_ssm_glu_w, v_ssm_glu_b, v_gm_v_gain, v_gm_w_s, v_gm_b_s, v_gain_ssm_out, v_gain_gm_out, v_mix_w_out, v_norm_ffn2, v_ffn2_w_in, v_ffn2_w_out, v_norm_final):
    wts = dict(norm_ffn1=norm_ffn1, ffn1_w_in=ffn1_w_in, ffn1_w_out=ffn1_w_out, norm_mix=norm_mix, mix_w_in=mix_w_in,
               ssm_a_re=ssm_a_re, ssm_a_im=ssm_a_im, ssm_log_dt=ssm_log_dt, ssm_b_re=ssm_b_re, ssm_b_im=ssm_b_im,
               ssm_c_re=ssm_c_re, ssm_c_im=ssm_c_im, ssm_d=ssm_d, ssm_glu_w=ssm_glu_w, ssm_glu_b=ssm_glu_b,
               gm_v_gain=gm_v_gain, gm_w_s=gm_w_s, gm_b_s=gm_b_s, gain_ssm_out=gain_ssm_out, gain_gm_out=gain_gm_out,
               mix_w_out=mix_w_out, norm_ffn2=norm_ffn2, ffn2_w_in=ffn2_w_in, ffn2_w_out=ffn2_w_out,
               norm_final=norm_final)
    mom = dict(norm_ffn1=m_norm_ffn1, ffn1_w_in=m_ffn1_w_in, ffn1_w_out=m_ffn1_w_out, norm_mix=m_norm_mix,
               mix_w_in=m_mix_w_in, ssm_a_re=m_ssm_a_re, ssm_a_im=m_ssm_a_im, ssm_log_dt=m_ssm_log_dt,
               ssm_b_re=m_ssm_b_re, ssm_b_im=m_ssm_b_im, ssm_c_re=m_ssm_c_re, ssm_c_im=m_ssm_c_im, ssm_d=m_ssm_d,
               ssm_glu_w=m_ssm_glu_w, ssm_glu_b=m_ssm_glu_b, gm_v_gain=m_gm_v_gain, gm_w_s=m_gm_w_s, gm_b_s=m_gm_b_s,
               gain_ssm_out=m_gain_ssm_out, gain_gm_out=m_gain_gm_out, mix_w_out=m_mix_w_out, norm_ffn2=m_norm_ffn2,
               ffn2_w_in=m_ffn2_w_in, ffn2_w_out=m_ffn2_w_out, norm_final=m_norm_final)
    var = dict(norm_ffn1=v_norm_ffn1, ffn1_w_in=v_ffn1_w_in, ffn1_w_out=v_ffn1_w_out, norm_mix=v_norm_mix,
               mix_w_in=v_mix_w_in, ssm_a_re=v_ssm_a_re, ssm_a_im=v_ssm_a_im, ssm_log_dt=v_ssm_log_dt,
               ssm_b_re=v_ssm_b_re, ssm_b_im=v_ssm_b_im, ssm_c_re=v_ssm_c_re, ssm_c_im=v_ssm_c_im, ssm_d=v_ssm_d,
               ssm_glu_w=v_ssm_glu_w, ssm_glu_b=v_ssm_glu_b, gm_v_gain=v_gm_v_gain, gm_w_s=v_gm_w_s, gm_b_s=v_gm_b_s,
               gain_ssm_out=v_gain_ssm_out, gain_gm_out=v_gain_gm_out, mix_w_out=v_mix_w_out, norm_ffn2=v_norm_ffn2,
               ffn2_w_in=v_ffn2_w_in, ffn2_w_out=v_ffn2_w_out, norm_final=v_norm_final)

    nseq, seq, d = x.shape
    n = nseq * seq
    depth = norm_ffn1.shape[0]
    width = gain_ssm_out.shape[1]
    groups = ssm_a_re.shape[1]
    heads = gm_w_s.shape[1]
    core = lax.axis_index("c").astype(jnp.int32).reshape(1)

    is_col = dict(BIG)
    full = {name: [None] * depth for name, _ in BIG}

    def gather_comm(pairs):
        return _gather_comm([wts[nm][l].astype(BF16) for nm, l in pairs], [is_col[nm] for nm, _ in pairs])

    def store(pairs, arrays):
        for (nm, l), w in zip(pairs, arrays):
            full[nm][l] = w

    pairs = [("ffn1_w_in", 0)]
    store(pairs, _run_comm(gather_comm(pairs), "all_gather_first"))

    xs = x.reshape(n, d)
    saved = []
    for l in range(depth):
        pairs = [("ffn1_w_out", l)] + ([("mix_w_in", l), ("mix_w_out", l)] if l == 0 else [])
        x1, s_ffn1 = _ffn_fwd(xs, norm_ffn1[l], full["ffn1_w_in"][l], lambda l=l: full["ffn1_w_out"][l], "ffn1",
                              hosted=(functools.partial(gather_comm, pairs), functools.partial(store, pairs)))
        pairs = [("ffn2_w_out", l)]
        hm, u_ssm, zgm, got = _mix_in_fwd(x1, norm_mix[l], full["mix_w_in"][l], width, "mix_in",
                                          comm=gather_comm(pairs))
        store(pairs, got)
        bt_re = jnp.swapaxes(ssm_b_re[l], 1, 2)
        bt_im = jnp.swapaxes(ssm_b_im[l], 1, 2)
        disc_in = (ssm_a_re[l], ssm_a_im[l], ssm_log_dt[l].reshape(groups, 1), bt_re, bt_im)
        lr, li, bbr, bbi = _disc_fwd(*disc_in)
        ops = _ssm_operands(lr, li, bbr, bbi, ssm_c_re[l], ssm_c_im[l], ssm_d[l], ssm_glu_w[l], ssm_glu_b[l])
        pairs = [("ffn2_w_in", l)]
        y_ssm, hb, got = _ssm_fwd_pair(u_ssm, ops, nseq, "s5_fwd", comm=gather_comm(pairs))
        store(pairs, got)
        bias_tile = jnp.broadcast_to(gm_b_s[l].T[:, :, None], (GM_CHUNK, heads, GM_HEAD_DIM)).reshape(GM_CHUNK, width)
        y_gm = _gmlp_fwd(zgm, gm_v_gain[l], gm_w_s[l], bias_tile, "gmlp_fwd")
        if l + 1 < depth:
            pairs = [("mix_w_in", l + 1), ("mix_w_out", l + 1)]
            ycat, x2, got = _mix_out_fwd(y_ssm, y_gm, gain_ssm_out[l], gain_gm_out[l], full["mix_w_out"][l], x1,
                                         "mix_out_hosting", comm=gather_comm(pairs))
            store(pairs, got)
        else:
            ycat, x2 = _mix_out_fwd(y_ssm, y_gm, gain_ssm_out[l], gain_gm_out[l], full["mix_w_out"][l], x1, "mix_out")
        hosted = None
        if l + 1 < depth:
            pairs = [("ffn1_w_in", l + 1)]
            hosted = (functools.partial(gather_comm, pairs), functools.partial(store, pairs))
        x3, s_ffn2 = _ffn_fwd(x2, norm_ffn2[l], full["ffn2_w_in"][l], full["ffn2_w_out"][l], "ffn2", hosted=hosted)
        saved.append(dict(ffn1=s_ffn1, x1=x1, hm=hm, zgm=zgm, disc_in=disc_in, ops=ops, u_ssm=u_ssm, hb=hb, y_ssm=y_ssm,
                          bias_tile=bias_tile, y_gm=y_gm, ycat=ycat, ffn2=s_ffn2))
        xs = x3

    dx, g_norm_final, loss_part = _loss_head(xs, norm_final, loss_target.reshape(n, d))
    big = {name: [None] * depth for name, _ in BIG}
    small = {name: [None] * depth for name in SMALL if name != "norm_final"}
    gpb = GROUPS_PER_BLOCK
    s_blk = STATES_PER_BLOCK
    psum_of, reduced, grads = {}, {}, {}
    shared_early = ["ffn2_w_in", "ffn2_w_out", "mix_w_in", "mix_w_out"]

    def swap_comm(pairs):
        return _pair_exchange_comm([big[nm][l] for nm, l in pairs], [is_col[nm] for nm, _ in pairs])

    def take_swapped(pairs, others):
        for (nm, l), other in zip(pairs, others):
            psum_of[nm, l] = _pair_sum(big[nm][l], other, is_col[nm], core, f"grad_pair_sum_{nm}")

    def send_comm(pairs):
        return _chip_exchange_comm([psum_of[p] for p in pairs], [is_col[nm] for nm, _ in pairs])

    def take_sent(pairs, slots):
        for (nm, l), s in zip(pairs, slots):
            reduced[nm] = _chip_sum(s, core, l, depth, reduced.get(nm), f"grad_chip_sum_{nm}")

    def hosting(make, take, pairs):
        return functools.partial(make, pairs), functools.partial(take, pairs)

    for l in reversed(range(depth)):
        sv = saved[l]
        above = [(nm, l + 1) for nm in ("mix_w_in", "mix_w_out", "ffn1_w_in", "ffn1_w_out")] if l + 1 < depth else []
        dx, small["norm_ffn2"][l], big["ffn2_w_in"][l], big["ffn2_w_out"][l] = _ffn_bwd(
            dx, sv["ffn2"], norm_ffn2[l], full["ffn2_w_in"][l], full["ffn2_w_out"][l], "ffn2",
            hooks={"out_dx": hosting(swap_comm, take_swapped, above)} if above else None)
        mine = [("ffn2_w_in", l), ("ffn2_w_out", l)]
        dy_ssm, dy_gm, small["gain_ssm_out"][l], small["gain_gm_out"][l], got = _mix_out_bwd(
            dx, full["mix_w_out"][l], sv["y_ssm"], sv["y_gm"], gain_ssm_out[l], gain_gm_out[l], "mix_out_dx",
            comm=swap_comm(mine))
        take_swapped(mine, got)
        big["mix_w_out"][l] = _matmul(sv["ycat"], dx, "tn", tm=1024, tn=1024, tk=2048, name="mix_out_dw")
        dzgm, small["gm_w_s"][l], dbias_tile, small["gm_v_gain"][l] = _gmlp_bwd(
            sv["zgm"], dy_gm, gm_v_gain[l], gm_w_s[l], sv["bias_tile"], "gmlp_bwd")
        small["gm_b_s"][l] = dbias_tile.reshape(GM_CHUNK, heads, GM_HEAD_DIM).sum(-1).T
        (du_ssm, dlam, db_bd, dct_bd, dd, dw_bd, dbias), got = _ssm_bwd_pair(
            sv["u_ssm"], dy_ssm, sv["hb"], sv["ops"], nseq, "s5_bwd", comm=send_comm(mine + above))
        take_sent(mine + above, got)
        dlr = dlam[:, 0, :s_blk].reshape(groups, SSM_STATE)
        dli = dlam[:, 0, s_blk:].reshape(groups, SSM_STATE)
        dbbr = _block_diag_extract(db_bd[:, :, :s_blk], gpb)
        dbbi = _block_diag_extract(db_bd[:, :, s_blk:], gpb)
        da_re, da_im, dldt, dbt_re, dbt_im = _disc_bwd(*sv["disc_in"], dlr, dli, dbbr, dbbi)
        small["ssm_a_re"][l], small["ssm_a_im"][l], small["ssm_log_dt"][l] = da_re, da_im, dldt.reshape(groups)
        small["ssm_b_re"][l] = jnp.swapaxes(dbt_re, 1, 2)
        small["ssm_b_im"][l] = jnp.swapaxes(dbt_im, 1, 2)
        small["ssm_c_re"][l] = _block_diag_extract(dct_bd[:, :, :s_blk], gpb)
        small["ssm_c_im"][l] = -_block_diag_extract(dct_bd[:, :, s_blk:], gpb)
        small["ssm_d"][l] = dd.reshape(groups, SSM_CH)
        small["ssm_glu_w"][l] = jnp.concatenate(
            [_block_diag_extract(dw_bd[:, :, :LANES], gpb), _block_diag_extract(dw_bd[:, :, LANES:], gpb)], axis=-1)
        small["ssm_glu_b"][l] = jnp.concatenate(
            [dbias[:, 0, :LANES].reshape(groups, SSM_CH), dbias[:, 0, LANES:].reshape(groups, SSM_CH)], axis=-1)
        cols_mi = 3 * width
        dw_mi = _matmul(sv["hm"], du_ssm, "tn", tm=1024, tn=width, tk=2048, name="mix_in_dw_ssm", out_cols=cols_mi)
        big["mix_w_in"][l] = _matmul(sv["hm"], dzgm, "tn", tm=1024, tn=width, tk=2048, name="mix_in_dw_gm",
                                     out_cols=cols_mi, col_off=width, into=dw_mi)
        dx, small["norm_mix"][l] = _proj_in_bwd([(du_ssm, 0), (dzgm, width)], full["mix_w_in"][l], sv["x1"],
                                                norm_mix[l], dx, "mix_in_dx")
        hooks = None
        if l == 0:
            mix, w_out_0, w_in_0 = [("mix_w_in", 0), ("mix_w_out", 0)], [("ffn1_w_out", 0)], [("ffn1_w_in", 0)]

            def last_make():
                return _merge_comms(_merge_comms(send_comm(w_in_0), swap_comm(w_out_0)),
                                    _pair_share_comm([reduced[nm] for nm in shared_early]))

            def last_take(got):
                take_sent(w_in_0, got[:1])
                take_swapped(w_out_0, got[1:2])
                grads.update(zip(shared_early, got[2:]))

            hooks = {"out_dx": hosting(swap_comm, take_swapped, mix), "in_dw_g": hosting(send_comm, take_sent, mix),
                     "out_dw": hosting(swap_comm, take_swapped, w_in_0), "in_dx": (last_make, last_take)}

        def publish(which, dw, l=l):
            big[f"ffn1_w_{which}"][l] = dw

        dx, small["norm_ffn1"][l], big["ffn1_w_in"][l], big["ffn1_w_out"][l] = _ffn_bwd(
            dx, sv["ffn1"], norm_ffn1[l], full["ffn1_w_in"][l], full["ffn1_w_out"][l], "ffn1",
            hooks=hooks, publish=publish, late_out_dw=(l == 0))
    grad_x = dx.reshape(nseq, seq, d)

    pieces = [jnp.stack(small[name]).reshape(-1) for name in SMALL if name != "norm_final"]
    pieces += [g_norm_final.reshape(-1), loss_part.reshape(1)]
    sizes = [p.shape[0] for p in pieces]
    total = sum(sizes)
    rows = -(-total // (LANES * N_DEV * SUBLANES)) * N_DEV * SUBLANES
    pad = rows * LANES - total
    tail = [("ffn1_w_out", 0)]
    flat_g, got = _all_reduce_small(
        jnp.concatenate(pieces + [jnp.zeros((pad,), F32)]).reshape(rows, LANES), send_comm(tail))
    take_sent(tail, got)
    flat_g = flat_g.reshape(-1)
    loss = flat_g[total - 1]

    names = [name for name, _ in BIG if name not in shared_early]
    grads.update(zip(names, _run_comm(_pair_share_comm([reduced[nm] for nm in names]), "grad_pair_share")))
    offs = 0
    for name, size in zip(SMALL, sizes[:-1]):
        grads[name] = flat_g[offs:offs + size].reshape(wts[name].shape)
        offs += size

    delta, new_m, new_v = {}, {}, {}
    for name, _ in BIG:
        shape = wts[name].shape
        two_d = lambda a: a.reshape(shape[0] * shape[1], shape[2])
        go, dl, nm, nv = _adamw(two_d(wts[name]), two_d(grads[name]), two_d(mom[name]), two_d(var[name]),
                                f"adamw_{name}")
        grads[name] = go.reshape(shape)
        delta[name], new_m[name], new_v[name] = dl.reshape(shape), nm.reshape(shape), nv.reshape(shape)
    at_least_2d = lambda a: a.reshape(1, -1) if a.ndim == 1 else a
    dls, nms, nvs = _adamw_many(*[[at_least_2d(tree[k]) for k in SMALL] for tree in (wts, grads, mom, var)],
                                "adamw_small")
    for name, dl, nm, nv in zip(SMALL, dls, nms, nvs):
        shape = wts[name].shape
        delta[name], new_m[name], new_v[name] = dl.reshape(shape), nm.reshape(shape), nv.reshape(shape)

    return (loss, grad_x, *[grads[k] for k in WEIGHTS], *[delta[k] for k in WEIGHTS],
            *[new_m[k] for k in WEIGHTS], *[new_v[k] for k in WEIGHTS])
```

```python
import functools
import math

import jax
import jax.numpy as jnp
from jax import lax
from jax.experimental import pallas as pl
from jax.experimental.pallas import tpu as pltpu

F32 = jnp.float32
BF16 = jnp.bfloat16
MESH = pl.DeviceIdType.MESH

EPS = 1e-6
SSM_CH = 16
SSM_STATE = 64
GM_CHUNK = 128
GM_HEAD_DIM = 128
SUBLANES = 8
LANES = 128
GROUPS_PER_BLOCK = LANES // SSM_CH
STATES_PER_BLOCK = GROUPS_PER_BLOCK * SSM_STATE
SSM_TIME_CHUNK = 128
N_CHIPS = 4
N_DEV = 8

ADAM_LR = 0.001
ADAM_B1 = 0.9
ADAM_B2 = 0.999
ADAM_EPS = 1e-08
ADAM_WD = 0.01
ADAM_STEP = 10

VMEM_LIMIT = 56 * 1024 * 1024


def _tile(dim, pref, align):
    best = None
    t = align
    while t <= min(dim, pref):
        if dim % t == 0:
            best = t
        t += align
    return best if best is not None else dim


def _params(*sem):
    return pltpu.CompilerParams(dimension_semantics=sem, vmem_limit_bytes=VMEM_LIMIT)


def _gelu(x):
    c = math.sqrt(2.0 / math.pi)
    return 0.5 * x * (1.0 + jnp.tanh(c * (x + 0.044715 * x * x * x)))


def _gelu_and_grad(x):
    c = math.sqrt(2.0 / math.pi)
    t = jnp.tanh(c * (x + 0.044715 * x * x * x))
    g = 0.5 * x * (1.0 + t)
    dg = 0.5 * (1.0 + t) + 0.5 * x * (1.0 - t * t) * c * (1.0 + 3.0 * 0.044715 * x * x)
    return g, dg


def _sigmoid(x):
    return 0.5 * jnp.tanh(0.5 * x) + 0.5


def _matmul(a, b, mode, *, out_dtype=F32, scale=1.0, res=None, tm=512, tn=1024, tk=1024, name="mm",
            out_cols=None, col_off=0, into=None, comm=None):
    if mode == "nn":
        (m, k), (k2, n) = a.shape, b.shape
    elif mode == "nt":
        (m, k), (n, k2) = a.shape, b.shape
    else:
        (k, m), (k2, n) = a.shape, b.shape
    assert k == k2, (a.shape, b.shape, mode)
    tm = _tile(m, tm, 16 if mode != "tn" else LANES)
    tn = _tile(n, tn, LANES)
    tk = _tile(k, tk, LANES if mode != "tn" else 16)
    nk = k // tk
    grid = (m // tm, n // tn, nk)
    if mode == "nn":
        a_spec = pl.BlockSpec((tm, tk), lambda i, j, kk: (i, kk))
        b_spec = pl.BlockSpec((tk, tn), lambda i, j, kk: (kk, j))
        dims = (((1,), (0,)), ((), ()))
    elif mode == "nt":
        a_spec = pl.BlockSpec((tm, tk), lambda i, j, kk: (i, kk))
        b_spec = pl.BlockSpec((tn, tk), lambda i, j, kk: (j, kk))
        dims = (((1,), (1,)), ((), ()))
    else:
        a_spec = pl.BlockSpec((tk, tm), lambda i, j, kk: (kk, i))
        b_spec = pl.BlockSpec((tk, tn), lambda i, j, kk: (kk, j))
        dims = (((0,), (0,)), ((), ()))
    assert col_off % tn == 0
    off = col_off // tn
    r_spec = pl.BlockSpec((tm, tn), lambda i, j, kk: (i, j))
    o_spec = pl.BlockSpec((tm, tn), lambda i, j, kk: (i, j + off))
    has_res = res is not None
    has_into = into is not None

    def body(*refs):
        refs = list(refs)
        a_ref, b_ref = refs[:2]
        pos = 2
        r_ref = None
        if has_res:
            r_ref = refs[pos]
            pos += 1
        if has_into:
            pos += 1
        o_ref = refs[pos]
        acc_ref = refs[pos + 1] if nk > 1 else None
        part = lax.dot_general(a_ref[...].astype(BF16), b_ref[...].astype(BF16), dims,
                               preferred_element_type=F32)

        def finish(r):
            if scale != 1.0:
                r = r * scale
            if has_res:
                r = r + r_ref[...].astype(F32)
            o_ref[...] = r.astype(o_ref.dtype)

        if nk == 1:
            finish(part)
        else:
            kk = pl.program_id(2)

            @pl.when(kk == 0)
            def _():
                acc_ref[...] = part

            @pl.when(kk > 0)
            def _():
                acc_ref[...] += part

            @pl.when(kk == nk - 1)
            def _():
                finish(acc_ref[...])

    in_specs = [a_spec, b_spec]
    args = [a, b]
    if has_res:
        in_specs.append(r_spec)
        args.append(res)
    aliases = {}
    if has_into:
        in_specs.append(pl.BlockSpec(memory_space=pl.ANY))
        args.append(into)
        aliases = {len(args) - 1: 0}
    (out,), comm_outs = _call(
        body, name=name, grid=grid, in_specs=in_specs, out_specs=[o_spec],
        out_shape=[jax.ShapeDtypeStruct((m, n if out_cols is None else out_cols), out_dtype)],
        scratch_shapes=[pltpu.VMEM((tm, tn), F32)] if nk > 1 else [],
        aliases=aliases, semantics=("parallel", "parallel", "arbitrary"), args=args, comm=comm)
    return out if comm is None else (out, comm_outs)


class _Comm:
    def __init__(self, ins, outs, sems, start, finish, alias=None, middle=None):
        self.ins, self.outs, self.sems, self.start, self.finish = list(ins), list(outs), list(sems), start, finish
        self.alias = dict(alias or {})
        self.middle = middle


def _merge_comms(a, b):
    assert a.middle is None and b.middle is None
    cut = (len(a.ins), len(a.outs), len(a.sems))

    def both(which):
        def run(ins, outs, sems):
            getattr(a, which)(ins[:cut[0]], outs[:cut[1]], sems[:cut[2]])
            getattr(b, which)(ins[cut[0]:], outs[cut[1]:], sems[cut[2]:])
        return run

    alias = dict(a.alias)
    alias.update({cut[0] + ci: cut[1] + co for ci, co in b.alias.items()})
    return _Comm(a.ins + b.ins, a.outs + b.outs, a.sems + b.sems, both("start"), both("finish"), alias=alias)


def _call(body, *, name, grid, in_specs, out_specs, out_shape, args, scratch_shapes=(), semantics=(), aliases=None,
          comm=None):
    in_specs, out_specs, out_shape = list(in_specs), list(out_specs), list(out_shape)
    scratch_shapes = list(scratch_shapes)
    aliases = dict(aliases or {})
    if comm is None:
        outs = pl.pallas_call(
            body, name=name, grid=grid, in_specs=in_specs, out_specs=out_specs, out_shape=out_shape,
            scratch_shapes=scratch_shapes, input_output_aliases=aliases, compiler_params=_params(*semantics),
        )(*args)
        return list(outs), []
    n_in, n_out, n_scr = len(in_specs), len(out_specs), len(scratch_shapes)
    c_in, c_out = len(comm.ins), len(comm.outs)
    for ci, co in comm.alias.items():
        aliases[n_in + ci] = n_out + co

    def hosted(*refs):
        refs = list(refs)
        ins, cins = refs[:n_in], refs[n_in:n_in + c_in]
        p = n_in + c_in
        outs, couts = refs[p:p + n_out], refs[p + n_out:p + n_out + c_out]
        p += n_out + c_out
        scr, sems = refs[p:p + n_scr], refs[p + n_scr:]
        ids = [pl.program_id(a) for a in range(len(grid))]
        first = functools.reduce(jnp.logical_and, [i == 0 for i in ids])
        last = functools.reduce(jnp.logical_and, [i == g - 1 for i, g in zip(ids, grid)])

        total = math.prod(grid)
        late = comm.middle is not None and total >= 4

        @pl.when(first)
        def _():
            comm.start(cins, couts, sems)

        if late:
            flat = functools.reduce(lambda acc, ig: acc * ig[1] + ig[0], zip(ids, grid), 0)

            @pl.when(flat == (3 * total) // 4)
            def _():
                comm.middle(cins, couts, sems)

        body(*ins, *outs, *scr)

        @pl.when(last)
        def _():
            if comm.middle is not None and not late:
                comm.middle(cins, couts, sems)
            comm.finish(cins, couts, sems)

    any_spec = pl.BlockSpec(memory_space=pl.ANY)
    outs = pl.pallas_call(
        hosted, name=name, grid=grid, in_specs=in_specs + [any_spec] * c_in, out_specs=out_specs + [any_spec] * c_out,
        out_shape=out_shape + comm.outs, scratch_shapes=scratch_shapes + comm.sems, input_output_aliases=aliases,
        compiler_params=_params(*(["arbitrary"] * len(grid))),
    )(*args, *comm.ins)
    return list(outs[:n_out]), list(outs[n_out:])


def _run_comm(comm, name):
    c_in, c_out = len(comm.ins), len(comm.outs)

    def body(*refs):
        refs = list(refs)
        cins, couts, sems = refs[:c_in], refs[c_in:c_in + c_out], refs[c_in + c_out:]
        comm.start(cins, couts, sems)
        if comm.middle is not None:
            comm.middle(cins, couts, sems)
        comm.finish(cins, couts, sems)

    any_spec = pl.BlockSpec(memory_space=pl.ANY)
    return list(pl.pallas_call(
        body, name=name, in_specs=[any_spec] * c_in, out_specs=[any_spec] * c_out, out_shape=comm.outs,
        scratch_shapes=comm.sems, input_output_aliases=comm.alias,
    )(*comm.ins))


def _loss_head(x, gain, target):
    n, d = x.shape
    tm = _tile(n, 512, 8)
    steps = n // tm

    def body(x_ref, g_ref, t_ref, dx_ref, dg_ref, loss_ref, acc_ref, lacc_ref):
        i = pl.program_id(0)
        xv = x_ref[...]
        g = g_ref[...]
        r = lax.rsqrt(jnp.mean(xv * xv, axis=-1, keepdims=True) + EPS)
        xh = xv * r
        err = xh * g - t_ref[...]
        dy = err * (1.0 / d)
        dyg = dy * g
        mean = jnp.mean(dyg * xh, axis=-1, keepdims=True)
        dx_ref[...] = r * (dyg - xh * mean)
        part = jnp.sum((dy * xh).reshape(tm // SUBLANES, SUBLANES, d), axis=0)
        lpart = jnp.sum((err * err).reshape(tm // SUBLANES, SUBLANES, d), axis=0)

        @pl.when(i == 0)
        def _():
            acc_ref[...] = part
            lacc_ref[...] = lpart

        @pl.when(i > 0)
        def _():
            acc_ref[...] += part
            lacc_ref[...] += lpart

        @pl.when(i == steps - 1)
        def _():
            dg_ref[...] = jnp.sum(acc_ref[...], axis=0, keepdims=True)
            tot = jnp.sum(jnp.sum(lacc_ref[...], axis=0, keepdims=True), axis=1, keepdims=True)
            loss_ref[...] = jnp.broadcast_to(tot * (0.5 / d), loss_ref.shape)

    row = pl.BlockSpec((tm, d), lambda i: (i, 0))
    vec = pl.BlockSpec((1, d), lambda i: (0, 0))
    dx, dg, loss = pl.pallas_call(
        body, name="loss_head", grid=(steps,),
        in_specs=[row, vec, row],
        out_specs=[row, vec, pl.BlockSpec((1, LANES), lambda i: (0, 0))],
        out_shape=[jax.ShapeDtypeStruct((n, d), F32), jax.ShapeDtypeStruct((1, d), F32),
                   jax.ShapeDtypeStruct((1, LANES), F32)],
        scratch_shapes=[pltpu.VMEM((SUBLANES, d), F32), pltpu.VMEM((SUBLANES, d), F32)],
        compiler_params=_params("arbitrary"),
    )(x, gain.reshape(1, d), target)
    return dx, dg.reshape(d), loss[0, 0]


def _rms_rows(xv):
    return lax.rsqrt(jnp.mean(xv * xv, axis=-1, keepdims=True) + EPS)


def _ffn_in_fwd(x, gain, w_in, name, comm=None):
    n, d = x.shape
    f = w_in.shape[1] // 2
    tm = _tile(n, 256, 16)
    tn = _tile(f, 4096, LANES)
    nj = f // tn

    def body(x_ref, gain_ref, wg_ref, wu_ref, h_ref, t_ref, q_ref, a_ref):
        @pl.when(pl.program_id(1) == 0)
        def _():
            xv = x_ref[...]
            h_ref[...] = (xv * _rms_rows(xv) * gain_ref[...]).astype(h_ref.dtype)

        h = h_ref[...]
        g = jnp.dot(h, wg_ref[...], preferred_element_type=F32)
        u = jnp.dot(h, wu_ref[...], preferred_element_type=F32)
        s = _sigmoid(g)
        t = g * s
        t_ref[...] = t.astype(t_ref.dtype)
        q_ref[...] = (u * (s + t * (1.0 - s))).astype(q_ref.dtype)
        a_ref[...] = (t * u).astype(a_ref.dtype)

    row = pl.BlockSpec((tm, d), lambda i, j: (i, 0))
    tile = pl.BlockSpec((tm, tn), lambda i, j: (i, j))
    act = jax.ShapeDtypeStruct((n, f), BF16)
    outs, comm_outs = _call(
        body, name=name, grid=(n // tm, nj),
        in_specs=[row, pl.BlockSpec((1, d), lambda i, j: (0, 0)),
                  pl.BlockSpec((d, tn), lambda i, j: (0, j)), pl.BlockSpec((d, tn), lambda i, j: (0, j + nj))],
        out_specs=[row, tile, tile, tile],
        out_shape=[jax.ShapeDtypeStruct((n, d), BF16), act, act, act],
        semantics=("parallel", "arbitrary"), args=(x, gain.reshape(1, d), w_in, w_in), comm=comm)
    return outs if comm is None else (outs, comm_outs)


def _ffn_out_bwd(dout, w_out, t, q, name, comm=None):
    n, d = dout.shape
    f = w_out.shape[0]
    tm = _tile(n, 256, 16)
    tn = _tile(f, 4096, LANES)

    def body(d_ref, w_ref, t_ref, q_ref, dg_ref, du_ref):
        da = 0.5 * lax.dot_general(d_ref[...].astype(BF16), w_ref[...], (((1,), (1,)), ((), ())),
                                   preferred_element_type=F32)
        dg_ref[...] = (da * q_ref[...].astype(F32)).astype(dg_ref.dtype)
        du_ref[...] = (da * t_ref[...].astype(F32)).astype(du_ref.dtype)

    tile = pl.BlockSpec((tm, tn), lambda i, j: (i, j))
    act = jax.ShapeDtypeStruct((n, f), BF16)
    outs, comm_outs = _call(
        body, name=name, grid=(n // tm, f // tn),
        in_specs=[pl.BlockSpec((tm, d), lambda i, j: (i, 0)), pl.BlockSpec((tn, d), lambda i, j: (j, 0)), tile, tile],
        out_specs=[tile, tile], out_shape=[act, act],
        semantics=("parallel", "parallel"), args=(dout, w_out, t, q), comm=comm)
    return outs if comm is None else (outs, comm_outs)


def _proj_in_bwd(parts, w, x, gain, dres, name, comm=None):
    n, d = x.shape
    tm = _tile(n, 512, 8)
    steps = n // tm
    np_ = len(parts)
    offs = [off for _, off in parts]
    widths = [a.shape[1] for a, _ in parts]

    def body(*refs):
        a_refs = refs[:np_]
        w_ref, x_ref, g_ref, dr_ref, dx_ref, dg_ref, acc_ref = refs[np_:]
        i = pl.program_id(0)
        dh = None
        for a_ref, off, kp in zip(a_refs, offs, widths):
            part = lax.dot_general(a_ref[...].astype(BF16), w_ref[:, off:off + kp], (((1,), (1,)), ((), ())),
                                   preferred_element_type=F32)
            dh = part if dh is None else dh + part
        xv = x_ref[...]
        r = _rms_rows(xv)
        xh = xv * r
        dyg = dh * g_ref[...]
        mean = jnp.mean(dyg * xh, axis=-1, keepdims=True)
        dx_ref[...] = dr_ref[...] + r * (dyg - xh * mean)
        part = jnp.sum((dh * xh).reshape(tm // SUBLANES, SUBLANES, d), axis=0)

        @pl.when(i == 0)
        def _():
            acc_ref[...] = part

        @pl.when(i > 0)
        def _():
            acc_ref[...] += part

        @pl.when(i == steps - 1)
        def _():
            dg_ref[...] = jnp.sum(acc_ref[...], axis=0, keepdims=True)

    row = pl.BlockSpec((tm, d), lambda i: (i, 0))
    vec = pl.BlockSpec((1, d), lambda i: (0, 0))
    (dx, dg), comm_outs = _call(
        body, name=name, grid=(steps,),
        in_specs=[pl.BlockSpec((tm, kp), lambda i: (i, 0)) for kp in widths]
        + [pl.BlockSpec(w.shape, lambda i: (0, 0), pipeline_mode=pl.Buffered(1)), row, vec, row],
        out_specs=[row, vec],
        out_shape=[jax.ShapeDtypeStruct((n, d), F32), jax.ShapeDtypeStruct((1, d), F32)],
        scratch_shapes=[pltpu.VMEM((SUBLANES, d), F32)],
        semantics=("arbitrary",), args=(*[a for a, _ in parts], w, x, gain.reshape(1, d), dres), comm=comm)
    return (dx, dg.reshape(d)) if comm is None else (dx, dg.reshape(d), comm_outs)


def _mix_in_fwd(x, gain, w, width, name, comm=None):
    n, d = x.shape
    cols = w.shape[1]
    tm = _tile(n, 512, 16)

    def body(x_ref, gain_ref, w_ref, h_ref, u_ref, z_ref):
        xv = x_ref[...]
        h = (xv * _rms_rows(xv) * gain_ref[...]).astype(h_ref.dtype)
        h_ref[...] = h
        z = jnp.dot(h, w_ref[...], preferred_element_type=F32)
        u_ref[...] = z[:, 0:width]
        z_ref[...] = z[:, width:cols]

    row = pl.BlockSpec((tm, d), lambda i: (i, 0))
    outs, comm_outs = _call(
        body, name=name, grid=(n // tm,),
        in_specs=[row, pl.BlockSpec((1, d), lambda i: (0, 0)), pl.BlockSpec((d, cols), lambda i: (0, 0))],
        out_specs=[row, pl.BlockSpec((tm, width), lambda i: (i, 0)), pl.BlockSpec((tm, cols - width), lambda i: (i, 0))],
        out_shape=[jax.ShapeDtypeStruct((n, d), BF16), jax.ShapeDtypeStruct((n, width), F32),
                   jax.ShapeDtypeStruct((n, cols - width), F32)],
        semantics=("parallel",), args=(x, gain.reshape(1, d), w), comm=comm)
    return outs if comm is None else (*outs, comm_outs)


def _tril_mask():
    t = lax.broadcasted_iota(jnp.int32, (GM_CHUNK, GM_CHUNK), 0)
    s = lax.broadcasted_iota(jnp.int32, (GM_CHUNK, GM_CHUNK), 1)
    return s <= t


def _gmlp_fwd(zgm, v_gain, w_s, bias_tile, name):
    n, w2 = zgm.shape
    w = w2 // 2
    heads = w // GM_HEAD_DIM
    tm = _tile(n, 512, GM_CHUNK)
    nq = tm // GM_CHUNK

    def body(u_ref, v_ref, gain_ref, w_ref, b_ref, o_ref):
        mask = _tril_mask()
        ug = _gelu(u_ref[...])
        vg = _gelu(v_ref[...])
        for h in range(heads):
            cols = slice(h * GM_HEAD_DIM, (h + 1) * GM_HEAD_DIM)
            vh = vg[:, cols]
            r = lax.rsqrt(jnp.mean(vh * vh, axis=-1, keepdims=True) + EPS)
            vn = (vh * r * gain_ref[:, cols]).astype(BF16)
            wm = jnp.where(mask, w_ref[h], 0.0).astype(BF16)
            for q in range(nq):
                rows = slice(q * GM_CHUNK, (q + 1) * GM_CHUNK)
                s = jnp.dot(wm, vn[rows], preferred_element_type=F32) + b_ref[:, cols]
                o_ref[rows, cols] = ug[rows, cols] * s

    return pl.pallas_call(
        body, name=name, grid=(n // tm,),
        in_specs=[pl.BlockSpec((tm, w), lambda i: (i, 0)), pl.BlockSpec((tm, w), lambda i: (i, 1)),
                  pl.BlockSpec((1, w), lambda i: (0, 0)),
                  pl.BlockSpec((heads, GM_CHUNK, GM_CHUNK), lambda i: (0, 0, 0)),
                  pl.BlockSpec((GM_CHUNK, w), lambda i: (0, 0))],
        out_specs=pl.BlockSpec((tm, w), lambda i: (i, 0)),
        out_shape=jax.ShapeDtypeStruct((n, w), F32),
        compiler_params=_params("parallel"),
    )(zgm, zgm, v_gain.reshape(1, w), w_s, bias_tile)


def _gmlp_bwd(zgm, dy, v_gain, w_s, bias_tile, name):
    n, w2 = zgm.shape
    w = w2 // 2
    heads = w // GM_HEAD_DIM
    tm = _tile(n, 512, GM_CHUNK)
    nq = tm // GM_CHUNK
    steps = n // tm

    def body(z_ref, dy_ref, gain_ref, w_ref, b_ref, dz_ref, dw_ref, db_ref, dgain_ref):
        i = pl.program_id(0)
        mask = _tril_mask()

        @pl.when(i == 0)
        def _():
            dw_ref[...] = jnp.zeros_like(dw_ref)
            db_ref[...] = jnp.zeros_like(db_ref)
            dgain_ref[...] = jnp.zeros_like(dgain_ref)

        ug, dug_du = _gelu_and_grad(z_ref[:, 0:w])
        vg, dvg_dv = _gelu_and_grad(z_ref[:, w:w2])
        dyv = dy_ref[...]
        for h in range(heads):
            cols = slice(h * GM_HEAD_DIM, (h + 1) * GM_HEAD_DIM)
            vh = vg[:, cols]
            r = lax.rsqrt(jnp.mean(vh * vh, axis=-1, keepdims=True) + EPS)
            vhat = vh * r
            gain = gain_ref[:, cols]
            vn = (vhat * gain).astype(BF16)
            wm = jnp.where(mask, w_ref[h], 0.0).astype(BF16)
            dvn_parts = []
            for q in range(nq):
                rows = slice(q * GM_CHUNK, (q + 1) * GM_CHUNK)
                s = jnp.dot(wm, vn[rows], preferred_element_type=F32) + b_ref[:, cols]
                dyq = dyv[rows, cols]
                dz_ref[rows, cols] = (dyq * s * dug_du[rows, cols]).astype(dz_ref.dtype)
                ds = dyq * ug[rows, cols]
                db_ref[:, cols] += ds
                dsb = ds.astype(BF16)
                dw_ref[h] += lax.dot_general(dsb, vn[rows], (((1,), (1,)), ((), ())), preferred_element_type=F32)
                dvn_parts.append(lax.dot_general(wm, dsb, (((0,), (0,)), ((), ())), preferred_element_type=F32))
            dvn = jnp.concatenate(dvn_parts, axis=0) if nq > 1 else dvn_parts[0]
            dgain_ref[:, cols] += jnp.sum(dvn * vhat, axis=0, keepdims=True)
            dvhat = dvn * gain
            mean = jnp.mean(dvhat * vhat, axis=-1, keepdims=True)
            dz_ref[:, w + h * GM_HEAD_DIM:w + (h + 1) * GM_HEAD_DIM] = (
                r * (dvhat - vhat * mean) * dvg_dv[:, cols]).astype(dz_ref.dtype)

        @pl.when(i == steps - 1)
        def _():
            for h in range(heads):
                dw_ref[h] = jnp.where(mask, dw_ref[h], 0.0)

    dz, dw, db, dgain = pl.pallas_call(
        body, name=name, grid=(steps,),
        in_specs=[pl.BlockSpec((tm, w2), lambda i: (i, 0)), pl.BlockSpec((tm, w), lambda i: (i, 0)),
                  pl.BlockSpec((1, w), lambda i: (0, 0)),
                  pl.BlockSpec((heads, GM_CHUNK, GM_CHUNK), lambda i: (0, 0, 0)),
                  pl.BlockSpec((GM_CHUNK, w), lambda i: (0, 0))],
        out_specs=[pl.BlockSpec((tm, w2), lambda i: (i, 0)),
                   pl.BlockSpec((heads, GM_CHUNK, GM_CHUNK), lambda i: (0, 0, 0)),
                   pl.BlockSpec((GM_CHUNK, w), lambda i: (0, 0)),
                   pl.BlockSpec((1, w), lambda i: (0, 0))],
        out_shape=[jax.ShapeDtypeStruct((n, w2), BF16), jax.ShapeDtypeStruct((heads, GM_CHUNK, GM_CHUNK), F32),
                   jax.ShapeDtypeStruct((GM_CHUNK, w), F32), jax.ShapeDtypeStruct((1, w), F32)],
        compiler_params=_params("arbitrary"),
    )(zgm, dy, v_gain.reshape(1, w), w_s, bias_tile)
    return dz, dw, db, dgain.reshape(w)


def _mix_out_fwd(y_ssm, y_gm, g1, g2, w_out, x, name, comm=None):
    n, w = y_ssm.shape
    d = w_out.shape[1]
    tm = _tile(n, 512, 16)

    def body(a_ref, b_ref, g1_ref, g2_ref, w_ref, x_ref, ycat_ref, o_ref):
        for src, g_ref, lo in ((a_ref, g1_ref, 0), (b_ref, g2_ref, w)):
            v = src[...]
            ycat_ref[:, lo:lo + w] = (v * _rms_rows(v) * g_ref[...]).astype(ycat_ref.dtype)
        o_ref[...] = x_ref[...] + jnp.dot(ycat_ref[...], w_ref[...], preferred_element_type=F32)

    row = pl.BlockSpec((tm, w), lambda i: (i, 0))
    vec = pl.BlockSpec((1, w), lambda i: (0, 0))
    outs, comm_outs = _call(
        body, name=name, grid=(n // tm,),
        in_specs=[row, row, vec, vec, pl.BlockSpec((2 * w, d), lambda i: (0, 0)), pl.BlockSpec((tm, d), lambda i: (i, 0))],
        out_specs=[pl.BlockSpec((tm, 2 * w), lambda i: (i, 0)), pl.BlockSpec((tm, d), lambda i: (i, 0))],
        out_shape=[jax.ShapeDtypeStruct((n, 2 * w), BF16), jax.ShapeDtypeStruct((n, d), F32)],
        semantics=("parallel",), args=(y_ssm, y_gm, g1.reshape(1, w), g2.reshape(1, w), w_out, x), comm=comm)
    return outs if comm is None else (*outs, comm_outs)


def _mix_out_bwd(dx, w_out, y_ssm, y_gm, g1, g2, name, comm=None):
    n, w = y_ssm.shape
    d = w_out.shape[1]
    tm = _tile(n, 512, 8)
    steps = n // tm

    def body(dx_ref, w_ref, a_ref, b_ref, g1_ref, g2_ref, da_ref, db_ref, dg1_ref, dg2_ref):
        i = pl.program_id(0)

        @pl.when(i == 0)
        def _():
            dg1_ref[...] = jnp.zeros_like(dg1_ref)
            dg2_ref[...] = jnp.zeros_like(dg2_ref)

        dycat = lax.dot_general(dx_ref[...].astype(BF16), w_ref[...], (((1,), (1,)), ((), ())),
                                preferred_element_type=F32)
        for src, g_ref, lo, dst, dg_ref in ((a_ref, g1_ref, 0, da_ref, dg1_ref), (b_ref, g2_ref, w, db_ref, dg2_ref)):
            v = src[...]
            dh = dycat[:, lo:lo + w]
            r = _rms_rows(v)
            vh = v * r
            dyg = dh * g_ref[...]
            mean = jnp.mean(dyg * vh, axis=-1, keepdims=True)
            dst[...] = r * (dyg - vh * mean)
            dg_ref[...] += jnp.sum(dh * vh, axis=0, keepdims=True)

    row = pl.BlockSpec((tm, w), lambda i: (i, 0))
    vec = pl.BlockSpec((1, w), lambda i: (0, 0))
    (da, db, dg1, dg2), comm_outs = _call(
        body, name=name, grid=(steps,),
        in_specs=[pl.BlockSpec((tm, d), lambda i: (i, 0)), pl.BlockSpec((2 * w, d), lambda i: (0, 0)), row, row, vec, vec],
        out_specs=[row, row, vec, vec],
        out_shape=[jax.ShapeDtypeStruct((n, w), F32), jax.ShapeDtypeStruct((n, w), F32),
                   jax.ShapeDtypeStruct((1, w), F32), jax.ShapeDtypeStruct((1, w), F32)],
        semantics=("arbitrary",), args=(dx, w_out, y_ssm, y_gm, g1.reshape(1, w), g2.reshape(1, w)), comm=comm)
    res = (da, db, dg1.reshape(w), dg2.reshape(w))
    return res if comm is None else (*res, comm_outs)


def _discretise(a_re, a_im, log_dt, bt_re, bt_im):
    dt = jnp.exp(log_dt)
    e = jnp.exp(a_re * dt)
    ang = a_im * dt
    lr = e * jnp.cos(ang)
    li = e * jnp.sin(ang)
    den = a_re * a_re + a_im * a_im
    cr = ((lr - 1.0) * a_re + li * a_im) / den
    ci = (li * a_re - (lr - 1.0) * a_im) / den
    cr3 = cr[:, None, :]
    ci3 = ci[:, None, :]
    return lr, li, cr3 * bt_re - ci3 * bt_im, cr3 * bt_im + ci3 * bt_re


def _disc_fwd(a_re, a_im, log_dt, bt_re, bt_im):
    g, p = a_re.shape
    c = bt_re.shape[1]

    def body(are_ref, aim_ref, ldt_ref, bre_ref, bim_ref, lr_ref, li_ref, bbr_ref, bbi_ref):
        lr, li, bbr, bbi = _discretise(are_ref[...], aim_ref[...], ldt_ref[...], bre_ref[...], bim_ref[...])
        lr_ref[...] = lr
        li_ref[...] = li
        bbr_ref[...] = bbr
        bbi_ref[...] = bbi

    return pl.pallas_call(
        body, name="s5_discretise",
        out_shape=[jax.ShapeDtypeStruct((g, p), F32), jax.ShapeDtypeStruct((g, p), F32),
                   jax.ShapeDtypeStruct((g, c, p), F32), jax.ShapeDtypeStruct((g, c, p), F32)],
    )(a_re, a_im, log_dt, bt_re, bt_im)


def _disc_bwd(a_re, a_im, log_dt, bt_re, bt_im, dlr, dli, dbbr, dbbi):
    g, p = a_re.shape
    c = bt_re.shape[1]

    def body(are_ref, aim_ref, ldt_ref, bre_ref, bim_ref, dlr_ref, dli_ref, dbbr_ref, dbbi_ref,
             dare_ref, daim_ref, dldt_ref, dbre_ref, dbim_ref):
        _, vjp = jax.vjp(_discretise, are_ref[...], aim_ref[...], ldt_ref[...], bre_ref[...], bim_ref[...])
        dare, daim, dldt, dbre, dbim = vjp((dlr_ref[...], dli_ref[...], dbbr_ref[...], dbbi_ref[...]))
        dare_ref[...] = dare
        daim_ref[...] = daim
        dldt_ref[...] = dldt
        dbre_ref[...] = dbre
        dbim_ref[...] = dbim

    return pl.pallas_call(
        body, name="s5_discretise_bwd",
        out_shape=[jax.ShapeDtypeStruct((g, p), F32), jax.ShapeDtypeStruct((g, p), F32),
                   jax.ShapeDtypeStruct((g, 1), F32),
                   jax.ShapeDtypeStruct((g, c, p), F32), jax.ShapeDtypeStruct((g, c, p), F32)],
    )(a_re, a_im, log_dt, bt_re, bt_im, dlr, dli, dbbr, dbbi)


def _block_diag(w, nb):
    g, a, b = w.shape
    gpb = g // nb
    eye = jnp.eye(gpb, dtype=w.dtype)
    w4 = w.reshape(nb, gpb, a, b)
    return jnp.einsum("ngab,gh->ngahb", w4, eye).reshape(nb, gpb * a, gpb * b)


def _block_diag_extract(m, gpb):
    nb, ga, gb = m.shape
    a, b = ga // gpb, gb // gpb
    m5 = m.reshape(nb, gpb, a, gpb, b)
    idx = jnp.arange(gpb)
    return m5[:, idx, :, idx, :].transpose(1, 0, 2, 3).reshape(nb * gpb, a, b)


def _ssm_operands(lr, li, bbr, bbi, c_re, c_im, d_skip, glu_w, glu_b):
    g = lr.shape[0]
    nb = g // GROUPS_PER_BLOCK
    s = STATES_PER_BLOCK
    lam = jnp.concatenate([lr.reshape(nb, 1, s), li.reshape(nb, 1, s)], axis=-1)
    b_bd = jnp.concatenate([_block_diag(bbr, nb), _block_diag(bbi, nb)], axis=-1)
    ct_re = jnp.swapaxes(c_re, 1, 2)
    ct_im = jnp.swapaxes(c_im, 1, 2)
    c_bd = jnp.concatenate([_block_diag(ct_re, nb), -_block_diag(ct_im, nb)], axis=1)
    dsk = d_skip.reshape(nb, 1, LANES)
    w_bd = jnp.concatenate([_block_diag(glu_w[:, :, :SSM_CH], nb), _block_diag(glu_w[:, :, SSM_CH:], nb)], axis=-1)
    bias = jnp.concatenate([glu_b[:, :SSM_CH].reshape(nb, 1, LANES), glu_b[:, SSM_CH:].reshape(nb, 1, LANES)], axis=-1)
    return lam, b_bd.astype(BF16), c_bd.astype(BF16), dsk, w_bd.astype(BF16), bias


def _roll_rows(v, shift):
    return v if shift % SUBLANES == 0 else pltpu.roll(v, shift % SUBLANES, 0)


def _scan_chunk_rows(seq, nseq):
    return _tile(seq, max(8 * SSM_TIME_CHUNK // nseq, 8), max(SUBLANES // nseq, 1) * 8)


def _stage_lams(lam_ref, e, nseq):
    s = STATES_PER_BLOCK
    lr = jnp.broadcast_to(lam_ref[e, :, 0:s], (SUBLANES, s))
    li = jnp.broadcast_to(lam_ref[e, :, s:2 * s], (SUBLANES, s))
    if nseq == SUBLANES:
        return [(lr, li)]
    row = lax.broadcasted_iota(jnp.int32, (SUBLANES, s), 0)
    out = []
    for j in range(SUBLANES // nseq):
        mine = jnp.logical_and(row >= j * nseq, row < (j + 1) * nseq)
        out.append((jnp.where(mine, lr, 0.0), jnp.where(mine, li, 0.0)))
    return out


def _scan_with(nblk, step, carry, between):
    runs = len(between)
    per = nblk // runs
    for i in range(runs):
        hi = nblk if i == runs - 1 else (i + 1) * per
        carry = lax.fori_loop(i * per, hi, step, carry, unroll=True)
        between[i]()
    return carry


def _ssm_fwd_pair(u, ops, nseq, name, comm=None):
    lam, b_bd, c_bd, dsk, w_bd, bias = ops
    rows_total, w = u.shape
    seq = rows_total // nseq
    nb = w // LANES
    s = STATES_PER_BLOCK
    tc = _scan_chunk_rows(seq, nseq)
    nk = seq // tc
    rows = tc * nseq
    nblk = rows // SUBLANES
    ng = 4 if nb % 4 == 0 else 2
    two = 2 * LANES
    ncol = 4

    def body(u_ref, lam_ref, b_ref, c_ref, d_ref, w_ref, bias_ref, y_ref, hb_ref, *scratch):
        bufs, st, rbufs = scratch[:ng], scratch[ng], scratch[ng + 1:]
        k = pl.program_id(1)

        @pl.when(k == 0)
        def _():
            st[...] = jnp.zeros_like(st)

        hb_ref[...] = st[...]
        for q in range(nseq):
            for e in range(ng):
                rbufs[e][pl.ds(q, tc, stride=nseq), :] = u_ref[q, :, e * LANES:(e + 1) * LANES]

        def u_of(e):
            return rbufs[e][...]

        def project_in(e, j):
            cols = slice(j * (2 * s // ncol), (j + 1) * (2 * s // ncol))
            bufs[e][:, cols] = jnp.dot(u_of(e).astype(BF16), b_ref[e, :, cols], preferred_element_type=F32)

        def scan(e, between):
            buf = bufs[e]
            lams = _stage_lams(lam_ref, e, nseq)

            def step(i, carry):
                pr, pi = carry
                r0 = pl.multiple_of(i * SUBLANES, SUBLANES)
                outr = buf[pl.ds(r0, SUBLANES), 0:s]
                outi = buf[pl.ds(r0, SUBLANES), s:2 * s]
                for lr, li in lams:
                    rr = _roll_rows(pr, nseq)
                    ri = _roll_rows(pi, nseq)
                    outr = outr + (lr * rr - li * ri)
                    outi = outi + (lr * ri + li * rr)
                    pr, pi = outr, outi
                buf[pl.ds(r0, SUBLANES), 0:s] = outr
                buf[pl.ds(r0, SUBLANES), s:2 * s] = outi
                return outr, outi

            lo = e * 2 * s
            hr, hi = _scan_with(nblk, step, (st[:, lo:lo + s], st[:, lo + s:lo + 2 * s]), between)
            st[:, lo:lo + s] = hr
            st[:, lo + s:lo + 2 * s] = hi

        part = {}

        def project_out(e, j):
            ks = slice(j * (2 * s // ncol), (j + 1) * (2 * s // ncol))
            p = jnp.dot(bufs[e][:, ks].astype(BF16), c_ref[e, ks, :], preferred_element_type=F32)
            part[e] = p if j == 0 else part[e] + p

        def finish(e):
            y = part[e] + d_ref[e] * u_of(e)
            z = jnp.dot(_gelu(y).astype(BF16), w_ref[e], preferred_element_type=F32) + bias_ref[e]
            part[e] = z[:, 0:LANES] * _sigmoid(z[:, LANES:two])

        def pieces(fn, e):
            return [functools.partial(fn, e, j) for j in range(ncol)]

        for piece in pieces(project_in, 0):
            piece()
        for e in range(ng):
            between = pieces(project_in, e + 1) if e + 1 < ng else []
            if e >= 1:
                between = between + pieces(project_out, e - 1) + [functools.partial(finish, e - 1)]
            scan(e, between)
        for piece in pieces(project_out, ng - 1):
            piece()
        finish(ng - 1)
        for e in range(ng):
            rbufs[e][...] = part[e]
            for q in range(nseq):
                y_ref[q, :, e * LANES:(e + 1) * LANES] = rbufs[e][pl.ds(q, tc, stride=nseq), :]

    blk = lambda shape: pl.BlockSpec(shape, lambda b, k: (b, 0, 0))
    tok = pl.BlockSpec((nseq, tc, ng * LANES), lambda b, k: (0, k, b))
    (y, hb), comm_outs = _call(
        body, name=name, grid=(nb // ng, nk),
        in_specs=[tok, blk((ng, 1, 2 * s)), blk((ng, LANES, 2 * s)), blk((ng, 2 * s, LANES)),
                  blk((ng, 1, LANES)), blk((ng, LANES, two)), blk((ng, 1, two))],
        out_specs=[tok, pl.BlockSpec((SUBLANES, ng * 2 * s), lambda b, k: (k, b))],
        out_shape=[jax.ShapeDtypeStruct((nseq, seq, w), F32),
                   jax.ShapeDtypeStruct((nk * SUBLANES, nb * 2 * s), F32)],
        scratch_shapes=[pltpu.VMEM((rows, 2 * s), F32)] * ng + [pltpu.VMEM((SUBLANES, ng * 2 * s), F32)]
        + [pltpu.VMEM((rows, LANES), F32)] * ng,
        semantics=("parallel", "arbitrary"),
        args=(u.reshape(nseq, seq, w), lam, b_bd, c_bd, dsk, w_bd, bias), comm=comm)
    y = y.reshape(nseq * seq, w)
    return (y, hb) if comm is None else (y, hb, comm_outs)


def _ssm_bwd_pair(u, dout, hb, ops, nseq, name, comm=None):
    lam, b_bd, c_bd, dsk, w_bd, bias = ops
    rows_total, w = u.shape
    seq = rows_total // nseq
    nb = w // LANES
    s = STATES_PER_BLOCK
    tc = _scan_chunk_rows(seq, nseq)
    nk = seq // tc
    rows = tc * nseq
    nblk = rows // SUBLANES
    stages = SUBLANES // nseq
    two = 2 * LANES
    ncol = 4
    cw = 2 * s // ncol
    tn_dims = (((0,), (0,)), ((), ()))
    nt_dims = (((1,), (1,)), ((), ()))

    def body(u_ref, dy_ref, hb_ref, lam_ref, b_ref, c_ref, d_ref, w_ref, bias_ref,
             du_ref, dlam_ref, db_ref, dct_ref, dd_ref, dw_ref, dbias_ref,
             hbuf_a, hbuf_b, gbuf_a, gbuf_b, gst, lacc, ru_a, ru_b, rd_a, rd_b):
        k = pl.program_id(1)

        @pl.when(k == 0)
        def _():
            gst[...] = jnp.zeros_like(gst)
            lacc[...] = jnp.zeros_like(lacc)
            db_ref[...] = jnp.zeros_like(db_ref)
            dct_ref[...] = jnp.zeros_like(dct_ref)
            dd_ref[...] = jnp.zeros_like(dd_ref)
            dw_ref[...] = jnp.zeros_like(dw_ref)
            dbias_ref[...] = jnp.zeros_like(dbias_ref)

        hbufs, gbufs, rus, rds = (hbuf_a, hbuf_b), (gbuf_a, gbuf_b), (ru_a, ru_b), (rd_a, rd_b)
        for q in range(nseq):
            for e in range(2):
                rus[e][pl.ds(q, tc, stride=nseq), :] = u_ref[q, :, e * LANES:(e + 1) * LANES]
                rds[e][pl.ds(q, tc, stride=nseq), :] = dy_ref[q, :, e * LANES:(e + 1) * LANES]
        row = lax.broadcasted_iota(jnp.int32, (SUBLANES, s), 0)
        cols = [slice(j * cw, (j + 1) * cw) for j in range(ncol)]
        val = [{}, {}]

        def project_in(e, j):
            hbufs[e][:, cols[j]] = jnp.dot(rus[e][...].astype(BF16), b_ref[e, :, cols[j]], preferred_element_type=F32)

        def scan_fwd(e, between):
            buf = hbufs[e]
            lams = _stage_lams(lam_ref, e, nseq)

            def step(i, carry):
                pr, pi = carry
                r0 = pl.multiple_of(i * SUBLANES, SUBLANES)
                outr = buf[pl.ds(r0, SUBLANES), 0:s]
                outi = buf[pl.ds(r0, SUBLANES), s:2 * s]
                for lr, li in lams:
                    rr = _roll_rows(pr, nseq)
                    ri = _roll_rows(pi, nseq)
                    outr = outr + (lr * rr - li * ri)
                    outi = outi + (lr * ri + li * rr)
                    pr, pi = outr, outi
                buf[pl.ds(r0, SUBLANES), 0:s] = outr
                buf[pl.ds(r0, SUBLANES), s:2 * s] = outi
                return outr, outi

            lo = e * 2 * s
            _scan_with(nblk, step, (hb_ref[:, lo:lo + s], hb_ref[:, lo + s:lo + 2 * s]), between)

        def y_part(e, j):
            p = jnp.dot(hbufs[e][:, cols[j]].astype(BF16), c_ref[e, cols[j], :], preferred_element_type=F32)
            val[e]["y"] = p if j == 0 else val[e]["y"] + p

        def gate(e):
            v = val[e]
            uu = rus[e][...]
            yg, dyg_dy = _gelu_and_grad(v.pop("y") + d_ref[e] * uu)
            yg16 = yg.astype(BF16)
            z = jnp.dot(yg16, w_ref[e], preferred_element_type=F32) + bias_ref[e]
            sg = _sigmoid(z[:, LANES:two])
            dout_e = rds[e][...]
            dz = jnp.concatenate([dout_e * sg, dout_e * z[:, 0:LANES] * sg * (1.0 - sg)], axis=-1)
            dz16 = dz.astype(BF16)
            dw_ref[e] += lax.dot_general(yg16, dz16, tn_dims, preferred_element_type=F32)
            dbias_ref[e] += jnp.sum(dz, axis=0, keepdims=True)
            dy = lax.dot_general(dz16, w_ref[e], nt_dims, preferred_element_type=F32) * dyg_dy
            dd_ref[e] += jnp.sum(dy * uu, axis=0, keepdims=True)
            v["dy"] = dy
            v["dy16"] = dy.astype(BF16)

        def dc_part(e, j):
            dct_ref[e, :, cols[j]] += lax.dot_general(val[e]["dy16"], hbufs[e][:, cols[j]].astype(BF16), tn_dims,
                                                      preferred_element_type=F32)

        def dh_part(e, j):
            gbufs[e][:, cols[j]] = lax.dot_general(val[e]["dy16"], c_ref[e, cols[j], :], nt_dims,
                                                   preferred_element_type=F32)

        def scan_bwd(e, between):
            hbuf, gbuf = hbufs[e], gbufs[e]
            lams = _stage_lams(lam_ref, e, nseq)
            lo = e * 2 * s

            def step(i, carry):
                pr, pi, ar, ai = carry
                blk = nblk - 1 - i
                r0 = pl.multiple_of(blk * SUBLANES, SUBLANES)
                outr = gbuf[pl.ds(r0, SUBLANES), 0:s]
                outi = gbuf[pl.ds(r0, SUBLANES), s:2 * s]
                for lr, li in reversed(lams):
                    rr = _roll_rows(pr, SUBLANES - nseq)
                    ri = _roll_rows(pi, SUBLANES - nseq)
                    outr = outr + (lr * rr + li * ri)
                    outi = outi + (lr * ri - li * rr)
                    pr, pi = outr, outi
                gbuf[pl.ds(r0, SUBLANES), 0:s] = outr
                gbuf[pl.ds(r0, SUBLANES), s:2 * s] = outi
                p0 = pl.multiple_of(jnp.maximum(blk - 1, 0) * SUBLANES, SUBLANES)
                first = blk == 0
                before_r = jnp.where(first, hb_ref[:, lo:lo + s], hbuf[pl.ds(p0, SUBLANES), 0:s])
                before_i = jnp.where(first, hb_ref[:, lo + s:lo + 2 * s], hbuf[pl.ds(p0, SUBLANES), s:2 * s])
                if stages > 1:
                    last_rows = row >= SUBLANES - nseq
                    before_r = _roll_rows(jnp.where(last_rows, before_r, hbuf[pl.ds(r0, SUBLANES), 0:s]), nseq)
                    before_i = _roll_rows(jnp.where(last_rows, before_i, hbuf[pl.ds(r0, SUBLANES), s:2 * s]), nseq)
                return (outr, outi, ar + outr * before_r + outi * before_i, ai - outr * before_i + outi * before_r)

            gr, gi, ar, ai = _scan_with(
                nblk, step, (gst[:, lo:lo + s], gst[:, lo + s:lo + 2 * s], lacc[:, lo:lo + s], lacc[:, lo + s:lo + 2 * s]),
                between)
            gst[:, lo:lo + s] = gr
            gst[:, lo + s:lo + 2 * s] = gi
            lacc[:, lo:lo + s] = ar
            lacc[:, lo + s:lo + 2 * s] = ai

        def du_part(e, j):
            p = lax.dot_general(gbufs[e][:, cols[j]].astype(BF16), b_ref[e, :, cols[j]], nt_dims,
                                preferred_element_type=F32)
            val[e]["du"] = (val[e].pop("dy") * d_ref[e] + p) if j == 0 else val[e]["du"] + p

        def db_part(e, j):
            db_ref[e, :, cols[j]] += lax.dot_general(rus[e][...].astype(BF16), gbufs[e][:, cols[j]].astype(BF16),
                                                     tn_dims, preferred_element_type=F32)

        def parts(fn, e):
            return [functools.partial(fn, e, j) for j in range(ncol)]

        middle_of = lambda e: parts(y_part, e) + [functools.partial(gate, e)] + parts(dc_part, e) + parts(dh_part, e)
        last_of = lambda e: parts(du_part, e) + parts(db_part, e)
        for piece in parts(project_in, 0):
            piece()
        scan_fwd(0, parts(project_in, 1))
        scan_fwd(1, middle_of(0))
        scan_bwd(0, middle_of(1))
        scan_bwd(1, last_of(0))
        for piece in last_of(1):
            piece()
        for e in range(2):
            rus[e][...] = val[e]["du"]
            for q in range(nseq):
                du_ref[q, :, e * LANES:(e + 1) * LANES] = rus[e][pl.ds(q, tc, stride=nseq), :].astype(du_ref.dtype)

        @pl.when(k == nk - 1)
        def _():
            for e in range(2):
                dlam_ref[e] = jnp.sum(lacc[:, e * 2 * s:(e + 1) * 2 * s], axis=0, keepdims=True)

    blk = lambda shape: pl.BlockSpec(shape, lambda b, k: (b, 0, 0))
    tok = pl.BlockSpec((nseq, tc, two), lambda b, k: (0, nk - 1 - k, b))
    outs, comm_outs = _call(
        body, name=name, grid=(nb // 2, nk),
        in_specs=[tok, tok, pl.BlockSpec((SUBLANES, 4 * s), lambda b, k: (nk - 1 - k, b)),
                  blk((2, 1, 2 * s)), blk((2, LANES, 2 * s)), blk((2, 2 * s, LANES)),
                  blk((2, 1, LANES)), blk((2, LANES, two)), blk((2, 1, two))],
        out_specs=[tok, blk((2, 1, 2 * s)), blk((2, LANES, 2 * s)), blk((2, LANES, 2 * s)),
                   blk((2, 1, LANES)), blk((2, LANES, two)), blk((2, 1, two))],
        out_shape=[jax.ShapeDtypeStruct((nseq, seq, w), BF16),
                   jax.ShapeDtypeStruct((nb, 1, 2 * s), F32), jax.ShapeDtypeStruct((nb, LANES, 2 * s), F32),
                   jax.ShapeDtypeStruct((nb, LANES, 2 * s), F32), jax.ShapeDtypeStruct((nb, 1, LANES), F32),
                   jax.ShapeDtypeStruct((nb, LANES, two), F32), jax.ShapeDtypeStruct((nb, 1, two), F32)],
        scratch_shapes=[pltpu.VMEM((rows, 2 * s), F32)] * 4
        + [pltpu.VMEM((SUBLANES, 4 * s), F32), pltpu.VMEM((SUBLANES, 4 * s), F32)]
        + [pltpu.VMEM((rows, LANES), F32)] * 4,
        semantics=("parallel", "arbitrary"),
        args=(u.reshape(nseq, seq, w), dout.reshape(nseq, seq, w), hb, lam, b_bd, c_bd, dsk, w_bd, bias), comm=comm)
    outs[0] = outs[0].reshape(nseq * seq, w)
    return outs if comm is None else (outs, comm_outs)


ANY = pl.BlockSpec(memory_space=pl.ANY)

BIG = (("ffn1_w_in", True), ("ffn1_w_out", False), ("mix_w_in", True), ("mix_w_out", False),
       ("ffn2_w_in", True), ("ffn2_w_out", False))


def _my_place():
    return lax.axis_index("x"), lax.axis_index("y"), lax.axis_index("c")


def _other_chips(x, y):
    return [(1 - x, y), (x, 1 - y), (1 - x, 1 - y)]


def _half_of_shard(ref, col_sharded, chip, core):
    full_rows, full_cols = ref.shape
    if col_sharded:
        hr, cs = full_rows // 2, full_cols // N_CHIPS
        return ref.at[pl.ds(pl.multiple_of(core * hr, 8), hr), pl.ds(chip * cs, cs)]
    rs = full_rows // N_CHIPS
    return ref.at[pl.ds(pl.multiple_of(chip * rs + core * (rs // 2), 8), rs // 2), :]


def _gather_comm(shards, cols):
    full_shapes = [(sh.shape[0], sh.shape[1] * N_CHIPS) if col else (sh.shape[0] * N_CHIPS, sh.shape[1])
                   for sh, col in zip(shards, cols)]
    nw = len(shards)

    def first_copies(ins, outs, sems):
        send_sems, recv_sems, local_sems = sems
        x, y, c = _my_place()
        me = 2 * x + y
        locals_, sends = [], []
        for wi in range(nw):
            src, dst = ins[wi], outs[wi]
            rs, cs = src.shape
            hs = rs // 2
            if cols[wi]:
                place = dst.at[:, pl.ds(me * cs, cs)]
            else:
                place = dst.at[pl.ds(pl.multiple_of(me * rs, 8), rs), :]
            locals_.append(pltpu.make_async_copy(src, place, local_sems.at[wi]))
            my_half = src.at[pl.ds(pl.multiple_of(c * hs, 8), hs), :]
            for j, (px, py) in enumerate(_other_chips(x, y)):
                sends.append(pltpu.make_async_remote_copy(
                    src_ref=my_half, dst_ref=_half_of_shard(dst, cols[wi], me, c),
                    send_sem=send_sems.at[wi * 6 + j], recv_sem=recv_sems.at[wi * 6 + j],
                    device_id=(px, py, c), device_id_type=MESH))
        return locals_, sends

    def start(ins, outs, sems):
        locals_, sends = first_copies(ins, outs, sems)
        for cp in locals_ + sends:
            cp.start()

    def forwards(outs, sems, wait_landed):
        send_sems, recv_sems, _ = sems
        x, y, c = _my_place()
        out = []
        for wi in range(nw):
            dst = outs[wi]
            for j, (px, py) in enumerate(_other_chips(x, y)):
                got = _half_of_shard(dst, cols[wi], 2 * px + py, c)
                if wait_landed:
                    pltpu.make_async_remote_copy(
                        src_ref=got, dst_ref=got, send_sem=send_sems.at[wi * 6 + j], recv_sem=recv_sems.at[wi * 6 + j],
                        device_id=(px, py, c), device_id_type=MESH).wait_recv()
                out.append(pltpu.make_async_remote_copy(
                    src_ref=got, dst_ref=got, send_sem=send_sems.at[wi * 6 + 3 + j], recv_sem=recv_sems.at[wi * 6 + 3 + j],
                    device_id=(x, y, 1 - c), device_id_type=MESH))
                if wait_landed:
                    out[-1].start()
        return out

    def middle(ins, outs, sems):
        forwards(outs, sems, True)

    def finish(ins, outs, sems):
        send_sems, recv_sems, _ = sems
        x, y, c = _my_place()
        locals_, sends = first_copies(ins, outs, sems)
        for wi in range(nw):
            dst = outs[wi]
            for j, (px, py) in enumerate(_other_chips(x, y)):
                theirs = _half_of_shard(dst, cols[wi], 2 * px + py, 1 - c)
                pltpu.make_async_remote_copy(
                    src_ref=theirs, dst_ref=theirs, send_sem=send_sems.at[wi * 6 + 3 + j],
                    recv_sem=recv_sems.at[wi * 6 + 3 + j], device_id=(x, y, 1 - c), device_id_type=MESH).wait_recv()
        for cp in sends + forwards(outs, sems, False):
            cp.wait_send()
        for cp in locals_:
            cp.wait()

    return _Comm(shards, [jax.ShapeDtypeStruct(s, BF16) for s in full_shapes],
                 [pltpu.SemaphoreType.DMA((6 * nw,)), pltpu.SemaphoreType.DMA((6 * nw,)),
                  pltpu.SemaphoreType.DMA((nw,))], start, finish, middle=middle)


def _pair_exchange_comm(grads, cols):
    nw = len(grads)
    n_copies = sum(1 if col else N_CHIPS for col in cols)

    def copies(ins, outs, sems):
        send_sems, recv_sems = sems
        x, y, c = _my_place()
        out = []
        for wi in range(nw):
            src, dst = ins[wi], outs[wi]
            fr = src.shape[0]
            if cols[wi]:
                hr = fr // 2
                pieces = [(src.at[pl.ds(pl.multiple_of((1 - c) * hr, 8), hr), :], dst)]
            else:
                rs = fr // N_CHIPS
                hs = rs // 2
                pieces = [(src.at[pl.ds(pl.multiple_of(k * rs + (1 - c) * hs, 8), hs), :],
                           dst.at[pl.ds(k * hs, hs), :]) for k in range(N_CHIPS)]
            for s_ref, d_ref in pieces:
                out.append(pltpu.make_async_remote_copy(
                    src_ref=s_ref, dst_ref=d_ref, send_sem=send_sems.at[len(out)], recv_sem=recv_sems.at[len(out)],
                    device_id=(x, y, 1 - c), device_id_type=MESH))
        return out

    def start(ins, outs, sems):
        for cp in copies(ins, outs, sems):
            cp.start()

    def finish(ins, outs, sems):
        for cp in copies(ins, outs, sems):
            cp.wait()

    return _Comm(grads, [jax.ShapeDtypeStruct((g.shape[0] // 2, g.shape[1]), F32) for g in grads],
                 [pltpu.SemaphoreType.DMA((n_copies,)), pltpu.SemaphoreType.DMA((n_copies,))], start, finish)


def _pair_sum(grad, other, col, core, name):
    fr, fc = grad.shape
    pieces = 1 if col else N_CHIPS
    pr = fr // 2 // pieces
    gview = grad.reshape(pieces * 2, pr, fc)
    oview = other.reshape(pieces, pr, fc)
    tr = _tile(pr, 256, 16)

    def body(c_ref, g_ref, o_ref, out_ref):
        out_ref[...] = (g_ref[...] + o_ref[...]).astype(out_ref.dtype)

    out = pl.pallas_call(
        body, name=name,
        grid_spec=pltpu.PrefetchScalarGridSpec(
            num_scalar_prefetch=1, grid=(pieces, pr // tr),
            in_specs=[pl.BlockSpec((1, tr, fc), lambda p, i, cref: (p * 2 + cref[0], i, 0)),
                      pl.BlockSpec((1, tr, fc), lambda p, i, cref: (p, i, 0))],
            out_specs=pl.BlockSpec((1, tr, fc), lambda p, i, cref: (p, i, 0))),
        out_shape=jax.ShapeDtypeStruct((pieces, pr, fc), BF16),
        compiler_params=_params("parallel", "parallel"),
    )(core, gview, oview)
    return out.reshape(fr // 2, fc)


def _chip_exchange_comm(psums, cols):
    nw = len(psums)
    out_shapes = [(N_CHIPS, p.shape[0], p.shape[1] // N_CHIPS) if col else (N_CHIPS, p.shape[0] // N_CHIPS, p.shape[1])
                  for p, col in zip(psums, cols)]

    def copies(ins, outs, sems):
        send_sems, recv_sems, local_sems = sems
        x, y, c = _my_place()
        me = 2 * x + y
        out = []
        for wi in range(nw):
            src = ins[wi]
            mine = outs[wi].at[me]

            def piece(chip, src=src, col=cols[wi]):
                if col:
                    cs = src.shape[1] // N_CHIPS
                    return src.at[:, pl.ds(chip * cs, cs)]
                ps = src.shape[0] // N_CHIPS
                return src.at[pl.ds(pl.multiple_of(chip * ps, 8), ps), :]

            out.append(pltpu.make_async_copy(piece(me), mine, local_sems.at[wi]))
            for j, (px, py) in enumerate(_other_chips(x, y)):
                out.append(pltpu.make_async_remote_copy(
                    src_ref=piece(2 * px + py), dst_ref=mine,
                    send_sem=send_sems.at[wi * 3 + j], recv_sem=recv_sems.at[wi * 3 + j],
                    device_id=(px, py, c), device_id_type=MESH))
        return out

    def start(ins, outs, sems):
        for cp in copies(ins, outs, sems):
            cp.start()

    def finish(ins, outs, sems):
        for cp in copies(ins, outs, sems):
            cp.wait()

    return _Comm(psums, [jax.ShapeDtypeStruct(s, BF16) for s in out_shapes],
                 [pltpu.SemaphoreType.DMA((3 * nw,)), pltpu.SemaphoreType.DMA((3 * nw,)),
                  pltpu.SemaphoreType.DMA((nw,))], start, finish)


def _chip_sum(slots, core, layer, layers, into, name):
    _, hr, cs = slots.shape
    tr = _tile(hr, 256, 16)

    def body(c_ref, s_ref, *rest):
        out_ref = rest[-1]
        acc = s_ref[0].astype(F32)
        for i in range(1, N_CHIPS):
            acc = acc + s_ref[i].astype(F32)
        out_ref[0] = acc

    in_specs = [pl.BlockSpec((N_CHIPS, tr, cs), lambda i, cref: (0, i, 0))]
    args = [core, slots]
    aliases = {}
    if into is not None:
        in_specs.append(pl.BlockSpec(memory_space=pl.ANY))
        args.append(into.reshape(layers * 2, hr, cs))
        aliases = {2: 0}
    out = pl.pallas_call(
        body, name=name,
        grid_spec=pltpu.PrefetchScalarGridSpec(
            num_scalar_prefetch=1, grid=(hr // tr,), in_specs=in_specs,
            out_specs=pl.BlockSpec((1, tr, cs), lambda i, cref: (layer * 2 + cref[0], i, 0))),
        out_shape=jax.ShapeDtypeStruct((layers * 2, hr, cs), F32),
        input_output_aliases=aliases,
        compiler_params=_params("parallel"),
    )(*args)
    return out.reshape(layers, 2 * hr, cs)


def _pair_share_comm(reduced):
    nw = len(reduced)

    def copies(ins, outs, sems):
        send_sems, recv_sems = sems
        x, y, c = _my_place()
        out = []
        for wi in range(nw):
            hs = outs[wi].shape[1] // 2
            mine = outs[wi].at[:, pl.ds(pl.multiple_of(c * hs, 8), hs), :]
            out.append(pltpu.make_async_remote_copy(
                src_ref=mine, dst_ref=mine, send_sem=send_sems.at[wi], recv_sem=recv_sems.at[wi],
                device_id=(x, y, 1 - c), device_id_type=MESH))
        return out

    def start(ins, outs, sems):
        for cp in copies(ins, outs, sems):
            cp.start()

    def finish(ins, outs, sems):
        for cp in copies(ins, outs, sems):
            cp.wait()

    return _Comm(reduced, [jax.ShapeDtypeStruct(r.shape, F32) for r in reduced],
                 [pltpu.SemaphoreType.DMA((nw,)), pltpu.SemaphoreType.DMA((nw,))], start, finish,
                 alias={i: i for i in range(nw)})


def _all_reduce_small(flat, comm):
    rows, lanes = flat.shape
    seg = rows // N_DEV
    c_in, c_out = len(comm.ins), len(comm.outs)

    def body(*refs):
        refs = list(refs)
        in_ref, cins = refs[0], refs[1:1 + c_in]
        out_ref, couts = refs[1 + c_in], refs[2 + c_in:2 + c_in + c_out]
        recv_ref, send_sems, recv_sems = refs[2 + c_in + c_out:5 + c_in + c_out]
        csems = refs[5 + c_in + c_out:]
        comm.start(cins, couts, csems)
        x, y, c = _my_place()
        me = 4 * x + 2 * y + c

        def peer(r):
            fx, fy, fc = (r >> 2) & 1, (r >> 1) & 1, r & 1
            px = jnp.where(fx == 1, 1 - x, x)
            py = jnp.where(fy == 1, 1 - y, y)
            pc = jnp.where(fc == 1, 1 - c, c)
            return px, py, pc

        first = []
        for r in range(1, N_DEV):
            px, py, pc = peer(r)
            theirs = in_ref.at[pl.ds(pl.multiple_of((4 * px + 2 * py + pc) * seg, 8), seg), :]
            cp = pltpu.make_async_remote_copy(
                src_ref=theirs, dst_ref=recv_ref.at[r], send_sem=send_sems.at[r - 1], recv_sem=recv_sems.at[r - 1],
                device_id=(px, py, pc), device_id_type=MESH)
            cp.start()
            first.append(cp)
        for cp in first:
            cp.wait()
        my_rows = pl.ds(pl.multiple_of(me * seg, 8), seg)
        acc = in_ref[my_rows, :]
        for r in range(1, N_DEV):
            acc = acc + recv_ref[r]
        out_ref[my_rows, :] = acc
        second = []
        for r in range(1, N_DEV):
            px, py, pc = peer(r)
            cp = pltpu.make_async_remote_copy(
                src_ref=out_ref.at[my_rows, :], dst_ref=out_ref.at[my_rows, :],
                send_sem=send_sems.at[6 + r], recv_sem=recv_sems.at[6 + r],
                device_id=(px, py, pc), device_id_type=MESH)
            cp.start()
            second.append(cp)
        for r in range(1, N_DEV):
            px, py, pc = peer(r)
            theirs = out_ref.at[pl.ds(pl.multiple_of((4 * px + 2 * py + pc) * seg, 8), seg), :]
            pltpu.make_async_remote_copy(
                src_ref=theirs, dst_ref=theirs, send_sem=send_sems.at[6 + r], recv_sem=recv_sems.at[6 + r],
                device_id=(px, py, pc), device_id_type=MESH).wait_recv()
        for cp in second:
            cp.wait_send()
        comm.finish(cins, couts, csems)

    vm = pl.BlockSpec(memory_space=pltpu.VMEM)
    any_spec = pl.BlockSpec(memory_space=pl.ANY)
    outs = pl.pallas_call(
        body, name="all_reduce_small",
        in_specs=[vm] + [any_spec] * c_in, out_specs=[vm] + [any_spec] * c_out,
        out_shape=[jax.ShapeDtypeStruct((rows, lanes), F32)] + comm.outs,
        scratch_shapes=[pltpu.VMEM((N_DEV, seg, lanes), F32),
                        pltpu.SemaphoreType.DMA((2 * (N_DEV - 1),)), pltpu.SemaphoreType.DMA((2 * (N_DEV - 1),))]
        + comm.sems,
        input_output_aliases={1 + ci: 1 + co for ci, co in comm.alias.items()},
        compiler_params=pltpu.CompilerParams(vmem_limit_bytes=VMEM_LIMIT),
    )(flat, *comm.ins)
    return outs[0], list(outs[1:])


def _adamw_update(w_ref, g_ref, m_ref, v_ref, d_ref, nm_ref, nv_ref):
    c1 = 1.0 - ADAM_B1 ** ADAM_STEP
    c2 = 1.0 - ADAM_B2 ** ADAM_STEP
    gv = g_ref[...]
    nm = ADAM_B1 * m_ref[...] + (1.0 - ADAM_B1) * gv
    nv = ADAM_B2 * v_ref[...] + (1.0 - ADAM_B2) * (gv * gv)
    d_ref[...] = -ADAM_LR * ((nm / c1) / (jnp.sqrt(nv / c2) + ADAM_EPS) + ADAM_WD * w_ref[...])
    nm_ref[...] = nm
    nv_ref[...] = nv


def _adamw_many(ws, gs, ms, vs, name):
    n = len(ws)

    def body(*refs):
        for i in range(n):
            _adamw_update(*[refs[k * n + i] for k in range(7)])

    shapes = [jax.ShapeDtypeStruct(w.shape, F32) for w in ws]
    outs = pl.pallas_call(
        body, name=name, out_shape=shapes * 3,
        compiler_params=pltpu.CompilerParams(vmem_limit_bytes=VMEM_LIMIT),
    )(*ws, *gs, *ms, *vs)
    return outs[:n], outs[n:2 * n], outs[2 * n:]


def _adamw(w, g, m, v, name):
    rows, cols = w.shape
    tr = _tile(rows, 256, 8)

    def body(w_ref, g_ref, m_ref, v_ref, go_ref, d_ref, nm_ref, nv_ref):
        go_ref[...] = g_ref[...]
        _adamw_update(w_ref, g_ref, m_ref, v_ref, d_ref, nm_ref, nv_ref)

    blk = pl.BlockSpec((tr, cols), lambda i: (i, 0))
    sds = jax.ShapeDtypeStruct((rows, cols), F32)
    return pl.pallas_call(
        body, name=name, grid=(rows // tr,),
        in_specs=[blk] * 4, out_specs=[blk] * 4, out_shape=[sds] * 4,
        compiler_params=_params("parallel"),
    )(w, g, m, v)


SMALL = ("norm_ffn1", "norm_mix", "ssm_a_re", "ssm_a_im", "ssm_log_dt", "ssm_b_re", "ssm_b_im", "ssm_c_re",
         "ssm_c_im", "ssm_d", "ssm_glu_w", "ssm_glu_b", "gm_v_gain", "gm_w_s", "gm_b_s", "gain_ssm_out",
         "gain_gm_out", "norm_ffn2", "norm_final")
WEIGHTS = ("norm_ffn1", "ffn1_w_in", "ffn1_w_out", "norm_mix", "mix_w_in", "ssm_a_re", "ssm_a_im", "ssm_log_dt",
           "ssm_b_re", "ssm_b_im", "ssm_c_re", "ssm_c_im", "ssm_d", "ssm_glu_w", "ssm_glu_b", "gm_v_gain", "gm_w_s",
           "gm_b_s", "gain_ssm_out", "gain_gm_out", "mix_w_out", "norm_ffn2", "ffn2_w_in", "ffn2_w_out", "norm_final")


def _ffn_fwd(x, gain, w_in, w_out, tag, hosted=None):
    if hosted is None:
        h, t, q, a = _ffn_in_fwd(x, gain, w_in, f"{tag}_in")
    else:
        (h, t, q, a), got = _ffn_in_fwd(x, gain, w_in, f"{tag}_in_hosting", comm=hosted[0]())
        hosted[1](got)
    if callable(w_out):
        w_out = w_out()
    out = _matmul(a, w_out, "nn", scale=0.5, res=x, tm=512, tn=1024, tk=4096, name=f"{tag}_out")
    return out, (x, h, t, q, a)


def _ffn_bwd(dout, saved, gain, w_in, w_out, tag, hooks=None, publish=None, late_out_dw=False):
    x, h, t, q, a = saved
    f = t.shape[1]
    hooks = hooks or {}

    def hosted(key, fn, *args, name, **kw):
        if key not in hooks:
            return fn(*args, name=name, **kw)
        make, take = hooks[key]
        *res, got = fn(*args, name=f"{name}_hosting", comm=make(), **kw)
        take(got)
        return res[0] if len(res) == 1 else tuple(res)

    def out_dw():
        dw = hosted("out_dw", _matmul, a, dout, "tn", scale=0.5, tm=1536, tn=1024, tk=2048, name=f"{tag}_out_dw")
        if publish is not None:
            publish("out", dw)
        return dw

    dg, du = hosted("out_dx", _ffn_out_bwd, dout, w_out, t, q, name=f"{tag}_out_dx")
    if not late_out_dw:
        dw_out = out_dw()
    dw_in = hosted("in_dw_g", _matmul, h, dg, "tn", tm=512, tn=1536, tk=4096, name=f"{tag}_in_dw_g",
                   out_cols=2 * f)
    dw_in = hosted("in_dw_u", _matmul, h, du, "tn", tm=512, tn=1536, tk=4096, name=f"{tag}_in_dw_u",
                   out_cols=2 * f, col_off=f, into=dw_in)
    if publish is not None:
        publish("in", dw_in)
    if late_out_dw:
        dw_out = out_dw()
    dx, dgain = hosted("in_dx", _proj_in_bwd, [(dg, 0), (du, f)], w_in, x, gain, dout, name=f"{tag}_in_dx")
    return dx, dgain, dw_in, dw_out


def kernel(x, norm_ffn1, ffn1_w_in, ffn1_w_out, norm_mix, mix_w_in, ssm_a_re, ssm_a_im, ssm_log_dt, ssm_b_re, ssm_b_im, ssm_c_re, ssm_c_im, ssm_d, ssm_glu_w, ssm_glu_b, gm_v_gain, gm_w_s, gm_b_s, gain_ssm_out, gain_gm_out, mix_w_out, norm_ffn2, ffn2_w_in, ffn2_w_out, norm_final, loss_target, m_norm_ffn1, m_ffn1_w_in, m_ffn1_w_out, m_norm_mix, m_mix_w_in, m_ssm_a_re, m_ssm_a_im, m_ssm_log_dt, m_ssm_b_re, m_ssm_b_im, m_ssm_c_re, m_ssm_c_im, m_ssm_d, m_ssm_glu_w, m_ssm_glu_b, m_gm_v_gain, m_gm_w_s, m_gm_b_s, m_gain_ssm_out, m_gain_gm_out, m_mix_w_out, m_norm_ffn2, m_ffn2_w_in, m_ffn2_w_out, m_norm_final, v_norm_ffn1, v_ffn1_w_in, v_ffn1_w_out, v_norm_mix, v_mix_w_in, v_ssm_a_re, v_ssm_a_im, v_ssm_log_dt, v_ssm_b_re, v_ssm_b_im, v_ssm_c_re, v_ssm_c_im, v_ssm_d, v_ssm_glu_w, v_ssm_glu_b, v_gm_v_gain, v_gm_w_s, v_gm_b_s, v_gain_ssm_out, v_gain_gm_out, v_mix_w_out, v_norm_ffn2, v_ffn2_w_in, v_ffn2_w_out, v_norm_final):
    wts = dict(norm_ffn1=norm_ffn1, ffn1_w_in=ffn1_w_in, ffn1_w_out=ffn1_w_out, norm_mix=norm_mix, mix_w_in=mix_w_in,
               ssm_a_re=ssm_a_re, ssm_a_im=ssm_a_im, ssm_log_dt=ssm_log_dt, ssm_b_re=ssm_b_re, ssm_b_im=ssm_b_im,
               ssm_c_re=ssm_c_re, ssm_c_im=ssm_c_im, ssm_d=ssm_d, ssm_glu_w=ssm_glu_w, ssm_glu_b=ssm_glu_b,
               gm_v_gain=gm_v_gain, gm_w_s=gm_w_s, gm_b_s=gm_b_s, gain_ssm_out=gain_ssm_out, gain_gm_out=gain_gm_out,
               mix_w_out=mix_w_out, norm_ffn2=norm_ffn2, ffn2_w_in=ffn2_w_in, ffn2_w_out=ffn2_w_out,
               norm_final=norm_final)
    mom = dict(norm_ffn1=m_norm_ffn1, ffn1_w_in=m_ffn1_w_in, ffn1_w_out=m_ffn1_w_out, norm_mix=m_norm_mix,
               mix_w_in=m_mix_w_in, ssm_a_re=m_ssm_a_re, ssm_a_im=m_ssm_a_im, ssm_log_dt=m_ssm_log_dt,
               ssm_b_re=m_ssm_b_re, ssm_b_im=m_ssm_b_im, ssm_c_re=m_ssm_c_re, ssm_c_im=m_ssm_c_im, ssm_d=m_ssm_d,
               ssm_glu_w=m_ssm_glu_w, ssm_glu_b=m_ssm_glu_b, gm_v_gain=m_gm_v_gain, gm_w_s=m_gm_w_s, gm_b_s=m_gm_b_s,
               gain_ssm_out=m_gain_ssm_out, gain_gm_out=m_gain_gm_out, mix_w_out=m_mix_w_out, norm_ffn2=m_norm_ffn2,
               ffn2_w_in=m_ffn2_w_in, ffn2_w_out=m_ffn2_w_out, norm_final=m_norm_final)
    var = dict(norm_ffn1=v_norm_ffn1, ffn1_w_in=v_ffn1_w_in, ffn1_w_out=v_ffn1_w_out, norm_mix=v_norm_mix,
               mix_w_in=v_mix_w_in, ssm_a_re=v_ssm_a_re, ssm_a_im=v_ssm_a_im, ssm_log_dt=v_ssm_log_dt,
               ssm_b_re=v_ssm_b_re, ssm_b_im=v_ssm_b_im, ssm_c_re=v_ssm_c_re, ssm_c_im=v_ssm_c_im, ssm_d=v_ssm_d,
               ssm_glu_w=v_ssm_glu_w, ssm_glu_b=v_ssm_glu_b, gm_v_gain=v_gm_v_gain, gm_w_s=v_gm_w_s, gm_b_s=v_gm_b_s,
               gain_ssm_out=v_gain_ssm_out, gain_gm_out=v_gain_gm_out, mix_w_out=v_mix_w_out, norm_ffn2=v_norm_ffn2,
               ffn2_w_in=v_ffn2_w_in, ffn2_w_out=v_ffn2_w_out, norm_final=v_norm_final)

    nseq, seq, d = x.shape
    n = nseq * seq
    depth = norm_ffn1.shape[0]
    width = gain_ssm_out.shape[1]
    groups = ssm_a_re.shape[1]
    heads = gm_w_s.shape[1]
    core = lax.axis_index("c").astype(jnp.int32).reshape(1)

    is_col = dict(BIG)
    full = {name: [None] * depth for name, _ in BIG}

    def gather_comm(pairs):
        return _gather_comm([wts[nm][l].astype(BF16) for nm, l in pairs], [is_col[nm] for nm, _ in pairs])

    def store(pairs, arrays):
        for (nm, l), w in zip(pairs, arrays):
            full[nm][l] = w

    pairs = [("ffn1_w_in", 0)]
    store(pairs, _run_comm(gather_comm(pairs), "all_gather_first"))

    xs = x.reshape(n, d)
    saved = []
    for l in range(depth):
        pairs = [("ffn1_w_out", l)] + ([("mix_w_in", l), ("mix_w_out", l)] if l == 0 else [])
        x1, s_ffn1 = _ffn_fwd(xs, norm_ffn1[l], full["ffn1_w_in"][l], lambda l=l: full["ffn1_w_out"][l], "ffn1",
                              hosted=(functools.partial(gather_comm, pairs), functools.partial(store, pairs)))
        pairs = [("ffn2_w_out", l)]
        hm, u_ssm, zgm, got = _mix_in_fwd(x1, norm_mix[l], full["mix_w_in"][l], width, "mix_in",
                                          comm=gather_comm(pairs))
        store(pairs, got)
        bt_re = jnp.swapaxes(ssm_b_re[l], 1, 2)
        bt_im = jnp.swapaxes(ssm_b_im[l], 1, 2)
        disc_in = (ssm_a_re[l], ssm_a_im[l], ssm_log_dt[l].reshape(groups, 1), bt_re, bt_im)
        lr, li, bbr, bbi = _disc_fwd(*disc_in)
        ops = _ssm_operands(lr, li, bbr, bbi, ssm_c_re[l], ssm_c_im[l], ssm_d[l], ssm_glu_w[l], ssm_glu_b[l])
        pairs = [("ffn2_w_in", l)]
        y_ssm, hb, got = _ssm_fwd_pair(u_ssm, ops, nseq, "s5_fwd", comm=gather_comm(pairs))
        store(pairs, got)
        bias_tile = jnp.broadcast_to(gm_b_s[l].T[:, :, None], (GM_CHUNK, heads, GM_HEAD_DIM)).reshape(GM_CHUNK, width)
        y_gm = _gmlp_fwd(zgm, gm_v_gain[l], gm_w_s[l], bias_tile, "gmlp_fwd")
        if l + 1 < depth:
            pairs = [("mix_w_in", l + 1), ("mix_w_out", l + 1)]
            ycat, x2, got = _mix_out_fwd(y_ssm, y_gm, gain_ssm_out[l], gain_gm_out[l], full["mix_w_out"][l], x1,
                                         "mix_out_hosting", comm=gather_comm(pairs))
            store(pairs, got)
        else:
            ycat, x2 = _mix_out_fwd(y_ssm, y_gm, gain_ssm_out[l], gain_gm_out[l], full["mix_w_out"][l], x1, "mix_out")
        hosted = None
        if l + 1 < depth:
            pairs = [("ffn1_w_in", l + 1)]
            hosted = (functools.partial(gather_comm, pairs), functools.partial(store, pairs))
        x3, s_ffn2 = _ffn_fwd(x2, norm_ffn2[l], full["ffn2_w_in"][l], full["ffn2_w_out"][l], "ffn2", hosted=hosted)
        saved.append(dict(ffn1=s_ffn1, x1=x1, hm=hm, zgm=zgm, disc_in=disc_in, ops=ops, u_ssm=u_ssm, hb=hb, y_ssm=y_ssm,
                          bias_tile=bias_tile, y_gm=y_gm, ycat=ycat, ffn2=s_ffn2))
        xs = x3

    dx, g_norm_final, loss_part = _loss_head(xs, norm_final, loss_target.reshape(n, d))
    big = {name: [None] * depth for name, _ in BIG}
    small = {name: [None] * depth for name in SMALL if name != "norm_final"}
    gpb = GROUPS_PER_BLOCK
    s_blk = STATES_PER_BLOCK
    psum_of, reduced, grads = {}, {}, {}
    shared_early = ["ffn2_w_in", "ffn2_w_out", "mix_w_in", "mix_w_out"]

    def swap_comm(pairs):
        return _pair_exchange_comm([big[nm][l] for nm, l in pairs], [is_col[nm] for nm, _ in pairs])

    def take_swapped(pairs, others):
        for (nm, l), other in zip(pairs, others):
            psum_of[nm, l] = _pair_sum(big[nm][l], other, is_col[nm], core, f"grad_pair_sum_{nm}")

    def send_comm(pairs):
        return _chip_exchange_comm([psum_of[p] for p in pairs], [is_col[nm] for nm, _ in pairs])

    def take_sent(pairs, slots):
        for (nm, l), s in zip(pairs, slots):
            reduced[nm] = _chip_sum(s, core, l, depth, reduced.get(nm), f"grad_chip_sum_{nm}")

    def hosting(make, take, pairs):
        return functools.partial(make, pairs), functools.partial(take, pairs)

    for l in reversed(range(depth)):
        sv = saved[l]
        above = [(nm, l + 1) for nm in ("mix_w_in", "mix_w_out", "ffn1_w_in", "ffn1_w_out")] if l + 1 < depth else []
        dx, small["norm_ffn2"][l], big["ffn2_w_in"][l], big["ffn2_w_out"][l] = _ffn_bwd(
            dx, sv["ffn2"], norm_ffn2[l], full["ffn2_w_in"][l], full["ffn2_w_out"][l], "ffn2",
            hooks={"out_dx": hosting(swap_comm, take_swapped, above)} if above else None)
        mine = [("ffn2_w_in", l), ("ffn2_w_out", l)]
        dy_ssm, dy_gm, small["gain_ssm_out"][l], small["gain_gm_out"][l], got = _mix_out_bwd(
            dx, full["mix_w_out"][l], sv["y_ssm"], sv["y_gm"], gain_ssm_out[l], gain_gm_out[l], "mix_out_dx",
            comm=swap_comm(mine))
        take_swapped(mine, got)
        big["mix_w_out"][l] = _matmul(sv["ycat"], dx, "tn", tm=1024, tn=1024, tk=2048, name="mix_out_dw")
        dzgm, small["gm_w_s"][l], dbias_tile, small["gm_v_gain"][l] = _gmlp_bwd(
            sv["zgm"], dy_gm, gm_v_gain[l], gm_w_s[l], sv["bias_tile"], "gmlp_bwd")
        small["gm_b_s"][l] = dbias_tile.reshape(GM_CHUNK, heads, GM_HEAD_DIM).sum(-1).T
        (du_ssm, dlam, db_bd, dct_bd, dd, dw_bd, dbias), got = _ssm_bwd_pair(
            sv["u_ssm"], dy_ssm, sv["hb"], sv["ops"], nseq, "s5_bwd", comm=send_comm(mine + above))
        take_sent(mine + above, got)
        dlr = dlam[:, 0, :s_blk].reshape(groups, SSM_STATE)
        dli = dlam[:, 0, s_blk:].reshape(groups, SSM_STATE)
        dbbr = _block_diag_extract(db_bd[:, :, :s_blk], gpb)
        dbbi = _block_diag_extract(db_bd[:, :, s_blk:], gpb)
        da_re, da_im, dldt, dbt_re, dbt_im = _disc_bwd(*sv["disc_in"], dlr, dli, dbbr, dbbi)
        small["ssm_a_re"][l], small["ssm_a_im"][l], small["ssm_log_dt"][l] = da_re, da_im, dldt.reshape(groups)
        small["ssm_b_re"][l] = jnp.swapaxes(dbt_re, 1, 2)
        small["ssm_b_im"][l] = jnp.swapaxes(dbt_im, 1, 2)
        small["ssm_c_re"][l] = _block_diag_extract(dct_bd[:, :, :s_blk], gpb)
        small["ssm_c_im"][l] = -_block_diag_extract(dct_bd[:, :, s_blk:], gpb)
        small["ssm_d"][l] = dd.reshape(groups, SSM_CH)
        small["ssm_glu_w"][l] = jnp.concatenate(
            [_block_diag_extract(dw_bd[:, :, :LANES], gpb), _block_diag_extract(dw_bd[:, :, LANES:], gpb)], axis=-1)
        small["ssm_glu_b"][l] = jnp.concatenate(
            [dbias[:, 0, :LANES].reshape(groups, SSM_CH), dbias[:, 0, LANES:].reshape(groups, SSM_CH)], axis=-1)
        cols_mi = 3 * width
        dw_mi = _matmul(sv["hm"], du_ssm, "tn", tm=1024, tn=width, tk=2048, name="mix_in_dw_ssm", out_cols=cols_mi)
        big["mix_w_in"][l] = _matmul(sv["hm"], dzgm, "tn", tm=1024, tn=width, tk=2048, name="mix_in_dw_gm",
                                     out_cols=cols_mi, col_off=width, into=dw_mi)
        dx, small["norm_mix"][l] = _proj_in_bwd([(du_ssm, 0), (dzgm, width)], full["mix_w_in"][l], sv["x1"],
                                                norm_mix[l], dx, "mix_in_dx")
        hooks = None
        if l == 0:
            mix, w_out_0, w_in_0 = [("mix_w_in", 0), ("mix_w_out", 0)], [("ffn1_w_out", 0)], [("ffn1_w_in", 0)]

            def last_make():
                return _merge_comms(_merge_comms(send_comm(w_in_0), swap_comm(w_out_0)),
                                    _pair_share_comm([reduced[nm] for nm in shared_early]))

            def last_take(got):
                take_sent(w_in_0, got[:1])
                take_swapped(w_out_0, got[1:2])
                grads.update(zip(shared_early, got[2:]))

            hooks = {"out_dx": hosting(swap_comm, take_swapped, mix), "in_dw_g": hosting(send_comm, take_sent, mix),
                     "out_dw": hosting(swap_comm, take_swapped, w_in_0), "in_dx": (last_make, last_take)}

        def publish(which, dw, l=l):
            big[f"ffn1_w_{which}"][l] = dw

        dx, small["norm_ffn1"][l], big["ffn1_w_in"][l], big["ffn1_w_out"][l] = _ffn_bwd(
            dx, sv["ffn1"], norm_ffn1[l], full["ffn1_w_in"][l], full["ffn1_w_out"][l], "ffn1",
            hooks=hooks, publish=publish, late_out_dw=(l == 0))
    grad_x = dx.reshape(nseq, seq, d)

    pieces = [jnp.stack(small[name]).reshape(-1) for name in SMALL if name != "norm_final"]
    pieces += [g_norm_final.reshape(-1), loss_part.reshape(1)]
    sizes = [p.shape[0] for p in pieces]
    total = sum(sizes)
    rows = -(-total // (LANES * N_DEV * SUBLANES)) * N_DEV * SUBLANES
    pad = rows * LANES - total
    tail = [("ffn1_w_out", 0)]
    flat_g, got = _all_reduce_small(
        jnp.concatenate(pieces + [jnp.zeros((pad,), F32)]).reshape(rows, LANES), send_comm(tail))
    take_sent(tail, got)
    flat_g = flat_g.reshape(-1)
    loss = flat_g[total - 1]

    names = [name for name, _ in BIG if name not in shared_early]
    grads.update(zip(names, _run_comm(_pair_share_comm([reduced[nm] for nm in names]), "grad_pair_share")))
    offs = 0
    for name, size in zip(SMALL, sizes[:-1]):
        grads[name] = flat_g[offs:offs + size].reshape(wts[name].shape)
        offs += size

    delta, new_m, new_v = {}, {}, {}
    for name, _ in BIG:
        shape = wts[name].shape
        two_d = lambda a: a.reshape(shape[0] * shape[1], shape[2])
        go, dl, nm, nv = _adamw(two_d(wts[name]), two_d(grads[name]), two_d(mom[name]), two_d(var[name]),
                                f"adamw_{name}")
        grads[name] = go.reshape(shape)
        delta[name], new_m[name], new_v[name] = dl.reshape(shape), nm.reshape(shape), nv.reshape(shape)
    at_least_2d = lambda a: a.reshape(1, -1) if a.ndim == 1 else a
    dls, nms, nvs = _adamw_many(*[[at_least_2d(tree[k]) for k in SMALL] for tree in (wts, grads, mom, var)],
                                "adamw_small")
    for name, dl, nm, nv in zip(SMALL, dls, nms, nvs):
        shape = wts[name].shape
        delta[name], new_m[name], new_v[name] = dl.reshape(shape), nm.reshape(shape), nv.reshape(shape)

    return (loss, grad_x, *[grads[k] for k in WEIGHTS], *[delta[k] for k in WEIGHTS],
            *[new_m[k] for k in WEIGHTS], *[new_v[k] for k in WEIGHTS])
```
